```python
import jax, jax.numpy as jnp
from jax import lax
import numpy as np

D_MODEL = 1024
BATCH = 32
SEQ = 2048
DEPTH = 2

HEAD_DIM = 64
N_Q_HEADS = 16
N_KV_HEADS = 4
GQA_GROUP = N_Q_HEADS // N_KV_HEADS
ATTN_WIDTH = N_Q_HEADS * HEAD_DIM
KV_WIDTH = N_KV_HEADS * HEAD_DIM
WINDOW = 128
BLOCK = WINDOW
CONV_WIDTH = D_MODEL
CONV_K = 3
D_FF = 4 * D_MODEL
RMS_EPS = 1e-6
NEG_INF = -1e30
IN_SPLIT_SIZES = (ATTN_WIDTH, KV_WIDTH, KV_WIDTH, CONV_WIDTH, CONV_WIDTH, CONV_WIDTH, D_MODEL, D_MODEL)
IN_COLS = sum(IN_SPLIT_SIZES)

kernel_name = "hybrid_swa_sink_alibi_shortconv_gated_block"


def rmsnorm(x, g):
    xf = x.astype(jnp.float32)
    y = xf * lax.rsqrt(jnp.mean(xf * xf, axis=-1, keepdims=True) + RMS_EPS)
    return (y * g.astype(jnp.float32)).astype(x.dtype)


def alibi_slopes():
    h = np.arange(1, N_Q_HEADS + 1, dtype=np.float32)
    return jnp.asarray(np.power(np.float32(2.0), -8.0 * h / N_Q_HEADS), dtype=jnp.float32)


def sliding_window_sink_attention(q, k, v, sinks):
    B, S = q.shape[0], q.shape[1]
    nblk = S // BLOCK
    qb = q.reshape(B, nblk, BLOCK, N_KV_HEADS, GQA_GROUP, HEAD_DIM)
    kb = k.reshape(B, nblk, BLOCK, N_KV_HEADS, HEAD_DIM)
    vb = v.reshape(B, nblk, BLOCK, N_KV_HEADS, HEAD_DIM)
    pad = ((0, 0), (1, 0), (0, 0), (0, 0), (0, 0))
    k_band = jnp.concatenate([jnp.pad(kb[:, :-1], pad), kb], axis=2)
    v_band = jnp.concatenate([jnp.pad(vb[:, :-1], pad), vb], axis=2)
    qb = jnp.moveaxis(qb, 1, 0)
    k_band = jnp.moveaxis(k_band, 1, 0)
    v_band = jnp.moveaxis(v_band, 1, 0)

    r = jnp.arange(BLOCK)[:, None]
    j = jnp.arange(2 * BLOCK)[None, :]
    dist = BLOCK + r - j
    in_window = (dist >= 0) & (dist < WINDOW)
    slopes = alibi_slopes().reshape(N_KV_HEADS, GQA_GROUP)
    bias = -slopes[:, :, None, None] * dist.astype(jnp.float32)[None, None]
    sink = sinks.astype(jnp.float32).reshape(1, N_KV_HEADS, GQA_GROUP, 1, 1)
    scale = HEAD_DIM ** -0.5

    def one_block(args):
        qi, ki, vi, i = args
        s = jnp.einsum('bqhgd,bkhd->bhgqk', qi, ki, preferred_element_type=jnp.float32) * scale + bias
        valid = in_window & ((i - 1) * BLOCK + j >= 0)
        s = jnp.where(valid, s, NEG_INF)
        m = jnp.maximum(jnp.max(s, axis=-1, keepdims=True), sink)
        p = jnp.exp(s - m)
        denom = jnp.sum(p, axis=-1, keepdims=True) + jnp.exp(sink - m)
        return jnp.einsum('bhgqk,bkhd->bqhgd', (p / denom).astype(vi.dtype), vi)

    out = lax.map(one_block, (qb, k_band, v_band, jnp.arange(nblk)))
    return jnp.moveaxis(out, 0, 1).reshape(B, S, ATTN_WIDTH)


def gated_short_conv(b_gate, c_gate, u, conv_w, conv_b):
    y = c_gate * u
    z = lax.conv_general_dilated(
        y, conv_w[:, None, :], window_strides=(1,), padding=[(CONV_K - 1, 0)],
        dimension_numbers=('NWC', 'WIO', 'NWC'), feature_group_count=CONV_WIDTH)
    return b_gate * (z + conv_b)


def _fwd_setup_inputs(seed: int = 0) -> dict:
    key = jax.random.key(seed)
    ks = jax.random.split(key, 18)
    f32 = jnp.float32
    nrm = lambda k, shape, s: jax.random.normal(k, shape, f32) * s
    return {
        "x": nrm(ks[0], (BATCH, SEQ, D_MODEL), 1.0),
        "g_mix": 1.0 + nrm(ks[1], (DEPTH, D_MODEL), 0.02),
        "w_in": nrm(ks[2], (DEPTH, D_MODEL, IN_COLS), D_MODEL ** -0.5),
        "b_gates": nrm(ks[3], (DEPTH, 2 * D_MODEL), 0.02),
        "sinks": nrm(ks[4], (DEPTH, N_Q_HEADS), 1.0),
        "w_attn_out": nrm(ks[5], (DEPTH, ATTN_WIDTH, D_MODEL), ATTN_WIDTH ** -0.5),
        "conv_w": nrm(ks[6], (DEPTH, CONV_K, CONV_WIDTH), CONV_K ** -0.5),
        "conv_b": nrm(ks[7], (DEPTH, CONV_WIDTH), 0.02),
        "w_conv_out": nrm(ks[8], (DEPTH, CONV_WIDTH, D_MODEL), CONV_WIDTH ** -0.5),
        "w_o": nrm(ks[9], (DEPTH, D_MODEL, D_MODEL), D_MODEL ** -0.5),
        "g_mlp": 1.0 + nrm(ks[10], (DEPTH, D_MODEL), 0.02),
        "w_up": nrm(ks[11], (DEPTH, D_MODEL, D_FF), D_MODEL ** -0.5),
        "w_down": nrm(ks[12], (DEPTH, D_FF, D_MODEL), D_FF ** -0.5),
        "g_final": 1.0 + nrm(ks[13], (D_MODEL,), 0.02),
    }


def _fwd_reference(x, g_mix, w_in, b_gates, sinks, w_attn_out, conv_w, conv_b, w_conv_out, w_o,
              g_mlp, w_up, w_down, g_final):
    B, S, _ = x.shape
    split_idx = list(np.cumsum(IN_SPLIT_SIZES)[:-1])
    for l in range(DEPTH):
        h = rmsnorm(x, g_mix[l])
        proj = jnp.einsum('bsd,dc->bsc', h, w_in[l])
        q, k, v, cb, cc, cu, ga, gc = jnp.split(proj, split_idx, axis=-1)
        q = q.reshape(B, S, N_Q_HEADS, HEAD_DIM)
        k = k.reshape(B, S, N_KV_HEADS, HEAD_DIM)
        v = v.reshape(B, S, N_KV_HEADS, HEAD_DIM)
        y_attn = jnp.einsum('bse,ed->bsd', sliding_window_sink_attention(q, k, v, sinks[l]), w_attn_out[l])
        y_conv = jnp.einsum('bse,ed->bsd', gated_short_conv(cb, cc, cu, conv_w[l], conv_b[l]), w_conv_out[l])
        gate_a = jax.nn.sigmoid(ga + b_gates[l, :D_MODEL])
        gate_c = jax.nn.sigmoid(gc + b_gates[l, D_MODEL:])
        merged = gate_a * y_attn + gate_c * y_conv
        x = x + jnp.einsum('bsd,de->bse', merged, w_o[l])
        h2 = rmsnorm(x, g_mlp[l])
        u = jnp.square(jax.nn.relu(jnp.einsum('bsd,df->bsf', h2, w_up[l])))
        x = x + jnp.einsum('bsf,fd->bsd', u, w_down[l])
    return rmsnorm(x, g_final)


import jax as _jax
import jax.numpy as _jnp

TWIN_FORMAT = 'train_step'
FWD_PARAMS = ['x', 'g_mix', 'w_in', 'b_gates', 'sinks', 'w_attn_out', 'conv_w', 'conv_b', 'w_conv_out', 'w_o', 'g_mlp', 'w_up', 'w_down', 'g_final']
TWIN_WEIGHTS = ['g_mix', 'w_in', 'b_gates', 'sinks', 'w_attn_out', 'conv_w', 'conv_b', 'w_conv_out', 'w_o', 'g_mlp', 'w_up', 'w_down', 'g_final']
TWIN_DIFF_INPUT = 'x'
TWIN_INPUTS = ['x', 'g_mix', 'w_in', 'b_gates', 'sinks', 'w_attn_out', 'conv_w', 'conv_b', 'w_conv_out', 'w_o', 'g_mlp', 'w_up', 'w_down', 'g_final', 'loss_target', 'm_g_mix', 'm_w_in', 'm_b_gates', 'm_sinks', 'm_w_attn_out', 'm_conv_w', 'm_conv_b', 'm_w_conv_out', 'm_w_o', 'm_g_mlp', 'm_w_up', 'm_w_down', 'm_g_final', 'v_g_mix', 'v_w_in', 'v_b_gates', 'v_sinks', 'v_w_attn_out', 'v_conv_w', 'v_conv_b', 'v_w_conv_out', 'v_w_o', 'v_g_mlp', 'v_w_up', 'v_w_down', 'v_g_final']
TWIN_OUTPUTS = ['loss', 'grad_x', 'grad_g_mix', 'grad_w_in', 'grad_b_gates', 'grad_sinks', 'grad_w_attn_out', 'grad_conv_w', 'grad_conv_b', 'grad_w_conv_out', 'grad_w_o', 'grad_g_mlp', 'grad_w_up', 'grad_w_down', 'grad_g_final', 'delta_g_mix', 'delta_w_in', 'delta_b_gates', 'delta_sinks', 'delta_w_attn_out', 'delta_conv_w', 'delta_conv_b', 'delta_w_conv_out', 'delta_w_o', 'delta_g_mlp', 'delta_w_up', 'delta_w_down', 'delta_g_final', 'new_m_g_mix', 'new_m_w_in', 'new_m_b_gates', 'new_m_sinks', 'new_m_w_attn_out', 'new_m_conv_w', 'new_m_conv_b', 'new_m_w_conv_out', 'new_m_w_o', 'new_m_g_mlp', 'new_m_w_up', 'new_m_w_down', 'new_m_g_final', 'new_v_g_mix', 'new_v_w_in', 'new_v_b_gates', 'new_v_sinks', 'new_v_w_attn_out', 'new_v_conv_w', 'new_v_conv_b', 'new_v_w_conv_out', 'new_v_w_o', 'new_v_g_mlp', 'new_v_w_up', 'new_v_w_down', 'new_v_g_final']
TWIN_LEAF_KINDS = {'loss': 'loss', 'grad_x': 'grad_x', 'grad_g_mix': 'grad_w', 'grad_w_in': 'grad_w', 'grad_b_gates': 'grad_w', 'grad_sinks': 'grad_w', 'grad_w_attn_out': 'grad_w', 'grad_conv_w': 'grad_w', 'grad_conv_b': 'grad_w', 'grad_w_conv_out': 'grad_w', 'grad_w_o': 'grad_w', 'grad_g_mlp': 'grad_w', 'grad_w_up': 'grad_w', 'grad_w_down': 'grad_w', 'grad_g_final': 'grad_w', 'delta_g_mix': 'delta_w', 'delta_w_in': 'delta_w', 'delta_b_gates': 'delta_w', 'delta_sinks': 'delta_w', 'delta_w_attn_out': 'delta_w', 'delta_conv_w': 'delta_w', 'delta_conv_b': 'delta_w', 'delta_w_conv_out': 'delta_w', 'delta_w_o': 'delta_w', 'delta_g_mlp': 'delta_w', 'delta_w_up': 'delta_w', 'delta_w_down': 'delta_w', 'delta_g_final': 'delta_w', 'new_m_g_mix': 'new_m', 'new_m_w_in': 'new_m', 'new_m_b_gates': 'new_m', 'new_m_sinks': 'new_m', 'new_m_w_attn_out': 'new_m', 'new_m_conv_w': 'new_m', 'new_m_conv_b': 'new_m', 'new_m_w_conv_out': 'new_m', 'new_m_w_o': 'new_m', 'new_m_g_mlp': 'new_m', 'new_m_w_up': 'new_m', 'new_m_w_down': 'new_m', 'new_m_g_final': 'new_m', 'new_v_g_mix': 'new_v', 'new_v_w_in': 'new_v', 'new_v_b_gates': 'new_v', 'new_v_sinks': 'new_v', 'new_v_w_attn_out': 'new_v', 'new_v_conv_w': 'new_v', 'new_v_conv_b': 'new_v', 'new_v_w_conv_out': 'new_v', 'new_v_w_o': 'new_v', 'new_v_g_mlp': 'new_v', 'new_v_w_up': 'new_v', 'new_v_w_down': 'new_v', 'new_v_g_final': 'new_v'}


def _forward(args):
    return _fwd_reference(*[args[k] for k in FWD_PARAMS])


def _output_shape():
    out = _jax.eval_shape(lambda: _forward(_fwd_setup_inputs(0)))
    return out.shape, out.dtype

N_MICROBATCH = 1
ADAM_LR = 0.001
ADAM_B1 = 0.9
ADAM_B2 = 0.999
ADAM_EPS = 1e-08
ADAM_WD = 0.01
ADAM_STEP = 10
PER_EXAMPLE_BATCH_AXIS = {'x': 0, 'loss_target': 0}
SHARED_INPUTS = []
_WEIGHT_DTYPES = {'g_mix': _jnp.float32, 'w_in': _jnp.float32, 'b_gates': _jnp.float32, 'sinks': _jnp.float32, 'w_attn_out': _jnp.float32, 'conv_w': _jnp.float32, 'conv_b': _jnp.float32, 'w_conv_out': _jnp.float32, 'w_o': _jnp.float32, 'g_mlp': _jnp.float32, 'w_up': _jnp.float32, 'w_down': _jnp.float32, 'g_final': _jnp.float32}
MOMENT_SCALE = {'g_mix': 2.196924e-01, 'w_in': 8.639616e-02, 'b_gates': 3.566716e-02, 'sinks': 3.912143e-02, 'w_attn_out': 4.121082e-02, 'conv_w': 1.200847e-01, 'conv_b': 1.190823e-01, 'w_conv_out': 1.195129e-01, 'w_o': 1.263726e-01, 'g_mlp': 2.255713e-01, 'w_up': 1.055159e-01, 'w_down': 2.120746e-01, 'g_final': 6.511528e+01}


def _to_microbatches(a, axis):
    t = _jnp.moveaxis(a, axis, 0)
    t = t.reshape((N_MICROBATCH, t.shape[0] // N_MICROBATCH) + t.shape[1:])
    return _jnp.moveaxis(t, 1, axis + 1)


def setup_inputs(seed: int = 0) -> dict:
    inp = _fwd_setup_inputs(seed)
    key = _jax.random.fold_in(_jax.random.key(seed), 7919)
    shape, _ = _output_shape()
    out = dict(inp)
    out["loss_target"] = _jax.random.normal(_jax.random.fold_in(key, 0), shape, _jnp.float32)
    for i, name in enumerate(TWIN_WEIGHTS):
        w = inp[name].astype(_jnp.float32)
        if MOMENT_SCALE is None:
            s = _jnp.sqrt(_jnp.mean(_jnp.square(w)) + 1e-30)
        else:
            s = MOMENT_SCALE[name]
        km, kv = _jax.random.split(_jax.random.fold_in(key, i + 1))
        out[name] = w
        out["m_" + name] = s * _jax.random.normal(km, w.shape, _jnp.float32)
        out["v_" + name] = (s * s) * _jax.random.uniform(kv, w.shape, _jnp.float32, 0.5, 1.5)
    if N_MICROBATCH > 1:
        for name, axis in PER_EXAMPLE_BATCH_AXIS.items():
            out[name] = _to_microbatches(out[name], axis)
    return {'x': out['x'], 'g_mix': out['g_mix'], 'w_in': out['w_in'], 'b_gates': out['b_gates'], 'sinks': out['sinks'], 'w_attn_out': out['w_attn_out'], 'conv_w': out['conv_w'], 'conv_b': out['conv_b'], 'w_conv_out': out['w_conv_out'], 'w_o': out['w_o'], 'g_mlp': out['g_mlp'], 'w_up': out['w_up'], 'w_down': out['w_down'], 'g_final': out['g_final'], 'loss_target': out['loss_target'], 'm_g_mix': out['m_g_mix'], 'm_w_in': out['m_w_in'], 'm_b_gates': out['m_b_gates'], 'm_sinks': out['m_sinks'], 'm_w_attn_out': out['m_w_attn_out'], 'm_conv_w': out['m_conv_w'], 'm_conv_b': out['m_conv_b'], 'm_w_conv_out': out['m_w_conv_out'], 'm_w_o': out['m_w_o'], 'm_g_mlp': out['m_g_mlp'], 'm_w_up': out['m_w_up'], 'm_w_down': out['m_w_down'], 'm_g_final': out['m_g_final'], 'v_g_mix': out['v_g_mix'], 'v_w_in': out['v_w_in'], 'v_b_gates': out['v_b_gates'], 'v_sinks': out['v_sinks'], 'v_w_attn_out': out['v_w_attn_out'], 'v_conv_w': out['v_conv_w'], 'v_conv_b': out['v_conv_b'], 'v_w_conv_out': out['v_w_conv_out'], 'v_w_o': out['v_w_o'], 'v_g_mlp': out['v_g_mlp'], 'v_w_up': out['v_w_up'], 'v_w_down': out['v_w_down'], 'v_g_final': out['v_g_final']}


def _loss(weights, diff, rest, loss_target):
    with _jax.named_scope("forward"):
        args = {**rest, TWIN_DIFF_INPUT: diff, **{k: w.astype(_WEIGHT_DTYPES[k]) for k, w in weights.items()}}
        y = _forward(args)
    with _jax.named_scope("loss_head"):
        err = _jnp.square(y.astype(_jnp.float32) - loss_target)
        return 0.5 * _jnp.sum(_jnp.mean(err, axis=-1)) if err.ndim else 0.5 * err


def _adamw(w, g, m, v):
    m = ADAM_B1 * m + (1.0 - ADAM_B1) * g
    v = ADAM_B2 * v + (1.0 - ADAM_B2) * _jnp.square(g)
    m_hat = m / (1.0 - ADAM_B1 ** ADAM_STEP)
    v_hat = v / (1.0 - ADAM_B2 ** ADAM_STEP)
    delta = -ADAM_LR * (m_hat / (_jnp.sqrt(v_hat) + ADAM_EPS) + ADAM_WD * w)
    return delta, m, v


def reference(x, g_mix, w_in, b_gates, sinks, w_attn_out, conv_w, conv_b, w_conv_out, w_o, g_mlp, w_up, w_down, g_final, loss_target, m_g_mix, m_w_in, m_b_gates, m_sinks, m_w_attn_out, m_conv_w, m_conv_b, m_w_conv_out, m_w_o, m_g_mlp, m_w_up, m_w_down, m_g_final, v_g_mix, v_w_in, v_b_gates, v_sinks, v_w_attn_out, v_conv_w, v_conv_b, v_w_conv_out, v_w_o, v_g_mlp, v_w_up, v_w_down, v_g_final):
    given = dict(x=x, g_mix=g_mix, w_in=w_in, b_gates=b_gates, sinks=sinks, w_attn_out=w_attn_out, conv_w=conv_w, conv_b=conv_b, w_conv_out=w_conv_out, w_o=w_o, g_mlp=g_mlp, w_up=w_up, w_down=w_down, g_final=g_final, loss_target=loss_target, m_g_mix=m_g_mix, m_w_in=m_w_in, m_b_gates=m_b_gates, m_sinks=m_sinks, m_w_attn_out=m_w_attn_out, m_conv_w=m_conv_w, m_conv_b=m_conv_b, m_w_conv_out=m_w_conv_out, m_w_o=m_w_o, m_g_mlp=m_g_mlp, m_w_up=m_w_up, m_w_down=m_w_down, m_g_final=m_g_final, v_g_mix=v_g_mix, v_w_in=v_w_in, v_b_gates=v_b_gates, v_sinks=v_sinks, v_w_attn_out=v_w_attn_out, v_conv_w=v_conv_w, v_conv_b=v_conv_b, v_w_conv_out=v_w_conv_out, v_w_o=v_w_o, v_g_mlp=v_g_mlp, v_w_up=v_w_up, v_w_down=v_w_down, v_g_final=v_g_final)
    weights = {n: given[n] for n in TWIN_WEIGHTS}
    shared = {n: given[n] for n in SHARED_INPUTS}
    per_example = {n: given[n] for n in ['x']}
    grad_fn = _jax.value_and_grad(_loss, argnums=(0, 1))

    def one_microbatch(ex, loss_target):
        ex = dict(ex)
        diff = ex.pop(TWIN_DIFF_INPUT)
        return grad_fn(weights, diff, {**shared, **ex}, loss_target)

    if N_MICROBATCH == 1:
        loss, (grad_w, grad_x) = one_microbatch(per_example, given["loss_target"])
    else:
        def body(carry, xs):
            loss_sum, grad_sum = carry
            l_k, (gw_k, gx_k) = one_microbatch(xs[0], xs[1])
            with _jax.named_scope("update"):
                return (loss_sum + l_k, _jax.tree.map(_jnp.add, grad_sum, gw_k)), gx_k

        init = (_jnp.zeros((), _jnp.float32), _jax.tree.map(_jnp.zeros_like, weights))
        (loss, grad_w), grad_x = _jax.lax.scan(body, init, (per_example, given["loss_target"]))
    with _jax.named_scope("update"):
        delta_w, new_m, new_v = {}, {}, {}
        for n in TWIN_WEIGHTS:
            delta_w[n], new_m[n], new_v[n] = _adamw(weights[n], grad_w[n], given["m_" + n], given["v_" + n])
    return (loss, grad_x, *[grad_w[n] for n in TWIN_WEIGHTS], *[delta_w[n] for n in TWIN_WEIGHTS],
            *[new_m[n] for n in TWIN_WEIGHTS], *[new_v[n] for n in TWIN_WEIGHTS])
```

```python
import functools

import numpy as np
import jax
import jax.numpy as jnp
from jax import lax
from jax.experimental import pallas as pl
from jax.experimental.pallas import tpu as pltpu

D = 1024
NG = 5 * D
NQ = 1024
NKV = 256
NP = NQ + 2 * NKV + NG
F = 4096
HD = 64
GQ = 4
WIN = 128
L = 2
NDEV = 8
EPS = 1e-6
NEG = -1e30
SMALL_ROWS = 24
LR, B1, B2, AEPS, WD, STEP = 0.001, 0.9, 0.999, 1e-08, 0.01, 10

BF = jnp.bfloat16
F32 = jnp.float32
MESH = pl.DeviceIdType.MESH
VMEM_LIMIT = 60 * 1024 * 1024

NN = ((1,), (0,))
NT = ((1,), (1,))
TN = ((0,), (0,))


def _dot(a, b, dims):
    return lax.dot_general(a, b, (dims, ((), ())), preferred_element_type=F32)


def _params(sem=None):
    return pltpu.CompilerParams(dimension_semantics=sem, vmem_limit_bytes=VMEM_LIMIT)


def _resident(shape, imap):
    return pl.BlockSpec(shape, imap, pipeline_mode=pl.Buffered(1))


def _rms(x, g):
    r = lax.rsqrt(jnp.mean(x * x, axis=-1, keepdims=True) + EPS)
    return x * r * g


def _rms_bwd(dy, x, g):
    r = lax.rsqrt(jnp.mean(x * x, axis=-1, keepdims=True) + EPS)
    xh = x * r
    dxh = dy * g
    dx = r * (dxh - xh * jnp.mean(dxh * xh, axis=-1, keepdims=True))
    return dx, jnp.sum(dy * xh, axis=0, keepdims=True)


def _shift_rows(y, k, edge_rows, down):
    n = y.shape[0]
    rid = lax.broadcasted_iota(jnp.int32, y.shape, 0)
    out = pltpu.roll(y, k if down else n - k, 0)
    for t, row in enumerate(edge_rows):
        out = jnp.where(rid == (t if down else n - k + t), row, out)
    return out


def _inproj_fwd(x, g, wint, l, tm):
    T = x.shape[0]

    def body(x_ref, g_ref, w_ref, gt_ref, q_ref, k_ref, v_ref, h_ref):
        h = _rms(x_ref[...], g_ref[...]).astype(BF)
        h_ref[...] = h
        q_ref[...] = _dot(h, w_ref[0:NQ, :], NT).astype(BF)
        k_ref[...] = _dot(h, w_ref[NQ:NQ + NKV, :], NT).astype(BF)
        v_ref[...] = _dot(h, w_ref[NQ + NKV:NQ + 2 * NKV, :], NT).astype(BF)
        for s in range(5):
            lo = NQ + 2 * NKV + s * D
            gt_ref[:, s * D:(s + 1) * D] = _dot(h, w_ref[lo:lo + D, :], NT).astype(BF)

    tok = lambda w: pl.BlockSpec((tm, w), lambda i: (i, 0))
    return pl.pallas_call(
        body, name=f"inproj_fwd{l}", grid=(T // tm,),
        in_specs=[tok(D), _resident((1, D), lambda i: (0, 0)), _resident((None, NP, D), lambda i: (l, 0, 0))],
        out_specs=[tok(NG), tok(NQ), tok(NKV), tok(NKV), tok(D)],
        out_shape=[jax.ShapeDtypeStruct((T, w), BF) for w in (NG, NQ, NKV, NKV, D)],
        compiler_params=_params(("arbitrary",)),
    )(x, g, wint)


def _head_cols(sm_ref, row, head0):
    gidx = lax.broadcasted_iota(jnp.int32, (GQ * WIN, 1), 0) // WIN
    col = jnp.zeros((GQ * WIN, 1), F32)
    for g in range(GQ):
        col = jnp.where(gidx == g, sm_ref[row, head0 + g], col)
    return col


def _band_geometry():
    r = lax.broadcasted_iota(jnp.int32, (GQ * WIN, 2 * WIN), 0) % WIN
    j = lax.broadcasted_iota(jnp.int32, (GQ * WIN, 2 * WIN), 1)
    dist = WIN + r - j
    return j, dist, (dist >= 0) & (dist < WIN)


def _band_probs(q_ref, k_ref, i, kvh, bias, sink, j, inwin):
    prev = jnp.maximum(i - 1, 0)
    r0 = pl.multiple_of(i * WIN, WIN)
    p0 = pl.multiple_of(prev * WIN, WIN)
    q = q_ref[pl.ds(r0, WIN), kvh * GQ * HD:(kvh + 1) * GQ * HD]
    qs = jnp.concatenate([q[:, g * HD:(g + 1) * HD] for g in range(GQ)], axis=0)
    kb = jnp.concatenate([k_ref[pl.ds(p0, WIN), kvh * HD:(kvh + 1) * HD],
                          k_ref[pl.ds(r0, WIN), kvh * HD:(kvh + 1) * HD]], axis=0)
    s = _dot(qs, kb, NT) * (HD ** -0.5) + bias
    s = jnp.where(inwin & ((j >= WIN) | (i > 0)), s, NEG)
    m = jnp.maximum(jnp.max(s, axis=-1, keepdims=True), sink)
    p = jnp.exp(s - m)
    ps = jnp.exp(sink - m)
    den = jnp.sum(p, axis=-1, keepdims=True) + ps
    return qs, kb, p / den, ps / den, r0, p0


def _attn_fwd(q, k, v, sm, l, S):
    T = q.shape[0]
    nblk = S // WIN

    def body(sm_ref, q_ref, k_ref, v_ref, o_ref):
        pj = pl.program_id(1)
        j, dist, inwin = _band_geometry()
        distf = dist.astype(F32)
        for kvh in range(2):
            head0 = pj * 8 + kvh * GQ
            sink = _head_cols(sm_ref, 0, head0)
            bias = -_head_cols(sm_ref, 1, head0) * distf

            def blk(i, c):
                _, _, pn, _, r0, p0 = _band_probs(q_ref, k_ref, i, kvh, bias, sink, j, inwin)
                vb = jnp.concatenate([v_ref[pl.ds(p0, WIN), kvh * HD:(kvh + 1) * HD],
                                      v_ref[pl.ds(r0, WIN), kvh * HD:(kvh + 1) * HD]], axis=0)
                o = _dot(pn.astype(BF), vb, NN)
                for g in range(GQ):
                    c0 = (kvh * GQ + g) * HD
                    o_ref[pl.ds(r0, WIN), c0:c0 + HD] = o[g * WIN:(g + 1) * WIN].astype(BF)
                return c

            lax.fori_loop(0, nblk, blk, 0)

    return pl.pallas_call(
        body, name=f"attn_fwd{l}", grid=(T // S, 2),
        in_specs=[pl.BlockSpec(memory_space=pltpu.SMEM),
                  pl.BlockSpec((S, 512), lambda s, p: (s, p)),
                  pl.BlockSpec((S, 128), lambda s, p: (s, p)),
                  pl.BlockSpec((S, 128), lambda s, p: (s, p))],
        out_specs=pl.BlockSpec((S, 512), lambda s, p: (s, p)),
        out_shape=jax.ShapeDtypeStruct((T, D), BF),
        compiler_params=_params(("arbitrary", "arbitrary")),
    )(sm, q, k, v)


def _conv_taps(y, hy1, hy2, cw_ref):
    y1 = _shift_rows(y, 1, [hy1], True)
    y2 = _shift_rows(y, 2, [hy2, hy1], True)
    z = cw_ref[0:1, :] * y2 + cw_ref[1:2, :] * y1 + cw_ref[2:3, :] * y
    return z, y1, y2


def _halo_products(cch_ref, cuh_ref, first):
    hy1 = cch_ref[15:16, :].astype(F32) * cuh_ref[15:16, :].astype(F32)
    hy2 = cch_ref[14:15, :].astype(F32) * cuh_ref[14:15, :].astype(F32)
    return jnp.where(first, 0.0, hy1), jnp.where(first, 0.0, hy2)


def _mixer_fwd(x, gt, att, w3, bg, cw, cbias, l, S, tm):
    T = x.shape[0]

    def body(x_ref, cb_ref, cc_ref, cu_ref, ga_ref, gc_ref, cch_ref, cuh_ref, att_ref, wao_ref, wco_ref, wo_ref,
             bg_ref, cw_ref, cbias_ref, x1_ref, mg_ref, co_ref, ya_ref, yc_ref):
        first = (pl.program_id(0) * tm) % S == 0
        y = cc_ref[...].astype(F32) * cu_ref[...].astype(F32)
        hy1, hy2 = _halo_products(cch_ref, cuh_ref, first)
        z, _, _ = _conv_taps(y, hy1, hy2, cw_ref)
        co = (cb_ref[...].astype(F32) * (z + cbias_ref[...])).astype(BF)
        co_ref[...] = co
        yc = _dot(co, wco_ref[...], NN)
        ya = _dot(att_ref[...], wao_ref[...], NN)
        ya_ref[...] = ya
        yc_ref[...] = yc
        sa = jax.nn.sigmoid(ga_ref[...].astype(F32) + bg_ref[0:1, :])
        sc = jax.nn.sigmoid(gc_ref[...].astype(F32) + bg_ref[1:2, :])
        mg = (sa * ya + sc * yc).astype(BF)
        mg_ref[...] = mg
        x1_ref[...] = x_ref[...] + _dot(mg, wo_ref[...], NN)

    tok = lambda: pl.BlockSpec((tm, D), lambda i: (i, 0))
    seg = lambda s: pl.BlockSpec((tm, D), lambda i: (i, s))
    halo = lambda s: pl.BlockSpec((16, D), lambda i: (jnp.maximum(i * (tm // 16) - 1, 0), s))
    wsp = lambda k: _resident((None, None, D, D), lambda i: (l, k, 0, 0))
    row = lambda n: _resident((n, D), lambda i: (0, 0))
    return pl.pallas_call(
        body, name=f"mixer_fwd{l}", grid=(T // tm,),
        in_specs=[tok(), seg(0), seg(1), seg(2), seg(3), seg(4), halo(1), halo(2), tok(), wsp(0), wsp(1), wsp(2),
                  row(2), row(8), row(1)],
        out_specs=[tok(), tok(), tok(), tok(), tok()],
        out_shape=[jax.ShapeDtypeStruct((T, D), dt) for dt in (F32, BF, BF, F32, F32)],
        compiler_params=_params(("arbitrary",)),
    )(x, gt, gt, gt, gt, gt, gt, gt, att, w3, w3, w3, bg, cw, cbias)


def _mlp_fwd(x1, g, w2, l, tm):
    T = x1.shape[0]
    FC = 1024

    def body(x_ref, g_ref, wup_ref, wdn_ref, x2_ref, h_ref, a_ref):
        x = x_ref[...]
        h = _rms(x, g_ref[...]).astype(BF)
        h_ref[...] = h
        acc = x
        for c in range(F // FC):
            a = _dot(h, wup_ref[c * FC:(c + 1) * FC, :], NT)
            a_ref[:, c * FC:(c + 1) * FC] = a.astype(BF)
            u = jnp.maximum(a, 0.0)
            acc = acc + _dot((u * u).astype(BF), wdn_ref[c * FC:(c + 1) * FC, :], NN)
        x2_ref[...] = acc

    tok = lambda w: pl.BlockSpec((tm, w), lambda i: (i, 0))
    wsp = lambda k: _resident((None, None, F, D), lambda i: (l, k, 0, 0))
    return pl.pallas_call(
        body, name=f"mlp_fwd{l}", grid=(T // tm,),
        in_specs=[tok(D), _resident((1, D), lambda i: (0, 0)), wsp(0), wsp(1)],
        out_specs=[tok(D), tok(D), tok(F)],
        out_shape=[jax.ShapeDtypeStruct((T, D), F32), jax.ShapeDtypeStruct((T, D), BF),
                   jax.ShapeDtypeStruct((T, F), BF)],
        compiler_params=_params(("arbitrary",)),
    )(x1, g, w2, w2)


def _loss_head(x, target, g, tm):
    T = x.shape[0]
    nt = T // tm

    def body(x_ref, t_ref, g_ref, dx_ref, acc_ref):
        i = pl.program_id(0)

        @pl.when(i == 0)
        def _():
            acc_ref[...] = jnp.zeros_like(acc_ref)

        x = x_ref[...]
        gv = g_ref[...]
        err = _rms(x, gv) - t_ref[...]
        dx, dg = _rms_bwd(err * (1.0 / D), x, gv)
        dx_ref[...] = dx
        acc_ref[0:1, :] += dg
        acc_ref[1:2, :] += jnp.sum(err * err, axis=0, keepdims=True)

        @pl.when(i == nt - 1)
        def _():
            acc_ref[1:2, :] = jnp.zeros((1, D), F32) + (0.5 / D) * jnp.sum(acc_ref[1:2, :])

    tok = lambda: pl.BlockSpec((tm, D), lambda i: (i, 0))
    return pl.pallas_call(
        body, name="loss_head", grid=(nt,),
        in_specs=[tok(), tok(), _resident((1, D), lambda i: (0, 0))],
        out_specs=[tok(), pl.BlockSpec((8, D), lambda i: (0, 0))],
        out_shape=[jax.ShapeDtypeStruct((T, D), F32), jax.ShapeDtypeStruct((8, D), F32)],
        compiler_params=_params(("arbitrary",)),
    )(x, target, g)


def _mlp_bwd(dx2, x1, a, g, w2, l, tm):
    T = dx2.shape[0]
    FC = 1024

    def body(d_ref, x_ref, a_ref, g_ref, wup_ref, wdn_ref, da_ref, dx1_ref, db_ref, acc_ref):
        @pl.when(pl.program_id(0) == 0)
        def _():
            acc_ref[...] = jnp.zeros_like(acc_ref)

        d = d_ref[...]
        db = d.astype(BF)
        db_ref[...] = db
        dh = jnp.zeros((tm, D), F32)
        for c in range(F // FC):
            du = _dot(db, wdn_ref[c * FC:(c + 1) * FC, :], NT)
            da = (du * (2.0 * jnp.maximum(a_ref[:, c * FC:(c + 1) * FC].astype(F32), 0.0))).astype(BF)
            da_ref[:, c * FC:(c + 1) * FC] = da
            dh = dh + _dot(da, wup_ref[c * FC:(c + 1) * FC, :], NN)
        dx, dg = _rms_bwd(dh, x_ref[...], g_ref[...])
        dx1_ref[...] = d + dx
        acc_ref[0:1, :] += dg

    tok = lambda w: pl.BlockSpec((tm, w), lambda i: (i, 0))
    wsp = lambda k: _resident((None, None, F, D), lambda i: (l, k, 0, 0))
    return pl.pallas_call(
        body, name=f"mlp_bwd{l}", grid=(T // tm,),
        in_specs=[tok(D), tok(D), tok(F), _resident((1, D), lambda i: (0, 0)), wsp(0), wsp(1)],
        out_specs=[tok(F), tok(D), tok(D), pl.BlockSpec((8, D), lambda i: (0, 0))],
        out_shape=[jax.ShapeDtypeStruct((T, F), BF), jax.ShapeDtypeStruct((T, D), F32),
                   jax.ShapeDtypeStruct((T, D), BF), jax.ShapeDtypeStruct((8, D), F32)],
        compiler_params=_params(("arbitrary",)),
    )(dx2, x1, a, g, w2, w2)


def _mixer_bwd(dx1, gt, ya, yc, w3, bg, cw, cbias, l, S, tm):
    T = dx1.shape[0]
    nt = T // tm

    def body(d_ref, cb_ref, cc_ref, cu_ref, ga_ref, gc_ref, cch_ref, cuh_ref, ya_ref, yc_ref, wao_ref, wco_ref, wo_ref,
             bg_ref, cw_ref, cbias_ref, dg_ref, datt_ref, dya_ref, dyc_ref, db_ref, acc_ref, carry_ref):
        i = pl.program_id(0)
        ti = nt - 1 - i

        @pl.when(i == 0)
        def _():
            acc_ref[...] = jnp.zeros_like(acc_ref)

        @pl.when(((ti + 1) * tm) % S == 0)
        def _():
            carry_ref[...] = jnp.zeros_like(carry_ref)

        db = d_ref[...].astype(BF)
        db_ref[...] = db
        dm = _dot(db, wo_ref[...], NT)
        sa = jax.nn.sigmoid(ga_ref[...].astype(F32) + bg_ref[0:1, :])
        sc = jax.nn.sigmoid(gc_ref[...].astype(F32) + bg_ref[1:2, :])
        dya = (dm * sa).astype(BF)
        dyc = (dm * sc).astype(BF)
        dya_ref[...] = dya
        dyc_ref[...] = dyc
        dga = dm * ya_ref[...] * sa * (1.0 - sa)
        dgc = dm * yc_ref[...] * sc * (1.0 - sc)
        dg_ref[:, 3 * D:4 * D] = dga.astype(BF)
        dg_ref[:, 4 * D:5 * D] = dgc.astype(BF)
        acc_ref[0:1, :] += jnp.sum(dga, axis=0, keepdims=True)
        acc_ref[1:2, :] += jnp.sum(dgc, axis=0, keepdims=True)
        datt_ref[...] = _dot(dya, wao_ref[...], NT).astype(BF)
        dco = _dot(dyc, wco_ref[...], NT)

        cc = cc_ref[...].astype(F32)
        cu = cu_ref[...].astype(F32)
        cb = cb_ref[...].astype(F32)
        y = cc * cu
        hy1, hy2 = _halo_products(cch_ref, cuh_ref, (ti * tm) % S == 0)
        z, y1, y2 = _conv_taps(y, hy1, hy2, cw_ref)
        dg_ref[:, 0:D] = (dco * (z + cbias_ref[...])).astype(BF)
        dz = dco * cb
        acc_ref[2:3, :] += jnp.sum(dz, axis=0, keepdims=True)
        acc_ref[3:4, :] += jnp.sum(dz * y2, axis=0, keepdims=True)
        acc_ref[4:5, :] += jnp.sum(dz * y1, axis=0, keepdims=True)
        acc_ref[5:6, :] += jnp.sum(dz * y, axis=0, keepdims=True)
        n1 = carry_ref[0:1, :]
        n2 = carry_ref[1:2, :]
        dzu1 = _shift_rows(dz, 1, [n1], False)
        dzu2 = _shift_rows(dz, 2, [n1, n2], False)
        dy = cw_ref[2:3, :] * dz + cw_ref[1:2, :] * dzu1 + cw_ref[0:1, :] * dzu2
        dg_ref[:, D:2 * D] = (dy * cu).astype(BF)
        dg_ref[:, 2 * D:3 * D] = (dy * cc).astype(BF)
        carry_ref[...] = dz[0:8, :]

    tok = lambda w=D: pl.BlockSpec((tm, w), lambda i: (nt - 1 - i, 0))
    seg = lambda s: pl.BlockSpec((tm, D), lambda i: (nt - 1 - i, s))
    halo = lambda s: pl.BlockSpec((16, D), lambda i: (jnp.maximum((nt - 1 - i) * (tm // 16) - 1, 0), s))
    wsp = lambda k: _resident((None, None, D, D), lambda i: (l, k, 0, 0))
    row = lambda n: _resident((n, D), lambda i: (0, 0))
    return pl.pallas_call(
        body, name=f"mixer_bwd{l}", grid=(nt,),
        in_specs=[tok(), seg(0), seg(1), seg(2), seg(3), seg(4), halo(1), halo(2), tok(), tok(), wsp(0), wsp(1), wsp(2),
                  row(2), row(8), row(1)],
        out_specs=[tok(NG), tok(), tok(), tok(), tok(), pl.BlockSpec((8, D), lambda i: (0, 0))],
        out_shape=[jax.ShapeDtypeStruct((T, NG), BF)] + [jax.ShapeDtypeStruct((T, D), BF)] * 4
        + [jax.ShapeDtypeStruct((8, D), F32)],
        scratch_shapes=[pltpu.VMEM((8, D), F32)],
        compiler_params=_params(("arbitrary",)),
    )(dx1, gt, gt, gt, gt, gt, gt, gt, ya, yc, w3, w3, w3, bg, cw, cbias)


def _attn_bwd(q, k, v, att, datt, sm, l, S):
    T = q.shape[0]
    nblk = S // WIN
    scale = HD ** -0.5

    def body(sm_ref, q_ref, k_ref, v_ref, o_ref, do_ref, dq_ref, dk_ref, dv_ref, ds_ref, dka_ref, dva_ref):
        pj = pl.program_id(1)
        j, dist, inwin = _band_geometry()
        distf = dist.astype(F32)
        dka_ref[...] = jnp.zeros_like(dka_ref)
        dva_ref[...] = jnp.zeros_like(dva_ref)
        for kvh in range(2):
            head0 = pj * 8 + kvh * GQ
            sink = _head_cols(sm_ref, 0, head0)
            bias = -_head_cols(sm_ref, 1, head0) * distf
            hc = slice(kvh * HD, (kvh + 1) * HD)

            def blk(i, dsink):
                qs, kb, pn, psn, r0, p0 = _band_probs(q_ref, k_ref, i, kvh, bias, sink, j, inwin)
                vb = jnp.concatenate([v_ref[pl.ds(p0, WIN), hc], v_ref[pl.ds(r0, WIN), hc]], axis=0)
                qc = slice(kvh * GQ * HD, (kvh + 1) * GQ * HD)
                do = do_ref[pl.ds(r0, WIN), qc]
                o = o_ref[pl.ds(r0, WIN), qc]
                dos = jnp.concatenate([do[:, g * HD:(g + 1) * HD] for g in range(GQ)], axis=0)
                os_ = jnp.concatenate([o[:, g * HD:(g + 1) * HD] for g in range(GQ)], axis=0)
                delta = jnp.sum(dos.astype(F32) * os_.astype(F32), axis=-1, keepdims=True)
                dp = _dot(dos, vb, NT)
                dsb = (pn * (dp - delta)).astype(BF)
                dqs = _dot(dsb, kb, NN) * scale
                for g in range(GQ):
                    c0 = (kvh * GQ + g) * HD
                    dq_ref[pl.ds(r0, WIN), c0:c0 + HD] = dqs[g * WIN:(g + 1) * WIN].astype(BF)
                dkb = _dot(dsb, qs, TN) * scale
                dvb = _dot(pn.astype(BF), dos, TN)
                dka_ref[pl.ds(p0, WIN), hc] += dkb[0:WIN]
                dka_ref[pl.ds(r0, WIN), hc] += dkb[WIN:2 * WIN]
                dva_ref[pl.ds(p0, WIN), hc] += dvb[0:WIN]
                dva_ref[pl.ds(r0, WIN), hc] += dvb[WIN:2 * WIN]
                return dsink - psn * delta

            dsink = lax.fori_loop(0, nblk, blk, jnp.zeros((GQ * WIN, 1), F32))
            for g in range(GQ):
                tot = jnp.sum(dsink[g * WIN:(g + 1) * WIN])
                ds_ref[kvh * GQ + g:kvh * GQ + g + 1, :] = jnp.zeros((1, 128), F32) + tot
        dk_ref[...] = dka_ref[...].astype(BF)
        dv_ref[...] = dva_ref[...].astype(BF)

    wide = lambda: pl.BlockSpec((S, 512), lambda s, p: (s, p))
    narrow = lambda: pl.BlockSpec((S, 128), lambda s, p: (s, p))
    return pl.pallas_call(
        body, name=f"attn_bwd{l}", grid=(T // S, 2),
        in_specs=[pl.BlockSpec(memory_space=pltpu.SMEM), wide(), narrow(), narrow(), wide(), wide()],
        out_specs=[wide(), narrow(), narrow(), pl.BlockSpec((None, None, 8, 128), lambda s, p: (s, p, 0, 0))],
        out_shape=[jax.ShapeDtypeStruct((T, NQ), BF), jax.ShapeDtypeStruct((T, NKV), BF),
                   jax.ShapeDtypeStruct((T, NKV), BF), jax.ShapeDtypeStruct((T // S, 2, 8, 128), F32)],
        scratch_shapes=[pltpu.VMEM((S, 128), F32), pltpu.VMEM((S, 128), F32)],
        compiler_params=_params(("arbitrary", "arbitrary")),
    )(sm, q, k, v, att, datt)


def _inproj_bwd(dgt, dq, dk, dv, x, dres, g, wint, l, tm):
    T = x.shape[0]

    def body(dg_ref, dq_ref, dk_ref, dv_ref, x_ref, dr_ref, g_ref, w_ref, dx_ref, acc_ref):
        @pl.when(pl.program_id(0) == 0)
        def _():
            acc_ref[...] = jnp.zeros_like(acc_ref)

        dh = _dot(dq_ref[...], w_ref[0:NQ, :], NN)
        dh = dh + _dot(dk_ref[...], w_ref[NQ:NQ + NKV, :], NN)
        dh = dh + _dot(dv_ref[...], w_ref[NQ + NKV:NQ + 2 * NKV, :], NN)
        dh = dh + _dot(dg_ref[...], w_ref[NQ + 2 * NKV:NP, :], NN)
        dx, dg = _rms_bwd(dh, x_ref[...], g_ref[...])
        dx_ref[...] = dr_ref[...] + dx
        acc_ref[0:1, :] += dg

    tok = lambda w: pl.BlockSpec((tm, w), lambda i: (i, 0))
    return pl.pallas_call(
        body, name=f"inproj_bwd{l}", grid=(T // tm,),
        in_specs=[tok(NG), tok(NQ), tok(NKV), tok(NKV), tok(D), tok(D), _resident((1, D), lambda i: (0, 0)),
                  _resident((None, NP, D), lambda i: (l, 0, 0))],
        out_specs=[tok(D), pl.BlockSpec((8, D), lambda i: (0, 0))],
        out_shape=[jax.ShapeDtypeStruct((T, D), F32), jax.ShapeDtypeStruct((8, D), F32)],
        compiler_params=_params(("arbitrary",)),
    )(dgt, dq, dk, dv, x, dres, g, wint)


def _wgrad(a, b, l, rows, row0, into, name, relu2=False):
    T, M = a.shape
    tmm = next(t for t in (1024, 512, 256) if M % t == 0 and row0 % t == 0)
    tk = min(2048, T)
    nk = T // tk
    blk0 = row0 // tmm

    def body(*refs):
        a_ref, b_ref = refs[0], refs[1]
        o_ref, acc_ref = refs[-2], refs[-1]
        kk = pl.program_id(1)

        @pl.when(kk == 0)
        def _():
            acc_ref[...] = jnp.zeros_like(acc_ref)

        av = a_ref[...]
        if relu2:
            t = jnp.maximum(av.astype(F32), 0.0)
            av = (t * t).astype(BF)
        acc_ref[...] += _dot(av, b_ref[...], TN)

        @pl.when(kk == nk - 1)
        def _():
            o_ref[...] = acc_ref[...].astype(BF)

    in_specs = [pl.BlockSpec((tk, tmm), lambda j, kk: (kk, j)), pl.BlockSpec((tk, D), lambda j, kk: (kk, 0))]
    args = [a, b]
    aliases = {}
    if into is not None:
        in_specs.append(pl.BlockSpec(memory_space=pl.ANY))
        args.append(into)
        aliases = {2: 0}
    return pl.pallas_call(
        body, name=name, grid=(M // tmm, nk),
        in_specs=in_specs,
        out_specs=pl.BlockSpec((None, tmm, D), lambda j, kk: (l, blk0 + j, 0)),
        out_shape=jax.ShapeDtypeStruct((L, rows, D), BF),
        scratch_shapes=[pltpu.VMEM((tmm, D), F32)],
        input_output_aliases=aliases,
        compiler_params=_params(("arbitrary", "arbitrary")),
    )(*args)


def _position():
    x, y, c = lax.axis_index("x"), lax.axis_index("y"), lax.axis_index("c")
    return x, y, c


def _all_gather(shards):
    n = len(shards)

    def body(*refs):
        src, dst = refs[:n], refs[n:2 * n]
        send_sems, recv_sems, local_sems = refs[2 * n:]
        x, y, c = _position()
        me, sibling = (x, y, c), (x, y, 1 - c)
        chips = [(1 - x, y), (x, 1 - y), (1 - x, 1 - y)]

        def rows(a, p):
            return dst[a].at[:, :, 4 * p[0] + 2 * p[1] + p[2]]

        def copy(a, k, block, to, from_src=False):
            return pltpu.make_async_remote_copy(
                src_ref=src[a] if from_src else rows(a, block), dst_ref=rows(a, block),
                send_sem=send_sems.at[a, k], recv_sem=recv_sems.at[a, k], device_id=to, device_id_type=MESH)

        mine = [pltpu.make_async_copy(src[a], rows(a, me), local_sems.at[a]) for a in range(n)]
        for cp in mine:
            cp.start()
        first = []
        for a in range(n):
            first.append(copy(a, 0, me, sibling, True))
            first += [copy(a, 1 + t, me, (*chip, c), True) for t, chip in enumerate(chips)]
        for cp in first:
            cp.start()
        passed = []
        for t, chip in enumerate(chips):
            for a in range(n):
                copy(a, 1 + t, (*chip, c), me).wait_recv()
                cp = copy(a, 4 + t, (*chip, c), sibling)
                cp.start()
                passed.append(cp)
        for a in range(n):
            copy(a, 0, sibling, me).wait_recv()
            for t, chip in enumerate(chips):
                copy(a, 4 + t, (*chip, 1 - c), me).wait_recv()
        for cp in first + passed:
            cp.wait_send()
        for cp in mine:
            cp.wait()

    anyspec = pl.BlockSpec(memory_space=pl.ANY)
    return pl.pallas_call(
        body, name="all_gather_weights",
        in_specs=[anyspec] * n, out_specs=[anyspec] * n,
        out_shape=[jax.ShapeDtypeStruct(s.shape[:2] + (NDEV,) + s.shape[2:], s.dtype) for s in shards],
        scratch_shapes=[pltpu.SemaphoreType.DMA((n, 7)), pltpu.SemaphoreType.DMA((n, 7)),
                        pltpu.SemaphoreType.DMA((n,))],
        compiler_params=pltpu.CompilerParams(has_side_effects=True),
    )(*shards)


def _exchange_grads(grads, small):
    n = len(grads)

    def body(*refs):
        src, small_ref = refs[:n], refs[n]
        land, small_land = refs[n + 1:2 * n + 1], refs[2 * n + 1]
        send_sems, recv_sems, local_sems = refs[2 * n + 2:]
        x, y, c = _position()
        me = 4 * x + 2 * y + c

        def parts(peer_idx):
            out = [(src[a].at[:, peer_idx], land[a].at[me]) for a in range(n)]
            return out + [(small_ref, small_land.at[me])]

        local = [pltpu.make_async_copy(s, d, local_sems.at[a]) for a, (s, d) in enumerate(parts(me))]
        for cp in local:
            cp.start()
        sent = []
        for rel in range(1, NDEV):
            px = 1 - x if rel & 4 else x
            py = 1 - y if rel & 2 else y
            pc = 1 - c if rel & 1 else c
            for a, (s, d) in enumerate(parts(4 * px + 2 * py + pc)):
                cp = pltpu.make_async_remote_copy(
                    src_ref=s, dst_ref=d, send_sem=send_sems.at[a, rel - 1], recv_sem=recv_sems.at[a, rel - 1],
                    device_id=(px, py, pc), device_id_type=MESH)
                cp.start()
                sent.append(cp)
        for cp in sent:
            cp.wait_recv()
        for cp in sent:
            cp.wait_send()
        for cp in local:
            cp.wait()

    anyspec = pl.BlockSpec(memory_space=pl.ANY)
    out_shape = [jax.ShapeDtypeStruct((NDEV, g.shape[0]) + g.shape[2:], g.dtype) for g in grads]
    out_shape.append(jax.ShapeDtypeStruct((NDEV,) + small.shape, small.dtype))
    return pl.pallas_call(
        body, name="exchange_grads",
        in_specs=[anyspec] * (n + 1), out_specs=[anyspec] * (n + 1), out_shape=out_shape,
        scratch_shapes=[pltpu.SemaphoreType.DMA((n + 1, 7)), pltpu.SemaphoreType.DMA((n + 1, 7)),
                        pltpu.SemaphoreType.DMA((n + 1,))],
        compiler_params=pltpu.CompilerParams(has_side_effects=True),
    )(*grads, small)


def _adamw(w, g, m, v):
    m = B1 * m + (1.0 - B1) * g
    v = B2 * v + (1.0 - B2) * (g * g)
    m_hat = m / (1.0 - B1 ** STEP)
    v_hat = v / (1.0 - B2 ** STEP)
    return -LR * (m_hat / (jnp.sqrt(v_hat) + AEPS) + WD * w), m, v


def _adam_sum(land, w, m, v, name):
    _, _, r, _ = land.shape
    tr = 208 if r % 208 == 0 else (256 if r % 256 == 0 else r)

    def body(land_ref, w_ref, m_ref, v_ref, g_ref, d_ref, nm_ref, nv_ref):
        g = land_ref[0].astype(F32)
        for s in range(1, NDEV):
            g = g + land_ref[s].astype(F32)
        g_ref[...] = g
        d_ref[...], nm_ref[...], nv_ref[...] = _adamw(w_ref[...], g, m_ref[...], v_ref[...])

    blk = lambda: pl.BlockSpec((None, tr, D), lambda li, j: (li, j, 0))
    return pl.pallas_call(
        body, name=name, grid=(L, r // tr),
        in_specs=[pl.BlockSpec((NDEV, None, tr, D), lambda li, j: (0, li, j, 0)), blk(), blk(), blk()],
        out_specs=[blk()] * 4, out_shape=[jax.ShapeDtypeStruct(w.shape, F32)] * 4,
        compiler_params=_params(("arbitrary", "arbitrary")),
    )(land, w, m, v)


def _adam_small(land, w, m, v):
    def body(land_ref, w_ref, m_ref, v_ref, g_ref, d_ref, nm_ref, nv_ref):
        g = land_ref[0]
        for s in range(1, NDEV):
            g = g + land_ref[s]
        g_ref[...] = g
        d_ref[...], nm_ref[...], nv_ref[...] = _adamw(w_ref[...], g, m_ref[...], v_ref[...])

    return pl.pallas_call(
        body, name="adam_small", out_shape=[jax.ShapeDtypeStruct(w.shape, F32)] * 4,
        compiler_params=_params(),
    )(land, w, m, v)


def _pack_small(g_mix, b_gates, sinks, conv_b, g_mlp, g_final, conv_w_rows, extra):
    sink_row = jnp.zeros((1, D), F32).at[0, :2 * 16].set(sinks.reshape(-1))
    return jnp.concatenate([g_mix, b_gates.reshape(4, D), sink_row, conv_b, g_mlp, g_final.reshape(1, D),
                            conv_w_rows, extra, jnp.zeros((SMALL_ROWS - 19, D), F32)], axis=0)


def kernel(x, g_mix, w_in, b_gates, sinks, w_attn_out, conv_w, conv_b, w_conv_out, w_o, g_mlp, w_up, w_down, g_final, loss_target, m_g_mix, m_w_in, m_b_gates, m_sinks, m_w_attn_out, m_conv_w, m_conv_b, m_w_conv_out, m_w_o, m_g_mlp, m_w_up, m_w_down, m_g_final, v_g_mix, v_w_in, v_b_gates, v_sinks, v_w_attn_out, v_conv_w, v_conv_b, v_w_conv_out, v_w_o, v_g_mlp, v_w_up, v_w_down, v_g_final):
    nseq, S, _ = x.shape
    T = nseq * S
    tm_in = min(512, S)
    tm = min(256, S)
    xi, yi, ci = _position()
    me = 4 * xi + 2 * yi + ci
    tr = lambda t: jnp.swapaxes(t, 1, 2)

    win_t, wup_t = tr(w_in), tr(w_up)
    shards = [win_t.astype(BF)[:, None],
              jnp.stack([w_attn_out, w_conv_out, w_o], axis=1).astype(BF),
              jnp.stack([wup_t, w_down], axis=1).astype(BF),
              jnp.pad(conv_w, ((0, 0), (0, 5), (0, 0)))[:, None]]
    wint_g, w3_g, w2_g, cw_g = _all_gather(shards)
    wint = wint_g.reshape(L, NP, D)
    w3 = w3_g.reshape(L, 3, D, D)
    w2 = w2_g.reshape(L, 2, F, D)
    cw = jnp.swapaxes(cw_g[:, 0], 1, 2).reshape(L, 8, D)
    slopes = np.power(np.float32(2.0), -8.0 * np.arange(1, 17, dtype=np.float32) / 16).astype(np.float32)

    xf = x.reshape(T, D)
    saved = []
    cur = xf
    for l in range(L):
        sm = jnp.stack([sinks[l], jnp.asarray(slopes)])
        bg = b_gates[l].reshape(2, D)
        gt, q, k, v, h = _inproj_fwd(cur, g_mix[l:l + 1], wint, l, tm_in)
        att = _attn_fwd(q, k, v, sm, l, S)
        x1, mg, co, ya, yc = _mixer_fwd(cur, gt, att, w3, bg, cw[l], conv_b[l:l + 1], l, S, tm)
        x2, h2, a = _mlp_fwd(x1, g_mlp[l:l + 1], w2, l, tm)
        saved.append(dict(x=cur, gt=gt, q=q, k=k, v=v, h=h, att=att, x1=x1, mg=mg, co=co, ya=ya, yc=yc, h2=h2, a=a,
                          sm=sm, bg=bg))
        cur = x2

    dcur, acc_loss = _loss_head(cur, loss_target.reshape(T, D), g_final.reshape(1, D), tm_in)

    d_win = d_wao = d_wco = d_wo = d_wup = d_wdn = None
    acc_in, acc_mix, acc_mlp, dsink = [None] * L, [None] * L, [None] * L, [None] * L
    for l in reversed(range(L)):
        sv = saved[l]
        da, dx1, dx2b, acc_mlp[l] = _mlp_bwd(dcur, sv["x1"], sv["a"], g_mlp[l:l + 1], w2, l, tm)
        d_wdn = _wgrad(sv["a"], dx2b, l, F, 0, d_wdn, f"wgrad_down{l}", relu2=True)
        d_wup = _wgrad(da, sv["h2"], l, F, 0, d_wup, f"wgrad_up{l}")
        dgt, datt, dya, dyc, dx1b, acc_mix[l] = _mixer_bwd(dx1, sv["gt"], sv["ya"], sv["yc"], w3, sv["bg"], cw[l],
                                                         conv_b[l:l + 1], l, S, tm)
        d_wo = _wgrad(sv["mg"], dx1b, l, D, 0, d_wo, f"wgrad_o{l}")
        d_wao = _wgrad(sv["att"], dya, l, D, 0, d_wao, f"wgrad_attn_out{l}")
        d_wco = _wgrad(sv["co"], dyc, l, D, 0, d_wco, f"wgrad_conv_out{l}")
        dq, dk, dv, dsink[l] = _attn_bwd(sv["q"], sv["k"], sv["v"], sv["att"], datt, sv["sm"], l, S)
        dcur, acc_in[l] = _inproj_bwd(dgt, dq, dk, dv, sv["x"], dx1, g_mix[l:l + 1], wint, l, tm_in)
        d_win = _wgrad(dq, sv["h"], l, NP, 0, d_win, f"wgrad_q{l}")
        d_win = _wgrad(dk, sv["h"], l, NP, NQ, d_win, f"wgrad_k{l}")
        d_win = _wgrad(dv, sv["h"], l, NP, NQ + NKV, d_win, f"wgrad_v{l}")
        d_win = _wgrad(dgt, sv["h"], l, NP, NQ + 2 * NKV, d_win, f"wgrad_gates{l}")

    dsinks = jnp.stack([dsink[l][:, :, :, 0].sum(axis=0).reshape(16) for l in range(L)])
    small = _pack_small(
        jnp.concatenate([acc_in[l][0:1] for l in range(L)]),
        jnp.stack([acc_mix[l][0:2].reshape(2 * D) for l in range(L)]),
        dsinks,
        jnp.concatenate([acc_mix[l][2:3] for l in range(L)]),
        jnp.concatenate([acc_mlp[l][0:1] for l in range(L)]),
        acc_loss[0],
        jnp.concatenate([acc_mix[l][3:6] for l in range(L)]),
        acc_loss[1:2])

    grads = [d_win.reshape(L, NDEV, NP // NDEV, D), d_wao.reshape(L, NDEV, D // NDEV, D),
             d_wco.reshape(L, NDEV, D // NDEV, D), d_wo.reshape(L, NDEV, D // NDEV, D),
             d_wup.reshape(L, NDEV, F // NDEV, D), d_wdn.reshape(L, NDEV, F // NDEV, D)]
    *lands, small_land = _exchange_grads(grads, small)

    big = {}
    for name, land, (w_, m_, v_) in zip(
            ["w_in", "w_attn_out", "w_conv_out", "w_o", "w_up", "w_down"], lands,
            [(win_t, tr(m_w_in), tr(v_w_in)), (w_attn_out, m_w_attn_out, v_w_attn_out),
             (w_conv_out, m_w_conv_out, v_w_conv_out), (w_o, m_w_o, v_w_o),
             (wup_t, tr(m_w_up), tr(v_w_up)), (w_down, m_w_down, v_w_down)]):
        outs = _adam_sum(land, w_, m_, v_, f"adam_{name}")
        big[name] = [tr(o) for o in outs] if name in ("w_in", "w_up") else outs

    def place(cw_shard):
        return lax.dynamic_update_slice(jnp.zeros((2 * 3, D), F32), cw_shard.reshape(2 * 3, D // NDEV), (0, me * (D // NDEV)))

    zero_row = jnp.zeros((1, D), F32)
    packs = [_pack_small(a, b, c_, d_, e, f, place(g_), zero_row) for a, b, c_, d_, e, f, g_ in
             [(g_mix, b_gates, sinks, conv_b, g_mlp, g_final, conv_w),
              (m_g_mix, m_b_gates, m_sinks, m_conv_b, m_g_mlp, m_g_final, m_conv_w),
              (v_g_mix, v_b_gates, v_sinks, v_conv_b, v_g_mlp, v_g_final, v_conv_w)]]
    small_out = _adam_small(small_land, *packs)

    def unpack(p):
        cwp = lax.dynamic_slice(p[12:18], (0, me * (D // NDEV)), (6, D // NDEV)).reshape(L, 3, D // NDEV)
        return dict(g_mix=p[0:2], b_gates=p[2:6].reshape(L, 2 * D), sinks=p[6, :32].reshape(L, 16), conv_b=p[7:9],
                    g_mlp=p[9:11], g_final=p[11], conv_w=cwp)

    small_kinds = [unpack(p) for p in small_out]
    order = ["g_mix", "w_in", "b_gates", "sinks", "w_attn_out", "conv_w", "conv_b", "w_conv_out", "w_o", "g_mlp",
             "w_up", "w_down", "g_final"]
    out = [small_out[0][18, 0], dcur.reshape(nseq, S, D)]
    for kind in range(4):
        for name in order:
            out.append(big[name][kind] if name in big else small_kinds[kind][name])
    return tuple(out)
```

```python
import numpy as np
import jax
import jax.numpy as jnp
from jax import lax
from jax.experimental import pallas as pl
from jax.experimental.pallas import tpu as pltpu

D = 1024
NG = 5 * D
NQ = 1024
NKV = 256
NP = NQ + 2 * NKV + NG
F = 4096
HD = 64
GQ = 4
WIN = 128
L = 2
NDEV = 8
EPS = 1e-6
NEG = -1e30
SMALL_ROWS = 24
LR, B1, B2, AEPS, WD, STEP = 0.001, 0.9, 0.999, 1e-08, 0.01, 10

BF = jnp.bfloat16
F32 = jnp.float32
MESH = pl.DeviceIdType.MESH
VMEM_LIMIT = 60 * 1024 * 1024
ANY = pl.BlockSpec(memory_space=pl.ANY)

NN = ((1,), (0,))
NT = ((1,), (1,))
TN = ((0,), (0,))


def _dot(a, b, dims):
    return lax.dot_general(a, b, (dims, ((), ())), preferred_element_type=F32)


def _resident(shape, imap):
    return pl.BlockSpec(shape, imap, pipeline_mode=pl.Buffered(1))


def _position():
    return lax.axis_index("x"), lax.axis_index("y"), lax.axis_index("c")


class _Gather:
    def __init__(self, shards):
        n = len(shards)
        self.inputs = list(shards)
        self.out_shape = [jax.ShapeDtypeStruct((s.shape[0], NDEV) + s.shape[1:], s.dtype) for s in shards]
        self.scratch = [pltpu.SemaphoreType.DMA((n, 7)), pltpu.SemaphoreType.DMA((n, 7)), pltpu.SemaphoreType.DMA((n,))]

    def _plan(self, src, dst, sems):
        send_sems, recv_sems, local_sems = sems
        n = len(src)
        x, y, c = _position()
        me, sibling = (x, y, c), (x, y, 1 - c)
        chips = [(1 - x, y), (x, 1 - y), (1 - x, 1 - y)]

        def rows(a, p):
            return dst[a].at[:, 4 * p[0] + 2 * p[1] + p[2]]

        def copy(a, k, block, to, from_src=False):
            return pltpu.make_async_remote_copy(
                src_ref=src[a] if from_src else rows(a, block), dst_ref=rows(a, block),
                send_sem=send_sems.at[a, k], recv_sem=recv_sems.at[a, k], device_id=to, device_id_type=MESH)

        mine = [pltpu.make_async_copy(src[a], rows(a, me), local_sems.at[a]) for a in range(n)]
        first = []
        for a in range(n):
            first.append(copy(a, 0, me, sibling, True))
            first += [copy(a, 1 + t, me, (*chip, c), True) for t, chip in enumerate(chips)]
        return n, c, me, sibling, chips, copy, mine, first

    def start(self, src, dst, sems):
        *_, mine, first = self._plan(src, dst, sems)
        for cp in mine + first:
            cp.start()

    def finish(self, src, dst, sems):
        n, c, me, sibling, chips, copy, mine, first = self._plan(src, dst, sems)
        passed = []
        for t, chip in enumerate(chips):
            for a in range(n):
                copy(a, 1 + t, (*chip, c), me).wait_recv()
                cp = copy(a, 4 + t, (*chip, c), sibling)
                cp.start()
                passed.append(cp)
        for a in range(n):
            copy(a, 0, sibling, me).wait_recv()
            for t, chip in enumerate(chips):
                copy(a, 4 + t, (*chip, 1 - c), me).wait_recv()
        for cp in first + passed:
            cp.wait_send()
        for cp in mine:
            cp.wait()


class _Exchange:
    def __init__(self, grads, everyone=()):
        self.inputs = list(grads) + list(everyone)
        self.n_blocked = len(grads)
        n = len(self.inputs)
        self.out_shape = [jax.ShapeDtypeStruct(g.shape, g.dtype) for g in grads]
        self.out_shape += [jax.ShapeDtypeStruct((NDEV,) + e.shape, e.dtype) for e in everyone]
        self.scratch = [pltpu.SemaphoreType.DMA((n, 7)), pltpu.SemaphoreType.DMA((n, 7)), pltpu.SemaphoreType.DMA((n,))]

    def _plan(self, src, land, sems):
        send_sems, recv_sems, local_sems = sems
        x, y, c = _position()
        me = 4 * x + 2 * y + c

        def parts(peer_idx):
            return [(s.at[peer_idx] if a < self.n_blocked else s, land[a].at[me]) for a, s in enumerate(src)]

        local = [pltpu.make_async_copy(s, d, local_sems.at[a]) for a, (s, d) in enumerate(parts(me))]
        sent = []
        for rel in range(1, NDEV):
            px = 1 - x if rel & 4 else x
            py = 1 - y if rel & 2 else y
            pc = 1 - c if rel & 1 else c
            for a, (s, d) in enumerate(parts(4 * px + 2 * py + pc)):
                sent.append(pltpu.make_async_remote_copy(
                    src_ref=s, dst_ref=d, send_sem=send_sems.at[a, rel - 1], recv_sem=recv_sems.at[a, rel - 1],
                    device_id=(px, py, pc), device_id_type=MESH))
        return local, sent

    def start(self, src, land, sems):
        local, sent = self._plan(src, land, sems)
        for cp in local + sent:
            cp.start()

    def finish(self, src, land, sems):
        local, sent = self._plan(src, land, sems)
        for cp in sent:
            cp.wait_recv()
        for cp in sent:
            cp.wait_send()
        for cp in local:
            cp.wait()


def _call(body, *, name, grid, in_specs, out_specs, out_shape, args, scratch_shapes=(), aliases=None, side=None):
    sem = ("arbitrary",) * len(grid)
    if side is None:
        outs = pl.pallas_call(
            body, name=name, grid=grid, in_specs=in_specs, out_specs=out_specs, out_shape=out_shape,
            scratch_shapes=list(scratch_shapes), input_output_aliases=aliases or {},
            compiler_params=pltpu.CompilerParams(dimension_semantics=sem, vmem_limit_bytes=VMEM_LIMIT))(*args)
        return outs, []
    ni, no, ns = len(in_specs), len(out_specs), len(scratch_shapes)
    si, so = len(side.inputs), len(side.out_shape)

    def hosted(*refs):
        ins, refs = refs[:ni], refs[ni:]
        sins, refs = refs[:si], refs[si:]
        outs, refs = refs[:no], refs[no:]
        souts, refs = refs[:so], refs[so:]
        scr, sscr = refs[:ns], refs[ns:]
        at_first = pl.program_id(0) == 0
        at_last = pl.program_id(0) == grid[0] - 1
        for d in range(1, len(grid)):
            at_first &= pl.program_id(d) == 0
            at_last &= pl.program_id(d) == grid[d] - 1

        @pl.when(at_first)
        def _():
            side.start(sins, souts, sscr)

        body(*ins, *outs, *scr)

        @pl.when(at_last)
        def _():
            side.finish(sins, souts, sscr)

    outs = pl.pallas_call(
        hosted, name=name, grid=grid, in_specs=list(in_specs) + [ANY] * si, out_specs=list(out_specs) + [ANY] * so,
        out_shape=list(out_shape) + side.out_shape, scratch_shapes=list(scratch_shapes) + side.scratch,
        input_output_aliases=aliases or {},
        compiler_params=pltpu.CompilerParams(dimension_semantics=sem, vmem_limit_bytes=VMEM_LIMIT, has_side_effects=True),
    )(*args, *side.inputs)
    return outs[:no], outs[no:]


def _remote_only(side, name):
    n = len(side.inputs)

    def body(*refs):
        src, dst, sems = refs[:n], refs[n:n + len(side.out_shape)], refs[n + len(side.out_shape):]
        side.start(src, dst, sems)
        side.finish(src, dst, sems)

    return pl.pallas_call(
        body, name=name, in_specs=[ANY] * n, out_specs=[ANY] * len(side.out_shape), out_shape=side.out_shape,
        scratch_shapes=side.scratch, compiler_params=pltpu.CompilerParams(has_side_effects=True))(*side.inputs)


def _rms(x, g):
    r = lax.rsqrt(jnp.mean(x * x, axis=-1, keepdims=True) + EPS)
    return x * r * g


def _rms_bwd(dy, x, g):
    r = lax.rsqrt(jnp.mean(x * x, axis=-1, keepdims=True) + EPS)
    xh = x * r
    dxh = dy * g
    dx = r * (dxh - xh * jnp.mean(dxh * xh, axis=-1, keepdims=True))
    return dx, jnp.sum(dy * xh, axis=0, keepdims=True)


def _shift_rows(y, k, edge_rows, down):
    n = y.shape[0]
    rid = lax.broadcasted_iota(jnp.int32, y.shape, 0)
    out = pltpu.roll(y, k if down else n - k, 0)
    for t, row in enumerate(edge_rows):
        out = jnp.where(rid == (t if down else n - k + t), row, out)
    return out


def _zero_at_first_step(acc_ref):
    first = pl.program_id(0) == 0

    @pl.when(first)
    def _():
        acc_ref[...] = jnp.zeros_like(acc_ref)


ROW1 = lambda: _resident((1, D), lambda i: (0, 0))
ACC = lambda: pl.BlockSpec((8, D), lambda i: (0, 0))


def _inproj_fwd(x, g, wint, l, tm, side=None):
    T = x.shape[0]

    def body(x_ref, g_ref, w_ref, gt_ref, q_ref, k_ref, v_ref, h_ref):
        h = _rms(x_ref[...], g_ref[...]).astype(BF)
        h_ref[...] = h
        q_ref[...] = _dot(h, w_ref[0:NQ, :], NT).astype(BF)
        k_ref[...] = _dot(h, w_ref[NQ:NQ + NKV, :], NT).astype(BF)
        v_ref[...] = _dot(h, w_ref[NQ + NKV:NQ + 2 * NKV, :], NT).astype(BF)
        for s in range(5):
            lo = NQ + 2 * NKV + s * D
            gt_ref[:, s * D:(s + 1) * D] = _dot(h, w_ref[lo:lo + D, :], NT).astype(BF)

    tok = lambda w: pl.BlockSpec((tm, w), lambda i: (i, 0))
    return _call(
        body, name=f"inproj_fwd{l}", grid=(T // tm,),
        in_specs=[tok(D), ROW1(), _resident((NP, D), lambda i: (0, 0))],
        out_specs=[tok(NG), tok(NQ), tok(NKV), tok(NKV), tok(D)],
        out_shape=[jax.ShapeDtypeStruct((T, w), BF) for w in (NG, NQ, NKV, NKV, D)],
        args=(x, g, wint), side=side)


def _head_cols(sm_ref, row, head0):
    gidx = lax.broadcasted_iota(jnp.int32, (GQ * WIN, 1), 0) // WIN
    col = jnp.zeros((GQ * WIN, 1), F32)
    for g in range(GQ):
        col = jnp.where(gidx == g, sm_ref[row, head0 + g], col)
    return col


def _band_geometry():
    r = lax.broadcasted_iota(jnp.int32, (GQ * WIN, 2 * WIN), 0) % WIN
    j = lax.broadcasted_iota(jnp.int32, (GQ * WIN, 2 * WIN), 1)
    dist = WIN + r - j
    return j, dist, (dist >= 0) & (dist < WIN)


def _band_probs(q_ref, k_ref, i, kvh, bias, sink, j, inwin):
    prev = jnp.maximum(i - 1, 0)
    r0 = pl.multiple_of(i * WIN, WIN)
    p0 = pl.multiple_of(prev * WIN, WIN)
    q = q_ref[pl.ds(r0, WIN), kvh * GQ * HD:(kvh + 1) * GQ * HD]
    qs = jnp.concatenate([q[:, g * HD:(g + 1) * HD] for g in range(GQ)], axis=0)
    kb = jnp.concatenate([k_ref[pl.ds(p0, WIN), kvh * HD:(kvh + 1) * HD],
                          k_ref[pl.ds(r0, WIN), kvh * HD:(kvh + 1) * HD]], axis=0)
    s = _dot(qs, kb, NT) * (HD ** -0.5) + bias
    s = jnp.where(inwin & ((j >= WIN) | (i > 0)), s, NEG)
    m = jnp.maximum(jnp.max(s, axis=-1, keepdims=True), sink)
    p = jnp.exp(s - m)
    ps = jnp.exp(sink - m)
    den = jnp.sum(p, axis=-1, keepdims=True) + ps
    return qs, kb, p / den, ps / den, r0, p0


def _attn_fwd(q, k, v, sm, l, S, side=None):
    T = q.shape[0]
    nblk = S // WIN

    def body(sm_ref, q_ref, k_ref, v_ref, o_ref):
        pj = pl.program_id(1)
        j, dist, inwin = _band_geometry()
        distf = dist.astype(F32)
        for kvh in range(2):
            head0 = pj * 8 + kvh * GQ
            sink = _head_cols(sm_ref, 0, head0)
            bias = -_head_cols(sm_ref, 1, head0) * distf

            def blk(i, c):
                _, _, pn, _, r0, p0 = _band_probs(q_ref, k_ref, i, kvh, bias, sink, j, inwin)
                vb = jnp.concatenate([v_ref[pl.ds(p0, WIN), kvh * HD:(kvh + 1) * HD],
                                      v_ref[pl.ds(r0, WIN), kvh * HD:(kvh + 1) * HD]], axis=0)
                o = _dot(pn.astype(BF), vb, NN)
                for g in range(GQ):
                    c0 = (kvh * GQ + g) * HD
                    o_ref[pl.ds(r0, WIN), c0:c0 + HD] = o[g * WIN:(g + 1) * WIN].astype(BF)
                return c

            lax.fori_loop(0, nblk, blk, 0)

    wide = lambda: pl.BlockSpec((S, 512), lambda s, p: (s, p))
    narrow = lambda: pl.BlockSpec((S, 128), lambda s, p: (s, p))
    return _call(
        body, name=f"attn_fwd{l}", grid=(T // S, 2),
        in_specs=[pl.BlockSpec(memory_space=pltpu.SMEM), wide(), narrow(), narrow()],
        out_specs=[wide()], out_shape=[jax.ShapeDtypeStruct((T, D), BF)],
        args=(sm, q, k, v), side=side)


def _conv_taps(y, hy1, hy2, cw_ref):
    y1 = _shift_rows(y, 1, [hy1], True)
    y2 = _shift_rows(y, 2, [hy2, hy1], True)
    z = cw_ref[0:1, :] * y2 + cw_ref[1:2, :] * y1 + cw_ref[2:3, :] * y
    return z, y1, y2


def _halo_products(cch_ref, cuh_ref, first):
    hy1 = cch_ref[15:16, :].astype(F32) * cuh_ref[15:16, :].astype(F32)
    hy2 = cch_ref[14:15, :].astype(F32) * cuh_ref[14:15, :].astype(F32)
    return jnp.where(first, 0.0, hy1), jnp.where(first, 0.0, hy2)


def _mixer_fwd(x, gt, att, w3, bg, cw, cbias, l, S, tm, side=None):
    T = x.shape[0]

    def body(x_ref, cb_ref, cc_ref, cu_ref, ga_ref, gc_ref, cch_ref, cuh_ref, att_ref, wao_ref, wco_ref, wo_ref,
             bg_ref, cw_ref, cbias_ref, x1_ref, mg_ref, co_ref, ya_ref, yc_ref):
        first = (pl.program_id(0) * tm) % S == 0
        y = cc_ref[...].astype(F32) * cu_ref[...].astype(F32)
        hy1, hy2 = _halo_products(cch_ref, cuh_ref, first)
        z, _, _ = _conv_taps(y, hy1, hy2, cw_ref)
        co = (cb_ref[...].astype(F32) * (z + cbias_ref[...])).astype(BF)
        co_ref[...] = co
        yc = _dot(co, wco_ref[...], NN)
        ya = _dot(att_ref[...], wao_ref[...], NN)
        ya_ref[...] = ya
        yc_ref[...] = yc
        sa = jax.nn.sigmoid(ga_ref[...].astype(F32) + bg_ref[0:1, :])
        sc = jax.nn.sigmoid(gc_ref[...].astype(F32) + bg_ref[1:2, :])
        mg = (sa * ya + sc * yc).astype(BF)
        mg_ref[...] = mg
        x1_ref[...] = x_ref[...] + _dot(mg, wo_ref[...], NN)

    tok = lambda: pl.BlockSpec((tm, D), lambda i: (i, 0))
    seg = lambda s: pl.BlockSpec((tm, D), lambda i: (i, s))
    halo = lambda s: pl.BlockSpec((16, D), lambda i: (jnp.maximum(i * (tm // 16) - 1, 0), s))
    wsp = lambda k: _resident((None, D, D), lambda i: (k, 0, 0))
    row = lambda n: _resident((n, D), lambda i: (0, 0))
    return _call(
        body, name=f"mixer_fwd{l}", grid=(T // tm,),
        in_specs=[tok(), seg(0), seg(1), seg(2), seg(3), seg(4), halo(1), halo(2), tok(), wsp(0), wsp(1), wsp(2),
                  row(2), row(8), row(1)],
        out_specs=[tok(), tok(), tok(), tok(), tok()],
        out_shape=[jax.ShapeDtypeStruct((T, D), dt) for dt in (F32, BF, BF, F32, F32)],
        args=(x, gt, gt, gt, gt, gt, gt, gt, att, w3, w3, w3, bg, cw, cbias), side=side)


def _mlp_fwd(x1, g, w2, l, tm, side=None):
    T = x1.shape[0]
    FC = 1024

    def body(x_ref, g_ref, wup_ref, wdn_ref, x2_ref, h_ref, a_ref):
        x = x_ref[...]
        h = _rms(x, g_ref[...]).astype(BF)
        h_ref[...] = h
        acc = x
        for c in range(F // FC):
            a = _dot(h, wup_ref[c * FC:(c + 1) * FC, :], NT)
            a_ref[:, c * FC:(c + 1) * FC] = a.astype(BF)
            u = jnp.maximum(a, 0.0)
            acc = acc + _dot((u * u).astype(BF), wdn_ref[c * FC:(c + 1) * FC, :], NN)
        x2_ref[...] = acc

    tok = lambda w: pl.BlockSpec((tm, w), lambda i: (i, 0))
    wsp = lambda k: _resident((None, F, D), lambda i: (k, 0, 0))
    return _call(
        body, name=f"mlp_fwd{l}", grid=(T // tm,),
        in_specs=[tok(D), ROW1(), wsp(0), wsp(1)],
        out_specs=[tok(D), tok(D), tok(F)],
        out_shape=[jax.ShapeDtypeStruct((T, D), F32), jax.ShapeDtypeStruct((T, D), BF),
                   jax.ShapeDtypeStruct((T, F), BF)],
        args=(x1, g, w2, w2), side=side)


def _loss_head(x, target, g, tm):
    T = x.shape[0]
    nt = T // tm

    def body(x_ref, t_ref, g_ref, dx_ref, acc_ref):
        _zero_at_first_step(acc_ref)
        x = x_ref[...]
        gv = g_ref[...]
        err = _rms(x, gv) - t_ref[...]
        dx, dg = _rms_bwd(err * (1.0 / D), x, gv)
        dx_ref[...] = dx
        acc_ref[0:1, :] += dg
        acc_ref[1:2, :] += jnp.sum(err * err, axis=0, keepdims=True)

        @pl.when(pl.program_id(0) == nt - 1)
        def _():
            acc_ref[1:2, :] = jnp.zeros((1, D), F32) + (0.5 / D) * jnp.sum(acc_ref[1:2, :])

    tok = lambda: pl.BlockSpec((tm, D), lambda i: (i, 0))
    (dx, acc), _ = _call(
        body, name="loss_head", grid=(nt,), in_specs=[tok(), tok(), ROW1()], out_specs=[tok(), ACC()],
        out_shape=[jax.ShapeDtypeStruct((T, D), F32), jax.ShapeDtypeStruct((8, D), F32)], args=(x, target, g))
    return dx, acc


def _mlp_bwd(dx2, x1, a, g, w2, l, tm, side=None):
    T = dx2.shape[0]
    FC = 1024

    def body(d_ref, x_ref, a_ref, g_ref, wup_ref, wdn_ref, da_ref, dx1_ref, db_ref, acc_ref):
        _zero_at_first_step(acc_ref)
        d = d_ref[...]
        db = d.astype(BF)
        db_ref[...] = db
        dh = jnp.zeros((tm, D), F32)
        for c in range(F // FC):
            du = _dot(db, wdn_ref[c * FC:(c + 1) * FC, :], NT)
            da = (du * (2.0 * jnp.maximum(a_ref[:, c * FC:(c + 1) * FC].astype(F32), 0.0))).astype(BF)
            da_ref[:, c * FC:(c + 1) * FC] = da
            dh = dh + _dot(da, wup_ref[c * FC:(c + 1) * FC, :], NN)
        dx, dg = _rms_bwd(dh, x_ref[...], g_ref[...])
        dx1_ref[...] = d + dx
        acc_ref[0:1, :] += dg

    tok = lambda w: pl.BlockSpec((tm, w), lambda i: (i, 0))
    wsp = lambda k: _resident((None, F, D), lambda i: (k, 0, 0))
    return _call(
        body, name=f"mlp_bwd{l}", grid=(T // tm,),
        in_specs=[tok(D), tok(D), tok(F), ROW1(), wsp(0), wsp(1)],
        out_specs=[tok(F), tok(D), tok(D), ACC()],
        out_shape=[jax.ShapeDtypeStruct((T, F), BF), jax.ShapeDtypeStruct((T, D), F32),
                   jax.ShapeDtypeStruct((T, D), BF), jax.ShapeDtypeStruct((8, D), F32)],
        args=(dx2, x1, a, g, w2, w2), side=side)


def _mixer_bwd(dx1, gt, ya, yc, w3, bg, cw, cbias, l, S, tm, side=None):
    T = dx1.shape[0]
    nt = T // tm

    def body(d_ref, cb_ref, cc_ref, cu_ref, ga_ref, gc_ref, cch_ref, cuh_ref, ya_ref, yc_ref, wao_ref, wco_ref, wo_ref,
             bg_ref, cw_ref, cbias_ref, dg_ref, datt_ref, dya_ref, dyc_ref, db_ref, acc_ref, carry_ref):
        ti = nt - 1 - pl.program_id(0)
        _zero_at_first_step(acc_ref)

        @pl.when(((ti + 1) * tm) % S == 0)
        def _():
            carry_ref[...] = jnp.zeros_like(carry_ref)

        db = d_ref[...].astype(BF)
        db_ref[...] = db
        dm = _dot(db, wo_ref[...], NT)
        sa = jax.nn.sigmoid(ga_ref[...].astype(F32) + bg_ref[0:1, :])
        sc = jax.nn.sigmoid(gc_ref[...].astype(F32) + bg_ref[1:2, :])
        dya = (dm * sa).astype(BF)
        dyc = (dm * sc).astype(BF)
        dya_ref[...] = dya
        dyc_ref[...] = dyc
        dga = dm * ya_ref[...] * sa * (1.0 - sa)
        dgc = dm * yc_ref[...] * sc * (1.0 - sc)
        dg_ref[:, 3 * D:4 * D] = dga.astype(BF)
        dg_ref[:, 4 * D:5 * D] = dgc.astype(BF)
        acc_ref[0:1, :] += jnp.sum(dga, axis=0, keepdims=True)
        acc_ref[1:2, :] += jnp.sum(dgc, axis=0, keepdims=True)
        datt_ref[...] = _dot(dya, wao_ref[...], NT).astype(BF)
        dco = _dot(dyc, wco_ref[...], NT)

        cc = cc_ref[...].astype(F32)
        cu = cu_ref[...].astype(F32)
        cb = cb_ref[...].astype(F32)
        y = cc * cu
        hy1, hy2 = _halo_products(cch_ref, cuh_ref, (ti * tm) % S == 0)
        z, y1, y2 = _conv_taps(y, hy1, hy2, cw_ref)
        dg_ref[:, 0:D] = (dco * (z + cbias_ref[...])).astype(BF)
        dz = dco * cb
        acc_ref[2:3, :] += jnp.sum(dz, axis=0, keepdims=True)
        acc_ref[3:4, :] += jnp.sum(dz * y2, axis=0, keepdims=True)
        acc_ref[4:5, :] += jnp.sum(dz * y1, axis=0, keepdims=True)
        acc_ref[5:6, :] += jnp.sum(dz * y, axis=0, keepdims=True)
        n1 = carry_ref[0:1, :]
        n2 = carry_ref[1:2, :]
        dzu1 = _shift_rows(dz, 1, [n1], False)
        dzu2 = _shift_rows(dz, 2, [n1, n2], False)
        dy = cw_ref[2:3, :] * dz + cw_ref[1:2, :] * dzu1 + cw_ref[0:1, :] * dzu2
        dg_ref[:, D:2 * D] = (dy * cu).astype(BF)
        dg_ref[:, 2 * D:3 * D] = (dy * cc).astype(BF)
        carry_ref[...] = dz[0:8, :]

    tok = lambda w=D: pl.BlockSpec((tm, w), lambda i: (nt - 1 - i, 0))
    seg = lambda s: pl.BlockSpec((tm, D), lambda i: (nt - 1 - i, s))
    halo = lambda s: pl.BlockSpec((16, D), lambda i: (jnp.maximum((nt - 1 - i) * (tm // 16) - 1, 0), s))
    wsp = lambda k: _resident((None, D, D), lambda i: (k, 0, 0))
    row = lambda n: _resident((n, D), lambda i: (0, 0))
    return _call(
        body, name=f"mixer_bwd{l}", grid=(nt,),
        in_specs=[tok(), seg(0), seg(1), seg(2), seg(3), seg(4), halo(1), halo(2), tok(), tok(), wsp(0), wsp(1), wsp(2),
                  row(2), row(8), row(1)],
        out_specs=[tok(NG), tok(), tok(), tok(), tok(), ACC()],
        out_shape=[jax.ShapeDtypeStruct((T, NG), BF)] + [jax.ShapeDtypeStruct((T, D), BF)] * 4
        + [jax.ShapeDtypeStruct((8, D), F32)],
        scratch_shapes=[pltpu.VMEM((8, D), F32)],
        args=(dx1, gt, gt, gt, gt, gt, gt, gt, ya, yc, w3, w3, w3, bg, cw, cbias), side=side)


def _attn_bwd(q, k, v, att, datt, sm, l, S, side=None):
    T = q.shape[0]
    nblk = S // WIN
    scale = HD ** -0.5

    def body(sm_ref, q_ref, k_ref, v_ref, o_ref, do_ref, dq_ref, dk_ref, dv_ref, ds_ref, dka_ref, dva_ref):
        pj = pl.program_id(1)
        j, dist, inwin = _band_geometry()
        distf = dist.astype(F32)
        dka_ref[...] = jnp.zeros_like(dka_ref)
        dva_ref[...] = jnp.zeros_like(dva_ref)
        for kvh in range(2):
            head0 = pj * 8 + kvh * GQ
            sink = _head_cols(sm_ref, 0, head0)
            bias = -_head_cols(sm_ref, 1, head0) * distf
            hc = slice(kvh * HD, (kvh + 1) * HD)

            def blk(i, dsink):
                qs, kb, pn, psn, r0, p0 = _band_probs(q_ref, k_ref, i, kvh, bias, sink, j, inwin)
                vb = jnp.concatenate([v_ref[pl.ds(p0, WIN), hc], v_ref[pl.ds(r0, WIN), hc]], axis=0)
                qc = slice(kvh * GQ * HD, (kvh + 1) * GQ * HD)
                do = do_ref[pl.ds(r0, WIN), qc]
                o = o_ref[pl.ds(r0, WIN), qc]
                dos = jnp.concatenate([do[:, g * HD:(g + 1) * HD] for g in range(GQ)], axis=0)
                os_ = jnp.concatenate([o[:, g * HD:(g + 1) * HD] for g in range(GQ)], axis=0)
                delta = jnp.sum(dos.astype(F32) * os_.astype(F32), axis=-1, keepdims=True)
                dp = _dot(dos, vb, NT)
                dsb = (pn * (dp - delta)).astype(BF)
                dqs = _dot(dsb, kb, NN) * scale
                for g in range(GQ):
                    c0 = (kvh * GQ + g) * HD
                    dq_ref[pl.ds(r0, WIN), c0:c0 + HD] = dqs[g * WIN:(g + 1) * WIN].astype(BF)
                dkb = _dot(dsb, qs, TN) * scale
                dvb = _dot(pn.astype(BF), dos, TN)
                dka_ref[pl.ds(p0, WIN), hc] += dkb[0:WIN]
                dka_ref[pl.ds(r0, WIN), hc] += dkb[WIN:2 * WIN]
                dva_ref[pl.ds(p0, WIN), hc] += dvb[0:WIN]
                dva_ref[pl.ds(r0, WIN), hc] += dvb[WIN:2 * WIN]
                return dsink - psn * delta

            dsink = lax.fori_loop(0, nblk, blk, jnp.zeros((GQ * WIN, 1), F32))
            for g in range(GQ):
                tot = jnp.sum(dsink[g * WIN:(g + 1) * WIN])
                ds_ref[kvh * GQ + g:kvh * GQ + g + 1, :] = jnp.zeros((1, 128), F32) + tot
        dk_ref[...] = dka_ref[...].astype(BF)
        dv_ref[...] = dva_ref[...].astype(BF)

    wide = lambda: pl.BlockSpec((S, 512), lambda s, p: (s, p))
    narrow = lambda: pl.BlockSpec((S, 128), lambda s, p: (s, p))
    return _call(
        body, name=f"attn_bwd{l}", grid=(T // S, 2),
        in_specs=[pl.BlockSpec(memory_space=pltpu.SMEM), wide(), narrow(), narrow(), wide(), wide()],
        out_specs=[wide(), narrow(), narrow(), pl.BlockSpec((None, None, 8, 128), lambda s, p: (s, p, 0, 0))],
        out_shape=[jax.ShapeDtypeStruct((T, NQ), BF), jax.ShapeDtypeStruct((T, NKV), BF),
                   jax.ShapeDtypeStruct((T, NKV), BF), jax.ShapeDtypeStruct((T // S, 2, 8, 128), F32)],
        scratch_shapes=[pltpu.VMEM((S, 128), F32), pltpu.VMEM((S, 128), F32)],
        args=(sm, q, k, v, att, datt), side=side)


def _inproj_bwd(dgt, dq, dk, dv, x, dres, g, wint, l, tm, side=None):
    T = x.shape[0]

    def body(dg_ref, dq_ref, dk_ref, dv_ref, x_ref, dr_ref, g_ref, w_ref, dx_ref, acc_ref):
        _zero_at_first_step(acc_ref)
        dh = _dot(dq_ref[...], w_ref[0:NQ, :], NN)
        dh = dh + _dot(dk_ref[...], w_ref[NQ:NQ + NKV, :], NN)
        dh = dh + _dot(dv_ref[...], w_ref[NQ + NKV:NQ + 2 * NKV, :], NN)
        dh = dh + _dot(dg_ref[...], w_ref[NQ + 2 * NKV:NP, :], NN)
        dx, dg = _rms_bwd(dh, x_ref[...], g_ref[...])
        dx_ref[...] = dr_ref[...] + dx
        acc_ref[0:1, :] += dg

    tok = lambda w: pl.BlockSpec((tm, w), lambda i: (i, 0))
    return _call(
        body, name=f"inproj_bwd{l}", grid=(T // tm,),
        in_specs=[tok(NG), tok(NQ), tok(NKV), tok(NKV), tok(D), tok(D), ROW1(), _resident((NP, D), lambda i: (0, 0))],
        out_specs=[tok(D), ACC()],
        out_shape=[jax.ShapeDtypeStruct((T, D), F32), jax.ShapeDtypeStruct((8, D), F32)],
        args=(dgt, dq, dk, dv, x, dres, g, wint), side=side)


def _wgrad(a, b, rows, row0, into, name, relu2=False):
    T, M = a.shape
    tmm = next(t for t in (1024, 512, 256) if M % t == 0 and row0 % t == 0)
    tk = min(2048, T)
    nk = T // tk
    blk0 = row0 // tmm

    def body(*refs):
        a_ref, b_ref = refs[0], refs[1]
        o_ref, acc_ref = refs[-2], refs[-1]
        kk = pl.program_id(1)

        @pl.when(kk == 0)
        def _():
            acc_ref[...] = jnp.zeros_like(acc_ref)

        av = a_ref[...]
        if relu2:
            t = jnp.maximum(av.astype(F32), 0.0)
            av = (t * t).astype(BF)
        acc_ref[...] += _dot(av, b_ref[...], TN)

        @pl.when(kk == nk - 1)
        def _():
            o_ref[...] = acc_ref[...].astype(BF)

    in_specs = [pl.BlockSpec((tk, tmm), lambda j, kk: (kk, j)), pl.BlockSpec((tk, D), lambda j, kk: (kk, 0))]
    args = [a, b]
    if into is not None:
        in_specs.append(ANY)
        args.append(into)
    (out,), _ = _call(
        body, name=name, grid=(M // tmm, nk), in_specs=in_specs,
        out_specs=[pl.BlockSpec((tmm, D), lambda j, kk: (blk0 + j, 0))],
        out_shape=[jax.ShapeDtypeStruct((rows, D), BF)], scratch_shapes=[pltpu.VMEM((tmm, D), F32)],
        aliases={2: 0} if into is not None else None, args=args)
    return out


def _adamw(w, g, m, v):
    m = B1 * m + (1.0 - B1) * g
    v = B2 * v + (1.0 - B2) * (g * g)
    m_hat = m / (1.0 - B1 ** STEP)
    v_hat = v / (1.0 - B2 ** STEP)
    return -LR * (m_hat / (jnp.sqrt(v_hat) + AEPS) + WD * w), m, v


def _adam_sum(land, w, m, v, l, into, name):
    _, r, _ = land.shape
    tr = 208 if r % 208 == 0 else (256 if r % 256 == 0 else r)

    def body(land_ref, w_ref, m_ref, v_ref, *rest):
        g_ref, d_ref, nm_ref, nv_ref = rest[-4:]
        g = land_ref[0].astype(F32)
        for s in range(1, NDEV):
            g = g + land_ref[s].astype(F32)
        g_ref[...] = g
        d_ref[...], nm_ref[...], nv_ref[...] = _adamw(w_ref[...], g, m_ref[...], v_ref[...])

    blk = lambda: pl.BlockSpec((None, tr, D), lambda j: (l, j, 0))
    in_specs = [pl.BlockSpec((NDEV, tr, D), lambda j: (0, j, 0)), blk(), blk(), blk()]
    args = [land, w, m, v]
    aliases = None
    if into is not None:
        in_specs += [ANY] * 4
        args += list(into)
        aliases = {4 + t: t for t in range(4)}
    outs, _ = _call(body, name=name, grid=(r // tr,), in_specs=in_specs, out_specs=[blk()] * 4,
                    out_shape=[jax.ShapeDtypeStruct(w.shape, F32)] * 4, aliases=aliases, args=args)
    return outs


def _adam_small(land, w, m, v):
    def body(land_ref, w_ref, m_ref, v_ref, g_ref, d_ref, nm_ref, nv_ref):
        g = land_ref[0]
        for s in range(1, NDEV):
            g = g + land_ref[s]
        g_ref[...] = g
        d_ref[...], nm_ref[...], nv_ref[...] = _adamw(w_ref[...], g, m_ref[...], v_ref[...])

    return pl.pallas_call(
        body, name="adam_small", out_shape=[jax.ShapeDtypeStruct(w.shape, F32)] * 4,
        compiler_params=pltpu.CompilerParams(vmem_limit_bytes=VMEM_LIMIT))(land, w, m, v)


def _pack_small(g_mix, b_gates, sinks, conv_b, g_mlp, g_final, conv_w_rows, extra):
    sink_row = jnp.zeros((1, D), F32).at[0, :2 * 16].set(sinks.reshape(-1))
    return jnp.concatenate([g_mix, b_gates.reshape(4, D), sink_row, conv_b, g_mlp, g_final.reshape(1, D),
                            conv_w_rows, extra, jnp.zeros((SMALL_ROWS - 19, D), F32)], axis=0)


def kernel(x, g_mix, w_in, b_gates, sinks, w_attn_out, conv_w, conv_b, w_conv_out, w_o, g_mlp, w_up, w_down, g_final, loss_target, m_g_mix, m_w_in, m_b_gates, m_sinks, m_w_attn_out, m_conv_w, m_conv_b, m_w_conv_out, m_w_o, m_g_mlp, m_w_up, m_w_down, m_g_final, v_g_mix, v_w_in, v_b_gates, v_sinks, v_w_attn_out, v_conv_w, v_conv_b, v_w_conv_out, v_w_o, v_g_mlp, v_w_up, v_w_down, v_g_final):
    nseq, S, _ = x.shape
    T = nseq * S
    tm_in = min(512, S)
    tm = min(256, S)
    xi, yi, ci = _position()
    me = 4 * xi + 2 * yi + ci
    tr = lambda t: jnp.swapaxes(t, 1, 2)
    blocks = lambda t: t.reshape(NDEV, t.shape[0] // NDEV, D)

    win_t, wup_t = tr(w_in), tr(w_up)
    sh_win = [win_t[l].astype(BF)[None] for l in range(L)]
    sh_w3 = [jnp.stack([w_attn_out[l], w_conv_out[l], w_o[l]]).astype(BF) for l in range(L)]
    sh_w2 = [jnp.stack([wup_t[l], w_down[l]]).astype(BF) for l in range(L)]
    wint, w3, w2 = [None] * L, [None] * L, [None] * L
    wint0_g, cw_g = _remote_only(_Gather([sh_win[0], jnp.pad(conv_w, ((0, 0), (0, 5), (0, 0)))]), "gather_first")
    wint[0] = wint0_g.reshape(NP, D)
    cw = jnp.swapaxes(cw_g, 1, 2).reshape(L, 8, D)
    slopes = np.power(np.float32(2.0), -8.0 * np.arange(1, 17, dtype=np.float32) / 16).astype(np.float32)

    xf = x.reshape(T, D)
    saved = []
    cur = xf
    for l in range(L):
        sm = jnp.stack([sinks[l], jnp.asarray(slopes)])
        bg = b_gates[l].reshape(2, D)
        (gt, q, k, v, h), got = _inproj_fwd(cur, g_mix[l:l + 1], wint[l], l, tm_in,
                                            side=_Gather([sh_w3[0]]) if l == 0 else None)
        if l == 0:
            w3[0] = got[0].reshape(3, D, D)
        (att,), got = _attn_fwd(q, k, v, sm, l, S, side=_Gather([sh_w2[0], sh_win[1]]) if l == 0 else None)
        if l == 0:
            w2[0], wint[1] = got[0].reshape(2, F, D), got[1].reshape(NP, D)
        (x1, mg, co, ya, yc), got = _mixer_fwd(cur, gt, att, w3[l], bg, cw[l], conv_b[l:l + 1], l, S, tm,
                                               side=_Gather([sh_w3[1]]) if l == 0 else None)
        if l == 0:
            w3[1] = got[0].reshape(3, D, D)
        (x2, h2, a), got = _mlp_fwd(x1, g_mlp[l:l + 1], w2[l], l, tm, side=_Gather([sh_w2[1]]) if l == 0 else None)
        if l == 0:
            w2[1] = got[0].reshape(2, F, D)
        saved.append(dict(x=cur, gt=gt, q=q, k=k, v=v, h=h, att=att, x1=x1, mg=mg, co=co, ya=ya, yc=yc, h2=h2, a=a,
                          sm=sm, bg=bg))
        cur = x2

    dcur, acc_loss = _loss_head(cur, loss_target.reshape(T, D), g_final.reshape(1, D), tm_in)

    masters = {"w_in": (win_t, tr(m_w_in), tr(v_w_in)), "w_attn_out": (w_attn_out, m_w_attn_out, v_w_attn_out),
               "w_conv_out": (w_conv_out, m_w_conv_out, v_w_conv_out), "w_o": (w_o, m_w_o, v_w_o),
               "w_up": (wup_t, tr(m_w_up), tr(v_w_up)), "w_down": (w_down, m_w_down, v_w_down)}
    big = {name: None for name in masters}

    def adam(name, land, l):
        big[name] = _adam_sum(land, *masters[name], l, big[name], f"adam_{name}{l}")

    acc_in, acc_mix, acc_mlp, dsink = [None] * L, [None] * L, [None] * L, [None] * L
    pending = None
    for l in reversed(range(L)):
        sv = saved[l]
        side = _Exchange([blocks(pending[0])]) if pending else None
        (da, dx1, dx2b, acc_mlp[l]), got = _mlp_bwd(dcur, sv["x1"], sv["a"], g_mlp[l:l + 1], w2[l], l, tm, side=side)
        if pending:
            adam("w_in", got[0], pending[1])
        d_wdn = _wgrad(sv["a"], dx2b, F, 0, None, f"wgrad_down{l}", relu2=True)
        d_wup = _wgrad(da, sv["h2"], F, 0, None, f"wgrad_up{l}")
        (dgt, datt, dya, dyc, dx1b, acc_mix[l]), got = _mixer_bwd(
            dx1, sv["gt"], sv["ya"], sv["yc"], w3[l], sv["bg"], cw[l], conv_b[l:l + 1], l, S, tm,
            side=_Exchange([blocks(d_wdn)]))
        adam("w_down", got[0], l)
        d_wo = _wgrad(sv["mg"], dx1b, D, 0, None, f"wgrad_o{l}")
        d_wao = _wgrad(sv["att"], dya, D, 0, None, f"wgrad_attn_out{l}")
        d_wco = _wgrad(sv["co"], dyc, D, 0, None, f"wgrad_conv_out{l}")
        d_win = _wgrad(dgt, sv["h"], NP, NQ + 2 * NKV, None, f"wgrad_gates{l}")
        (dq, dk, dv, dsink[l]), got = _attn_bwd(
            sv["q"], sv["k"], sv["v"], sv["att"], datt, sv["sm"], l, S,
            side=_Exchange([blocks(d_wup), blocks(d_wo), blocks(d_wao), blocks(d_wco)]))
        for name, land in zip(["w_up", "w_o", "w_attn_out", "w_conv_out"], got):
            adam(name, land, l)
        d_win = _wgrad(dq, sv["h"], NP, 0, d_win, f"wgrad_q{l}")
        d_win = _wgrad(dk, sv["h"], NP, NQ, d_win, f"wgrad_k{l}")
        d_win = _wgrad(dv, sv["h"], NP, NQ + NKV, d_win, f"wgrad_v{l}")
        side = _Exchange([blocks(d_win)]) if l == 0 else None
        (dcur, acc_in[l]), got = _inproj_bwd(dgt, dq, dk, dv, sv["x"], dx1, g_mix[l:l + 1], wint[l], l, tm_in, side=side)
        if l == 0:
            adam("w_in", got[0], 0)
        else:
            pending = (d_win, l)

    dsinks = jnp.stack([dsink[l][:, :, :, 0].sum(axis=0).reshape(16) for l in range(L)])
    small = _pack_small(
        jnp.concatenate([acc_in[l][0:1] for l in range(L)]),
        jnp.stack([acc_mix[l][0:2].reshape(2 * D) for l in range(L)]),
        dsinks,
        jnp.concatenate([acc_mix[l][2:3] for l in range(L)]),
        jnp.concatenate([acc_mlp[l][0:1] for l in range(L)]),
        acc_loss[0],
        jnp.concatenate([acc_mix[l][3:6] for l in range(L)]),
        acc_loss[1:2])
    (small_land,) = _remote_only(_Exchange([], [small]), "exchange_small")

    def place(cw_shard):
        return lax.dynamic_update_slice(jnp.zeros((2 * 3, D), F32), cw_shard.reshape(2 * 3, D // NDEV), (0, me * (D // NDEV)))

    zero_row = jnp.zeros((1, D), F32)
    packs = [_pack_small(a, b, c_, d_, e, f, place(g_), zero_row) for a, b, c_, d_, e, f, g_ in
             [(g_mix, b_gates, sinks, conv_b, g_mlp, g_final, conv_w),
              (m_g_mix, m_b_gates, m_sinks, m_conv_b, m_g_mlp, m_g_final, m_conv_w),
              (v_g_mix, v_b_gates, v_sinks, v_conv_b, v_g_mlp, v_g_final, v_conv_w)]]
    small_out = _adam_small(small_land, *packs)

    def unpack(p):
        cwp = lax.dynamic_slice(p[12:18], (0, me * (D // NDEV)), (6, D // NDEV)).reshape(L, 3, D // NDEV)
        return dict(g_mix=p[0:2], b_gates=p[2:6].reshape(L, 2 * D), sinks=p[6, :32].reshape(L, 16), conv_b=p[7:9],
                    g_mlp=p[9:11], g_final=p[11], conv_w=cwp)

    small_kinds = [unpack(p) for p in small_out]
    for name in ("w_in", "w_up"):
        big[name] = [tr(o) for o in big[name]]
    order = ["g_mix", "w_in", "b_gates", "sinks", "w_attn_out", "conv_w", "conv_b", "w_conv_out", "w_o", "g_mlp",
             "w_up", "w_down", "g_final"]
    out = [small_out[0][18, 0], dcur.reshape(nseq, S, D)]
    for kind in range(4):
        for name in order:
            out.append(big[name][kind] if name in big else small_kinds[kind][name])
    return tuple(out)
```

```python
import numpy as np
import jax
import jax.numpy as jnp
from jax import lax
from jax.experimental import pallas as pl
from jax.experimental.pallas import tpu as pltpu

D = 1024
NG = 5 * D
NQ = 1024
NKV = 256
NP = NQ + 2 * NKV + NG
F = 4096
HD = 64
GQ = 4
WIN = 128
L = 2
NDEV = 8
EPS = 1e-6
NEG = -1e30
SMALL_ROWS = 24
LR, B1, B2, AEPS, WD, STEP = 0.001, 0.9, 0.999, 1e-08, 0.01, 10

BF = jnp.bfloat16
F32 = jnp.float32
MESH = pl.DeviceIdType.MESH
VMEM_LIMIT = 60 * 1024 * 1024
ANY = pl.BlockSpec(memory_space=pl.ANY)

NN = ((1,), (0,))
NT = ((1,), (1,))
TN = ((0,), (0,))


def _dot(a, b, dims):
    return lax.dot_general(a, b, (dims, ((), ())), preferred_element_type=F32)


def _resident(shape, imap):
    return pl.BlockSpec(shape, imap, pipeline_mode=pl.Buffered(1))


def _position():
    return lax.axis_index("x"), lax.axis_index("y"), lax.axis_index("c")


class _Gather:
    def __init__(self, shards):
        n = len(shards)
        self.inputs = list(shards)
        self.out_shape = [jax.ShapeDtypeStruct((s.shape[0], NDEV) + s.shape[1:], s.dtype) for s in shards]
        self.scratch = [pltpu.SemaphoreType.DMA((n, 7)), pltpu.SemaphoreType.DMA((n, 7)), pltpu.SemaphoreType.DMA((n,))]

    def _plan(self, src, dst, sems):
        send_sems, recv_sems, local_sems = sems
        n = len(src)
        x, y, c = _position()
        me, sibling = (x, y, c), (x, y, 1 - c)
        chips = [(1 - x, y), (x, 1 - y), (1 - x, 1 - y)]

        def rows(a, p):
            return dst[a].at[:, 4 * p[0] + 2 * p[1] + p[2]]

        def copy(a, k, block, to, from_src=False):
            return pltpu.make_async_remote_copy(
                src_ref=src[a] if from_src else rows(a, block), dst_ref=rows(a, block),
                send_sem=send_sems.at[a, k], recv_sem=recv_sems.at[a, k], device_id=to, device_id_type=MESH)

        mine = [pltpu.make_async_copy(src[a], rows(a, me), local_sems.at[a]) for a in range(n)]
        first = []
        for a in range(n):
            first.append(copy(a, 0, me, sibling, True))
            first += [copy(a, 1 + t, me, (*chip, c), True) for t, chip in enumerate(chips)]
        return n, c, me, sibling, chips, copy, mine, first

    def start(self, src, dst, sems):
        *_, mine, first = self._plan(src, dst, sems)
        for cp in mine + first:
            cp.start()

    def finish(self, src, dst, sems):
        n, c, me, sibling, chips, copy, mine, first = self._plan(src, dst, sems)
        passed = []
        for t, chip in enumerate(chips):
            for a in range(n):
                copy(a, 1 + t, (*chip, c), me).wait_recv()
                cp = copy(a, 4 + t, (*chip, c), sibling)
                cp.start()
                passed.append(cp)
        for a in range(n):
            copy(a, 0, sibling, me).wait_recv()
            for t, chip in enumerate(chips):
                copy(a, 4 + t, (*chip, 1 - c), me).wait_recv()
        for cp in first + passed:
            cp.wait_send()
        for cp in mine:
            cp.wait()


class _Exchange:
    def __init__(self, grads, everyone=()):
        self.inputs = list(grads) + list(everyone)
        self.n_blocked = len(grads)
        n = len(self.inputs)
        self.out_shape = [jax.ShapeDtypeStruct(g.shape, g.dtype) for g in grads]
        self.out_shape += [jax.ShapeDtypeStruct((NDEV,) + e.shape, e.dtype) for e in everyone]
        self.scratch = [pltpu.SemaphoreType.DMA((n, 7)), pltpu.SemaphoreType.DMA((n, 7)), pltpu.SemaphoreType.DMA((n,))]

    def _plan(self, src, land, sems):
        send_sems, recv_sems, local_sems = sems
        x, y, c = _position()
        me = 4 * x + 2 * y + c

        def parts(peer_idx):
            return [(s.at[peer_idx] if a < self.n_blocked else s, land[a].at[me]) for a, s in enumerate(src)]

        local = [pltpu.make_async_copy(s, d, local_sems.at[a]) for a, (s, d) in enumerate(parts(me))]
        sent = []
        for rel in range(1, NDEV):
            px = 1 - x if rel & 4 else x
            py = 1 - y if rel & 2 else y
            pc = 1 - c if rel & 1 else c
            for a, (s, d) in enumerate(parts(4 * px + 2 * py + pc)):
                sent.append(pltpu.make_async_remote_copy(
                    src_ref=s, dst_ref=d, send_sem=send_sems.at[a, rel - 1], recv_sem=recv_sems.at[a, rel - 1],
                    device_id=(px, py, pc), device_id_type=MESH))
        return local, sent

    def start(self, src, land, sems):
        local, sent = self._plan(src, land, sems)
        for cp in local + sent:
            cp.start()

    def finish(self, src, land, sems):
        local, sent = self._plan(src, land, sems)
        for cp in sent:
            cp.wait_recv()
        for cp in sent:
            cp.wait_send()
        for cp in local:
            cp.wait()


def _call(body, *, name, grid, in_specs, out_specs, out_shape, args, scratch_shapes=(), aliases=None, side=None):
    sem = ("arbitrary",) * len(grid)
    if side is None:
        outs = pl.pallas_call(
            body, name=name, grid=grid, in_specs=in_specs, out_specs=out_specs, out_shape=out_shape,
            scratch_shapes=list(scratch_shapes), input_output_aliases=aliases or {},
            compiler_params=pltpu.CompilerParams(dimension_semantics=sem, vmem_limit_bytes=VMEM_LIMIT))(*args)
        return outs, []
    ni, no, ns = len(in_specs), len(out_specs), len(scratch_shapes)
    si, so = len(side.inputs), len(side.out_shape)

    def hosted(*refs):
        ins, refs = refs[:ni], refs[ni:]
        sins, refs = refs[:si], refs[si:]
        outs, refs = refs[:no], refs[no:]
        souts, refs = refs[:so], refs[so:]
        scr, sscr = refs[:ns], refs[ns:]
        at_first = pl.program_id(0) == 0
        at_last = pl.program_id(0) == grid[0] - 1
        for d in range(1, len(grid)):
            at_first &= pl.program_id(d) == 0
            at_last &= pl.program_id(d) == grid[d] - 1

        @pl.when(at_first)
        def _():
            side.start(sins, souts, sscr)

        body(*ins, *outs, *scr)

        @pl.when(at_last)
        def _():
            side.finish(sins, souts, sscr)

    outs = pl.pallas_call(
        hosted, name=name, grid=grid, in_specs=list(in_specs) + [ANY] * si, out_specs=list(out_specs) + [ANY] * so,
        out_shape=list(out_shape) + side.out_shape, scratch_shapes=list(scratch_shapes) + side.scratch,
        input_output_aliases=aliases or {},
        compiler_params=pltpu.CompilerParams(dimension_semantics=sem, vmem_limit_bytes=VMEM_LIMIT, has_side_effects=True),
    )(*args, *side.inputs)
    return outs[:no], outs[no:]


def _remote_only(side, name):
    n = len(side.inputs)

    def body(*refs):
        src, dst, sems = refs[:n], refs[n:n + len(side.out_shape)], refs[n + len(side.out_shape):]
        side.start(src, dst, sems)
        side.finish(src, dst, sems)

    return pl.pallas_call(
        body, name=name, in_specs=[ANY] * n, out_specs=[ANY] * len(side.out_shape), out_shape=side.out_shape,
        scratch_shapes=side.scratch, compiler_params=pltpu.CompilerParams(has_side_effects=True))(*side.inputs)


def _rms(x, g):
    r = lax.rsqrt(jnp.mean(x * x, axis=-1, keepdims=True) + EPS)
    return x * r * g


def _rms_bwd(dy, x, g):
    r = lax.rsqrt(jnp.mean(x * x, axis=-1, keepdims=True) + EPS)
    xh = x * r
    dxh = dy * g
    dx = r * (dxh - xh * jnp.mean(dxh * xh, axis=-1, keepdims=True))
    return dx, jnp.sum(dy * xh, axis=0, keepdims=True)


def _shift_rows(y, k, edge_rows, down):
    n = y.shape[0]
    rid = lax.broadcasted_iota(jnp.int32, y.shape, 0)
    out = pltpu.roll(y, k if down else n - k, 0)
    for t, row in enumerate(edge_rows):
        out = jnp.where(rid == (t if down else n - k + t), row, out)
    return out


def _zero_at_first_step(acc_ref):
    first = pl.program_id(0) == 0

    @pl.when(first)
    def _():
        acc_ref[...] = jnp.zeros_like(acc_ref)


ROW1 = lambda: _resident((1, D), lambda i: (0, 0))
ACC = lambda: pl.BlockSpec((8, D), lambda i: (0, 0))


def _inproj_fwd(x, g, wint, l, tm, side=None):
    T = x.shape[0]

    def body(x_ref, g_ref, w_ref, gt_ref, q_ref, k_ref, v_ref, h_ref):
        h = _rms(x_ref[...], g_ref[...]).astype(BF)
        h_ref[...] = h
        q_ref[...] = _dot(h, w_ref[0:NQ, :], NT).astype(BF)
        k_ref[...] = _dot(h, w_ref[NQ:NQ + NKV, :], NT).astype(BF)
        v_ref[...] = _dot(h, w_ref[NQ + NKV:NQ + 2 * NKV, :], NT).astype(BF)
        for s in range(5):
            lo = NQ + 2 * NKV + s * D
            gt_ref[:, s * D:(s + 1) * D] = _dot(h, w_ref[lo:lo + D, :], NT).astype(BF)

    tok = lambda w: pl.BlockSpec((tm, w), lambda i: (i, 0))
    return _call(
        body, name=f"inproj_fwd{l}", grid=(T // tm,),
        in_specs=[tok(D), ROW1(), _resident((NP, D), lambda i: (0, 0))],
        out_specs=[tok(NG), tok(NQ), tok(NKV), tok(NKV), tok(D)],
        out_shape=[jax.ShapeDtypeStruct((T, w), BF) for w in (NG, NQ, NKV, NKV, D)],
        args=(x, g, wint), side=side)


def _head_cols(sm_ref, row, head0):
    gidx = lax.broadcasted_iota(jnp.int32, (GQ * WIN, 1), 0) // WIN
    col = jnp.zeros((GQ * WIN, 1), F32)
    for g in range(GQ):
        col = jnp.where(gidx == g, sm_ref[row, head0 + g], col)
    return col


def _band_biases(sm_ref, pj):
    r = lax.broadcasted_iota(jnp.int32, (GQ * WIN, 2 * WIN), 0) % WIN
    j = lax.broadcasted_iota(jnp.int32, (GQ * WIN, 2 * WIN), 1)
    dist = WIN + r - j
    inwin = (dist >= 0) & (dist < WIN)
    distf = dist.astype(F32)
    out = []
    for kvh in range(2):
        head0 = pj * 8 + kvh * GQ
        bias = jnp.where(inwin, -_head_cols(sm_ref, 1, head0) * distf, NEG)
        out.append((_head_cols(sm_ref, 0, head0), bias, jnp.where(j >= WIN, bias, NEG)))
    return out


def _stack_heads(t):
    return jnp.concatenate([t[:, g * HD:(g + 1) * HD] for g in range(GQ)], axis=0)


def _band_rows(ref, r0, p0, kvh):
    cols = slice(kvh * HD, (kvh + 1) * HD)
    return jnp.concatenate([ref[pl.ds(p0, WIN), cols], ref[pl.ds(r0, WIN), cols]], axis=0)


def _band_probs(q_ref, k_ref, r0, p0, kvh, bias, sink):
    qs = _stack_heads(q_ref[pl.ds(r0, WIN), kvh * GQ * HD:(kvh + 1) * GQ * HD] * jnp.asarray(HD ** -0.5, BF))
    kb = _band_rows(k_ref, r0, p0, kvh)
    s = _dot(qs, kb, NT) + bias
    m = jnp.maximum(jnp.max(s, axis=-1, keepdims=True), sink)
    p = jnp.exp(s - m)
    ps = jnp.exp(sink - m)
    inv = 1.0 / (jnp.sum(p, axis=-1, keepdims=True) + ps)
    return qs, kb, p, ps, inv


def _attn_fwd(q, k, v, sm, l, S, side=None):
    T = q.shape[0]
    nblk = S // WIN

    def body(sm_ref, q_ref, k_ref, v_ref, o_ref):
        biases = _band_biases(sm_ref, pl.program_id(1))

        def block(i, first):
            r0 = 0 if first else pl.multiple_of(i * WIN, WIN)
            p0 = 0 if first else pl.multiple_of(i * WIN - WIN, WIN)
            for kvh in range(2):
                sink, bias, bias0 = biases[kvh]
                _, _, p, _, inv = _band_probs(q_ref, k_ref, r0, p0, kvh, bias0 if first else bias, sink)
                o = _dot(p.astype(BF), _band_rows(v_ref, r0, p0, kvh), NN) * inv
                for g in range(GQ):
                    c0 = (kvh * GQ + g) * HD
                    o_ref[pl.ds(r0, WIN), c0:c0 + HD] = o[g * WIN:(g + 1) * WIN].astype(BF)

        block(0, True)

        def rest(i, c):
            block(i, False)
            return c

        lax.fori_loop(1, nblk, rest, 0)

    wide = lambda: pl.BlockSpec((S, 512), lambda s, p: (s, p))
    narrow = lambda: pl.BlockSpec((S, 128), lambda s, p: (s, p))
    return _call(
        body, name=f"attn_fwd{l}", grid=(T // S, 2),
        in_specs=[pl.BlockSpec(memory_space=pltpu.SMEM), wide(), narrow(), narrow()],
        out_specs=[wide()], out_shape=[jax.ShapeDtypeStruct((T, D), BF)],
        args=(sm, q, k, v), side=side)


def _conv_taps(y, hy1, hy2, cw_ref):
    y1 = _shift_rows(y, 1, [hy1], True)
    y2 = _shift_rows(y, 2, [hy2, hy1], True)
    z = cw_ref[0:1, :] * y2 + cw_ref[1:2, :] * y1 + cw_ref[2:3, :] * y
    return z, y1, y2


def _halo_products(cch_ref, cuh_ref, first):
    hy1 = cch_ref[15:16, :].astype(F32) * cuh_ref[15:16, :].astype(F32)
    hy2 = cch_ref[14:15, :].astype(F32) * cuh_ref[14:15, :].astype(F32)
    return jnp.where(first, 0.0, hy1), jnp.where(first, 0.0, hy2)


def _mixer_fwd(x, gt, att, w3, bg, cw, cbias, l, S, tm, side=None):
    T = x.shape[0]

    def body(x_ref, cb_ref, cc_ref, cu_ref, ga_ref, gc_ref, cch_ref, cuh_ref, att_ref, wao_ref, wco_ref, wo_ref,
             bg_ref, cw_ref, cbias_ref, x1_ref, mg_ref, co_ref, ya_ref, yc_ref):
        first = (pl.program_id(0) * tm) % S == 0
        y = cc_ref[...].astype(F32) * cu_ref[...].astype(F32)
        hy1, hy2 = _halo_products(cch_ref, cuh_ref, first)
        z, _, _ = _conv_taps(y, hy1, hy2, cw_ref)
        co = (cb_ref[...].astype(F32) * (z + cbias_ref[...])).astype(BF)
        co_ref[...] = co
        yc = _dot(co, wco_ref[...], NN)
        ya = _dot(att_ref[...], wao_ref[...], NN)
        ya_ref[...] = ya
        yc_ref[...] = yc
        sa = jax.nn.sigmoid(ga_ref[...].astype(F32) + bg_ref[0:1, :])
        sc = jax.nn.sigmoid(gc_ref[...].astype(F32) + bg_ref[1:2, :])
        mg = (sa * ya + sc * yc).astype(BF)
        mg_ref[...] = mg
        x1_ref[...] = x_ref[...] + _dot(mg, wo_ref[...], NN)

    tok = lambda: pl.BlockSpec((tm, D), lambda i: (i, 0))
    seg = lambda s: pl.BlockSpec((tm, D), lambda i: (i, s))
    halo = lambda s: pl.BlockSpec((16, D), lambda i: (jnp.maximum(i * (tm // 16) - 1, 0), s))
    wsp = lambda k: _resident((None, D, D), lambda i: (k, 0, 0))
    row = lambda n: _resident((n, D), lambda i: (0, 0))
    return _call(
        body, name=f"mixer_fwd{l}", grid=(T // tm,),
        in_specs=[tok(), seg(0), seg(1), seg(2), seg(3), seg(4), halo(1), halo(2), tok(), wsp(0), wsp(1), wsp(2),
                  row(2), row(8), row(1)],
        out_specs=[tok(), tok(), tok(), tok(), tok()],
        out_shape=[jax.ShapeDtypeStruct((T, D), dt) for dt in (F32, BF, BF, F32, F32)],
        args=(x, gt, gt, gt, gt, gt, gt, gt, att, w3, w3, w3, bg, cw, cbias), side=side)


def _mlp_fwd(x1, g, w2, l, tm, side=None):
    T = x1.shape[0]
    FC = 1024

    def body(x_ref, g_ref, wup_ref, wdn_ref, x2_ref, h_ref, a_ref):
        x = x_ref[...]
        h = _rms(x, g_ref[...]).astype(BF)
        h_ref[...] = h
        acc = x
        for c in range(F // FC):
            a = _dot(h, wup_ref[c * FC:(c + 1) * FC, :], NT)
            a_ref[:, c * FC:(c + 1) * FC] = a.astype(BF)
            u = jnp.maximum(a, 0.0)
            acc = acc + _dot((u * u).astype(BF), wdn_ref[c * FC:(c + 1) * FC, :], NN)
        x2_ref[...] = acc

    tok = lambda w: pl.BlockSpec((tm, w), lambda i: (i, 0))
    wsp = lambda k: _resident((None, F, D), lambda i: (k, 0, 0))
    return _call(
        body, name=f"mlp_fwd{l}", grid=(T // tm,),
        in_specs=[tok(D), ROW1(), wsp(0), wsp(1)],
        out_specs=[tok(D), tok(D), tok(F)],
        out_shape=[jax.ShapeDtypeStruct((T, D), F32), jax.ShapeDtypeStruct((T, D), BF),
                   jax.ShapeDtypeStruct((T, F), BF)],
        args=(x1, g, w2, w2), side=side)


def _loss_head(x, target, g, tm):
    T = x.shape[0]
    nt = T // tm

    def body(x_ref, t_ref, g_ref, dx_ref, acc_ref):
        _zero_at_first_step(acc_ref)
        x = x_ref[...]
        gv = g_ref[...]
        err = _rms(x, gv) - t_ref[...]
        dx, dg = _rms_bwd(err * (1.0 / D), x, gv)
        dx_ref[...] = dx
        acc_ref[0:1, :] += dg
        acc_ref[1:2, :] += jnp.sum(err * err, axis=0, keepdims=True)

        @pl.when(pl.program_id(0) == nt - 1)
        def _():
            acc_ref[1:2, :] = jnp.zeros((1, D), F32) + (0.5 / D) * jnp.sum(acc_ref[1:2, :])

    tok = lambda: pl.BlockSpec((tm, D), lambda i: (i, 0))
    (dx, acc), _ = _call(
        body, name="loss_head", grid=(nt,), in_specs=[tok(), tok(), ROW1()], out_specs=[tok(), ACC()],
        out_shape=[jax.ShapeDtypeStruct((T, D), F32), jax.ShapeDtypeStruct((8, D), F32)], args=(x, target, g))
    return dx, acc


def _mlp_bwd(dx2, x1, a, g, w2, l, tm, side=None):
    T = dx2.shape[0]
    FC = 1024

    def body(d_ref, x_ref, a_ref, g_ref, wup_ref, wdn_ref, da_ref, dx1_ref, db_ref, acc_ref):
        _zero_at_first_step(acc_ref)
        d = d_ref[...]
        db = d.astype(BF)
        db_ref[...] = db
        dh = jnp.zeros((tm, D), F32)
        for c in range(F // FC):
            du = _dot(db, wdn_ref[c * FC:(c + 1) * FC, :], NT)
            da = (du * (2.0 * jnp.maximum(a_ref[:, c * FC:(c + 1) * FC].astype(F32), 0.0))).astype(BF)
            da_ref[:, c * FC:(c + 1) * FC] = da
            dh = dh + _dot(da, wup_ref[c * FC:(c + 1) * FC, :], NN)
        dx, dg = _rms_bwd(dh, x_ref[...], g_ref[...])
        dx1_ref[...] = d + dx
        acc_ref[0:1, :] += dg

    tok = lambda w: pl.BlockSpec((tm, w), lambda i: (i, 0))
    wsp = lambda k: _resident((None, F, D), lambda i: (k, 0, 0))
    return _call(
        body, name=f"mlp_bwd{l}", grid=(T // tm,),
        in_specs=[tok(D), tok(D), tok(F), ROW1(), wsp(0), wsp(1)],
        out_specs=[tok(F), tok(D), tok(D), ACC()],
        out_shape=[jax.ShapeDtypeStruct((T, F), BF), jax.ShapeDtypeStruct((T, D), F32),
                   jax.ShapeDtypeStruct((T, D), BF), jax.ShapeDtypeStruct((8, D), F32)],
        args=(dx2, x1, a, g, w2, w2), side=side)


def _mixer_bwd(dx1, gt, ya, yc, w3, bg, cw, cbias, l, S, tm, side=None):
    T = dx1.shape[0]
    nt = T // tm

    def body(d_ref, cb_ref, cc_ref, cu_ref, ga_ref, gc_ref, cch_ref, cuh_ref, ya_ref, yc_ref, wao_ref, wco_ref, wo_ref,
             bg_ref, cw_ref, cbias_ref, dg_ref, datt_ref, dya_ref, dyc_ref, db_ref, acc_ref, carry_ref):
        ti = nt - 1 - pl.program_id(0)
        _zero_at_first_step(acc_ref)

        @pl.when(((ti + 1) * tm) % S == 0)
        def _():
            carry_ref[...] = jnp.zeros_like(carry_ref)

        db = d_ref[...].astype(BF)
        db_ref[...] = db
        dm = _dot(db, wo_ref[...], NT)
        sa = jax.nn.sigmoid(ga_ref[...].astype(F32) + bg_ref[0:1, :])
        sc = jax.nn.sigmoid(gc_ref[...].astype(F32) + bg_ref[1:2, :])
        dya = (dm * sa).astype(BF)
        dyc = (dm * sc).astype(BF)
        dya_ref[...] = dya
        dyc_ref[...] = dyc
        dga = dm * ya_ref[...] * sa * (1.0 - sa)
        dgc = dm * yc_ref[...] * sc * (1.0 - sc)
        dg_ref[:, 3 * D:4 * D] = dga.astype(BF)
        dg_ref[:, 4 * D:5 * D] = dgc.astype(BF)
        acc_ref[0:1, :] += jnp.sum(dga, axis=0, keepdims=True)
        acc_ref[1:2, :] += jnp.sum(dgc, axis=0, keepdims=True)
        datt_ref[...] = _dot(dya, wao_ref[...], NT).astype(BF)
        dco = _dot(dyc, wco_ref[...], NT)

        cc = cc_ref[...].astype(F32)
        cu = cu_ref[...].astype(F32)
        cb = cb_ref[...].astype(F32)
        y = cc * cu
        hy1, hy2 = _halo_products(cch_ref, cuh_ref, (ti * tm) % S == 0)
        z, y1, y2 = _conv_taps(y, hy1, hy2, cw_ref)
        dg_ref[:, 0:D] = (dco * (z + cbias_ref[...])).astype(BF)
        dz = dco * cb
        acc_ref[2:3, :] += jnp.sum(dz, axis=0, keepdims=True)
        acc_ref[3:4, :] += jnp.sum(dz * y2, axis=0, keepdims=True)
        acc_ref[4:5, :] += jnp.sum(dz * y1, axis=0, keepdims=True)
        acc_ref[5:6, :] += jnp.sum(dz * y, axis=0, keepdims=True)
        n1 = carry_ref[0:1, :]
        n2 = carry_ref[1:2, :]
        dzu1 = _shift_rows(dz, 1, [n1], False)
        dzu2 = _shift_rows(dz, 2, [n1, n2], False)
        dy = cw_ref[2:3, :] * dz + cw_ref[1:2, :] * dzu1 + cw_ref[0:1, :] * dzu2
        dg_ref[:, D:2 * D] = (dy * cu).astype(BF)
        dg_ref[:, 2 * D:3 * D] = (dy * cc).astype(BF)
        carry_ref[...] = dz[0:8, :]

    tok = lambda w=D: pl.BlockSpec((tm, w), lambda i: (nt - 1 - i, 0))
    seg = lambda s: pl.BlockSpec((tm, D), lambda i: (nt - 1 - i, s))
    halo = lambda s: pl.BlockSpec((16, D), lambda i: (jnp.maximum((nt - 1 - i) * (tm // 16) - 1, 0), s))
    wsp = lambda k: _resident((None, D, D), lambda i: (k, 0, 0))
    row = lambda n: _resident((n, D), lambda i: (0, 0))
    return _call(
        body, name=f"mixer_bwd{l}", grid=(nt,),
        in_specs=[tok(), seg(0), seg(1), seg(2), seg(3), seg(4), halo(1), halo(2), tok(), tok(), wsp(0), wsp(1), wsp(2),
                  row(2), row(8), row(1)],
        out_specs=[tok(NG), tok(), tok(), tok(), tok(), ACC()],
        out_shape=[jax.ShapeDtypeStruct((T, NG), BF)] + [jax.ShapeDtypeStruct((T, D), BF)] * 4
        + [jax.ShapeDtypeStruct((8, D), F32)],
        scratch_shapes=[pltpu.VMEM((8, D), F32)],
        args=(dx1, gt, gt, gt, gt, gt, gt, gt, ya, yc, w3, w3, w3, bg, cw, cbias), side=side)


def _attn_bwd(q, k, v, att, datt, sm, l, S, side=None):
    T = q.shape[0]
    nblk = S // WIN
    scale = HD ** -0.5

    def body(sm_ref, q_ref, k_ref, v_ref, o_ref, do_ref, dq_ref, dk_ref, dv_ref, ds_ref, dka_ref, dva_ref):
        biases = _band_biases(sm_ref, pl.program_id(1))
        dka_ref[...] = jnp.zeros_like(dka_ref)
        dva_ref[...] = jnp.zeros_like(dva_ref)

        def block(i, first, dsinks):
            r0 = 0 if first else pl.multiple_of(i * WIN, WIN)
            p0 = 0 if first else pl.multiple_of(i * WIN - WIN, WIN)
            out = []
            for kvh in range(2):
                sink, bias, bias0 = biases[kvh]
                hc = slice(kvh * HD, (kvh + 1) * HD)
                qc = slice(kvh * GQ * HD, (kvh + 1) * GQ * HD)
                qs, kb, p, ps, inv = _band_probs(q_ref, k_ref, r0, p0, kvh, bias0 if first else bias, sink)
                vb = _band_rows(v_ref, r0, p0, kvh)
                dos = _stack_heads(do_ref[pl.ds(r0, WIN), qc])
                os_ = _stack_heads(o_ref[pl.ds(r0, WIN), qc])
                delta = jnp.sum(dos.astype(F32) * os_.astype(F32), axis=-1, keepdims=True)
                pn = p * inv
                dsb = (pn * (_dot(dos, vb, NT) - delta)).astype(BF)
                dqs = _dot(dsb, kb, NN) * scale
                for g in range(GQ):
                    c0 = (kvh * GQ + g) * HD
                    dq_ref[pl.ds(r0, WIN), c0:c0 + HD] = dqs[g * WIN:(g + 1) * WIN].astype(BF)
                dkb = _dot(dsb, qs, TN)
                dvb = _dot(pn.astype(BF), dos, TN)
                if not first:
                    dka_ref[pl.ds(p0, WIN), hc] += dkb[0:WIN]
                    dva_ref[pl.ds(p0, WIN), hc] += dvb[0:WIN]
                dka_ref[pl.ds(r0, WIN), hc] += dkb[WIN:2 * WIN]
                dva_ref[pl.ds(r0, WIN), hc] += dvb[WIN:2 * WIN]
                out.append(dsinks[kvh] - ps * inv * delta)
            return tuple(out)

        zero = jnp.zeros((GQ * WIN, 1), F32)
        dsinks = lax.fori_loop(1, nblk, lambda i, c: block(i, False, c), block(0, True, (zero, zero)))
        for kvh in range(2):
            for g in range(GQ):
                tot = jnp.sum(dsinks[kvh][g * WIN:(g + 1) * WIN])
                ds_ref[kvh * GQ + g:kvh * GQ + g + 1, :] = jnp.zeros((1, 128), F32) + tot
        dk_ref[...] = dka_ref[...].astype(BF)
        dv_ref[...] = dva_ref[...].astype(BF)

    wide = lambda: pl.BlockSpec((S, 512), lambda s, p: (s, p))
    narrow = lambda: pl.BlockSpec((S, 128), lambda s, p: (s, p))
    return _call(
        body, name=f"attn_bwd{l}", grid=(T // S, 2),
        in_specs=[pl.BlockSpec(memory_space=pltpu.SMEM), wide(), narrow(), narrow(), wide(), wide()],
        out_specs=[wide(), narrow(), narrow(), pl.BlockSpec((None, None, 8, 128), lambda s, p: (s, p, 0, 0))],
        out_shape=[jax.ShapeDtypeStruct((T, NQ), BF), jax.ShapeDtypeStruct((T, NKV), BF),
                   jax.ShapeDtypeStruct((T, NKV), BF), jax.ShapeDtypeStruct((T // S, 2, 8, 128), F32)],
        scratch_shapes=[pltpu.VMEM((S, 128), F32), pltpu.VMEM((S, 128), F32)],
        args=(sm, q, k, v, att, datt), side=side)


def _inproj_bwd(dgt, dq, dk, dv, x, dres, g, wint, l, tm, side=None):
    T = x.shape[0]

    def body(dg_ref, dq_ref, dk_ref, dv_ref, x_ref, dr_ref, g_ref, w_ref, dx_ref, acc_ref):
        _zero_at_first_step(acc_ref)
        dh = _dot(dq_ref[...], w_ref[0:NQ, :], NN)
        dh = dh + _dot(dk_ref[...], w_ref[NQ:NQ + NKV, :], NN)
        dh = dh + _dot(dv_ref[...], w_ref[NQ + NKV:NQ + 2 * NKV, :], NN)
        dh = dh + _dot(dg_ref[...], w_ref[NQ + 2 * NKV:NP, :], NN)
        dx, dg = _rms_bwd(dh, x_ref[...], g_ref[...])
        dx_ref[...] = dr_ref[...] + dx
        acc_ref[0:1, :] += dg

    tok = lambda w: pl.BlockSpec((tm, w), lambda i: (i, 0))
    return _call(
        body, name=f"inproj_bwd{l}", grid=(T // tm,),
        in_specs=[tok(NG), tok(NQ), tok(NKV), tok(NKV), tok(D), tok(D), ROW1(), _resident((NP, D), lambda i: (0, 0))],
        out_specs=[tok(D), ACC()],
        out_shape=[jax.ShapeDtypeStruct((T, D), F32), jax.ShapeDtypeStruct((8, D), F32)],
        args=(dgt, dq, dk, dv, x, dres, g, wint), side=side)


def _wgrad(a, b, rows, row0, into, name, relu2=False):
    T, M = a.shape
    tmm = next(t for t in (1024, 512, 256) if M % t == 0 and row0 % t == 0)
    tk = min(2048, T)
    nk = T // tk
    blk0 = row0 // tmm

    def body(*refs):
        a_ref, b_ref = refs[0], refs[1]
        o_ref, acc_ref = refs[-2], refs[-1]
        kk = pl.program_id(1)

        @pl.when(kk == 0)
        def _():
            acc_ref[...] = jnp.zeros_like(acc_ref)

        av = a_ref[...]
        if relu2:
            t = jnp.maximum(av.astype(F32), 0.0)
            av = (t * t).astype(BF)
        acc_ref[...] += _dot(av, b_ref[...], TN)

        @pl.when(kk == nk - 1)
        def _():
            o_ref[...] = acc_ref[...].astype(BF)

    in_specs = [pl.BlockSpec((tk, tmm), lambda j, kk: (kk, j)), pl.BlockSpec((tk, D), lambda j, kk: (kk, 0))]
    args = [a, b]
    if into is not None:
        in_specs.append(ANY)
        args.append(into)
    (out,), _ = _call(
        body, name=name, grid=(M // tmm, nk), in_specs=in_specs,
        out_specs=[pl.BlockSpec((tmm, D), lambda j, kk: (blk0 + j, 0))],
        out_shape=[jax.ShapeDtypeStruct((rows, D), BF)], scratch_shapes=[pltpu.VMEM((tmm, D), F32)],
        aliases={2: 0} if into is not None else None, args=args)
    return out


def _adamw(w, g, m, v):
    m = B1 * m + (1.0 - B1) * g
    v = B2 * v + (1.0 - B2) * (g * g)
    m_hat = m / (1.0 - B1 ** STEP)
    v_hat = v / (1.0 - B2 ** STEP)
    return -LR * (m_hat / (jnp.sqrt(v_hat) + AEPS) + WD * w), m, v


def _adam_sum(land, w, m, v, l, into, name):
    _, r, _ = land.shape
    tr = 208 if r % 208 == 0 else (256 if r % 256 == 0 else r)

    def body(land_ref, w_ref, m_ref, v_ref, *rest):
        g_ref, d_ref, nm_ref, nv_ref = rest[-4:]
        g = land_ref[0].astype(F32)
        for s in range(1, NDEV):
            g = g + land_ref[s].astype(F32)
        g_ref[...] = g
        d_ref[...], nm_ref[...], nv_ref[...] = _adamw(w_ref[...], g, m_ref[...], v_ref[...])

    blk = lambda: pl.BlockSpec((None, tr, D), lambda j: (l, j, 0))
    in_specs = [pl.BlockSpec((NDEV, tr, D), lambda j: (0, j, 0)), blk(), blk(), blk()]
    args = [land, w, m, v]
    aliases = None
    if into is not None:
        in_specs += [ANY] * 4
        args += list(into)
        aliases = {4 + t: t for t in range(4)}
    outs, _ = _call(body, name=name, grid=(r // tr,), in_specs=in_specs, out_specs=[blk()] * 4,
                    out_shape=[jax.ShapeDtypeStruct(w.shape, F32)] * 4, aliases=aliases, args=args)
    return outs


def _adam_small(land, w, m, v):
    def body(land_ref, w_ref, m_ref, v_ref, g_ref, d_ref, nm_ref, nv_ref):
        g = land_ref[0]
        for s in range(1, NDEV):
            g = g + land_ref[s]
        g_ref[...] = g
        d_ref[...], nm_ref[...], nv_ref[...] = _adamw(w_ref[...], g, m_ref[...], v_ref[...])

    return pl.pallas_call(
        body, name="adam_small", out_shape=[jax.ShapeDtypeStruct(w.shape, F32)] * 4,
        compiler_params=pltpu.CompilerParams(vmem_limit_bytes=VMEM_LIMIT))(land, w, m, v)


def _pack_small(g_mix, b_gates, sinks, conv_b, g_mlp, g_final, conv_w_rows, extra):
    sink_row = jnp.zeros((1, D), F32).at[0, :2 * 16].set(sinks.reshape(-1))
    return jnp.concatenate([g_mix, b_gates.reshape(4, D), sink_row, conv_b, g_mlp, g_final.reshape(1, D),
                            conv_w_rows, extra, jnp.zeros((SMALL_ROWS - 19, D), F32)], axis=0)


def kernel(x, g_mix, w_in, b_gates, sinks, w_attn_out, conv_w, conv_b, w_conv_out, w_o, g_mlp, w_up, w_down, g_final, loss_target, m_g_mix, m_w_in, m_b_gates, m_sinks, m_w_attn_out, m_conv_w, m_conv_b, m_w_conv_out, m_w_o, m_g_mlp, m_w_up, m_w_down, m_g_final, v_g_mix, v_w_in, v_b_gates, v_sinks, v_w_attn_out, v_conv_w, v_conv_b, v_w_conv_out, v_w_o, v_g_mlp, v_w_up, v_w_down, v_g_final):
    nseq, S, _ = x.shape
    T = nseq * S
    tm_in = min(512, S)
    tm = min(256, S)
    xi, yi, ci = _position()
    me = 4 * xi + 2 * yi + ci
    tr = lambda t: jnp.swapaxes(t, 1, 2)
    blocks = lambda t: t.reshape(NDEV, t.shape[0] // NDEV, D)

    win_t, wup_t = tr(w_in), tr(w_up)
    sh_win = [win_t[l].astype(BF)[None] for l in range(L)]
    sh_w3 = [jnp.stack([w_attn_out[l], w_conv_out[l], w_o[l]]).astype(BF) for l in range(L)]
    sh_w2 = [jnp.stack([wup_t[l], w_down[l]]).astype(BF) for l in range(L)]
    wint, w3, w2 = [None] * L, [None] * L, [None] * L
    wint0_g, cw_g = _remote_only(_Gather([sh_win[0], jnp.pad(conv_w, ((0, 0), (0, 5), (0, 0)))]), "gather_first")
    wint[0] = wint0_g.reshape(NP, D)
    cw = jnp.swapaxes(cw_g, 1, 2).reshape(L, 8, D)
    slopes = np.power(np.float32(2.0), -8.0 * np.arange(1, 17, dtype=np.float32) / 16).astype(np.float32)

    xf = x.reshape(T, D)
    saved = []
    cur = xf
    for l in range(L):
        sm = jnp.stack([sinks[l], jnp.asarray(slopes)])
        bg = b_gates[l].reshape(2, D)
        (gt, q, k, v, h), got = _inproj_fwd(cur, g_mix[l:l + 1], wint[l], l, tm_in,
                                            side=_Gather([sh_w3[0]]) if l == 0 else None)
        if l == 0:
            w3[0] = got[0].reshape(3, D, D)
        (att,), got = _attn_fwd(q, k, v, sm, l, S, side=_Gather([sh_w2[0], sh_win[1]]) if l == 0 else None)
        if l == 0:
            w2[0], wint[1] = got[0].reshape(2, F, D), got[1].reshape(NP, D)
        (x1, mg, co, ya, yc), got = _mixer_fwd(cur, gt, att, w3[l], bg, cw[l], conv_b[l:l + 1], l, S, tm,
                                               side=_Gather([sh_w3[1]]) if l == 0 else None)
        if l == 0:
            w3[1] = got[0].reshape(3, D, D)
        (x2, h2, a), got = _mlp_fwd(x1, g_mlp[l:l + 1], w2[l], l, tm, side=_Gather([sh_w2[1]]) if l == 0 else None)
        if l == 0:
            w2[1] = got[0].reshape(2, F, D)
        saved.append(dict(x=cur, gt=gt, q=q, k=k, v=v, h=h, att=att, x1=x1, mg=mg, co=co, ya=ya, yc=yc, h2=h2, a=a,
                          sm=sm, bg=bg))
        cur = x2

    dcur, acc_loss = _loss_head(cur, loss_target.reshape(T, D), g_final.reshape(1, D), tm_in)

    masters = {"w_in": (win_t, tr(m_w_in), tr(v_w_in)), "w_attn_out": (w_attn_out, m_w_attn_out, v_w_attn_out),
               "w_conv_out": (w_conv_out, m_w_conv_out, v_w_conv_out), "w_o": (w_o, m_w_o, v_w_o),
               "w_up": (wup_t, tr(m_w_up), tr(v_w_up)), "w_down": (w_down, m_w_down, v_w_down)}
    big = {name: None for name in masters}

    def adam(name, land, l):
        big[name] = _adam_sum(land, *masters[name], l, big[name], f"adam_{name}{l}")

    acc_in, acc_mix, acc_mlp, dsink = [None] * L, [None] * L, [None] * L, [None] * L
    pending = None
    for l in reversed(range(L)):
        sv = saved[l]
        side = _Exchange([blocks(pending[0])]) if pending else None
        (da, dx1, dx2b, acc_mlp[l]), got = _mlp_bwd(dcur, sv["x1"], sv["a"], g_mlp[l:l + 1], w2[l], l, tm, side=side)
        if pending:
            adam("w_in", got[0], pending[1])
        d_wdn = _wgrad(sv["a"], dx2b, F, 0, None, f"wgrad_down{l}", relu2=True)
        d_wup = _wgrad(da, sv["h2"], F, 0, None, f"wgrad_up{l}")
        (dgt, datt, dya, dyc, dx1b, acc_mix[l]), got = _mixer_bwd(
            dx1, sv["gt"], sv["ya"], sv["yc"], w3[l], sv["bg"], cw[l], conv_b[l:l + 1], l, S, tm,
            side=_Exchange([blocks(d_wdn)]))
        adam("w_down", got[0], l)
        d_wo = _wgrad(sv["mg"], dx1b, D, 0, None, f"wgrad_o{l}")
        d_wao = _wgrad(sv["att"], dya, D, 0, None, f"wgrad_attn_out{l}")
        d_wco = _wgrad(sv["co"], dyc, D, 0, None, f"wgrad_conv_out{l}")
        d_win = _wgrad(dgt, sv["h"], NP, NQ + 2 * NKV, None, f"wgrad_gates{l}")
        (dq, dk, dv, dsink[l]), got = _attn_bwd(
            sv["q"], sv["k"], sv["v"], sv["att"], datt, sv["sm"], l, S,
            side=_Exchange([blocks(d_wup), blocks(d_wo), blocks(d_wao), blocks(d_wco)]))
        for name, land in zip(["w_up", "w_o", "w_attn_out", "w_conv_out"], got):
            adam(name, land, l)
        d_win = _wgrad(dq, sv["h"], NP, 0, d_win, f"wgrad_q{l}")
        d_win = _wgrad(dk, sv["h"], NP, NQ, d_win, f"wgrad_k{l}")
        d_win = _wgrad(dv, sv["h"], NP, NQ + NKV, d_win, f"wgrad_v{l}")
        side = _Exchange([blocks(d_win)]) if l == 0 else None
        (dcur, acc_in[l]), got = _inproj_bwd(dgt, dq, dk, dv, sv["x"], dx1, g_mix[l:l + 1], wint[l], l, tm_in, side=side)
        if l == 0:
            adam("w_in", got[0], 0)
        else:
            pending = (d_win, l)

    dsinks = jnp.stack([dsink[l][:, :, :, 0].sum(axis=0).reshape(16) for l in range(L)])
    small = _pack_small(
        jnp.concatenate([acc_in[l][0:1] for l in range(L)]),
        jnp.stack([acc_mix[l][0:2].reshape(2 * D) for l in range(L)]),
        dsinks,
        jnp.concatenate([acc_mix[l][2:3] for l in range(L)]),
        jnp.concatenate([acc_mlp[l][0:1] for l in range(L)]),
        acc_loss[0],
        jnp.concatenate([acc_mix[l][3:6] for l in range(L)]),
        acc_loss[1:2])
    (small_land,) = _remote_only(_Exchange([], [small]), "exchange_small")

    def place(cw_shard):
        return lax.dynamic_update_slice(jnp.zeros((2 * 3, D), F32), cw_shard.reshape(2 * 3, D // NDEV), (0, me * (D // NDEV)))

    zero_row = jnp.zeros((1, D), F32)
    packs = [_pack_small(a, b, c_, d_, e, f, place(g_), zero_row) for a, b, c_, d_, e, f, g_ in
             [(g_mix, b_gates, sinks, conv_b, g_mlp, g_final, conv_w),
              (m_g_mix, m_b_gates, m_sinks, m_conv_b, m_g_mlp, m_g_final, m_conv_w),
              (v_g_mix, v_b_gates, v_sinks, v_conv_b, v_g_mlp, v_g_final, v_conv_w)]]
    small_out = _adam_small(small_land, *packs)

    def unpack(p):
        cwp = lax.dynamic_slice(p[12:18], (0, me * (D // NDEV)), (6, D // NDEV)).reshape(L, 3, D // NDEV)
        return dict(g_mix=p[0:2], b_gates=p[2:6].reshape(L, 2 * D), sinks=p[6, :32].reshape(L, 16), conv_b=p[7:9],
                    g_mlp=p[9:11], g_final=p[11], conv_w=cwp)

    small_kinds = [unpack(p) for p in small_out]
    for name in ("w_in", "w_up"):
        big[name] = [tr(o) for o in big[name]]
    order = ["g_mix", "w_in", "b_gates", "sinks", "w_attn_out", "conv_w", "conv_b", "w_conv_out", "w_o", "g_mlp",
             "w_up", "w_down", "g_final"]
    out = [small_out[0][18, 0], dcur.reshape(nseq, S, D)]
    for kind in range(4):
        for name in order:
            out.append(big[name][kind] if name in big else small_kinds[kind][name])
    return tuple(out)
```

```python
import numpy as np
import jax
import jax.numpy as jnp
from jax import lax
from jax.experimental import pallas as pl
from jax.experimental.pallas import tpu as pltpu

D = 1024
NG = 5 * D
NQ = 1024
NKV = 256
NP = NQ + 2 * NKV + NG
F = 4096
HD = 64
GQ = 4
WIN = 128
L = 2
NDEV = 8
EPS = 1e-6
NEG = -1e30
SMALL_ROWS = 24
LR, B1, B2, AEPS, WD, STEP = 0.001, 0.9, 0.999, 1e-08, 0.01, 10

BF = jnp.bfloat16
F32 = jnp.float32
MESH = pl.DeviceIdType.MESH
VMEM_LIMIT = 60 * 1024 * 1024
ANY = pl.BlockSpec(memory_space=pl.ANY)

NN = ((1,), (0,))
NT = ((1,), (1,))
TN = ((0,), (0,))


def _dot(a, b, dims):
    return lax.dot_general(a, b, (dims, ((), ())), preferred_element_type=F32)


def _resident(shape, imap):
    return pl.BlockSpec(shape, imap, pipeline_mode=pl.Buffered(1))


def _position():
    return lax.axis_index("x"), lax.axis_index("y"), lax.axis_index("c")


class _Gather:
    def __init__(self, shards):
        n = len(shards)
        self.inputs = list(shards)
        self.out_shape = [jax.ShapeDtypeStruct((s.shape[0], NDEV) + s.shape[1:], s.dtype) for s in shards]
        self.scratch = [pltpu.SemaphoreType.DMA((n, 7)), pltpu.SemaphoreType.DMA((n, 7)), pltpu.SemaphoreType.DMA((n,))]

    def _plan(self, src, dst, sems):
        send_sems, recv_sems, local_sems = sems
        n = len(src)
        x, y, c = _position()
        me, sibling = (x, y, c), (x, y, 1 - c)
        chips = [(1 - x, y), (x, 1 - y), (1 - x, 1 - y)]

        def rows(a, p):
            return dst[a].at[:, 4 * p[0] + 2 * p[1] + p[2]]

        def copy(a, k, block, to, from_src=False):
            return pltpu.make_async_remote_copy(
                src_ref=src[a] if from_src else rows(a, block), dst_ref=rows(a, block),
                send_sem=send_sems.at[a, k], recv_sem=recv_sems.at[a, k], device_id=to, device_id_type=MESH)

        mine = [pltpu.make_async_copy(src[a], rows(a, me), local_sems.at[a]) for a in range(n)]
        first = []
        for a in range(n):
            first.append(copy(a, 0, me, sibling, True))
            first += [copy(a, 1 + t, me, (*chip, c), True) for t, chip in enumerate(chips)]
        return n, c, me, sibling, chips, copy, mine, first

    def start(self, src, dst, sems):
        *_, mine, first = self._plan(src, dst, sems)
        for cp in mine + first:
            cp.start()

    def finish(self, src, dst, sems):
        n, c, me, sibling, chips, copy, mine, first = self._plan(src, dst, sems)
        passed = []
        for t, chip in enumerate(chips):
            for a in range(n):
                copy(a, 1 + t, (*chip, c), me).wait_recv()
                cp = copy(a, 4 + t, (*chip, c), sibling)
                cp.start()
                passed.append(cp)
        for a in range(n):
            copy(a, 0, sibling, me).wait_recv()
            for t, chip in enumerate(chips):
                copy(a, 4 + t, (*chip, 1 - c), me).wait_recv()
        for cp in first + passed:
            cp.wait_send()
        for cp in mine:
            cp.wait()


class _Exchange:
    def __init__(self, grads, everyone=()):
        self.inputs = list(grads) + list(everyone)
        self.n_blocked = len(grads)
        n = len(self.inputs)
        self.out_shape = [jax.ShapeDtypeStruct(g.shape, g.dtype) for g in grads]
        self.out_shape += [jax.ShapeDtypeStruct((NDEV,) + e.shape, e.dtype) for e in everyone]
        self.scratch = [pltpu.SemaphoreType.DMA((n, 7)), pltpu.SemaphoreType.DMA((n, 7)), pltpu.SemaphoreType.DMA((n,))]

    def _plan(self, src, land, sems):
        send_sems, recv_sems, local_sems = sems
        x, y, c = _position()
        me = 4 * x + 2 * y + c

        def parts(peer_idx):
            return [(s.at[peer_idx] if a < self.n_blocked else s, land[a].at[me]) for a, s in enumerate(src)]

        local = [pltpu.make_async_copy(s, d, local_sems.at[a]) for a, (s, d) in enumerate(parts(me))]
        sent = []
        for rel in range(1, NDEV):
            px = 1 - x if rel & 4 else x
            py = 1 - y if rel & 2 else y
            pc = 1 - c if rel & 1 else c
            for a, (s, d) in enumerate(parts(4 * px + 2 * py + pc)):
                sent.append(pltpu.make_async_remote_copy(
                    src_ref=s, dst_ref=d, send_sem=send_sems.at[a, rel - 1], recv_sem=recv_sems.at[a, rel - 1],
                    device_id=(px, py, pc), device_id_type=MESH))
        return local, sent

    def start(self, src, land, sems):
        local, sent = self._plan(src, land, sems)
        for cp in local + sent:
            cp.start()

    def finish(self, src, land, sems):
        local, sent = self._plan(src, land, sems)
        for cp in sent:
            cp.wait_recv()
        for cp in sent:
            cp.wait_send()
        for cp in local:
            cp.wait()


def _call(body, *, name, grid, in_specs, out_specs, out_shape, args, scratch_shapes=(), aliases=None, side=None):
    sem = ("arbitrary",) * len(grid)
    if side is None:
        outs = pl.pallas_call(
            body, name=name, grid=grid, in_specs=in_specs, out_specs=out_specs, out_shape=out_shape,
            scratch_shapes=list(scratch_shapes), input_output_aliases=aliases or {},
            compiler_params=pltpu.CompilerParams(dimension_semantics=sem, vmem_limit_bytes=VMEM_LIMIT))(*args)
        return outs, []
    ni, no, ns = len(in_specs), len(out_specs), len(scratch_shapes)
    si, so = len(side.inputs), len(side.out_shape)

    def hosted(*refs):
        ins, refs = refs[:ni], refs[ni:]
        sins, refs = refs[:si], refs[si:]
        outs, refs = refs[:no], refs[no:]
        souts, refs = refs[:so], refs[so:]
        scr, sscr = refs[:ns], refs[ns:]
        at_first = pl.program_id(0) == 0
        at_last = pl.program_id(0) == grid[0] - 1
        for d in range(1, len(grid)):
            at_first &= pl.program_id(d) == 0
            at_last &= pl.program_id(d) == grid[d] - 1

        @pl.when(at_first)
        def _():
            side.start(sins, souts, sscr)

        body(*ins, *outs, *scr)

        @pl.when(at_last)
        def _():
            side.finish(sins, souts, sscr)

    outs = pl.pallas_call(
        hosted, name=name, grid=grid, in_specs=list(in_specs) + [ANY] * si, out_specs=list(out_specs) + [ANY] * so,
        out_shape=list(out_shape) + side.out_shape, scratch_shapes=list(scratch_shapes) + side.scratch,
        input_output_aliases=aliases or {},
        compiler_params=pltpu.CompilerParams(dimension_semantics=sem, vmem_limit_bytes=VMEM_LIMIT, has_side_effects=True),
    )(*args, *side.inputs)
    return outs[:no], outs[no:]


def _remote_only(side, name):
    n = len(side.inputs)

    def body(*refs):
        src, dst, sems = refs[:n], refs[n:n + len(side.out_shape)], refs[n + len(side.out_shape):]
        side.start(src, dst, sems)
        side.finish(src, dst, sems)

    return pl.pallas_call(
        body, name=name, in_specs=[ANY] * n, out_specs=[ANY] * len(side.out_shape), out_shape=side.out_shape,
        scratch_shapes=side.scratch, compiler_params=pltpu.CompilerParams(has_side_effects=True))(*side.inputs)


def _rms(x, g):
    r = lax.rsqrt(jnp.mean(x * x, axis=-1, keepdims=True) + EPS)
    return x * r * g


def _rms_bwd(dy, x, g):
    r = lax.rsqrt(jnp.mean(x * x, axis=-1, keepdims=True) + EPS)
    xh = x * r
    dxh = dy * g
    dx = r * (dxh - xh * jnp.mean(dxh * xh, axis=-1, keepdims=True))
    return dx, jnp.sum(dy * xh, axis=0, keepdims=True)


def _shift_rows(y, k, edge_rows, down):
    n = y.shape[0]
    rid = lax.broadcasted_iota(jnp.int32, y.shape, 0)
    out = pltpu.roll(y, k if down else n - k, 0)
    for t, row in enumerate(edge_rows):
        out = jnp.where(rid == (t if down else n - k + t), row, out)
    return out


def _zero_at_first_step(acc_ref):
    first = pl.program_id(0) == 0

    @pl.when(first)
    def _():
        acc_ref[...] = jnp.zeros_like(acc_ref)


ROW1 = lambda: _resident((1, D), lambda i: (0, 0))
ACC = lambda: pl.BlockSpec((8, D), lambda i: (0, 0))


def _inproj_fwd(x, g, wint, l, tm, side=None):
    T = x.shape[0]

    def body(x_ref, g_ref, w_ref, gt_ref, q_ref, k_ref, v_ref, h_ref):
        h = _rms(x_ref[...], g_ref[...]).astype(BF)
        h_ref[...] = h
        q_ref[...] = _dot(w_ref[0:NQ, :], h, NT).astype(BF)
        k_ref[...] = _dot(w_ref[NQ:NQ + NKV, :], h, NT).astype(BF)
        v_ref[...] = _dot(w_ref[NQ + NKV:NQ + 2 * NKV, :], h, NT).astype(BF)
        for s in range(5):
            lo = NQ + 2 * NKV + s * D
            gt_ref[:, s * D:(s + 1) * D] = _dot(h, w_ref[lo:lo + D, :], NT).astype(BF)

    tok = lambda w: pl.BlockSpec((tm, w), lambda i: (i, 0))
    feat = lambda w: pl.BlockSpec((w, tm), lambda i: (0, i))
    return _call(
        body, name=f"inproj_fwd{l}", grid=(T // tm,),
        in_specs=[tok(D), ROW1(), _resident((NP, D), lambda i: (0, 0))],
        out_specs=[tok(NG), feat(NQ), feat(NKV), feat(NKV), tok(D)],
        out_shape=[jax.ShapeDtypeStruct((T, NG), BF)] + [jax.ShapeDtypeStruct((w, T), BF) for w in (NQ, NKV, NKV)]
        + [jax.ShapeDtypeStruct((T, D), BF)],
        args=(x, g, wint), side=side)


def _band_geometry():
    j = lax.broadcasted_iota(jnp.int32, (2 * WIN, WIN), 0)
    r = lax.broadcasted_iota(jnp.int32, (2 * WIN, WIN), 1)
    dist = WIN + r - j
    dist0 = r - j
    return (dist.astype(F32), (dist >= 0) & (dist < WIN)), (dist0.astype(F32), dist0 >= 0)


def _pair_biases(sm_ref, pj):
    return [[jnp.where(ok, -sm_ref[1, pj * 2 * GQ + h] * dist, NEG) for h in range(2 * GQ)]
            for dist, ok in _band_geometry()]


def _heads_on_lanes(ref, kvh, r0):
    return jnp.concatenate([ref[(kvh * GQ + g) * HD:(kvh * GQ + g + 1) * HD, pl.ds(r0, WIN)] for g in range(GQ)], axis=1)


def _band_probs(sm_ref, head0, q_ref, k_ref, r0, p0, kvh, biases):
    qt = _heads_on_lanes(q_ref, kvh, r0) * jnp.asarray(HD ** -0.5, BF)
    kt = k_ref[kvh * HD:(kvh + 1) * HD, pl.ds(p0, 2 * WIN)]
    st = _dot(kt, qt, TN)
    heads = []
    for g in range(GQ):
        sink = sm_ref[0, head0 + g]
        s = st[:, g * WIN:(g + 1) * WIN] + biases[kvh * GQ + g]
        m = jnp.maximum(jnp.max(s, axis=0, keepdims=True), sink)
        p = jnp.exp(s - m)
        ps = jnp.exp(sink - m)
        heads.append((p, ps, 1.0 / (jnp.sum(p, axis=0, keepdims=True) + ps)))
    return qt, kt, heads


def _attn_fwd(q, k, v, sm, l, S, side=None):
    T = q.shape[1]
    nblk = S // WIN

    def body(sm_ref, q_ref, k_ref, v_ref, o_ref):
        pj = pl.program_id(1)
        biases = _pair_biases(sm_ref, pj)

        def block(i, first):
            r0 = 0 if first else pl.multiple_of(i * WIN, WIN)
            p0 = 0 if first else pl.multiple_of(i * WIN - WIN, WIN)
            for kvh in range(2):
                head0 = (2 * pj + kvh) * GQ
                _, _, heads = _band_probs(sm_ref, head0, q_ref, k_ref, r0, p0, kvh, biases[first])
                pt = jnp.concatenate([p.astype(BF) for p, _, _ in heads], axis=1)
                ot = _dot(v_ref[kvh * HD:(kvh + 1) * HD, pl.ds(p0, 2 * WIN)], pt, NN)
                for g in range(GQ):
                    rows = slice((kvh * GQ + g) * HD, (kvh * GQ + g + 1) * HD)
                    o_ref[rows, pl.ds(r0, WIN)] = (ot[:, g * WIN:(g + 1) * WIN] * heads[g][2]).astype(BF)

        block(0, True)

        def rest(i, c):
            block(i, False)
            return c

        lax.fori_loop(1, nblk, rest, 0)

    wide = lambda: pl.BlockSpec((2 * GQ * HD, S), lambda s, p: (p, s))
    narrow = lambda: pl.BlockSpec((2 * HD, S), lambda s, p: (p, s))
    return _call(
        body, name=f"attn_fwd{l}", grid=(T // S, 2),
        in_specs=[pl.BlockSpec(memory_space=pltpu.SMEM), wide(), narrow(), narrow()],
        out_specs=[wide()], out_shape=[jax.ShapeDtypeStruct((D, T), BF)],
        args=(sm, q, k, v), side=side)


def _conv_taps(y, hy1, hy2, cw_ref):
    y1 = _shift_rows(y, 1, [hy1], True)
    y2 = _shift_rows(y, 2, [hy2, hy1], True)
    z = cw_ref[0:1, :] * y2 + cw_ref[1:2, :] * y1 + cw_ref[2:3, :] * y
    return z, y1, y2


def _halo_products(cch_ref, cuh_ref, first):
    hy1 = cch_ref[15:16, :].astype(F32) * cuh_ref[15:16, :].astype(F32)
    hy2 = cch_ref[14:15, :].astype(F32) * cuh_ref[14:15, :].astype(F32)
    return jnp.where(first, 0.0, hy1), jnp.where(first, 0.0, hy2)


def _mixer_fwd(x, gt, att, w3, bg, cw, cbias, l, S, tm, side=None):
    T = x.shape[0]

    def body(x_ref, cb_ref, cc_ref, cu_ref, ga_ref, gc_ref, cch_ref, cuh_ref, att_ref, wao_ref, wco_ref, wo_ref,
             bg_ref, cw_ref, cbias_ref, x1_ref, mg_ref, co_ref, ya_ref, yc_ref):
        first = (pl.program_id(0) * tm) % S == 0
        y = cc_ref[...].astype(F32) * cu_ref[...].astype(F32)
        hy1, hy2 = _halo_products(cch_ref, cuh_ref, first)
        z, _, _ = _conv_taps(y, hy1, hy2, cw_ref)
        co = (cb_ref[...].astype(F32) * (z + cbias_ref[...])).astype(BF)
        co_ref[...] = co
        yc = _dot(co, wco_ref[...], NN)
        ya = _dot(att_ref[...], wao_ref[...], TN)
        ya_ref[...] = ya
        yc_ref[...] = yc
        sa = jax.nn.sigmoid(ga_ref[...].astype(F32) + bg_ref[0:1, :])
        sc = jax.nn.sigmoid(gc_ref[...].astype(F32) + bg_ref[1:2, :])
        mg = (sa * ya + sc * yc).astype(BF)
        mg_ref[...] = mg
        x1_ref[...] = x_ref[...] + _dot(mg, wo_ref[...], NN)

    tok = lambda: pl.BlockSpec((tm, D), lambda i: (i, 0))
    seg = lambda s: pl.BlockSpec((tm, D), lambda i: (i, s))
    halo = lambda s: pl.BlockSpec((16, D), lambda i: (jnp.maximum(i * (tm // 16) - 1, 0), s))
    wsp = lambda k: _resident((None, D, D), lambda i: (k, 0, 0))
    row = lambda n: _resident((n, D), lambda i: (0, 0))
    return _call(
        body, name=f"mixer_fwd{l}", grid=(T // tm,),
        in_specs=[tok(), seg(0), seg(1), seg(2), seg(3), seg(4), halo(1), halo(2),
                  pl.BlockSpec((D, tm), lambda i: (0, i)), wsp(0), wsp(1), wsp(2), row(2), row(8), row(1)],
        out_specs=[tok(), tok(), tok(), tok(), tok()],
        out_shape=[jax.ShapeDtypeStruct((T, D), dt) for dt in (F32, BF, BF, F32, F32)],
        args=(x, gt, gt, gt, gt, gt, gt, gt, att, w3, w3, w3, bg, cw, cbias), side=side)


def _mlp_fwd(x1, g, w2, l, tm, side=None):
    T = x1.shape[0]
    FC = 1024

    def body(x_ref, g_ref, wup_ref, wdn_ref, x2_ref, h_ref, a_ref):
        x = x_ref[...]
        h = _rms(x, g_ref[...]).astype(BF)
        h_ref[...] = h
        acc = x
        for c in range(F // FC):
            a = _dot(h, wup_ref[c * FC:(c + 1) * FC, :], NT)
            a_ref[:, c * FC:(c + 1) * FC] = a.astype(BF)
            u = jnp.maximum(a, 0.0)
            acc = acc + _dot((u * u).astype(BF), wdn_ref[c * FC:(c + 1) * FC, :], NN)
        x2_ref[...] = acc

    tok = lambda w: pl.BlockSpec((tm, w), lambda i: (i, 0))
    wsp = lambda k: _resident((None, F, D), lambda i: (k, 0, 0))
    return _call(
        body, name=f"mlp_fwd{l}", grid=(T // tm,),
        in_specs=[tok(D), ROW1(), wsp(0), wsp(1)],
        out_specs=[tok(D), tok(D), tok(F)],
        out_shape=[jax.ShapeDtypeStruct((T, D), F32), jax.ShapeDtypeStruct((T, D), BF),
                   jax.ShapeDtypeStruct((T, F), BF)],
        args=(x1, g, w2, w2), side=side)


def _loss_head(x, target, g, tm):
    T = x.shape[0]
    nt = T // tm

    def body(x_ref, t_ref, g_ref, dx_ref, acc_ref):
        _zero_at_first_step(acc_ref)
        x = x_ref[...]
        gv = g_ref[...]
        err = _rms(x, gv) - t_ref[...]
        dx, dg = _rms_bwd(err * (1.0 / D), x, gv)
        dx_ref[...] = dx
        acc_ref[0:1, :] += dg
        acc_ref[1:2, :] += jnp.sum(err * err, axis=0, keepdims=True)

        @pl.when(pl.program_id(0) == nt - 1)
        def _():
            acc_ref[1:2, :] = jnp.zeros((1, D), F32) + (0.5 / D) * jnp.sum(acc_ref[1:2, :])

    tok = lambda: pl.BlockSpec((tm, D), lambda i: (i, 0))
    (dx, acc), _ = _call(
        body, name="loss_head", grid=(nt,), in_specs=[tok(), tok(), ROW1()], out_specs=[tok(), ACC()],
        out_shape=[jax.ShapeDtypeStruct((T, D), F32), jax.ShapeDtypeStruct((8, D), F32)], args=(x, target, g))
    return dx, acc


def _mlp_bwd(dx2, x1, a, g, w2, l, tm, side=None):
    T = dx2.shape[0]
    FC = 1024

    def body(d_ref, x_ref, a_ref, g_ref, wup_ref, wdn_ref, da_ref, dx1_ref, db_ref, acc_ref):
        _zero_at_first_step(acc_ref)
        d = d_ref[...]
        db = d.astype(BF)
        db_ref[...] = db
        dh = jnp.zeros((tm, D), F32)
        for c in range(F // FC):
            du = _dot(db, wdn_ref[c * FC:(c + 1) * FC, :], NT)
            da = (du * (2.0 * jnp.maximum(a_ref[:, c * FC:(c + 1) * FC].astype(F32), 0.0))).astype(BF)
            da_ref[:, c * FC:(c + 1) * FC] = da
            dh = dh + _dot(da, wup_ref[c * FC:(c + 1) * FC, :], NN)
        dx, dg = _rms_bwd(dh, x_ref[...], g_ref[...])
        dx1_ref[...] = d + dx
        acc_ref[0:1, :] += dg

    tok = lambda w: pl.BlockSpec((tm, w), lambda i: (i, 0))
    wsp = lambda k: _resident((None, F, D), lambda i: (k, 0, 0))
    return _call(
        body, name=f"mlp_bwd{l}", grid=(T // tm,),
        in_specs=[tok(D), tok(D), tok(F), ROW1(), wsp(0), wsp(1)],
        out_specs=[tok(F), tok(D), tok(D), ACC()],
        out_shape=[jax.ShapeDtypeStruct((T, F), BF), jax.ShapeDtypeStruct((T, D), F32),
                   jax.ShapeDtypeStruct((T, D), BF), jax.ShapeDtypeStruct((8, D), F32)],
        args=(dx2, x1, a, g, w2, w2), side=side)


def _mixer_bwd(dx1, gt, ya, yc, w3, bg, cw, cbias, l, S, tm, side=None):
    T = dx1.shape[0]
    nt = T // tm

    def body(d_ref, cb_ref, cc_ref, cu_ref, ga_ref, gc_ref, cch_ref, cuh_ref, ya_ref, yc_ref, wao_ref, wco_ref, wo_ref,
             bg_ref, cw_ref, cbias_ref, dg_ref, datt_ref, dya_ref, dyc_ref, db_ref, acc_ref, carry_ref):
        ti = nt - 1 - pl.program_id(0)
        _zero_at_first_step(acc_ref)

        @pl.when(((ti + 1) * tm) % S == 0)
        def _():
            carry_ref[...] = jnp.zeros_like(carry_ref)

        db = d_ref[...].astype(BF)
        db_ref[...] = db
        dm = _dot(db, wo_ref[...], NT)
        sa = jax.nn.sigmoid(ga_ref[...].astype(F32) + bg_ref[0:1, :])
        sc = jax.nn.sigmoid(gc_ref[...].astype(F32) + bg_ref[1:2, :])
        dya = (dm * sa).astype(BF)
        dyc = (dm * sc).astype(BF)
        dya_ref[...] = dya
        dyc_ref[...] = dyc
        dga = dm * ya_ref[...] * sa * (1.0 - sa)
        dgc = dm * yc_ref[...] * sc * (1.0 - sc)
        dg_ref[:, 3 * D:4 * D] = dga.astype(BF)
        dg_ref[:, 4 * D:5 * D] = dgc.astype(BF)
        acc_ref[0:1, :] += jnp.sum(dga, axis=0, keepdims=True)
        acc_ref[1:2, :] += jnp.sum(dgc, axis=0, keepdims=True)
        datt_ref[...] = _dot(wao_ref[...], dya, NT).astype(BF)
        dco = _dot(dyc, wco_ref[...], NT)

        cc = cc_ref[...].astype(F32)
        cu = cu_ref[...].astype(F32)
        cb = cb_ref[...].astype(F32)
        y = cc * cu
        hy1, hy2 = _halo_products(cch_ref, cuh_ref, (ti * tm) % S == 0)
        z, y1, y2 = _conv_taps(y, hy1, hy2, cw_ref)
        dg_ref[:, 0:D] = (dco * (z + cbias_ref[...])).astype(BF)
        dz = dco * cb
        acc_ref[2:3, :] += jnp.sum(dz, axis=0, keepdims=True)
        acc_ref[3:4, :] += jnp.sum(dz * y2, axis=0, keepdims=True)
        acc_ref[4:5, :] += jnp.sum(dz * y1, axis=0, keepdims=True)
        acc_ref[5:6, :] += jnp.sum(dz * y, axis=0, keepdims=True)
        n1 = carry_ref[0:1, :]
        n2 = carry_ref[1:2, :]
        dzu1 = _shift_rows(dz, 1, [n1], False)
        dzu2 = _shift_rows(dz, 2, [n1, n2], False)
        dy = cw_ref[2:3, :] * dz + cw_ref[1:2, :] * dzu1 + cw_ref[0:1, :] * dzu2
        dg_ref[:, D:2 * D] = (dy * cu).astype(BF)
        dg_ref[:, 2 * D:3 * D] = (dy * cc).astype(BF)
        carry_ref[...] = dz[0:8, :]

    tok = lambda w=D: pl.BlockSpec((tm, w), lambda i: (nt - 1 - i, 0))
    seg = lambda s: pl.BlockSpec((tm, D), lambda i: (nt - 1 - i, s))
    halo = lambda s: pl.BlockSpec((16, D), lambda i: (jnp.maximum((nt - 1 - i) * (tm // 16) - 1, 0), s))
    wsp = lambda k: _resident((None, D, D), lambda i: (k, 0, 0))
    row = lambda n: _resident((n, D), lambda i: (0, 0))
    return _call(
        body, name=f"mixer_bwd{l}", grid=(nt,),
        in_specs=[tok(), seg(0), seg(1), seg(2), seg(3), seg(4), halo(1), halo(2), tok(), tok(), wsp(0), wsp(1), wsp(2),
                  row(2), row(8), row(1)],
        out_specs=[tok(NG), pl.BlockSpec((D, tm), lambda i: (0, nt - 1 - i)), tok(), tok(), tok(), ACC()],
        out_shape=[jax.ShapeDtypeStruct((T, NG), BF), jax.ShapeDtypeStruct((D, T), BF)]
        + [jax.ShapeDtypeStruct((T, D), BF)] * 3 + [jax.ShapeDtypeStruct((8, D), F32)],
        scratch_shapes=[pltpu.VMEM((8, D), F32)],
        args=(dx1, gt, gt, gt, gt, gt, gt, gt, ya, yc, w3, w3, w3, bg, cw, cbias), side=side)


def _attn_bwd(q, k, v, att, datt, sm, l, S, side=None):
    T = q.shape[1]
    nblk = S // WIN
    scale = HD ** -0.5

    def body(sm_ref, q_ref, k_ref, v_ref, o_ref, do_ref, dq_ref, dk_ref, dv_ref, ds_ref, dka_ref, dva_ref):
        pj = pl.program_id(1)
        biases = _pair_biases(sm_ref, pj)
        dka_ref[...] = jnp.zeros_like(dka_ref)
        dva_ref[...] = jnp.zeros_like(dva_ref)

        def block(i, first, dsinks):
            r0 = 0 if first else pl.multiple_of(i * WIN, WIN)
            p0 = 0 if first else pl.multiple_of(i * WIN - WIN, WIN)
            out = []
            for kvh in range(2):
                head0 = (2 * pj + kvh) * GQ
                rows = slice(kvh * HD, (kvh + 1) * HD)
                qt, kt, heads = _band_probs(sm_ref, head0, q_ref, k_ref, r0, p0, kvh, biases[first])
                inv = jnp.concatenate([h[2] for h in heads], axis=1)
                dos = _heads_on_lanes(do_ref, kvh, r0).astype(F32) * inv
                delta = jnp.sum(dos * _heads_on_lanes(o_ref, kvh, r0).astype(F32), axis=0, keepdims=True)
                dosb = dos.astype(BF)
                dpt = _dot(v_ref[rows, pl.ds(p0, 2 * WIN)], dosb, TN)
                dst = jnp.concatenate(
                    [(p * (dpt[:, g * WIN:(g + 1) * WIN] - delta[:, g * WIN:(g + 1) * WIN])).astype(BF)
                     for g, (p, _, _) in enumerate(heads)], axis=1)
                pt = jnp.concatenate([p.astype(BF) for p, _, _ in heads], axis=1)
                dqt = _dot(kt, dst, NN) * scale
                for g in range(GQ):
                    hr = slice((kvh * GQ + g) * HD, (kvh * GQ + g + 1) * HD)
                    dq_ref[hr, pl.ds(r0, WIN)] = dqt[:, g * WIN:(g + 1) * WIN].astype(BF)
                dka_ref[rows, pl.ds(p0, 2 * WIN)] += _dot(qt, dst, NT)
                dva_ref[rows, pl.ds(p0, 2 * WIN)] += _dot(dosb, pt, NT)
                ps = jnp.concatenate([h[1] for h in heads], axis=1)
                out.append(dsinks[kvh] - ps * delta)
            return tuple(out)

        zero = jnp.zeros((1, GQ * WIN), F32)
        dsinks = lax.fori_loop(1, nblk, lambda i, c: block(i, False, c), block(0, True, (zero, zero)))
        for kvh in range(2):
            for g in range(GQ):
                tot = jnp.sum(dsinks[kvh][:, g * WIN:(g + 1) * WIN])
                ds_ref[kvh * GQ + g:kvh * GQ + g + 1, :] = jnp.zeros((1, 128), F32) + tot
        dk_ref[...] = dka_ref[...].astype(BF)
        dv_ref[...] = dva_ref[...].astype(BF)

    wide = lambda: pl.BlockSpec((2 * GQ * HD, S), lambda s, p: (p, s))
    narrow = lambda: pl.BlockSpec((2 * HD, S), lambda s, p: (p, s))
    return _call(
        body, name=f"attn_bwd{l}", grid=(T // S, 2),
        in_specs=[pl.BlockSpec(memory_space=pltpu.SMEM), wide(), narrow(), narrow(), wide(), wide()],
        out_specs=[wide(), narrow(), narrow(), pl.BlockSpec((None, None, 8, 128), lambda s, p: (s, p, 0, 0))],
        out_shape=[jax.ShapeDtypeStruct((NQ, T), BF), jax.ShapeDtypeStruct((NKV, T), BF),
                   jax.ShapeDtypeStruct((NKV, T), BF), jax.ShapeDtypeStruct((T // S, 2, 8, 128), F32)],
        scratch_shapes=[pltpu.VMEM((2 * HD, S), F32), pltpu.VMEM((2 * HD, S), F32)],
        args=(sm, q, k, v, att, datt), side=side)


def _inproj_bwd(dgt, dq, dk, dv, x, dres, g, wint, l, tm, side=None):
    T = x.shape[0]

    def body(dg_ref, dq_ref, dk_ref, dv_ref, x_ref, dr_ref, g_ref, w_ref, dx_ref, acc_ref):
        _zero_at_first_step(acc_ref)
        dh = _dot(dq_ref[...], w_ref[0:NQ, :], TN)
        dh = dh + _dot(dk_ref[...], w_ref[NQ:NQ + NKV, :], TN)
        dh = dh + _dot(dv_ref[...], w_ref[NQ + NKV:NQ + 2 * NKV, :], TN)
        dh = dh + _dot(dg_ref[...], w_ref[NQ + 2 * NKV:NP, :], NN)
        dx, dg = _rms_bwd(dh, x_ref[...], g_ref[...])
        dx_ref[...] = dr_ref[...] + dx
        acc_ref[0:1, :] += dg

    tok = lambda w: pl.BlockSpec((tm, w), lambda i: (i, 0))
    feat = lambda w: pl.BlockSpec((w, tm), lambda i: (0, i))
    return _call(
        body, name=f"inproj_bwd{l}", grid=(T // tm,),
        in_specs=[tok(NG), feat(NQ), feat(NKV), feat(NKV), tok(D), tok(D), ROW1(), _resident((NP, D), lambda i: (0, 0))],
        out_specs=[tok(D), ACC()],
        out_shape=[jax.ShapeDtypeStruct((T, D), F32), jax.ShapeDtypeStruct((8, D), F32)],
        args=(dgt, dq, dk, dv, x, dres, g, wint), side=side)


def _wgrad(a, b, rows, row0, into, name, relu2=False, a_is_transposed=False):
    M, T = a.shape if a_is_transposed else a.shape[::-1]
    tmm = next(t for t in (1024, 512, 256) if M % t == 0 and row0 % t == 0)
    tk = min(2048, T)
    nk = T // tk
    blk0 = row0 // tmm

    def body(*refs):
        a_ref, b_ref = refs[0], refs[1]
        o_ref, acc_ref = refs[-2], refs[-1]
        kk = pl.program_id(1)

        @pl.when(kk == 0)
        def _():
            acc_ref[...] = jnp.zeros_like(acc_ref)

        av = a_ref[...]
        if relu2:
            t = jnp.maximum(av.astype(F32), 0.0)
            av = (t * t).astype(BF)
        acc_ref[...] += _dot(av, b_ref[...], NN if a_is_transposed else TN)

        @pl.when(kk == nk - 1)
        def _():
            o_ref[...] = acc_ref[...].astype(BF)

    a_spec = pl.BlockSpec((tmm, tk), lambda j, kk: (j, kk)) if a_is_transposed else pl.BlockSpec((tk, tmm), lambda j, kk: (kk, j))
    in_specs = [a_spec, pl.BlockSpec((tk, D), lambda j, kk: (kk, 0))]
    args = [a, b]
    if into is not None:
        in_specs.append(ANY)
        args.append(into)
    (out,), _ = _call(
        body, name=name, grid=(M // tmm, nk), in_specs=in_specs,
        out_specs=[pl.BlockSpec((tmm, D), lambda j, kk: (blk0 + j, 0))],
        out_shape=[jax.ShapeDtypeStruct((rows, D), BF)], scratch_shapes=[pltpu.VMEM((tmm, D), F32)],
        aliases={2: 0} if into is not None else None, args=args)
    return out


def _adamw(w, g, m, v):
    m = B1 * m + (1.0 - B1) * g
    v = B2 * v + (1.0 - B2) * (g * g)
    m_hat = m / (1.0 - B1 ** STEP)
    v_hat = v / (1.0 - B2 ** STEP)
    return -LR * (m_hat / (jnp.sqrt(v_hat) + AEPS) + WD * w), m, v


def _adam_sum(land, w, m, v, l, into, name):
    _, r, _ = land.shape
    tr = 208 if r % 208 == 0 else (256 if r % 256 == 0 else r)

    def body(land_ref, w_ref, m_ref, v_ref, *rest):
        g_ref, d_ref, nm_ref, nv_ref = rest[-4:]
        g = land_ref[0].astype(F32)
        for s in range(1, NDEV):
            g = g + land_ref[s].astype(F32)
        g_ref[...] = g
        d_ref[...], nm_ref[...], nv_ref[...] = _adamw(w_ref[...], g, m_ref[...], v_ref[...])

    blk = lambda: pl.BlockSpec((None, tr, D), lambda j: (l, j, 0))
    in_specs = [pl.BlockSpec((NDEV, tr, D), lambda j: (0, j, 0)), blk(), blk(), blk()]
    args = [land, w, m, v]
    aliases = None
    if into is not None:
        in_specs += [ANY] * 4
        args += list(into)
        aliases = {4 + t: t for t in range(4)}
    outs, _ = _call(body, name=name, grid=(r // tr,), in_specs=in_specs, out_specs=[blk()] * 4,
                    out_shape=[jax.ShapeDtypeStruct(w.shape, F32)] * 4, aliases=aliases, args=args)
    return outs


def _adam_small(land, w, m, v):
    def body(land_ref, w_ref, m_ref, v_ref, g_ref, d_ref, nm_ref, nv_ref):
        g = land_ref[0]
        for s in range(1, NDEV):
            g = g + land_ref[s]
        g_ref[...] = g
        d_ref[...], nm_ref[...], nv_ref[...] = _adamw(w_ref[...], g, m_ref[...], v_ref[...])

    return pl.pallas_call(
        body, name="adam_small", out_shape=[jax.ShapeDtypeStruct(w.shape, F32)] * 4,
        compiler_params=pltpu.CompilerParams(vmem_limit_bytes=VMEM_LIMIT))(land, w, m, v)


def _pack_small(g_mix, b_gates, sinks, conv_b, g_mlp, g_final, conv_w_rows, extra):
    sink_row = jnp.zeros((1, D), F32).at[0, :2 * 16].set(sinks.reshape(-1))
    return jnp.concatenate([g_mix, b_gates.reshape(4, D), sink_row, conv_b, g_mlp, g_final.reshape(1, D),
                            conv_w_rows, extra, jnp.zeros((SMALL_ROWS - 19, D), F32)], axis=0)


def kernel(x, g_mix, w_in, b_gates, sinks, w_attn_out, conv_w, conv_b, w_conv_out, w_o, g_mlp, w_up, w_down, g_final, loss_target, m_g_mix, m_w_in, m_b_gates, m_sinks, m_w_attn_out, m_conv_w, m_conv_b, m_w_conv_out, m_w_o, m_g_mlp, m_w_up, m_w_down, m_g_final, v_g_mix, v_w_in, v_b_gates, v_sinks, v_w_attn_out, v_conv_w, v_conv_b, v_w_conv_out, v_w_o, v_g_mlp, v_w_up, v_w_down, v_g_final):
    nseq, S, _ = x.shape
    T = nseq * S
    tm_in = min(512, S)
    tm = min(256, S)
    xi, yi, ci = _position()
    me = 4 * xi + 2 * yi + ci
    tr = lambda t: jnp.swapaxes(t, 1, 2)
    blocks = lambda t: t.reshape(NDEV, t.shape[0] // NDEV, D)

    win_t, wup_t = tr(w_in), tr(w_up)
    sh_win = [win_t[l].astype(BF)[None] for l in range(L)]
    sh_w3 = [jnp.stack([w_attn_out[l], w_conv_out[l], w_o[l]]).astype(BF) for l in range(L)]
    sh_w2 = [jnp.stack([wup_t[l], w_down[l]]).astype(BF) for l in range(L)]
    wint, w3, w2 = [None] * L, [None] * L, [None] * L
    wint0_g, cw_g = _remote_only(_Gather([sh_win[0], jnp.pad(conv_w, ((0, 0), (0, 5), (0, 0)))]), "gather_first")
    wint[0] = wint0_g.reshape(NP, D)
    cw = jnp.swapaxes(cw_g, 1, 2).reshape(L, 8, D)
    slopes = np.power(np.float32(2.0), -8.0 * np.arange(1, 17, dtype=np.float32) / 16).astype(np.float32)

    xf = x.reshape(T, D)
    saved = []
    cur = xf
    for l in range(L):
        sm = jnp.stack([sinks[l], jnp.asarray(slopes)])
        bg = b_gates[l].reshape(2, D)
        (gt, q, k, v, h), got = _inproj_fwd(cur, g_mix[l:l + 1], wint[l], l, tm_in,
                                            side=_Gather([sh_w3[0]]) if l == 0 else None)
        if l == 0:
            w3[0] = got[0].reshape(3, D, D)
        (att,), got = _attn_fwd(q, k, v, sm, l, S, side=_Gather([sh_w2[0], sh_win[1]]) if l == 0 else None)
        if l == 0:
            w2[0], wint[1] = got[0].reshape(2, F, D), got[1].reshape(NP, D)
        (x1, mg, co, ya, yc), got = _mixer_fwd(cur, gt, att, w3[l], bg, cw[l], conv_b[l:l + 1], l, S, tm,
                                               side=_Gather([sh_w3[1]]) if l == 0 else None)
        if l == 0:
            w3[1] = got[0].reshape(3, D, D)
        (x2, h2, a), got = _mlp_fwd(x1, g_mlp[l:l + 1], w2[l], l, tm, side=_Gather([sh_w2[1]]) if l == 0 else None)
        if l == 0:
            w2[1] = got[0].reshape(2, F, D)
        saved.append(dict(x=cur, gt=gt, q=q, k=k, v=v, h=h, att=att, x1=x1, mg=mg, co=co, ya=ya, yc=yc, h2=h2, a=a,
                          sm=sm, bg=bg))
        cur = x2

    dcur, acc_loss = _loss_head(cur, loss_target.reshape(T, D), g_final.reshape(1, D), tm_in)

    masters = {"w_in": (win_t, tr(m_w_in), tr(v_w_in)), "w_attn_out": (w_attn_out, m_w_attn_out, v_w_attn_out),
               "w_conv_out": (w_conv_out, m_w_conv_out, v_w_conv_out), "w_o": (w_o, m_w_o, v_w_o),
               "w_up": (wup_t, tr(m_w_up), tr(v_w_up)), "w_down": (w_down, m_w_down, v_w_down)}
    big = {name: None for name in masters}

    def adam(name, land, l):
        big[name] = _adam_sum(land, *masters[name], l, big[name], f"adam_{name}{l}")

    acc_in, acc_mix, acc_mlp, dsink = [None] * L, [None] * L, [None] * L, [None] * L
    pending = None
    for l in reversed(range(L)):
        sv = saved[l]
        side = _Exchange([blocks(pending[0])]) if pending else None
        (da, dx1, dx2b, acc_mlp[l]), got = _mlp_bwd(dcur, sv["x1"], sv["a"], g_mlp[l:l + 1], w2[l], l, tm, side=side)
        if pending:
            adam("w_in", got[0], pending[1])
        d_wdn = _wgrad(sv["a"], dx2b, F, 0, None, f"wgrad_down{l}", relu2=True)
        d_wup = _wgrad(da, sv["h2"], F, 0, None, f"wgrad_up{l}")
        (dgt, datt, dya, dyc, dx1b, acc_mix[l]), got = _mixer_bwd(
            dx1, sv["gt"], sv["ya"], sv["yc"], w3[l], sv["bg"], cw[l], conv_b[l:l + 1], l, S, tm,
            side=_Exchange([blocks(d_wdn)]))
        adam("w_down", got[0], l)
        d_wo = _wgrad(sv["mg"], dx1b, D, 0, None, f"wgrad_o{l}")
        d_wao = _wgrad(sv["att"], dya, D, 0, None, f"wgrad_attn_out{l}", a_is_transposed=True)
        d_wco = _wgrad(sv["co"], dyc, D, 0, None, f"wgrad_conv_out{l}")
        d_win = _wgrad(dgt, sv["h"], NP, NQ + 2 * NKV, None, f"wgrad_gates{l}")
        (dq, dk, dv, dsink[l]), got = _attn_bwd(
            sv["q"], sv["k"], sv["v"], sv["att"], datt, sv["sm"], l, S,
            side=_Exchange([blocks(d_wup), blocks(d_wo), blocks(d_wao), blocks(d_wco)]))
        for name, land in zip(["w_up", "w_o", "w_attn_out", "w_conv_out"], got):
            adam(name, land, l)
        d_win = _wgrad(dq, sv["h"], NP, 0, d_win, f"wgrad_q{l}", a_is_transposed=True)
        d_win = _wgrad(dk, sv["h"], NP, NQ, d_win, f"wgrad_k{l}", a_is_transposed=True)
        d_win = _wgrad(dv, sv["h"], NP, NQ + NKV, d_win, f"wgrad_v{l}", a_is_transposed=True)
        side = _Exchange([blocks(d_win)]) if l == 0 else None
        (dcur, acc_in[l]), got = _inproj_bwd(dgt, dq, dk, dv, sv["x"], dx1, g_mix[l:l + 1], wint[l], l, tm_in, side=side)
        if l == 0:
            adam("w_in", got[0], 0)
        else:
            pending = (d_win, l)

    dsinks = jnp.stack([dsink[l][:, :, :, 0].sum(axis=0).reshape(16) for l in range(L)])
    small = _pack_small(
        jnp.concatenate([acc_in[l][0:1] for l in range(L)]),
        jnp.stack([acc_mix[l][0:2].reshape(2 * D) for l in range(L)]),
        dsinks,
        jnp.concatenate([acc_mix[l][2:3] for l in range(L)]),
        jnp.concatenate([acc_mlp[l][0:1] for l in range(L)]),
        acc_loss[0],
        jnp.concatenate([acc_mix[l][3:6] for l in range(L)]),
        acc_loss[1:2])
    (small_land,) = _remote_only(_Exchange([], [small]), "exchange_small")

    def place(cw_shard):
        return lax.dynamic_update_slice(jnp.zeros((2 * 3, D), F32), cw_shard.reshape(2 * 3, D // NDEV), (0, me * (D // NDEV)))

    zero_row = jnp.zeros((1, D), F32)
    packs = [_pack_small(a, b, c_, d_, e, f, place(g_), zero_row) for a, b, c_, d_, e, f, g_ in
             [(g_mix, b_gates, sinks, conv_b, g_mlp, g_final, conv_w),
              (m_g_mix, m_b_gates, m_sinks, m_conv_b, m_g_mlp, m_g_final, m_conv_w),
              (v_g_mix, v_b_gates, v_sinks, v_conv_b, v_g_mlp, v_g_final, v_conv_w)]]
    small_out = _adam_small(small_land, *packs)

    def unpack(p):
        cwp = lax.dynamic_slice(p[12:18], (0, me * (D // NDEV)), (6, D // NDEV)).reshape(L, 3, D // NDEV)
        return dict(g_mix=p[0:2], b_gates=p[2:6].reshape(L, 2 * D), sinks=p[6, :32].reshape(L, 16), conv_b=p[7:9],
                    g_mlp=p[9:11], g_final=p[11], conv_w=cwp)

    small_kinds = [unpack(p) for p in small_out]
    for name in ("w_in", "w_up"):
        big[name] = [tr(o) for o in big[name]]
    order = ["g_mix", "w_in", "b_gates", "sinks", "w_attn_out", "conv_w", "conv_b", "w_conv_out", "w_o", "g_mlp",
             "w_up", "w_down", "g_final"]
    out = [small_out[0][18, 0], dcur.reshape(nseq, S, D)]
    for kind in range(4):
        for name in order:
            out.append(big[name][kind] if name in big else small_kinds[kind][name])
    return tuple(out)
```

```python
import numpy as np
import jax
import jax.numpy as jnp
from jax import lax
from jax.experimental import pallas as pl
from jax.experimental.pallas import tpu as pltpu

D = 1024
NG = 5 * D
NQ = 1024
NKV = 256
NP = NQ + 2 * NKV + NG
F = 4096
HD = 64
GQ = 4
WIN = 128
L = 2
NDEV = 8
EPS = 1e-6
NEG = -1e30
SMALL_ROWS = 24
LR, B1, B2, AEPS, WD, STEP = 0.001, 0.9, 0.999, 1e-08, 0.01, 10

BF = jnp.bfloat16
F32 = jnp.float32
MESH = pl.DeviceIdType.MESH
VMEM_LIMIT = 60 * 1024 * 1024
ANY = pl.BlockSpec(memory_space=pl.ANY)

NN = ((1,), (0,))
NT = ((1,), (1,))
TN = ((0,), (0,))


def _dot(a, b, dims):
    return lax.dot_general(a, b, (dims, ((), ())), preferred_element_type=F32)


def _resident(shape, imap):
    return pl.BlockSpec(shape, imap, pipeline_mode=pl.Buffered(1))


def _position():
    return lax.axis_index("x"), lax.axis_index("y"), lax.axis_index("c")


class _Gather:
    def __init__(self, shards):
        n = len(shards)
        self.inputs = list(shards)
        self.out_shape = [jax.ShapeDtypeStruct((s.shape[0], NDEV) + s.shape[1:], s.dtype) for s in shards]
        self.scratch = [pltpu.SemaphoreType.DMA((n, 7)), pltpu.SemaphoreType.DMA((n, 7)), pltpu.SemaphoreType.DMA((n,))]

    def _plan(self, src, dst, sems):
        send_sems, recv_sems, local_sems = sems
        n = len(src)
        x, y, c = _position()
        me, sibling = (x, y, c), (x, y, 1 - c)
        chips = [(1 - x, y), (x, 1 - y), (1 - x, 1 - y)]

        def rows(a, p):
            return dst[a].at[:, 4 * p[0] + 2 * p[1] + p[2]]

        def copy(a, k, block, to, from_src=False):
            return pltpu.make_async_remote_copy(
                src_ref=src[a] if from_src else rows(a, block), dst_ref=rows(a, block),
                send_sem=send_sems.at[a, k], recv_sem=recv_sems.at[a, k], device_id=to, device_id_type=MESH)

        mine = [pltpu.make_async_copy(src[a], rows(a, me), local_sems.at[a]) for a in range(n)]
        first = []
        for a in range(n):
            first.append(copy(a, 0, me, sibling, True))
            first += [copy(a, 1 + t, me, (*chip, c), True) for t, chip in enumerate(chips)]
        return n, c, me, sibling, chips, copy, mine, first

    def start(self, src, dst, sems):
        *_, mine, first = self._plan(src, dst, sems)
        for cp in mine + first:
            cp.start()

    def finish(self, src, dst, sems):
        n, c, me, sibling, chips, copy, mine, first = self._plan(src, dst, sems)
        passed = []
        for t, chip in enumerate(chips):
            for a in range(n):
                copy(a, 1 + t, (*chip, c), me).wait_recv()
                cp = copy(a, 4 + t, (*chip, c), sibling)
                cp.start()
                passed.append(cp)
        for a in range(n):
            copy(a, 0, sibling, me).wait_recv()
            for t, chip in enumerate(chips):
                copy(a, 4 + t, (*chip, 1 - c), me).wait_recv()
        for cp in first + passed:
            cp.wait_send()
        for cp in mine:
            cp.wait()


class _Exchange:
    def __init__(self, grads, everyone=()):
        self.inputs = list(grads) + list(everyone)
        self.n_blocked = len(grads)
        n = len(self.inputs)
        self.out_shape = [jax.ShapeDtypeStruct(g.shape, g.dtype) for g in grads]
        self.out_shape += [jax.ShapeDtypeStruct((NDEV,) + e.shape, e.dtype) for e in everyone]
        self.scratch = [pltpu.SemaphoreType.DMA((n, 7)), pltpu.SemaphoreType.DMA((n, 7)), pltpu.SemaphoreType.DMA((n,))]

    def _plan(self, src, land, sems):
        send_sems, recv_sems, local_sems = sems
        x, y, c = _position()
        me = 4 * x + 2 * y + c

        def parts(peer_idx):
            return [(s.at[peer_idx] if a < self.n_blocked else s, land[a].at[me]) for a, s in enumerate(src)]

        local = [pltpu.make_async_copy(s, d, local_sems.at[a]) for a, (s, d) in enumerate(parts(me))]
        sent = []
        for rel in range(1, NDEV):
            px = 1 - x if rel & 4 else x
            py = 1 - y if rel & 2 else y
            pc = 1 - c if rel & 1 else c
            for a, (s, d) in enumerate(parts(4 * px + 2 * py + pc)):
                sent.append(pltpu.make_async_remote_copy(
                    src_ref=s, dst_ref=d, send_sem=send_sems.at[a, rel - 1], recv_sem=recv_sems.at[a, rel - 1],
                    device_id=(px, py, pc), device_id_type=MESH))
        return local, sent

    def start(self, src, land, sems):
        local, sent = self._plan(src, land, sems)
        for cp in local + sent:
            cp.start()

    def finish(self, src, land, sems):
        local, sent = self._plan(src, land, sems)
        for cp in sent:
            cp.wait_recv()
        for cp in sent:
            cp.wait_send()
        for cp in local:
            cp.wait()


def _call(body, *, name, grid, in_specs, out_specs, out_shape, args, scratch_shapes=(), aliases=None, side=None):
    sem = ("arbitrary",) * len(grid)
    if side is None:
        outs = pl.pallas_call(
            body, name=name, grid=grid, in_specs=in_specs, out_specs=out_specs, out_shape=out_shape,
            scratch_shapes=list(scratch_shapes), input_output_aliases=aliases or {},
            compiler_params=pltpu.CompilerParams(dimension_semantics=sem, vmem_limit_bytes=VMEM_LIMIT))(*args)
        return outs, []
    ni, no, ns = len(in_specs), len(out_specs), len(scratch_shapes)
    si, so = len(side.inputs), len(side.out_shape)

    def hosted(*refs):
        ins, refs = refs[:ni], refs[ni:]
        sins, refs = refs[:si], refs[si:]
        outs, refs = refs[:no], refs[no:]
        souts, refs = refs[:so], refs[so:]
        scr, sscr = refs[:ns], refs[ns:]
        at_first = pl.program_id(0) == 0
        at_last = pl.program_id(0) == grid[0] - 1
        for d in range(1, len(grid)):
            at_first &= pl.program_id(d) == 0
            at_last &= pl.program_id(d) == grid[d] - 1

        @pl.when(at_first)
        def _():
            side.start(sins, souts, sscr)

        body(*ins, *outs, *scr)

        @pl.when(at_last)
        def _():
            side.finish(sins, souts, sscr)

    outs = pl.pallas_call(
        hosted, name=name, grid=grid, in_specs=list(in_specs) + [ANY] * si, out_specs=list(out_specs) + [ANY] * so,
        out_shape=list(out_shape) + side.out_shape, scratch_shapes=list(scratch_shapes) + side.scratch,
        input_output_aliases=aliases or {},
        compiler_params=pltpu.CompilerParams(dimension_semantics=sem, vmem_limit_bytes=VMEM_LIMIT, has_side_effects=True),
    )(*args, *side.inputs)
    return outs[:no], outs[no:]


def _remote_only(side, name):
    n = len(side.inputs)

    def body(*refs):
        src, dst, sems = refs[:n], refs[n:n + len(side.out_shape)], refs[n + len(side.out_shape):]
        side.start(src, dst, sems)
        side.finish(src, dst, sems)

    return pl.pallas_call(
        body, name=name, in_specs=[ANY] * n, out_specs=[ANY] * len(side.out_shape), out_shape=side.out_shape,
        scratch_shapes=side.scratch, compiler_params=pltpu.CompilerParams(has_side_effects=True))(*side.inputs)


def _rms(x, g):
    r = lax.rsqrt(jnp.mean(x * x, axis=-1, keepdims=True) + EPS)
    return x * r * g


def _rms_bwd(dy, x, g):
    r = lax.rsqrt(jnp.mean(x * x, axis=-1, keepdims=True) + EPS)
    xh = x * r
    dxh = dy * g
    dx = r * (dxh - xh * jnp.mean(dxh * xh, axis=-1, keepdims=True))
    return dx, jnp.sum(dy * xh, axis=0, keepdims=True)


def _zero_at_first_step(acc_ref):
    first = pl.program_id(0) == 0

    @pl.when(first)
    def _():
        acc_ref[...] = jnp.zeros_like(acc_ref)


ROW1 = lambda: _resident((1, D), lambda i: (0, 0))
ACC = lambda: pl.BlockSpec((8, D), lambda i: (0, 0))


def _inproj_fwd(x, g, wint, l, tm, side=None):
    T = x.shape[0]

    def body(x_ref, g_ref, w_ref, gt_ref, q_ref, k_ref, v_ref, h_ref):
        h = _rms(x_ref[...], g_ref[...]).astype(BF)
        h_ref[...] = h
        q_ref[...] = _dot(w_ref[0:NQ, :], h, NT).astype(BF)
        k_ref[...] = _dot(w_ref[NQ:NQ + NKV, :], h, NT).astype(BF)
        v_ref[...] = _dot(w_ref[NQ + NKV:NQ + 2 * NKV, :], h, NT).astype(BF)
        for s in range(5):
            lo = NQ + 2 * NKV + s * D
            gt_ref[:, s * D:(s + 1) * D] = _dot(h, w_ref[lo:lo + D, :], NT).astype(BF)

    tok = lambda w: pl.BlockSpec((tm, w), lambda i: (i, 0))
    feat = lambda w: pl.BlockSpec((w, tm), lambda i: (0, i))
    return _call(
        body, name=f"inproj_fwd{l}", grid=(T // tm,),
        in_specs=[tok(D), ROW1(), _resident((NP, D), lambda i: (0, 0))],
        out_specs=[tok(NG), feat(NQ), feat(NKV), feat(NKV), tok(D)],
        out_shape=[jax.ShapeDtypeStruct((T, NG), BF)] + [jax.ShapeDtypeStruct((w, T), BF) for w in (NQ, NKV, NKV)]
        + [jax.ShapeDtypeStruct((T, D), BF)],
        args=(x, g, wint), side=side)


def _band_geometry():
    j = lax.broadcasted_iota(jnp.int32, (2 * WIN, WIN), 0)
    r = lax.broadcasted_iota(jnp.int32, (2 * WIN, WIN), 1)
    dist = WIN + r - j
    dist0 = r - j
    return (dist.astype(F32), (dist >= 0) & (dist < WIN)), (dist0.astype(F32), dist0 >= 0)


def _pair_biases(sm_ref, pj):
    return [[jnp.where(ok, -sm_ref[1, pj * 2 * GQ + h] * dist, NEG) for h in range(2 * GQ)]
            for dist, ok in _band_geometry()]


def _heads_on_lanes(ref, kvh, r0):
    return jnp.concatenate([ref[(kvh * GQ + g) * HD:(kvh * GQ + g + 1) * HD, pl.ds(r0, WIN)] for g in range(GQ)], axis=1)


def _band_probs(sm_ref, head0, q_ref, k_ref, r0, p0, kvh, biases):
    qt = _heads_on_lanes(q_ref, kvh, r0) * jnp.asarray(HD ** -0.5, BF)
    kt = k_ref[kvh * HD:(kvh + 1) * HD, pl.ds(p0, 2 * WIN)]
    st = _dot(kt, qt, TN)
    heads = []
    for g in range(GQ):
        sink = sm_ref[0, head0 + g]
        s = st[:, g * WIN:(g + 1) * WIN] + biases[kvh * GQ + g]
        m = jnp.maximum(jnp.max(s, axis=0, keepdims=True), sink)
        p = jnp.exp(s - m)
        ps = jnp.exp(sink - m)
        heads.append((p, ps, 1.0 / (jnp.sum(p, axis=0, keepdims=True) + ps)))
    return qt, kt, heads


def _attn_fwd(q, k, v, sm, l, S, side=None):
    T = q.shape[1]
    nblk = S // WIN
    unroll = 3 if (nblk - 1) % 3 == 0 else 1

    def body(sm_ref, q_ref, k_ref, v_ref, o_ref):
        pj = pl.program_id(1)
        biases = _pair_biases(sm_ref, pj)

        def block(i, first):
            r0 = 0 if first else pl.multiple_of(i * WIN, WIN)
            p0 = 0 if first else pl.multiple_of(i * WIN - WIN, WIN)
            for kvh in range(2):
                head0 = (2 * pj + kvh) * GQ
                _, _, heads = _band_probs(sm_ref, head0, q_ref, k_ref, r0, p0, kvh, biases[first])
                pt = jnp.concatenate([p.astype(BF) for p, _, _ in heads], axis=1)
                ot = _dot(v_ref[kvh * HD:(kvh + 1) * HD, pl.ds(p0, 2 * WIN)], pt, NN)
                for g in range(GQ):
                    rows = slice((kvh * GQ + g) * HD, (kvh * GQ + g + 1) * HD)
                    o_ref[rows, pl.ds(r0, WIN)] = (ot[:, g * WIN:(g + 1) * WIN] * heads[g][2]).astype(BF)

        block(0, True)

        def rest(t, c):
            for u in range(unroll):
                block(1 + unroll * t + u, False)
            return c

        lax.fori_loop(0, (nblk - 1) // unroll, rest, 0)

    wide = lambda: pl.BlockSpec((2 * GQ * HD, S), lambda s, p: (p, s))
    narrow = lambda: pl.BlockSpec((2 * HD, S), lambda s, p: (p, s))
    return _call(
        body, name=f"attn_fwd{l}", grid=(T // S, 2),
        in_specs=[pl.BlockSpec(memory_space=pltpu.SMEM), wide(), narrow(), narrow()],
        out_specs=[wide()], out_shape=[jax.ShapeDtypeStruct((D, T), BF)],
        args=(sm, q, k, v), side=side)


def _shift_rows(y, k, edge_rows, down):
    n = y.shape[0]
    rid = lax.broadcasted_iota(jnp.int32, y.shape, 0)
    out = pltpu.roll(y, k if down else n - k, 0)
    for t, row in enumerate(edge_rows):
        out = jnp.where(rid == (t if down else n - k + t), row, out)
    return out


def _mixer_fwd(x, gt, att, w3, bg, cw, cbias, l, S, tm, side=None):
    T = x.shape[0]

    def body(x_ref, cb_ref, cc_ref, cu_ref, ga_ref, gc_ref, cch_ref, cuh_ref, att_ref, wao_ref, wco_ref, wo_ref,
             bg_ref, cw_ref, cbias_ref, x1_ref, mg_ref, co_ref, ya_ref, yc_ref, zb_ref):
        first = (pl.program_id(0) * tm) % S == 0
        y = cc_ref[...].astype(F32) * cu_ref[...].astype(F32)
        hy1 = cch_ref[15:16, :].astype(F32) * cuh_ref[15:16, :].astype(F32)
        hy2 = cch_ref[14:15, :].astype(F32) * cuh_ref[14:15, :].astype(F32)
        hy1, hy2 = jnp.where(first, 0.0, hy1), jnp.where(first, 0.0, hy2)
        z = (cw_ref[0:1, :] * _shift_rows(y, 2, [hy2, hy1], True) + cw_ref[1:2, :] * _shift_rows(y, 1, [hy1], True)
             + cw_ref[2:3, :] * y)
        zb = z + cbias_ref[...]
        zb_ref[...] = zb.astype(BF)
        co = (cb_ref[...].astype(F32) * zb).astype(BF)
        co_ref[...] = co
        yc = _dot(co, wco_ref[...], NN)
        ya = _dot(att_ref[...], wao_ref[...], TN)
        ya_ref[...] = ya.astype(BF)
        yc_ref[...] = yc.astype(BF)
        sa = jax.nn.sigmoid(ga_ref[...].astype(F32) + bg_ref[0:1, :])
        sc = jax.nn.sigmoid(gc_ref[...].astype(F32) + bg_ref[1:2, :])
        mg = (sa * ya + sc * yc).astype(BF)
        mg_ref[...] = mg
        x1_ref[...] = x_ref[...] + _dot(mg, wo_ref[...], NN)

    tok = lambda: pl.BlockSpec((tm, D), lambda i: (i, 0))
    seg = lambda s: pl.BlockSpec((tm, D), lambda i: (i, s))
    halo = lambda s: pl.BlockSpec((16, D), lambda i: (jnp.maximum(i * (tm // 16) - 1, 0), s))
    wsp = lambda k: _resident((None, D, D), lambda i: (k, 0, 0))
    row = lambda n: _resident((n, D), lambda i: (0, 0))
    return _call(
        body, name=f"mixer_fwd{l}", grid=(T // tm,),
        in_specs=[tok(), seg(0), seg(1), seg(2), seg(3), seg(4), halo(1), halo(2),
                  pl.BlockSpec((D, tm), lambda i: (0, i)), wsp(0), wsp(1), wsp(2), row(2), row(8), row(1)],
        out_specs=[tok()] * 6,
        out_shape=[jax.ShapeDtypeStruct((T, D), dt) for dt in (F32, BF, BF, BF, BF, BF)],
        args=(x, gt, gt, gt, gt, gt, gt, gt, att, w3, w3, w3, bg, cw, cbias), side=side)


def _mlp_fwd(x1, g, w2, l, tm, side=None):
    T = x1.shape[0]
    FC = 1024

    def body(x_ref, g_ref, wup_ref, wdn_ref, x2_ref, h_ref, a_ref):
        x = x_ref[...]
        h = _rms(x, g_ref[...]).astype(BF)
        h_ref[...] = h
        acc = x
        for c in range(F // FC):
            a = _dot(h, wup_ref[c * FC:(c + 1) * FC, :], NT)
            a_ref[:, c * FC:(c + 1) * FC] = a.astype(BF)
            u = jnp.maximum(a, 0.0)
            acc = acc + _dot((u * u).astype(BF), wdn_ref[c * FC:(c + 1) * FC, :], NN)
        x2_ref[...] = acc

    tok = lambda w: pl.BlockSpec((tm, w), lambda i: (i, 0))
    wsp = lambda k: _resident((None, F, D), lambda i: (k, 0, 0))
    return _call(
        body, name=f"mlp_fwd{l}", grid=(T // tm,),
        in_specs=[tok(D), ROW1(), wsp(0), wsp(1)],
        out_specs=[tok(D), tok(D), tok(F)],
        out_shape=[jax.ShapeDtypeStruct((T, D), F32), jax.ShapeDtypeStruct((T, D), BF),
                   jax.ShapeDtypeStruct((T, F), BF)],
        args=(x1, g, w2, w2), side=side)


def _loss_head(x, target, g, tm):
    T = x.shape[0]
    nt = T // tm

    def body(x_ref, t_ref, g_ref, dx_ref, acc_ref):
        _zero_at_first_step(acc_ref)
        x = x_ref[...]
        gv = g_ref[...]
        err = _rms(x, gv) - t_ref[...]
        dx, dg = _rms_bwd(err * (1.0 / D), x, gv)
        dx_ref[...] = dx
        acc_ref[0:1, :] += dg
        acc_ref[1:2, :] += jnp.sum(err * err, axis=0, keepdims=True)

        @pl.when(pl.program_id(0) == nt - 1)
        def _():
            acc_ref[1:2, :] = jnp.zeros((1, D), F32) + (0.5 / D) * jnp.sum(acc_ref[1:2, :])

    tok = lambda: pl.BlockSpec((tm, D), lambda i: (i, 0))
    (dx, acc), _ = _call(
        body, name="loss_head", grid=(nt,), in_specs=[tok(), tok(), ROW1()], out_specs=[tok(), ACC()],
        out_shape=[jax.ShapeDtypeStruct((T, D), F32), jax.ShapeDtypeStruct((8, D), F32)], args=(x, target, g))
    return dx, acc


def _mlp_bwd(dx2, x1, a, g, w2, l, tm, side=None):
    T = dx2.shape[0]
    FC = 1024

    def body(d_ref, x_ref, a_ref, g_ref, wup_ref, wdn_ref, da_ref, dx1_ref, db_ref, acc_ref):
        _zero_at_first_step(acc_ref)
        d = d_ref[...]
        db = d.astype(BF)
        db_ref[...] = db
        dh = jnp.zeros((tm, D), F32)
        for c in range(F // FC):
            du = _dot(db, wdn_ref[c * FC:(c + 1) * FC, :], NT)
            da = (du * (2.0 * jnp.maximum(a_ref[:, c * FC:(c + 1) * FC].astype(F32), 0.0))).astype(BF)
            da_ref[:, c * FC:(c + 1) * FC] = da
            dh = dh + _dot(da, wup_ref[c * FC:(c + 1) * FC, :], NN)
        dx, dg = _rms_bwd(dh, x_ref[...], g_ref[...])
        dx1_ref[...] = d + dx
        acc_ref[0:1, :] += dg

    tok = lambda w: pl.BlockSpec((tm, w), lambda i: (i, 0))
    wsp = lambda k: _resident((None, F, D), lambda i: (k, 0, 0))
    return _call(
        body, name=f"mlp_bwd{l}", grid=(T // tm,),
        in_specs=[tok(D), tok(D), tok(F), ROW1(), wsp(0), wsp(1)],
        out_specs=[tok(F), tok(D), tok(D), ACC()],
        out_shape=[jax.ShapeDtypeStruct((T, F), BF), jax.ShapeDtypeStruct((T, D), F32),
                   jax.ShapeDtypeStruct((T, D), BF), jax.ShapeDtypeStruct((8, D), F32)],
        args=(dx2, x1, a, g, w2, w2), side=side)


def _mixer_bwd(dx1, gt, ya, yc, zb, w3, bg, cw, l, S, tm, side=None):
    T = dx1.shape[0]
    nt = T // tm

    def body(d_ref, cb_ref, cc_ref, cu_ref, ga_ref, gc_ref, ya_ref, yc_ref, zb_ref, wao_ref, wco_ref, wo_ref,
             bg_ref, cw_ref, dg_ref, datt_ref, dya_ref, dyc_ref, db_ref, acc_ref, carry_ref):
        ti = nt - 1 - pl.program_id(0)
        _zero_at_first_step(acc_ref)

        @pl.when(((ti + 1) * tm) % S == 0)
        def _():
            carry_ref[...] = jnp.zeros_like(carry_ref)

        db = d_ref[...].astype(BF)
        db_ref[...] = db
        dm = _dot(db, wo_ref[...], NT)
        sa = jax.nn.sigmoid(ga_ref[...].astype(F32) + bg_ref[0:1, :])
        sc = jax.nn.sigmoid(gc_ref[...].astype(F32) + bg_ref[1:2, :])
        dya = (dm * sa).astype(BF)
        dyc = (dm * sc).astype(BF)
        dya_ref[...] = dya
        dyc_ref[...] = dyc
        dga = dm * ya_ref[...].astype(F32) * sa * (1.0 - sa)
        dgc = dm * yc_ref[...].astype(F32) * sc * (1.0 - sc)
        dg_ref[:, 3 * D:4 * D] = dga.astype(BF)
        dg_ref[:, 4 * D:5 * D] = dgc.astype(BF)
        acc_ref[0:1, :] += jnp.sum(dga, axis=0, keepdims=True)
        acc_ref[1:2, :] += jnp.sum(dgc, axis=0, keepdims=True)
        datt_ref[...] = _dot(wao_ref[...], dya, NT).astype(BF)
        dco = _dot(dyc, wco_ref[...], NT)

        cc = cc_ref[...].astype(F32)
        cu = cu_ref[...].astype(F32)
        y = cc * cu
        dg_ref[:, 0:D] = (dco * zb_ref[...].astype(F32)).astype(BF)
        dz = dco * cb_ref[...].astype(F32)
        u1 = _shift_rows(dz, 1, [carry_ref[0:1, :]], False)
        u2 = _shift_rows(dz, 2, [carry_ref[0:1, :], carry_ref[1:2, :]], False)
        acc_ref[2:3, :] += jnp.sum(dz, axis=0, keepdims=True)
        acc_ref[3:4, :] += jnp.sum(u2 * y, axis=0, keepdims=True)
        acc_ref[4:5, :] += jnp.sum(u1 * y, axis=0, keepdims=True)
        acc_ref[5:6, :] += jnp.sum(dz * y, axis=0, keepdims=True)
        dy = cw_ref[2:3, :] * dz + cw_ref[1:2, :] * u1 + cw_ref[0:1, :] * u2
        dg_ref[:, D:2 * D] = (dy * cu).astype(BF)
        dg_ref[:, 2 * D:3 * D] = (dy * cc).astype(BF)
        carry_ref[...] = dz[0:8, :]

    tok = lambda w=D: pl.BlockSpec((tm, w), lambda i: (nt - 1 - i, 0))
    seg = lambda s: pl.BlockSpec((tm, D), lambda i: (nt - 1 - i, s))
    wsp = lambda k: _resident((None, D, D), lambda i: (k, 0, 0))
    row = lambda n: _resident((n, D), lambda i: (0, 0))
    return _call(
        body, name=f"mixer_bwd{l}", grid=(nt,),
        in_specs=[tok(), seg(0), seg(1), seg(2), seg(3), seg(4), tok(), tok(), tok(), wsp(0), wsp(1), wsp(2),
                  row(2), row(8)],
        out_specs=[tok(NG), pl.BlockSpec((D, tm), lambda i: (0, nt - 1 - i)), tok(), tok(), tok(), ACC()],
        out_shape=[jax.ShapeDtypeStruct((T, NG), BF), jax.ShapeDtypeStruct((D, T), BF)]
        + [jax.ShapeDtypeStruct((T, D), BF)] * 3 + [jax.ShapeDtypeStruct((8, D), F32)],
        scratch_shapes=[pltpu.VMEM((8, D), F32)],
        args=(dx1, gt, gt, gt, gt, gt, ya, yc, zb, w3, w3, w3, bg, cw), side=side)


def _attn_bwd(q, k, v, att, datt, sm, l, S, side=None):
    T = q.shape[1]
    nblk = S // WIN
    unroll = 3 if (nblk - 1) % 3 == 0 else 1
    scale = HD ** -0.5

    def body(sm_ref, q_ref, k_ref, v_ref, o_ref, do_ref, dq_ref, dk_ref, dv_ref, ds_ref, dka_ref, dva_ref):
        pj = pl.program_id(1)
        biases = _pair_biases(sm_ref, pj)
        dka_ref[...] = jnp.zeros_like(dka_ref)
        dva_ref[...] = jnp.zeros_like(dva_ref)

        def block(i, first, dsinks):
            r0 = 0 if first else pl.multiple_of(i * WIN, WIN)
            p0 = 0 if first else pl.multiple_of(i * WIN - WIN, WIN)
            out = []
            for kvh in range(2):
                head0 = (2 * pj + kvh) * GQ
                rows = slice(kvh * HD, (kvh + 1) * HD)
                qt, kt, heads = _band_probs(sm_ref, head0, q_ref, k_ref, r0, p0, kvh, biases[first])
                inv = jnp.concatenate([h[2] for h in heads], axis=1)
                dos = _heads_on_lanes(do_ref, kvh, r0).astype(F32) * inv
                delta = jnp.sum(dos * _heads_on_lanes(o_ref, kvh, r0).astype(F32), axis=0, keepdims=True)
                dosb = dos.astype(BF)
                dpt = _dot(v_ref[rows, pl.ds(p0, 2 * WIN)], dosb, TN)
                dst = jnp.concatenate(
                    [(p * (dpt[:, g * WIN:(g + 1) * WIN] - delta[:, g * WIN:(g + 1) * WIN])).astype(BF)
                     for g, (p, _, _) in enumerate(heads)], axis=1)
                pt = jnp.concatenate([p.astype(BF) for p, _, _ in heads], axis=1)
                dqt = _dot(kt, dst, NN) * scale
                for g in range(GQ):
                    hr = slice((kvh * GQ + g) * HD, (kvh * GQ + g + 1) * HD)
                    dq_ref[hr, pl.ds(r0, WIN)] = dqt[:, g * WIN:(g + 1) * WIN].astype(BF)
                dka_ref[rows, pl.ds(p0, 2 * WIN)] += _dot(qt, dst, NT)
                dva_ref[rows, pl.ds(p0, 2 * WIN)] += _dot(dosb, pt, NT)
                ps = jnp.concatenate([h[1] for h in heads], axis=1)
                out.append(dsinks[kvh] - ps * delta)
            return tuple(out)

        zero = jnp.zeros((1, GQ * WIN), F32)
        def rest(t, c):
            for u in range(unroll):
                c = block(1 + unroll * t + u, False, c)
            return c

        dsinks = lax.fori_loop(0, (nblk - 1) // unroll, rest, block(0, True, (zero, zero)))
        for kvh in range(2):
            for g in range(GQ):
                tot = jnp.sum(dsinks[kvh][:, g * WIN:(g + 1) * WIN])
                ds_ref[kvh * GQ + g:kvh * GQ + g + 1, :] = jnp.zeros((1, 128), F32) + tot
        dk_ref[...] = dka_ref[...].astype(BF)
        dv_ref[...] = dva_ref[...].astype(BF)

    wide = lambda: pl.BlockSpec((2 * GQ * HD, S), lambda s, p: (p, s))
    narrow = lambda: pl.BlockSpec((2 * HD, S), lambda s, p: (p, s))
    return _call(
        body, name=f"attn_bwd{l}", grid=(T // S, 2),
        in_specs=[pl.BlockSpec(memory_space=pltpu.SMEM), wide(), narrow(), narrow(), wide(), wide()],
        out_specs=[wide(), narrow(), narrow(), pl.BlockSpec((None, None, 8, 128), lambda s, p: (s, p, 0, 0))],
        out_shape=[jax.ShapeDtypeStruct((NQ, T), BF), jax.ShapeDtypeStruct((NKV, T), BF),
                   jax.ShapeDtypeStruct((NKV, T), BF), jax.ShapeDtypeStruct((T // S, 2, 8, 128), F32)],
        scratch_shapes=[pltpu.VMEM((2 * HD, S), F32), pltpu.VMEM((2 * HD, S), F32)],
        args=(sm, q, k, v, att, datt), side=side)


def _inproj_bwd(dgt, dq, dk, dv, x, dres, g, wint, l, tm, side=None):
    T = x.shape[0]

    def body(dg_ref, dq_ref, dk_ref, dv_ref, x_ref, dr_ref, g_ref, w_ref, dx_ref, acc_ref):
        _zero_at_first_step(acc_ref)
        dh = _dot(dq_ref[...], w_ref[0:NQ, :], TN)
        dh = dh + _dot(dk_ref[...], w_ref[NQ:NQ + NKV, :], TN)
        dh = dh + _dot(dv_ref[...], w_ref[NQ + NKV:NQ + 2 * NKV, :], TN)
        dh = dh + _dot(dg_ref[...], w_ref[NQ + 2 * NKV:NP, :], NN)
        dx, dg = _rms_bwd(dh, x_ref[...], g_ref[...])
        dx_ref[...] = dr_ref[...] + dx
        acc_ref[0:1, :] += dg

    tok = lambda w: pl.BlockSpec((tm, w), lambda i: (i, 0))
    feat = lambda w: pl.BlockSpec((w, tm), lambda i: (0, i))
    return _call(
        body, name=f"inproj_bwd{l}", grid=(T // tm,),
        in_specs=[tok(NG), feat(NQ), feat(NKV), feat(NKV), tok(D), tok(D), ROW1(), _resident((NP, D), lambda i: (0, 0))],
        out_specs=[tok(D), ACC()],
        out_shape=[jax.ShapeDtypeStruct((T, D), F32), jax.ShapeDtypeStruct((8, D), F32)],
        args=(dgt, dq, dk, dv, x, dres, g, wint), side=side)


def _wgrad(a, b, rows, row0, into, name, relu2=False, a_is_transposed=False):
    M, T = a.shape if a_is_transposed else a.shape[::-1]
    tmm = next(t for t in (1024, 512, 256) if M % t == 0 and row0 % t == 0)
    tk = min(2048, T)
    nk = T // tk
    blk0 = row0 // tmm

    def body(*refs):
        a_ref, b_ref = refs[0], refs[1]
        o_ref, acc_ref = refs[-2], refs[-1]
        kk = pl.program_id(1)

        @pl.when(kk == 0)
        def _():
            acc_ref[...] = jnp.zeros_like(acc_ref)

        av = a_ref[...]
        if relu2:
            t = jnp.maximum(av.astype(F32), 0.0)
            av = (t * t).astype(BF)
        acc_ref[...] += _dot(av, b_ref[...], NN if a_is_transposed else TN)

        @pl.when(kk == nk - 1)
        def _():
            o_ref[...] = acc_ref[...].astype(BF)

    a_spec = pl.BlockSpec((tmm, tk), lambda j, kk: (j, kk)) if a_is_transposed else pl.BlockSpec((tk, tmm), lambda j, kk: (kk, j))
    in_specs = [a_spec, pl.BlockSpec((tk, D), lambda j, kk: (kk, 0))]
    args = [a, b]
    if into is not None:
        in_specs.append(ANY)
        args.append(into)
    (out,), _ = _call(
        body, name=name, grid=(M // tmm, nk), in_specs=in_specs,
        out_specs=[pl.BlockSpec((tmm, D), lambda j, kk: (blk0 + j, 0))],
        out_shape=[jax.ShapeDtypeStruct((rows, D), BF)], scratch_shapes=[pltpu.VMEM((tmm, D), F32)],
        aliases={2: 0} if into is not None else None, args=args)
    return out


def _adamw(w, g, m, v):
    m = B1 * m + (1.0 - B1) * g
    v = B2 * v + (1.0 - B2) * (g * g)
    m_hat = m / (1.0 - B1 ** STEP)
    v_hat = v / (1.0 - B2 ** STEP)
    return -LR * (m_hat / (jnp.sqrt(v_hat) + AEPS) + WD * w), m, v


def _adam_sum(land, w, m, v, l, into, name):
    _, r, _ = land.shape
    tr = 208 if r % 208 == 0 else (256 if r % 256 == 0 else r)

    def body(land_ref, w_ref, m_ref, v_ref, *rest):
        g_ref, d_ref, nm_ref, nv_ref = rest[-4:]
        g = land_ref[0].astype(F32)
        for s in range(1, NDEV):
            g = g + land_ref[s].astype(F32)
        g_ref[...] = g
        d_ref[...], nm_ref[...], nv_ref[...] = _adamw(w_ref[...], g, m_ref[...], v_ref[...])

    blk = lambda: pl.BlockSpec((None, tr, D), lambda j: (l, j, 0))
    in_specs = [pl.BlockSpec((NDEV, tr, D), lambda j: (0, j, 0)), blk(), blk(), blk()]
    args = [land, w, m, v]
    aliases = None
    if into is not None:
        in_specs += [ANY] * 4
        args += list(into)
        aliases = {4 + t: t for t in range(4)}
    outs, _ = _call(body, name=name, grid=(r // tr,), in_specs=in_specs, out_specs=[blk()] * 4,
                    out_shape=[jax.ShapeDtypeStruct(w.shape, F32)] * 4, aliases=aliases, args=args)
    return outs


def _adam_small(land, w, m, v):
    def body(land_ref, w_ref, m_ref, v_ref, g_ref, d_ref, nm_ref, nv_ref):
        g = land_ref[0]
        for s in range(1, NDEV):
            g = g + land_ref[s]
        g_ref[...] = g
        d_ref[...], nm_ref[...], nv_ref[...] = _adamw(w_ref[...], g, m_ref[...], v_ref[...])

    return pl.pallas_call(
        body, name="adam_small", out_shape=[jax.ShapeDtypeStruct(w.shape, F32)] * 4,
        compiler_params=pltpu.CompilerParams(vmem_limit_bytes=VMEM_LIMIT))(land, w, m, v)


def _pack_small(g_mix, b_gates, sinks, conv_b, g_mlp, g_final, conv_w_rows, extra):
    sink_row = jnp.zeros((1, D), F32).at[0, :2 * 16].set(sinks.reshape(-1))
    return jnp.concatenate([g_mix, b_gates.reshape(4, D), sink_row, conv_b, g_mlp, g_final.reshape(1, D),
                            conv_w_rows, extra, jnp.zeros((SMALL_ROWS - 19, D), F32)], axis=0)


def kernel(x, g_mix, w_in, b_gates, sinks, w_attn_out, conv_w, conv_b, w_conv_out, w_o, g_mlp, w_up, w_down, g_final, loss_target, m_g_mix, m_w_in, m_b_gates, m_sinks, m_w_attn_out, m_conv_w, m_conv_b, m_w_conv_out, m_w_o, m_g_mlp, m_w_up, m_w_down, m_g_final, v_g_mix, v_w_in, v_b_gates, v_sinks, v_w_attn_out, v_conv_w, v_conv_b, v_w_conv_out, v_w_o, v_g_mlp, v_w_up, v_w_down, v_g_final):
    nseq, S, _ = x.shape
    T = nseq * S
    tm_in = min(512, S)
    tm = min(256, S)
    xi, yi, ci = _position()
    me = 4 * xi + 2 * yi + ci
    tr = lambda t: jnp.swapaxes(t, 1, 2)
    blocks = lambda t: t.reshape(NDEV, t.shape[0] // NDEV, D)

    win_t, wup_t = tr(w_in), tr(w_up)
    sh_win = [win_t[l].astype(BF)[None] for l in range(L)]
    sh_w3 = [jnp.stack([w_attn_out[l], w_conv_out[l], w_o[l]]).astype(BF) for l in range(L)]
    sh_w2 = [jnp.stack([wup_t[l], w_down[l]]).astype(BF) for l in range(L)]
    wint, w3, w2 = [None] * L, [None] * L, [None] * L
    wint0_g, cw_g = _remote_only(_Gather([sh_win[0], jnp.pad(conv_w, ((0, 0), (0, 5), (0, 0)))]), "gather_first")
    wint[0] = wint0_g.reshape(NP, D)
    cw = jnp.swapaxes(cw_g, 1, 2).reshape(L, 8, D)
    slopes = np.power(np.float32(2.0), -8.0 * np.arange(1, 17, dtype=np.float32) / 16).astype(np.float32)

    xf = x.reshape(T, D)
    saved = []
    cur = xf
    for l in range(L):
        sm = jnp.stack([sinks[l], jnp.asarray(slopes)])
        bg = b_gates[l].reshape(2, D)
        (gt, q, k, v, h), got = _inproj_fwd(cur, g_mix[l:l + 1], wint[l], l, tm_in,
                                            side=_Gather([sh_w3[0], sh_w2[0]]) if l == 0 else None)
        if l == 0:
            w3[0], w2[0] = got[0].reshape(3, D, D), got[1].reshape(2, F, D)
        (att,), got = _attn_fwd(q, k, v, sm, l, S, side=_Gather([sh_win[1]]) if l == 0 else None)
        if l == 0:
            wint[1] = got[0].reshape(NP, D)
        (x1, mg, co, ya, yc, zb), got = _mixer_fwd(cur, gt, att, w3[l], bg, cw[l], conv_b[l:l + 1], l, S, tm,
                                               side=_Gather([sh_w3[1]]) if l == 0 else None)
        if l == 0:
            w3[1] = got[0].reshape(3, D, D)
        (x2, h2, a), got = _mlp_fwd(x1, g_mlp[l:l + 1], w2[l], l, tm, side=_Gather([sh_w2[1]]) if l == 0 else None)
        if l == 0:
            w2[1] = got[0].reshape(2, F, D)
        saved.append(dict(x=cur, gt=gt, q=q, k=k, v=v, h=h, att=att, x1=x1, mg=mg, co=co, ya=ya, yc=yc, zb=zb, h2=h2, a=a,
                          sm=sm, bg=bg))
        cur = x2

    dcur, acc_loss = _loss_head(cur, loss_target.reshape(T, D), g_final.reshape(1, D), tm_in)

    masters = {"w_in": (win_t, tr(m_w_in), tr(v_w_in)), "w_attn_out": (w_attn_out, m_w_attn_out, v_w_attn_out),
               "w_conv_out": (w_conv_out, m_w_conv_out, v_w_conv_out), "w_o": (w_o, m_w_o, v_w_o),
               "w_up": (wup_t, tr(m_w_up), tr(v_w_up)), "w_down": (w_down, m_w_down, v_w_down)}
    big = {name: None for name in masters}

    def adam(name, land, l):
        big[name] = _adam_sum(land, *masters[name], l, big[name], f"adam_{name}{l}")

    acc_in, acc_mix, acc_mlp, dsink = [None] * L, [None] * L, [None] * L, [None] * L
    pending = None
    for l in reversed(range(L)):
        sv = saved[l]
        side = _Exchange([blocks(pending[0])]) if pending else None
        (da, dx1, dx2b, acc_mlp[l]), got = _mlp_bwd(dcur, sv["x1"], sv["a"], g_mlp[l:l + 1], w2[l], l, tm, side=side)
        if pending:
            adam("w_in", got[0], pending[1])
        d_wdn = _wgrad(sv["a"], dx2b, F, 0, None, f"wgrad_down{l}", relu2=True)
        d_wup = _wgrad(da, sv["h2"], F, 0, None, f"wgrad_up{l}")
        (dgt, datt, dya, dyc, dx1b, acc_mix[l]), got = _mixer_bwd(
            dx1, sv["gt"], sv["ya"], sv["yc"], sv["zb"], w3[l], sv["bg"], cw[l], l, S, tm,
            side=_Exchange([blocks(d_wdn)]))
        adam("w_down", got[0], l)
        d_wo = _wgrad(sv["mg"], dx1b, D, 0, None, f"wgrad_o{l}")
        d_wao = _wgrad(sv["att"], dya, D, 0, None, f"wgrad_attn_out{l}", a_is_transposed=True)
        d_wco = _wgrad(sv["co"], dyc, D, 0, None, f"wgrad_conv_out{l}")
        d_win = _wgrad(dgt, sv["h"], NP, NQ + 2 * NKV, None, f"wgrad_gates{l}")
        (dq, dk, dv, dsink[l]), got = _attn_bwd(
            sv["q"], sv["k"], sv["v"], sv["att"], datt, sv["sm"], l, S,
            side=_Exchange([blocks(d_wup), blocks(d_wo), blocks(d_wao), blocks(d_wco)]))
        for name, land in zip(["w_up", "w_o", "w_attn_out", "w_conv_out"], got):
            adam(name, land, l)
        d_win = _wgrad(dq, sv["h"], NP, 0, d_win, f"wgrad_q{l}", a_is_transposed=True)
        d_win = _wgrad(dk, sv["h"], NP, NQ, d_win, f"wgrad_k{l}", a_is_transposed=True)
        d_win = _wgrad(dv, sv["h"], NP, NQ + NKV, d_win, f"wgrad_v{l}", a_is_transposed=True)
        side = _Exchange([blocks(d_win)]) if l == 0 else None
        (dcur, acc_in[l]), got = _inproj_bwd(dgt, dq, dk, dv, sv["x"], dx1, g_mix[l:l + 1], wint[l], l, tm_in, side=side)
        if l == 0:
            adam("w_in", got[0], 0)
        else:
            pending = (d_win, l)

    dsinks = jnp.stack([dsink[l][:, :, :, 0].sum(axis=0).reshape(16) for l in range(L)])
    small = _pack_small(
        jnp.concatenate([acc_in[l][0:1] for l in range(L)]),
        jnp.stack([acc_mix[l][0:2].reshape(2 * D) for l in range(L)]),
        dsinks,
        jnp.concatenate([acc_mix[l][2:3] for l in range(L)]),
        jnp.concatenate([acc_mlp[l][0:1] for l in range(L)]),
        acc_loss[0],
        jnp.concatenate([acc_mix[l][3:6] for l in range(L)]),
        acc_loss[1:2])
    (small_land,) = _remote_only(_Exchange([], [small]), "exchange_small")

    def place(cw_shard):
        return lax.dynamic_update_slice(jnp.zeros((2 * 3, D), F32), cw_shard.reshape(2 * 3, D // NDEV), (0, me * (D // NDEV)))

    zero_row = jnp.zeros((1, D), F32)
    packs = [_pack_small(a, b, c_, d_, e, f, place(g_), zero_row) for a, b, c_, d_, e, f, g_ in
             [(g_mix, b_gates, sinks, conv_b, g_mlp, g_final, conv_w),
              (m_g_mix, m_b_gates, m_sinks, m_conv_b, m_g_mlp, m_g_final, m_conv_w),
              (v_g_mix, v_b_gates, v_sinks, v_conv_b, v_g_mlp, v_g_final, v_conv_w)]]
    small_out = _adam_small(small_land, *packs)

    def unpack(p):
        cwp = lax.dynamic_slice(p[12:18], (0, me * (D // NDEV)), (6, D // NDEV)).reshape(L, 3, D // NDEV)
        return dict(g_mix=p[0:2], b_gates=p[2:6].reshape(L, 2 * D), sinks=p[6, :32].reshape(L, 16), conv_b=p[7:9],
                    g_mlp=p[9:11], g_final=p[11], conv_w=cwp)

    small_kinds = [unpack(p) for p in small_out]
    for name in ("w_in", "w_up"):
        big[name] = [tr(o) for o in big[name]]
    order = ["g_mix", "w_in", "b_gates", "sinks", "w_attn_out", "conv_w", "conv_b", "w_conv_out", "w_o", "g_mlp",
             "w_up", "w_down", "g_final"]
    out = [small_out[0][18, 0], dcur.reshape(nseq, S, D)]
    for kind in range(4):
        for name in order:
            out.append(big[name][kind] if name in big else small_kinds[kind][name])
    return tuple(out)
```

```python
import numpy as np
import jax
import jax.numpy as jnp
from jax import lax
from jax.experimental import pallas as pl
from jax.experimental.pallas import tpu as pltpu

D = 1024
NG = 5 * D
NQ = 1024
NKV = 256
NP = NQ + 2 * NKV + NG
F = 4096
HD = 64
GQ = 4
WIN = 128
L = 2
NDEV = 8
EPS = 1e-6
NEG = -1e30
SMALL_ROWS = 24
LR, B1, B2, AEPS, WD, STEP = 0.001, 0.9, 0.999, 1e-08, 0.01, 10

BF = jnp.bfloat16
F32 = jnp.float32
MESH = pl.DeviceIdType.MESH
VMEM_LIMIT = 60 * 1024 * 1024
ANY = pl.BlockSpec(memory_space=pl.ANY)

NN = ((1,), (0,))
NT = ((1,), (1,))
TN = ((0,), (0,))


def _dot(a, b, dims):
    return lax.dot_general(a, b, (dims, ((), ())), preferred_element_type=F32)


def _resident(shape, imap):
    return pl.BlockSpec(shape, imap, pipeline_mode=pl.Buffered(1))


def _position():
    return lax.axis_index("x"), lax.axis_index("y"), lax.axis_index("c")


class _Gather:
    def __init__(self, shards):
        n = len(shards)
        self.inputs = list(shards)
        self.out_shape = [jax.ShapeDtypeStruct((s.shape[0], NDEV) + s.shape[1:], s.dtype) for s in shards]
        self.scratch = [pltpu.SemaphoreType.DMA((n, 7)), pltpu.SemaphoreType.DMA((n, 7)), pltpu.SemaphoreType.DMA((n,))]

    def _plan(self, src, dst, sems):
        send_sems, recv_sems, local_sems = sems
        n = len(src)
        x, y, c = _position()
        me, sibling = (x, y, c), (x, y, 1 - c)
        chips = [(1 - x, y), (x, 1 - y), (1 - x, 1 - y)]

        def rows(a, p):
            return dst[a].at[:, 4 * p[0] + 2 * p[1] + p[2]]

        def copy(a, k, block, to, from_src=False):
            return pltpu.make_async_remote_copy(
                src_ref=src[a] if from_src else rows(a, block), dst_ref=rows(a, block),
                send_sem=send_sems.at[a, k], recv_sem=recv_sems.at[a, k], device_id=to, device_id_type=MESH)

        mine = [pltpu.make_async_copy(src[a], rows(a, me), local_sems.at[a]) for a in range(n)]
        first = []
        for a in range(n):
            first.append(copy(a, 0, me, sibling, True))
            first += [copy(a, 1 + t, me, (*chip, c), True) for t, chip in enumerate(chips)]
        return n, c, me, sibling, chips, copy, mine, first

    def start(self, src, dst, sems):
        *_, mine, first = self._plan(src, dst, sems)
        for cp in mine + first:
            cp.start()

    def finish(self, src, dst, sems):
        n, c, me, sibling, chips, copy, mine, first = self._plan(src, dst, sems)
        passed = []
        for t, chip in enumerate(chips):
            for a in range(n):
                copy(a, 1 + t, (*chip, c), me).wait_recv()
                cp = copy(a, 4 + t, (*chip, c), sibling)
                cp.start()
                passed.append(cp)
        for a in range(n):
            copy(a, 0, sibling, me).wait_recv()
            for t, chip in enumerate(chips):
                copy(a, 4 + t, (*chip, 1 - c), me).wait_recv()
        for cp in first + passed:
            cp.wait_send()
        for cp in mine:
            cp.wait()


class _Exchange:
    def __init__(self, grads, everyone=()):
        self.inputs = list(grads) + list(everyone)
        self.n_blocked = len(grads)
        n = len(self.inputs)
        self.out_shape = [jax.ShapeDtypeStruct(g.shape, g.dtype) for g in grads]
        self.out_shape += [jax.ShapeDtypeStruct((NDEV,) + e.shape, e.dtype) for e in everyone]
        self.scratch = [pltpu.SemaphoreType.DMA((n, 7)), pltpu.SemaphoreType.DMA((n, 7)), pltpu.SemaphoreType.DMA((n,))]

    def _plan(self, src, land, sems):
        send_sems, recv_sems, local_sems = sems
        x, y, c = _position()
        me = 4 * x + 2 * y + c

        def parts(peer_idx):
            return [(s.at[peer_idx] if a < self.n_blocked else s, land[a].at[me]) for a, s in enumerate(src)]

        local = [pltpu.make_async_copy(s, d, local_sems.at[a]) for a, (s, d) in enumerate(parts(me))]
        sent = []
        for rel in range(1, NDEV):
            px = 1 - x if rel & 4 else x
            py = 1 - y if rel & 2 else y
            pc = 1 - c if rel & 1 else c
            for a, (s, d) in enumerate(parts(4 * px + 2 * py + pc)):
                sent.append(pltpu.make_async_remote_copy(
                    src_ref=s, dst_ref=d, send_sem=send_sems.at[a, rel - 1], recv_sem=recv_sems.at[a, rel - 1],
                    device_id=(px, py, pc), device_id_type=MESH))
        return local, sent

    def start(self, src, land, sems):
        local, sent = self._plan(src, land, sems)
        for cp in local + sent:
            cp.start()

    def finish(self, src, land, sems):
        local, sent = self._plan(src, land, sems)
        for cp in sent:
            cp.wait_recv()
        for cp in sent:
            cp.wait_send()
        for cp in local:
            cp.wait()


def _call(body, *, name, grid, in_specs, out_specs, out_shape, args, scratch_shapes=(), aliases=None, side=None):
    sem = ("arbitrary",) * len(grid)
    if side is None:
        outs = pl.pallas_call(
            body, name=name, grid=grid, in_specs=in_specs, out_specs=out_specs, out_shape=out_shape,
            scratch_shapes=list(scratch_shapes), input_output_aliases=aliases or {},
            compiler_params=pltpu.CompilerParams(dimension_semantics=sem, vmem_limit_bytes=VMEM_LIMIT))(*args)
        return outs, []
    ni, no, ns = len(in_specs), len(out_specs), len(scratch_shapes)
    si, so = len(side.inputs), len(side.out_shape)

    def hosted(*refs):
        ins, refs = refs[:ni], refs[ni:]
        sins, refs = refs[:si], refs[si:]
        outs, refs = refs[:no], refs[no:]
        souts, refs = refs[:so], refs[so:]
        scr, sscr = refs[:ns], refs[ns:]
        at_first = pl.program_id(0) == 0
        at_last = pl.program_id(0) == grid[0] - 1
        for d in range(1, len(grid)):
            at_first &= pl.program_id(d) == 0
            at_last &= pl.program_id(d) == grid[d] - 1

        @pl.when(at_first)
        def _():
            side.start(sins, souts, sscr)

        body(*ins, *outs, *scr)

        @pl.when(at_last)
        def _():
            side.finish(sins, souts, sscr)

    outs = pl.pallas_call(
        hosted, name=name, grid=grid, in_specs=list(in_specs) + [ANY] * si, out_specs=list(out_specs) + [ANY] * so,
        out_shape=list(out_shape) + side.out_shape, scratch_shapes=list(scratch_shapes) + side.scratch,
        input_output_aliases=aliases or {},
        compiler_params=pltpu.CompilerParams(dimension_semantics=sem, vmem_limit_bytes=VMEM_LIMIT, has_side_effects=True),
    )(*args, *side.inputs)
    return outs[:no], outs[no:]


def _remote_only(side, name):
    n = len(side.inputs)

    def body(*refs):
        src, dst, sems = refs[:n], refs[n:n + len(side.out_shape)], refs[n + len(side.out_shape):]
        side.start(src, dst, sems)
        side.finish(src, dst, sems)

    return pl.pallas_call(
        body, name=name, in_specs=[ANY] * n, out_specs=[ANY] * len(side.out_shape), out_shape=side.out_shape,
        scratch_shapes=side.scratch, compiler_params=pltpu.CompilerParams(has_side_effects=True))(*side.inputs)


def _rms(x, g):
    r = lax.rsqrt(jnp.mean(x * x, axis=-1, keepdims=True) + EPS)
    return x * r * g


def _rms_bwd(dy, x, g):
    r = lax.rsqrt(jnp.mean(x * x, axis=-1, keepdims=True) + EPS)
    xh = x * r
    dxh = dy * g
    dx = r * (dxh - xh * jnp.mean(dxh * xh, axis=-1, keepdims=True))
    return dx, jnp.sum(dy * xh, axis=0, keepdims=True)


def _zero_at_first_step(acc_ref):
    first = pl.program_id(0) == 0

    @pl.when(first)
    def _():
        acc_ref[...] = jnp.zeros_like(acc_ref)


ROW1 = lambda: _resident((1, D), lambda i: (0, 0))
ACC = lambda: pl.BlockSpec((8, D), lambda i: (0, 0))


def _inproj_fwd(x, g, wint, l, tm, side=None):
    T = x.shape[0]

    def body(x_ref, g_ref, w_ref, gt_ref, q_ref, k_ref, v_ref, h_ref):
        h = _rms(x_ref[...], g_ref[...]).astype(BF)
        h_ref[...] = h
        q_ref[...] = _dot(w_ref[0:NQ, :], h, NT).astype(BF)
        k_ref[...] = _dot(w_ref[NQ:NQ + NKV, :], h, NT).astype(BF)
        v_ref[...] = _dot(w_ref[NQ + NKV:NQ + 2 * NKV, :], h, NT).astype(BF)
        for s in range(5):
            lo = NQ + 2 * NKV + s * D
            gt_ref[:, s * D:(s + 1) * D] = _dot(h, w_ref[lo:lo + D, :], NT).astype(BF)

    tok = lambda w: pl.BlockSpec((tm, w), lambda i: (i, 0))
    feat = lambda w: pl.BlockSpec((w, tm), lambda i: (0, i))
    return _call(
        body, name=f"inproj_fwd{l}", grid=(T // tm,),
        in_specs=[tok(D), ROW1(), _resident((NP, D), lambda i: (0, 0))],
        out_specs=[tok(NG), feat(NQ), feat(NKV), feat(NKV), tok(D)],
        out_shape=[jax.ShapeDtypeStruct((T, NG), BF)] + [jax.ShapeDtypeStruct((w, T), BF) for w in (NQ, NKV, NKV)]
        + [jax.ShapeDtypeStruct((T, D), BF)],
        args=(x, g, wint), side=side)


def _band_geometry():
    j = lax.broadcasted_iota(jnp.int32, (2 * WIN, WIN), 0)
    r = lax.broadcasted_iota(jnp.int32, (2 * WIN, WIN), 1)
    dist = WIN + r - j
    dist0 = r - j
    return (dist.astype(F32), (dist >= 0) & (dist < WIN)), (dist0.astype(F32), dist0 >= 0)


def _pair_biases(sm_ref, pj):
    return [[jnp.where(ok, -sm_ref[1, pj * 2 * GQ + h] * dist, NEG) for h in range(2 * GQ)]
            for dist, ok in _band_geometry()]


def _reduce_rows(x, pair, whole):
    while x.shape[0] > 8:
        half = x.shape[0] // 2
        x = pair(x[:half], x[half:])
    return whole(x, axis=0, keepdims=True)


def _heads_on_lanes(ref, kvh, r0):
    return jnp.concatenate([ref[(kvh * GQ + g) * HD:(kvh * GQ + g + 1) * HD, pl.ds(r0, WIN)] for g in range(GQ)], axis=1)


def _band_probs(sm_ref, head0, q_ref, k_ref, r0, p0, kvh, biases):
    qt = _heads_on_lanes(q_ref, kvh, r0) * jnp.asarray(HD ** -0.5, BF)
    kt = k_ref[kvh * HD:(kvh + 1) * HD, pl.ds(p0, 2 * WIN)]
    st = _dot(kt, qt, TN)
    heads = []
    for g in range(GQ):
        sink = sm_ref[0, head0 + g]
        s = st[:, g * WIN:(g + 1) * WIN] + biases[kvh * GQ + g]
        m = jnp.maximum(_reduce_rows(s, jnp.maximum, jnp.max), sink)
        p = jnp.exp(s - m)
        ps = jnp.exp(sink - m)
        heads.append((p, ps, 1.0 / (_reduce_rows(p, jnp.add, jnp.sum) + ps)))
    return qt, kt, heads


def _attn_fwd(q, k, v, sm, l, S, side=None):
    T = q.shape[1]
    nblk = S // WIN
    unroll = next(u for u in (5, 3, 1) if (nblk - 1) % u == 0)

    def body(sm_ref, q_ref, k_ref, v_ref, o_ref):
        pj = pl.program_id(1)
        biases = _pair_biases(sm_ref, pj)

        def block(i, first):
            r0 = 0 if first else pl.multiple_of(i * WIN, WIN)
            p0 = 0 if first else pl.multiple_of(i * WIN - WIN, WIN)
            for kvh in range(2):
                head0 = (2 * pj + kvh) * GQ
                _, _, heads = _band_probs(sm_ref, head0, q_ref, k_ref, r0, p0, kvh, biases[first])
                pt = jnp.concatenate([p.astype(BF) for p, _, _ in heads], axis=1)
                ot = _dot(v_ref[kvh * HD:(kvh + 1) * HD, pl.ds(p0, 2 * WIN)], pt, NN)
                for g in range(GQ):
                    rows = slice((kvh * GQ + g) * HD, (kvh * GQ + g + 1) * HD)
                    o_ref[rows, pl.ds(r0, WIN)] = (ot[:, g * WIN:(g + 1) * WIN] * heads[g][2]).astype(BF)

        block(0, True)

        def rest(t, c):
            for u in range(unroll):
                block(1 + unroll * t + u, False)
            return c

        lax.fori_loop(0, (nblk - 1) // unroll, rest, 0)

    wide = lambda: pl.BlockSpec((2 * GQ * HD, S), lambda s, p: (p, s))
    narrow = lambda: pl.BlockSpec((2 * HD, S), lambda s, p: (p, s))
    return _call(
        body, name=f"attn_fwd{l}", grid=(T // S, 2),
        in_specs=[pl.BlockSpec(memory_space=pltpu.SMEM), wide(), narrow(), narrow()],
        out_specs=[wide()], out_shape=[jax.ShapeDtypeStruct((D, T), BF)],
        args=(sm, q, k, v), side=side)


def _shift_rows(y, k, edge_rows, down):
    n = y.shape[0]
    rid = lax.broadcasted_iota(jnp.int32, y.shape, 0)
    out = pltpu.roll(y, k if down else n - k, 0)
    for t, row in enumerate(edge_rows):
        out = jnp.where(rid == (t if down else n - k + t), row, out)
    return out


def _mixer_fwd(x, gt, att, w3, bg, cw, cbias, l, S, tm, side=None):
    T = x.shape[0]

    def body(x_ref, cb_ref, cc_ref, cu_ref, ga_ref, gc_ref, cch_ref, cuh_ref, att_ref, wao_ref, wco_ref, wo_ref,
             bg_ref, cw_ref, cbias_ref, x1_ref, mg_ref, co_ref, ya_ref, yc_ref, zb_ref):
        first = (pl.program_id(0) * tm) % S == 0
        y = cc_ref[...].astype(F32) * cu_ref[...].astype(F32)
        hy1 = cch_ref[15:16, :].astype(F32) * cuh_ref[15:16, :].astype(F32)
        hy2 = cch_ref[14:15, :].astype(F32) * cuh_ref[14:15, :].astype(F32)
        hy1, hy2 = jnp.where(first, 0.0, hy1), jnp.where(first, 0.0, hy2)
        z = (cw_ref[0:1, :] * _shift_rows(y, 2, [hy2, hy1], True) + cw_ref[1:2, :] * _shift_rows(y, 1, [hy1], True)
             + cw_ref[2:3, :] * y)
        zb = z + cbias_ref[...]
        zb_ref[...] = zb.astype(BF)
        co = (cb_ref[...].astype(F32) * zb).astype(BF)
        co_ref[...] = co
        yc = _dot(co, wco_ref[...], NN)
        ya = _dot(att_ref[...], wao_ref[...], TN)
        ya_ref[...] = ya.astype(BF)
        yc_ref[...] = yc.astype(BF)
        sa = jax.nn.sigmoid(ga_ref[...].astype(F32) + bg_ref[0:1, :])
        sc = jax.nn.sigmoid(gc_ref[...].astype(F32) + bg_ref[1:2, :])
        mg = (sa * ya + sc * yc).astype(BF)
        mg_ref[...] = mg
        x1_ref[...] = x_ref[...] + _dot(mg, wo_ref[...], NN)

    tok = lambda: pl.BlockSpec((tm, D), lambda i: (i, 0))
    seg = lambda s: pl.BlockSpec((tm, D), lambda i: (i, s))
    halo = lambda s: pl.BlockSpec((16, D), lambda i: (jnp.maximum(i * (tm // 16) - 1, 0), s))
    wsp = lambda k: _resident((None, D, D), lambda i: (k, 0, 0))
    row = lambda n: _resident((n, D), lambda i: (0, 0))
    return _call(
        body, name=f"mixer_fwd{l}", grid=(T // tm,),
        in_specs=[tok(), seg(0), seg(1), seg(2), seg(3), seg(4), halo(1), halo(2),
                  pl.BlockSpec((D, tm), lambda i: (0, i)), wsp(0), wsp(1), wsp(2), row(2), row(8), row(1)],
        out_specs=[tok()] * 6,
        out_shape=[jax.ShapeDtypeStruct((T, D), dt) for dt in (F32, BF, BF, BF, BF, BF)],
        args=(x, gt, gt, gt, gt, gt, gt, gt, att, w3, w3, w3, bg, cw, cbias), side=side)


def _mlp_fwd(x1, g, w2, l, tm, side=None):
    T = x1.shape[0]
    FC = 1024

    def body(x_ref, g_ref, wup_ref, wdn_ref, x2_ref, h_ref, a_ref):
        x = x_ref[...]
        h = _rms(x, g_ref[...]).astype(BF)
        h_ref[...] = h
        acc = x
        for c in range(F // FC):
            a = _dot(h, wup_ref[c * FC:(c + 1) * FC, :], NT)
            a_ref[:, c * FC:(c + 1) * FC] = a.astype(BF)
            u = jnp.maximum(a, 0.0)
            acc = acc + _dot((u * u).astype(BF), wdn_ref[c * FC:(c + 1) * FC, :], NN)
        x2_ref[...] = acc

    tok = lambda w: pl.BlockSpec((tm, w), lambda i: (i, 0))
    wsp = lambda k: _resident((None, F, D), lambda i: (k, 0, 0))
    return _call(
        body, name=f"mlp_fwd{l}", grid=(T // tm,),
        in_specs=[tok(D), ROW1(), wsp(0), wsp(1)],
        out_specs=[tok(D), tok(D), tok(F)],
        out_shape=[jax.ShapeDtypeStruct((T, D), F32), jax.ShapeDtypeStruct((T, D), BF),
                   jax.ShapeDtypeStruct((T, F), BF)],
        args=(x1, g, w2, w2), side=side)


def _loss_head(x, target, g, tm):
    T = x.shape[0]
    nt = T // tm

    def body(x_ref, t_ref, g_ref, dx_ref, acc_ref):
        _zero_at_first_step(acc_ref)
        x = x_ref[...]
        gv = g_ref[...]
        err = _rms(x, gv) - t_ref[...]
        dx, dg = _rms_bwd(err * (1.0 / D), x, gv)
        dx_ref[...] = dx
        acc_ref[0:1, :] += dg
        acc_ref[1:2, :] += jnp.sum(err * err, axis=0, keepdims=True)

        @pl.when(pl.program_id(0) == nt - 1)
        def _():
            acc_ref[1:2, :] = jnp.zeros((1, D), F32) + (0.5 / D) * jnp.sum(acc_ref[1:2, :])

    tok = lambda: pl.BlockSpec((tm, D), lambda i: (i, 0))
    (dx, acc), _ = _call(
        body, name="loss_head", grid=(nt,), in_specs=[tok(), tok(), ROW1()], out_specs=[tok(), ACC()],
        out_shape=[jax.ShapeDtypeStruct((T, D), F32), jax.ShapeDtypeStruct((8, D), F32)], args=(x, target, g))
    return dx, acc


def _mlp_bwd(dx2, x1, a, g, w2, l, tm, side=None):
    T = dx2.shape[0]
    FC = 1024

    def body(d_ref, x_ref, a_ref, g_ref, wup_ref, wdn_ref, da_ref, dx1_ref, db_ref, acc_ref):
        _zero_at_first_step(acc_ref)
        d = d_ref[...]
        db = d.astype(BF)
        db_ref[...] = db
        dh = jnp.zeros((tm, D), F32)
        for c in range(F // FC):
            du = _dot(db, wdn_ref[c * FC:(c + 1) * FC, :], NT)
            da = (du * (2.0 * jnp.maximum(a_ref[:, c * FC:(c + 1) * FC].astype(F32), 0.0))).astype(BF)
            da_ref[:, c * FC:(c + 1) * FC] = da
            dh = dh + _dot(da, wup_ref[c * FC:(c + 1) * FC, :], NN)
        dx, dg = _rms_bwd(dh, x_ref[...], g_ref[...])
        dx1_ref[...] = d + dx
        acc_ref[0:1, :] += dg

    tok = lambda w: pl.BlockSpec((tm, w), lambda i: (i, 0))
    wsp = lambda k: _resident((None, F, D), lambda i: (k, 0, 0))
    return _call(
        body, name=f"mlp_bwd{l}", grid=(T // tm,),
        in_specs=[tok(D), tok(D), tok(F), ROW1(), wsp(0), wsp(1)],
        out_specs=[tok(F), tok(D), tok(D), ACC()],
        out_shape=[jax.ShapeDtypeStruct((T, F), BF), jax.ShapeDtypeStruct((T, D), F32),
                   jax.ShapeDtypeStruct((T, D), BF), jax.ShapeDtypeStruct((8, D), F32)],
        args=(dx2, x1, a, g, w2, w2), side=side)


def _mixer_bwd(dx1, gt, ya, yc, zb, w3, bg, cw, l, S, tm, side=None):
    T = dx1.shape[0]
    nt = T // tm

    def body(d_ref, cb_ref, cc_ref, cu_ref, ga_ref, gc_ref, ya_ref, yc_ref, zb_ref, wao_ref, wco_ref, wo_ref,
             bg_ref, cw_ref, dg_ref, datt_ref, dya_ref, dyc_ref, db_ref, acc_ref, carry_ref):
        ti = nt - 1 - pl.program_id(0)
        _zero_at_first_step(acc_ref)

        @pl.when(((ti + 1) * tm) % S == 0)
        def _():
            carry_ref[...] = jnp.zeros_like(carry_ref)

        db = d_ref[...].astype(BF)
        db_ref[...] = db
        dm = _dot(db, wo_ref[...], NT)
        sa = jax.nn.sigmoid(ga_ref[...].astype(F32) + bg_ref[0:1, :])
        sc = jax.nn.sigmoid(gc_ref[...].astype(F32) + bg_ref[1:2, :])
        dya = (dm * sa).astype(BF)
        dyc = (dm * sc).astype(BF)
        dya_ref[...] = dya
        dyc_ref[...] = dyc
        dga = dm * ya_ref[...].astype(F32) * sa * (1.0 - sa)
        dgc = dm * yc_ref[...].astype(F32) * sc * (1.0 - sc)
        dg_ref[:, 3 * D:4 * D] = dga.astype(BF)
        dg_ref[:, 4 * D:5 * D] = dgc.astype(BF)
        acc_ref[0:1, :] += jnp.sum(dga, axis=0, keepdims=True)
        acc_ref[1:2, :] += jnp.sum(dgc, axis=0, keepdims=True)
        datt_ref[...] = _dot(wao_ref[...], dya, NT).astype(BF)
        dco = _dot(dyc, wco_ref[...], NT)

        cc = cc_ref[...].astype(F32)
        cu = cu_ref[...].astype(F32)
        y = cc * cu
        dg_ref[:, 0:D] = (dco * zb_ref[...].astype(F32)).astype(BF)
        dz = dco * cb_ref[...].astype(F32)
        u1 = _shift_rows(dz, 1, [carry_ref[0:1, :]], False)
        u2 = _shift_rows(dz, 2, [carry_ref[0:1, :], carry_ref[1:2, :]], False)
        acc_ref[2:3, :] += jnp.sum(dz, axis=0, keepdims=True)
        acc_ref[3:4, :] += jnp.sum(u2 * y, axis=0, keepdims=True)
        acc_ref[4:5, :] += jnp.sum(u1 * y, axis=0, keepdims=True)
        acc_ref[5:6, :] += jnp.sum(dz * y, axis=0, keepdims=True)
        dy = cw_ref[2:3, :] * dz + cw_ref[1:2, :] * u1 + cw_ref[0:1, :] * u2
        dg_ref[:, D:2 * D] = (dy * cu).astype(BF)
        dg_ref[:, 2 * D:3 * D] = (dy * cc).astype(BF)
        carry_ref[...] = dz[0:8, :]

    tok = lambda w=D: pl.BlockSpec((tm, w), lambda i: (nt - 1 - i, 0))
    seg = lambda s: pl.BlockSpec((tm, D), lambda i: (nt - 1 - i, s))
    wsp = lambda k: _resident((None, D, D), lambda i: (k, 0, 0))
    row = lambda n: _resident((n, D), lambda i: (0, 0))
    return _call(
        body, name=f"mixer_bwd{l}", grid=(nt,),
        in_specs=[tok(), seg(0), seg(1), seg(2), seg(3), seg(4), tok(), tok(), tok(), wsp(0), wsp(1), wsp(2),
                  row(2), row(8)],
        out_specs=[tok(NG), pl.BlockSpec((D, tm), lambda i: (0, nt - 1 - i)), tok(), tok(), tok(), ACC()],
        out_shape=[jax.ShapeDtypeStruct((T, NG), BF), jax.ShapeDtypeStruct((D, T), BF)]
        + [jax.ShapeDtypeStruct((T, D), BF)] * 3 + [jax.ShapeDtypeStruct((8, D), F32)],
        scratch_shapes=[pltpu.VMEM((8, D), F32)],
        args=(dx1, gt, gt, gt, gt, gt, ya, yc, zb, w3, w3, w3, bg, cw), side=side)


def _attn_bwd(q, k, v, att, datt, sm, l, S, side=None):
    T = q.shape[1]
    nblk = S // WIN
    unroll = next(u for u in (5, 3, 1) if (nblk - 1) % u == 0)
    scale = HD ** -0.5

    def body(sm_ref, q_ref, k_ref, v_ref, o_ref, do_ref, dq_ref, dk_ref, dv_ref, ds_ref, dka_ref, dva_ref):
        pj = pl.program_id(1)
        biases = _pair_biases(sm_ref, pj)
        dka_ref[...] = jnp.zeros_like(dka_ref)
        dva_ref[...] = jnp.zeros_like(dva_ref)

        def block(i, first, dsinks):
            r0 = 0 if first else pl.multiple_of(i * WIN, WIN)
            p0 = 0 if first else pl.multiple_of(i * WIN - WIN, WIN)
            out = []
            for kvh in range(2):
                head0 = (2 * pj + kvh) * GQ
                rows = slice(kvh * HD, (kvh + 1) * HD)
                qt, kt, heads = _band_probs(sm_ref, head0, q_ref, k_ref, r0, p0, kvh, biases[first])
                inv = jnp.concatenate([h[2] for h in heads], axis=1)
                dos = _heads_on_lanes(do_ref, kvh, r0).astype(F32) * inv
                delta = jnp.sum(dos * _heads_on_lanes(o_ref, kvh, r0).astype(F32), axis=0, keepdims=True)
                dosb = dos.astype(BF)
                dpt = _dot(v_ref[rows, pl.ds(p0, 2 * WIN)], dosb, TN)
                dst = jnp.concatenate(
                    [(p * (dpt[:, g * WIN:(g + 1) * WIN] - delta[:, g * WIN:(g + 1) * WIN])).astype(BF)
                     for g, (p, _, _) in enumerate(heads)], axis=1)
                pt = jnp.concatenate([p.astype(BF) for p, _, _ in heads], axis=1)
                dqt = _dot(kt, dst, NN) * scale
                for g in range(GQ):
                    hr = slice((kvh * GQ + g) * HD, (kvh * GQ + g + 1) * HD)
                    dq_ref[hr, pl.ds(r0, WIN)] = dqt[:, g * WIN:(g + 1) * WIN].astype(BF)
                dka_ref[rows, pl.ds(p0, 2 * WIN)] += _dot(qt, dst, NT)
                dva_ref[rows, pl.ds(p0, 2 * WIN)] += _dot(dosb, pt, NT)
                ps = jnp.concatenate([h[1] for h in heads], axis=1)
                out.append(dsinks[kvh] - ps * delta)
            return tuple(out)

        zero = jnp.zeros((1, GQ * WIN), F32)
        def rest(t, c):
            for u in range(unroll):
                c = block(1 + unroll * t + u, False, c)
            return c

        dsinks = lax.fori_loop(0, (nblk - 1) // unroll, rest, block(0, True, (zero, zero)))
        for kvh in range(2):
            for g in range(GQ):
                tot = jnp.sum(dsinks[kvh][:, g * WIN:(g + 1) * WIN])
                ds_ref[kvh * GQ + g:kvh * GQ + g + 1, :] = jnp.zeros((1, 128), F32) + tot
        dk_ref[...] = dka_ref[...].astype(BF)
        dv_ref[...] = dva_ref[...].astype(BF)

    wide = lambda: pl.BlockSpec((2 * GQ * HD, S), lambda s, p: (p, s))
    narrow = lambda: pl.BlockSpec((2 * HD, S), lambda s, p: (p, s))
    return _call(
        body, name=f"attn_bwd{l}", grid=(T // S, 2),
        in_specs=[pl.BlockSpec(memory_space=pltpu.SMEM), wide(), narrow(), narrow(), wide(), wide()],
        out_specs=[wide(), narrow(), narrow(), pl.BlockSpec((None, None, 8, 128), lambda s, p: (s, p, 0, 0))],
        out_shape=[jax.ShapeDtypeStruct((NQ, T), BF), jax.ShapeDtypeStruct((NKV, T), BF),
                   jax.ShapeDtypeStruct((NKV, T), BF), jax.ShapeDtypeStruct((T // S, 2, 8, 128), F32)],
        scratch_shapes=[pltpu.VMEM((2 * HD, S), F32), pltpu.VMEM((2 * HD, S), F32)],
        args=(sm, q, k, v, att, datt), side=side)


def _inproj_bwd(dgt, dq, dk, dv, x, dres, g, wint, l, tm, side=None):
    T = x.shape[0]

    def body(dg_ref, dq_ref, dk_ref, dv_ref, x_ref, dr_ref, g_ref, w_ref, dx_ref, acc_ref):
        _zero_at_first_step(acc_ref)
        dh = _dot(dq_ref[...], w_ref[0:NQ, :], TN)
        dh = dh + _dot(dk_ref[...], w_ref[NQ:NQ + NKV, :], TN)
        dh = dh + _dot(dv_ref[...], w_ref[NQ + NKV:NQ + 2 * NKV, :], TN)
        dh = dh + _dot(dg_ref[...], w_ref[NQ + 2 * NKV:NP, :], NN)
        dx, dg = _rms_bwd(dh, x_ref[...], g_ref[...])
        dx_ref[...] = dr_ref[...] + dx
        acc_ref[0:1, :] += dg

    tok = lambda w: pl.BlockSpec((tm, w), lambda i: (i, 0))
    feat = lambda w: pl.BlockSpec((w, tm), lambda i: (0, i))
    return _call(
        body, name=f"inproj_bwd{l}", grid=(T // tm,),
        in_specs=[tok(NG), feat(NQ), feat(NKV), feat(NKV), tok(D), tok(D), ROW1(), _resident((NP, D), lambda i: (0, 0))],
        out_specs=[tok(D), ACC()],
        out_shape=[jax.ShapeDtypeStruct((T, D), F32), jax.ShapeDtypeStruct((8, D), F32)],
        args=(dgt, dq, dk, dv, x, dres, g, wint), side=side)


def _wgrad(a, b, rows, row0, into, name, relu2=False, a_is_transposed=False):
    M, T = a.shape if a_is_transposed else a.shape[::-1]
    tmm = next(t for t in (1024, 512, 256) if M % t == 0 and row0 % t == 0)
    tk = min(2048, T)
    nk = T // tk
    blk0 = row0 // tmm

    def body(*refs):
        a_ref, b_ref = refs[0], refs[1]
        o_ref, acc_ref = refs[-2], refs[-1]
        kk = pl.program_id(1)

        @pl.when(kk == 0)
        def _():
            acc_ref[...] = jnp.zeros_like(acc_ref)

        av = a_ref[...]
        if relu2:
            t = jnp.maximum(av.astype(F32), 0.0)
            av = (t * t).astype(BF)
        acc_ref[...] += _dot(av, b_ref[...], NN if a_is_transposed else TN)

        @pl.when(kk == nk - 1)
        def _():
            o_ref[...] = acc_ref[...].astype(BF)

    a_spec = pl.BlockSpec((tmm, tk), lambda j, kk: (j, kk)) if a_is_transposed else pl.BlockSpec((tk, tmm), lambda j, kk: (kk, j))
    in_specs = [a_spec, pl.BlockSpec((tk, D), lambda j, kk: (kk, 0))]
    args = [a, b]
    if into is not None:
        in_specs.append(ANY)
        args.append(into)
    (out,), _ = _call(
        body, name=name, grid=(M // tmm, nk), in_specs=in_specs,
        out_specs=[pl.BlockSpec((tmm, D), lambda j, kk: (blk0 + j, 0))],
        out_shape=[jax.ShapeDtypeStruct((rows, D), BF)], scratch_shapes=[pltpu.VMEM((tmm, D), F32)],
        aliases={2: 0} if into is not None else None, args=args)
    return out


def _adamw(w, g, m, v):
    m = B1 * m + (1.0 - B1) * g
    v = B2 * v + (1.0 - B2) * (g * g)
    m_hat = m / (1.0 - B1 ** STEP)
    v_hat = v / (1.0 - B2 ** STEP)
    return -LR * (m_hat / (jnp.sqrt(v_hat) + AEPS) + WD * w), m, v


def _adam_sum(land, w, m, v, l, into, name):
    _, r, _ = land.shape
    tr = 208 if r % 208 == 0 else (256 if r % 256 == 0 else r)

    def body(land_ref, w_ref, m_ref, v_ref, *rest):
        g_ref, d_ref, nm_ref, nv_ref = rest[-4:]
        g = land_ref[0].astype(F32)
        for s in range(1, NDEV):
            g = g + land_ref[s].astype(F32)
        g_ref[...] = g
        d_ref[...], nm_ref[...], nv_ref[...] = _adamw(w_ref[...], g, m_ref[...], v_ref[...])

    blk = lambda: pl.BlockSpec((None, tr, D), lambda j: (l, j, 0))
    in_specs = [pl.BlockSpec((NDEV, tr, D), lambda j: (0, j, 0)), blk(), blk(), blk()]
    args = [land, w, m, v]
    aliases = None
    if into is not None:
        in_specs += [ANY] * 4
        args += list(into)
        aliases = {4 + t: t for t in range(4)}
    outs, _ = _call(body, name=name, grid=(r // tr,), in_specs=in_specs, out_specs=[blk()] * 4,
                    out_shape=[jax.ShapeDtypeStruct(w.shape, F32)] * 4, aliases=aliases, args=args)
    return outs


def _adam_small(land, w, m, v):
    def body(land_ref, w_ref, m_ref, v_ref, g_ref, d_ref, nm_ref, nv_ref):
        g = land_ref[0]
        for s in range(1, NDEV):
            g = g + land_ref[s]
        g_ref[...] = g
        d_ref[...], nm_ref[...], nv_ref[...] = _adamw(w_ref[...], g, m_ref[...], v_ref[...])

    return pl.pallas_call(
        body, name="adam_small", out_shape=[jax.ShapeDtypeStruct(w.shape, F32)] * 4,
        compiler_params=pltpu.CompilerParams(vmem_limit_bytes=VMEM_LIMIT))(land, w, m, v)


def _pack_small(g_mix, b_gates, sinks, conv_b, g_mlp, g_final, conv_w_rows, extra):
    sink_row = jnp.zeros((1, D), F32).at[0, :2 * 16].set(sinks.reshape(-1))
    return jnp.concatenate([g_mix, b_gates.reshape(4, D), sink_row, conv_b, g_mlp, g_final.reshape(1, D),
                            conv_w_rows, extra, jnp.zeros((SMALL_ROWS - 19, D), F32)], axis=0)


def kernel(x, g_mix, w_in, b_gates, sinks, w_attn_out, conv_w, conv_b, w_conv_out, w_o, g_mlp, w_up, w_down, g_final, loss_target, m_g_mix, m_w_in, m_b_gates, m_sinks, m_w_attn_out, m_conv_w, m_conv_b, m_w_conv_out, m_w_o, m_g_mlp, m_w_up, m_w_down, m_g_final, v_g_mix, v_w_in, v_b_gates, v_sinks, v_w_attn_out, v_conv_w, v_conv_b, v_w_conv_out, v_w_o, v_g_mlp, v_w_up, v_w_down, v_g_final):
    nseq, S, _ = x.shape
    T = nseq * S
    tm_in = min(512, S)
    tm = min(256, S)
    xi, yi, ci = _position()
    me = 4 * xi + 2 * yi + ci
    tr = lambda t: jnp.swapaxes(t, 1, 2)
    blocks = lambda t: t.reshape(NDEV, t.shape[0] // NDEV, D)

    win_t, wup_t = tr(w_in), tr(w_up)
    sh_win = [win_t[l].astype(BF)[None] for l in range(L)]
    sh_w3 = [jnp.stack([w_attn_out[l], w_conv_out[l], w_o[l]]).astype(BF) for l in range(L)]
    sh_w2 = [jnp.stack([wup_t[l], w_down[l]]).astype(BF) for l in range(L)]
    wint, w3, w2 = [None] * L, [None] * L, [None] * L
    wint0_g, cw_g = _remote_only(_Gather([sh_win[0], jnp.pad(conv_w, ((0, 0), (0, 5), (0, 0)))]), "gather_first")
    wint[0] = wint0_g.reshape(NP, D)
    cw = jnp.swapaxes(cw_g, 1, 2).reshape(L, 8, D)
    slopes = np.power(np.float32(2.0), -8.0 * np.arange(1, 17, dtype=np.float32) / 16).astype(np.float32)

    xf = x.reshape(T, D)
    saved = []
    cur = xf
    for l in range(L):
        sm = jnp.stack([sinks[l], jnp.asarray(slopes)])
        bg = b_gates[l].reshape(2, D)
        (gt, q, k, v, h), got = _inproj_fwd(cur, g_mix[l:l + 1], wint[l], l, tm_in, side=_Gather([sh_w3[l]]))
        w3[l] = got[0].reshape(3, D, D)
        (att,), got = _attn_fwd(q, k, v, sm, l, S, side=_Gather([sh_w2[l]]))
        w2[l] = got[0].reshape(2, F, D)
        (x1, mg, co, ya, yc, zb), got = _mixer_fwd(cur, gt, att, w3[l], bg, cw[l], conv_b[l:l + 1], l, S, tm,
                                                   side=_Gather([sh_win[l + 1]]) if l + 1 < L else None)
        if l + 1 < L:
            wint[l + 1] = got[0].reshape(NP, D)
        (x2, h2, a), _ = _mlp_fwd(x1, g_mlp[l:l + 1], w2[l], l, tm)
        saved.append(dict(x=cur, gt=gt, q=q, k=k, v=v, h=h, att=att, x1=x1, mg=mg, co=co, ya=ya, yc=yc, zb=zb, h2=h2, a=a,
                          sm=sm, bg=bg))
        cur = x2

    dcur, acc_loss = _loss_head(cur, loss_target.reshape(T, D), g_final.reshape(1, D), tm_in)

    masters = {"w_in": (win_t, tr(m_w_in), tr(v_w_in)), "w_attn_out": (w_attn_out, m_w_attn_out, v_w_attn_out),
               "w_conv_out": (w_conv_out, m_w_conv_out, v_w_conv_out), "w_o": (w_o, m_w_o, v_w_o),
               "w_up": (wup_t, tr(m_w_up), tr(v_w_up)), "w_down": (w_down, m_w_down, v_w_down)}
    big = {name: None for name in masters}

    def adam(name, land, l):
        big[name] = _adam_sum(land, *masters[name], l, big[name], f"adam_{name}{l}")

    acc_in, acc_mix, acc_mlp, dsink = [None] * L, [None] * L, [None] * L, [None] * L
    pending = None
    for l in reversed(range(L)):
        sv = saved[l]
        side = _Exchange([blocks(pending[0])]) if pending else None
        (da, dx1, dx2b, acc_mlp[l]), got = _mlp_bwd(dcur, sv["x1"], sv["a"], g_mlp[l:l + 1], w2[l], l, tm, side=side)
        if pending:
            adam("w_in", got[0], pending[1])
        d_wdn = _wgrad(sv["a"], dx2b, F, 0, None, f"wgrad_down{l}", relu2=True)
        d_wup = _wgrad(da, sv["h2"], F, 0, None, f"wgrad_up{l}")
        (dgt, datt, dya, dyc, dx1b, acc_mix[l]), got = _mixer_bwd(
            dx1, sv["gt"], sv["ya"], sv["yc"], sv["zb"], w3[l], sv["bg"], cw[l], l, S, tm,
            side=_Exchange([blocks(d_wdn)]))
        adam("w_down", got[0], l)
        d_wo = _wgrad(sv["mg"], dx1b, D, 0, None, f"wgrad_o{l}")
        d_wao = _wgrad(sv["att"], dya, D, 0, None, f"wgrad_attn_out{l}", a_is_transposed=True)
        d_wco = _wgrad(sv["co"], dyc, D, 0, None, f"wgrad_conv_out{l}")
        d_win = _wgrad(dgt, sv["h"], NP, NQ + 2 * NKV, None, f"wgrad_gates{l}")
        (dq, dk, dv, dsink[l]), got = _attn_bwd(
            sv["q"], sv["k"], sv["v"], sv["att"], datt, sv["sm"], l, S,
            side=_Exchange([blocks(d_wup), blocks(d_wo), blocks(d_wao), blocks(d_wco)]))
        for name, land in zip(["w_up", "w_o", "w_attn_out", "w_conv_out"], got):
            adam(name, land, l)
        d_win = _wgrad(dq, sv["h"], NP, 0, d_win, f"wgrad_q{l}", a_is_transposed=True)
        d_win = _wgrad(dk, sv["h"], NP, NQ, d_win, f"wgrad_k{l}", a_is_transposed=True)
        d_win = _wgrad(dv, sv["h"], NP, NQ + NKV, d_win, f"wgrad_v{l}", a_is_transposed=True)
        side = _Exchange([blocks(d_win)]) if l == 0 else None
        (dcur, acc_in[l]), got = _inproj_bwd(dgt, dq, dk, dv, sv["x"], dx1, g_mix[l:l + 1], wint[l], l, tm_in, side=side)
        if l == 0:
            adam("w_in", got[0], 0)
        else:
            pending = (d_win, l)

    dsinks = jnp.stack([dsink[l][:, :, :, 0].sum(axis=0).reshape(16) for l in range(L)])
    small = _pack_small(
        jnp.concatenate([acc_in[l][0:1] for l in range(L)]),
        jnp.stack([acc_mix[l][0:2].reshape(2 * D) for l in range(L)]),
        dsinks,
        jnp.concatenate([acc_mix[l][2:3] for l in range(L)]),
        jnp.concatenate([acc_mlp[l][0:1] for l in range(L)]),
        acc_loss[0],
        jnp.concatenate([acc_mix[l][3:6] for l in range(L)]),
        acc_loss[1:2])
    (small_land,) = _remote_only(_Exchange([], [small]), "exchange_small")

    def place(cw_shard):
        return lax.dynamic_update_slice(jnp.zeros((2 * 3, D), F32), cw_shard.reshape(2 * 3, D // NDEV), (0, me * (D // NDEV)))

    zero_row = jnp.zeros((1, D), F32)
    packs = [_pack_small(a, b, c_, d_, e, f, place(g_), zero_row) for a, b, c_, d_, e, f, g_ in
             [(g_mix, b_gates, sinks, conv_b, g_mlp, g_final, conv_w),
              (m_g_mix, m_b_gates, m_sinks, m_conv_b, m_g_mlp, m_g_final, m_conv_w),
              (v_g_mix, v_b_gates, v_sinks, v_conv_b, v_g_mlp, v_g_final, v_conv_w)]]
    small_out = _adam_small(small_land, *packs)

    def unpack(p):
        cwp = lax.dynamic_slice(p[12:18], (0, me * (D // NDEV)), (6, D // NDEV)).reshape(L, 3, D // NDEV)
        return dict(g_mix=p[0:2], b_gates=p[2:6].reshape(L, 2 * D), sinks=p[6, :32].reshape(L, 16), conv_b=p[7:9],
                    g_mlp=p[9:11], g_final=p[11], conv_w=cwp)

    small_kinds = [unpack(p) for p in small_out]
    for name in ("w_in", "w_up"):
        big[name] = [tr(o) for o in big[name]]
    order = ["g_mix", "w_in", "b_gates", "sinks", "w_attn_out", "conv_w", "conv_b", "w_conv_out", "w_o", "g_mlp",
             "w_up", "w_down", "g_final"]
    out = [small_out[0][18, 0], dcur.reshape(nseq, S, D)]
    for kind in range(4):
        for name in order:
            out.append(big[name][kind] if name in big else small_kinds[kind][name])
    return tuple(out)
```

```python
import numpy as np
import jax
import jax.numpy as jnp
from jax import lax
from jax.experimental import pallas as pl
from jax.experimental.pallas import tpu as pltpu

D = 1024
NG = 5 * D
NQ = 1024
NKV = 256
NP = NQ + 2 * NKV + NG
F = 4096
HD = 64
GQ = 4
WIN = 128
L = 2
NDEV = 8
EPS = 1e-6
NEG = -1e30
SMALL_ROWS = 24
LR, B1, B2, AEPS, WD, STEP = 0.001, 0.9, 0.999, 1e-08, 0.01, 10

BF = jnp.bfloat16
F32 = jnp.float32
MESH = pl.DeviceIdType.MESH
VMEM_LIMIT = 60 * 1024 * 1024
ANY = pl.BlockSpec(memory_space=pl.ANY)

NN = ((1,), (0,))
NT = ((1,), (1,))
TN = ((0,), (0,))


def _dot(a, b, dims):
    return lax.dot_general(a, b, (dims, ((), ())), preferred_element_type=F32)


def _resident(shape, imap):
    return pl.BlockSpec(shape, imap, pipeline_mode=pl.Buffered(1))


def _position():
    return lax.axis_index("x"), lax.axis_index("y"), lax.axis_index("c")


class _Gather:
    def __init__(self, shards):
        n = len(shards)
        self.inputs = list(shards)
        self.out_shape = [jax.ShapeDtypeStruct((s.shape[0], NDEV) + s.shape[1:], s.dtype) for s in shards]
        self.scratch = [pltpu.SemaphoreType.DMA((n, 7)), pltpu.SemaphoreType.DMA((n, 7)), pltpu.SemaphoreType.DMA((n,))]

    def _plan(self, src, dst, sems):
        send_sems, recv_sems, local_sems = sems
        n = len(src)
        x, y, c = _position()
        me, sibling = (x, y, c), (x, y, 1 - c)
        chips = [(1 - x, y), (x, 1 - y), (1 - x, 1 - y)]

        def rows(a, p):
            return dst[a].at[:, 4 * p[0] + 2 * p[1] + p[2]]

        def copy(a, k, block, to, from_src=False):
            return pltpu.make_async_remote_copy(
                src_ref=src[a] if from_src else rows(a, block), dst_ref=rows(a, block),
                send_sem=send_sems.at[a, k], recv_sem=recv_sems.at[a, k], device_id=to, device_id_type=MESH)

        mine = [pltpu.make_async_copy(src[a], rows(a, me), local_sems.at[a]) for a in range(n)]
        first = []
        for a in range(n):
            first.append(copy(a, 0, me, sibling, True))
            first += [copy(a, 1 + t, me, (*chip, c), True) for t, chip in enumerate(chips)]
        return n, c, me, sibling, chips, copy, mine, first

    def start(self, src, dst, sems):
        *_, mine, first = self._plan(src, dst, sems)
        for cp in mine + first:
            cp.start()

    def relay(self, src, dst, sems):
        n, c, me, sibling, chips, copy, _, _ = self._plan(src, dst, sems)
        for t, chip in enumerate(chips):
            for a in range(n):
                copy(a, 1 + t, (*chip, c), me).wait_recv()
                copy(a, 4 + t, (*chip, c), sibling).start()

    def finish(self, src, dst, sems):
        n, c, me, sibling, chips, copy, mine, first = self._plan(src, dst, sems)
        for a in range(n):
            copy(a, 0, sibling, me).wait_recv()
            for t, chip in enumerate(chips):
                copy(a, 4 + t, (*chip, 1 - c), me).wait_recv()
        for cp in first + [copy(a, 4 + t, (*chip, c), sibling) for t, chip in enumerate(chips) for a in range(n)]:
            cp.wait_send()
        for cp in mine:
            cp.wait()


class _Exchange:
    def __init__(self, grads, everyone=()):
        self.inputs = list(grads) + list(everyone)
        self.n_blocked = len(grads)
        n = len(self.inputs)
        self.out_shape = [jax.ShapeDtypeStruct(g.shape, g.dtype) for g in grads]
        self.out_shape += [jax.ShapeDtypeStruct((NDEV,) + e.shape, e.dtype) for e in everyone]
        self.scratch = [pltpu.SemaphoreType.DMA((n, 7)), pltpu.SemaphoreType.DMA((n, 7)), pltpu.SemaphoreType.DMA((n,))]

    def _plan(self, src, land, sems):
        send_sems, recv_sems, local_sems = sems
        x, y, c = _position()
        me = 4 * x + 2 * y + c

        def parts(peer_idx):
            return [(s.at[peer_idx] if a < self.n_blocked else s, land[a].at[me]) for a, s in enumerate(src)]

        local = [pltpu.make_async_copy(s, d, local_sems.at[a]) for a, (s, d) in enumerate(parts(me))]
        sent = []
        for rel in range(1, NDEV):
            px = 1 - x if rel & 4 else x
            py = 1 - y if rel & 2 else y
            pc = 1 - c if rel & 1 else c
            for a, (s, d) in enumerate(parts(4 * px + 2 * py + pc)):
                sent.append(pltpu.make_async_remote_copy(
                    src_ref=s, dst_ref=d, send_sem=send_sems.at[a, rel - 1], recv_sem=recv_sems.at[a, rel - 1],
                    device_id=(px, py, pc), device_id_type=MESH))
        return local, sent

    def start(self, src, land, sems):
        local, sent = self._plan(src, land, sems)
        for cp in local + sent:
            cp.start()

    def relay(self, src, land, sems):
        pass

    def finish(self, src, land, sems):
        local, sent = self._plan(src, land, sems)
        for cp in sent:
            cp.wait_recv()
        for cp in sent:
            cp.wait_send()
        for cp in local:
            cp.wait()


def _call(body, *, name, grid, in_specs, out_specs, out_shape, args, scratch_shapes=(), aliases=None, side=None,
          relay_at=1.0):
    sem = ("arbitrary",) * len(grid)
    if side is None:
        outs = pl.pallas_call(
            body, name=name, grid=grid, in_specs=in_specs, out_specs=out_specs, out_shape=out_shape,
            scratch_shapes=list(scratch_shapes), input_output_aliases=aliases or {},
            compiler_params=pltpu.CompilerParams(dimension_semantics=sem, vmem_limit_bytes=VMEM_LIMIT))(*args)
        return outs, []
    ni, no, ns = len(in_specs), len(out_specs), len(scratch_shapes)
    si, so = len(side.inputs), len(side.out_shape)

    def hosted(*refs):
        ins, refs = refs[:ni], refs[ni:]
        sins, refs = refs[:si], refs[si:]
        outs, refs = refs[:no], refs[no:]
        souts, refs = refs[:so], refs[so:]
        scr, sscr = refs[:ns], refs[ns:]
        step = pl.program_id(0)
        for d in range(1, len(grid)):
            step = step * grid[d] + pl.program_id(d)
        last = int(np.prod(grid)) - 1

        @pl.when(step == 0)
        def _():
            side.start(sins, souts, sscr)

        body(*ins, *outs, *scr)

        @pl.when(step == min(int(relay_at * last), last))
        def _():
            side.relay(sins, souts, sscr)

        @pl.when(step == last)
        def _():
            side.finish(sins, souts, sscr)

    outs = pl.pallas_call(
        hosted, name=name, grid=grid, in_specs=list(in_specs) + [ANY] * si, out_specs=list(out_specs) + [ANY] * so,
        out_shape=list(out_shape) + side.out_shape, scratch_shapes=list(scratch_shapes) + side.scratch,
        input_output_aliases=aliases or {},
        compiler_params=pltpu.CompilerParams(dimension_semantics=sem, vmem_limit_bytes=VMEM_LIMIT, has_side_effects=True),
    )(*args, *side.inputs)
    return outs[:no], outs[no:]


def _remote_only(side, name):
    n = len(side.inputs)

    def body(*refs):
        src, dst, sems = refs[:n], refs[n:n + len(side.out_shape)], refs[n + len(side.out_shape):]
        side.start(src, dst, sems)
        side.relay(src, dst, sems)
        side.finish(src, dst, sems)

    return pl.pallas_call(
        body, name=name, in_specs=[ANY] * n, out_specs=[ANY] * len(side.out_shape), out_shape=side.out_shape,
        scratch_shapes=side.scratch, compiler_params=pltpu.CompilerParams(has_side_effects=True))(*side.inputs)


def _rms(x, g):
    r = lax.rsqrt(jnp.mean(x * x, axis=-1, keepdims=True) + EPS)
    return x * r * g


def _rms_bwd(dy, x, g):
    r = lax.rsqrt(jnp.mean(x * x, axis=-1, keepdims=True) + EPS)
    xh = x * r
    dxh = dy * g
    dx = r * (dxh - xh * jnp.mean(dxh * xh, axis=-1, keepdims=True))
    return dx, jnp.sum(dy * xh, axis=0, keepdims=True)


def _zero_at_first_step(acc_ref):
    first = pl.program_id(0) == 0

    @pl.when(first)
    def _():
        acc_ref[...] = jnp.zeros_like(acc_ref)


ROW1 = lambda: _resident((1, D), lambda i: (0, 0))
ACC = lambda: pl.BlockSpec((8, D), lambda i: (0, 0))


def _inproj_fwd(x, g, wint, l, tm, side=None, relay_at=1.0):
    T = x.shape[0]

    def body(x_ref, g_ref, w_ref, gt_ref, q_ref, k_ref, v_ref, h_ref):
        h = _rms(x_ref[...], g_ref[...]).astype(BF)
        h_ref[...] = h
        q_ref[...] = _dot(w_ref[0:NQ, :], h, NT).astype(BF)
        k_ref[...] = _dot(w_ref[NQ:NQ + NKV, :], h, NT).astype(BF)
        v_ref[...] = _dot(w_ref[NQ + NKV:NQ + 2 * NKV, :], h, NT).astype(BF)
        for s in range(5):
            lo = NQ + 2 * NKV + s * D
            gt_ref[:, s * D:(s + 1) * D] = _dot(h, w_ref[lo:lo + D, :], NT).astype(BF)

    tok = lambda w: pl.BlockSpec((tm, w), lambda i: (i, 0))
    feat = lambda w: pl.BlockSpec((w, tm), lambda i: (0, i))
    return _call(
        body, name=f"inproj_fwd{l}", grid=(T // tm,),
        in_specs=[tok(D), ROW1(), _resident((NP, D), lambda i: (0, 0))],
        out_specs=[tok(NG), feat(NQ), feat(NKV), feat(NKV), tok(D)],
        out_shape=[jax.ShapeDtypeStruct((T, NG), BF)] + [jax.ShapeDtypeStruct((w, T), BF) for w in (NQ, NKV, NKV)]
        + [jax.ShapeDtypeStruct((T, D), BF)],
        args=(x, g, wint), side=side, relay_at=relay_at)


def _band_geometry():
    j = lax.broadcasted_iota(jnp.int32, (2 * WIN, WIN), 0)
    r = lax.broadcasted_iota(jnp.int32, (2 * WIN, WIN), 1)
    dist = WIN + r - j
    dist0 = r - j
    return (dist.astype(F32), (dist >= 0) & (dist < WIN)), (dist0.astype(F32), dist0 >= 0)


def _pair_biases(sm_ref, pj):
    return [[jnp.where(ok, -sm_ref[1, pj * 2 * GQ + h] * dist, NEG) for h in range(2 * GQ)]
            for dist, ok in _band_geometry()]


def _reduce_rows(x, pair, whole):
    while x.shape[0] > 8:
        half = x.shape[0] // 2
        x = pair(x[:half], x[half:])
    return whole(x, axis=0, keepdims=True)


def _heads_on_lanes(ref, kvh, r0):
    return jnp.concatenate([ref[(kvh * GQ + g) * HD:(kvh * GQ + g + 1) * HD, pl.ds(r0, WIN)] for g in range(GQ)], axis=1)


def _band_probs(sm_ref, head0, q_ref, k_ref, r0, p0, kvh, biases):
    qt = _heads_on_lanes(q_ref, kvh, r0) * jnp.asarray(HD ** -0.5, BF)
    kt = k_ref[kvh * HD:(kvh + 1) * HD, pl.ds(p0, 2 * WIN)]
    st = _dot(kt, qt, TN)
    heads = []
    for g in range(GQ):
        sink = sm_ref[0, head0 + g]
        s = st[:, g * WIN:(g + 1) * WIN] + biases[kvh * GQ + g]
        m = jnp.maximum(_reduce_rows(s, jnp.maximum, jnp.max), sink)
        p = jnp.exp(s - m)
        ps = jnp.exp(sink - m)
        heads.append((p, ps, 1.0 / (_reduce_rows(p, jnp.add, jnp.sum) + ps)))
    return qt, kt, heads


def _attn_fwd(q, k, v, sm, l, S, side=None, relay_at=1.0):
    T = q.shape[1]
    nblk = S // WIN
    unroll = next(u for u in (5, 3, 1) if (nblk - 1) % u == 0)

    def body(sm_ref, q_ref, k_ref, v_ref, o_ref):
        pj = pl.program_id(1)
        biases = _pair_biases(sm_ref, pj)

        def block(i, first):
            r0 = 0 if first else pl.multiple_of(i * WIN, WIN)
            p0 = 0 if first else pl.multiple_of(i * WIN - WIN, WIN)
            for kvh in range(2):
                head0 = (2 * pj + kvh) * GQ
                _, _, heads = _band_probs(sm_ref, head0, q_ref, k_ref, r0, p0, kvh, biases[first])
                pt = jnp.concatenate([p.astype(BF) for p, _, _ in heads], axis=1)
                ot = _dot(v_ref[kvh * HD:(kvh + 1) * HD, pl.ds(p0, 2 * WIN)], pt, NN)
                for g in range(GQ):
                    rows = slice((kvh * GQ + g) * HD, (kvh * GQ + g + 1) * HD)
                    o_ref[rows, pl.ds(r0, WIN)] = (ot[:, g * WIN:(g + 1) * WIN] * heads[g][2]).astype(BF)

        block(0, True)

        def rest(t, c):
            for u in range(unroll):
                block(1 + unroll * t + u, False)
            return c

        lax.fori_loop(0, (nblk - 1) // unroll, rest, 0)

    wide = lambda: pl.BlockSpec((2 * GQ * HD, S), lambda s, p: (p, s))
    narrow = lambda: pl.BlockSpec((2 * HD, S), lambda s, p: (p, s))
    return _call(
        body, name=f"attn_fwd{l}", grid=(T // S, 2),
        in_specs=[pl.BlockSpec(memory_space=pltpu.SMEM), wide(), narrow(), narrow()],
        out_specs=[wide()], out_shape=[jax.ShapeDtypeStruct((D, T), BF)],
        args=(sm, q, k, v), side=side, relay_at=relay_at)


def _shift_rows(y, k, edge_rows, down):
    n = y.shape[0]
    rid = lax.broadcasted_iota(jnp.int32, y.shape, 0)
    out = pltpu.roll(y, k if down else n - k, 0)
    for t, row in enumerate(edge_rows):
        out = jnp.where(rid == (t if down else n - k + t), row, out)
    return out


def _mixer_fwd(x, gt, att, w3, bg, cw, cbias, l, S, tm, side=None, relay_at=1.0):
    T = x.shape[0]

    def body(x_ref, cb_ref, cc_ref, cu_ref, ga_ref, gc_ref, cch_ref, cuh_ref, att_ref, wao_ref, wco_ref, wo_ref,
             bg_ref, cw_ref, cbias_ref, x1_ref, mg_ref, co_ref, ya_ref, yc_ref, zb_ref):
        first = (pl.program_id(0) * tm) % S == 0
        y = cc_ref[...].astype(F32) * cu_ref[...].astype(F32)
        hy1 = cch_ref[15:16, :].astype(F32) * cuh_ref[15:16, :].astype(F32)
        hy2 = cch_ref[14:15, :].astype(F32) * cuh_ref[14:15, :].astype(F32)
        hy1, hy2 = jnp.where(first, 0.0, hy1), jnp.where(first, 0.0, hy2)
        z = (cw_ref[0:1, :] * _shift_rows(y, 2, [hy2, hy1], True) + cw_ref[1:2, :] * _shift_rows(y, 1, [hy1], True)
             + cw_ref[2:3, :] * y)
        zb = z + cbias_ref[...]
        zb_ref[...] = zb.astype(BF)
        co = (cb_ref[...].astype(F32) * zb).astype(BF)
        co_ref[...] = co
        yc = _dot(co, wco_ref[...], NN)
        ya = _dot(att_ref[...], wao_ref[...], TN)
        ya_ref[...] = ya.astype(BF)
        yc_ref[...] = yc.astype(BF)
        sa = jax.nn.sigmoid(ga_ref[...].astype(F32) + bg_ref[0:1, :])
        sc = jax.nn.sigmoid(gc_ref[...].astype(F32) + bg_ref[1:2, :])
        mg = (sa * ya + sc * yc).astype(BF)
        mg_ref[...] = mg
        x1_ref[...] = x_ref[...] + _dot(mg, wo_ref[...], NN)

    tok = lambda: pl.BlockSpec((tm, D), lambda i: (i, 0))
    seg = lambda s: pl.BlockSpec((tm, D), lambda i: (i, s))
    halo = lambda s: pl.BlockSpec((16, D), lambda i: (jnp.maximum(i * (tm // 16) - 1, 0), s))
    wsp = lambda k: _resident((None, D, D), lambda i: (k, 0, 0))
    row = lambda n: _resident((n, D), lambda i: (0, 0))
    return _call(
        body, name=f"mixer_fwd{l}", grid=(T // tm,),
        in_specs=[tok(), seg(0), seg(1), seg(2), seg(3), seg(4), halo(1), halo(2),
                  pl.BlockSpec((D, tm), lambda i: (0, i)), wsp(0), wsp(1), wsp(2), row(2), row(8), row(1)],
        out_specs=[tok()] * 6,
        out_shape=[jax.ShapeDtypeStruct((T, D), dt) for dt in (F32, BF, BF, BF, BF, BF)],
        args=(x, gt, gt, gt, gt, gt, gt, gt, att, w3, w3, w3, bg, cw, cbias), side=side, relay_at=relay_at)


def _mlp_fwd(x1, g, w2, l, tm, side=None):
    T = x1.shape[0]
    FC = 1024

    def body(x_ref, g_ref, wup_ref, wdn_ref, x2_ref, h_ref, a_ref):
        x = x_ref[...]
        h = _rms(x, g_ref[...]).astype(BF)
        h_ref[...] = h
        acc = x
        for c in range(F // FC):
            a = _dot(h, wup_ref[c * FC:(c + 1) * FC, :], NT)
            a_ref[:, c * FC:(c + 1) * FC] = a.astype(BF)
            u = jnp.maximum(a, 0.0)
            acc = acc + _dot((u * u).astype(BF), wdn_ref[c * FC:(c + 1) * FC, :], NN)
        x2_ref[...] = acc

    tok = lambda w: pl.BlockSpec((tm, w), lambda i: (i, 0))
    wsp = lambda k: _resident((None, F, D), lambda i: (k, 0, 0))
    return _call(
        body, name=f"mlp_fwd{l}", grid=(T // tm,),
        in_specs=[tok(D), ROW1(), wsp(0), wsp(1)],
        out_specs=[tok(D), tok(D), tok(F)],
        out_shape=[jax.ShapeDtypeStruct((T, D), F32), jax.ShapeDtypeStruct((T, D), BF),
                   jax.ShapeDtypeStruct((T, F), BF)],
        args=(x1, g, w2, w2), side=side)


def _loss_head(x, target, g, tm):
    T = x.shape[0]
    nt = T // tm

    def body(x_ref, t_ref, g_ref, dx_ref, acc_ref):
        _zero_at_first_step(acc_ref)
        x = x_ref[...]
        gv = g_ref[...]
        err = _rms(x, gv) - t_ref[...]
        dx, dg = _rms_bwd(err * (1.0 / D), x, gv)
        dx_ref[...] = dx
        acc_ref[0:1, :] += dg
        acc_ref[1:2, :] += jnp.sum(err * err, axis=0, keepdims=True)

        @pl.when(pl.program_id(0) == nt - 1)
        def _():
            acc_ref[1:2, :] = jnp.zeros((1, D), F32) + (0.5 / D) * jnp.sum(acc_ref[1:2, :])

    tok = lambda: pl.BlockSpec((tm, D), lambda i: (i, 0))
    (dx, acc), _ = _call(
        body, name="loss_head", grid=(nt,), in_specs=[tok(), tok(), ROW1()], out_specs=[tok(), ACC()],
        out_shape=[jax.ShapeDtypeStruct((T, D), F32), jax.ShapeDtypeStruct((8, D), F32)], args=(x, target, g))
    return dx, acc


def _mlp_bwd(dx2, x1, a, g, w2, l, tm, side=None):
    T = dx2.shape[0]
    FC = 1024

    def body(d_ref, x_ref, a_ref, g_ref, wup_ref, wdn_ref, da_ref, dx1_ref, db_ref, acc_ref):
        _zero_at_first_step(acc_ref)
        d = d_ref[...]
        db = d.astype(BF)
        db_ref[...] = db
        dh = jnp.zeros((tm, D), F32)
        for c in range(F // FC):
            du = _dot(db, wdn_ref[c * FC:(c + 1) * FC, :], NT)
            da = (du * (2.0 * jnp.maximum(a_ref[:, c * FC:(c + 1) * FC].astype(F32), 0.0))).astype(BF)
            da_ref[:, c * FC:(c + 1) * FC] = da
            dh = dh + _dot(da, wup_ref[c * FC:(c + 1) * FC, :], NN)
        dx, dg = _rms_bwd(dh, x_ref[...], g_ref[...])
        dx1_ref[...] = d + dx
        acc_ref[0:1, :] += dg

    tok = lambda w: pl.BlockSpec((tm, w), lambda i: (i, 0))
    wsp = lambda k: _resident((None, F, D), lambda i: (k, 0, 0))
    return _call(
        body, name=f"mlp_bwd{l}", grid=(T // tm,),
        in_specs=[tok(D), tok(D), tok(F), ROW1(), wsp(0), wsp(1)],
        out_specs=[tok(F), tok(D), tok(D), ACC()],
        out_shape=[jax.ShapeDtypeStruct((T, F), BF), jax.ShapeDtypeStruct((T, D), F32),
                   jax.ShapeDtypeStruct((T, D), BF), jax.ShapeDtypeStruct((8, D), F32)],
        args=(dx2, x1, a, g, w2, w2), side=side)


def _mixer_bwd(dx1, gt, ya, yc, zb, w3, bg, cw, l, S, tm, side=None):
    T = dx1.shape[0]
    nt = T // tm

    def body(d_ref, cb_ref, cc_ref, cu_ref, ga_ref, gc_ref, ya_ref, yc_ref, zb_ref, wao_ref, wco_ref, wo_ref,
             bg_ref, cw_ref, dg_ref, datt_ref, dya_ref, dyc_ref, db_ref, acc_ref, carry_ref):
        ti = nt - 1 - pl.program_id(0)
        _zero_at_first_step(acc_ref)

        @pl.when(((ti + 1) * tm) % S == 0)
        def _():
            carry_ref[...] = jnp.zeros_like(carry_ref)

        db = d_ref[...].astype(BF)
        db_ref[...] = db
        dm = _dot(db, wo_ref[...], NT)
        sa = jax.nn.sigmoid(ga_ref[...].astype(F32) + bg_ref[0:1, :])
        sc = jax.nn.sigmoid(gc_ref[...].astype(F32) + bg_ref[1:2, :])
        dya = (dm * sa).astype(BF)
        dyc = (dm * sc).astype(BF)
        dya_ref[...] = dya
        dyc_ref[...] = dyc
        dga = dm * ya_ref[...].astype(F32) * sa * (1.0 - sa)
        dgc = dm * yc_ref[...].astype(F32) * sc * (1.0 - sc)
        dg_ref[:, 3 * D:4 * D] = dga.astype(BF)
        dg_ref[:, 4 * D:5 * D] = dgc.astype(BF)
        acc_ref[0:1, :] += jnp.sum(dga, axis=0, keepdims=True)
        acc_ref[1:2, :] += jnp.sum(dgc, axis=0, keepdims=True)
        datt_ref[...] = _dot(wao_ref[...], dya, NT).astype(BF)
        dco = _dot(dyc, wco_ref[...], NT)

        cc = cc_ref[...].astype(F32)
        cu = cu_ref[...].astype(F32)
        y = cc * cu
        dg_ref[:, 0:D] = (dco * zb_ref[...].astype(F32)).astype(BF)
        dz = dco * cb_ref[...].astype(F32)
        u1 = _shift_rows(dz, 1, [carry_ref[0:1, :]], False)
        u2 = _shift_rows(dz, 2, [carry_ref[0:1, :], carry_ref[1:2, :]], False)
        acc_ref[2:3, :] += jnp.sum(dz, axis=0, keepdims=True)
        acc_ref[3:4, :] += jnp.sum(u2 * y, axis=0, keepdims=True)
        acc_ref[4:5, :] += jnp.sum(u1 * y, axis=0, keepdims=True)
        acc_ref[5:6, :] += jnp.sum(dz * y, axis=0, keepdims=True)
        dy = cw_ref[2:3, :] * dz + cw_ref[1:2, :] * u1 + cw_ref[0:1, :] * u2
        dg_ref[:, D:2 * D] = (dy * cu).astype(BF)
        dg_ref[:, 2 * D:3 * D] = (dy * cc).astype(BF)
        carry_ref[...] = dz[0:8, :]

    tok = lambda w=D: pl.BlockSpec((tm, w), lambda i: (nt - 1 - i, 0))
    seg = lambda s: pl.BlockSpec((tm, D), lambda i: (nt - 1 - i, s))
    wsp = lambda k: _resident((None, D, D), lambda i: (k, 0, 0))
    row = lambda n: _resident((n, D), lambda i: (0, 0))
    return _call(
        body, name=f"mixer_bwd{l}", grid=(nt,),
        in_specs=[tok(), seg(0), seg(1), seg(2), seg(3), seg(4), tok(), tok(), tok(), wsp(0), wsp(1), wsp(2),
                  row(2), row(8)],
        out_specs=[tok(NG), pl.BlockSpec((D, tm), lambda i: (0, nt - 1 - i)), tok(), tok(), tok(), ACC()],
        out_shape=[jax.ShapeDtypeStruct((T, NG), BF), jax.ShapeDtypeStruct((D, T), BF)]
        + [jax.ShapeDtypeStruct((T, D), BF)] * 3 + [jax.ShapeDtypeStruct((8, D), F32)],
        scratch_shapes=[pltpu.VMEM((8, D), F32)],
        args=(dx1, gt, gt, gt, gt, gt, ya, yc, zb, w3, w3, w3, bg, cw), side=side)


def _attn_bwd(q, k, v, att, datt, sm, l, S, side=None):
    T = q.shape[1]
    nblk = S // WIN
    unroll = next(u for u in (5, 3, 1) if (nblk - 1) % u == 0)
    scale = HD ** -0.5

    def body(sm_ref, q_ref, k_ref, v_ref, o_ref, do_ref, dq_ref, dk_ref, dv_ref, ds_ref, dka_ref, dva_ref):
        pj = pl.program_id(1)
        biases = _pair_biases(sm_ref, pj)
        dka_ref[...] = jnp.zeros_like(dka_ref)
        dva_ref[...] = jnp.zeros_like(dva_ref)

        def block(i, first, dsinks):
            r0 = 0 if first else pl.multiple_of(i * WIN, WIN)
            p0 = 0 if first else pl.multiple_of(i * WIN - WIN, WIN)
            out = []
            for kvh in range(2):
                head0 = (2 * pj + kvh) * GQ
                rows = slice(kvh * HD, (kvh + 1) * HD)
                qt, kt, heads = _band_probs(sm_ref, head0, q_ref, k_ref, r0, p0, kvh, biases[first])
                inv = jnp.concatenate([h[2] for h in heads], axis=1)
                dos = _heads_on_lanes(do_ref, kvh, r0).astype(F32) * inv
                delta = jnp.sum(dos * _heads_on_lanes(o_ref, kvh, r0).astype(F32), axis=0, keepdims=True)
                dosb = dos.astype(BF)
                dpt = _dot(v_ref[rows, pl.ds(p0, 2 * WIN)], dosb, TN)
                dst = jnp.concatenate(
                    [(p * (dpt[:, g * WIN:(g + 1) * WIN] - delta[:, g * WIN:(g + 1) * WIN])).astype(BF)
                     for g, (p, _, _) in enumerate(heads)], axis=1)
                pt = jnp.concatenate([p.astype(BF) for p, _, _ in heads], axis=1)
                dqt = _dot(kt, dst, NN) * scale
                for g in range(GQ):
                    hr = slice((kvh * GQ + g) * HD, (kvh * GQ + g + 1) * HD)
                    dq_ref[hr, pl.ds(r0, WIN)] = dqt[:, g * WIN:(g + 1) * WIN].astype(BF)
                dka_ref[rows, pl.ds(p0, 2 * WIN)] += _dot(qt, dst, NT)
                dva_ref[rows, pl.ds(p0, 2 * WIN)] += _dot(dosb, pt, NT)
                ps = jnp.concatenate([h[1] for h in heads], axis=1)
                out.append(dsinks[kvh] - ps * delta)
            return tuple(out)

        zero = jnp.zeros((1, GQ * WIN), F32)
        def rest(t, c):
            for u in range(unroll):
                c = block(1 + unroll * t + u, False, c)
            return c

        dsinks = lax.fori_loop(0, (nblk - 1) // unroll, rest, block(0, True, (zero, zero)))
        for kvh in range(2):
            for g in range(GQ):
                tot = jnp.sum(dsinks[kvh][:, g * WIN:(g + 1) * WIN])
                ds_ref[kvh * GQ + g:kvh * GQ + g + 1, :] = jnp.zeros((1, 128), F32) + tot
        dk_ref[...] = dka_ref[...].astype(BF)
        dv_ref[...] = dva_ref[...].astype(BF)

    wide = lambda: pl.BlockSpec((2 * GQ * HD, S), lambda s, p: (p, s))
    narrow = lambda: pl.BlockSpec((2 * HD, S), lambda s, p: (p, s))
    return _call(
        body, name=f"attn_bwd{l}", grid=(T // S, 2),
        in_specs=[pl.BlockSpec(memory_space=pltpu.SMEM), wide(), narrow(), narrow(), wide(), wide()],
        out_specs=[wide(), narrow(), narrow(), pl.BlockSpec((None, None, 8, 128), lambda s, p: (s, p, 0, 0))],
        out_shape=[jax.ShapeDtypeStruct((NQ, T), BF), jax.ShapeDtypeStruct((NKV, T), BF),
                   jax.ShapeDtypeStruct((NKV, T), BF), jax.ShapeDtypeStruct((T // S, 2, 8, 128), F32)],
        scratch_shapes=[pltpu.VMEM((2 * HD, S), F32), pltpu.VMEM((2 * HD, S), F32)],
        args=(sm, q, k, v, att, datt), side=side)


def _inproj_bwd(dgt, dq, dk, dv, x, dres, g, wint, l, tm, side=None):
    T = x.shape[0]

    def body(dg_ref, dq_ref, dk_ref, dv_ref, x_ref, dr_ref, g_ref, w_ref, dx_ref, acc_ref):
        _zero_at_first_step(acc_ref)
        dh = _dot(dq_ref[...], w_ref[0:NQ, :], TN)
        dh = dh + _dot(dk_ref[...], w_ref[NQ:NQ + NKV, :], TN)
        dh = dh + _dot(dv_ref[...], w_ref[NQ + NKV:NQ + 2 * NKV, :], TN)
        dh = dh + _dot(dg_ref[...], w_ref[NQ + 2 * NKV:NP, :], NN)
        dx, dg = _rms_bwd(dh, x_ref[...], g_ref[...])
        dx_ref[...] = dr_ref[...] + dx
        acc_ref[0:1, :] += dg

    tok = lambda w: pl.BlockSpec((tm, w), lambda i: (i, 0))
    feat = lambda w: pl.BlockSpec((w, tm), lambda i: (0, i))
    return _call(
        body, name=f"inproj_bwd{l}", grid=(T // tm,),
        in_specs=[tok(NG), feat(NQ), feat(NKV), feat(NKV), tok(D), tok(D), ROW1(), _resident((NP, D), lambda i: (0, 0))],
        out_specs=[tok(D), ACC()],
        out_shape=[jax.ShapeDtypeStruct((T, D), F32), jax.ShapeDtypeStruct((8, D), F32)],
        args=(dgt, dq, dk, dv, x, dres, g, wint), side=side)


def _wgrad(a, b, rows, row0, into, name, relu2=False, a_is_transposed=False):
    M, T = a.shape if a_is_transposed else a.shape[::-1]
    tmm = next(t for t in (1024, 512, 256) if M % t == 0 and row0 % t == 0)
    tk = min(2048 if relu2 else 4096, T)
    nk = T // tk
    blk0 = row0 // tmm

    def body(*refs):
        a_ref, b_ref = refs[0], refs[1]
        o_ref, acc_ref = refs[-2], refs[-1]
        kk = pl.program_id(1)

        @pl.when(kk == 0)
        def _():
            acc_ref[...] = jnp.zeros_like(acc_ref)

        av = a_ref[...]
        if relu2:
            t = jnp.maximum(av.astype(F32), 0.0)
            av = (t * t).astype(BF)
        acc_ref[...] += _dot(av, b_ref[...], NN if a_is_transposed else TN)

        @pl.when(kk == nk - 1)
        def _():
            o_ref[...] = acc_ref[...].astype(BF)

    a_spec = pl.BlockSpec((tmm, tk), lambda j, kk: (j, kk)) if a_is_transposed else pl.BlockSpec((tk, tmm), lambda j, kk: (kk, j))
    in_specs = [a_spec, pl.BlockSpec((tk, D), lambda j, kk: (kk, 0))]
    args = [a, b]
    if into is not None:
        in_specs.append(ANY)
        args.append(into)
    (out,), _ = _call(
        body, name=name, grid=(M // tmm, nk), in_specs=in_specs,
        out_specs=[pl.BlockSpec((tmm, D), lambda j, kk: (blk0 + j, 0))],
        out_shape=[jax.ShapeDtypeStruct((rows, D), BF)], scratch_shapes=[pltpu.VMEM((tmm, D), F32)],
        aliases={2: 0} if into is not None else None, args=args)
    return out


def _adamw(w, g, m, v):
    m = B1 * m + (1.0 - B1) * g
    v = B2 * v + (1.0 - B2) * (g * g)
    m_hat = m / (1.0 - B1 ** STEP)
    v_hat = v / (1.0 - B2 ** STEP)
    return -LR * (m_hat / (jnp.sqrt(v_hat) + AEPS) + WD * w), m, v


def _adam_sum(land, w, m, v, l, into, name):
    _, r, _ = land.shape
    tr = 208 if r % 208 == 0 else (256 if r % 256 == 0 else r)

    def body(land_ref, w_ref, m_ref, v_ref, *rest):
        g_ref, d_ref, nm_ref, nv_ref = rest[-4:]
        g = land_ref[0].astype(F32)
        for s in range(1, NDEV):
            g = g + land_ref[s].astype(F32)
        g_ref[...] = g
        d_ref[...], nm_ref[...], nv_ref[...] = _adamw(w_ref[...], g, m_ref[...], v_ref[...])

    blk = lambda: pl.BlockSpec((None, tr, D), lambda j: (l, j, 0))
    in_specs = [pl.BlockSpec((NDEV, tr, D), lambda j: (0, j, 0)), blk(), blk(), blk()]
    args = [land, w, m, v]
    aliases = None
    if into is not None:
        in_specs += [ANY] * 4
        args += list(into)
        aliases = {4 + t: t for t in range(4)}
    outs, _ = _call(body, name=name, grid=(r // tr,), in_specs=in_specs, out_specs=[blk()] * 4,
                    out_shape=[jax.ShapeDtypeStruct(w.shape, F32)] * 4, aliases=aliases, args=args)
    return outs


SMALL_NAMES = ("g_mix", "b_gates", "sinks", "conv_b", "g_mlp", "g_final", "conv_w")


def _adam_small(land, me, masters):
    n = len(SMALL_NAMES)
    lanes = D // NDEV

    def body(land_ref, me_ref, *refs):
        ins, outs, loss_ref, gs_ref = refs[:3 * n], refs[3 * n:7 * n], refs[7 * n], refs[7 * n + 1]
        g = land_ref[0]
        for s in range(1, NDEV):
            g = g + land_ref[s]
        gs_ref[...] = g

        def update(k, g_piece, idx):
            w_ref, m_ref, v_ref = ins[3 * k:3 * k + 3]
            outs[4 * k][idx] = g_piece
            outs[4 * k + 1][idx], outs[4 * k + 2][idx], outs[4 * k + 3][idx] = _adamw(w_ref[idx], g_piece, m_ref[idx], v_ref[idx])

        whole = (slice(None), slice(None))
        update(0, gs_ref[0:2, :], whole)
        for l in range(L):
            for h in range(2):
                update(1, gs_ref[2 + 2 * l + h:3 + 2 * l + h, :], (slice(l, l + 1), slice(h * D, (h + 1) * D)))
            update(2, gs_ref[6:7, 16 * l:16 * (l + 1)], (slice(l, l + 1), slice(None)))
        update(3, gs_ref[7:9, :], whole)
        update(4, gs_ref[9:11, :], whole)
        update(5, gs_ref[11:12, :], whole)
        mine = pl.ds(pl.multiple_of(me_ref[0] * lanes, lanes), lanes)
        for l in range(L):
            for k in range(3):
                update(6, gs_ref[12 + 3 * l + k:13 + 3 * l + k, mine], (l, slice(k, k + 1), slice(None)))
        loss_ref[...] = gs_ref[18:19, 0:1]

    flat = [t for name in SMALL_NAMES for t in masters[name]]
    vmem = pl.BlockSpec(memory_space=pltpu.VMEM)
    outs = pl.pallas_call(
        body, name="adam_small",
        in_specs=[vmem, pl.BlockSpec(memory_space=pltpu.SMEM)] + [vmem] * len(flat),
        out_shape=[jax.ShapeDtypeStruct(masters[name][0].shape, F32) for name in SMALL_NAMES for _ in range(4)]
        + [jax.ShapeDtypeStruct((1, 1), F32)],
        scratch_shapes=[pltpu.VMEM((SMALL_ROWS, D), F32)],
        compiler_params=pltpu.CompilerParams(vmem_limit_bytes=VMEM_LIMIT))(land, me, *flat)
    return {name: outs[4 * k:4 * k + 4] for k, name in enumerate(SMALL_NAMES)}, outs[-1]


def _pack_small(g_mix, b_gates, sinks, conv_b, g_mlp, g_final, conv_w_rows, extra):
    sink_row = jnp.zeros((1, D), F32).at[0, :2 * 16].set(sinks.reshape(-1))
    return jnp.concatenate([g_mix, b_gates.reshape(4, D), sink_row, conv_b, g_mlp, g_final.reshape(1, D),
                            conv_w_rows, extra, jnp.zeros((SMALL_ROWS - 19, D), F32)], axis=0)


def kernel(x, g_mix, w_in, b_gates, sinks, w_attn_out, conv_w, conv_b, w_conv_out, w_o, g_mlp, w_up, w_down, g_final, loss_target, m_g_mix, m_w_in, m_b_gates, m_sinks, m_w_attn_out, m_conv_w, m_conv_b, m_w_conv_out, m_w_o, m_g_mlp, m_w_up, m_w_down, m_g_final, v_g_mix, v_w_in, v_b_gates, v_sinks, v_w_attn_out, v_conv_w, v_conv_b, v_w_conv_out, v_w_o, v_g_mlp, v_w_up, v_w_down, v_g_final):
    nseq, S, _ = x.shape
    T = nseq * S
    tm_in = min(512, S)
    tm = min(256, S)
    xi, yi, ci = _position()
    me = 4 * xi + 2 * yi + ci
    tr = lambda t: jnp.swapaxes(t, 1, 2)
    blocks = lambda t: t.reshape(NDEV, t.shape[0] // NDEV, D)

    win_t, wup_t = tr(w_in), tr(w_up)
    sh_win = [win_t[l].astype(BF)[None] for l in range(L)]
    sh_w3 = [jnp.stack([w_attn_out[l], w_conv_out[l], w_o[l]]).astype(BF) for l in range(L)]
    sh_w2 = [jnp.stack([wup_t[l], w_down[l]]).astype(BF) for l in range(L)]
    wint, w3, w2 = [None] * L, [None] * L, [None] * L
    wint0_g, cw_g = _remote_only(_Gather([sh_win[0], jnp.pad(conv_w, ((0, 0), (0, 5), (0, 0)))]), "gather_first")
    wint[0] = wint0_g.reshape(NP, D)
    cw = jnp.swapaxes(cw_g, 1, 2).reshape(L, 8, D)
    slopes = np.power(np.float32(2.0), -8.0 * np.arange(1, 17, dtype=np.float32) / 16).astype(np.float32)

    xf = x.reshape(T, D)
    saved = []
    cur = xf
    for l in range(L):
        sm = jnp.stack([sinks[l], jnp.asarray(slopes)])
        bg = b_gates[l].reshape(2, D)
        (gt, q, k, v, h), got = _inproj_fwd(cur, g_mix[l:l + 1], wint[l], l, tm_in, side=_Gather([sh_w2[l]]),
                                            relay_at=0.85)
        w2[l] = got[0].reshape(2, F, D)
        (att,), got = _attn_fwd(q, k, v, sm, l, S, side=_Gather([sh_w3[l]]), relay_at=0.5)
        w3[l] = got[0].reshape(3, D, D)
        (x1, mg, co, ya, yc, zb), got = _mixer_fwd(cur, gt, att, w3[l], bg, cw[l], conv_b[l:l + 1], l, S, tm,
                                                   side=_Gather([sh_win[l + 1]]) if l + 1 < L else None, relay_at=0.85)
        if l + 1 < L:
            wint[l + 1] = got[0].reshape(NP, D)
        (x2, h2, a), _ = _mlp_fwd(x1, g_mlp[l:l + 1], w2[l], l, tm_in)
        saved.append(dict(x=cur, gt=gt, q=q, k=k, v=v, h=h, att=att, x1=x1, mg=mg, co=co, ya=ya, yc=yc, zb=zb, h2=h2, a=a,
                          sm=sm, bg=bg))
        cur = x2

    dcur, acc_loss = _loss_head(cur, loss_target.reshape(T, D), g_final.reshape(1, D), tm_in)

    masters = {"w_in": (win_t, tr(m_w_in), tr(v_w_in)), "w_attn_out": (w_attn_out, m_w_attn_out, v_w_attn_out),
               "w_conv_out": (w_conv_out, m_w_conv_out, v_w_conv_out), "w_o": (w_o, m_w_o, v_w_o),
               "w_up": (wup_t, tr(m_w_up), tr(v_w_up)), "w_down": (w_down, m_w_down, v_w_down)}
    big = {name: None for name in masters}

    def adam(name, land, l):
        big[name] = _adam_sum(land, *masters[name], l, big[name], f"adam_{name}{l}")

    acc_in, acc_mix, acc_mlp, dsink = [None] * L, [None] * L, [None] * L, [None] * L
    pending = None
    for l in reversed(range(L)):
        sv = saved[l]
        side = _Exchange([blocks(pending[0])]) if pending else None
        (da, dx1, dx2b, acc_mlp[l]), got = _mlp_bwd(dcur, sv["x1"], sv["a"], g_mlp[l:l + 1], w2[l], l, tm, side=side)
        if pending:
            adam("w_in", got[0], pending[1])
        d_wdn = _wgrad(sv["a"], dx2b, F, 0, None, f"wgrad_down{l}", relu2=True)
        d_wup = _wgrad(da, sv["h2"], F, 0, None, f"wgrad_up{l}")
        (dgt, datt, dya, dyc, dx1b, acc_mix[l]), got = _mixer_bwd(
            dx1, sv["gt"], sv["ya"], sv["yc"], sv["zb"], w3[l], sv["bg"], cw[l], l, S, tm,
            side=_Exchange([blocks(d_wdn)]))
        adam("w_down", got[0], l)
        d_wo = _wgrad(sv["mg"], dx1b, D, 0, None, f"wgrad_o{l}")
        d_wao = _wgrad(sv["att"], dya, D, 0, None, f"wgrad_attn_out{l}", a_is_transposed=True)
        d_wco = _wgrad(sv["co"], dyc, D, 0, None, f"wgrad_conv_out{l}")
        d_win = _wgrad(dgt, sv["h"], NP, NQ + 2 * NKV, None, f"wgrad_gates{l}")
        (dq, dk, dv, dsink[l]), got = _attn_bwd(
            sv["q"], sv["k"], sv["v"], sv["att"], datt, sv["sm"], l, S,
            side=_Exchange([blocks(d_wup), blocks(d_wo), blocks(d_wao), blocks(d_wco)]))
        for name, land in zip(["w_up", "w_o", "w_attn_out", "w_conv_out"], got):
            adam(name, land, l)
        d_win = _wgrad(dq, sv["h"], NP, 0, d_win, f"wgrad_q{l}", a_is_transposed=True)
        d_win = _wgrad(dk, sv["h"], NP, NQ, d_win, f"wgrad_k{l}", a_is_transposed=True)
        d_win = _wgrad(dv, sv["h"], NP, NQ + NKV, d_win, f"wgrad_v{l}", a_is_transposed=True)
        side = _Exchange([blocks(d_win)]) if l == 0 else None
        (dcur, acc_in[l]), got = _inproj_bwd(dgt, dq, dk, dv, sv["x"], dx1, g_mix[l:l + 1], wint[l], l, tm_in, side=side)
        if l == 0:
            adam("w_in", got[0], 0)
        else:
            pending = (d_win, l)

    dsinks = jnp.stack([dsink[l][:, :, :, 0].sum(axis=0).reshape(16) for l in range(L)])
    small = _pack_small(
        jnp.concatenate([acc_in[l][0:1] for l in range(L)]),
        jnp.stack([acc_mix[l][0:2].reshape(2 * D) for l in range(L)]),
        dsinks,
        jnp.concatenate([acc_mix[l][2:3] for l in range(L)]),
        jnp.concatenate([acc_mlp[l][0:1] for l in range(L)]),
        acc_loss[0],
        jnp.concatenate([acc_mix[l][3:6] for l in range(L)]),
        acc_loss[1:2])
    (small_land,) = _remote_only(_Exchange([], [small]), "exchange_small")

    row = lambda t: t.reshape(1, D)
    small_out, loss = _adam_small(small_land, me.reshape(1).astype(jnp.int32), {
        "g_mix": (g_mix, m_g_mix, v_g_mix), "b_gates": (b_gates, m_b_gates, v_b_gates), "sinks": (sinks, m_sinks, v_sinks),
        "conv_b": (conv_b, m_conv_b, v_conv_b), "g_mlp": (g_mlp, m_g_mlp, v_g_mlp),
        "g_final": (row(g_final), row(m_g_final), row(v_g_final)), "conv_w": (conv_w, m_conv_w, v_conv_w)})
    small_out["g_final"] = [t.reshape(D) for t in small_out["g_final"]]
    for name in ("w_in", "w_up"):
        big[name] = [tr(o) for o in big[name]]
    order = ["g_mix", "w_in", "b_gates", "sinks", "w_attn_out", "conv_w", "conv_b", "w_conv_out", "w_o", "g_mlp",
             "w_up", "w_down", "g_final"]
    out = [loss.reshape(()), dcur.reshape(nseq, S, D)]
    for kind in range(4):
        for name in order:
            out.append(big[name][kind] if name in big else small_out[name][kind])
    return tuple(out)
```

```python
import numpy as np
import jax
import jax.numpy as jnp
from jax import lax
from jax.experimental import pallas as pl
from jax.experimental.pallas import tpu as pltpu

D = 1024
NG = 5 * D
NQ = 1024
NKV = 256
NP = NQ + 2 * NKV + NG
F = 4096
HD = 64
GQ = 4
WIN = 128
L = 2
NDEV = 8
EPS = 1e-6
NEG = -1e30
SMALL_ROWS = 24
LR, B1, B2, AEPS, WD, STEP = 0.001, 0.9, 0.999, 1e-08, 0.01, 10

BF = jnp.bfloat16
F32 = jnp.float32
MESH = pl.DeviceIdType.MESH
VMEM_LIMIT = 60 * 1024 * 1024
ANY = pl.BlockSpec(memory_space=pl.ANY)

NN = ((1,), (0,))
NT = ((1,), (1,))
TN = ((0,), (0,))


def _dot(a, b, dims):
    return lax.dot_general(a, b, (dims, ((), ())), preferred_element_type=F32)


def _resident(shape, imap):
    return pl.BlockSpec(shape, imap, pipeline_mode=pl.Buffered(1))


def _position():
    return lax.axis_index("x"), lax.axis_index("y"), lax.axis_index("c")


class _Gather:
    def __init__(self, shards):
        n = len(shards)
        self.inputs = list(shards)
        self.out_shape = [jax.ShapeDtypeStruct((s.shape[0], NDEV) + s.shape[1:], s.dtype) for s in shards]
        self.scratch = [pltpu.SemaphoreType.DMA((n, 7)), pltpu.SemaphoreType.DMA((n, 7)), pltpu.SemaphoreType.DMA((n,))]

    def _plan(self, src, dst, sems):
        send_sems, recv_sems, local_sems = sems
        n = len(src)
        x, y, c = _position()
        me, sibling = (x, y, c), (x, y, 1 - c)
        chips = [(1 - x, y), (x, 1 - y), (1 - x, 1 - y)]

        def rows(a, p):
            return dst[a].at[:, 4 * p[0] + 2 * p[1] + p[2]]

        def copy(a, k, block, to, from_src=False):
            return pltpu.make_async_remote_copy(
                src_ref=src[a] if from_src else rows(a, block), dst_ref=rows(a, block),
                send_sem=send_sems.at[a, k], recv_sem=recv_sems.at[a, k], device_id=to, device_id_type=MESH)

        mine = [pltpu.make_async_copy(src[a], rows(a, me), local_sems.at[a]) for a in range(n)]
        first = []
        for a in range(n):
            first.append(copy(a, 0, me, sibling, True))
            first += [copy(a, 1 + t, me, (*chip, c), True) for t, chip in enumerate(chips)]
        return n, c, me, sibling, chips, copy, mine, first

    def start(self, src, dst, sems):
        *_, mine, first = self._plan(src, dst, sems)
        for cp in mine + first:
            cp.start()

    def relay(self, src, dst, sems):
        n, c, me, sibling, chips, copy, _, _ = self._plan(src, dst, sems)
        for t, chip in enumerate(chips):
            for a in range(n):
                copy(a, 1 + t, (*chip, c), me).wait_recv()
                copy(a, 4 + t, (*chip, c), sibling).start()

    def finish(self, src, dst, sems):
        n, c, me, sibling, chips, copy, mine, first = self._plan(src, dst, sems)
        for a in range(n):
            copy(a, 0, sibling, me).wait_recv()
            for t, chip in enumerate(chips):
                copy(a, 4 + t, (*chip, 1 - c), me).wait_recv()
        for cp in first + [copy(a, 4 + t, (*chip, c), sibling) for t, chip in enumerate(chips) for a in range(n)]:
            cp.wait_send()
        for cp in mine:
            cp.wait()


class _Exchange:
    def __init__(self, grads, everyone=()):
        self.inputs = list(grads) + list(everyone)
        self.n_blocked = len(grads)
        n = len(self.inputs)
        self.out_shape = [jax.ShapeDtypeStruct(g.shape, g.dtype) for g in grads]
        self.out_shape += [jax.ShapeDtypeStruct((NDEV,) + e.shape, e.dtype) for e in everyone]
        self.scratch = [pltpu.SemaphoreType.DMA((n, 7)), pltpu.SemaphoreType.DMA((n, 7)), pltpu.SemaphoreType.DMA((n,))]

    def _plan(self, src, land, sems):
        send_sems, recv_sems, local_sems = sems
        x, y, c = _position()
        me = 4 * x + 2 * y + c

        def parts(peer_idx):
            return [(s.at[peer_idx] if a < self.n_blocked else s, land[a].at[me]) for a, s in enumerate(src)]

        local = [pltpu.make_async_copy(s, d, local_sems.at[a]) for a, (s, d) in enumerate(parts(me))]
        sent = []
        for rel in range(1, NDEV):
            px = 1 - x if rel & 4 else x
            py = 1 - y if rel & 2 else y
            pc = 1 - c if rel & 1 else c
            for a, (s, d) in enumerate(parts(4 * px + 2 * py + pc)):
                sent.append(pltpu.make_async_remote_copy(
                    src_ref=s, dst_ref=d, send_sem=send_sems.at[a, rel - 1], recv_sem=recv_sems.at[a, rel - 1],
                    device_id=(px, py, pc), device_id_type=MESH))
        return local, sent

    def start(self, src, land, sems):
        local, sent = self._plan(src, land, sems)
        for cp in local + sent:
            cp.start()

    def relay(self, src, land, sems):
        pass

    def finish(self, src, land, sems):
        local, sent = self._plan(src, land, sems)
        for cp in sent:
            cp.wait_recv()
        for cp in sent:
            cp.wait_send()
        for cp in local:
            cp.wait()


def _call(body, *, name, grid, in_specs, out_specs, out_shape, args, scratch_shapes=(), aliases=None, side=None,
          relay_at=1.0):
    sem = ("arbitrary",) * len(grid)
    if side is None:
        outs = pl.pallas_call(
            body, name=name, grid=grid, in_specs=in_specs, out_specs=out_specs, out_shape=out_shape,
            scratch_shapes=list(scratch_shapes), input_output_aliases=aliases or {},
            compiler_params=pltpu.CompilerParams(dimension_semantics=sem, vmem_limit_bytes=VMEM_LIMIT))(*args)
        return outs, []
    ni, no, ns = len(in_specs), len(out_specs), len(scratch_shapes)
    si, so = len(side.inputs), len(side.out_shape)

    def hosted(*refs):
        ins, refs = refs[:ni], refs[ni:]
        sins, refs = refs[:si], refs[si:]
        outs, refs = refs[:no], refs[no:]
        souts, refs = refs[:so], refs[so:]
        scr, sscr = refs[:ns], refs[ns:]
        step = pl.program_id(0)
        for d in range(1, len(grid)):
            step = step * grid[d] + pl.program_id(d)
        last = int(np.prod(grid)) - 1

        @pl.when(step == 0)
        def _():
            side.start(sins, souts, sscr)

        body(*ins, *outs, *scr)

        @pl.when(step == min(int(relay_at * last), last))
        def _():
            side.relay(sins, souts, sscr)

        @pl.when(step == last)
        def _():
            side.finish(sins, souts, sscr)

    outs = pl.pallas_call(
        hosted, name=name, grid=grid, in_specs=list(in_specs) + [ANY] * si, out_specs=list(out_specs) + [ANY] * so,
        out_shape=list(out_shape) + side.out_shape, scratch_shapes=list(scratch_shapes) + side.scratch,
        input_output_aliases=aliases or {},
        compiler_params=pltpu.CompilerParams(dimension_semantics=sem, vmem_limit_bytes=VMEM_LIMIT, has_side_effects=True),
    )(*args, *side.inputs)
    return outs[:no], outs[no:]


def _remote_only(side, name):
    n = len(side.inputs)

    def body(*refs):
        src, dst, sems = refs[:n], refs[n:n + len(side.out_shape)], refs[n + len(side.out_shape):]
        side.start(src, dst, sems)
        side.relay(src, dst, sems)
        side.finish(src, dst, sems)

    return pl.pallas_call(
        body, name=name, in_specs=[ANY] * n, out_specs=[ANY] * len(side.out_shape), out_shape=side.out_shape,
        scratch_shapes=side.scratch, compiler_params=pltpu.CompilerParams(has_side_effects=True))(*side.inputs)


def _rms(x, g):
    r = lax.rsqrt(jnp.mean(x * x, axis=-1, keepdims=True) + EPS)
    return x * r * g


def _rms_bwd(dy, x, g):
    r = lax.rsqrt(jnp.mean(x * x, axis=-1, keepdims=True) + EPS)
    xh = x * r
    dxh = dy * g
    dx = r * (dxh - xh * jnp.mean(dxh * xh, axis=-1, keepdims=True))
    return dx, jnp.sum(dy * xh, axis=0, keepdims=True)


def _zero_at_first_step(acc_ref):
    first = pl.program_id(0) == 0

    @pl.when(first)
    def _():
        acc_ref[...] = jnp.zeros_like(acc_ref)


ROW1 = lambda: _resident((1, D), lambda i: (0, 0))
ACC = lambda: pl.BlockSpec((8, D), lambda i: (0, 0))


def _inproj_fwd(x, g, wint, l, tm, side=None, relay_at=1.0):
    T = x.shape[0]

    def body(x_ref, g_ref, w_ref, gt_ref, q_ref, k_ref, v_ref, h_ref):
        h = _rms(x_ref[...], g_ref[...]).astype(BF)
        h_ref[...] = h
        q_ref[...] = _dot(w_ref[0:NQ, :], h, NT).astype(BF)
        k_ref[...] = _dot(w_ref[NQ:NQ + NKV, :], h, NT).astype(BF)
        v_ref[...] = _dot(w_ref[NQ + NKV:NQ + 2 * NKV, :], h, NT).astype(BF)
        for s in range(5):
            lo = NQ + 2 * NKV + s * D
            gt_ref[:, s * D:(s + 1) * D] = _dot(h, w_ref[lo:lo + D, :], NT).astype(BF)

    tok = lambda w: pl.BlockSpec((tm, w), lambda i: (i, 0))
    feat = lambda w: pl.BlockSpec((w, tm), lambda i: (0, i))
    return _call(
        body, name=f"inproj_fwd{l}", grid=(T // tm,),
        in_specs=[tok(D), ROW1(), _resident((NP, D), lambda i: (0, 0))],
        out_specs=[tok(NG), feat(NQ), feat(NKV), feat(NKV), tok(D)],
        out_shape=[jax.ShapeDtypeStruct((T, NG), BF)] + [jax.ShapeDtypeStruct((w, T), BF) for w in (NQ, NKV, NKV)]
        + [jax.ShapeDtypeStruct((T, D), BF)],
        args=(x, g, wint), side=side, relay_at=relay_at)


def _band_geometry():
    j = lax.broadcasted_iota(jnp.int32, (2 * WIN, WIN), 0)
    r = lax.broadcasted_iota(jnp.int32, (2 * WIN, WIN), 1)
    dist = WIN + r - j
    dist0 = r - j
    return (dist.astype(F32), (dist >= 0) & (dist < WIN)), (dist0.astype(F32), dist0 >= 0)


def _pair_biases(sm_ref, pj):
    return [[jnp.where(ok, -sm_ref[1, pj * 2 * GQ + h] * dist, NEG) for h in range(2 * GQ)]
            for dist, ok in _band_geometry()]


def _reduce_rows(x, pair, whole):
    while x.shape[0] > 8:
        half = x.shape[0] // 2
        x = pair(x[:half], x[half:])
    return whole(x, axis=0, keepdims=True)


def _heads_on_lanes(ref, kvh, r0):
    return jnp.concatenate([ref[(kvh * GQ + g) * HD:(kvh * GQ + g + 1) * HD, pl.ds(r0, WIN)] for g in range(GQ)], axis=1)


def _band_probs(sm_ref, head0, q_ref, k_ref, r0, p0, kvh, biases):
    qt = _heads_on_lanes(q_ref, kvh, r0) * jnp.asarray(HD ** -0.5, BF)
    kt = k_ref[kvh * HD:(kvh + 1) * HD, pl.ds(p0, 2 * WIN)]
    st = _dot(kt, qt, TN)
    heads = []
    for g in range(GQ):
        sink = sm_ref[0, head0 + g]
        s = st[:, g * WIN:(g + 1) * WIN] + biases[kvh * GQ + g]
        m = jnp.maximum(_reduce_rows(s, jnp.maximum, jnp.max), sink)
        p = jnp.exp(s - m)
        ps = jnp.exp(sink - m)
        heads.append((p, ps, 1.0 / (_reduce_rows(p, jnp.add, jnp.sum) + ps)))
    return qt, kt, heads


def _attn_fwd(q, k, v, sm, l, S, side=None, relay_at=1.0):
    T = q.shape[1]
    nblk = S // WIN
    unroll = next(u for u in (5, 3, 1) if (nblk - 1) % u == 0)

    def body(sm_ref, q_ref, k_ref, v_ref, o_ref):
        pj = pl.program_id(1)
        biases = _pair_biases(sm_ref, pj)

        def block(i, first):
            r0 = 0 if first else pl.multiple_of(i * WIN, WIN)
            p0 = 0 if first else pl.multiple_of(i * WIN - WIN, WIN)
            for kvh in range(2):
                head0 = (2 * pj + kvh) * GQ
                _, _, heads = _band_probs(sm_ref, head0, q_ref, k_ref, r0, p0, kvh, biases[first])
                pt = jnp.concatenate([p.astype(BF) for p, _, _ in heads], axis=1)
                ot = _dot(v_ref[kvh * HD:(kvh + 1) * HD, pl.ds(p0, 2 * WIN)], pt, NN)
                for g in range(GQ):
                    rows = slice((kvh * GQ + g) * HD, (kvh * GQ + g + 1) * HD)
                    o_ref[rows, pl.ds(r0, WIN)] = (ot[:, g * WIN:(g + 1) * WIN] * heads[g][2]).astype(BF)

        block(0, True)

        def rest(t, c):
            for u in range(unroll):
                block(1 + unroll * t + u, False)
            return c

        lax.fori_loop(0, (nblk - 1) // unroll, rest, 0)

    wide = lambda: pl.BlockSpec((2 * GQ * HD, S), lambda s, p: (p, s))
    narrow = lambda: pl.BlockSpec((2 * HD, S), lambda s, p: (p, s))
    return _call(
        body, name=f"attn_fwd{l}", grid=(T // S, 2),
        in_specs=[pl.BlockSpec(memory_space=pltpu.SMEM), wide(), narrow(), narrow()],
        out_specs=[wide()], out_shape=[jax.ShapeDtypeStruct((D, T), BF)],
        args=(sm, q, k, v), side=side, relay_at=relay_at)


def _shift_rows(y, k, edge_rows, down):
    n = y.shape[0]
    rid = lax.broadcasted_iota(jnp.int32, y.shape, 0)
    out = pltpu.roll(y, k if down else n - k, 0)
    for t, row in enumerate(edge_rows):
        out = jnp.where(rid == (t if down else n - k + t), row, out)
    return out


def _mixer_fwd(x, gt, att, w3, bg, cw, cbias, l, S, tm, side=None, relay_at=1.0):
    T = x.shape[0]

    def body(x_ref, cb_ref, cc_ref, cu_ref, ga_ref, gc_ref, cch_ref, cuh_ref, att_ref, wao_ref, wco_ref, wo_ref,
             bg_ref, cw_ref, cbias_ref, x1_ref, mg_ref, co_ref, ya_ref, yc_ref, zb_ref):
        first = (pl.program_id(0) * tm) % S == 0
        y = cc_ref[...].astype(F32) * cu_ref[...].astype(F32)
        hy1 = cch_ref[15:16, :].astype(F32) * cuh_ref[15:16, :].astype(F32)
        hy2 = cch_ref[14:15, :].astype(F32) * cuh_ref[14:15, :].astype(F32)
        hy1, hy2 = jnp.where(first, 0.0, hy1), jnp.where(first, 0.0, hy2)
        z = (cw_ref[0:1, :] * _shift_rows(y, 2, [hy2, hy1], True) + cw_ref[1:2, :] * _shift_rows(y, 1, [hy1], True)
             + cw_ref[2:3, :] * y)
        zb = z + cbias_ref[...]
        zb_ref[...] = zb.astype(BF)
        co = (cb_ref[...].astype(F32) * zb).astype(BF)
        co_ref[...] = co
        yc = _dot(co, wco_ref[...], NN)
        ya = _dot(att_ref[...], wao_ref[...], TN)
        ya_ref[...] = ya.astype(BF)
        yc_ref[...] = yc.astype(BF)
        sa = jax.nn.sigmoid(ga_ref[...].astype(F32) + bg_ref[0:1, :])
        sc = jax.nn.sigmoid(gc_ref[...].astype(F32) + bg_ref[1:2, :])
        mg = (sa * ya + sc * yc).astype(BF)
        mg_ref[...] = mg
        x1_ref[...] = x_ref[...] + _dot(mg, wo_ref[...], NN)

    tok = lambda: pl.BlockSpec((tm, D), lambda i: (i, 0))
    seg = lambda s: pl.BlockSpec((tm, D), lambda i: (i, s))
    halo = lambda s: pl.BlockSpec((16, D), lambda i: (jnp.maximum(i * (tm // 16) - 1, 0), s))
    wsp = lambda k: _resident((None, D, D), lambda i: (k, 0, 0))
    row = lambda n: _resident((n, D), lambda i: (0, 0))
    return _call(
        body, name=f"mixer_fwd{l}", grid=(T // tm,),
        in_specs=[tok(), seg(0), seg(1), seg(2), seg(3), seg(4), halo(1), halo(2),
                  pl.BlockSpec((D, tm), lambda i: (0, i)), wsp(0), wsp(1), wsp(2), row(2), row(8), row(1)],
        out_specs=[tok()] * 6,
        out_shape=[jax.ShapeDtypeStruct((T, D), dt) for dt in (F32, BF, BF, BF, BF, BF)],
        args=(x, gt, gt, gt, gt, gt, gt, gt, att, w3, w3, w3, bg, cw, cbias), side=side, relay_at=relay_at)


def _mlp_fwd(x1, g, w2, l, tm, head=None):
    T = x1.shape[0]
    nt = T // tm
    FC = 1024

    def body(x_ref, g_ref, wup_ref, wdn_ref, *rest):
        out_ref, h_ref, a_ref = rest[-4:-1] if head else rest
        x = x_ref[...]
        h = _rms(x, g_ref[...]).astype(BF)
        h_ref[...] = h
        acc = x
        for c in range(F // FC):
            a = _dot(h, wup_ref[c * FC:(c + 1) * FC, :], NT)
            a_ref[:, c * FC:(c + 1) * FC] = a.astype(BF)
            u = jnp.maximum(a, 0.0)
            acc = acc + _dot((u * u).astype(BF), wdn_ref[c * FC:(c + 1) * FC, :], NN)
        if not head:
            out_ref[...] = acc
            return
        t_ref, gf_ref, acc_ref = rest[0], rest[1], rest[-1]
        _zero_at_first_step(acc_ref)
        gf = gf_ref[...]
        err = _rms(acc, gf) - t_ref[...]
        out_ref[...], dg = _rms_bwd(err * (1.0 / D), acc, gf)
        acc_ref[0:1, :] += dg
        acc_ref[1:2, :] += jnp.sum(err * err, axis=0, keepdims=True)

        @pl.when(pl.program_id(0) == nt - 1)
        def _():
            acc_ref[1:2, :] = jnp.zeros((1, D), F32) + (0.5 / D) * jnp.sum(acc_ref[1:2, :])

    tok = lambda w: pl.BlockSpec((tm, w), lambda i: (i, 0))
    wsp = lambda k: _resident((None, F, D), lambda i: (k, 0, 0))
    outs, _ = _call(
        body, name=f"mlp_fwd{l}", grid=(nt,),
        in_specs=[tok(D), ROW1(), wsp(0), wsp(1)] + ([tok(D), ROW1()] if head else []),
        out_specs=[tok(D), tok(D), tok(F)] + ([ACC()] if head else []),
        out_shape=[jax.ShapeDtypeStruct((T, D), F32), jax.ShapeDtypeStruct((T, D), BF), jax.ShapeDtypeStruct((T, F), BF)]
        + ([jax.ShapeDtypeStruct((8, D), F32)] if head else []),
        args=(x1, g, w2, w2) + (tuple(head) if head else ()))
    return outs


def _mlp_bwd(dx2, x1, a, g, w2, l, tm, side=None):
    T = dx2.shape[0]
    FC = 1024

    def body(d_ref, x_ref, a_ref, g_ref, wup_ref, wdn_ref, da_ref, dx1_ref, db_ref, acc_ref):
        _zero_at_first_step(acc_ref)
        d = d_ref[...]
        db = d.astype(BF)
        db_ref[...] = db
        dh = jnp.zeros((tm, D), F32)
        for c in range(F // FC):
            du = _dot(db, wdn_ref[c * FC:(c + 1) * FC, :], NT)
            da = (du * (2.0 * jnp.maximum(a_ref[:, c * FC:(c + 1) * FC].astype(F32), 0.0))).astype(BF)
            da_ref[:, c * FC:(c + 1) * FC] = da
            dh = dh + _dot(da, wup_ref[c * FC:(c + 1) * FC, :], NN)
        dx, dg = _rms_bwd(dh, x_ref[...], g_ref[...])
        dx1_ref[...] = d + dx
        acc_ref[0:1, :] += dg

    tok = lambda w: pl.BlockSpec((tm, w), lambda i: (i, 0))
    wsp = lambda k: _resident((None, F, D), lambda i: (k, 0, 0))
    return _call(
        body, name=f"mlp_bwd{l}", grid=(T // tm,),
        in_specs=[tok(D), tok(D), tok(F), ROW1(), wsp(0), wsp(1)],
        out_specs=[tok(F), tok(D), tok(D), ACC()],
        out_shape=[jax.ShapeDtypeStruct((T, F), BF), jax.ShapeDtypeStruct((T, D), F32),
                   jax.ShapeDtypeStruct((T, D), BF), jax.ShapeDtypeStruct((8, D), F32)],
        args=(dx2, x1, a, g, w2, w2), side=side)


def _mixer_bwd(dx1, gt, ya, yc, zb, w3, bg, cw, l, S, tm, side=None):
    T = dx1.shape[0]
    nt = T // tm

    def body(d_ref, cb_ref, cc_ref, cu_ref, ga_ref, gc_ref, ya_ref, yc_ref, zb_ref, wao_ref, wco_ref, wo_ref,
             bg_ref, cw_ref, dg_ref, datt_ref, dya_ref, dyc_ref, db_ref, acc_ref, carry_ref):
        ti = nt - 1 - pl.program_id(0)
        _zero_at_first_step(acc_ref)

        @pl.when(((ti + 1) * tm) % S == 0)
        def _():
            carry_ref[...] = jnp.zeros_like(carry_ref)

        db = d_ref[...].astype(BF)
        db_ref[...] = db
        dm = _dot(db, wo_ref[...], NT)
        sa = jax.nn.sigmoid(ga_ref[...].astype(F32) + bg_ref[0:1, :])
        sc = jax.nn.sigmoid(gc_ref[...].astype(F32) + bg_ref[1:2, :])
        dya32 = dm * sa
        dyc32 = dm * sc
        dya = dya32.astype(BF)
        dyc = dyc32.astype(BF)
        dya_ref[...] = dya
        dyc_ref[...] = dyc
        dga = dya32 * ya_ref[...].astype(F32) * (1.0 - sa)
        dgc = dyc32 * yc_ref[...].astype(F32) * (1.0 - sc)
        dg_ref[:, 3 * D:4 * D] = dga.astype(BF)
        dg_ref[:, 4 * D:5 * D] = dgc.astype(BF)
        acc_ref[0:1, :] += jnp.sum(dga, axis=0, keepdims=True)
        acc_ref[1:2, :] += jnp.sum(dgc, axis=0, keepdims=True)
        datt_ref[...] = _dot(wao_ref[...], dya, NT).astype(BF)
        dco = _dot(dyc, wco_ref[...], NT)

        cc = cc_ref[...].astype(F32)
        cu = cu_ref[...].astype(F32)
        y = cc * cu
        dg_ref[:, 0:D] = (dco * zb_ref[...].astype(F32)).astype(BF)
        dz = dco * cb_ref[...].astype(F32)
        u1 = _shift_rows(dz, 1, [carry_ref[0:1, :]], False)
        u2 = _shift_rows(dz, 2, [carry_ref[0:1, :], carry_ref[1:2, :]], False)
        acc_ref[2:3, :] += jnp.sum(dz, axis=0, keepdims=True)
        acc_ref[3:4, :] += jnp.sum(u2 * y, axis=0, keepdims=True)
        acc_ref[4:5, :] += jnp.sum(u1 * y, axis=0, keepdims=True)
        acc_ref[5:6, :] += jnp.sum(dz * y, axis=0, keepdims=True)
        dy = cw_ref[2:3, :] * dz + cw_ref[1:2, :] * u1 + cw_ref[0:1, :] * u2
        dg_ref[:, D:2 * D] = (dy * cu).astype(BF)
        dg_ref[:, 2 * D:3 * D] = (dy * cc).astype(BF)
        carry_ref[...] = dz[0:8, :]

    tok = lambda w=D: pl.BlockSpec((tm, w), lambda i: (nt - 1 - i, 0))
    seg = lambda s: pl.BlockSpec((tm, D), lambda i: (nt - 1 - i, s))
    wsp = lambda k: _resident((None, D, D), lambda i: (k, 0, 0))
    row = lambda n: _resident((n, D), lambda i: (0, 0))
    return _call(
        body, name=f"mixer_bwd{l}", grid=(nt,),
        in_specs=[tok(), seg(0), seg(1), seg(2), seg(3), seg(4), tok(), tok(), tok(), wsp(0), wsp(1), wsp(2),
                  row(2), row(8)],
        out_specs=[tok(NG), pl.BlockSpec((D, tm), lambda i: (0, nt - 1 - i)), tok(), tok(), tok(), ACC()],
        out_shape=[jax.ShapeDtypeStruct((T, NG), BF), jax.ShapeDtypeStruct((D, T), BF)]
        + [jax.ShapeDtypeStruct((T, D), BF)] * 3 + [jax.ShapeDtypeStruct((8, D), F32)],
        scratch_shapes=[pltpu.VMEM((8, D), F32)],
        args=(dx1, gt, gt, gt, gt, gt, ya, yc, zb, w3, w3, w3, bg, cw), side=side)


def _attn_bwd(q, k, v, att, datt, sm, l, S, side=None):
    T = q.shape[1]
    nblk = S // WIN
    unroll = next(u for u in (5, 3, 1) if (nblk - 1) % u == 0)
    scale = HD ** -0.5

    def body(sm_ref, q_ref, k_ref, v_ref, o_ref, do_ref, dq_ref, dk_ref, dv_ref, ds_ref, dka_ref, dva_ref):
        pj = pl.program_id(1)
        biases = _pair_biases(sm_ref, pj)
        dka_ref[...] = jnp.zeros_like(dka_ref)
        dva_ref[...] = jnp.zeros_like(dva_ref)

        def block(i, first, dsinks):
            r0 = 0 if first else pl.multiple_of(i * WIN, WIN)
            p0 = 0 if first else pl.multiple_of(i * WIN - WIN, WIN)
            out = []
            for kvh in range(2):
                head0 = (2 * pj + kvh) * GQ
                rows = slice(kvh * HD, (kvh + 1) * HD)
                qt, kt, heads = _band_probs(sm_ref, head0, q_ref, k_ref, r0, p0, kvh, biases[first])
                inv = jnp.concatenate([h[2] for h in heads], axis=1)
                dos = _heads_on_lanes(do_ref, kvh, r0).astype(F32) * inv
                delta = jnp.sum(dos * _heads_on_lanes(o_ref, kvh, r0).astype(F32), axis=0, keepdims=True)
                dosb = dos.astype(BF)
                dpt = _dot(v_ref[rows, pl.ds(p0, 2 * WIN)], dosb, TN)
                dst = jnp.concatenate(
                    [(p * (dpt[:, g * WIN:(g + 1) * WIN] - delta[:, g * WIN:(g + 1) * WIN])).astype(BF)
                     for g, (p, _, _) in enumerate(heads)], axis=1)
                pt = jnp.concatenate([p.astype(BF) for p, _, _ in heads], axis=1)
                dqt = _dot(kt, dst, NN) * scale
                for g in range(GQ):
                    hr = slice((kvh * GQ + g) * HD, (kvh * GQ + g + 1) * HD)
                    dq_ref[hr, pl.ds(r0, WIN)] = dqt[:, g * WIN:(g + 1) * WIN].astype(BF)
                dka_ref[rows, pl.ds(p0, 2 * WIN)] += _dot(qt, dst, NT)
                dva_ref[rows, pl.ds(p0, 2 * WIN)] += _dot(dosb, pt, NT)
                ps = jnp.concatenate([h[1] for h in heads], axis=1)
                out.append(dsinks[kvh] - ps * delta)
            return tuple(out)

        zero = jnp.zeros((1, GQ * WIN), F32)
        def rest(t, c):
            for u in range(unroll):
                c = block(1 + unroll * t + u, False, c)
            return c

        dsinks = lax.fori_loop(0, (nblk - 1) // unroll, rest, block(0, True, (zero, zero)))
        for kvh in range(2):
            for g in range(GQ):
                tot = jnp.sum(dsinks[kvh][:, g * WIN:(g + 1) * WIN])
                ds_ref[kvh * GQ + g:kvh * GQ + g + 1, :] = jnp.zeros((1, 128), F32) + tot
        dk_ref[...] = dka_ref[...].astype(BF)
        dv_ref[...] = dva_ref[...].astype(BF)

    wide = lambda: pl.BlockSpec((2 * GQ * HD, S), lambda s, p: (p, s))
    narrow = lambda: pl.BlockSpec((2 * HD, S), lambda s, p: (p, s))
    return _call(
        body, name=f"attn_bwd{l}", grid=(T // S, 2),
        in_specs=[pl.BlockSpec(memory_space=pltpu.SMEM), wide(), narrow(), narrow(), wide(), wide()],
        out_specs=[wide(), narrow(), narrow(), pl.BlockSpec((None, None, 8, 128), lambda s, p: (s, p, 0, 0))],
        out_shape=[jax.ShapeDtypeStruct((NQ, T), BF), jax.ShapeDtypeStruct((NKV, T), BF),
                   jax.ShapeDtypeStruct((NKV, T), BF), jax.ShapeDtypeStruct((T // S, 2, 8, 128), F32)],
        scratch_shapes=[pltpu.VMEM((2 * HD, S), F32), pltpu.VMEM((2 * HD, S), F32)],
        args=(sm, q, k, v, att, datt), side=side)


def _inproj_bwd(dgt, dq, dk, dv, x, dres, g, wint, l, tm, side=None):
    T = x.shape[0]

    def body(dg_ref, dq_ref, dk_ref, dv_ref, x_ref, dr_ref, g_ref, w_ref, dx_ref, acc_ref):
        _zero_at_first_step(acc_ref)
        dh = _dot(dq_ref[...], w_ref[0:NQ, :], TN)
        dh = dh + _dot(dk_ref[...], w_ref[NQ:NQ + NKV, :], TN)
        dh = dh + _dot(dv_ref[...], w_ref[NQ + NKV:NQ + 2 * NKV, :], TN)
        dh = dh + _dot(dg_ref[...], w_ref[NQ + 2 * NKV:NP, :], NN)
        dx, dg = _rms_bwd(dh, x_ref[...], g_ref[...])
        dx_ref[...] = dr_ref[...] + dx
        acc_ref[0:1, :] += dg

    tok = lambda w: pl.BlockSpec((tm, w), lambda i: (i, 0))
    feat = lambda w: pl.BlockSpec((w, tm), lambda i: (0, i))
    return _call(
        body, name=f"inproj_bwd{l}", grid=(T // tm,),
        in_specs=[tok(NG), feat(NQ), feat(NKV), feat(NKV), tok(D), tok(D), ROW1(), _resident((NP, D), lambda i: (0, 0))],
        out_specs=[tok(D), ACC()],
        out_shape=[jax.ShapeDtypeStruct((T, D), F32), jax.ShapeDtypeStruct((8, D), F32)],
        args=(dgt, dq, dk, dv, x, dres, g, wint), side=side)


def _wgrad(a, b, rows, row0, into, name, relu2=False, a_is_transposed=False):
    M, T = a.shape if a_is_transposed else a.shape[::-1]
    tmm = next(t for t in (1024, 512, 256) if M % t == 0 and row0 % t == 0)
    tk = min(4096 if M > 4096 else 2048, T)
    nk = T // tk
    blk0 = row0 // tmm

    def body(*refs):
        a_ref, b_ref = refs[0], refs[1]
        o_ref, acc_ref = refs[-2], refs[-1]
        kk = pl.program_id(1)

        @pl.when(kk == 0)
        def _():
            acc_ref[...] = jnp.zeros_like(acc_ref)

        av = a_ref[...]
        if relu2:
            t = jnp.maximum(av.astype(F32), 0.0)
            av = (t * t).astype(BF)
        acc_ref[...] += _dot(av, b_ref[...], NN if a_is_transposed else TN)

        @pl.when(kk == nk - 1)
        def _():
            o_ref[...] = acc_ref[...].astype(BF)

    a_spec = pl.BlockSpec((tmm, tk), lambda j, kk: (j, kk)) if a_is_transposed else pl.BlockSpec((tk, tmm), lambda j, kk: (kk, j))
    in_specs = [a_spec, pl.BlockSpec((tk, D), lambda j, kk: (kk, 0))]
    args = [a, b]
    if into is not None:
        in_specs.append(ANY)
        args.append(into)
    (out,), _ = _call(
        body, name=name, grid=(M // tmm, nk), in_specs=in_specs,
        out_specs=[pl.BlockSpec((tmm, D), lambda j, kk: (blk0 + j, 0))],
        out_shape=[jax.ShapeDtypeStruct((rows, D), BF)], scratch_shapes=[pltpu.VMEM((tmm, D), F32)],
        aliases={2: 0} if into is not None else None, args=args)
    return out


def _adamw(w, g, m, v):
    m = B1 * m + (1.0 - B1) * g
    v = B2 * v + (1.0 - B2) * (g * g)
    m_hat = m / (1.0 - B1 ** STEP)
    v_hat = v / (1.0 - B2 ** STEP)
    return -LR * (m_hat / (jnp.sqrt(v_hat) + AEPS) + WD * w), m, v


def _adam_sum(land, w, m, v, l, into, name):
    _, r, _ = land.shape
    tr = 208 if r % 208 == 0 else (256 if r % 256 == 0 else r)

    def body(land_ref, w_ref, m_ref, v_ref, *rest):
        g_ref, d_ref, nm_ref, nv_ref = rest[-4:]
        g = land_ref[0].astype(F32)
        for s in range(1, NDEV):
            g = g + land_ref[s].astype(F32)
        g_ref[...] = g
        d_ref[...], nm_ref[...], nv_ref[...] = _adamw(w_ref[...], g, m_ref[...], v_ref[...])

    blk = lambda: pl.BlockSpec((None, tr, D), lambda j: (l, j, 0))
    in_specs = [pl.BlockSpec((NDEV, tr, D), lambda j: (0, j, 0)), blk(), blk(), blk()]
    args = [land, w, m, v]
    aliases = None
    if into is not None:
        in_specs += [ANY] * 4
        args += list(into)
        aliases = {4 + t: t for t in range(4)}
    outs, _ = _call(body, name=name, grid=(r // tr,), in_specs=in_specs, out_specs=[blk()] * 4,
                    out_shape=[jax.ShapeDtypeStruct(w.shape, F32)] * 4, aliases=aliases, args=args)
    return outs


SMALL_NAMES = ("g_mix", "b_gates", "sinks", "conv_b", "g_mlp", "g_final", "conv_w")


def _adam_small(land, me, masters):
    n = len(SMALL_NAMES)
    lanes = D // NDEV

    def body(land_ref, me_ref, *refs):
        ins, outs, loss_ref, gs_ref = refs[:3 * n], refs[3 * n:7 * n], refs[7 * n], refs[7 * n + 1]
        g = land_ref[0]
        for s in range(1, NDEV):
            g = g + land_ref[s]
        gs_ref[...] = g

        def update(k, g_piece, idx):
            w_ref, m_ref, v_ref = ins[3 * k:3 * k + 3]
            outs[4 * k][idx] = g_piece
            outs[4 * k + 1][idx], outs[4 * k + 2][idx], outs[4 * k + 3][idx] = _adamw(w_ref[idx], g_piece, m_ref[idx], v_ref[idx])

        whole = (slice(None), slice(None))
        update(0, gs_ref[0:2, :], whole)
        for l in range(L):
            for h in range(2):
                update(1, gs_ref[2 + 2 * l + h:3 + 2 * l + h, :], (slice(l, l + 1), slice(h * D, (h + 1) * D)))
            update(2, gs_ref[6:7, 16 * l:16 * (l + 1)], (slice(l, l + 1), slice(None)))
        update(3, gs_ref[7:9, :], whole)
        update(4, gs_ref[9:11, :], whole)
        update(5, gs_ref[11:12, :], whole)
        mine = pl.ds(pl.multiple_of(me_ref[0] * lanes, lanes), lanes)
        for l in range(L):
            for k in range(3):
                update(6, gs_ref[12 + 3 * l + k:13 + 3 * l + k, mine], (l, slice(k, k + 1), slice(None)))
        loss_ref[...] = gs_ref[18:19, 0:1]

    flat = [t for name in SMALL_NAMES for t in masters[name]]
    vmem = pl.BlockSpec(memory_space=pltpu.VMEM)
    outs = pl.pallas_call(
        body, name="adam_small",
        in_specs=[vmem, pl.BlockSpec(memory_space=pltpu.SMEM)] + [vmem] * len(flat),
        out_shape=[jax.ShapeDtypeStruct(masters[name][0].shape, F32) for name in SMALL_NAMES for _ in range(4)]
        + [jax.ShapeDtypeStruct((1, 1), F32)],
        scratch_shapes=[pltpu.VMEM((SMALL_ROWS, D), F32)],
        compiler_params=pltpu.CompilerParams(vmem_limit_bytes=VMEM_LIMIT))(land, me, *flat)
    return {name: outs[4 * k:4 * k + 4] for k, name in enumerate(SMALL_NAMES)}, outs[-1]


def _pack_small(g_mix, b_gates, sinks, conv_b, g_mlp, g_final, conv_w_rows, extra):
    sink_row = jnp.zeros((1, D), F32).at[0, :2 * 16].set(sinks.reshape(-1))
    return jnp.concatenate([g_mix, b_gates.reshape(4, D), sink_row, conv_b, g_mlp, g_final.reshape(1, D),
                            conv_w_rows, extra, jnp.zeros((SMALL_ROWS - 19, D), F32)], axis=0)


def kernel(x, g_mix, w_in, b_gates, sinks, w_attn_out, conv_w, conv_b, w_conv_out, w_o, g_mlp, w_up, w_down, g_final, loss_target, m_g_mix, m_w_in, m_b_gates, m_sinks, m_w_attn_out, m_conv_w, m_conv_b, m_w_conv_out, m_w_o, m_g_mlp, m_w_up, m_w_down, m_g_final, v_g_mix, v_w_in, v_b_gates, v_sinks, v_w_attn_out, v_conv_w, v_conv_b, v_w_conv_out, v_w_o, v_g_mlp, v_w_up, v_w_down, v_g_final):
    nseq, S, _ = x.shape
    T = nseq * S
    tm_in = min(512, S)
    tm = min(256, S)
    xi, yi, ci = _position()
    me = 4 * xi + 2 * yi + ci
    tr = lambda t: jnp.swapaxes(t, 1, 2)
    blocks = lambda t: t.reshape(NDEV, t.shape[0] // NDEV, D)

    win_t, wup_t = tr(w_in), tr(w_up)
    sh_win = [win_t[l].astype(BF)[None] for l in range(L)]
    sh_w3 = [jnp.stack([w_attn_out[l], w_conv_out[l], w_o[l]]).astype(BF) for l in range(L)]
    sh_w2 = [jnp.stack([wup_t[l], w_down[l]]).astype(BF) for l in range(L)]
    wint, w3, w2 = [None] * L, [None] * L, [None] * L
    wint0_g, cw_g = _remote_only(_Gather([sh_win[0], jnp.pad(conv_w, ((0, 0), (0, 5), (0, 0)))]), "gather_first")
    wint[0] = wint0_g.reshape(NP, D)
    cw = jnp.swapaxes(cw_g, 1, 2).reshape(L, 8, D)
    slopes = np.power(np.float32(2.0), -8.0 * np.arange(1, 17, dtype=np.float32) / 16).astype(np.float32)

    xf = x.reshape(T, D)
    saved = []
    cur = xf
    for l in range(L):
        sm = jnp.stack([sinks[l], jnp.asarray(slopes)])
        bg = b_gates[l].reshape(2, D)
        (gt, q, k, v, h), got = _inproj_fwd(cur, g_mix[l:l + 1], wint[l], l, tm_in, side=_Gather([sh_w2[l]]),
                                            relay_at=0.85)
        w2[l] = got[0].reshape(2, F, D)
        (att,), got = _attn_fwd(q, k, v, sm, l, S, side=_Gather([sh_w3[l]]), relay_at=0.5)
        w3[l] = got[0].reshape(3, D, D)
        (x1, mg, co, ya, yc, zb), got = _mixer_fwd(cur, gt, att, w3[l], bg, cw[l], conv_b[l:l + 1], l, S, tm,
                                                   side=_Gather([sh_win[l + 1]]) if l + 1 < L else None, relay_at=0.85)
        if l + 1 < L:
            wint[l + 1] = got[0].reshape(NP, D)
        head = (loss_target.reshape(T, D), g_final.reshape(1, D)) if l == L - 1 else None
        nxt, h2, a, *acc_loss = _mlp_fwd(x1, g_mlp[l:l + 1], w2[l], l, tm_in, head=head)
        saved.append(dict(x=cur, gt=gt, q=q, k=k, v=v, h=h, att=att, x1=x1, mg=mg, co=co, ya=ya, yc=yc, zb=zb, h2=h2, a=a,
                          sm=sm, bg=bg))
        cur = nxt
    dcur, acc_loss = cur, acc_loss[0]

    masters = {"w_in": (win_t, tr(m_w_in), tr(v_w_in)), "w_attn_out": (w_attn_out, m_w_attn_out, v_w_attn_out),
               "w_conv_out": (w_conv_out, m_w_conv_out, v_w_conv_out), "w_o": (w_o, m_w_o, v_w_o),
               "w_up": (wup_t, tr(m_w_up), tr(v_w_up)), "w_down": (w_down, m_w_down, v_w_down)}
    big = {name: None for name in masters}

    def adam(name, land, l):
        big[name] = _adam_sum(land, *masters[name], l, big[name], f"adam_{name}{l}")

    acc_in, acc_mix, acc_mlp, dsink = [None] * L, [None] * L, [None] * L, [None] * L
    pending = None
    for l in reversed(range(L)):
        sv = saved[l]
        side = _Exchange([blocks(pending[0])]) if pending else None
        (da, dx1, dx2b, acc_mlp[l]), got = _mlp_bwd(dcur, sv["x1"], sv["a"], g_mlp[l:l + 1], w2[l], l, tm_in, side=side)
        if pending:
            adam("w_in", got[0], pending[1])
        d_wdn = _wgrad(sv["a"], dx2b, F, 0, None, f"wgrad_down{l}", relu2=True)
        d_wup = _wgrad(da, sv["h2"], F, 0, None, f"wgrad_up{l}")
        (dgt, datt, dya, dyc, dx1b, acc_mix[l]), got = _mixer_bwd(
            dx1, sv["gt"], sv["ya"], sv["yc"], sv["zb"], w3[l], sv["bg"], cw[l], l, S, tm,
            side=_Exchange([blocks(d_wdn)]))
        adam("w_down", got[0], l)
        d_wo = _wgrad(sv["mg"], dx1b, D, 0, None, f"wgrad_o{l}")
        d_wao = _wgrad(sv["att"], dya, D, 0, None, f"wgrad_attn_out{l}", a_is_transposed=True)
        d_wco = _wgrad(sv["co"], dyc, D, 0, None, f"wgrad_conv_out{l}")
        d_win = _wgrad(dgt, sv["h"], NP, NQ + 2 * NKV, None, f"wgrad_gates{l}")
        (dq, dk, dv, dsink[l]), got = _attn_bwd(
            sv["q"], sv["k"], sv["v"], sv["att"], datt, sv["sm"], l, S,
            side=_Exchange([blocks(d_wup), blocks(d_wo), blocks(d_wao), blocks(d_wco)]))
        for name, land in zip(["w_up", "w_o", "w_attn_out", "w_conv_out"], got):
            adam(name, land, l)
        d_win = _wgrad(dq, sv["h"], NP, 0, d_win, f"wgrad_q{l}", a_is_transposed=True)
        d_win = _wgrad(dk, sv["h"], NP, NQ, d_win, f"wgrad_k{l}", a_is_transposed=True)
        d_win = _wgrad(dv, sv["h"], NP, NQ + NKV, d_win, f"wgrad_v{l}", a_is_transposed=True)
        side = _Exchange([blocks(d_win)]) if l == 0 else None
        (dcur, acc_in[l]), got = _inproj_bwd(dgt, dq, dk, dv, sv["x"], dx1, g_mix[l:l + 1], wint[l], l, tm_in, side=side)
        if l == 0:
            adam("w_in", got[0], 0)
        else:
            pending = (d_win, l)

    dsinks = jnp.stack([dsink[l][:, :, :, 0].sum(axis=0).reshape(16) for l in range(L)])
    small = _pack_small(
        jnp.concatenate([acc_in[l][0:1] for l in range(L)]),
        jnp.stack([acc_mix[l][0:2].reshape(2 * D) for l in range(L)]),
        dsinks,
        jnp.concatenate([acc_mix[l][2:3] for l in range(L)]),
        jnp.concatenate([acc_mlp[l][0:1] for l in range(L)]),
        acc_loss[0],
        jnp.concatenate([acc_mix[l][3:6] for l in range(L)]),
        acc_loss[1:2])
    (small_land,) = _remote_only(_Exchange([], [small]), "exchange_small")

    row = lambda t: t.reshape(1, D)
    small_out, loss = _adam_small(small_land, me.reshape(1).astype(jnp.int32), {
        "g_mix": (g_mix, m_g_mix, v_g_mix), "b_gates": (b_gates, m_b_gates, v_b_gates), "sinks": (sinks, m_sinks, v_sinks),
        "conv_b": (conv_b, m_conv_b, v_conv_b), "g_mlp": (g_mlp, m_g_mlp, v_g_mlp),
        "g_final": (row(g_final), row(m_g_final), row(v_g_final)), "conv_w": (conv_w, m_conv_w, v_conv_w)})
    small_out["g_final"] = [t.reshape(D) for t in small_out["g_final"]]
    for name in ("w_in", "w_up"):
        big[name] = [tr(o) for o in big[name]]
    order = ["g_mix", "w_in", "b_gates", "sinks", "w_attn_out", "conv_w", "conv_b", "w_conv_out", "w_o", "g_mlp",
             "w_up", "w_down", "g_final"]
    out = [loss.reshape(()), dcur.reshape(nseq, S, D)]
    for kind in range(4):
        for name in order:
            out.append(big[name][kind] if name in big else small_out[name][kind])
    return tuple(out)
```

```python
import numpy as np
import jax
import jax.numpy as jnp
from jax import lax
from jax.experimental import pallas as pl
from jax.experimental.pallas import tpu as pltpu

D = 1024
NG = 5 * D
NQ = 1024
NKV = 256
NP = NQ + 2 * NKV + NG
F = 4096
HD = 64
GQ = 4
WIN = 128
L = 2
NDEV = 8
EPS = 1e-6
NEG = -1e30
SMALL_ROWS = 24
LR, B1, B2, AEPS, WD, STEP = 0.001, 0.9, 0.999, 1e-08, 0.01, 10

BF = jnp.bfloat16
F32 = jnp.float32
MESH = pl.DeviceIdType.MESH
VMEM_LIMIT = 60 * 1024 * 1024
ANY = pl.BlockSpec(memory_space=pl.ANY)

NN = ((1,), (0,))
NT = ((1,), (1,))
TN = ((0,), (0,))


def _dot(a, b, dims):
    return lax.dot_general(a, b, (dims, ((), ())), preferred_element_type=F32)


def _resident(shape, imap):
    return pl.BlockSpec(shape, imap, pipeline_mode=pl.Buffered(1))


def _position():
    return lax.axis_index("x"), lax.axis_index("y"), lax.axis_index("c")


class _Gather:
    def __init__(self, shards):
        n = len(shards)
        self.aliases = {}
        self.inputs = list(shards)
        self.out_shape = [jax.ShapeDtypeStruct((s.shape[0], NDEV) + s.shape[1:], s.dtype) for s in shards]
        self.scratch = [pltpu.SemaphoreType.DMA((n, 7)), pltpu.SemaphoreType.DMA((n, 7)), pltpu.SemaphoreType.DMA((n,))]

    def _plan(self, src, dst, sems):
        send_sems, recv_sems, local_sems = sems
        n = len(src)
        x, y, c = _position()
        me, sibling = (x, y, c), (x, y, 1 - c)
        chips = [(1 - x, y), (x, 1 - y), (1 - x, 1 - y)]

        def rows(a, p):
            return dst[a].at[:, 4 * p[0] + 2 * p[1] + p[2]]

        def copy(a, k, block, to, from_src=False):
            return pltpu.make_async_remote_copy(
                src_ref=src[a] if from_src else rows(a, block), dst_ref=rows(a, block),
                send_sem=send_sems.at[a, k], recv_sem=recv_sems.at[a, k], device_id=to, device_id_type=MESH)

        mine = [pltpu.make_async_copy(src[a], rows(a, me), local_sems.at[a]) for a in range(n)]
        first = []
        for a in range(n):
            first.append(copy(a, 0, me, sibling, True))
            first += [copy(a, 1 + t, me, (*chip, c), True) for t, chip in enumerate(chips)]
        return n, c, me, sibling, chips, copy, mine, first

    def start(self, src, dst, sems):
        *_, mine, first = self._plan(src, dst, sems)
        for cp in mine + first:
            cp.start()

    def relay(self, src, dst, sems):
        n, c, me, sibling, chips, copy, _, _ = self._plan(src, dst, sems)
        for t, chip in enumerate(chips):
            for a in range(n):
                copy(a, 1 + t, (*chip, c), me).wait_recv()
                copy(a, 4 + t, (*chip, c), sibling).start()

    def finish(self, src, dst, sems):
        n, c, me, sibling, chips, copy, mine, first = self._plan(src, dst, sems)
        for a in range(n):
            copy(a, 0, sibling, me).wait_recv()
            for t, chip in enumerate(chips):
                copy(a, 4 + t, (*chip, 1 - c), me).wait_recv()
        for cp in first + [copy(a, 4 + t, (*chip, c), sibling) for t, chip in enumerate(chips) for a in range(n)]:
            cp.wait_send()
        for cp in mine:
            cp.wait()


class _Exchange:
    def __init__(self, grads, everyone=(), only=None, into=None):
        n_src = len(grads) + len(everyone)
        self.n_blocked = len(grads)
        self.only = dict(only or {})
        self.into = dict(into or {})
        self.inputs = list(grads) + list(everyone) + [self.into[a] for a in sorted(self.into)]
        self.aliases = {n_src + t: a for t, a in enumerate(sorted(self.into))}
        self.out_shape = [jax.ShapeDtypeStruct(g.shape, g.dtype) for g in grads]
        self.out_shape += [jax.ShapeDtypeStruct((NDEV,) + e.shape, e.dtype) for e in everyone]
        self.scratch = [pltpu.SemaphoreType.DMA((n_src, 7)), pltpu.SemaphoreType.DMA((n_src, 7)),
                        pltpu.SemaphoreType.DMA((n_src,))]

    def _plan(self, src, land, sems):
        send_sems, recv_sems, local_sems = sems
        x, y, c = _position()
        me = 4 * x + 2 * y + c
        n_src = len(land)

        def takes(a, idx):
            if a not in self.only:
                return True
            lo, hi = self.only[a]
            return (idx >= lo) & (idx < hi)

        def parts(peer_idx):
            return [(src[a].at[peer_idx] if a < self.n_blocked else src[a], land[a].at[me]) for a in range(n_src)]

        local = [(pltpu.make_async_copy(s, d, local_sems.at[a]), takes(a, me)) for a, (s, d) in enumerate(parts(me))]
        sent, received = [], []
        for rel in range(1, NDEV):
            px = 1 - x if rel & 4 else x
            py = 1 - y if rel & 2 else y
            pc = 1 - c if rel & 1 else c
            peer = 4 * px + 2 * py + pc
            for a, (s, d) in enumerate(parts(peer)):
                cp = pltpu.make_async_remote_copy(
                    src_ref=s, dst_ref=d, send_sem=send_sems.at[a, rel - 1], recv_sem=recv_sems.at[a, rel - 1],
                    device_id=(px, py, pc), device_id_type=MESH)
                sent.append((cp, takes(a, peer)))
                received.append((cp, takes(a, me)))
        return local, sent, received

    @staticmethod
    def _each(pairs, act):
        for cp, exists in pairs:
            if exists is True:
                act(cp)
            else:
                pl.when(exists)(lambda cp=cp: act(cp))

    def start(self, src, land, sems):
        local, sent, _ = self._plan(src, land, sems)
        self._each(local + sent, lambda cp: cp.start())

    def relay(self, src, land, sems):
        pass

    def finish(self, src, land, sems):
        local, sent, received = self._plan(src, land, sems)
        self._each(received, lambda cp: cp.wait_recv())
        self._each(sent, lambda cp: cp.wait_send())
        self._each(local, lambda cp: cp.wait())


def _call(body, *, name, grid, in_specs, out_specs, out_shape, args, scratch_shapes=(), aliases=None, side=None,
          relay_at=1.0):
    sem = ("arbitrary",) * len(grid)
    if side is None:
        outs = pl.pallas_call(
            body, name=name, grid=grid, in_specs=in_specs, out_specs=out_specs, out_shape=out_shape,
            scratch_shapes=list(scratch_shapes), input_output_aliases=aliases or {},
            compiler_params=pltpu.CompilerParams(dimension_semantics=sem, vmem_limit_bytes=VMEM_LIMIT))(*args)
        return outs, []
    ni, no, ns = len(in_specs), len(out_specs), len(scratch_shapes)
    si, so = len(side.inputs), len(side.out_shape)

    def hosted(*refs):
        ins, refs = refs[:ni], refs[ni:]
        sins, refs = refs[:si], refs[si:]
        outs, refs = refs[:no], refs[no:]
        souts, refs = refs[:so], refs[so:]
        scr, sscr = refs[:ns], refs[ns:]
        step = pl.program_id(0)
        for d in range(1, len(grid)):
            step = step * grid[d] + pl.program_id(d)
        last = int(np.prod(grid)) - 1

        @pl.when(step == 0)
        def _():
            side.start(sins, souts, sscr)

        body(*ins, *outs, *scr)

        @pl.when(step == min(int(relay_at * last), last))
        def _():
            side.relay(sins, souts, sscr)

        @pl.when(step == last)
        def _():
            side.finish(sins, souts, sscr)

    outs = pl.pallas_call(
        hosted, name=name, grid=grid, in_specs=list(in_specs) + [ANY] * si, out_specs=list(out_specs) + [ANY] * so,
        out_shape=list(out_shape) + side.out_shape, scratch_shapes=list(scratch_shapes) + side.scratch,
        input_output_aliases={**(aliases or {}), **{ni + k: no + j for k, j in side.aliases.items()}},
        compiler_params=pltpu.CompilerParams(dimension_semantics=sem, vmem_limit_bytes=VMEM_LIMIT, has_side_effects=True),
    )(*args, *side.inputs)
    return outs[:no], outs[no:]


def _remote_only(side, name):
    n = len(side.inputs)

    def body(*refs):
        src, dst, sems = refs[:n], refs[n:n + len(side.out_shape)], refs[n + len(side.out_shape):]
        side.start(src, dst, sems)
        side.relay(src, dst, sems)
        side.finish(src, dst, sems)

    return pl.pallas_call(
        body, name=name, in_specs=[ANY] * n, out_specs=[ANY] * len(side.out_shape), out_shape=side.out_shape,
        scratch_shapes=side.scratch, compiler_params=pltpu.CompilerParams(has_side_effects=True))(*side.inputs)


def _rms(x, g):
    r = lax.rsqrt(jnp.mean(x * x, axis=-1, keepdims=True) + EPS)
    return x * r * g


def _rms_bwd(dy, x, g):
    r = lax.rsqrt(jnp.mean(x * x, axis=-1, keepdims=True) + EPS)
    xh = x * r
    dxh = dy * g
    dx = r * (dxh - xh * jnp.mean(dxh * xh, axis=-1, keepdims=True))
    return dx, jnp.sum(dy * xh, axis=0, keepdims=True)


def _zero_at_first_step(acc_ref):
    first = pl.program_id(0) == 0

    @pl.when(first)
    def _():
        acc_ref[...] = jnp.zeros_like(acc_ref)


ROW1 = lambda: _resident((1, D), lambda i: (0, 0))
ACC = lambda: pl.BlockSpec((8, D), lambda i: (0, 0))


def _inproj_fwd(x, g, wint, l, tm, side=None, relay_at=1.0):
    T = x.shape[0]

    def body(x_ref, g_ref, w_ref, gt_ref, q_ref, k_ref, v_ref, h_ref):
        h = _rms(x_ref[...], g_ref[...]).astype(BF)
        h_ref[...] = h
        q_ref[...] = _dot(w_ref[0:NQ, :], h, NT).astype(BF)
        k_ref[...] = _dot(w_ref[NQ:NQ + NKV, :], h, NT).astype(BF)
        v_ref[...] = _dot(w_ref[NQ + NKV:NQ + 2 * NKV, :], h, NT).astype(BF)
        for s in range(5):
            lo = NQ + 2 * NKV + s * D
            gt_ref[:, s * D:(s + 1) * D] = _dot(h, w_ref[lo:lo + D, :], NT).astype(BF)

    tok = lambda w: pl.BlockSpec((tm, w), lambda i: (i, 0))
    feat = lambda w: pl.BlockSpec((w, tm), lambda i: (0, i))
    return _call(
        body, name=f"inproj_fwd{l}", grid=(T // tm,),
        in_specs=[tok(D), ROW1(), _resident((NP, D), lambda i: (0, 0))],
        out_specs=[tok(NG), feat(NQ), feat(NKV), feat(NKV), tok(D)],
        out_shape=[jax.ShapeDtypeStruct((T, NG), BF)] + [jax.ShapeDtypeStruct((w, T), BF) for w in (NQ, NKV, NKV)]
        + [jax.ShapeDtypeStruct((T, D), BF)],
        args=(x, g, wint), side=side, relay_at=relay_at)


def _band_geometry():
    j = lax.broadcasted_iota(jnp.int32, (2 * WIN, WIN), 0)
    r = lax.broadcasted_iota(jnp.int32, (2 * WIN, WIN), 1)
    dist = WIN + r - j
    dist0 = r - j
    return (dist.astype(F32), (dist >= 0) & (dist < WIN)), (dist0.astype(F32), dist0 >= 0)


def _pair_biases(sm_ref, pj):
    return [[jnp.where(ok, -sm_ref[1, pj * 2 * GQ + h] * dist, NEG) for h in range(2 * GQ)]
            for dist, ok in _band_geometry()]


def _reduce_rows(x, pair, whole):
    while x.shape[0] > 8:
        half = x.shape[0] // 2
        x = pair(x[:half], x[half:])
    return whole(x, axis=0, keepdims=True)


def _heads_on_lanes(ref, kvh, r0):
    return jnp.concatenate([ref[(kvh * GQ + g) * HD:(kvh * GQ + g + 1) * HD, pl.ds(r0, WIN)] for g in range(GQ)], axis=1)


def _band_probs(sm_ref, head0, q_ref, k_ref, r0, p0, kvh, biases):
    qt = _heads_on_lanes(q_ref, kvh, r0) * jnp.asarray(HD ** -0.5, BF)
    kt = k_ref[kvh * HD:(kvh + 1) * HD, pl.ds(p0, 2 * WIN)]
    st = _dot(kt, qt, TN)
    heads = []
    for g in range(GQ):
        sink = sm_ref[0, head0 + g]
        s = st[:, g * WIN:(g + 1) * WIN] + biases[kvh * GQ + g]
        m = jnp.maximum(_reduce_rows(s, jnp.maximum, jnp.max), sink)
        p = jnp.exp(s - m)
        ps = jnp.exp(sink - m)
        heads.append((p, ps, 1.0 / (_reduce_rows(p, jnp.add, jnp.sum) + ps)))
    return qt, kt, heads


def _attn_fwd(q, k, v, sm, l, S, side=None, relay_at=1.0):
    T = q.shape[1]
    nblk = S // WIN
    unroll = next(u for u in (5, 3, 1) if (nblk - 1) % u == 0)

    def body(sm_ref, q_ref, k_ref, v_ref, o_ref):
        pj = pl.program_id(1)
        biases = _pair_biases(sm_ref, pj)

        def block(i, first):
            r0 = 0 if first else pl.multiple_of(i * WIN, WIN)
            p0 = 0 if first else pl.multiple_of(i * WIN - WIN, WIN)
            for kvh in range(2):
                head0 = (2 * pj + kvh) * GQ
                _, _, heads = _band_probs(sm_ref, head0, q_ref, k_ref, r0, p0, kvh, biases[first])
                pt = jnp.concatenate([p.astype(BF) for p, _, _ in heads], axis=1)
                ot = _dot(v_ref[kvh * HD:(kvh + 1) * HD, pl.ds(p0, 2 * WIN)], pt, NN)
                for g in range(GQ):
                    rows = slice((kvh * GQ + g) * HD, (kvh * GQ + g + 1) * HD)
                    o_ref[rows, pl.ds(r0, WIN)] = (ot[:, g * WIN:(g + 1) * WIN] * heads[g][2]).astype(BF)

        block(0, True)

        def rest(t, c):
            for u in range(unroll):
                block(1 + unroll * t + u, False)
            return c

        lax.fori_loop(0, (nblk - 1) // unroll, rest, 0)

    wide = lambda: pl.BlockSpec((2 * GQ * HD, S), lambda s, p: (p, s))
    narrow = lambda: pl.BlockSpec((2 * HD, S), lambda s, p: (p, s))
    return _call(
        body, name=f"attn_fwd{l}", grid=(T // S, 2),
        in_specs=[pl.BlockSpec(memory_space=pltpu.SMEM), wide(), narrow(), narrow()],
        out_specs=[wide()], out_shape=[jax.ShapeDtypeStruct((D, T), BF)],
        args=(sm, q, k, v), side=side, relay_at=relay_at)


def _shift_rows(y, k, edge_rows, down):
    n = y.shape[0]
    rid = lax.broadcasted_iota(jnp.int32, y.shape, 0)
    out = pltpu.roll(y, k if down else n - k, 0)
    for t, row in enumerate(edge_rows):
        out = jnp.where(rid == (t if down else n - k + t), row, out)
    return out


def _mixer_fwd(x, gt, att, w3, bg, cw, cbias, l, S, tm, side=None, relay_at=1.0):
    T = x.shape[0]

    def body(x_ref, cb_ref, cc_ref, cu_ref, ga_ref, gc_ref, cch_ref, cuh_ref, att_ref, wao_ref, wco_ref, wo_ref,
             bg_ref, cw_ref, cbias_ref, x1_ref, mg_ref, co_ref, ya_ref, yc_ref, zb_ref):
        first = (pl.program_id(0) * tm) % S == 0
        y = cc_ref[...].astype(F32) * cu_ref[...].astype(F32)
        hy1 = cch_ref[15:16, :].astype(F32) * cuh_ref[15:16, :].astype(F32)
        hy2 = cch_ref[14:15, :].astype(F32) * cuh_ref[14:15, :].astype(F32)
        hy1, hy2 = jnp.where(first, 0.0, hy1), jnp.where(first, 0.0, hy2)
        z = (cw_ref[0:1, :] * _shift_rows(y, 2, [hy2, hy1], True) + cw_ref[1:2, :] * _shift_rows(y, 1, [hy1], True)
             + cw_ref[2:3, :] * y)
        zb = z + cbias_ref[...]
        zb_ref[...] = zb.astype(BF)
        co = (cb_ref[...].astype(F32) * zb).astype(BF)
        co_ref[...] = co
        yc = _dot(co, wco_ref[...], NN)
        ya = _dot(att_ref[...], wao_ref[...], TN)
        ya_ref[...] = ya.astype(BF)
        yc_ref[...] = yc.astype(BF)
        sa = jax.nn.sigmoid(ga_ref[...].astype(F32) + bg_ref[0:1, :])
        sc = jax.nn.sigmoid(gc_ref[...].astype(F32) + bg_ref[1:2, :])
        mg = (sa * ya + sc * yc).astype(BF)
        mg_ref[...] = mg
        x1_ref[...] = x_ref[...] + _dot(mg, wo_ref[...], NN)

    tok = lambda: pl.BlockSpec((tm, D), lambda i: (i, 0))
    seg = lambda s: pl.BlockSpec((tm, D), lambda i: (i, s))
    halo = lambda s: pl.BlockSpec((16, D), lambda i: (jnp.maximum(i * (tm // 16) - 1, 0), s))
    wsp = lambda k: _resident((None, D, D), lambda i: (k, 0, 0))
    row = lambda n: _resident((n, D), lambda i: (0, 0))
    return _call(
        body, name=f"mixer_fwd{l}", grid=(T // tm,),
        in_specs=[tok(), seg(0), seg(1), seg(2), seg(3), seg(4), halo(1), halo(2),
                  pl.BlockSpec((D, tm), lambda i: (0, i)), wsp(0), wsp(1), wsp(2), row(2), row(8), row(1)],
        out_specs=[tok()] * 6,
        out_shape=[jax.ShapeDtypeStruct((T, D), dt) for dt in (F32, BF, BF, BF, BF, BF)],
        args=(x, gt, gt, gt, gt, gt, gt, gt, att, w3, w3, w3, bg, cw, cbias), side=side, relay_at=relay_at)


def _mlp_fwd(x1, g, w2, l, tm, head=None):
    T = x1.shape[0]
    nt = T // tm
    FC = 1024

    def body(x_ref, g_ref, wup_ref, wdn_ref, *rest):
        out_ref, h_ref, a_ref = rest[-4:-1] if head else rest
        x = x_ref[...]
        h = _rms(x, g_ref[...]).astype(BF)
        h_ref[...] = h
        acc = x
        for c in range(F // FC):
            a = _dot(h, wup_ref[c * FC:(c + 1) * FC, :], NT)
            a_ref[:, c * FC:(c + 1) * FC] = a.astype(BF)
            u = jnp.maximum(a, 0.0)
            acc = acc + _dot((u * u).astype(BF), wdn_ref[c * FC:(c + 1) * FC, :], NN)
        if not head:
            out_ref[...] = acc
            return
        t_ref, gf_ref, acc_ref = rest[0], rest[1], rest[-1]
        _zero_at_first_step(acc_ref)
        gf = gf_ref[...]
        err = _rms(acc, gf) - t_ref[...]
        out_ref[...], dg = _rms_bwd(err * (1.0 / D), acc, gf)
        acc_ref[0:1, :] += dg
        acc_ref[1:2, :] += jnp.sum(err * err, axis=0, keepdims=True)

        @pl.when(pl.program_id(0) == nt - 1)
        def _():
            acc_ref[1:2, :] = jnp.zeros((1, D), F32) + (0.5 / D) * jnp.sum(acc_ref[1:2, :])

    tok = lambda w: pl.BlockSpec((tm, w), lambda i: (i, 0))
    wsp = lambda k: _resident((None, F, D), lambda i: (k, 0, 0))
    outs, _ = _call(
        body, name=f"mlp_fwd{l}", grid=(nt,),
        in_specs=[tok(D), ROW1(), wsp(0), wsp(1)] + ([tok(D), ROW1()] if head else []),
        out_specs=[tok(D), tok(D), tok(F)] + ([ACC()] if head else []),
        out_shape=[jax.ShapeDtypeStruct((T, D), F32), jax.ShapeDtypeStruct((T, D), BF), jax.ShapeDtypeStruct((T, F), BF)]
        + ([jax.ShapeDtypeStruct((8, D), F32)] if head else []),
        args=(x1, g, w2, w2) + (tuple(head) if head else ()))
    return outs


def _mlp_bwd(dx2, x1, a, g, w2, l, tm, side=None):
    T = dx2.shape[0]
    FC = 1024

    def body(d_ref, x_ref, a_ref, g_ref, wup_ref, wdn_ref, da_ref, dx1_ref, db_ref, acc_ref):
        _zero_at_first_step(acc_ref)
        d = d_ref[...]
        db = d.astype(BF)
        db_ref[...] = db
        dh = jnp.zeros((tm, D), F32)
        for c in range(F // FC):
            du = _dot(db, wdn_ref[c * FC:(c + 1) * FC, :], NT)
            da = (du * (2.0 * jnp.maximum(a_ref[:, c * FC:(c + 1) * FC].astype(F32), 0.0))).astype(BF)
            da_ref[:, c * FC:(c + 1) * FC] = da
            dh = dh + _dot(da, wup_ref[c * FC:(c + 1) * FC, :], NN)
        dx, dg = _rms_bwd(dh, x_ref[...], g_ref[...])
        dx1_ref[...] = d + dx
        acc_ref[0:1, :] += dg

    tok = lambda w: pl.BlockSpec((tm, w), lambda i: (i, 0))
    wsp = lambda k: _resident((None, F, D), lambda i: (k, 0, 0))
    return _call(
        body, name=f"mlp_bwd{l}", grid=(T // tm,),
        in_specs=[tok(D), tok(D), tok(F), ROW1(), wsp(0), wsp(1)],
        out_specs=[tok(F), tok(D), tok(D), ACC()],
        out_shape=[jax.ShapeDtypeStruct((T, F), BF), jax.ShapeDtypeStruct((T, D), F32),
                   jax.ShapeDtypeStruct((T, D), BF), jax.ShapeDtypeStruct((8, D), F32)],
        args=(dx2, x1, a, g, w2, w2), side=side)


def _mixer_bwd(dx1, gt, ya, yc, zb, w3, bg, cw, l, S, tm, side=None):
    T = dx1.shape[0]
    nt = T // tm

    def body(d_ref, cb_ref, cc_ref, cu_ref, ga_ref, gc_ref, ya_ref, yc_ref, zb_ref, wao_ref, wco_ref, wo_ref,
             bg_ref, cw_ref, dg_ref, datt_ref, dya_ref, dyc_ref, db_ref, acc_ref, carry_ref):
        ti = nt - 1 - pl.program_id(0)
        _zero_at_first_step(acc_ref)

        @pl.when(((ti + 1) * tm) % S == 0)
        def _():
            carry_ref[...] = jnp.zeros_like(carry_ref)

        db = d_ref[...].astype(BF)
        db_ref[...] = db
        dm = _dot(db, wo_ref[...], NT)
        sa = jax.nn.sigmoid(ga_ref[...].astype(F32) + bg_ref[0:1, :])
        sc = jax.nn.sigmoid(gc_ref[...].astype(F32) + bg_ref[1:2, :])
        dya32 = dm * sa
        dyc32 = dm * sc
        dya = dya32.astype(BF)
        dyc = dyc32.astype(BF)
        dya_ref[...] = dya
        dyc_ref[...] = dyc
        dga = dya32 * ya_ref[...].astype(F32) * (1.0 - sa)
        dgc = dyc32 * yc_ref[...].astype(F32) * (1.0 - sc)
        dg_ref[:, 3 * D:4 * D] = dga.astype(BF)
        dg_ref[:, 4 * D:5 * D] = dgc.astype(BF)
        acc_ref[0:1, :] += jnp.sum(dga, axis=0, keepdims=True)
        acc_ref[1:2, :] += jnp.sum(dgc, axis=0, keepdims=True)
        datt_ref[...] = _dot(wao_ref[...], dya, NT).astype(BF)
        dco = _dot(dyc, wco_ref[...], NT)

        cc = cc_ref[...].astype(F32)
        cu = cu_ref[...].astype(F32)
        y = cc * cu
        dg_ref[:, 0:D] = (dco * zb_ref[...].astype(F32)).astype(BF)
        dz = dco * cb_ref[...].astype(F32)
        u1 = _shift_rows(dz, 1, [carry_ref[0:1, :]], False)
        u2 = _shift_rows(dz, 2, [carry_ref[0:1, :], carry_ref[1:2, :]], False)
        acc_ref[2:3, :] += jnp.sum(dz, axis=0, keepdims=True)
        acc_ref[3:4, :] += jnp.sum(u2 * y, axis=0, keepdims=True)
        acc_ref[4:5, :] += jnp.sum(u1 * y, axis=0, keepdims=True)
        acc_ref[5:6, :] += jnp.sum(dz * y, axis=0, keepdims=True)
        dy = cw_ref[2:3, :] * dz + cw_ref[1:2, :] * u1 + cw_ref[0:1, :] * u2
        dg_ref[:, D:2 * D] = (dy * cu).astype(BF)
        dg_ref[:, 2 * D:3 * D] = (dy * cc).astype(BF)
        carry_ref[...] = dz[0:8, :]

    tok = lambda w=D: pl.BlockSpec((tm, w), lambda i: (nt - 1 - i, 0))
    seg = lambda s: pl.BlockSpec((tm, D), lambda i: (nt - 1 - i, s))
    wsp = lambda k: _resident((None, D, D), lambda i: (k, 0, 0))
    row = lambda n: _resident((n, D), lambda i: (0, 0))
    return _call(
        body, name=f"mixer_bwd{l}", grid=(nt,),
        in_specs=[tok(), seg(0), seg(1), seg(2), seg(3), seg(4), tok(), tok(), tok(), wsp(0), wsp(1), wsp(2),
                  row(2), row(8)],
        out_specs=[tok(NG), pl.BlockSpec((D, tm), lambda i: (0, nt - 1 - i)), tok(), tok(), tok(), ACC()],
        out_shape=[jax.ShapeDtypeStruct((T, NG), BF), jax.ShapeDtypeStruct((D, T), BF)]
        + [jax.ShapeDtypeStruct((T, D), BF)] * 3 + [jax.ShapeDtypeStruct((8, D), F32)],
        scratch_shapes=[pltpu.VMEM((8, D), F32)],
        args=(dx1, gt, gt, gt, gt, gt, ya, yc, zb, w3, w3, w3, bg, cw), side=side)


def _attn_bwd(q, k, v, att, datt, sm, l, S, side=None):
    T = q.shape[1]
    nblk = S // WIN
    unroll = next(u for u in (5, 3, 1) if (nblk - 1) % u == 0)
    scale = HD ** -0.5

    def body(sm_ref, q_ref, k_ref, v_ref, o_ref, do_ref, dq_ref, dk_ref, dv_ref, ds_ref, dka_ref, dva_ref):
        pj = pl.program_id(1)
        biases = _pair_biases(sm_ref, pj)
        dka_ref[...] = jnp.zeros_like(dka_ref)
        dva_ref[...] = jnp.zeros_like(dva_ref)

        def block(i, first, dsinks):
            r0 = 0 if first else pl.multiple_of(i * WIN, WIN)
            p0 = 0 if first else pl.multiple_of(i * WIN - WIN, WIN)
            out = []
            for kvh in range(2):
                head0 = (2 * pj + kvh) * GQ
                rows = slice(kvh * HD, (kvh + 1) * HD)
                qt, kt, heads = _band_probs(sm_ref, head0, q_ref, k_ref, r0, p0, kvh, biases[first])
                inv = jnp.concatenate([h[2] for h in heads], axis=1)
                dos = _heads_on_lanes(do_ref, kvh, r0).astype(F32) * inv
                delta = jnp.sum(dos * _heads_on_lanes(o_ref, kvh, r0).astype(F32), axis=0, keepdims=True)
                dosb = dos.astype(BF)
                dpt = _dot(v_ref[rows, pl.ds(p0, 2 * WIN)], dosb, TN)
                dst = jnp.concatenate(
                    [(p * (dpt[:, g * WIN:(g + 1) * WIN] - delta[:, g * WIN:(g + 1) * WIN])).astype(BF)
                     for g, (p, _, _) in enumerate(heads)], axis=1)
                pt = jnp.concatenate([p.astype(BF) for p, _, _ in heads], axis=1)
                dqt = _dot(kt, dst, NN) * scale
                for g in range(GQ):
                    hr = slice((kvh * GQ + g) * HD, (kvh * GQ + g + 1) * HD)
                    dq_ref[hr, pl.ds(r0, WIN)] = dqt[:, g * WIN:(g + 1) * WIN].astype(BF)
                dka_ref[rows, pl.ds(p0, 2 * WIN)] += _dot(qt, dst, NT)
                dva_ref[rows, pl.ds(p0, 2 * WIN)] += _dot(dosb, pt, NT)
                ps = jnp.concatenate([h[1] for h in heads], axis=1)
                out.append(dsinks[kvh] - ps * delta)
            return tuple(out)

        zero = jnp.zeros((1, GQ * WIN), F32)
        def rest(t, c):
            for u in range(unroll):
                c = block(1 + unroll * t + u, False, c)
            return c

        dsinks = lax.fori_loop(0, (nblk - 1) // unroll, rest, block(0, True, (zero, zero)))
        for kvh in range(2):
            for g in range(GQ):
                tot = jnp.sum(dsinks[kvh][:, g * WIN:(g + 1) * WIN])
                ds_ref[kvh * GQ + g:kvh * GQ + g + 1, :] = jnp.zeros((1, 128), F32) + tot
        dk_ref[...] = dka_ref[...].astype(BF)
        dv_ref[...] = dva_ref[...].astype(BF)

    wide = lambda: pl.BlockSpec((2 * GQ * HD, S), lambda s, p: (p, s))
    narrow = lambda: pl.BlockSpec((2 * HD, S), lambda s, p: (p, s))
    return _call(
        body, name=f"attn_bwd{l}", grid=(T // S, 2),
        in_specs=[pl.BlockSpec(memory_space=pltpu.SMEM), wide(), narrow(), narrow(), wide(), wide()],
        out_specs=[wide(), narrow(), narrow(), pl.BlockSpec((None, None, 8, 128), lambda s, p: (s, p, 0, 0))],
        out_shape=[jax.ShapeDtypeStruct((NQ, T), BF), jax.ShapeDtypeStruct((NKV, T), BF),
                   jax.ShapeDtypeStruct((NKV, T), BF), jax.ShapeDtypeStruct((T // S, 2, 8, 128), F32)],
        scratch_shapes=[pltpu.VMEM((2 * HD, S), F32), pltpu.VMEM((2 * HD, S), F32)],
        args=(sm, q, k, v, att, datt), side=side)


def _inproj_bwd(dgt, dq, dk, dv, x, dres, g, wint, l, tm, side=None):
    T = x.shape[0]

    def body(dg_ref, dq_ref, dk_ref, dv_ref, x_ref, dr_ref, g_ref, w_ref, dx_ref, acc_ref):
        _zero_at_first_step(acc_ref)
        dh = _dot(dq_ref[...], w_ref[0:NQ, :], TN)
        dh = dh + _dot(dk_ref[...], w_ref[NQ:NQ + NKV, :], TN)
        dh = dh + _dot(dv_ref[...], w_ref[NQ + NKV:NQ + 2 * NKV, :], TN)
        dh = dh + _dot(dg_ref[...], w_ref[NQ + 2 * NKV:NP, :], NN)
        dx, dg = _rms_bwd(dh, x_ref[...], g_ref[...])
        dx_ref[...] = dr_ref[...] + dx
        acc_ref[0:1, :] += dg

    tok = lambda w: pl.BlockSpec((tm, w), lambda i: (i, 0))
    feat = lambda w: pl.BlockSpec((w, tm), lambda i: (0, i))
    return _call(
        body, name=f"inproj_bwd{l}", grid=(T // tm,),
        in_specs=[tok(NG), feat(NQ), feat(NKV), feat(NKV), tok(D), tok(D), ROW1(), _resident((NP, D), lambda i: (0, 0))],
        out_specs=[tok(D), ACC()],
        out_shape=[jax.ShapeDtypeStruct((T, D), F32), jax.ShapeDtypeStruct((8, D), F32)],
        args=(dgt, dq, dk, dv, x, dres, g, wint), side=side)


def _wgrad(a, b, rows, row0, into, name, relu2=False, a_is_transposed=False, side=None):
    M, T = a.shape if a_is_transposed else a.shape[::-1]
    tmm = next(t for t in (1024, 512, 256) if M % t == 0 and row0 % t == 0)
    tk = min(4096 if M > 4096 else 2048, T)
    nk = T // tk
    blk0 = row0 // tmm

    def body(*refs):
        a_ref, b_ref = refs[0], refs[1]
        o_ref, acc_ref = refs[-2], refs[-1]
        kk = pl.program_id(1)

        @pl.when(kk == 0)
        def _():
            acc_ref[...] = jnp.zeros_like(acc_ref)

        av = a_ref[...]
        if relu2:
            t = jnp.maximum(av.astype(F32), 0.0)
            av = (t * t).astype(BF)
        acc_ref[...] += _dot(av, b_ref[...], NN if a_is_transposed else TN)

        @pl.when(kk == nk - 1)
        def _():
            o_ref[...] = acc_ref[...].astype(BF)

    a_spec = pl.BlockSpec((tmm, tk), lambda j, kk: (j, kk)) if a_is_transposed else pl.BlockSpec((tk, tmm), lambda j, kk: (kk, j))
    in_specs = [a_spec, pl.BlockSpec((tk, D), lambda j, kk: (kk, 0))]
    args = [a, b]
    if into is not None:
        in_specs.append(ANY)
        args.append(into)
    (out,), got = _call(
        body, name=name, grid=(M // tmm, nk), in_specs=in_specs,
        out_specs=[pl.BlockSpec((tmm, D), lambda j, kk: (blk0 + j, 0))],
        out_shape=[jax.ShapeDtypeStruct((rows, D), BF)], scratch_shapes=[pltpu.VMEM((tmm, D), F32)],
        aliases={2: 0} if into is not None else None, args=args, side=side)
    return (out, got) if side is not None else out


def _adamw(w, g, m, v):
    m = B1 * m + (1.0 - B1) * g
    v = B2 * v + (1.0 - B2) * (g * g)
    m_hat = m / (1.0 - B1 ** STEP)
    v_hat = v / (1.0 - B2 ** STEP)
    return -LR * (m_hat / (jnp.sqrt(v_hat) + AEPS) + WD * w), m, v


def _adam_sum(land, w, m, v, l, into, name, side=None):
    _, r, _ = land.shape
    tr = 208 if r % 208 == 0 else (256 if r % 256 == 0 else r)

    def body(land_ref, w_ref, m_ref, v_ref, *rest):
        g_ref, d_ref, nm_ref, nv_ref = rest[-4:]
        g = land_ref[0].astype(F32)
        for s in range(1, NDEV):
            g = g + land_ref[s].astype(F32)
        g_ref[...] = g
        d_ref[...], nm_ref[...], nv_ref[...] = _adamw(w_ref[...], g, m_ref[...], v_ref[...])

    blk = lambda: pl.BlockSpec((None, tr, D), lambda j: (l, j, 0))
    in_specs = [pl.BlockSpec((NDEV, tr, D), lambda j: (0, j, 0)), blk(), blk(), blk()]
    args = [land, w, m, v]
    aliases = None
    if into is not None:
        in_specs += [ANY] * 4
        args += list(into)
        aliases = {4 + t: t for t in range(4)}
    return _call(body, name=name, grid=(r // tr,), in_specs=in_specs, out_specs=[blk()] * 4,
                 out_shape=[jax.ShapeDtypeStruct(w.shape, F32)] * 4, aliases=aliases, args=args, side=side)


SMALL_NAMES = ("g_mix", "b_gates", "sinks", "conv_b", "g_mlp", "g_final", "conv_w")


def _adam_small(land, me, masters):
    n = len(SMALL_NAMES)
    lanes = D // NDEV

    def body(land_ref, me_ref, *refs):
        ins, outs, loss_ref, gs_ref = refs[:3 * n], refs[3 * n:7 * n], refs[7 * n], refs[7 * n + 1]
        g = land_ref[0]
        for s in range(1, NDEV):
            g = g + land_ref[s]
        gs_ref[...] = g

        def update(k, g_piece, idx):
            w_ref, m_ref, v_ref = ins[3 * k:3 * k + 3]
            outs[4 * k][idx] = g_piece
            outs[4 * k + 1][idx], outs[4 * k + 2][idx], outs[4 * k + 3][idx] = _adamw(w_ref[idx], g_piece, m_ref[idx], v_ref[idx])

        whole = (slice(None), slice(None))
        update(0, gs_ref[0:2, :], whole)
        for l in range(L):
            for h in range(2):
                update(1, gs_ref[2 + 2 * l + h:3 + 2 * l + h, :], (slice(l, l + 1), slice(h * D, (h + 1) * D)))
            update(2, gs_ref[6:7, 16 * l:16 * (l + 1)], (slice(l, l + 1), slice(None)))
        update(3, gs_ref[7:9, :], whole)
        update(4, gs_ref[9:11, :], whole)
        update(5, gs_ref[11:12, :], whole)
        mine = pl.ds(pl.multiple_of(me_ref[0] * lanes, lanes), lanes)
        for l in range(L):
            for k in range(3):
                update(6, gs_ref[12 + 3 * l + k:13 + 3 * l + k, mine], (l, slice(k, k + 1), slice(None)))
        loss_ref[...] = gs_ref[18:19, 0:1]

    flat = [t for name in SMALL_NAMES for t in masters[name]]
    vmem = pl.BlockSpec(memory_space=pltpu.VMEM)
    outs = pl.pallas_call(
        body, name="adam_small",
        in_specs=[vmem, pl.BlockSpec(memory_space=pltpu.SMEM)] + [vmem] * len(flat),
        out_shape=[jax.ShapeDtypeStruct(masters[name][0].shape, F32) for name in SMALL_NAMES for _ in range(4)]
        + [jax.ShapeDtypeStruct((1, 1), F32)],
        scratch_shapes=[pltpu.VMEM((SMALL_ROWS, D), F32)],
        compiler_params=pltpu.CompilerParams(vmem_limit_bytes=VMEM_LIMIT))(land, me, *flat)
    return {name: outs[4 * k:4 * k + 4] for k, name in enumerate(SMALL_NAMES)}, outs[-1]


def _pack_small(g_mix, b_gates, sinks, conv_b, g_mlp, g_final, conv_w_rows, extra):
    sink_row = jnp.zeros((1, D), F32).at[0, :2 * 16].set(sinks.reshape(-1))
    return jnp.concatenate([g_mix, b_gates.reshape(4, D), sink_row, conv_b, g_mlp, g_final.reshape(1, D),
                            conv_w_rows, extra, jnp.zeros((SMALL_ROWS - 19, D), F32)], axis=0)


def kernel(x, g_mix, w_in, b_gates, sinks, w_attn_out, conv_w, conv_b, w_conv_out, w_o, g_mlp, w_up, w_down, g_final, loss_target, m_g_mix, m_w_in, m_b_gates, m_sinks, m_w_attn_out, m_conv_w, m_conv_b, m_w_conv_out, m_w_o, m_g_mlp, m_w_up, m_w_down, m_g_final, v_g_mix, v_w_in, v_b_gates, v_sinks, v_w_attn_out, v_conv_w, v_conv_b, v_w_conv_out, v_w_o, v_g_mlp, v_w_up, v_w_down, v_g_final):
    nseq, S, _ = x.shape
    T = nseq * S
    tm_in = min(512, S)
    tm = min(256, S)
    xi, yi, ci = _position()
    me = 4 * xi + 2 * yi + ci
    tr = lambda t: jnp.swapaxes(t, 1, 2)
    blocks = lambda t: t.reshape(NDEV, t.shape[0] // NDEV, D)

    win_t, wup_t = tr(w_in), tr(w_up)
    sh_win = [win_t[l].astype(BF)[None] for l in range(L)]
    sh_w3 = [jnp.stack([w_attn_out[l], w_conv_out[l], w_o[l]]).astype(BF) for l in range(L)]
    sh_w2 = [jnp.stack([wup_t[l], w_down[l]]).astype(BF) for l in range(L)]
    wint, w3, w2 = [None] * L, [None] * L, [None] * L
    wint0_g, cw_g = _remote_only(_Gather([sh_win[0], jnp.pad(conv_w, ((0, 0), (0, 5), (0, 0)))]), "gather_first")
    wint[0] = wint0_g.reshape(NP, D)
    cw = jnp.swapaxes(cw_g, 1, 2).reshape(L, 8, D)
    slopes = np.power(np.float32(2.0), -8.0 * np.arange(1, 17, dtype=np.float32) / 16).astype(np.float32)

    xf = x.reshape(T, D)
    saved = []
    cur = xf
    for l in range(L):
        sm = jnp.stack([sinks[l], jnp.asarray(slopes)])
        bg = b_gates[l].reshape(2, D)
        (gt, q, k, v, h), got = _inproj_fwd(cur, g_mix[l:l + 1], wint[l], l, tm_in, side=_Gather([sh_w2[l]]),
                                            relay_at=0.85)
        w2[l] = got[0].reshape(2, F, D)
        (att,), got = _attn_fwd(q, k, v, sm, l, S, side=_Gather([sh_w3[l]]), relay_at=0.5)
        w3[l] = got[0].reshape(3, D, D)
        (x1, mg, co, ya, yc, zb), got = _mixer_fwd(cur, gt, att, w3[l], bg, cw[l], conv_b[l:l + 1], l, S, tm,
                                                   side=_Gather([sh_win[l + 1]]) if l + 1 < L else None, relay_at=0.85)
        if l + 1 < L:
            wint[l + 1] = got[0].reshape(NP, D)
        head = (loss_target.reshape(T, D), g_final.reshape(1, D)) if l == L - 1 else None
        nxt, h2, a, *acc_loss = _mlp_fwd(x1, g_mlp[l:l + 1], w2[l], l, tm_in, head=head)
        saved.append(dict(x=cur, gt=gt, q=q, k=k, v=v, h=h, att=att, x1=x1, mg=mg, co=co, ya=ya, yc=yc, zb=zb, h2=h2, a=a,
                          sm=sm, bg=bg))
        cur = nxt
    dcur, acc_loss = cur, acc_loss[0]

    masters = {"w_in": (win_t, tr(m_w_in), tr(v_w_in)), "w_attn_out": (w_attn_out, m_w_attn_out, v_w_attn_out),
               "w_conv_out": (w_conv_out, m_w_conv_out, v_w_conv_out), "w_o": (w_o, m_w_o, v_w_o),
               "w_up": (wup_t, tr(m_w_up), tr(v_w_up)), "w_down": (w_down, m_w_down, v_w_down)}
    big = {name: None for name in masters}

    def adam(name, land, l, side=None):
        big[name], got = _adam_sum(land, *masters[name], l, big[name], f"adam_{name}{l}", side=side)
        return got

    GATE_BLOCK = -(-(NQ + 2 * NKV) // (NP // NDEV))
    acc_in, acc_mix, acc_mlp, dsink = [None] * L, [None] * L, [None] * L, [None] * L
    for l in reversed(range(L)):
        sv = saved[l]
        (da, dx1, dx2b, acc_mlp[l]), _ = _mlp_bwd(dcur, sv["x1"], sv["a"], g_mlp[l:l + 1], w2[l], l, tm_in)
        d_wdn = _wgrad(sv["a"], dx2b, F, 0, None, f"wgrad_down{l}", relu2=True)
        d_wup = _wgrad(da, sv["h2"], F, 0, None, f"wgrad_up{l}")
        (dgt, datt, dya, dyc, dx1b, acc_mix[l]), got = _mixer_bwd(
            dx1, sv["gt"], sv["ya"], sv["yc"], sv["zb"], w3[l], sv["bg"], cw[l], l, S, tm,
            side=_Exchange([blocks(d_wdn)]))
        adam("w_down", got[0], l)
        d_wo = _wgrad(sv["mg"], dx1b, D, 0, None, f"wgrad_o{l}")
        d_wao = _wgrad(sv["att"], dya, D, 0, None, f"wgrad_attn_out{l}", a_is_transposed=True)
        d_wco = _wgrad(sv["co"], dyc, D, 0, None, f"wgrad_conv_out{l}")
        d_win, got = _wgrad(dgt, sv["h"], NP, NQ + 2 * NKV, None, f"wgrad_gates{l}", side=_Exchange([blocks(d_wup)]))
        adam("w_up", got[0], l)
        (dq, dk, dv, dsink[l]), got = _attn_bwd(
            sv["q"], sv["k"], sv["v"], sv["att"], datt, sv["sm"], l, S,
            side=_Exchange([blocks(d_win), blocks(d_wo), blocks(d_wao), blocks(d_wco)], only={0: (GATE_BLOCK, NDEV)}))
        for name, land in zip(["w_o", "w_attn_out", "w_conv_out"], got[1:]):
            adam(name, land, l)
        d_win = _wgrad(dq, sv["h"], NP, 0, d_win, f"wgrad_q{l}", a_is_transposed=True)
        d_win = _wgrad(dk, sv["h"], NP, NQ, d_win, f"wgrad_k{l}", a_is_transposed=True)
        d_win = _wgrad(dv, sv["h"], NP, NQ + NKV, d_win, f"wgrad_v{l}", a_is_transposed=True)
        (dcur, acc_in[l]), got_in = _inproj_bwd(
            dgt, dq, dk, dv, sv["x"], dx1, g_mix[l:l + 1], wint[l], l, tm_in,
            side=_Exchange([blocks(d_win)], only={0: (0, GATE_BLOCK)}, into={0: got[0]}))
        if l > 0:
            adam("w_in", got_in[0], l)

    dsinks = jnp.stack([dsink[l][:, :, :, 0].sum(axis=0).reshape(16) for l in range(L)])
    small = _pack_small(
        jnp.concatenate([acc_in[l][0:1] for l in range(L)]),
        jnp.stack([acc_mix[l][0:2].reshape(2 * D) for l in range(L)]),
        dsinks,
        jnp.concatenate([acc_mix[l][2:3] for l in range(L)]),
        jnp.concatenate([acc_mlp[l][0:1] for l in range(L)]),
        acc_loss[0],
        jnp.concatenate([acc_mix[l][3:6] for l in range(L)]),
        acc_loss[1:2])
    (small_land,) = adam("w_in", got_in[0], 0, side=_Exchange([], [small]))

    row = lambda t: t.reshape(1, D)
    small_out, loss = _adam_small(small_land, me.reshape(1).astype(jnp.int32), {
        "g_mix": (g_mix, m_g_mix, v_g_mix), "b_gates": (b_gates, m_b_gates, v_b_gates), "sinks": (sinks, m_sinks, v_sinks),
        "conv_b": (conv_b, m_conv_b, v_conv_b), "g_mlp": (g_mlp, m_g_mlp, v_g_mlp),
        "g_final": (row(g_final), row(m_g_final), row(v_g_final)), "conv_w": (conv_w, m_conv_w, v_conv_w)})
    small_out["g_final"] = [t.reshape(D) for t in small_out["g_final"]]
    for name in ("w_in", "w_up"):
        big[name] = [tr(o) for o in big[name]]
    order = ["g_mix", "w_in", "b_gates", "sinks", "w_attn_out", "conv_w", "conv_b", "w_conv_out", "w_o", "g_mlp",
             "w_up", "w_down", "g_final"]
    out = [loss.reshape(()), dcur.reshape(nseq, S, D)]
    for kind in range(4):
        for name in order:
            out.append(big[name][kind] if name in big else small_out[name][kind])
    return tuple(out)
```

```python
import numpy as np
import jax
import jax.numpy as jnp
from jax import lax
from jax.experimental import pallas as pl
from jax.experimental.pallas import tpu as pltpu

D = 1024
NG = 5 * D
NQ = 1024
NKV = 256
NP = NQ + 2 * NKV + NG
F = 4096
HD = 64
GQ = 4
WIN = 128
L = 2
NDEV = 8
EPS = 1e-6
NEG = -1e30
SMALL_ROWS = 24
LR, B1, B2, AEPS, WD, STEP = 0.001, 0.9, 0.999, 1e-08, 0.01, 10

BF = jnp.bfloat16
F32 = jnp.float32
MESH = pl.DeviceIdType.MESH
VMEM_LIMIT = 60 * 1024 * 1024
ANY = pl.BlockSpec(memory_space=pl.ANY)

NN = ((1,), (0,))
NT = ((1,), (1,))
TN = ((0,), (0,))


def _dot(a, b, dims):
    return lax.dot_general(a, b, (dims, ((), ())), preferred_element_type=F32)


def _resident(shape, imap):
    return pl.BlockSpec(shape, imap, pipeline_mode=pl.Buffered(1))


def _position():
    return lax.axis_index("x"), lax.axis_index("y"), lax.axis_index("c")


class _Gather:
    def __init__(self, shards):
        n = len(shards)
        self.inputs = list(shards)
        self.out_shape = [jax.ShapeDtypeStruct((s.shape[0], NDEV) + s.shape[1:], s.dtype) for s in shards]
        self.scratch = [pltpu.SemaphoreType.DMA((n, 7)), pltpu.SemaphoreType.DMA((n, 7)), pltpu.SemaphoreType.DMA((n,))]

    def _plan(self, src, dst, sems):
        send_sems, recv_sems, local_sems = sems
        n = len(src)
        x, y, c = _position()
        me, sibling = (x, y, c), (x, y, 1 - c)
        chips = [(1 - x, y), (x, 1 - y), (1 - x, 1 - y)]

        def rows(a, p):
            return dst[a].at[:, 4 * p[0] + 2 * p[1] + p[2]]

        def copy(a, k, block, to, from_src=False):
            return pltpu.make_async_remote_copy(
                src_ref=src[a] if from_src else rows(a, block), dst_ref=rows(a, block),
                send_sem=send_sems.at[a, k], recv_sem=recv_sems.at[a, k], device_id=to, device_id_type=MESH)

        mine = [pltpu.make_async_copy(src[a], rows(a, me), local_sems.at[a]) for a in range(n)]
        first = []
        for a in range(n):
            first.append(copy(a, 0, me, sibling, True))
            first += [copy(a, 1 + t, me, (*chip, c), True) for t, chip in enumerate(chips)]
        return n, c, me, sibling, chips, copy, mine, first

    def start(self, src, dst, sems):
        *_, mine, first = self._plan(src, dst, sems)
        for cp in mine + first:
            cp.start()

    def relay(self, src, dst, sems):
        n, c, me, sibling, chips, copy, _, _ = self._plan(src, dst, sems)
        for t, chip in enumerate(chips):
            for a in range(n):
                copy(a, 1 + t, (*chip, c), me).wait_recv()
                copy(a, 4 + t, (*chip, c), sibling).start()

    def finish(self, src, dst, sems):
        n, c, me, sibling, chips, copy, mine, first = self._plan(src, dst, sems)
        for a in range(n):
            copy(a, 0, sibling, me).wait_recv()
            for t, chip in enumerate(chips):
                copy(a, 4 + t, (*chip, 1 - c), me).wait_recv()
        for cp in first + [copy(a, 4 + t, (*chip, c), sibling) for t, chip in enumerate(chips) for a in range(n)]:
            cp.wait_send()
        for cp in mine:
            cp.wait()


class _Exchange:
    def __init__(self, grads, everyone=()):
        self.inputs = list(grads) + list(everyone)
        self.n_blocked = len(grads)
        n = len(self.inputs)
        self.out_shape = [jax.ShapeDtypeStruct(g.shape, g.dtype) for g in grads]
        self.out_shape += [jax.ShapeDtypeStruct((NDEV,) + e.shape, e.dtype) for e in everyone]
        self.scratch = [pltpu.SemaphoreType.DMA((n, 7)), pltpu.SemaphoreType.DMA((n, 7)), pltpu.SemaphoreType.DMA((n,))]

    def _plan(self, src, land, sems):
        send_sems, recv_sems, local_sems = sems
        x, y, c = _position()
        me = 4 * x + 2 * y + c

        def parts(peer_idx):
            return [(s.at[peer_idx] if a < self.n_blocked else s, land[a].at[me]) for a, s in enumerate(src)]

        local = [pltpu.make_async_copy(s, d, local_sems.at[a]) for a, (s, d) in enumerate(parts(me))]
        sent = []
        for rel in range(1, NDEV):
            px = 1 - x if rel & 4 else x
            py = 1 - y if rel & 2 else y
            pc = 1 - c if rel & 1 else c
            for a, (s, d) in enumerate(parts(4 * px + 2 * py + pc)):
                sent.append(pltpu.make_async_remote_copy(
                    src_ref=s, dst_ref=d, send_sem=send_sems.at[a, rel - 1], recv_sem=recv_sems.at[a, rel - 1],
                    device_id=(px, py, pc), device_id_type=MESH))
        return local, sent

    def start(self, src, land, sems):
        local, sent = self._plan(src, land, sems)
        for cp in local + sent:
            cp.start()

    def relay(self, src, land, sems):
        pass

    def finish(self, src, land, sems):
        local, sent = self._plan(src, land, sems)
        for cp in sent:
            cp.wait_recv()
        for cp in sent:
            cp.wait_send()
        for cp in local:
            cp.wait()


def _call(body, *, name, grid, in_specs, out_specs, out_shape, args, scratch_shapes=(), aliases=None, side=None,
          relay_at=1.0):
    sem = ("arbitrary",) * len(grid)
    if side is None:
        outs = pl.pallas_call(
            body, name=name, grid=grid, in_specs=in_specs, out_specs=out_specs, out_shape=out_shape,
            scratch_shapes=list(scratch_shapes), input_output_aliases=aliases or {},
            compiler_params=pltpu.CompilerParams(dimension_semantics=sem, vmem_limit_bytes=VMEM_LIMIT))(*args)
        return outs, []
    ni, no, ns = len(in_specs), len(out_specs), len(scratch_shapes)
    si, so = len(side.inputs), len(side.out_shape)

    def hosted(*refs):
        ins, refs = refs[:ni], refs[ni:]
        sins, refs = refs[:si], refs[si:]
        outs, refs = refs[:no], refs[no:]
        souts, refs = refs[:so], refs[so:]
        scr, sscr = refs[:ns], refs[ns:]
        step = pl.program_id(0)
        for d in range(1, len(grid)):
            step = step * grid[d] + pl.program_id(d)
        last = int(np.prod(grid)) - 1

        @pl.when(step == 0)
        def _():
            side.start(sins, souts, sscr)

        body(*ins, *outs, *scr)

        @pl.when(step == min(int(relay_at * last), last))
        def _():
            side.relay(sins, souts, sscr)

        @pl.when(step == last)
        def _():
            side.finish(sins, souts, sscr)

    outs = pl.pallas_call(
        hosted, name=name, grid=grid, in_specs=list(in_specs) + [ANY] * si, out_specs=list(out_specs) + [ANY] * so,
        out_shape=list(out_shape) + side.out_shape, scratch_shapes=list(scratch_shapes) + side.scratch,
        input_output_aliases=aliases or {},
        compiler_params=pltpu.CompilerParams(dimension_semantics=sem, vmem_limit_bytes=VMEM_LIMIT, has_side_effects=True),
    )(*args, *side.inputs)
    return outs[:no], outs[no:]


def _remote_only(side, name):
    n = len(side.inputs)

    def body(*refs):
        src, dst, sems = refs[:n], refs[n:n + len(side.out_shape)], refs[n + len(side.out_shape):]
        side.start(src, dst, sems)
        side.relay(src, dst, sems)
        side.finish(src, dst, sems)

    return pl.pallas_call(
        body, name=name, in_specs=[ANY] * n, out_specs=[ANY] * len(side.out_shape), out_shape=side.out_shape,
        scratch_shapes=side.scratch, compiler_params=pltpu.CompilerParams(has_side_effects=True))(*side.inputs)


def _rms(x, g):
    r = lax.rsqrt(jnp.mean(x * x, axis=-1, keepdims=True) + EPS)
    return x * r * g


def _rms_bwd(dy, x, g):
    r = lax.rsqrt(jnp.mean(x * x, axis=-1, keepdims=True) + EPS)
    xh = x * r
    dxh = dy * g
    dx = r * (dxh - xh * jnp.mean(dxh * xh, axis=-1, keepdims=True))
    return dx, jnp.sum(dy * xh, axis=0, keepdims=True)


def _zero_at_first_step(acc_ref):
    first = pl.program_id(0) == 0

    @pl.when(first)
    def _():
        acc_ref[...] = jnp.zeros_like(acc_ref)


ROW1 = lambda: _resident((1, D), lambda i: (0, 0))
ACC = lambda: pl.BlockSpec((8, D), lambda i: (0, 0))


def _inproj_fwd(x, g, wint, l, tm, side=None, relay_at=1.0):
    T = x.shape[0]

    def body(x_ref, g_ref, w_ref, gt_ref, q_ref, k_ref, v_ref, h_ref):
        h = _rms(x_ref[...], g_ref[...]).astype(BF)
        h_ref[...] = h
        q_ref[...] = _dot(w_ref[0:NQ, :], h, NT).astype(BF)
        k_ref[...] = _dot(w_ref[NQ:NQ + NKV, :], h, NT).astype(BF)
        v_ref[...] = _dot(w_ref[NQ + NKV:NQ + 2 * NKV, :], h, NT).astype(BF)
        for s in range(5):
            lo = NQ + 2 * NKV + s * D
            gt_ref[:, s * D:(s + 1) * D] = _dot(h, w_ref[lo:lo + D, :], NT).astype(BF)

    tok = lambda w: pl.BlockSpec((tm, w), lambda i: (i, 0))
    feat = lambda w: pl.BlockSpec((w, tm), lambda i: (0, i))
    return _call(
        body, name=f"inproj_fwd{l}", grid=(T // tm,),
        in_specs=[tok(D), ROW1(), _resident((NP, D), lambda i: (0, 0))],
        out_specs=[tok(NG), feat(NQ), feat(NKV), feat(NKV), tok(D)],
        out_shape=[jax.ShapeDtypeStruct((T, NG), BF)] + [jax.ShapeDtypeStruct((w, T), BF) for w in (NQ, NKV, NKV)]
        + [jax.ShapeDtypeStruct((T, D), BF)],
        args=(x, g, wint), side=side, relay_at=relay_at)


def _band_geometry():
    j = lax.broadcasted_iota(jnp.int32, (2 * WIN, WIN), 0)
    r = lax.broadcasted_iota(jnp.int32, (2 * WIN, WIN), 1)
    dist = WIN + r - j
    dist0 = r - j
    return (dist.astype(F32), (dist >= 0) & (dist < WIN)), (dist0.astype(F32), dist0 >= 0)


def _pair_biases(sm_ref, pj):
    return [[jnp.where(ok, -sm_ref[1, pj * 2 * GQ + h] * dist, NEG) for h in range(2 * GQ)]
            for dist, ok in _band_geometry()]


def _reduce_rows(x, pair, whole):
    while x.shape[0] > 8:
        half = x.shape[0] // 2
        x = pair(x[:half], x[half:])
    return whole(x, axis=0, keepdims=True)


def _heads_on_lanes(ref, kvh, r0):
    return jnp.concatenate([ref[(kvh * GQ + g) * HD:(kvh * GQ + g + 1) * HD, pl.ds(r0, WIN)] for g in range(GQ)], axis=1)


def _band_probs(sm_ref, head0, q_ref, k_ref, r0, p0, kvh, biases):
    qt = _heads_on_lanes(q_ref, kvh, r0) * jnp.asarray(HD ** -0.5, BF)
    kt = k_ref[kvh * HD:(kvh + 1) * HD, pl.ds(p0, 2 * WIN)]
    st = _dot(kt, qt, TN)
    heads = []
    for g in range(GQ):
        sink = sm_ref[0, head0 + g]
        s = st[:, g * WIN:(g + 1) * WIN] + biases[kvh * GQ + g]
        m = jnp.maximum(_reduce_rows(s, jnp.maximum, jnp.max), sink)
        p = jnp.exp(s - m)
        ps = jnp.exp(sink - m)
        heads.append((p, ps, 1.0 / (_reduce_rows(p, jnp.add, jnp.sum) + ps)))
    return qt, kt, heads


def _attn_fwd(q, k, v, sm, l, S, side=None, relay_at=1.0):
    T = q.shape[1]
    nblk = S // WIN
    unroll = next(u for u in (5, 3, 1) if (nblk - 1) % u == 0)

    def body(sm_ref, q_ref, k_ref, v_ref, o_ref):
        pj = pl.program_id(1)
        biases = _pair_biases(sm_ref, pj)

        def block(i, first):
            r0 = 0 if first else pl.multiple_of(i * WIN, WIN)
            p0 = 0 if first else pl.multiple_of(i * WIN - WIN, WIN)
            for kvh in range(2):
                head0 = (2 * pj + kvh) * GQ
                _, _, heads = _band_probs(sm_ref, head0, q_ref, k_ref, r0, p0, kvh, biases[first])
                pt = jnp.concatenate([p.astype(BF) for p, _, _ in heads], axis=1)
                ot = _dot(v_ref[kvh * HD:(kvh + 1) * HD, pl.ds(p0, 2 * WIN)], pt, NN)
                for g in range(GQ):
                    rows = slice((kvh * GQ + g) * HD, (kvh * GQ + g + 1) * HD)
                    o_ref[rows, pl.ds(r0, WIN)] = (ot[:, g * WIN:(g + 1) * WIN] * heads[g][2]).astype(BF)

        block(0, True)

        def rest(t, c):
            for u in range(unroll):
                block(1 + unroll * t + u, False)
            return c

        lax.fori_loop(0, (nblk - 1) // unroll, rest, 0)

    wide = lambda: pl.BlockSpec((2 * GQ * HD, S), lambda s, p: (p, s))
    narrow = lambda: pl.BlockSpec((2 * HD, S), lambda s, p: (p, s))
    return _call(
        body, name=f"attn_fwd{l}", grid=(T // S, 2),
        in_specs=[pl.BlockSpec(memory_space=pltpu.SMEM), wide(), narrow(), narrow()],
        out_specs=[wide()], out_shape=[jax.ShapeDtypeStruct((D, T), BF)],
        args=(sm, q, k, v), side=side, relay_at=relay_at)


def _shift_rows(y, k, edge_rows, down):
    n = y.shape[0]
    rid = lax.broadcasted_iota(jnp.int32, y.shape, 0)
    out = pltpu.roll(y, k if down else n - k, 0)
    for t, row in enumerate(edge_rows):
        out = jnp.where(rid == (t if down else n - k + t), row, out)
    return out


def _mixer_fwd(x, gt, att, w3, bg, cw, cbias, l, S, tm, side=None, relay_at=1.0):
    T = x.shape[0]

    def body(x_ref, cb_ref, cc_ref, cu_ref, ga_ref, gc_ref, cch_ref, cuh_ref, att_ref, wao_ref, wco_ref, wo_ref,
             bg_ref, cw_ref, cbias_ref, x1_ref, mg_ref, co_ref, ya_ref, yc_ref, zb_ref):
        first = (pl.program_id(0) * tm) % S == 0
        y = cc_ref[...].astype(F32) * cu_ref[...].astype(F32)
        hy1 = cch_ref[15:16, :].astype(F32) * cuh_ref[15:16, :].astype(F32)
        hy2 = cch_ref[14:15, :].astype(F32) * cuh_ref[14:15, :].astype(F32)
        hy1, hy2 = jnp.where(first, 0.0, hy1), jnp.where(first, 0.0, hy2)
        z = (cw_ref[0:1, :] * _shift_rows(y, 2, [hy2, hy1], True) + cw_ref[1:2, :] * _shift_rows(y, 1, [hy1], True)
             + cw_ref[2:3, :] * y)
        zb = z + cbias_ref[...]
        zb_ref[...] = zb.astype(BF)
        co = (cb_ref[...].astype(F32) * zb).astype(BF)
        co_ref[...] = co
        yc = _dot(co, wco_ref[...], NN)
        ya = _dot(att_ref[...], wao_ref[...], TN)
        ya_ref[...] = ya.astype(BF)
        yc_ref[...] = yc.astype(BF)
        sa = jax.nn.sigmoid(ga_ref[...].astype(F32) + bg_ref[0:1, :])
        sc = jax.nn.sigmoid(gc_ref[...].astype(F32) + bg_ref[1:2, :])
        mg = (sa * ya + sc * yc).astype(BF)
        mg_ref[...] = mg
        x1_ref[...] = x_ref[...] + _dot(mg, wo_ref[...], NN)

    tok = lambda: pl.BlockSpec((tm, D), lambda i: (i, 0))
    seg = lambda s: pl.BlockSpec((tm, D), lambda i: (i, s))
    halo = lambda s: pl.BlockSpec((16, D), lambda i: (jnp.maximum(i * (tm // 16) - 1, 0), s))
    wsp = lambda k: _resident((None, D, D), lambda i: (k, 0, 0))
    row = lambda n: _resident((n, D), lambda i: (0, 0))
    return _call(
        body, name=f"mixer_fwd{l}", grid=(T // tm,),
        in_specs=[tok(), seg(0), seg(1), seg(2), seg(3), seg(4), halo(1), halo(2),
                  pl.BlockSpec((D, tm), lambda i: (0, i)), wsp(0), wsp(1), wsp(2), row(2), row(8), row(1)],
        out_specs=[tok()] * 6,
        out_shape=[jax.ShapeDtypeStruct((T, D), dt) for dt in (F32, BF, BF, BF, BF, BF)],
        args=(x, gt, gt, gt, gt, gt, gt, gt, att, w3, w3, w3, bg, cw, cbias), side=side, relay_at=relay_at)


def _mlp_fwd(x1, g, w2, l, tm, head=None):
    T = x1.shape[0]
    nt = T // tm
    FC = 1024

    def body(x_ref, g_ref, wup_ref, wdn_ref, *rest):
        out_ref, h_ref, a_ref = rest[-4:-1] if head else rest
        x = x_ref[...]
        h = _rms(x, g_ref[...]).astype(BF)
        h_ref[...] = h
        acc = x
        for c in range(F // FC):
            a = _dot(h, wup_ref[c * FC:(c + 1) * FC, :], NT)
            a_ref[:, c * FC:(c + 1) * FC] = a.astype(BF)
            u = jnp.maximum(a, 0.0)
            acc = acc + _dot((u * u).astype(BF), wdn_ref[c * FC:(c + 1) * FC, :], NN)
        if not head:
            out_ref[...] = acc
            return
        t_ref, gf_ref, acc_ref = rest[0], rest[1], rest[-1]
        _zero_at_first_step(acc_ref)
        gf = gf_ref[...]
        err = _rms(acc, gf) - t_ref[...]
        out_ref[...], dg = _rms_bwd(err * (1.0 / D), acc, gf)
        acc_ref[0:1, :] += dg
        acc_ref[1:2, :] += jnp.sum(err * err, axis=0, keepdims=True)

        @pl.when(pl.program_id(0) == nt - 1)
        def _():
            acc_ref[1:2, :] = jnp.zeros((1, D), F32) + (0.5 / D) * jnp.sum(acc_ref[1:2, :])

    tok = lambda w: pl.BlockSpec((tm, w), lambda i: (i, 0))
    wsp = lambda k: _resident((None, F, D), lambda i: (k, 0, 0))
    outs, _ = _call(
        body, name=f"mlp_fwd{l}", grid=(nt,),
        in_specs=[tok(D), ROW1(), wsp(0), wsp(1)] + ([tok(D), ROW1()] if head else []),
        out_specs=[tok(D), tok(D), tok(F)] + ([ACC()] if head else []),
        out_shape=[jax.ShapeDtypeStruct((T, D), F32), jax.ShapeDtypeStruct((T, D), BF), jax.ShapeDtypeStruct((T, F), BF)]
        + ([jax.ShapeDtypeStruct((8, D), F32)] if head else []),
        args=(x1, g, w2, w2) + (tuple(head) if head else ()))
    return outs


def _mlp_bwd(dx2, x1, a, g, w2, l, tm, side=None):
    T = dx2.shape[0]
    FC = 1024

    def body(d_ref, x_ref, a_ref, g_ref, wup_ref, wdn_ref, da_ref, dx1_ref, db_ref, acc_ref):
        _zero_at_first_step(acc_ref)
        d = d_ref[...]
        db = d.astype(BF)
        db_ref[...] = db
        dh = jnp.zeros((tm, D), F32)
        for c in range(F // FC):
            du = _dot(db, wdn_ref[c * FC:(c + 1) * FC, :], NT)
            da = (du * (2.0 * jnp.maximum(a_ref[:, c * FC:(c + 1) * FC].astype(F32), 0.0))).astype(BF)
            da_ref[:, c * FC:(c + 1) * FC] = da
            dh = dh + _dot(da, wup_ref[c * FC:(c + 1) * FC, :], NN)
        dx, dg = _rms_bwd(dh, x_ref[...], g_ref[...])
        dx1_ref[...] = d + dx
        acc_ref[0:1, :] += dg

    tok = lambda w: pl.BlockSpec((tm, w), lambda i: (i, 0))
    wsp = lambda k: _resident((None, F, D), lambda i: (k, 0, 0))
    return _call(
        body, name=f"mlp_bwd{l}", grid=(T // tm,),
        in_specs=[tok(D), tok(D), tok(F), ROW1(), wsp(0), wsp(1)],
        out_specs=[tok(F), tok(D), tok(D), ACC()],
        out_shape=[jax.ShapeDtypeStruct((T, F), BF), jax.ShapeDtypeStruct((T, D), F32),
                   jax.ShapeDtypeStruct((T, D), BF), jax.ShapeDtypeStruct((8, D), F32)],
        args=(dx2, x1, a, g, w2, w2), side=side)


def _mixer_bwd(dx1, gt, ya, yc, zb, w3, bg, cw, l, S, tm, side=None):
    T = dx1.shape[0]
    nt = T // tm

    def body(d_ref, cb_ref, cc_ref, cu_ref, ga_ref, gc_ref, ya_ref, yc_ref, zb_ref, wao_ref, wco_ref, wo_ref,
             bg_ref, cw_ref, dg_ref, datt_ref, dya_ref, dyc_ref, db_ref, acc_ref, carry_ref):
        ti = nt - 1 - pl.program_id(0)
        _zero_at_first_step(acc_ref)

        @pl.when(((ti + 1) * tm) % S == 0)
        def _():
            carry_ref[...] = jnp.zeros_like(carry_ref)

        db = d_ref[...].astype(BF)
        db_ref[...] = db
        dm = _dot(db, wo_ref[...], NT)
        sa = jax.nn.sigmoid(ga_ref[...].astype(F32) + bg_ref[0:1, :])
        sc = jax.nn.sigmoid(gc_ref[...].astype(F32) + bg_ref[1:2, :])
        dya32 = dm * sa
        dyc32 = dm * sc
        dya = dya32.astype(BF)
        dyc = dyc32.astype(BF)
        dya_ref[...] = dya
        dyc_ref[...] = dyc
        dga = dya32 * ya_ref[...].astype(F32) * (1.0 - sa)
        dgc = dyc32 * yc_ref[...].astype(F32) * (1.0 - sc)
        dg_ref[:, 3 * D:4 * D] = dga.astype(BF)
        dg_ref[:, 4 * D:5 * D] = dgc.astype(BF)
        acc_ref[0:1, :] += jnp.sum(dga, axis=0, keepdims=True)
        acc_ref[1:2, :] += jnp.sum(dgc, axis=0, keepdims=True)
        datt_ref[...] = _dot(wao_ref[...], dya, NT).astype(BF)
        dco = _dot(dyc, wco_ref[...], NT)

        cc = cc_ref[...].astype(F32)
        cu = cu_ref[...].astype(F32)
        y = cc * cu
        dg_ref[:, 0:D] = (dco * zb_ref[...].astype(F32)).astype(BF)
        dz = dco * cb_ref[...].astype(F32)
        u1 = _shift_rows(dz, 1, [carry_ref[0:1, :]], False)
        u2 = _shift_rows(dz, 2, [carry_ref[0:1, :], carry_ref[1:2, :]], False)
        acc_ref[2:3, :] += jnp.sum(dz, axis=0, keepdims=True)
        acc_ref[3:4, :] += jnp.sum(u2 * y, axis=0, keepdims=True)
        acc_ref[4:5, :] += jnp.sum(u1 * y, axis=0, keepdims=True)
        acc_ref[5:6, :] += jnp.sum(dz * y, axis=0, keepdims=True)
        dy = cw_ref[2:3, :] * dz + cw_ref[1:2, :] * u1 + cw_ref[0:1, :] * u2
        dg_ref[:, D:2 * D] = (dy * cu).astype(BF)
        dg_ref[:, 2 * D:3 * D] = (dy * cc).astype(BF)
        carry_ref[...] = dz[0:8, :]

    tok = lambda w=D: pl.BlockSpec((tm, w), lambda i: (nt - 1 - i, 0))
    seg = lambda s: pl.BlockSpec((tm, D), lambda i: (nt - 1 - i, s))
    wsp = lambda k: _resident((None, D, D), lambda i: (k, 0, 0))
    row = lambda n: _resident((n, D), lambda i: (0, 0))
    return _call(
        body, name=f"mixer_bwd{l}", grid=(nt,),
        in_specs=[tok(), seg(0), seg(1), seg(2), seg(3), seg(4), tok(), tok(), tok(), wsp(0), wsp(1), wsp(2),
                  row(2), row(8)],
        out_specs=[tok(NG), pl.BlockSpec((D, tm), lambda i: (0, nt - 1 - i)), tok(), tok(), tok(), ACC()],
        out_shape=[jax.ShapeDtypeStruct((T, NG), BF), jax.ShapeDtypeStruct((D, T), BF)]
        + [jax.ShapeDtypeStruct((T, D), BF)] * 3 + [jax.ShapeDtypeStruct((8, D), F32)],
        scratch_shapes=[pltpu.VMEM((8, D), F32)],
        args=(dx1, gt, gt, gt, gt, gt, ya, yc, zb, w3, w3, w3, bg, cw), side=side)


def _attn_bwd(q, k, v, att, datt, sm, l, S, side=None):
    T = q.shape[1]
    nblk = S // WIN
    unroll = next(u for u in (5, 3, 1) if (nblk - 1) % u == 0)
    scale = HD ** -0.5

    def body(sm_ref, q_ref, k_ref, v_ref, o_ref, do_ref, dq_ref, dk_ref, dv_ref, ds_ref, dka_ref, dva_ref):
        pj = pl.program_id(1)
        biases = _pair_biases(sm_ref, pj)
        dka_ref[...] = jnp.zeros_like(dka_ref)
        dva_ref[...] = jnp.zeros_like(dva_ref)

        def block(i, first, dsinks):
            r0 = 0 if first else pl.multiple_of(i * WIN, WIN)
            p0 = 0 if first else pl.multiple_of(i * WIN - WIN, WIN)
            out = []
            for kvh in range(2):
                head0 = (2 * pj + kvh) * GQ
                rows = slice(kvh * HD, (kvh + 1) * HD)
                qt, kt, heads = _band_probs(sm_ref, head0, q_ref, k_ref, r0, p0, kvh, biases[first])
                inv = jnp.concatenate([h[2] for h in heads], axis=1)
                dos = _heads_on_lanes(do_ref, kvh, r0).astype(F32) * inv
                delta = jnp.sum(dos * _heads_on_lanes(o_ref, kvh, r0).astype(F32), axis=0, keepdims=True)
                dosb = dos.astype(BF)
                dpt = _dot(v_ref[rows, pl.ds(p0, 2 * WIN)], dosb, TN)
                dst = jnp.concatenate(
                    [(p * (dpt[:, g * WIN:(g + 1) * WIN] - delta[:, g * WIN:(g + 1) * WIN])).astype(BF)
                     for g, (p, _, _) in enumerate(heads)], axis=1)
                pt = jnp.concatenate([p.astype(BF) for p, _, _ in heads], axis=1)
                dqt = _dot(kt, dst, NN) * scale
                for g in range(GQ):
                    hr = slice((kvh * GQ + g) * HD, (kvh * GQ + g + 1) * HD)
                    dq_ref[hr, pl.ds(r0, WIN)] = dqt[:, g * WIN:(g + 1) * WIN].astype(BF)
                dka_ref[rows, pl.ds(p0, 2 * WIN)] += _dot(qt, dst, NT)
                dva_ref[rows, pl.ds(p0, 2 * WIN)] += _dot(dosb, pt, NT)
                ps = jnp.concatenate([h[1] for h in heads], axis=1)
                out.append(dsinks[kvh] - ps * delta)
            return tuple(out)

        zero = jnp.zeros((1, GQ * WIN), F32)
        def rest(t, c):
            for u in range(unroll):
                c = block(1 + unroll * t + u, False, c)
            return c

        dsinks = lax.fori_loop(0, (nblk - 1) // unroll, rest, block(0, True, (zero, zero)))
        for kvh in range(2):
            for g in range(GQ):
                tot = jnp.sum(dsinks[kvh][:, g * WIN:(g + 1) * WIN])
                ds_ref[kvh * GQ + g:kvh * GQ + g + 1, :] = jnp.zeros((1, 128), F32) + tot
        dk_ref[...] = dka_ref[...].astype(BF)
        dv_ref[...] = dva_ref[...].astype(BF)

    wide = lambda: pl.BlockSpec((2 * GQ * HD, S), lambda s, p: (p, s))
    narrow = lambda: pl.BlockSpec((2 * HD, S), lambda s, p: (p, s))
    return _call(
        body, name=f"attn_bwd{l}", grid=(T // S, 2),
        in_specs=[pl.BlockSpec(memory_space=pltpu.SMEM), wide(), narrow(), narrow(), wide(), wide()],
        out_specs=[wide(), narrow(), narrow(), pl.BlockSpec((None, None, 8, 128), lambda s, p: (s, p, 0, 0))],
        out_shape=[jax.ShapeDtypeStruct((NQ, T), BF), jax.ShapeDtypeStruct((NKV, T), BF),
                   jax.ShapeDtypeStruct((NKV, T), BF), jax.ShapeDtypeStruct((T // S, 2, 8, 128), F32)],
        scratch_shapes=[pltpu.VMEM((2 * HD, S), F32), pltpu.VMEM((2 * HD, S), F32)],
        args=(sm, q, k, v, att, datt), side=side)


def _inproj_bwd(dgt, dq, dk, dv, x, dres, g, wint, l, tm, side=None):
    T = x.shape[0]

    def body(dg_ref, dq_ref, dk_ref, dv_ref, x_ref, dr_ref, g_ref, w_ref, dx_ref, acc_ref):
        _zero_at_first_step(acc_ref)
        dh = _dot(dq_ref[...], w_ref[0:NQ, :], TN)
        dh = dh + _dot(dk_ref[...], w_ref[NQ:NQ + NKV, :], TN)
        dh = dh + _dot(dv_ref[...], w_ref[NQ + NKV:NQ + 2 * NKV, :], TN)
        dh = dh + _dot(dg_ref[...], w_ref[NQ + 2 * NKV:NP, :], NN)
        dx, dg = _rms_bwd(dh, x_ref[...], g_ref[...])
        dx_ref[...] = dr_ref[...] + dx
        acc_ref[0:1, :] += dg

    tok = lambda w: pl.BlockSpec((tm, w), lambda i: (i, 0))
    feat = lambda w: pl.BlockSpec((w, tm), lambda i: (0, i))
    return _call(
        body, name=f"inproj_bwd{l}", grid=(T // tm,),
        in_specs=[tok(NG), feat(NQ), feat(NKV), feat(NKV), tok(D), tok(D), ROW1(), _resident((NP, D), lambda i: (0, 0))],
        out_specs=[tok(D), ACC()],
        out_shape=[jax.ShapeDtypeStruct((T, D), F32), jax.ShapeDtypeStruct((8, D), F32)],
        args=(dgt, dq, dk, dv, x, dres, g, wint), side=side)


def _wgrad(a, b, rows, row0, into, name, relu2=False, a_is_transposed=False):
    M, T = a.shape if a_is_transposed else a.shape[::-1]
    tmm = next(t for t in (1024, 512, 256) if M % t == 0 and row0 % t == 0)
    tk = min(4096 if M > 4096 else 2048, T)
    nk = T // tk
    blk0 = row0 // tmm

    def body(*refs):
        a_ref, b_ref = refs[0], refs[1]
        o_ref, acc_ref = refs[-2], refs[-1]
        kk = pl.program_id(1)

        @pl.when(kk == 0)
        def _():
            acc_ref[...] = jnp.zeros_like(acc_ref)

        av = a_ref[...]
        if relu2:
            t = jnp.maximum(av.astype(F32), 0.0)
            av = (t * t).astype(BF)
        acc_ref[...] += _dot(av, b_ref[...], NN if a_is_transposed else TN)

        @pl.when(kk == nk - 1)
        def _():
            o_ref[...] = acc_ref[...].astype(BF)

    a_spec = pl.BlockSpec((tmm, tk), lambda j, kk: (j, kk)) if a_is_transposed else pl.BlockSpec((tk, tmm), lambda j, kk: (kk, j))
    in_specs = [a_spec, pl.BlockSpec((tk, D), lambda j, kk: (kk, 0))]
    args = [a, b]
    if into is not None:
        in_specs.append(ANY)
        args.append(into)
    (out,), _ = _call(
        body, name=name, grid=(M // tmm, nk), in_specs=in_specs,
        out_specs=[pl.BlockSpec((tmm, D), lambda j, kk: (blk0 + j, 0))],
        out_shape=[jax.ShapeDtypeStruct((rows, D), BF)], scratch_shapes=[pltpu.VMEM((tmm, D), F32)],
        aliases={2: 0} if into is not None else None, args=args)
    return out


def _adamw(w, g, m, v):
    m = B1 * m + (1.0 - B1) * g
    v = B2 * v + (1.0 - B2) * (g * g)
    m_hat = m / (1.0 - B1 ** STEP)
    v_hat = v / (1.0 - B2 ** STEP)
    return -LR * (m_hat / (jnp.sqrt(v_hat) + AEPS) + WD * w), m, v


def _adam_sum(land, w, m, v, l, into, name, side=None):
    _, r, _ = land.shape
    tr = 208 if r % 208 == 0 else (256 if r % 256 == 0 else r)

    def body(land_ref, w_ref, m_ref, v_ref, *rest):
        g_ref, d_ref, nm_ref, nv_ref = rest[-4:]
        g = land_ref[0].astype(F32)
        for s in range(1, NDEV):
            g = g + land_ref[s].astype(F32)
        g_ref[...] = g
        d_ref[...], nm_ref[...], nv_ref[...] = _adamw(w_ref[...], g, m_ref[...], v_ref[...])

    blk = lambda: pl.BlockSpec((None, tr, D), lambda j: (l, j, 0))
    in_specs = [pl.BlockSpec((NDEV, tr, D), lambda j: (0, j, 0)), blk(), blk(), blk()]
    args = [land, w, m, v]
    aliases = None
    if into is not None:
        in_specs += [ANY] * 4
        args += list(into)
        aliases = {4 + t: t for t in range(4)}
    return _call(body, name=name, grid=(r // tr,), in_specs=in_specs, out_specs=[blk()] * 4,
                 out_shape=[jax.ShapeDtypeStruct(w.shape, F32)] * 4, aliases=aliases, args=args, side=side)


SMALL_NAMES = ("g_mix", "b_gates", "sinks", "conv_b", "g_mlp", "g_final", "conv_w")


def _adam_small(land, me, masters):
    n = len(SMALL_NAMES)
    lanes = D // NDEV

    def body(land_ref, me_ref, *refs):
        ins, outs, loss_ref, gs_ref = refs[:3 * n], refs[3 * n:7 * n], refs[7 * n], refs[7 * n + 1]
        g = land_ref[0]
        for s in range(1, NDEV):
            g = g + land_ref[s]
        gs_ref[...] = g

        def update(k, g_piece, idx):
            w_ref, m_ref, v_ref = ins[3 * k:3 * k + 3]
            outs[4 * k][idx] = g_piece
            outs[4 * k + 1][idx], outs[4 * k + 2][idx], outs[4 * k + 3][idx] = _adamw(w_ref[idx], g_piece, m_ref[idx], v_ref[idx])

        whole = (slice(None), slice(None))
        update(0, gs_ref[0:2, :], whole)
        for l in range(L):
            for h in range(2):
                update(1, gs_ref[2 + 2 * l + h:3 + 2 * l + h, :], (slice(l, l + 1), slice(h * D, (h + 1) * D)))
            update(2, gs_ref[6:7, 16 * l:16 * (l + 1)], (slice(l, l + 1), slice(None)))
        update(3, gs_ref[7:9, :], whole)
        update(4, gs_ref[9:11, :], whole)
        update(5, gs_ref[11:12, :], whole)
        mine = pl.ds(pl.multiple_of(me_ref[0] * lanes, lanes), lanes)
        for l in range(L):
            for k in range(3):
                update(6, gs_ref[12 + 3 * l + k:13 + 3 * l + k, mine], (l, slice(k, k + 1), slice(None)))
        loss_ref[...] = gs_ref[18:19, 0:1]

    flat = [t for name in SMALL_NAMES for t in masters[name]]
    vmem = pl.BlockSpec(memory_space=pltpu.VMEM)
    outs = pl.pallas_call(
        body, name="adam_small",
        in_specs=[vmem, pl.BlockSpec(memory_space=pltpu.SMEM)] + [vmem] * len(flat),
        out_shape=[jax.ShapeDtypeStruct(masters[name][0].shape, F32) for name in SMALL_NAMES for _ in range(4)]
        + [jax.ShapeDtypeStruct((1, 1), F32)],
        scratch_shapes=[pltpu.VMEM((SMALL_ROWS, D), F32)],
        compiler_params=pltpu.CompilerParams(vmem_limit_bytes=VMEM_LIMIT))(land, me, *flat)
    return {name: outs[4 * k:4 * k + 4] for k, name in enumerate(SMALL_NAMES)}, outs[-1]


def _pack_small(g_mix, b_gates, sinks, conv_b, g_mlp, g_final, conv_w_rows, extra):
    sink_row = jnp.zeros((1, D), F32).at[0, :2 * 16].set(sinks.reshape(-1))
    return jnp.concatenate([g_mix, b_gates.reshape(4, D), sink_row, conv_b, g_mlp, g_final.reshape(1, D),
                            conv_w_rows, extra, jnp.zeros((SMALL_ROWS - 19, D), F32)], axis=0)


def kernel(x, g_mix, w_in, b_gates, sinks, w_attn_out, conv_w, conv_b, w_conv_out, w_o, g_mlp, w_up, w_down, g_final, loss_target, m_g_mix, m_w_in, m_b_gates, m_sinks, m_w_attn_out, m_conv_w, m_conv_b, m_w_conv_out, m_w_o, m_g_mlp, m_w_up, m_w_down, m_g_final, v_g_mix, v_w_in, v_b_gates, v_sinks, v_w_attn_out, v_conv_w, v_conv_b, v_w_conv_out, v_w_o, v_g_mlp, v_w_up, v_w_down, v_g_final):
    nseq, S, _ = x.shape
    T = nseq * S
    tm_in = min(512, S)
    tm = min(256, S)
    xi, yi, ci = _position()
    me = 4 * xi + 2 * yi + ci
    tr = lambda t: jnp.swapaxes(t, 1, 2)
    blocks = lambda t: t.reshape(NDEV, t.shape[0] // NDEV, D)

    win_t, wup_t = tr(w_in), tr(w_up)
    sh_win = [win_t[l].astype(BF)[None] for l in range(L)]
    sh_w3 = [jnp.stack([w_attn_out[l], w_conv_out[l], w_o[l]]).astype(BF) for l in range(L)]
    sh_w2 = [jnp.stack([wup_t[l], w_down[l]]).astype(BF) for l in range(L)]
    wint, w3, w2 = [None] * L, [None] * L, [None] * L
    wint0_g, cw_g = _remote_only(_Gather([sh_win[0], jnp.pad(conv_w, ((0, 0), (0, 5), (0, 0)))]), "gather_first")
    wint[0] = wint0_g.reshape(NP, D)
    cw = jnp.swapaxes(cw_g, 1, 2).reshape(L, 8, D)
    slopes = np.power(np.float32(2.0), -8.0 * np.arange(1, 17, dtype=np.float32) / 16).astype(np.float32)

    xf = x.reshape(T, D)
    saved = []
    cur = xf
    for l in range(L):
        sm = jnp.stack([sinks[l], jnp.asarray(slopes)])
        bg = b_gates[l].reshape(2, D)
        (gt, q, k, v, h), got = _inproj_fwd(cur, g_mix[l:l + 1], wint[l], l, tm_in, side=_Gather([sh_w2[l]]),
                                            relay_at=0.85)
        w2[l] = got[0].reshape(2, F, D)
        (att,), got = _attn_fwd(q, k, v, sm, l, S, side=_Gather([sh_w3[l]]), relay_at=0.5)
        w3[l] = got[0].reshape(3, D, D)
        (x1, mg, co, ya, yc, zb), got = _mixer_fwd(cur, gt, att, w3[l], bg, cw[l], conv_b[l:l + 1], l, S, tm,
                                                   side=_Gather([sh_win[l + 1]]) if l + 1 < L else None, relay_at=0.85)
        if l + 1 < L:
            wint[l + 1] = got[0].reshape(NP, D)
        head = (loss_target.reshape(T, D), g_final.reshape(1, D)) if l == L - 1 else None
        nxt, h2, a, *acc_loss = _mlp_fwd(x1, g_mlp[l:l + 1], w2[l], l, tm_in, head=head)
        saved.append(dict(x=cur, gt=gt, q=q, k=k, v=v, h=h, att=att, x1=x1, mg=mg, co=co, ya=ya, yc=yc, zb=zb, h2=h2, a=a,
                          sm=sm, bg=bg))
        cur = nxt
    dcur, acc_loss = cur, acc_loss[0]

    masters = {"w_in": (win_t, tr(m_w_in), tr(v_w_in)), "w_attn_out": (w_attn_out, m_w_attn_out, v_w_attn_out),
               "w_conv_out": (w_conv_out, m_w_conv_out, v_w_conv_out), "w_o": (w_o, m_w_o, v_w_o),
               "w_up": (wup_t, tr(m_w_up), tr(v_w_up)), "w_down": (w_down, m_w_down, v_w_down)}
    big = {name: None for name in masters}

    def adam(name, land, l, side=None):
        big[name], got = _adam_sum(land, *masters[name], l, big[name], f"adam_{name}{l}", side=side)
        return got

    acc_in, acc_mix, acc_mlp, dsink = [None] * L, [None] * L, [None] * L, [None] * L
    d_win_above = None
    for l in reversed(range(L)):
        sv = saved[l]
        side = _Exchange([blocks(d_win_above)]) if l + 1 < L else None
        (da, dx1, dx2b, acc_mlp[l]), got = _mlp_bwd(dcur, sv["x1"], sv["a"], g_mlp[l:l + 1], w2[l], l, tm_in, side=side)
        if l + 1 < L:
            adam("w_in", got[0], l + 1)
        d_wdn = _wgrad(sv["a"], dx2b, F, 0, None, f"wgrad_down{l}", relu2=True)
        d_wup = _wgrad(da, sv["h2"], F, 0, None, f"wgrad_up{l}")
        (dgt, datt, dya, dyc, dx1b, acc_mix[l]), got = _mixer_bwd(
            dx1, sv["gt"], sv["ya"], sv["yc"], sv["zb"], w3[l], sv["bg"], cw[l], l, S, tm,
            side=_Exchange([blocks(d_wdn)]))
        adam("w_down", got[0], l)
        d_wo = _wgrad(sv["mg"], dx1b, D, 0, None, f"wgrad_o{l}")
        d_wao = _wgrad(sv["att"], dya, D, 0, None, f"wgrad_attn_out{l}", a_is_transposed=True)
        d_wco = _wgrad(sv["co"], dyc, D, 0, None, f"wgrad_conv_out{l}")
        d_win = _wgrad(dgt, sv["h"], NP, NQ + 2 * NKV, None, f"wgrad_gates{l}")
        (dq, dk, dv, dsink[l]), got = _attn_bwd(
            sv["q"], sv["k"], sv["v"], sv["att"], datt, sv["sm"], l, S,
            side=_Exchange([blocks(d_wup), blocks(d_wo), blocks(d_wao), blocks(d_wco)]))
        for name, land in zip(["w_up", "w_o", "w_attn_out", "w_conv_out"], got):
            adam(name, land, l)
        d_win = _wgrad(dq, sv["h"], NP, 0, d_win, f"wgrad_q{l}", a_is_transposed=True)
        d_win = _wgrad(dk, sv["h"], NP, NQ, d_win, f"wgrad_k{l}", a_is_transposed=True)
        d_win = _wgrad(dv, sv["h"], NP, NQ + NKV, d_win, f"wgrad_v{l}", a_is_transposed=True)
        side = _Exchange([blocks(d_win)]) if l == 0 else None
        (dcur, acc_in[l]), got_in = _inproj_bwd(dgt, dq, dk, dv, sv["x"], dx1, g_mix[l:l + 1], wint[l], l, tm_in, side=side)
        d_win_above = d_win

    dsinks = jnp.stack([dsink[l][:, :, :, 0].sum(axis=0).reshape(16) for l in range(L)])
    small = _pack_small(
        jnp.concatenate([acc_in[l][0:1] for l in range(L)]),
        jnp.stack([acc_mix[l][0:2].reshape(2 * D) for l in range(L)]),
        dsinks,
        jnp.concatenate([acc_mix[l][2:3] for l in range(L)]),
        jnp.concatenate([acc_mlp[l][0:1] for l in range(L)]),
        acc_loss[0],
        jnp.concatenate([acc_mix[l][3:6] for l in range(L)]),
        acc_loss[1:2])
    (small_land,) = adam("w_in", got_in[0], 0, side=_Exchange([], [small]))

    row = lambda t: t.reshape(1, D)
    small_out, loss = _adam_small(small_land, me.reshape(1).astype(jnp.int32), {
        "g_mix": (g_mix, m_g_mix, v_g_mix), "b_gates": (b_gates, m_b_gates, v_b_gates), "sinks": (sinks, m_sinks, v_sinks),
        "conv_b": (conv_b, m_conv_b, v_conv_b), "g_mlp": (g_mlp, m_g_mlp, v_g_mlp),
        "g_final": (row(g_final), row(m_g_final), row(v_g_final)), "conv_w": (conv_w, m_conv_w, v_conv_w)})
    small_out["g_final"] = [t.reshape(D) for t in small_out["g_final"]]
    for name in ("w_in", "w_up"):
        big[name] = [tr(o) for o in big[name]]
    order = ["g_mix", "w_in", "b_gates", "sinks", "w_attn_out", "conv_w", "conv_b", "w_conv_out", "w_o", "g_mlp",
             "w_up", "w_down", "g_final"]
    out = [loss.reshape(()), dcur.reshape(nseq, S, D)]
    for kind in range(4):
        for name in order:
            out.append(big[name][kind] if name in big else small_out[name][kind])
    return tuple(out)
```

```python
import numpy as np
import jax
import jax.numpy as jnp
from jax import lax
from jax.experimental import pallas as pl
from jax.experimental.pallas import tpu as pltpu

D = 1024
NG = 5 * D
NQ = 1024
NKV = 256
NP = NQ + 2 * NKV + NG
F = 4096
HD = 64
GQ = 4
WIN = 128
L = 2
NDEV = 8
EPS = 1e-6
NEG = -1e30
SMALL_ROWS = 24
LR, B1, B2, AEPS, WD, STEP = 0.001, 0.9, 0.999, 1e-08, 0.01, 10

BF = jnp.bfloat16
F32 = jnp.float32
MESH = pl.DeviceIdType.MESH
VMEM_LIMIT = 60 * 1024 * 1024
ANY = pl.BlockSpec(memory_space=pl.ANY)

NN = ((1,), (0,))
NT = ((1,), (1,))
TN = ((0,), (0,))


def _dot(a, b, dims):
    return lax.dot_general(a, b, (dims, ((), ())), preferred_element_type=F32)


def _resident(shape, imap):
    return pl.BlockSpec(shape, imap, pipeline_mode=pl.Buffered(1))


def _position():
    return lax.axis_index("x"), lax.axis_index("y"), lax.axis_index("c")


class _Gather:
    def __init__(self, shards):
        n = len(shards)
        self.inputs = list(shards)
        self.out_shape = [jax.ShapeDtypeStruct((s.shape[0], NDEV) + s.shape[1:], s.dtype) for s in shards]
        self.scratch = [pltpu.SemaphoreType.DMA((n, 7)), pltpu.SemaphoreType.DMA((n, 7)), pltpu.SemaphoreType.DMA((n,))]

    def _plan(self, src, dst, sems):
        send_sems, recv_sems, local_sems = sems
        n = len(src)
        x, y, c = _position()
        me, sibling = (x, y, c), (x, y, 1 - c)
        chips = [(1 - x, y), (x, 1 - y), (1 - x, 1 - y)]

        def rows(a, p):
            return dst[a].at[:, 4 * p[0] + 2 * p[1] + p[2]]

        def copy(a, k, block, to, from_src=False):
            return pltpu.make_async_remote_copy(
                src_ref=src[a] if from_src else rows(a, block), dst_ref=rows(a, block),
                send_sem=send_sems.at[a, k], recv_sem=recv_sems.at[a, k], device_id=to, device_id_type=MESH)

        mine = [pltpu.make_async_copy(src[a], rows(a, me), local_sems.at[a]) for a in range(n)]
        first = []
        for a in range(n):
            first.append(copy(a, 0, me, sibling, True))
            first += [copy(a, 1 + t, me, (*chip, c), True) for t, chip in enumerate(chips)]
        return n, c, me, sibling, chips, copy, mine, first

    def start(self, src, dst, sems):
        *_, mine, first = self._plan(src, dst, sems)
        for cp in mine + first:
            cp.start()

    def relay(self, src, dst, sems):
        n, c, me, sibling, chips, copy, _, _ = self._plan(src, dst, sems)
        for t, chip in enumerate(chips):
            for a in range(n):
                copy(a, 1 + t, (*chip, c), me).wait_recv()
                copy(a, 4 + t, (*chip, c), sibling).start()

    def finish(self, src, dst, sems):
        n, c, me, sibling, chips, copy, mine, first = self._plan(src, dst, sems)
        for a in range(n):
            copy(a, 0, sibling, me).wait_recv()
            for t, chip in enumerate(chips):
                copy(a, 4 + t, (*chip, 1 - c), me).wait_recv()
        for cp in first + [copy(a, 4 + t, (*chip, c), sibling) for t, chip in enumerate(chips) for a in range(n)]:
            cp.wait_send()
        for cp in mine:
            cp.wait()


class _Exchange:
    def __init__(self, grads, everyone=()):
        self.inputs = list(grads) + list(everyone)
        self.n_blocked = len(grads)
        n = len(self.inputs)
        self.out_shape = [jax.ShapeDtypeStruct(g.shape, g.dtype) for g in grads]
        self.out_shape += [jax.ShapeDtypeStruct((NDEV,) + e.shape, e.dtype) for e in everyone]
        self.scratch = [pltpu.SemaphoreType.DMA((n, 7)), pltpu.SemaphoreType.DMA((n, 7)), pltpu.SemaphoreType.DMA((n,))]

    def _plan(self, src, land, sems):
        send_sems, recv_sems, local_sems = sems
        x, y, c = _position()
        me = 4 * x + 2 * y + c

        def parts(peer_idx):
            return [(s.at[peer_idx] if a < self.n_blocked else s, land[a].at[me]) for a, s in enumerate(src)]

        local = [pltpu.make_async_copy(s, d, local_sems.at[a]) for a, (s, d) in enumerate(parts(me))]
        sent = []
        for rel in range(1, NDEV):
            px = 1 - x if rel & 4 else x
            py = 1 - y if rel & 2 else y
            pc = 1 - c if rel & 1 else c
            for a, (s, d) in enumerate(parts(4 * px + 2 * py + pc)):
                sent.append(pltpu.make_async_remote_copy(
                    src_ref=s, dst_ref=d, send_sem=send_sems.at[a, rel - 1], recv_sem=recv_sems.at[a, rel - 1],
                    device_id=(px, py, pc), device_id_type=MESH))
        return local, sent

    def start(self, src, land, sems):
        local, sent = self._plan(src, land, sems)
        for cp in local + sent:
            cp.start()

    def relay(self, src, land, sems):
        pass

    def finish(self, src, land, sems):
        local, sent = self._plan(src, land, sems)
        for cp in sent:
            cp.wait_recv()
        for cp in sent:
            cp.wait_send()
        for cp in local:
            cp.wait()


def _call(body, *, name, grid, in_specs, out_specs, out_shape, args, scratch_shapes=(), aliases=None, side=None,
          relay_at=1.0):
    sem = ("arbitrary",) * len(grid)
    if side is None:
        outs = pl.pallas_call(
            body, name=name, grid=grid, in_specs=in_specs, out_specs=out_specs, out_shape=out_shape,
            scratch_shapes=list(scratch_shapes), input_output_aliases=aliases or {},
            compiler_params=pltpu.CompilerParams(dimension_semantics=sem, vmem_limit_bytes=VMEM_LIMIT))(*args)
        return outs, []
    ni, no, ns = len(in_specs), len(out_specs), len(scratch_shapes)
    si, so = len(side.inputs), len(side.out_shape)

    def hosted(*refs):
        ins, refs = refs[:ni], refs[ni:]
        sins, refs = refs[:si], refs[si:]
        outs, refs = refs[:no], refs[no:]
        souts, refs = refs[:so], refs[so:]
        scr, sscr = refs[:ns], refs[ns:]
        step = pl.program_id(0)
        for d in range(1, len(grid)):
            step = step * grid[d] + pl.program_id(d)
        last = int(np.prod(grid)) - 1

        @pl.when(step == 0)
        def _():
            side.start(sins, souts, sscr)

        body(*ins, *outs, *scr)

        @pl.when(step == min(int(relay_at * last), last))
        def _():
            side.relay(sins, souts, sscr)

        @pl.when(step == last)
        def _():
            side.finish(sins, souts, sscr)

    outs = pl.pallas_call(
        hosted, name=name, grid=grid, in_specs=list(in_specs) + [ANY] * si, out_specs=list(out_specs) + [ANY] * so,
        out_shape=list(out_shape) + side.out_shape, scratch_shapes=list(scratch_shapes) + side.scratch,
        input_output_aliases=aliases or {},
        compiler_params=pltpu.CompilerParams(dimension_semantics=sem, vmem_limit_bytes=VMEM_LIMIT, has_side_effects=True),
    )(*args, *side.inputs)
    return outs[:no], outs[no:]


def _remote_only(side, name):
    n = len(side.inputs)

    def body(*refs):
        src, dst, sems = refs[:n], refs[n:n + len(side.out_shape)], refs[n + len(side.out_shape):]
        side.start(src, dst, sems)
        side.relay(src, dst, sems)
        side.finish(src, dst, sems)

    return pl.pallas_call(
        body, name=name, in_specs=[ANY] * n, out_specs=[ANY] * len(side.out_shape), out_shape=side.out_shape,
        scratch_shapes=side.scratch, compiler_params=pltpu.CompilerParams(has_side_effects=True))(*side.inputs)


def _rms(x, g):
    r = lax.rsqrt(jnp.mean(x * x, axis=-1, keepdims=True) + EPS)
    return x * r * g


def _rms_bwd(dy, x, g):
    r = lax.rsqrt(jnp.mean(x * x, axis=-1, keepdims=True) + EPS)
    xh = x * r
    dxh = dy * g
    dx = r * (dxh - xh * jnp.mean(dxh * xh, axis=-1, keepdims=True))
    return dx, jnp.sum(dy * xh, axis=0, keepdims=True)


def _zero_at_first_step(acc_ref):
    first = pl.program_id(0) == 0

    @pl.when(first)
    def _():
        acc_ref[...] = jnp.zeros_like(acc_ref)


ROW1 = lambda: _resident((1, D), lambda i: (0, 0))
ACC = lambda: pl.BlockSpec((8, D), lambda i: (0, 0))


def _inproj_fwd(x, g, wint, l, tm, side=None, relay_at=1.0):
    T = x.shape[0]

    def body(x_ref, g_ref, w_ref, gt_ref, q_ref, k_ref, v_ref, h_ref):
        h = _rms(x_ref[...], g_ref[...]).astype(BF)
        h_ref[...] = h
        q_ref[...] = _dot(w_ref[0:NQ, :], h, NT).astype(BF)
        k_ref[...] = _dot(w_ref[NQ:NQ + NKV, :], h, NT).astype(BF)
        v_ref[...] = _dot(w_ref[NQ + NKV:NQ + 2 * NKV, :], h, NT).astype(BF)
        for s in range(5):
            lo = NQ + 2 * NKV + s * D
            gt_ref[:, s * D:(s + 1) * D] = _dot(h, w_ref[lo:lo + D, :], NT).astype(BF)

    tok = lambda w: pl.BlockSpec((tm, w), lambda i: (i, 0))
    feat = lambda w: pl.BlockSpec((w, tm), lambda i: (0, i))
    return _call(
        body, name=f"inproj_fwd{l}", grid=(T // tm,),
        in_specs=[tok(D), ROW1(), _resident((NP, D), lambda i: (0, 0))],
        out_specs=[tok(NG), feat(NQ), feat(NKV), feat(NKV), tok(D)],
        out_shape=[jax.ShapeDtypeStruct((T, NG), BF)] + [jax.ShapeDtypeStruct((w, T), BF) for w in (NQ, NKV, NKV)]
        + [jax.ShapeDtypeStruct((T, D), BF)],
        args=(x, g, wint), side=side, relay_at=relay_at)


def _band_geometry():
    j = lax.broadcasted_iota(jnp.int32, (2 * WIN, WIN), 0)
    r = lax.broadcasted_iota(jnp.int32, (2 * WIN, WIN), 1)
    dist = WIN + r - j
    dist0 = r - j
    return (dist.astype(F32), (dist >= 0) & (dist < WIN)), (dist0.astype(F32), dist0 >= 0)


def _pair_biases(sm_ref, pj):
    return [[jnp.where(ok, -sm_ref[1, pj * 2 * GQ + h] * dist, NEG) for h in range(2 * GQ)]
            for dist, ok in _band_geometry()]


def _reduce_rows(x, pair, whole):
    while x.shape[0] > 8:
        half = x.shape[0] // 2
        x = pair(x[:half], x[half:])
    return whole(x, axis=0, keepdims=True)


def _heads_on_lanes(ref, kvh, r0):
    return jnp.concatenate([ref[(kvh * GQ + g) * HD:(kvh * GQ + g + 1) * HD, pl.ds(r0, WIN)] for g in range(GQ)], axis=1)


def _band_probs(sm_ref, head0, q_ref, k_ref, r0, p0, kvh, biases):
    qt = _heads_on_lanes(q_ref, kvh, r0) * jnp.asarray(HD ** -0.5, BF)
    kt = k_ref[kvh * HD:(kvh + 1) * HD, pl.ds(p0, 2 * WIN)]
    st = _dot(kt, qt, TN)
    heads = []
    for g in range(GQ):
        sink = sm_ref[0, head0 + g]
        s = st[:, g * WIN:(g + 1) * WIN] + biases[kvh * GQ + g]
        m = jnp.maximum(_reduce_rows(s, jnp.maximum, jnp.max), sink)
        p = jnp.exp(s - m)
        ps = jnp.exp(sink - m)
        heads.append((p, ps, 1.0 / (_reduce_rows(p, jnp.add, jnp.sum) + ps)))
    return qt, kt, heads


def _attn_fwd(q, k, v, sm, l, S, side=None, relay_at=1.0):
    T = q.shape[1]
    nblk = S // WIN
    unroll = next(u for u in (5, 3, 1) if (nblk - 1) % u == 0)

    def body(sm_ref, q_ref, k_ref, v_ref, o_ref):
        pj = pl.program_id(1)
        biases = _pair_biases(sm_ref, pj)

        def block(i, first):
            r0 = 0 if first else pl.multiple_of(i * WIN, WIN)
            p0 = 0 if first else pl.multiple_of(i * WIN - WIN, WIN)
            for kvh in range(2):
                head0 = (2 * pj + kvh) * GQ
                _, _, heads = _band_probs(sm_ref, head0, q_ref, k_ref, r0, p0, kvh, biases[first])
                pt = jnp.concatenate([p.astype(BF) for p, _, _ in heads], axis=1)
                ot = _dot(v_ref[kvh * HD:(kvh + 1) * HD, pl.ds(p0, 2 * WIN)], pt, NN)
                for g in range(GQ):
                    rows = slice((kvh * GQ + g) * HD, (kvh * GQ + g + 1) * HD)
                    o_ref[rows, pl.ds(r0, WIN)] = (ot[:, g * WIN:(g + 1) * WIN] * heads[g][2]).astype(BF)

        block(0, True)

        def rest(t, c):
            for u in range(unroll):
                block(1 + unroll * t + u, False)
            return c

        lax.fori_loop(0, (nblk - 1) // unroll, rest, 0)

    wide = lambda: pl.BlockSpec((2 * GQ * HD, S), lambda s, p: (p, s))
    narrow = lambda: pl.BlockSpec((2 * HD, S), lambda s, p: (p, s))
    return _call(
        body, name=f"attn_fwd{l}", grid=(T // S, 2),
        in_specs=[pl.BlockSpec(memory_space=pltpu.SMEM), wide(), narrow(), narrow()],
        out_specs=[wide()], out_shape=[jax.ShapeDtypeStruct((D, T), BF)],
        args=(sm, q, k, v), side=side, relay_at=relay_at)


def _shift_rows(y, k, edge_rows, down):
    n = y.shape[0]
    rid = lax.broadcasted_iota(jnp.int32, y.shape, 0)
    out = pltpu.roll(y, k if down else n - k, 0)
    for t, row in enumerate(edge_rows):
        out = jnp.where(rid == (t if down else n - k + t), row, out)
    return out


def _mixer_fwd(x, gt, att, w3, bg, cw, cbias, l, S, tm, side=None, relay_at=1.0):
    T = x.shape[0]

    def body(x_ref, cb_ref, cc_ref, cu_ref, ga_ref, gc_ref, cch_ref, cuh_ref, att_ref, wao_ref, wco_ref, wo_ref,
             bg_ref, cw_ref, cbias_ref, x1_ref, mg_ref, co_ref, ya_ref, yc_ref, zb_ref):
        first = (pl.program_id(0) * tm) % S == 0
        y = cc_ref[...].astype(F32) * cu_ref[...].astype(F32)
        hy1 = cch_ref[15:16, :].astype(F32) * cuh_ref[15:16, :].astype(F32)
        hy2 = cch_ref[14:15, :].astype(F32) * cuh_ref[14:15, :].astype(F32)
        hy1, hy2 = jnp.where(first, 0.0, hy1), jnp.where(first, 0.0, hy2)
        z = (cw_ref[0:1, :] * _shift_rows(y, 2, [hy2, hy1], True) + cw_ref[1:2, :] * _shift_rows(y, 1, [hy1], True)
             + cw_ref[2:3, :] * y)
        zb = z + cbias_ref[...]
        zb_ref[...] = zb.astype(BF)
        co = (cb_ref[...].astype(F32) * zb).astype(BF)
        co_ref[...] = co
        yc = _dot(co, wco_ref[...], NN)
        ya = _dot(att_ref[...], wao_ref[...], TN)
        ya_ref[...] = ya.astype(BF)
        yc_ref[...] = yc.astype(BF)
        sa = jax.nn.sigmoid(ga_ref[...].astype(F32) + bg_ref[0:1, :])
        sc = jax.nn.sigmoid(gc_ref[...].astype(F32) + bg_ref[1:2, :])
        mg = (sa * ya + sc * yc).astype(BF)
        mg_ref[...] = mg
        x1_ref[...] = x_ref[...] + _dot(mg, wo_ref[...], NN)

    tok = lambda: pl.BlockSpec((tm, D), lambda i: (i, 0))
    seg = lambda s: pl.BlockSpec((tm, D), lambda i: (i, s))
    halo = lambda s: pl.BlockSpec((16, D), lambda i: (jnp.maximum(i * (tm // 16) - 1, 0), s))
    wsp = lambda k: _resident((None, D, D), lambda i: (k, 0, 0))
    row = lambda n: _resident((n, D), lambda i: (0, 0))
    return _call(
        body, name=f"mixer_fwd{l}", grid=(T // tm,),
        in_specs=[tok(), seg(0), seg(1), seg(2), seg(3), seg(4), halo(1), halo(2),
                  pl.BlockSpec((D, tm), lambda i: (0, i)), wsp(0), wsp(1), wsp(2), row(2), row(8), row(1)],
        out_specs=[tok()] * 6,
        out_shape=[jax.ShapeDtypeStruct((T, D), dt) for dt in (F32, BF, BF, BF, BF, BF)],
        args=(x, gt, gt, gt, gt, gt, gt, gt, att, w3, w3, w3, bg, cw, cbias), side=side, relay_at=relay_at)


def _mlp_fwd(x1, g, w2, l, tm, head=None):
    T = x1.shape[0]
    nt = T // tm
    FC = 1024

    def body(x_ref, g_ref, wup_ref, wdn_ref, *rest):
        out_ref, h_ref, a_ref = rest[-4:-1] if head else rest
        x = x_ref[...]
        h = _rms(x, g_ref[...]).astype(BF)
        h_ref[...] = h
        acc = x
        for c in range(F // FC):
            a = _dot(h, wup_ref[c * FC:(c + 1) * FC, :], NT)
            a_ref[:, c * FC:(c + 1) * FC] = a.astype(BF)
            u = jnp.maximum(a, 0.0)
            acc = acc + _dot((u * u).astype(BF), wdn_ref[c * FC:(c + 1) * FC, :], NN)
        if not head:
            out_ref[...] = acc
            return
        t_ref, gf_ref, acc_ref = rest[0], rest[1], rest[-1]
        _zero_at_first_step(acc_ref)
        gf = gf_ref[...]
        err = _rms(acc, gf) - t_ref[...]
        out_ref[...], dg = _rms_bwd(err * (1.0 / D), acc, gf)
        acc_ref[0:1, :] += dg
        acc_ref[1:2, :] += jnp.sum(err * err, axis=0, keepdims=True)

        @pl.when(pl.program_id(0) == nt - 1)
        def _():
            acc_ref[1:2, :] = jnp.zeros((1, D), F32) + (0.5 / D) * jnp.sum(acc_ref[1:2, :])

    tok = lambda w: pl.BlockSpec((tm, w), lambda i: (i, 0))
    wsp = lambda k: _resident((None, F, D), lambda i: (k, 0, 0))
    outs, _ = _call(
        body, name=f"mlp_fwd{l}", grid=(nt,),
        in_specs=[tok(D), ROW1(), wsp(0), wsp(1)] + ([tok(D), ROW1()] if head else []),
        out_specs=[tok(D), tok(D), tok(F)] + ([ACC()] if head else []),
        out_shape=[jax.ShapeDtypeStruct((T, D), F32), jax.ShapeDtypeStruct((T, D), BF), jax.ShapeDtypeStruct((T, F), BF)]
        + ([jax.ShapeDtypeStruct((8, D), F32)] if head else []),
        args=(x1, g, w2, w2) + (tuple(head) if head else ()))
    return outs


def _mlp_bwd(dx2, x1, a, g, w2, l, tm, side=None):
    T = dx2.shape[0]
    FC = 1024

    def body(d_ref, x_ref, a_ref, g_ref, wup_ref, wdn_ref, da_ref, dx1_ref, db_ref, acc_ref):
        _zero_at_first_step(acc_ref)
        d = d_ref[...]
        db = d.astype(BF)
        db_ref[...] = db
        dh = jnp.zeros((tm, D), F32)
        for c in range(F // FC):
            du = _dot(db, wdn_ref[c * FC:(c + 1) * FC, :], NT)
            da = (du * (2.0 * jnp.maximum(a_ref[:, c * FC:(c + 1) * FC].astype(F32), 0.0))).astype(BF)
            da_ref[:, c * FC:(c + 1) * FC] = da
            dh = dh + _dot(da, wup_ref[c * FC:(c + 1) * FC, :], NN)
        dx, dg = _rms_bwd(dh, x_ref[...], g_ref[...])
        dx1_ref[...] = d + dx
        acc_ref[0:1, :] += dg

    tok = lambda w: pl.BlockSpec((tm, w), lambda i: (i, 0))
    wsp = lambda k: _resident((None, F, D), lambda i: (k, 0, 0))
    return _call(
        body, name=f"mlp_bwd{l}", grid=(T // tm,),
        in_specs=[tok(D), tok(D), tok(F), ROW1(), wsp(0), wsp(1)],
        out_specs=[tok(F), tok(D), tok(D), ACC()],
        out_shape=[jax.ShapeDtypeStruct((T, F), BF), jax.ShapeDtypeStruct((T, D), F32),
                   jax.ShapeDtypeStruct((T, D), BF), jax.ShapeDtypeStruct((8, D), F32)],
        args=(dx2, x1, a, g, w2, w2), side=side)


def _mixer_bwd(dx1, gt, ya, yc, zb, w3, bg, cw, l, S, tm, side=None):
    T = dx1.shape[0]
    nt = T // tm

    def body(d_ref, cb_ref, cc_ref, cu_ref, ga_ref, gc_ref, ya_ref, yc_ref, zb_ref, wao_ref, wco_ref, wo_ref,
             bg_ref, cw_ref, dg_ref, datt_ref, dya_ref, dyc_ref, db_ref, acc_ref, carry_ref):
        ti = nt - 1 - pl.program_id(0)
        _zero_at_first_step(acc_ref)

        @pl.when(((ti + 1) * tm) % S == 0)
        def _():
            carry_ref[...] = jnp.zeros_like(carry_ref)

        db = d_ref[...].astype(BF)
        db_ref[...] = db
        dm = _dot(db, wo_ref[...], NT)
        sa = jax.nn.sigmoid(ga_ref[...].astype(F32) + bg_ref[0:1, :])
        sc = jax.nn.sigmoid(gc_ref[...].astype(F32) + bg_ref[1:2, :])
        dya32 = dm * sa
        dyc32 = dm * sc
        dya = dya32.astype(BF)
        dyc = dyc32.astype(BF)
        dya_ref[...] = dya
        dyc_ref[...] = dyc
        dga = dya32 * ya_ref[...].astype(F32) * (1.0 - sa)
        dgc = dyc32 * yc_ref[...].astype(F32) * (1.0 - sc)
        dg_ref[:, 3 * D:4 * D] = dga.astype(BF)
        dg_ref[:, 4 * D:5 * D] = dgc.astype(BF)
        acc_ref[0:1, :] += jnp.sum(dga, axis=0, keepdims=True)
        acc_ref[1:2, :] += jnp.sum(dgc, axis=0, keepdims=True)
        datt_ref[...] = _dot(wao_ref[...], dya, NT).astype(BF)
        dco = _dot(dyc, wco_ref[...], NT)

        cc = cc_ref[...].astype(F32)
        cu = cu_ref[...].astype(F32)
        y = cc * cu
        dg_ref[:, 0:D] = (dco * zb_ref[...].astype(F32)).astype(BF)
        dz = dco * cb_ref[...].astype(F32)
        u1 = _shift_rows(dz, 1, [carry_ref[0:1, :]], False)
        u2 = _shift_rows(dz, 2, [carry_ref[0:1, :], carry_ref[1:2, :]], False)
        acc_ref[2:3, :] += jnp.sum(dz, axis=0, keepdims=True)
        acc_ref[3:4, :] += jnp.sum(u2 * y, axis=0, keepdims=True)
        acc_ref[4:5, :] += jnp.sum(u1 * y, axis=0, keepdims=True)
        acc_ref[5:6, :] += jnp.sum(dz * y, axis=0, keepdims=True)
        dy = cw_ref[2:3, :] * dz + cw_ref[1:2, :] * u1 + cw_ref[0:1, :] * u2
        dg_ref[:, D:2 * D] = (dy * cu).astype(BF)
        dg_ref[:, 2 * D:3 * D] = (dy * cc).astype(BF)
        carry_ref[...] = dz[0:8, :]

    tok = lambda w=D: pl.BlockSpec((tm, w), lambda i: (nt - 1 - i, 0))
    seg = lambda s: pl.BlockSpec((tm, D), lambda i: (nt - 1 - i, s))
    wsp = lambda k: _resident((None, D, D), lambda i: (k, 0, 0))
    row = lambda n: _resident((n, D), lambda i: (0, 0))
    return _call(
        body, name=f"mixer_bwd{l}", grid=(nt,),
        in_specs=[tok(), seg(0), seg(1), seg(2), seg(3), seg(4), tok(), tok(), tok(), wsp(0), wsp(1), wsp(2),
                  row(2), row(8)],
        out_specs=[tok(NG), pl.BlockSpec((D, tm), lambda i: (0, nt - 1 - i)), tok(), tok(), tok(), ACC()],
        out_shape=[jax.ShapeDtypeStruct((T, NG), BF), jax.ShapeDtypeStruct((D, T), BF)]
        + [jax.ShapeDtypeStruct((T, D), BF)] * 3 + [jax.ShapeDtypeStruct((8, D), F32)],
        scratch_shapes=[pltpu.VMEM((8, D), F32)],
        args=(dx1, gt, gt, gt, gt, gt, ya, yc, zb, w3, w3, w3, bg, cw), side=side)


def _attn_bwd(q, k, v, att, datt, sm, l, S, side=None):
    T = q.shape[1]
    nblk = S // WIN
    unroll = next(u for u in (5, 3, 1) if (nblk - 1) % u == 0)
    scale = HD ** -0.5

    def body(sm_ref, q_ref, k_ref, v_ref, o_ref, do_ref, dq_ref, dk_ref, dv_ref, ds_ref, dka_ref, dva_ref):
        pj = pl.program_id(1)
        biases = _pair_biases(sm_ref, pj)
        dka_ref[...] = jnp.zeros_like(dka_ref)
        dva_ref[...] = jnp.zeros_like(dva_ref)

        def block(i, first, dsinks):
            r0 = 0 if first else pl.multiple_of(i * WIN, WIN)
            p0 = 0 if first else pl.multiple_of(i * WIN - WIN, WIN)
            out = []
            for kvh in range(2):
                head0 = (2 * pj + kvh) * GQ
                rows = slice(kvh * HD, (kvh + 1) * HD)
                qt, kt, heads = _band_probs(sm_ref, head0, q_ref, k_ref, r0, p0, kvh, biases[first])
                inv = jnp.concatenate([h[2] for h in heads], axis=1)
                dos = _heads_on_lanes(do_ref, kvh, r0).astype(F32) * inv
                delta = jnp.sum(dos * _heads_on_lanes(o_ref, kvh, r0).astype(F32), axis=0, keepdims=True)
                dosb = dos.astype(BF)
                dpt = _dot(v_ref[rows, pl.ds(p0, 2 * WIN)], dosb, TN)
                dst = jnp.concatenate(
                    [(p * (dpt[:, g * WIN:(g + 1) * WIN] - delta[:, g * WIN:(g + 1) * WIN])).astype(BF)
                     for g, (p, _, _) in enumerate(heads)], axis=1)
                pt = jnp.concatenate([p.astype(BF) for p, _, _ in heads], axis=1)
                dqt = _dot(kt, dst, NN) * scale
                for g in range(GQ):
                    hr = slice((kvh * GQ + g) * HD, (kvh * GQ + g + 1) * HD)
                    dq_ref[hr, pl.ds(r0, WIN)] = dqt[:, g * WIN:(g + 1) * WIN].astype(BF)
                dka_ref[rows, pl.ds(p0, 2 * WIN)] += _dot(qt, dst, NT)
                dva_ref[rows, pl.ds(p0, 2 * WIN)] += _dot(dosb, pt, NT)
                ps = jnp.concatenate([h[1] for h in heads], axis=1)
                out.append(dsinks[kvh] - ps * delta)
            return tuple(out)

        zero = jnp.zeros((1, GQ * WIN), F32)
        def rest(t, c):
            for u in range(unroll):
                c = block(1 + unroll * t + u, False, c)
            return c

        dsinks = lax.fori_loop(0, (nblk - 1) // unroll, rest, block(0, True, (zero, zero)))
        for kvh in range(2):
            for g in range(GQ):
                tot = jnp.sum(dsinks[kvh][:, g * WIN:(g + 1) * WIN])
                ds_ref[kvh * GQ + g:kvh * GQ + g + 1, :] = jnp.zeros((1, 128), F32) + tot
        dk_ref[...] = dka_ref[...].astype(BF)
        dv_ref[...] = dva_ref[...].astype(BF)

    wide = lambda: pl.BlockSpec((2 * GQ * HD, S), lambda s, p: (p, s))
    narrow = lambda: pl.BlockSpec((2 * HD, S), lambda s, p: (p, s))
    return _call(
        body, name=f"attn_bwd{l}", grid=(T // S, 2),
        in_specs=[pl.BlockSpec(memory_space=pltpu.SMEM), wide(), narrow(), narrow(), wide(), wide()],
        out_specs=[wide(), narrow(), narrow(), pl.BlockSpec((None, None, 8, 128), lambda s, p: (s, p, 0, 0))],
        out_shape=[jax.ShapeDtypeStruct((NQ, T), BF), jax.ShapeDtypeStruct((NKV, T), BF),
                   jax.ShapeDtypeStruct((NKV, T), BF), jax.ShapeDtypeStruct((T // S, 2, 8, 128), F32)],
        scratch_shapes=[pltpu.VMEM((2 * HD, S), F32), pltpu.VMEM((2 * HD, S), F32)],
        args=(sm, q, k, v, att, datt), side=side)


def _inproj_bwd(dgt, dq, dk, dv, x, dres, g, wint, l, tm, side=None):
    T = x.shape[0]

    def body(dg_ref, dq_ref, dk_ref, dv_ref, x_ref, dr_ref, g_ref, w_ref, dx_ref, acc_ref):
        _zero_at_first_step(acc_ref)
        dh = _dot(dq_ref[...], w_ref[0:NQ, :], TN)
        dh = dh + _dot(dk_ref[...], w_ref[NQ:NQ + NKV, :], TN)
        dh = dh + _dot(dv_ref[...], w_ref[NQ + NKV:NQ + 2 * NKV, :], TN)
        dh = dh + _dot(dg_ref[...], w_ref[NQ + 2 * NKV:NP, :], NN)
        dx, dg = _rms_bwd(dh, x_ref[...], g_ref[...])
        dx_ref[...] = dr_ref[...] + dx
        acc_ref[0:1, :] += dg

    tok = lambda w: pl.BlockSpec((tm, w), lambda i: (i, 0))
    feat = lambda w: pl.BlockSpec((w, tm), lambda i: (0, i))
    return _call(
        body, name=f"inproj_bwd{l}", grid=(T // tm,),
        in_specs=[tok(NG), feat(NQ), feat(NKV), feat(NKV), tok(D), tok(D), ROW1(), _resident((NP, D), lambda i: (0, 0))],
        out_specs=[tok(D), ACC()],
        out_shape=[jax.ShapeDtypeStruct((T, D), F32), jax.ShapeDtypeStruct((8, D), F32)],
        args=(dgt, dq, dk, dv, x, dres, g, wint), side=side)


def _wgrad(a, b, rows, row0, into, name, relu2=False, a_is_transposed=False):
    M, T = a.shape if a_is_transposed else a.shape[::-1]
    tmm = next(t for t in (1024, 512, 256) if M % t == 0 and row0 % t == 0)
    tk = min(4096 if M > 4096 else 2048, T)
    nk = T // tk
    blk0 = row0 // tmm

    def body(*refs):
        a_ref, b_ref = refs[0], refs[1]
        o_ref, acc_ref = refs[-2], refs[-1]
        kk = pl.program_id(1)

        @pl.when(kk == 0)
        def _():
            acc_ref[...] = jnp.zeros_like(acc_ref)

        av = a_ref[...]
        if relu2:
            t = jnp.maximum(av.astype(F32), 0.0)
            av = (t * t).astype(BF)
        acc_ref[...] += _dot(av, b_ref[...], NN if a_is_transposed else TN)

        @pl.when(kk == nk - 1)
        def _():
            o_ref[...] = acc_ref[...].astype(BF)

    a_spec = pl.BlockSpec((tmm, tk), lambda j, kk: (j, kk)) if a_is_transposed else pl.BlockSpec((tk, tmm), lambda j, kk: (kk, j))
    in_specs = [a_spec, pl.BlockSpec((tk, D), lambda j, kk: (kk, 0))]
    args = [a, b]
    if into is not None:
        in_specs.append(ANY)
        args.append(into)
    (out,), _ = _call(
        body, name=name, grid=(M // tmm, nk), in_specs=in_specs,
        out_specs=[pl.BlockSpec((tmm, D), lambda j, kk: (blk0 + j, 0))],
        out_shape=[jax.ShapeDtypeStruct((rows, D), BF)], scratch_shapes=[pltpu.VMEM((tmm, D), F32)],
        aliases={2: 0} if into is not None else None, args=args)
    return out


def _adamw(w, g, m, v):
    m = B1 * m + (1.0 - B1) * g
    v = B2 * v + (1.0 - B2) * (g * g)
    m_hat = m / (1.0 - B1 ** STEP)
    v_hat = v / (1.0 - B2 ** STEP)
    return -LR * (m_hat / (jnp.sqrt(v_hat) + AEPS) + WD * w), m, v


def _adam_sum(land, w, m, v, l, into, name):
    _, r, _ = land.shape
    tr = 208 if r % 208 == 0 else (256 if r % 256 == 0 else r)

    def body(land_ref, w_ref, m_ref, v_ref, *rest):
        g_ref, d_ref, nm_ref, nv_ref = rest[-4:]
        g = land_ref[0].astype(F32)
        for s in range(1, NDEV):
            g = g + land_ref[s].astype(F32)
        g_ref[...] = g
        d_ref[...], nm_ref[...], nv_ref[...] = _adamw(w_ref[...], g, m_ref[...], v_ref[...])

    blk = lambda: pl.BlockSpec((None, tr, D), lambda j: (l, j, 0))
    in_specs = [pl.BlockSpec((NDEV, tr, D), lambda j: (0, j, 0)), blk(), blk(), blk()]
    args = [land, w, m, v]
    aliases = None
    if into is not None:
        in_specs += [ANY] * 4
        args += list(into)
        aliases = {4 + t: t for t in range(4)}
    outs, _ = _call(body, name=name, grid=(r // tr,), in_specs=in_specs, out_specs=[blk()] * 4,
                    out_shape=[jax.ShapeDtypeStruct(w.shape, F32)] * 4, aliases=aliases, args=args)
    return outs


SMALL_NAMES = ("g_mix", "b_gates", "sinks", "conv_b", "g_mlp", "g_final", "conv_w")


def _adam_small(land, me, masters):
    n = len(SMALL_NAMES)
    lanes = D // NDEV

    def body(land_ref, me_ref, *refs):
        ins, outs, loss_ref, gs_ref = refs[:3 * n], refs[3 * n:7 * n], refs[7 * n], refs[7 * n + 1]
        g = land_ref[0]
        for s in range(1, NDEV):
            g = g + land_ref[s]
        gs_ref[...] = g

        def update(k, g_piece, idx):
            w_ref, m_ref, v_ref = ins[3 * k:3 * k + 3]
            outs[4 * k][idx] = g_piece
            outs[4 * k + 1][idx], outs[4 * k + 2][idx], outs[4 * k + 3][idx] = _adamw(w_ref[idx], g_piece, m_ref[idx], v_ref[idx])

        whole = (slice(None), slice(None))
        update(0, gs_ref[0:2, :], whole)
        for l in range(L):
            for h in range(2):
                update(1, gs_ref[2 + 2 * l + h:3 + 2 * l + h, :], (slice(l, l + 1), slice(h * D, (h + 1) * D)))
            update(2, gs_ref[6:7, 16 * l:16 * (l + 1)], (slice(l, l + 1), slice(None)))
        update(3, gs_ref[7:9, :], whole)
        update(4, gs_ref[9:11, :], whole)
        update(5, gs_ref[11:12, :], whole)
        mine = pl.ds(pl.multiple_of(me_ref[0] * lanes, lanes), lanes)
        for l in range(L):
            for k in range(3):
                update(6, gs_ref[12 + 3 * l + k:13 + 3 * l + k, mine], (l, slice(k, k + 1), slice(None)))
        loss_ref[...] = gs_ref[18:19, 0:1]

    flat = [t for name in SMALL_NAMES for t in masters[name]]
    vmem = pl.BlockSpec(memory_space=pltpu.VMEM)
    outs = pl.pallas_call(
        body, name="adam_small",
        in_specs=[vmem, pl.BlockSpec(memory_space=pltpu.SMEM)] + [vmem] * len(flat),
        out_shape=[jax.ShapeDtypeStruct(masters[name][0].shape, F32) for name in SMALL_NAMES for _ in range(4)]
        + [jax.ShapeDtypeStruct((1, 1), F32)],
        scratch_shapes=[pltpu.VMEM((SMALL_ROWS, D), F32)],
        compiler_params=pltpu.CompilerParams(vmem_limit_bytes=VMEM_LIMIT))(land, me, *flat)
    return {name: outs[4 * k:4 * k + 4] for k, name in enumerate(SMALL_NAMES)}, outs[-1]


def _pack_small(g_mix, b_gates, sinks, conv_b, g_mlp, g_final, conv_w_rows, extra):
    sink_row = jnp.zeros((1, D), F32).at[0, :2 * 16].set(sinks.reshape(-1))
    return jnp.concatenate([g_mix, b_gates.reshape(4, D), sink_row, conv_b, g_mlp, g_final.reshape(1, D),
                            conv_w_rows, extra, jnp.zeros((SMALL_ROWS - 19, D), F32)], axis=0)


def kernel(x, g_mix, w_in, b_gates, sinks, w_attn_out, conv_w, conv_b, w_conv_out, w_o, g_mlp, w_up, w_down, g_final, loss_target, m_g_mix, m_w_in, m_b_gates, m_sinks, m_w_attn_out, m_conv_w, m_conv_b, m_w_conv_out, m_w_o, m_g_mlp, m_w_up, m_w_down, m_g_final, v_g_mix, v_w_in, v_b_gates, v_sinks, v_w_attn_out, v_conv_w, v_conv_b, v_w_conv_out, v_w_o, v_g_mlp, v_w_up, v_w_down, v_g_final):
    nseq, S, _ = x.shape
    T = nseq * S
    tm_in = min(512, S)
    tm = min(512, S)
    xi, yi, ci = _position()
    me = 4 * xi + 2 * yi + ci
    tr = lambda t: jnp.swapaxes(t, 1, 2)
    blocks = lambda t: t.reshape(NDEV, t.shape[0] // NDEV, D)

    win_t, wup_t = tr(w_in), tr(w_up)
    sh_win = [win_t[l].astype(BF)[None] for l in range(L)]
    sh_w3 = [jnp.stack([w_attn_out[l], w_conv_out[l], w_o[l]]).astype(BF) for l in range(L)]
    sh_w2 = [jnp.stack([wup_t[l], w_down[l]]).astype(BF) for l in range(L)]
    wint, w3, w2 = [None] * L, [None] * L, [None] * L
    wint0_g, cw_g = _remote_only(_Gather([sh_win[0], jnp.pad(conv_w, ((0, 0), (0, 5), (0, 0)))]), "gather_first")
    wint[0] = wint0_g.reshape(NP, D)
    cw = jnp.swapaxes(cw_g, 1, 2).reshape(L, 8, D)
    slopes = np.power(np.float32(2.0), -8.0 * np.arange(1, 17, dtype=np.float32) / 16).astype(np.float32)

    xf = x.reshape(T, D)
    saved = []
    cur = xf
    for l in range(L):
        sm = jnp.stack([sinks[l], jnp.asarray(slopes)])
        bg = b_gates[l].reshape(2, D)
        (gt, q, k, v, h), got = _inproj_fwd(cur, g_mix[l:l + 1], wint[l], l, tm_in, side=_Gather([sh_w2[l]]),
                                            relay_at=0.85)
        w2[l] = got[0].reshape(2, F, D)
        (att,), got = _attn_fwd(q, k, v, sm, l, S, side=_Gather([sh_w3[l]]), relay_at=0.5)
        w3[l] = got[0].reshape(3, D, D)
        (x1, mg, co, ya, yc, zb), got = _mixer_fwd(cur, gt, att, w3[l], bg, cw[l], conv_b[l:l + 1], l, S, tm,
                                                   side=_Gather([sh_win[l + 1]]) if l + 1 < L else None, relay_at=0.85)
        if l + 1 < L:
            wint[l + 1] = got[0].reshape(NP, D)
        head = (loss_target.reshape(T, D), g_final.reshape(1, D)) if l == L - 1 else None
        nxt, h2, a, *acc_loss = _mlp_fwd(x1, g_mlp[l:l + 1], w2[l], l, tm_in, head=head)
        saved.append(dict(x=cur, gt=gt, q=q, k=k, v=v, h=h, att=att, x1=x1, mg=mg, co=co, ya=ya, yc=yc, zb=zb, h2=h2, a=a,
                          sm=sm, bg=bg))
        cur = nxt
    dcur, acc_loss = cur, acc_loss[0]

    masters = {"w_in": (win_t, tr(m_w_in), tr(v_w_in)), "w_attn_out": (w_attn_out, m_w_attn_out, v_w_attn_out),
               "w_conv_out": (w_conv_out, m_w_conv_out, v_w_conv_out), "w_o": (w_o, m_w_o, v_w_o),
               "w_up": (wup_t, tr(m_w_up), tr(v_w_up)), "w_down": (w_down, m_w_down, v_w_down)}
    big = {name: None for name in masters}

    def adam(name, land, l):
        big[name] = _adam_sum(land, *masters[name], l, big[name], f"adam_{name}{l}")

    acc_in, acc_mix, acc_mlp, dsink = [None] * L, [None] * L, [None] * L, [None] * L
    d_win_above = None
    for l in reversed(range(L)):
        sv = saved[l]
        side = _Exchange([blocks(d_win_above)]) if l + 1 < L else None
        (da, dx1, dx2b, acc_mlp[l]), got = _mlp_bwd(dcur, sv["x1"], sv["a"], g_mlp[l:l + 1], w2[l], l, tm_in, side=side)
        if l + 1 < L:
            adam("w_in", got[0], l + 1)
        d_wdn = _wgrad(sv["a"], dx2b, F, 0, None, f"wgrad_down{l}", relu2=True)
        d_wup = _wgrad(da, sv["h2"], F, 0, None, f"wgrad_up{l}")
        (dgt, datt, dya, dyc, dx1b, acc_mix[l]), got = _mixer_bwd(
            dx1, sv["gt"], sv["ya"], sv["yc"], sv["zb"], w3[l], sv["bg"], cw[l], l, S, tm,
            side=_Exchange([blocks(d_wdn)]))
        adam("w_down", got[0], l)
        d_wo = _wgrad(sv["mg"], dx1b, D, 0, None, f"wgrad_o{l}")
        d_wao = _wgrad(sv["att"], dya, D, 0, None, f"wgrad_attn_out{l}", a_is_transposed=True)
        d_wco = _wgrad(sv["co"], dyc, D, 0, None, f"wgrad_conv_out{l}")
        d_win = _wgrad(dgt, sv["h"], NP, NQ + 2 * NKV, None, f"wgrad_gates{l}")
        (dq, dk, dv, dsink[l]), got = _attn_bwd(
            sv["q"], sv["k"], sv["v"], sv["att"], datt, sv["sm"], l, S,
            side=_Exchange([blocks(d_wup), blocks(d_wo), blocks(d_wao), blocks(d_wco)]))
        for name, land in zip(["w_up", "w_o", "w_attn_out", "w_conv_out"], got):
            adam(name, land, l)
        d_win = _wgrad(dq, sv["h"], NP, 0, d_win, f"wgrad_q{l}", a_is_transposed=True)
        d_win = _wgrad(dk, sv["h"], NP, NQ, d_win, f"wgrad_k{l}", a_is_transposed=True)
        d_win = _wgrad(dv, sv["h"], NP, NQ + NKV, d_win, f"wgrad_v{l}", a_is_transposed=True)
        side = _Exchange([blocks(d_win)]) if l == 0 else None
        (dcur, acc_in[l]), got_in = _inproj_bwd(dgt, dq, dk, dv, sv["x"], dx1, g_mix[l:l + 1], wint[l], l, tm_in, side=side)
        d_win_above = d_win

    dsinks = jnp.stack([dsink[l][:, :, :, 0].sum(axis=0).reshape(16) for l in range(L)])
    small = _pack_small(
        jnp.concatenate([acc_in[l][0:1] for l in range(L)]),
        jnp.stack([acc_mix[l][0:2].reshape(2 * D) for l in range(L)]),
        dsinks,
        jnp.concatenate([acc_mix[l][2:3] for l in range(L)]),
        jnp.concatenate([acc_mlp[l][0:1] for l in range(L)]),
        acc_loss[0],
        jnp.concatenate([acc_mix[l][3:6] for l in range(L)]),
        acc_loss[1:2])
    adam("w_in", got_in[0], 0)
    (small_land,) = _remote_only(_Exchange([], [small]), "exchange_small")

    row = lambda t: t.reshape(1, D)
    small_out, loss = _adam_small(small_land, me.reshape(1).astype(jnp.int32), {
        "g_mix": (g_mix, m_g_mix, v_g_mix), "b_gates": (b_gates, m_b_gates, v_b_gates), "sinks": (sinks, m_sinks, v_sinks),
        "conv_b": (conv_b, m_conv_b, v_conv_b), "g_mlp": (g_mlp, m_g_mlp, v_g_mlp),
        "g_final": (row(g_final), row(m_g_final), row(v_g_final)), "conv_w": (conv_w, m_conv_w, v_conv_w)})
    small_out["g_final"] = [t.reshape(D) for t in small_out["g_final"]]
    for name in ("w_in", "w_up"):
        big[name] = [tr(o) for o in big[name]]
    order = ["g_mix", "w_in", "b_gates", "sinks", "w_attn_out", "conv_w", "conv_b", "w_conv_out", "w_o", "g_mlp",
             "w_up", "w_down", "g_final"]
    out = [loss.reshape(()), dcur.reshape(nseq, S, D)]
    for kind in range(4):
        for name in order:
            out.append(big[name][kind] if name in big else small_out[name][kind])
    return tuple(out)
```

```python
import numpy as np
import jax
import jax.numpy as jnp
from jax import lax
from jax.experimental import pallas as pl
from jax.experimental.pallas import tpu as pltpu

D = 1024
NG = 5 * D
NQ = 1024
NKV = 256
NP = NQ + 2 * NKV + NG
F = 4096
HD = 64
GQ = 4
WIN = 128
L = 2
NDEV = 8
EPS = 1e-6
NEG = -1e30
SMALL_ROWS = 24
LR, B1, B2, AEPS, WD, STEP = 0.001, 0.9, 0.999, 1e-08, 0.01, 10

BF = jnp.bfloat16
F32 = jnp.float32
MESH = pl.DeviceIdType.MESH
VMEM_LIMIT = 60 * 1024 * 1024
ANY = pl.BlockSpec(memory_space=pl.ANY)

NN = ((1,), (0,))
NT = ((1,), (1,))
TN = ((0,), (0,))


def _dot(a, b, dims):
    return lax.dot_general(a, b, (dims, ((), ())), preferred_element_type=F32)


def _resident(shape, imap):
    return pl.BlockSpec(shape, imap, pipeline_mode=pl.Buffered(1))


def _position():
    return lax.axis_index("x"), lax.axis_index("y"), lax.axis_index("c")


class _Gather:
    def __init__(self, shards):
        n = len(shards)
        self.inputs = list(shards)
        self.out_shape = [jax.ShapeDtypeStruct((s.shape[0], NDEV) + s.shape[1:], s.dtype) for s in shards]
        self.scratch = [pltpu.SemaphoreType.DMA((n, 7)), pltpu.SemaphoreType.DMA((n, 7)), pltpu.SemaphoreType.DMA((n,))]

    def _plan(self, src, dst, sems):
        send_sems, recv_sems, local_sems = sems
        n = len(src)
        x, y, c = _position()
        me, sibling = (x, y, c), (x, y, 1 - c)
        chips = [(1 - x, y), (x, 1 - y), (1 - x, 1 - y)]

        def rows(a, p):
            return dst[a].at[:, 4 * p[0] + 2 * p[1] + p[2]]

        def copy(a, k, block, to, from_src=False):
            return pltpu.make_async_remote_copy(
                src_ref=src[a] if from_src else rows(a, block), dst_ref=rows(a, block),
                send_sem=send_sems.at[a, k], recv_sem=recv_sems.at[a, k], device_id=to, device_id_type=MESH)

        mine = [pltpu.make_async_copy(src[a], rows(a, me), local_sems.at[a]) for a in range(n)]
        first = []
        for a in range(n):
            first.append(copy(a, 0, me, sibling, True))
            first += [copy(a, 1 + t, me, (*chip, c), True) for t, chip in enumerate(chips)]
        return n, c, me, sibling, chips, copy, mine, first

    def start(self, src, dst, sems):
        *_, mine, first = self._plan(src, dst, sems)
        for cp in mine + first:
            cp.start()

    def relay(self, src, dst, sems):
        n, c, me, sibling, chips, copy, _, _ = self._plan(src, dst, sems)
        for t, chip in enumerate(chips):
            for a in range(n):
                copy(a, 1 + t, (*chip, c), me).wait_recv()
                copy(a, 4 + t, (*chip, c), sibling).start()

    def finish(self, src, dst, sems):
        n, c, me, sibling, chips, copy, mine, first = self._plan(src, dst, sems)
        for a in range(n):
            copy(a, 0, sibling, me).wait_recv()
            for t, chip in enumerate(chips):
                copy(a, 4 + t, (*chip, 1 - c), me).wait_recv()
        for cp in first + [copy(a, 4 + t, (*chip, c), sibling) for t, chip in enumerate(chips) for a in range(n)]:
            cp.wait_send()
        for cp in mine:
            cp.wait()


class _Exchange:
    def __init__(self, grads, everyone=()):
        self.inputs = list(grads) + list(everyone)
        self.n_blocked = len(grads)
        n = len(self.inputs)
        self.out_shape = [jax.ShapeDtypeStruct(g.shape, g.dtype) for g in grads]
        self.out_shape += [jax.ShapeDtypeStruct((NDEV,) + e.shape, e.dtype) for e in everyone]
        self.scratch = [pltpu.SemaphoreType.DMA((n, 7)), pltpu.SemaphoreType.DMA((n, 7)), pltpu.SemaphoreType.DMA((n,))]

    def _plan(self, src, land, sems):
        send_sems, recv_sems, local_sems = sems
        x, y, c = _position()
        me = 4 * x + 2 * y + c

        def parts(peer_idx):
            return [(s.at[peer_idx] if a < self.n_blocked else s, land[a].at[me]) for a, s in enumerate(src)]

        local = [pltpu.make_async_copy(s, d, local_sems.at[a]) for a, (s, d) in enumerate(parts(me))]
        sent = []
        for rel in range(1, NDEV):
            px = 1 - x if rel & 4 else x
            py = 1 - y if rel & 2 else y
            pc = 1 - c if rel & 1 else c
            for a, (s, d) in enumerate(parts(4 * px + 2 * py + pc)):
                sent.append(pltpu.make_async_remote_copy(
                    src_ref=s, dst_ref=d, send_sem=send_sems.at[a, rel - 1], recv_sem=recv_sems.at[a, rel - 1],
                    device_id=(px, py, pc), device_id_type=MESH))
        return local, sent

    def start(self, src, land, sems):
        local, sent = self._plan(src, land, sems)
        for cp in local + sent:
            cp.start()

    def relay(self, src, land, sems):
        pass

    def finish(self, src, land, sems):
        local, sent = self._plan(src, land, sems)
        for cp in sent:
            cp.wait_recv()
        for cp in sent:
            cp.wait_send()
        for cp in local:
            cp.wait()


def _call(body, *, name, grid, in_specs, out_specs, out_shape, args, scratch_shapes=(), aliases=None, side=None,
          relay_at=1.0):
    sem = ("arbitrary",) * len(grid)
    if side is None:
        outs = pl.pallas_call(
            body, name=name, grid=grid, in_specs=in_specs, out_specs=out_specs, out_shape=out_shape,
            scratch_shapes=list(scratch_shapes), input_output_aliases=aliases or {},
            compiler_params=pltpu.CompilerParams(dimension_semantics=sem, vmem_limit_bytes=VMEM_LIMIT))(*args)
        return outs, []
    ni, no, ns = len(in_specs), len(out_specs), len(scratch_shapes)
    si, so = len(side.inputs), len(side.out_shape)

    def hosted(*refs):
        ins, refs = refs[:ni], refs[ni:]
        sins, refs = refs[:si], refs[si:]
        outs, refs = refs[:no], refs[no:]
        souts, refs = refs[:so], refs[so:]
        scr, sscr = refs[:ns], refs[ns:]
        step = pl.program_id(0)
        for d in range(1, len(grid)):
            step = step * grid[d] + pl.program_id(d)
        last = int(np.prod(grid)) - 1

        @pl.when(step == 0)
        def _():
            side.start(sins, souts, sscr)

        body(*ins, *outs, *scr)

        @pl.when(step == min(int(relay_at * last), last))
        def _():
            side.relay(sins, souts, sscr)

        @pl.when(step == last)
        def _():
            side.finish(sins, souts, sscr)

    outs = pl.pallas_call(
        hosted, name=name, grid=grid, in_specs=list(in_specs) + [ANY] * si, out_specs=list(out_specs) + [ANY] * so,
        out_shape=list(out_shape) + side.out_shape, scratch_shapes=list(scratch_shapes) + side.scratch,
        input_output_aliases=aliases or {},
        compiler_params=pltpu.CompilerParams(dimension_semantics=sem, vmem_limit_bytes=VMEM_LIMIT, has_side_effects=True),
    )(*args, *side.inputs)
    return outs[:no], outs[no:]


def _remote_only(side, name):
    n = len(side.inputs)

    def body(*refs):
        src, dst, sems = refs[:n], refs[n:n + len(side.out_shape)], refs[n + len(side.out_shape):]
        side.start(src, dst, sems)
        side.relay(src, dst, sems)
        side.finish(src, dst, sems)

    return pl.pallas_call(
        body, name=name, in_specs=[ANY] * n, out_specs=[ANY] * len(side.out_shape), out_shape=side.out_shape,
        scratch_shapes=side.scratch, compiler_params=pltpu.CompilerParams(has_side_effects=True))(*side.inputs)


def _rms(x, g):
    r = lax.rsqrt(jnp.mean(x * x, axis=-1, keepdims=True) + EPS)
    return x * r * g


def _rms_bwd(dy, x, g):
    r = lax.rsqrt(jnp.mean(x * x, axis=-1, keepdims=True) + EPS)
    xh = x * r
    dxh = dy * g
    dx = r * (dxh - xh * jnp.mean(dxh * xh, axis=-1, keepdims=True))
    return dx, jnp.sum(dy * xh, axis=0, keepdims=True)


def _zero_at_first_step(acc_ref):
    first = pl.program_id(0) == 0

    @pl.when(first)
    def _():
        acc_ref[...] = jnp.zeros_like(acc_ref)


ROW1 = lambda: _resident((1, D), lambda i: (0, 0))
ACC = lambda: pl.BlockSpec((8, D), lambda i: (0, 0))


def _inproj_fwd(x, g, wint, l, tm, side=None, relay_at=1.0):
    T = x.shape[0]

    def body(x_ref, g_ref, w_ref, gt_ref, q_ref, k_ref, v_ref, h_ref):
        h = _rms(x_ref[...], g_ref[...]).astype(BF)
        h_ref[...] = h
        q_ref[...] = _dot(w_ref[0:NQ, :], h, NT).astype(BF)
        k_ref[...] = _dot(w_ref[NQ:NQ + NKV, :], h, NT).astype(BF)
        v_ref[...] = _dot(w_ref[NQ + NKV:NQ + 2 * NKV, :], h, NT).astype(BF)
        for s in range(5):
            lo = NQ + 2 * NKV + s * D
            gt_ref[:, s * D:(s + 1) * D] = _dot(h, w_ref[lo:lo + D, :], NT).astype(BF)

    tok = lambda w: pl.BlockSpec((tm, w), lambda i: (i, 0))
    feat = lambda w: pl.BlockSpec((w, tm), lambda i: (0, i))
    return _call(
        body, name=f"inproj_fwd{l}", grid=(T // tm,),
        in_specs=[tok(D), ROW1(), _resident((NP, D), lambda i: (0, 0))],
        out_specs=[tok(NG), feat(NQ), feat(NKV), feat(NKV), tok(D)],
        out_shape=[jax.ShapeDtypeStruct((T, NG), BF)] + [jax.ShapeDtypeStruct((w, T), BF) for w in (NQ, NKV, NKV)]
        + [jax.ShapeDtypeStruct((T, D), BF)],
        args=(x, g, wint), side=side, relay_at=relay_at)


def _band_geometry():
    j = lax.broadcasted_iota(jnp.int32, (2 * WIN, WIN), 0)
    r = lax.broadcasted_iota(jnp.int32, (2 * WIN, WIN), 1)
    dist = WIN + r - j
    dist0 = r - j
    return (dist.astype(F32), (dist >= 0) & (dist < WIN)), (dist0.astype(F32), dist0 >= 0)


def _pair_biases(sm_ref, pj):
    return [[jnp.where(ok, -sm_ref[1, pj * 2 * GQ + h] * dist, NEG) for h in range(2 * GQ)]
            for dist, ok in _band_geometry()]


def _reduce_rows(x, pair, whole):
    while x.shape[0] > 8:
        half = x.shape[0] // 2
        x = pair(x[:half], x[half:])
    return whole(x, axis=0, keepdims=True)


def _heads_on_lanes(ref, kvh, r0):
    return jnp.concatenate([ref[(kvh * GQ + g) * HD:(kvh * GQ + g + 1) * HD, pl.ds(r0, WIN)] for g in range(GQ)], axis=1)


def _band_probs(sm_ref, head0, q_ref, k_ref, r0, p0, kvh, biases):
    qt = _heads_on_lanes(q_ref, kvh, r0) * jnp.asarray(HD ** -0.5, BF)
    kt = k_ref[kvh * HD:(kvh + 1) * HD, pl.ds(p0, 2 * WIN)]
    st = _dot(kt, qt, TN)
    heads = []
    for g in range(GQ):
        sink = sm_ref[0, head0 + g]
        s = st[:, g * WIN:(g + 1) * WIN] + biases[kvh * GQ + g]
        m = jnp.maximum(_reduce_rows(s, jnp.maximum, jnp.max), sink)
        p = jnp.exp(s - m)
        ps = jnp.exp(sink - m)
        heads.append((p, ps, 1.0 / (_reduce_rows(p, jnp.add, jnp.sum) + ps)))
    return qt, kt, heads


def _attn_fwd(q, k, v, sm, l, S, side=None, relay_at=1.0):
    T = q.shape[1]
    nblk = S // WIN
    unroll = next(u for u in (5, 3, 1) if (nblk - 1) % u == 0)

    def body(sm_ref, q_ref, k_ref, v_ref, o_ref):
        pj = pl.program_id(1)
        biases = _pair_biases(sm_ref, pj)

        def block(i, first):
            r0 = 0 if first else pl.multiple_of(i * WIN, WIN)
            p0 = 0 if first else pl.multiple_of(i * WIN - WIN, WIN)
            for kvh in range(2):
                head0 = (2 * pj + kvh) * GQ
                _, _, heads = _band_probs(sm_ref, head0, q_ref, k_ref, r0, p0, kvh, biases[first])
                pt = jnp.concatenate([p.astype(BF) for p, _, _ in heads], axis=1)
                ot = _dot(v_ref[kvh * HD:(kvh + 1) * HD, pl.ds(p0, 2 * WIN)], pt, NN)
                for g in range(GQ):
                    rows = slice((kvh * GQ + g) * HD, (kvh * GQ + g + 1) * HD)
                    o_ref[rows, pl.ds(r0, WIN)] = (ot[:, g * WIN:(g + 1) * WIN] * heads[g][2]).astype(BF)

        block(0, True)

        def rest(t, c):
            for u in range(unroll):
                block(1 + unroll * t + u, False)
            return c

        lax.fori_loop(0, (nblk - 1) // unroll, rest, 0)

    wide = lambda: pl.BlockSpec((2 * GQ * HD, S), lambda s, p: (p, s))
    narrow = lambda: pl.BlockSpec((2 * HD, S), lambda s, p: (p, s))
    return _call(
        body, name=f"attn_fwd{l}", grid=(T // S, 2),
        in_specs=[pl.BlockSpec(memory_space=pltpu.SMEM), wide(), narrow(), narrow()],
        out_specs=[wide()], out_shape=[jax.ShapeDtypeStruct((D, T), BF)],
        args=(sm, q, k, v), side=side, relay_at=relay_at)


def _shift_rows(y, k, edge_rows, down):
    n = y.shape[0]
    rid = lax.broadcasted_iota(jnp.int32, y.shape, 0)
    out = pltpu.roll(y, k if down else n - k, 0)
    for t, row in enumerate(edge_rows):
        out = jnp.where(rid == (t if down else n - k + t), row, out)
    return out


def _mixer_fwd(x, gt, att, w3, bg, cw, cbias, l, S, tm, side=None, relay_at=1.0):
    T = x.shape[0]

    def body(x_ref, cb_ref, cc_ref, cu_ref, ga_ref, gc_ref, cch_ref, cuh_ref, att_ref, wao_ref, wco_ref, wo_ref,
             bg_ref, cw_ref, cbias_ref, x1_ref, mg_ref, co_ref, ya_ref, yc_ref, zb_ref):
        first = (pl.program_id(0) * tm) % S == 0
        y = cc_ref[...].astype(F32) * cu_ref[...].astype(F32)
        hy1 = cch_ref[15:16, :].astype(F32) * cuh_ref[15:16, :].astype(F32)
        hy2 = cch_ref[14:15, :].astype(F32) * cuh_ref[14:15, :].astype(F32)
        hy1, hy2 = jnp.where(first, 0.0, hy1), jnp.where(first, 0.0, hy2)
        z = (cw_ref[0:1, :] * _shift_rows(y, 2, [hy2, hy1], True) + cw_ref[1:2, :] * _shift_rows(y, 1, [hy1], True)
             + cw_ref[2:3, :] * y)
        zb = z + cbias_ref[...]
        zb_ref[...] = zb.astype(BF)
        co = (cb_ref[...].astype(F32) * zb).astype(BF)
        co_ref[...] = co
        yc = _dot(co, wco_ref[...], NN)
        ya = _dot(att_ref[...], wao_ref[...], TN)
        ya_ref[...] = ya.astype(BF)
        yc_ref[...] = yc.astype(BF)
        sa = jax.nn.sigmoid(ga_ref[...].astype(F32) + bg_ref[0:1, :])
        sc = jax.nn.sigmoid(gc_ref[...].astype(F32) + bg_ref[1:2, :])
        mg = (sa * ya + sc * yc).astype(BF)
        mg_ref[...] = mg
        x1_ref[...] = x_ref[...] + _dot(mg, wo_ref[...], NN)

    tok = lambda: pl.BlockSpec((tm, D), lambda i: (i, 0))
    seg = lambda s: pl.BlockSpec((tm, D), lambda i: (i, s))
    halo = lambda s: pl.BlockSpec((16, D), lambda i: (jnp.maximum(i * (tm // 16) - 1, 0), s))
    wsp = lambda k: _resident((None, D, D), lambda i: (k, 0, 0))
    row = lambda n: _resident((n, D), lambda i: (0, 0))
    return _call(
        body, name=f"mixer_fwd{l}", grid=(T // tm,),
        in_specs=[tok(), seg(0), seg(1), seg(2), seg(3), seg(4), halo(1), halo(2),
                  pl.BlockSpec((D, tm), lambda i: (0, i)), wsp(0), wsp(1), wsp(2), row(2), row(8), row(1)],
        out_specs=[tok()] * 6,
        out_shape=[jax.ShapeDtypeStruct((T, D), dt) for dt in (F32, BF, BF, BF, BF, BF)],
        args=(x, gt, gt, gt, gt, gt, gt, gt, att, w3, w3, w3, bg, cw, cbias), side=side, relay_at=relay_at)


def _mlp_fwd(x1, g, w2, l, tm, head=None):
    T = x1.shape[0]
    nt = T // tm
    FC = 1024

    def body(x_ref, g_ref, wup_ref, wdn_ref, *rest):
        out_ref, h_ref, a_ref = rest[-4:-1] if head else rest
        x = x_ref[...]
        h = _rms(x, g_ref[...]).astype(BF)
        h_ref[...] = h
        acc = x
        for c in range(F // FC):
            a = _dot(h, wup_ref[c * FC:(c + 1) * FC, :], NT)
            a_ref[:, c * FC:(c + 1) * FC] = a.astype(BF)
            u = jnp.maximum(a, 0.0)
            acc = acc + _dot((u * u).astype(BF), wdn_ref[c * FC:(c + 1) * FC, :], NN)
        if not head:
            out_ref[...] = acc
            return
        t_ref, gf_ref, acc_ref = rest[0], rest[1], rest[-1]
        _zero_at_first_step(acc_ref)
        gf = gf_ref[...]
        err = _rms(acc, gf) - t_ref[...]
        out_ref[...], dg = _rms_bwd(err * (1.0 / D), acc, gf)
        acc_ref[0:1, :] += dg
        acc_ref[1:2, :] += jnp.sum(err * err, axis=0, keepdims=True)

        @pl.when(pl.program_id(0) == nt - 1)
        def _():
            acc_ref[1:2, :] = jnp.zeros((1, D), F32) + (0.5 / D) * jnp.sum(acc_ref[1:2, :])

    tok = lambda w: pl.BlockSpec((tm, w), lambda i: (i, 0))
    wsp = lambda k: _resident((None, F, D), lambda i: (k, 0, 0))
    outs, _ = _call(
        body, name=f"mlp_fwd{l}", grid=(nt,),
        in_specs=[tok(D), ROW1(), wsp(0), wsp(1)] + ([tok(D), ROW1()] if head else []),
        out_specs=[tok(D), tok(D), tok(F)] + ([ACC()] if head else []),
        out_shape=[jax.ShapeDtypeStruct((T, D), F32), jax.ShapeDtypeStruct((T, D), BF), jax.ShapeDtypeStruct((T, F), BF)]
        + ([jax.ShapeDtypeStruct((8, D), F32)] if head else []),
        args=(x1, g, w2, w2) + (tuple(head) if head else ()))
    return outs


def _mlp_bwd(dx2, x1, a, g, w2, l, tm, side=None):
    T = dx2.shape[0]
    FC = 1024

    def body(d_ref, x_ref, a_ref, g_ref, wup_ref, wdn_ref, da_ref, dx1_ref, db_ref, acc_ref):
        _zero_at_first_step(acc_ref)
        d = d_ref[...]
        db = d.astype(BF)
        db_ref[...] = db
        dh = jnp.zeros((tm, D), F32)
        for c in range(F // FC):
            du = _dot(db, wdn_ref[c * FC:(c + 1) * FC, :], NT)
            da = (du * (2.0 * jnp.maximum(a_ref[:, c * FC:(c + 1) * FC].astype(F32), 0.0))).astype(BF)
            da_ref[:, c * FC:(c + 1) * FC] = da
            dh = dh + _dot(da, wup_ref[c * FC:(c + 1) * FC, :], NN)
        dx, dg = _rms_bwd(dh, x_ref[...], g_ref[...])
        dx1_ref[...] = d + dx
        acc_ref[0:1, :] += dg

    tok = lambda w: pl.BlockSpec((tm, w), lambda i: (i, 0))
    wsp = lambda k: _resident((None, F, D), lambda i: (k, 0, 0))
    return _call(
        body, name=f"mlp_bwd{l}", grid=(T // tm,),
        in_specs=[tok(D), tok(D), tok(F), ROW1(), wsp(0), wsp(1)],
        out_specs=[tok(F), tok(D), tok(D), ACC()],
        out_shape=[jax.ShapeDtypeStruct((T, F), BF), jax.ShapeDtypeStruct((T, D), F32),
                   jax.ShapeDtypeStruct((T, D), BF), jax.ShapeDtypeStruct((8, D), F32)],
        args=(dx2, x1, a, g, w2, w2), side=side)


def _mixer_bwd(dx1, gt, ya, yc, zb, w3, bg, cw, l, S, tm, side=None):
    T = dx1.shape[0]
    nt = T // tm

    def body(d_ref, cb_ref, cc_ref, cu_ref, ga_ref, gc_ref, ya_ref, yc_ref, zb_ref, wao_ref, wco_ref, wo_ref,
             bg_ref, cw_ref, dg_ref, datt_ref, dya_ref, dyc_ref, db_ref, acc_ref, carry_ref):
        ti = nt - 1 - pl.program_id(0)
        _zero_at_first_step(acc_ref)

        @pl.when(((ti + 1) * tm) % S == 0)
        def _():
            carry_ref[...] = jnp.zeros_like(carry_ref)

        db = d_ref[...].astype(BF)
        db_ref[...] = db
        dm = _dot(db, wo_ref[...], NT)
        sa = jax.nn.sigmoid(ga_ref[...].astype(F32) + bg_ref[0:1, :])
        sc = jax.nn.sigmoid(gc_ref[...].astype(F32) + bg_ref[1:2, :])
        dya32 = dm * sa
        dyc32 = dm * sc
        dya = dya32.astype(BF)
        dyc = dyc32.astype(BF)
        dya_ref[...] = dya
        dyc_ref[...] = dyc
        dga = dya32 * ya_ref[...].astype(F32) * (1.0 - sa)
        dgc = dyc32 * yc_ref[...].astype(F32) * (1.0 - sc)
        dg_ref[:, 3 * D:4 * D] = dga.astype(BF)
        dg_ref[:, 4 * D:5 * D] = dgc.astype(BF)
        acc_ref[0:1, :] += jnp.sum(dga, axis=0, keepdims=True)
        acc_ref[1:2, :] += jnp.sum(dgc, axis=0, keepdims=True)
        datt_ref[...] = _dot(wao_ref[...], dya, NT).astype(BF)
        dco = _dot(dyc, wco_ref[...], NT)

        cc = cc_ref[...].astype(F32)
        cu = cu_ref[...].astype(F32)
        y = cc * cu
        dg_ref[:, 0:D] = (dco * zb_ref[...].astype(F32)).astype(BF)
        dz = dco * cb_ref[...].astype(F32)
        u1 = _shift_rows(dz, 1, [carry_ref[0:1, :]], False)
        u2 = _shift_rows(dz, 2, [carry_ref[0:1, :], carry_ref[1:2, :]], False)
        acc_ref[2:3, :] += jnp.sum(dz, axis=0, keepdims=True)
        acc_ref[3:4, :] += jnp.sum(u2 * y, axis=0, keepdims=True)
        acc_ref[4:5, :] += jnp.sum(u1 * y, axis=0, keepdims=True)
        acc_ref[5:6, :] += jnp.sum(dz * y, axis=0, keepdims=True)
        dy = cw_ref[2:3, :] * dz + cw_ref[1:2, :] * u1 + cw_ref[0:1, :] * u2
        dg_ref[:, D:2 * D] = (dy * cu).astype(BF)
        dg_ref[:, 2 * D:3 * D] = (dy * cc).astype(BF)
        carry_ref[...] = dz[0:8, :]

    tok = lambda w=D: pl.BlockSpec((tm, w), lambda i: (nt - 1 - i, 0))
    seg = lambda s: pl.BlockSpec((tm, D), lambda i: (nt - 1 - i, s))
    wsp = lambda k: _resident((None, D, D), lambda i: (k, 0, 0))
    row = lambda n: _resident((n, D), lambda i: (0, 0))
    return _call(
        body, name=f"mixer_bwd{l}", grid=(nt,),
        in_specs=[tok(), seg(0), seg(1), seg(2), seg(3), seg(4), tok(), tok(), tok(), wsp(0), wsp(1), wsp(2),
                  row(2), row(8)],
        out_specs=[tok(NG), pl.BlockSpec((D, tm), lambda i: (0, nt - 1 - i)), tok(), tok(), tok(), ACC()],
        out_shape=[jax.ShapeDtypeStruct((T, NG), BF), jax.ShapeDtypeStruct((D, T), BF)]
        + [jax.ShapeDtypeStruct((T, D), BF)] * 3 + [jax.ShapeDtypeStruct((8, D), F32)],
        scratch_shapes=[pltpu.VMEM((8, D), F32)],
        args=(dx1, gt, gt, gt, gt, gt, ya, yc, zb, w3, w3, w3, bg, cw), side=side)


def _attn_bwd(q, k, v, att, datt, sm, l, S, side=None):
    T = q.shape[1]
    nblk = S // WIN
    unroll = next(u for u in (5, 3, 1) if (nblk - 1) % u == 0)
    scale = HD ** -0.5

    def body(sm_ref, q_ref, k_ref, v_ref, o_ref, do_ref, dq_ref, dk_ref, dv_ref, ds_ref, dka_ref, dva_ref):
        pj = pl.program_id(1)
        biases = _pair_biases(sm_ref, pj)
        dka_ref[...] = jnp.zeros_like(dka_ref)
        dva_ref[...] = jnp.zeros_like(dva_ref)

        def block(i, first, dsinks):
            r0 = 0 if first else pl.multiple_of(i * WIN, WIN)
            p0 = 0 if first else pl.multiple_of(i * WIN - WIN, WIN)
            out = []
            for kvh in range(2):
                head0 = (2 * pj + kvh) * GQ
                rows = slice(kvh * HD, (kvh + 1) * HD)
                dob = _heads_on_lanes(do_ref, kvh, r0)
                dpt = _dot(v_ref[rows, pl.ds(p0, 2 * WIN)], dob, TN)
                qt, kt, heads = _band_probs(sm_ref, head0, q_ref, k_ref, r0, p0, kvh, biases[first])
                inv = jnp.concatenate([h[2] for h in heads], axis=1)
                do32 = dob.astype(F32)
                delta = jnp.sum(do32 * _heads_on_lanes(o_ref, kvh, r0).astype(F32), axis=0, keepdims=True) * inv
                dosb = (do32 * inv).astype(BF)
                dst = jnp.concatenate(
                    [(p * (dpt[:, g * WIN:(g + 1) * WIN] * h_inv - delta[:, g * WIN:(g + 1) * WIN])).astype(BF)
                     for g, (p, _, h_inv) in enumerate(heads)], axis=1)
                pt = jnp.concatenate([p.astype(BF) for p, _, _ in heads], axis=1)
                dqt = _dot(kt, dst, NN) * scale
                for g in range(GQ):
                    hr = slice((kvh * GQ + g) * HD, (kvh * GQ + g + 1) * HD)
                    dq_ref[hr, pl.ds(r0, WIN)] = dqt[:, g * WIN:(g + 1) * WIN].astype(BF)
                dka_ref[rows, pl.ds(p0, 2 * WIN)] += _dot(qt, dst, NT)
                dva_ref[rows, pl.ds(p0, 2 * WIN)] += _dot(dosb, pt, NT)
                ps = jnp.concatenate([h[1] for h in heads], axis=1)
                out.append(dsinks[kvh] - ps * delta)
            return tuple(out)

        zero = jnp.zeros((1, GQ * WIN), F32)
        def rest(t, c):
            for u in range(unroll):
                c = block(1 + unroll * t + u, False, c)
            return c

        dsinks = lax.fori_loop(0, (nblk - 1) // unroll, rest, block(0, True, (zero, zero)))
        for kvh in range(2):
            for g in range(GQ):
                tot = jnp.sum(dsinks[kvh][:, g * WIN:(g + 1) * WIN])
                ds_ref[kvh * GQ + g:kvh * GQ + g + 1, :] = jnp.zeros((1, 128), F32) + tot
        dk_ref[...] = dka_ref[...].astype(BF)
        dv_ref[...] = dva_ref[...].astype(BF)

    wide = lambda: pl.BlockSpec((2 * GQ * HD, S), lambda s, p: (p, s))
    narrow = lambda: pl.BlockSpec((2 * HD, S), lambda s, p: (p, s))
    return _call(
        body, name=f"attn_bwd{l}", grid=(T // S, 2),
        in_specs=[pl.BlockSpec(memory_space=pltpu.SMEM), wide(), narrow(), narrow(), wide(), wide()],
        out_specs=[wide(), narrow(), narrow(), pl.BlockSpec((None, None, 8, 128), lambda s, p: (s, p, 0, 0))],
        out_shape=[jax.ShapeDtypeStruct((NQ, T), BF), jax.ShapeDtypeStruct((NKV, T), BF),
                   jax.ShapeDtypeStruct((NKV, T), BF), jax.ShapeDtypeStruct((T // S, 2, 8, 128), F32)],
        scratch_shapes=[pltpu.VMEM((2 * HD, S), F32), pltpu.VMEM((2 * HD, S), F32)],
        args=(sm, q, k, v, att, datt), side=side)


def _inproj_bwd(dgt, dq, dk, dv, x, dres, g, wint, l, tm, side=None):
    T = x.shape[0]

    def body(dg_ref, dq_ref, dk_ref, dv_ref, x_ref, dr_ref, g_ref, w_ref, dx_ref, acc_ref):
        _zero_at_first_step(acc_ref)
        dh = _dot(dq_ref[...], w_ref[0:NQ, :], TN)
        dh = dh + _dot(dk_ref[...], w_ref[NQ:NQ + NKV, :], TN)
        dh = dh + _dot(dv_ref[...], w_ref[NQ + NKV:NQ + 2 * NKV, :], TN)
        dh = dh + _dot(dg_ref[...], w_ref[NQ + 2 * NKV:NP, :], NN)
        dx, dg = _rms_bwd(dh, x_ref[...], g_ref[...])
        dx_ref[...] = dr_ref[...] + dx
        acc_ref[0:1, :] += dg

    tok = lambda w: pl.BlockSpec((tm, w), lambda i: (i, 0))
    feat = lambda w: pl.BlockSpec((w, tm), lambda i: (0, i))
    return _call(
        body, name=f"inproj_bwd{l}", grid=(T // tm,),
        in_specs=[tok(NG), feat(NQ), feat(NKV), feat(NKV), tok(D), tok(D), ROW1(), _resident((NP, D), lambda i: (0, 0))],
        out_specs=[tok(D), ACC()],
        out_shape=[jax.ShapeDtypeStruct((T, D), F32), jax.ShapeDtypeStruct((8, D), F32)],
        args=(dgt, dq, dk, dv, x, dres, g, wint), side=side)


def _wgrad(a, b, rows, row0, into, name, relu2=False, a_is_transposed=False):
    M, T = a.shape if a_is_transposed else a.shape[::-1]
    tmm = next(t for t in (1024, 512, 256) if M % t == 0 and row0 % t == 0)
    tk = min(4096 if M > 4096 else 2048, T)
    nk = T // tk
    blk0 = row0 // tmm

    def body(*refs):
        a_ref, b_ref = refs[0], refs[1]
        o_ref, acc_ref = refs[-2], refs[-1]
        kk = pl.program_id(1)

        @pl.when(kk == 0)
        def _():
            acc_ref[...] = jnp.zeros_like(acc_ref)

        av = a_ref[...]
        if relu2:
            t = jnp.maximum(av.astype(F32), 0.0)
            av = (t * t).astype(BF)
        acc_ref[...] += _dot(av, b_ref[...], NN if a_is_transposed else TN)

        @pl.when(kk == nk - 1)
        def _():
            o_ref[...] = acc_ref[...].astype(BF)

    a_spec = pl.BlockSpec((tmm, tk), lambda j, kk: (j, kk)) if a_is_transposed else pl.BlockSpec((tk, tmm), lambda j, kk: (kk, j))
    in_specs = [a_spec, pl.BlockSpec((tk, D), lambda j, kk: (kk, 0))]
    args = [a, b]
    if into is not None:
        in_specs.append(ANY)
        args.append(into)
    (out,), _ = _call(
        body, name=name, grid=(M // tmm, nk), in_specs=in_specs,
        out_specs=[pl.BlockSpec((tmm, D), lambda j, kk: (blk0 + j, 0))],
        out_shape=[jax.ShapeDtypeStruct((rows, D), BF)], scratch_shapes=[pltpu.VMEM((tmm, D), F32)],
        aliases={2: 0} if into is not None else None, args=args)
    return out


def _wgrad_square(jobs, name):
    n = len(jobs)
    T = jobs[0][1].shape[0]
    tk = min(1024, T)
    nk = T // tk

    def body(*refs):
        ins, outs, acc_ref = refs[:2 * n], refs[2 * n:3 * n], refs[3 * n]
        j, kk = pl.program_id(0), pl.program_id(1)

        @pl.when(kk == 0)
        def _():
            acc_ref[...] = jnp.zeros_like(acc_ref)

        for p, (_, _, a_t) in enumerate(jobs):
            @pl.when(j == p)
            def _(p=p, a_t=a_t):
                acc_ref[...] += _dot(ins[2 * p][...], ins[2 * p + 1][...], NN if a_t else TN)

                @pl.when(kk == nk - 1)
                def _():
                    outs[p][...] = acc_ref[...].astype(BF)

    def step(p):
        return lambda j, kk: jnp.where(j == p, kk, jnp.where(j > p, nk - 1, 0))

    in_specs, args = [], []
    for p, (a, b, a_t) in enumerate(jobs):
        at = step(p)
        in_specs.append(pl.BlockSpec((D, tk), lambda j, kk, at=at: (0, at(j, kk))) if a_t
                        else pl.BlockSpec((tk, D), lambda j, kk, at=at: (at(j, kk), 0)))
        in_specs.append(pl.BlockSpec((tk, D), lambda j, kk, at=at: (at(j, kk), 0)))
        args += [a, b]
    outs, _ = _call(
        body, name=name, grid=(n, nk), in_specs=in_specs,
        out_specs=[pl.BlockSpec((D, D), lambda j, kk: (0, 0))] * n,
        out_shape=[jax.ShapeDtypeStruct((D, D), BF)] * n, scratch_shapes=[pltpu.VMEM((D, D), F32)], args=args)
    return outs


def _adamw(w, g, m, v):
    m = B1 * m + (1.0 - B1) * g
    v = B2 * v + (1.0 - B2) * (g * g)
    m_hat = m / (1.0 - B1 ** STEP)
    v_hat = v / (1.0 - B2 ** STEP)
    return -LR * (m_hat / (jnp.sqrt(v_hat) + AEPS) + WD * w), m, v


def _adam_sum(land, w, m, v, l, into, name):
    _, r, _ = land.shape
    tr = 208 if r % 208 == 0 else (256 if r % 256 == 0 else r)

    def body(land_ref, w_ref, m_ref, v_ref, *rest):
        g_ref, d_ref, nm_ref, nv_ref = rest[-4:]
        g = land_ref[0].astype(F32)
        for s in range(1, NDEV):
            g = g + land_ref[s].astype(F32)
        g_ref[...] = g
        d_ref[...], nm_ref[...], nv_ref[...] = _adamw(w_ref[...], g, m_ref[...], v_ref[...])

    blk = lambda: pl.BlockSpec((None, tr, D), lambda j: (l, j, 0))
    in_specs = [pl.BlockSpec((NDEV, tr, D), lambda j: (0, j, 0)), blk(), blk(), blk()]
    args = [land, w, m, v]
    aliases = None
    if into is not None:
        in_specs += [ANY] * 4
        args += list(into)
        aliases = {4 + t: t for t in range(4)}
    outs, _ = _call(body, name=name, grid=(r // tr,), in_specs=in_specs, out_specs=[blk()] * 4,
                    out_shape=[jax.ShapeDtypeStruct(w.shape, F32)] * 4, aliases=aliases, args=args)
    return outs


SMALL_NAMES = ("g_mix", "b_gates", "sinks", "conv_b", "g_mlp", "g_final", "conv_w")


def _adam_small(land, me, masters):
    n = len(SMALL_NAMES)
    lanes = D // NDEV

    def body(land_ref, me_ref, *refs):
        ins, outs, loss_ref, gs_ref = refs[:3 * n], refs[3 * n:7 * n], refs[7 * n], refs[7 * n + 1]
        g = land_ref[0]
        for s in range(1, NDEV):
            g = g + land_ref[s]
        gs_ref[...] = g

        def update(k, g_piece, idx):
            w_ref, m_ref, v_ref = ins[3 * k:3 * k + 3]
            outs[4 * k][idx] = g_piece
            outs[4 * k + 1][idx], outs[4 * k + 2][idx], outs[4 * k + 3][idx] = _adamw(w_ref[idx], g_piece, m_ref[idx], v_ref[idx])

        whole = (slice(None), slice(None))
        update(0, gs_ref[0:2, :], whole)
        for l in range(L):
            for h in range(2):
                update(1, gs_ref[2 + 2 * l + h:3 + 2 * l + h, :], (slice(l, l + 1), slice(h * D, (h + 1) * D)))
            update(2, gs_ref[6:7, 16 * l:16 * (l + 1)], (slice(l, l + 1), slice(None)))
        update(3, gs_ref[7:9, :], whole)
        update(4, gs_ref[9:11, :], whole)
        update(5, gs_ref[11:12, :], whole)
        mine = pl.ds(pl.multiple_of(me_ref[0] * lanes, lanes), lanes)
        for l in range(L):
            for k in range(3):
                update(6, gs_ref[12 + 3 * l + k:13 + 3 * l + k, mine], (l, slice(k, k + 1), slice(None)))
        loss_ref[...] = gs_ref[18:19, 0:1]

    flat = [t for name in SMALL_NAMES for t in masters[name]]
    vmem = pl.BlockSpec(memory_space=pltpu.VMEM)
    outs = pl.pallas_call(
        body, name="adam_small",
        in_specs=[vmem, pl.BlockSpec(memory_space=pltpu.SMEM)] + [vmem] * len(flat),
        out_shape=[jax.ShapeDtypeStruct(masters[name][0].shape, F32) for name in SMALL_NAMES for _ in range(4)]
        + [jax.ShapeDtypeStruct((1, 1), F32)],
        scratch_shapes=[pltpu.VMEM((SMALL_ROWS, D), F32)],
        compiler_params=pltpu.CompilerParams(vmem_limit_bytes=VMEM_LIMIT))(land, me, *flat)
    return {name: outs[4 * k:4 * k + 4] for k, name in enumerate(SMALL_NAMES)}, outs[-1]


def _pack_small(g_mix, b_gates, sinks, conv_b, g_mlp, g_final, conv_w_rows, extra):
    sink_row = jnp.zeros((1, D), F32).at[0, :2 * 16].set(sinks.reshape(-1))
    return jnp.concatenate([g_mix, b_gates.reshape(4, D), sink_row, conv_b, g_mlp, g_final.reshape(1, D),
                            conv_w_rows, extra, jnp.zeros((SMALL_ROWS - 19, D), F32)], axis=0)


def kernel(x, g_mix, w_in, b_gates, sinks, w_attn_out, conv_w, conv_b, w_conv_out, w_o, g_mlp, w_up, w_down, g_final, loss_target, m_g_mix, m_w_in, m_b_gates, m_sinks, m_w_attn_out, m_conv_w, m_conv_b, m_w_conv_out, m_w_o, m_g_mlp, m_w_up, m_w_down, m_g_final, v_g_mix, v_w_in, v_b_gates, v_sinks, v_w_attn_out, v_conv_w, v_conv_b, v_w_conv_out, v_w_o, v_g_mlp, v_w_up, v_w_down, v_g_final):
    nseq, S, _ = x.shape
    T = nseq * S
    tm_in = min(512, S)
    tm = min(512, S)
    xi, yi, ci = _position()
    me = 4 * xi + 2 * yi + ci
    tr = lambda t: jnp.swapaxes(t, 1, 2)
    blocks = lambda t: t.reshape(NDEV, t.shape[0] // NDEV, D)

    win_t, wup_t = tr(w_in), tr(w_up)
    sh_win = [win_t[l].astype(BF)[None] for l in range(L)]
    sh_w3 = [jnp.stack([w_attn_out[l], w_conv_out[l], w_o[l]]).astype(BF) for l in range(L)]
    sh_w2 = [jnp.stack([wup_t[l], w_down[l]]).astype(BF) for l in range(L)]
    wint, w3, w2 = [None] * L, [None] * L, [None] * L
    wint0_g, cw_g = _remote_only(_Gather([sh_win[0], jnp.pad(conv_w, ((0, 0), (0, 5), (0, 0)))]), "gather_first")
    wint[0] = wint0_g.reshape(NP, D)
    cw = jnp.swapaxes(cw_g, 1, 2).reshape(L, 8, D)
    slopes = np.power(np.float32(2.0), -8.0 * np.arange(1, 17, dtype=np.float32) / 16).astype(np.float32)

    xf = x.reshape(T, D)
    saved = []
    cur = xf
    for l in range(L):
        sm = jnp.stack([sinks[l], jnp.asarray(slopes)])
        bg = b_gates[l].reshape(2, D)
        (gt, q, k, v, h), got = _inproj_fwd(cur, g_mix[l:l + 1], wint[l], l, tm_in, side=_Gather([sh_w2[l]]),
                                            relay_at=0.85)
        w2[l] = got[0].reshape(2, F, D)
        (att,), got = _attn_fwd(q, k, v, sm, l, S, side=_Gather([sh_w3[l]]), relay_at=0.5)
        w3[l] = got[0].reshape(3, D, D)
        (x1, mg, co, ya, yc, zb), got = _mixer_fwd(cur, gt, att, w3[l], bg, cw[l], conv_b[l:l + 1], l, S, tm,
                                                   side=_Gather([sh_win[l + 1]]) if l + 1 < L else None, relay_at=0.85)
        if l + 1 < L:
            wint[l + 1] = got[0].reshape(NP, D)
        head = (loss_target.reshape(T, D), g_final.reshape(1, D)) if l == L - 1 else None
        nxt, h2, a, *acc_loss = _mlp_fwd(x1, g_mlp[l:l + 1], w2[l], l, tm_in, head=head)
        saved.append(dict(x=cur, gt=gt, q=q, k=k, v=v, h=h, att=att, x1=x1, mg=mg, co=co, ya=ya, yc=yc, zb=zb, h2=h2, a=a,
                          sm=sm, bg=bg))
        cur = nxt
    dcur, acc_loss = cur, acc_loss[0]

    masters = {"w_in": (win_t, tr(m_w_in), tr(v_w_in)), "w_attn_out": (w_attn_out, m_w_attn_out, v_w_attn_out),
               "w_conv_out": (w_conv_out, m_w_conv_out, v_w_conv_out), "w_o": (w_o, m_w_o, v_w_o),
               "w_up": (wup_t, tr(m_w_up), tr(v_w_up)), "w_down": (w_down, m_w_down, v_w_down)}
    big = {name: None for name in masters}

    def adam(name, land, l):
        big[name] = _adam_sum(land, *masters[name], l, big[name], f"adam_{name}{l}")

    acc_in, acc_mix, acc_mlp, dsink = [None] * L, [None] * L, [None] * L, [None] * L
    d_win_above = None
    for l in reversed(range(L)):
        sv = saved[l]
        side = _Exchange([blocks(d_win_above)]) if l + 1 < L else None
        (da, dx1, dx2b, acc_mlp[l]), got = _mlp_bwd(dcur, sv["x1"], sv["a"], g_mlp[l:l + 1], w2[l], l, tm_in, side=side)
        if l + 1 < L:
            adam("w_in", got[0], l + 1)
        d_wdn = _wgrad(sv["a"], dx2b, F, 0, None, f"wgrad_down{l}", relu2=True)
        d_wup = _wgrad(da, sv["h2"], F, 0, None, f"wgrad_up{l}")
        (dgt, datt, dya, dyc, dx1b, acc_mix[l]), got = _mixer_bwd(
            dx1, sv["gt"], sv["ya"], sv["yc"], sv["zb"], w3[l], sv["bg"], cw[l], l, S, tm,
            side=_Exchange([blocks(d_wdn)]))
        adam("w_down", got[0], l)
        d_wo, d_wao, d_wco = _wgrad_square(
            [(sv["mg"], dx1b, False), (sv["att"], dya, True), (sv["co"], dyc, False)], f"wgrad_mixer{l}")
        d_win = _wgrad(dgt, sv["h"], NP, NQ + 2 * NKV, None, f"wgrad_gates{l}")
        (dq, dk, dv, dsink[l]), got = _attn_bwd(
            sv["q"], sv["k"], sv["v"], sv["att"], datt, sv["sm"], l, S,
            side=_Exchange([blocks(d_wup), blocks(d_wo), blocks(d_wao), blocks(d_wco)]))
        for name, land in zip(["w_up", "w_o", "w_attn_out", "w_conv_out"], got):
            adam(name, land, l)
        d_win = _wgrad(dq, sv["h"], NP, 0, d_win, f"wgrad_q{l}", a_is_transposed=True)
        d_win = _wgrad(dk, sv["h"], NP, NQ, d_win, f"wgrad_k{l}", a_is_transposed=True)
        d_win = _wgrad(dv, sv["h"], NP, NQ + NKV, d_win, f"wgrad_v{l}", a_is_transposed=True)
        side = _Exchange([blocks(d_win)]) if l == 0 else None
        (dcur, acc_in[l]), got_in = _inproj_bwd(dgt, dq, dk, dv, sv["x"], dx1, g_mix[l:l + 1], wint[l], l, tm_in, side=side)
        d_win_above = d_win

    dsinks = jnp.stack([dsink[l][:, :, :, 0].sum(axis=0).reshape(16) for l in range(L)])
    small = _pack_small(
        jnp.concatenate([acc_in[l][0:1] for l in range(L)]),
        jnp.stack([acc_mix[l][0:2].reshape(2 * D) for l in range(L)]),
        dsinks,
        jnp.concatenate([acc_mix[l][2:3] for l in range(L)]),
        jnp.concatenate([acc_mlp[l][0:1] for l in range(L)]),
        acc_loss[0],
        jnp.concatenate([acc_mix[l][3:6] for l in range(L)]),
        acc_loss[1:2])
    adam("w_in", got_in[0], 0)
    (small_land,) = _remote_only(_Exchange([], [small]), "exchange_small")

    row = lambda t: t.reshape(1, D)
    small_out, loss = _adam_small(small_land, me.reshape(1).astype(jnp.int32), {
        "g_mix": (g_mix, m_g_mix, v_g_mix), "b_gates": (b_gates, m_b_gates, v_b_gates), "sinks": (sinks, m_sinks, v_sinks),
        "conv_b": (conv_b, m_conv_b, v_conv_b), "g_mlp": (g_mlp, m_g_mlp, v_g_mlp),
        "g_final": (row(g_final), row(m_g_final), row(v_g_final)), "conv_w": (conv_w, m_conv_w, v_conv_w)})
    small_out["g_final"] = [t.reshape(D) for t in small_out["g_final"]]
    for name in ("w_in", "w_up"):
        big[name] = [tr(o) for o in big[name]]
    order = ["g_mix", "w_in", "b_gates", "sinks", "w_attn_out", "conv_w", "conv_b", "w_conv_out", "w_o", "g_mlp",
             "w_up", "w_down", "g_final"]
    out = [loss.reshape(()), dcur.reshape(nseq, S, D)]
    for kind in range(4):
        for name in order:
            out.append(big[name][kind] if name in big else small_out[name][kind])
    return tuple(out)
```

```python
import numpy as np
import jax
import jax.numpy as jnp
from jax import lax
from jax.experimental import pallas as pl
from jax.experimental.pallas import tpu as pltpu

D = 1024
NG = 5 * D
NQ = 1024
NKV = 256
NP = NQ + 2 * NKV + NG
F = 4096
HD = 64
GQ = 4
WIN = 128
L = 2
NDEV = 8
EPS = 1e-6
NEG = -1e30
SMALL_ROWS = 24
LR, B1, B2, AEPS, WD, STEP = 0.001, 0.9, 0.999, 1e-08, 0.01, 10

BF = jnp.bfloat16
F32 = jnp.float32
MESH = pl.DeviceIdType.MESH
VMEM_LIMIT = 60 * 1024 * 1024
ANY = pl.BlockSpec(memory_space=pl.ANY)

NN = ((1,), (0,))
NT = ((1,), (1,))
TN = ((0,), (0,))


def _dot(a, b, dims):
    return lax.dot_general(a, b, (dims, ((), ())), preferred_element_type=F32)


def _resident(shape, imap):
    return pl.BlockSpec(shape, imap, pipeline_mode=pl.Buffered(1))


def _position():
    return lax.axis_index("x"), lax.axis_index("y"), lax.axis_index("c")


class _Gather:
    def __init__(self, shards):
        n = len(shards)
        self.inputs = list(shards)
        self.out_shape = [jax.ShapeDtypeStruct((s.shape[0], NDEV) + s.shape[1:], s.dtype) for s in shards]
        self.scratch = [pltpu.SemaphoreType.DMA((n, 7)), pltpu.SemaphoreType.DMA((n, 7)), pltpu.SemaphoreType.DMA((n,))]

    def _plan(self, src, dst, sems):
        send_sems, recv_sems, local_sems = sems
        n = len(src)
        x, y, c = _position()
        me, sibling = (x, y, c), (x, y, 1 - c)
        chips = [(1 - x, y), (x, 1 - y), (1 - x, 1 - y)]

        def rows(a, p):
            return dst[a].at[:, 4 * p[0] + 2 * p[1] + p[2]]

        def copy(a, k, block, to, from_src=False):
            return pltpu.make_async_remote_copy(
                src_ref=src[a] if from_src else rows(a, block), dst_ref=rows(a, block),
                send_sem=send_sems.at[a, k], recv_sem=recv_sems.at[a, k], device_id=to, device_id_type=MESH)

        mine = [pltpu.make_async_copy(src[a], rows(a, me), local_sems.at[a]) for a in range(n)]
        first = []
        for a in range(n):
            first.append(copy(a, 0, me, sibling, True))
            first += [copy(a, 1 + t, me, (*chip, c), True) for t, chip in enumerate(chips)]
        return n, c, me, sibling, chips, copy, mine, first

    def start(self, src, dst, sems):
        *_, mine, first = self._plan(src, dst, sems)
        for cp in mine + first:
            cp.start()

    def relay(self, src, dst, sems):
        n, c, me, sibling, chips, copy, _, _ = self._plan(src, dst, sems)
        for t, chip in enumerate(chips):
            for a in range(n):
                copy(a, 1 + t, (*chip, c), me).wait_recv()
                copy(a, 4 + t, (*chip, c), sibling).start()

    def finish(self, src, dst, sems):
        n, c, me, sibling, chips, copy, mine, first = self._plan(src, dst, sems)
        for a in range(n):
            copy(a, 0, sibling, me).wait_recv()
            for t, chip in enumerate(chips):
                copy(a, 4 + t, (*chip, 1 - c), me).wait_recv()
        for cp in first + [copy(a, 4 + t, (*chip, c), sibling) for t, chip in enumerate(chips) for a in range(n)]:
            cp.wait_send()
        for cp in mine:
            cp.wait()


class _Exchange:
    def __init__(self, grads, everyone=()):
        self.inputs = list(grads) + list(everyone)
        self.n_blocked = len(grads)
        n = len(self.inputs)
        self.out_shape = [jax.ShapeDtypeStruct(g.shape, g.dtype) for g in grads]
        self.out_shape += [jax.ShapeDtypeStruct((NDEV,) + e.shape, e.dtype) for e in everyone]
        self.scratch = [pltpu.SemaphoreType.DMA((n, 7)), pltpu.SemaphoreType.DMA((n, 7)), pltpu.SemaphoreType.DMA((n,))]

    def _plan(self, src, land, sems):
        send_sems, recv_sems, local_sems = sems
        x, y, c = _position()
        me = 4 * x + 2 * y + c

        def parts(peer_idx):
            return [(s.at[peer_idx] if a < self.n_blocked else s, land[a].at[me]) for a, s in enumerate(src)]

        local = [pltpu.make_async_copy(s, d, local_sems.at[a]) for a, (s, d) in enumerate(parts(me))]
        sent = []
        for rel in range(1, NDEV):
            px = 1 - x if rel & 4 else x
            py = 1 - y if rel & 2 else y
            pc = 1 - c if rel & 1 else c
            for a, (s, d) in enumerate(parts(4 * px + 2 * py + pc)):
                sent.append(pltpu.make_async_remote_copy(
                    src_ref=s, dst_ref=d, send_sem=send_sems.at[a, rel - 1], recv_sem=recv_sems.at[a, rel - 1],
                    device_id=(px, py, pc), device_id_type=MESH))
        return local, sent

    def start(self, src, land, sems):
        local, sent = self._plan(src, land, sems)
        for cp in local + sent:
            cp.start()

    def relay(self, src, land, sems):
        pass

    def finish(self, src, land, sems):
        local, sent = self._plan(src, land, sems)
        for cp in sent:
            cp.wait_recv()
        for cp in sent:
            cp.wait_send()
        for cp in local:
            cp.wait()


def _call(body, *, name, grid, in_specs, out_specs, out_shape, args, scratch_shapes=(), aliases=None, side=None,
          relay_at=1.0):
    sem = ("arbitrary",) * len(grid)
    if side is None:
        outs = pl.pallas_call(
            body, name=name, grid=grid, in_specs=in_specs, out_specs=out_specs, out_shape=out_shape,
            scratch_shapes=list(scratch_shapes), input_output_aliases=aliases or {},
            compiler_params=pltpu.CompilerParams(dimension_semantics=sem, vmem_limit_bytes=VMEM_LIMIT))(*args)
        return outs, []
    ni, no, ns = len(in_specs), len(out_specs), len(scratch_shapes)
    si, so = len(side.inputs), len(side.out_shape)

    def hosted(*refs):
        ins, refs = refs[:ni], refs[ni:]
        sins, refs = refs[:si], refs[si:]
        outs, refs = refs[:no], refs[no:]
        souts, refs = refs[:so], refs[so:]
        scr, sscr = refs[:ns], refs[ns:]
        step = pl.program_id(0)
        for d in range(1, len(grid)):
            step = step * grid[d] + pl.program_id(d)
        last = int(np.prod(grid)) - 1

        @pl.when(step == 0)
        def _():
            side.start(sins, souts, sscr)

        body(*ins, *outs, *scr)

        @pl.when(step == min(int(relay_at * last), last))
        def _():
            side.relay(sins, souts, sscr)

        @pl.when(step == last)
        def _():
            side.finish(sins, souts, sscr)

    outs = pl.pallas_call(
        hosted, name=name, grid=grid, in_specs=list(in_specs) + [ANY] * si, out_specs=list(out_specs) + [ANY] * so,
        out_shape=list(out_shape) + side.out_shape, scratch_shapes=list(scratch_shapes) + side.scratch,
        input_output_aliases=aliases or {},
        compiler_params=pltpu.CompilerParams(dimension_semantics=sem, vmem_limit_bytes=VMEM_LIMIT, has_side_effects=True),
    )(*args, *side.inputs)
    return outs[:no], outs[no:]


def _remote_only(side, name):
    n = len(side.inputs)

    def body(*refs):
        src, dst, sems = refs[:n], refs[n:n + len(side.out_shape)], refs[n + len(side.out_shape):]
        side.start(src, dst, sems)
        side.relay(src, dst, sems)
        side.finish(src, dst, sems)

    return pl.pallas_call(
        body, name=name, in_specs=[ANY] * n, out_specs=[ANY] * len(side.out_shape), out_shape=side.out_shape,
        scratch_shapes=side.scratch, compiler_params=pltpu.CompilerParams(has_side_effects=True))(*side.inputs)


def _rms(x, g):
    r = lax.rsqrt(jnp.mean(x * x, axis=-1, keepdims=True) + EPS)
    return x * r * g


def _rms_bwd(dy, x, g):
    r = lax.rsqrt(jnp.mean(x * x, axis=-1, keepdims=True) + EPS)
    xh = x * r
    dxh = dy * g
    dx = r * (dxh - xh * jnp.mean(dxh * xh, axis=-1, keepdims=True))
    return dx, jnp.sum(dy * xh, axis=0, keepdims=True)


def _zero_at_first_step(acc_ref):
    first = pl.program_id(0) == 0

    @pl.when(first)
    def _():
        acc_ref[...] = jnp.zeros_like(acc_ref)


ROW1 = lambda: _resident((1, D), lambda i: (0, 0))
ACC = lambda: pl.BlockSpec((8, D), lambda i: (0, 0))


def _inproj_fwd(x, g, wint, l, tm, side=None, relay_at=1.0):
    T = x.shape[0]

    def body(x_ref, g_ref, w_ref, gt_ref, q_ref, k_ref, v_ref, h_ref):
        h = _rms(x_ref[...], g_ref[...]).astype(BF)
        h_ref[...] = h
        q_ref[...] = _dot(w_ref[0:NQ, :], h, NT).astype(BF)
        k_ref[...] = _dot(w_ref[NQ:NQ + NKV, :], h, NT).astype(BF)
        v_ref[...] = _dot(w_ref[NQ + NKV:NQ + 2 * NKV, :], h, NT).astype(BF)
        for s in range(5):
            lo = NQ + 2 * NKV + s * D
            gt_ref[:, s * D:(s + 1) * D] = _dot(h, w_ref[lo:lo + D, :], NT).astype(BF)

    tok = lambda w: pl.BlockSpec((tm, w), lambda i: (i, 0))
    feat = lambda w: pl.BlockSpec((w, tm), lambda i: (0, i))
    return _call(
        body, name=f"inproj_fwd{l}", grid=(T // tm,),
        in_specs=[tok(D), ROW1(), _resident((NP, D), lambda i: (0, 0))],
        out_specs=[tok(NG), feat(NQ), feat(NKV), feat(NKV), tok(D)],
        out_shape=[jax.ShapeDtypeStruct((T, NG), BF)] + [jax.ShapeDtypeStruct((w, T), BF) for w in (NQ, NKV, NKV)]
        + [jax.ShapeDtypeStruct((T, D), BF)],
        args=(x, g, wint), side=side, relay_at=relay_at)


def _band_geometry():
    j = lax.broadcasted_iota(jnp.int32, (2 * WIN, WIN), 0)
    r = lax.broadcasted_iota(jnp.int32, (2 * WIN, WIN), 1)
    dist = WIN + r - j
    dist0 = r - j
    return (dist.astype(F32), (dist >= 0) & (dist < WIN)), (dist0.astype(F32), dist0 >= 0)


def _pair_biases(sm_ref, pj):
    return [[jnp.where(ok, -sm_ref[1, pj * 2 * GQ + h] * dist, NEG) for h in range(2 * GQ)]
            for dist, ok in _band_geometry()]


def _reduce_rows(x, pair, whole):
    while x.shape[0] > 8:
        half = x.shape[0] // 2
        x = pair(x[:half], x[half:])
    return whole(x, axis=0, keepdims=True)


def _heads_on_lanes(ref, kvh, r0):
    return jnp.concatenate([ref[(kvh * GQ + g) * HD:(kvh * GQ + g + 1) * HD, pl.ds(r0, WIN)] for g in range(GQ)], axis=1)


def _band_probs(sm_ref, head0, q_ref, k_ref, r0, p0, kvh, biases):
    qt = _heads_on_lanes(q_ref, kvh, r0) * jnp.asarray(HD ** -0.5, BF)
    kt = k_ref[kvh * HD:(kvh + 1) * HD, pl.ds(p0, 2 * WIN)]
    st = _dot(kt, qt, TN)
    heads = []
    for g in range(GQ):
        sink = sm_ref[0, head0 + g]
        s = st[:, g * WIN:(g + 1) * WIN] + biases[kvh * GQ + g]
        m = jnp.maximum(_reduce_rows(s, jnp.maximum, jnp.max), sink)
        p = jnp.exp(s - m)
        ps = jnp.exp(sink - m)
        heads.append((p, ps, 1.0 / (_reduce_rows(p, jnp.add, jnp.sum) + ps)))
    return qt, kt, heads


def _attn_fwd(q, k, v, sm, l, S, side=None, relay_at=1.0):
    T = q.shape[1]
    nblk = S // WIN
    unroll = next(u for u in (5, 3, 1) if (nblk - 1) % u == 0)

    def body(sm_ref, q_ref, k_ref, v_ref, o_ref):
        pj = pl.program_id(1)
        biases = _pair_biases(sm_ref, pj)

        def block(i, first):
            r0 = 0 if first else pl.multiple_of(i * WIN, WIN)
            p0 = 0 if first else pl.multiple_of(i * WIN - WIN, WIN)
            for kvh in range(2):
                head0 = (2 * pj + kvh) * GQ
                _, _, heads = _band_probs(sm_ref, head0, q_ref, k_ref, r0, p0, kvh, biases[first])
                pt = jnp.concatenate([p.astype(BF) for p, _, _ in heads], axis=1)
                ot = _dot(v_ref[kvh * HD:(kvh + 1) * HD, pl.ds(p0, 2 * WIN)], pt, NN)
                for g in range(GQ):
                    rows = slice((kvh * GQ + g) * HD, (kvh * GQ + g + 1) * HD)
                    o_ref[rows, pl.ds(r0, WIN)] = (ot[:, g * WIN:(g + 1) * WIN] * heads[g][2]).astype(BF)

        block(0, True)

        def rest(t, c):
            for u in range(unroll):
                block(1 + unroll * t + u, False)
            return c

        lax.fori_loop(0, (nblk - 1) // unroll, rest, 0)

    wide = lambda: pl.BlockSpec((2 * GQ * HD, S), lambda s, p: (p, s))
    narrow = lambda: pl.BlockSpec((2 * HD, S), lambda s, p: (p, s))
    return _call(
        body, name=f"attn_fwd{l}", grid=(T // S, 2),
        in_specs=[pl.BlockSpec(memory_space=pltpu.SMEM), wide(), narrow(), narrow()],
        out_specs=[wide()], out_shape=[jax.ShapeDtypeStruct((D, T), BF)],
        args=(sm, q, k, v), side=side, relay_at=relay_at)


def _shift_rows(y, k, edge_rows, down):
    n = y.shape[0]
    rid = lax.broadcasted_iota(jnp.int32, y.shape, 0)
    out = pltpu.roll(y, k if down else n - k, 0)
    for t, row in enumerate(edge_rows):
        out = jnp.where(rid == (t if down else n - k + t), row, out)
    return out


def _mixer_fwd(x, gt, att, w3, bg, cw, cbias, l, S, tm, side=None, relay_at=1.0):
    T = x.shape[0]

    def body(x_ref, cb_ref, cc_ref, cu_ref, ga_ref, gc_ref, cch_ref, cuh_ref, att_ref, wao_ref, wco_ref, wo_ref,
             bg_ref, cw_ref, cbias_ref, x1_ref, mg_ref, co_ref, ya_ref, yc_ref, zb_ref):
        first = (pl.program_id(0) * tm) % S == 0
        y = cc_ref[...].astype(F32) * cu_ref[...].astype(F32)
        hy1 = cch_ref[15:16, :].astype(F32) * cuh_ref[15:16, :].astype(F32)
        hy2 = cch_ref[14:15, :].astype(F32) * cuh_ref[14:15, :].astype(F32)
        hy1, hy2 = jnp.where(first, 0.0, hy1), jnp.where(first, 0.0, hy2)
        z = (cw_ref[0:1, :] * _shift_rows(y, 2, [hy2, hy1], True) + cw_ref[1:2, :] * _shift_rows(y, 1, [hy1], True)
             + cw_ref[2:3, :] * y)
        zb = z + cbias_ref[...]
        zb_ref[...] = zb.astype(BF)
        co = (cb_ref[...].astype(F32) * zb).astype(BF)
        co_ref[...] = co
        yc = _dot(co, wco_ref[...], NN)
        ya = _dot(att_ref[...], wao_ref[...], TN)
        ya_ref[...] = ya.astype(BF)
        yc_ref[...] = yc.astype(BF)
        sa = jax.nn.sigmoid(ga_ref[...].astype(F32) + bg_ref[0:1, :])
        sc = jax.nn.sigmoid(gc_ref[...].astype(F32) + bg_ref[1:2, :])
        mg = (sa * ya + sc * yc).astype(BF)
        mg_ref[...] = mg
        x1_ref[...] = x_ref[...] + _dot(mg, wo_ref[...], NN)

    tok = lambda: pl.BlockSpec((tm, D), lambda i: (i, 0))
    seg = lambda s: pl.BlockSpec((tm, D), lambda i: (i, s))
    halo = lambda s: pl.BlockSpec((16, D), lambda i: (jnp.maximum(i * (tm // 16) - 1, 0), s))
    wsp = lambda k: _resident((None, D, D), lambda i: (k, 0, 0))
    row = lambda n: _resident((n, D), lambda i: (0, 0))
    return _call(
        body, name=f"mixer_fwd{l}", grid=(T // tm,),
        in_specs=[tok(), seg(0), seg(1), seg(2), seg(3), seg(4), halo(1), halo(2),
                  pl.BlockSpec((D, tm), lambda i: (0, i)), wsp(0), wsp(1), wsp(2), row(2), row(8), row(1)],
        out_specs=[tok()] * 6,
        out_shape=[jax.ShapeDtypeStruct((T, D), dt) for dt in (F32, BF, BF, BF, BF, BF)],
        args=(x, gt, gt, gt, gt, gt, gt, gt, att, w3, w3, w3, bg, cw, cbias), side=side, relay_at=relay_at)


def _mlp_fwd(x1, g, w2, l, tm, head=None):
    T = x1.shape[0]
    nt = T // tm
    FC = 1024

    def body(x_ref, g_ref, wup_ref, wdn_ref, *rest):
        out_ref, h_ref, a_ref = rest[-4:-1] if head else rest
        x = x_ref[...]
        h = _rms(x, g_ref[...]).astype(BF)
        h_ref[...] = h
        acc = x
        for c in range(F // FC):
            a = _dot(h, wup_ref[c * FC:(c + 1) * FC, :], NT)
            a_ref[:, c * FC:(c + 1) * FC] = a.astype(BF)
            u = jnp.maximum(a, 0.0)
            acc = acc + _dot((u * u).astype(BF), wdn_ref[c * FC:(c + 1) * FC, :], NN)
        if not head:
            out_ref[...] = acc
            return
        t_ref, gf_ref, acc_ref = rest[0], rest[1], rest[-1]
        _zero_at_first_step(acc_ref)
        gf = gf_ref[...]
        err = _rms(acc, gf) - t_ref[...]
        out_ref[...], dg = _rms_bwd(err * (1.0 / D), acc, gf)
        acc_ref[0:1, :] += dg
        acc_ref[1:2, :] += jnp.sum(err * err, axis=0, keepdims=True)

        @pl.when(pl.program_id(0) == nt - 1)
        def _():
            acc_ref[1:2, :] = jnp.zeros((1, D), F32) + (0.5 / D) * jnp.sum(acc_ref[1:2, :])

    tok = lambda w: pl.BlockSpec((tm, w), lambda i: (i, 0))
    wsp = lambda k: _resident((None, F, D), lambda i: (k, 0, 0))
    outs, _ = _call(
        body, name=f"mlp_fwd{l}", grid=(nt,),
        in_specs=[tok(D), ROW1(), wsp(0), wsp(1)] + ([tok(D), ROW1()] if head else []),
        out_specs=[tok(D), tok(D), tok(F)] + ([ACC()] if head else []),
        out_shape=[jax.ShapeDtypeStruct((T, D), F32), jax.ShapeDtypeStruct((T, D), BF), jax.ShapeDtypeStruct((T, F), BF)]
        + ([jax.ShapeDtypeStruct((8, D), F32)] if head else []),
        args=(x1, g, w2, w2) + (tuple(head) if head else ()))
    return outs


def _mlp_bwd(dx2, x1, a, g, w2, l, tm, side=None):
    T = dx2.shape[0]
    FC = 1024

    def body(d_ref, x_ref, a_ref, g_ref, wup_ref, wdn_ref, da_ref, dx1_ref, db_ref, acc_ref):
        _zero_at_first_step(acc_ref)
        d = d_ref[...]
        db = d.astype(BF)
        db_ref[...] = db
        dh = jnp.zeros((tm, D), F32)
        for c in range(F // FC):
            du = _dot(db, wdn_ref[c * FC:(c + 1) * FC, :], NT)
            da = (du * (2.0 * jnp.maximum(a_ref[:, c * FC:(c + 1) * FC].astype(F32), 0.0))).astype(BF)
            da_ref[:, c * FC:(c + 1) * FC] = da
            dh = dh + _dot(da, wup_ref[c * FC:(c + 1) * FC, :], NN)
        dx, dg = _rms_bwd(dh, x_ref[...], g_ref[...])
        dx1_ref[...] = d + dx
        acc_ref[0:1, :] += dg

    tok = lambda w: pl.BlockSpec((tm, w), lambda i: (i, 0))
    wsp = lambda k: _resident((None, F, D), lambda i: (k, 0, 0))
    return _call(
        body, name=f"mlp_bwd{l}", grid=(T // tm,),
        in_specs=[tok(D), tok(D), tok(F), ROW1(), wsp(0), wsp(1)],
        out_specs=[tok(F), tok(D), tok(D), ACC()],
        out_shape=[jax.ShapeDtypeStruct((T, F), BF), jax.ShapeDtypeStruct((T, D), F32),
                   jax.ShapeDtypeStruct((T, D), BF), jax.ShapeDtypeStruct((8, D), F32)],
        args=(dx2, x1, a, g, w2, w2), side=side)


def _mixer_bwd(dx1, gt, ya, yc, zb, w3, bg, cw, l, S, tm, side=None):
    T = dx1.shape[0]
    nt = T // tm

    def body(d_ref, cb_ref, cc_ref, cu_ref, ga_ref, gc_ref, ya_ref, yc_ref, zb_ref, wao_ref, wco_ref, wo_ref,
             bg_ref, cw_ref, dg_ref, datt_ref, dya_ref, dyc_ref, db_ref, acc_ref, carry_ref):
        ti = nt - 1 - pl.program_id(0)
        _zero_at_first_step(acc_ref)

        @pl.when(((ti + 1) * tm) % S == 0)
        def _():
            carry_ref[...] = jnp.zeros_like(carry_ref)

        db = d_ref[...].astype(BF)
        db_ref[...] = db
        dm = _dot(db, wo_ref[...], NT)
        sa = jax.nn.sigmoid(ga_ref[...].astype(F32) + bg_ref[0:1, :])
        sc = jax.nn.sigmoid(gc_ref[...].astype(F32) + bg_ref[1:2, :])
        dya32 = dm * sa
        dyc32 = dm * sc
        dya = dya32.astype(BF)
        dyc = dyc32.astype(BF)
        dya_ref[...] = dya
        dyc_ref[...] = dyc
        dga = dya32 * ya_ref[...].astype(F32) * (1.0 - sa)
        dgc = dyc32 * yc_ref[...].astype(F32) * (1.0 - sc)
        dg_ref[:, 3 * D:4 * D] = dga.astype(BF)
        dg_ref[:, 4 * D:5 * D] = dgc.astype(BF)
        acc_ref[0:1, :] += jnp.sum(dga, axis=0, keepdims=True)
        acc_ref[1:2, :] += jnp.sum(dgc, axis=0, keepdims=True)
        datt_ref[...] = _dot(wao_ref[...], dya, NT).astype(BF)
        dco = _dot(dyc, wco_ref[...], NT)

        cc = cc_ref[...].astype(F32)
        cu = cu_ref[...].astype(F32)
        y = cc * cu
        dg_ref[:, 0:D] = (dco * zb_ref[...].astype(F32)).astype(BF)
        dz = dco * cb_ref[...].astype(F32)
        u1 = _shift_rows(dz, 1, [carry_ref[0:1, :]], False)
        u2 = _shift_rows(dz, 2, [carry_ref[0:1, :], carry_ref[1:2, :]], False)
        acc_ref[2:3, :] += jnp.sum(dz, axis=0, keepdims=True)
        acc_ref[3:4, :] += jnp.sum(u2 * y, axis=0, keepdims=True)
        acc_ref[4:5, :] += jnp.sum(u1 * y, axis=0, keepdims=True)
        acc_ref[5:6, :] += jnp.sum(dz * y, axis=0, keepdims=True)
        dy = cw_ref[2:3, :] * dz + cw_ref[1:2, :] * u1 + cw_ref[0:1, :] * u2
        dg_ref[:, D:2 * D] = (dy * cu).astype(BF)
        dg_ref[:, 2 * D:3 * D] = (dy * cc).astype(BF)
        carry_ref[...] = dz[0:8, :]

    tok = lambda w=D: pl.BlockSpec((tm, w), lambda i: (nt - 1 - i, 0))
    seg = lambda s: pl.BlockSpec((tm, D), lambda i: (nt - 1 - i, s))
    wsp = lambda k: _resident((None, D, D), lambda i: (k, 0, 0))
    row = lambda n: _resident((n, D), lambda i: (0, 0))
    return _call(
        body, name=f"mixer_bwd{l}", grid=(nt,),
        in_specs=[tok(), seg(0), seg(1), seg(2), seg(3), seg(4), tok(), tok(), tok(), wsp(0), wsp(1), wsp(2),
                  row(2), row(8)],
        out_specs=[tok(NG), pl.BlockSpec((D, tm), lambda i: (0, nt - 1 - i)), tok(), tok(), tok(), ACC()],
        out_shape=[jax.ShapeDtypeStruct((T, NG), BF), jax.ShapeDtypeStruct((D, T), BF)]
        + [jax.ShapeDtypeStruct((T, D), BF)] * 3 + [jax.ShapeDtypeStruct((8, D), F32)],
        scratch_shapes=[pltpu.VMEM((8, D), F32)],
        args=(dx1, gt, gt, gt, gt, gt, ya, yc, zb, w3, w3, w3, bg, cw), side=side)


def _attn_bwd(q, k, v, att, datt, sm, l, S, side=None):
    T = q.shape[1]
    nblk = S // WIN
    unroll = next(u for u in (5, 3, 1) if (nblk - 1) % u == 0)
    scale = HD ** -0.5

    def body(sm_ref, q_ref, k_ref, v_ref, o_ref, do_ref, dq_ref, dk_ref, dv_ref, ds_ref, dka_ref, dva_ref):
        pj = pl.program_id(1)
        biases = _pair_biases(sm_ref, pj)
        dka_ref[...] = jnp.zeros_like(dka_ref)
        dva_ref[...] = jnp.zeros_like(dva_ref)

        def block(i, first, dsinks):
            r0 = 0 if first else pl.multiple_of(i * WIN, WIN)
            p0 = 0 if first else pl.multiple_of(i * WIN - WIN, WIN)
            out = []
            for kvh in range(2):
                head0 = (2 * pj + kvh) * GQ
                rows = slice(kvh * HD, (kvh + 1) * HD)
                dob = _heads_on_lanes(do_ref, kvh, r0)
                dpt = _dot(v_ref[rows, pl.ds(p0, 2 * WIN)], dob, TN)
                qt, kt, heads = _band_probs(sm_ref, head0, q_ref, k_ref, r0, p0, kvh, biases[first])
                inv = jnp.concatenate([h[2] for h in heads], axis=1)
                do32 = dob.astype(F32)
                delta = jnp.sum(do32 * _heads_on_lanes(o_ref, kvh, r0).astype(F32), axis=0, keepdims=True) * inv
                dosb = (do32 * inv).astype(BF)
                dst = jnp.concatenate(
                    [(p * (dpt[:, g * WIN:(g + 1) * WIN] * h_inv - delta[:, g * WIN:(g + 1) * WIN])).astype(BF)
                     for g, (p, _, h_inv) in enumerate(heads)], axis=1)
                pt = jnp.concatenate([p.astype(BF) for p, _, _ in heads], axis=1)
                dqt = _dot(kt, dst, NN) * scale
                for g in range(GQ):
                    hr = slice((kvh * GQ + g) * HD, (kvh * GQ + g + 1) * HD)
                    dq_ref[hr, pl.ds(r0, WIN)] = dqt[:, g * WIN:(g + 1) * WIN].astype(BF)
                dka_ref[rows, pl.ds(p0, 2 * WIN)] += _dot(qt, dst, NT)
                dva_ref[rows, pl.ds(p0, 2 * WIN)] += _dot(dosb, pt, NT)
                ps = jnp.concatenate([h[1] for h in heads], axis=1)
                out.append(dsinks[kvh] - ps * delta)
            return tuple(out)

        zero = jnp.zeros((1, GQ * WIN), F32)
        def rest(t, c):
            for u in range(unroll):
                c = block(1 + unroll * t + u, False, c)
            return c

        dsinks = lax.fori_loop(0, (nblk - 1) // unroll, rest, block(0, True, (zero, zero)))
        for kvh in range(2):
            for g in range(GQ):
                tot = jnp.sum(dsinks[kvh][:, g * WIN:(g + 1) * WIN])
                ds_ref[kvh * GQ + g:kvh * GQ + g + 1, :] = jnp.zeros((1, 128), F32) + tot
        dk_ref[...] = dka_ref[...].astype(BF)
        dv_ref[...] = dva_ref[...].astype(BF)

    wide = lambda: pl.BlockSpec((2 * GQ * HD, S), lambda s, p: (p, s))
    narrow = lambda: pl.BlockSpec((2 * HD, S), lambda s, p: (p, s))
    return _call(
        body, name=f"attn_bwd{l}", grid=(T // S, 2),
        in_specs=[pl.BlockSpec(memory_space=pltpu.SMEM), wide(), narrow(), narrow(), wide(), wide()],
        out_specs=[wide(), narrow(), narrow(), pl.BlockSpec((None, None, 8, 128), lambda s, p: (s, p, 0, 0))],
        out_shape=[jax.ShapeDtypeStruct((NQ, T), BF), jax.ShapeDtypeStruct((NKV, T), BF),
                   jax.ShapeDtypeStruct((NKV, T), BF), jax.ShapeDtypeStruct((T // S, 2, 8, 128), F32)],
        scratch_shapes=[pltpu.VMEM((2 * HD, S), F32), pltpu.VMEM((2 * HD, S), F32)],
        args=(sm, q, k, v, att, datt), side=side)


def _inproj_bwd(dgt, dq, dk, dv, x, dres, g, wint, l, tm, side=None):
    T = x.shape[0]

    def body(dg_ref, dq_ref, dk_ref, dv_ref, x_ref, dr_ref, g_ref, w_ref, dx_ref, acc_ref):
        _zero_at_first_step(acc_ref)
        dh = _dot(dq_ref[...], w_ref[0:NQ, :], TN)
        dh = dh + _dot(dk_ref[...], w_ref[NQ:NQ + NKV, :], TN)
        dh = dh + _dot(dv_ref[...], w_ref[NQ + NKV:NQ + 2 * NKV, :], TN)
        dh = dh + _dot(dg_ref[...], w_ref[NQ + 2 * NKV:NP, :], NN)
        dx, dg = _rms_bwd(dh, x_ref[...], g_ref[...])
        dx_ref[...] = dr_ref[...] + dx
        acc_ref[0:1, :] += dg

    tok = lambda w: pl.BlockSpec((tm, w), lambda i: (i, 0))
    feat = lambda w: pl.BlockSpec((w, tm), lambda i: (0, i))
    return _call(
        body, name=f"inproj_bwd{l}", grid=(T // tm,),
        in_specs=[tok(NG), feat(NQ), feat(NKV), feat(NKV), tok(D), tok(D), ROW1(), _resident((NP, D), lambda i: (0, 0))],
        out_specs=[tok(D), ACC()],
        out_shape=[jax.ShapeDtypeStruct((T, D), F32), jax.ShapeDtypeStruct((8, D), F32)],
        args=(dgt, dq, dk, dv, x, dres, g, wint), side=side)


def _wgrad(a, b, rows, row0, into, name, relu2=False, a_is_transposed=False):
    M, T = a.shape if a_is_transposed else a.shape[::-1]
    tmm = next(t for t in (1024, 512, 256) if M % t == 0 and row0 % t == 0)
    tk = min(4096 if M > 4096 else 2048, T)
    nk = T // tk
    blk0 = row0 // tmm

    def body(*refs):
        a_ref, b_ref = refs[0], refs[1]
        o_ref, acc_ref = refs[-2], refs[-1]
        kk = pl.program_id(1)

        @pl.when(kk == 0)
        def _():
            acc_ref[...] = jnp.zeros_like(acc_ref)

        av = a_ref[...]
        if relu2:
            t = jnp.maximum(av.astype(F32), 0.0)
            av = (t * t).astype(BF)
        acc_ref[...] += _dot(av, b_ref[...], NN if a_is_transposed else TN)

        @pl.when(kk == nk - 1)
        def _():
            o_ref[...] = acc_ref[...].astype(BF)

    a_spec = pl.BlockSpec((tmm, tk), lambda j, kk: (j, kk)) if a_is_transposed else pl.BlockSpec((tk, tmm), lambda j, kk: (kk, j))
    in_specs = [a_spec, pl.BlockSpec((tk, D), lambda j, kk: (kk, 0))]
    args = [a, b]
    if into is not None:
        in_specs.append(ANY)
        args.append(into)
    (out,), _ = _call(
        body, name=name, grid=(M // tmm, nk), in_specs=in_specs,
        out_specs=[pl.BlockSpec((tmm, D), lambda j, kk: (blk0 + j, 0))],
        out_shape=[jax.ShapeDtypeStruct((rows, D), BF)], scratch_shapes=[pltpu.VMEM((tmm, D), F32)],
        aliases={2: 0} if into is not None else None, args=args)
    return out


def _wgrad_square(jobs, name):
    n = len(jobs)
    T = jobs[0][1].shape[0]
    tk = min(1024, T)
    nk = T // tk

    def body(*refs):
        ins, outs, acc_ref = refs[:2 * n], refs[2 * n:3 * n], refs[3 * n]
        j, kk = pl.program_id(0), pl.program_id(1)

        @pl.when(kk == 0)
        def _():
            acc_ref[...] = jnp.zeros_like(acc_ref)

        for p, (_, _, a_t) in enumerate(jobs):
            @pl.when(j == p)
            def _(p=p, a_t=a_t):
                acc_ref[...] += _dot(ins[2 * p][...], ins[2 * p + 1][...], NN if a_t else TN)

                @pl.when(kk == nk - 1)
                def _():
                    outs[p][...] = acc_ref[...].astype(BF)

    def step(p):
        return lambda j, kk: jnp.where(j == p, kk, jnp.where(j > p, nk - 1, 0))

    in_specs, args = [], []
    for p, (a, b, a_t) in enumerate(jobs):
        at = step(p)
        in_specs.append(pl.BlockSpec((D, tk), lambda j, kk, at=at: (0, at(j, kk))) if a_t
                        else pl.BlockSpec((tk, D), lambda j, kk, at=at: (at(j, kk), 0)))
        in_specs.append(pl.BlockSpec((tk, D), lambda j, kk, at=at: (at(j, kk), 0)))
        args += [a, b]
    outs, _ = _call(
        body, name=name, grid=(n, nk), in_specs=in_specs,
        out_specs=[pl.BlockSpec((D, D), lambda j, kk: (0, 0))] * n,
        out_shape=[jax.ShapeDtypeStruct((D, D), BF)] * n, scratch_shapes=[pltpu.VMEM((D, D), F32)], args=args)
    return outs


def _adamw(w, g, m, v):
    m = B1 * m + (1.0 - B1) * g
    v = B2 * v + (1.0 - B2) * (g * g)
    m_hat = m / (1.0 - B1 ** STEP)
    v_hat = v / (1.0 - B2 ** STEP)
    return -LR * (m_hat / (jnp.sqrt(v_hat) + AEPS) + WD * w), m, v


def _adam_sum(land, w, m, v, l, into, name, transposed=False):
    _, r, _ = land.shape
    tr = 208 if r % 208 == 0 else (256 if r % 256 == 0 else r)
    tc = 256

    def body(land_ref, w_ref, m_ref, v_ref, *rest):
        g_ref, d_ref, nm_ref, nv_ref = rest[-4:]
        g = land_ref[0].astype(F32)
        for s in range(1, NDEV):
            g = g + land_ref[s].astype(F32)
        if transposed:
            g = g.T
        g_ref[...] = g
        d_ref[...], nm_ref[...], nv_ref[...] = _adamw(w_ref[...], g, m_ref[...], v_ref[...])

    if transposed:
        blk = lambda: pl.BlockSpec((None, tc, r), lambda j: (l, j, 0))
        land_spec, grid = pl.BlockSpec((NDEV, r, tc), lambda j: (0, 0, j)), (D // tc,)
    else:
        blk = lambda: pl.BlockSpec((None, tr, D), lambda j: (l, j, 0))
        land_spec, grid = pl.BlockSpec((NDEV, tr, D), lambda j: (0, j, 0)), (r // tr,)
    in_specs = [land_spec, blk(), blk(), blk()]
    args = [land, w, m, v]
    aliases = None
    if into is not None:
        in_specs += [ANY] * 4
        args += list(into)
        aliases = {4 + t: t for t in range(4)}
    outs, _ = _call(body, name=name, grid=grid, in_specs=in_specs, out_specs=[blk()] * 4,
                    out_shape=[jax.ShapeDtypeStruct(w.shape, F32)] * 4, aliases=aliases, args=args)
    return outs


SMALL_NAMES = ("g_mix", "b_gates", "sinks", "conv_b", "g_mlp", "g_final", "conv_w")


def _adam_small(land, me, masters):
    n = len(SMALL_NAMES)
    lanes = D // NDEV

    def body(land_ref, me_ref, *refs):
        ins, outs, loss_ref, gs_ref = refs[:3 * n], refs[3 * n:7 * n], refs[7 * n], refs[7 * n + 1]
        g = land_ref[0]
        for s in range(1, NDEV):
            g = g + land_ref[s]
        gs_ref[...] = g

        def update(k, g_piece, idx):
            w_ref, m_ref, v_ref = ins[3 * k:3 * k + 3]
            outs[4 * k][idx] = g_piece
            outs[4 * k + 1][idx], outs[4 * k + 2][idx], outs[4 * k + 3][idx] = _adamw(w_ref[idx], g_piece, m_ref[idx], v_ref[idx])

        whole = (slice(None), slice(None))
        update(0, gs_ref[0:2, :], whole)
        for l in range(L):
            for h in range(2):
                update(1, gs_ref[2 + 2 * l + h:3 + 2 * l + h, :], (slice(l, l + 1), slice(h * D, (h + 1) * D)))
            update(2, gs_ref[6:7, 16 * l:16 * (l + 1)], (slice(l, l + 1), slice(None)))
        update(3, gs_ref[7:9, :], whole)
        update(4, gs_ref[9:11, :], whole)
        update(5, gs_ref[11:12, :], whole)
        mine = pl.ds(pl.multiple_of(me_ref[0] * lanes, lanes), lanes)
        for l in range(L):
            for k in range(3):
                update(6, gs_ref[12 + 3 * l + k:13 + 3 * l + k, mine], (l, slice(k, k + 1), slice(None)))
        loss_ref[...] = gs_ref[18:19, 0:1]

    flat = [t for name in SMALL_NAMES for t in masters[name]]
    vmem = pl.BlockSpec(memory_space=pltpu.VMEM)
    outs = pl.pallas_call(
        body, name="adam_small",
        in_specs=[vmem, pl.BlockSpec(memory_space=pltpu.SMEM)] + [vmem] * len(flat),
        out_shape=[jax.ShapeDtypeStruct(masters[name][0].shape, F32) for name in SMALL_NAMES for _ in range(4)]
        + [jax.ShapeDtypeStruct((1, 1), F32)],
        scratch_shapes=[pltpu.VMEM((SMALL_ROWS, D), F32)],
        compiler_params=pltpu.CompilerParams(vmem_limit_bytes=VMEM_LIMIT))(land, me, *flat)
    return {name: outs[4 * k:4 * k + 4] for k, name in enumerate(SMALL_NAMES)}, outs[-1]


def _pack_small(g_mix, b_gates, sinks, conv_b, g_mlp, g_final, conv_w_rows, extra):
    sink_row = jnp.zeros((1, D), F32).at[0, :2 * 16].set(sinks.reshape(-1))
    return jnp.concatenate([g_mix, b_gates.reshape(4, D), sink_row, conv_b, g_mlp, g_final.reshape(1, D),
                            conv_w_rows, extra, jnp.zeros((SMALL_ROWS - 19, D), F32)], axis=0)


def kernel(x, g_mix, w_in, b_gates, sinks, w_attn_out, conv_w, conv_b, w_conv_out, w_o, g_mlp, w_up, w_down, g_final, loss_target, m_g_mix, m_w_in, m_b_gates, m_sinks, m_w_attn_out, m_conv_w, m_conv_b, m_w_conv_out, m_w_o, m_g_mlp, m_w_up, m_w_down, m_g_final, v_g_mix, v_w_in, v_b_gates, v_sinks, v_w_attn_out, v_conv_w, v_conv_b, v_w_conv_out, v_w_o, v_g_mlp, v_w_up, v_w_down, v_g_final):
    nseq, S, _ = x.shape
    T = nseq * S
    tm_in = min(512, S)
    tm = min(512, S)
    xi, yi, ci = _position()
    me = 4 * xi + 2 * yi + ci
    tr = lambda t: jnp.swapaxes(t, 1, 2)
    blocks = lambda t: t.reshape(NDEV, t.shape[0] // NDEV, D)

    win_t, wup_t = tr(w_in), tr(w_up)
    sh_win = [win_t[l].astype(BF)[None] for l in range(L)]
    sh_w3 = [jnp.stack([w_attn_out[l], w_conv_out[l], w_o[l]]).astype(BF) for l in range(L)]
    sh_w2 = [jnp.stack([wup_t[l], w_down[l]]).astype(BF) for l in range(L)]
    wint, w3, w2 = [None] * L, [None] * L, [None] * L
    wint0_g, cw_g = _remote_only(_Gather([sh_win[0], jnp.pad(conv_w, ((0, 0), (0, 5), (0, 0)))]), "gather_first")
    wint[0] = wint0_g.reshape(NP, D)
    cw = jnp.swapaxes(cw_g, 1, 2).reshape(L, 8, D)
    slopes = np.power(np.float32(2.0), -8.0 * np.arange(1, 17, dtype=np.float32) / 16).astype(np.float32)

    xf = x.reshape(T, D)
    saved = []
    cur = xf
    for l in range(L):
        sm = jnp.stack([sinks[l], jnp.asarray(slopes)])
        bg = b_gates[l].reshape(2, D)
        (gt, q, k, v, h), got = _inproj_fwd(cur, g_mix[l:l + 1], wint[l], l, tm_in, side=_Gather([sh_w2[l]]),
                                            relay_at=0.85)
        w2[l] = got[0].reshape(2, F, D)
        (att,), got = _attn_fwd(q, k, v, sm, l, S, side=_Gather([sh_w3[l]]), relay_at=0.5)
        w3[l] = got[0].reshape(3, D, D)
        (x1, mg, co, ya, yc, zb), got = _mixer_fwd(cur, gt, att, w3[l], bg, cw[l], conv_b[l:l + 1], l, S, tm,
                                                   side=_Gather([sh_win[l + 1]]) if l + 1 < L else None, relay_at=0.85)
        if l + 1 < L:
            wint[l + 1] = got[0].reshape(NP, D)
        head = (loss_target.reshape(T, D), g_final.reshape(1, D)) if l == L - 1 else None
        nxt, h2, a, *acc_loss = _mlp_fwd(x1, g_mlp[l:l + 1], w2[l], l, tm_in, head=head)
        saved.append(dict(x=cur, gt=gt, q=q, k=k, v=v, h=h, att=att, x1=x1, mg=mg, co=co, ya=ya, yc=yc, zb=zb, h2=h2, a=a,
                          sm=sm, bg=bg))
        cur = nxt
    dcur, acc_loss = cur, acc_loss[0]

    masters = {"w_in": (w_in, m_w_in, v_w_in), "w_attn_out": (w_attn_out, m_w_attn_out, v_w_attn_out),
               "w_conv_out": (w_conv_out, m_w_conv_out, v_w_conv_out), "w_o": (w_o, m_w_o, v_w_o),
               "w_up": (w_up, m_w_up, v_w_up), "w_down": (w_down, m_w_down, v_w_down)}
    big = {name: None for name in masters}

    def adam(name, land, l):
        big[name] = _adam_sum(land, *masters[name], l, big[name], f"adam_{name}{l}", transposed=name in ("w_in", "w_up"))

    acc_in, acc_mix, acc_mlp, dsink = [None] * L, [None] * L, [None] * L, [None] * L
    d_win_above = None
    for l in reversed(range(L)):
        sv = saved[l]
        side = _Exchange([blocks(d_win_above)]) if l + 1 < L else None
        (da, dx1, dx2b, acc_mlp[l]), got = _mlp_bwd(dcur, sv["x1"], sv["a"], g_mlp[l:l + 1], w2[l], l, tm_in, side=side)
        if l + 1 < L:
            adam("w_in", got[0], l + 1)
        d_wdn = _wgrad(sv["a"], dx2b, F, 0, None, f"wgrad_down{l}", relu2=True)
        d_wup = _wgrad(da, sv["h2"], F, 0, None, f"wgrad_up{l}")
        (dgt, datt, dya, dyc, dx1b, acc_mix[l]), got = _mixer_bwd(
            dx1, sv["gt"], sv["ya"], sv["yc"], sv["zb"], w3[l], sv["bg"], cw[l], l, S, tm,
            side=_Exchange([blocks(d_wdn)]))
        adam("w_down", got[0], l)
        d_wo, d_wao, d_wco = _wgrad_square(
            [(sv["mg"], dx1b, False), (sv["att"], dya, True), (sv["co"], dyc, False)], f"wgrad_mixer{l}")
        d_win = _wgrad(dgt, sv["h"], NP, NQ + 2 * NKV, None, f"wgrad_gates{l}")
        (dq, dk, dv, dsink[l]), got = _attn_bwd(
            sv["q"], sv["k"], sv["v"], sv["att"], datt, sv["sm"], l, S,
            side=_Exchange([blocks(d_wup), blocks(d_wo), blocks(d_wao), blocks(d_wco)]))
        for name, land in zip(["w_up", "w_o", "w_attn_out", "w_conv_out"], got):
            adam(name, land, l)
        d_win = _wgrad(dq, sv["h"], NP, 0, d_win, f"wgrad_q{l}", a_is_transposed=True)
        d_win = _wgrad(dk, sv["h"], NP, NQ, d_win, f"wgrad_k{l}", a_is_transposed=True)
        d_win = _wgrad(dv, sv["h"], NP, NQ + NKV, d_win, f"wgrad_v{l}", a_is_transposed=True)
        side = _Exchange([blocks(d_win)]) if l == 0 else None
        (dcur, acc_in[l]), got_in = _inproj_bwd(dgt, dq, dk, dv, sv["x"], dx1, g_mix[l:l + 1], wint[l], l, tm_in, side=side)
        d_win_above = d_win

    dsinks = jnp.stack([dsink[l][:, :, :, 0].sum(axis=0).reshape(16) for l in range(L)])
    small = _pack_small(
        jnp.concatenate([acc_in[l][0:1] for l in range(L)]),
        jnp.stack([acc_mix[l][0:2].reshape(2 * D) for l in range(L)]),
        dsinks,
        jnp.concatenate([acc_mix[l][2:3] for l in range(L)]),
        jnp.concatenate([acc_mlp[l][0:1] for l in range(L)]),
        acc_loss[0],
        jnp.concatenate([acc_mix[l][3:6] for l in range(L)]),
        acc_loss[1:2])
    adam("w_in", got_in[0], 0)
    (small_land,) = _remote_only(_Exchange([], [small]), "exchange_small")

    row = lambda t: t.reshape(1, D)
    small_out, loss = _adam_small(small_land, me.reshape(1).astype(jnp.int32), {
        "g_mix": (g_mix, m_g_mix, v_g_mix), "b_gates": (b_gates, m_b_gates, v_b_gates), "sinks": (sinks, m_sinks, v_sinks),
        "conv_b": (conv_b, m_conv_b, v_conv_b), "g_mlp": (g_mlp, m_g_mlp, v_g_mlp),
        "g_final": (row(g_final), row(m_g_final), row(v_g_final)), "conv_w": (conv_w, m_conv_w, v_conv_w)})
    small_out["g_final"] = [t.reshape(D) for t in small_out["g_final"]]
    order = ["g_mix", "w_in", "b_gates", "sinks", "w_attn_out", "conv_w", "conv_b", "w_conv_out", "w_o", "g_mlp",
             "w_up", "w_down", "g_final"]
    out = [loss.reshape(()), dcur.reshape(nseq, S, D)]
    for kind in range(4):
        for name in order:
            out.append(big[name][kind] if name in big else small_out[name][kind])
    return tuple(out)
```

```python
import numpy as np
import jax
import jax.numpy as jnp
from jax import lax
from jax.experimental import pallas as pl
from jax.experimental.pallas import tpu as pltpu

D = 1024
NG = 5 * D
NQ = 1024
NKV = 256
NP = NQ + 2 * NKV + NG
F = 4096
HD = 64
GQ = 4
WIN = 128
L = 2
NDEV = 8
EPS = 1e-6
NEG = -1e30
SMALL_ROWS = 24
LR, B1, B2, AEPS, WD, STEP = 0.001, 0.9, 0.999, 1e-08, 0.01, 10

BF = jnp.bfloat16
F32 = jnp.float32
MESH = pl.DeviceIdType.MESH
VMEM_LIMIT = 60 * 1024 * 1024
ANY = pl.BlockSpec(memory_space=pl.ANY)

NN = ((1,), (0,))
NT = ((1,), (1,))
TN = ((0,), (0,))


def _dot(a, b, dims):
    return lax.dot_general(a, b, (dims, ((), ())), preferred_element_type=F32)


def _resident(shape, imap):
    return pl.BlockSpec(shape, imap, pipeline_mode=pl.Buffered(1))


def _position():
    return lax.axis_index("x"), lax.axis_index("y"), lax.axis_index("c")


class _Gather:
    def __init__(self, shards):
        n = len(shards)
        self.inputs = list(shards)
        self.out_shape = [jax.ShapeDtypeStruct((s.shape[0], NDEV) + s.shape[1:], s.dtype) for s in shards]
        self.scratch = [pltpu.SemaphoreType.DMA((n, 7)), pltpu.SemaphoreType.DMA((n, 7)), pltpu.SemaphoreType.DMA((n,))]

    def _plan(self, src, dst, sems):
        send_sems, recv_sems, local_sems = sems
        n = len(src)
        x, y, c = _position()
        me, sibling = (x, y, c), (x, y, 1 - c)
        chips = [(1 - x, y), (x, 1 - y), (1 - x, 1 - y)]

        def rows(a, p):
            return dst[a].at[:, 4 * p[0] + 2 * p[1] + p[2]]

        def copy(a, k, block, to, from_src=False):
            return pltpu.make_async_remote_copy(
                src_ref=src[a] if from_src else rows(a, block), dst_ref=rows(a, block),
                send_sem=send_sems.at[a, k], recv_sem=recv_sems.at[a, k], device_id=to, device_id_type=MESH)

        mine = [pltpu.make_async_copy(src[a], rows(a, me), local_sems.at[a]) for a in range(n)]
        first = []
        for a in range(n):
            first.append(copy(a, 0, me, sibling, True))
            first += [copy(a, 1 + t, me, (*chip, c), True) for t, chip in enumerate(chips)]
        return n, c, me, sibling, chips, copy, mine, first

    def start(self, src, dst, sems):
        *_, mine, first = self._plan(src, dst, sems)
        for cp in mine + first:
            cp.start()

    def relay(self, src, dst, sems):
        n, c, me, sibling, chips, copy, _, _ = self._plan(src, dst, sems)
        for t, chip in enumerate(chips):
            for a in range(n):
                copy(a, 1 + t, (*chip, c), me).wait_recv()
                copy(a, 4 + t, (*chip, c), sibling).start()

    def finish(self, src, dst, sems):
        n, c, me, sibling, chips, copy, mine, first = self._plan(src, dst, sems)
        for a in range(n):
            copy(a, 0, sibling, me).wait_recv()
            for t, chip in enumerate(chips):
                copy(a, 4 + t, (*chip, 1 - c), me).wait_recv()
        for cp in first + [copy(a, 4 + t, (*chip, c), sibling) for t, chip in enumerate(chips) for a in range(n)]:
            cp.wait_send()
        for cp in mine:
            cp.wait()


class _Exchange:
    def __init__(self, grads, everyone=()):
        self.inputs = list(grads) + list(everyone)
        self.n_blocked = len(grads)
        n = len(self.inputs)
        self.out_shape = [jax.ShapeDtypeStruct(g.shape, g.dtype) for g in grads]
        self.out_shape += [jax.ShapeDtypeStruct((NDEV,) + e.shape, e.dtype) for e in everyone]
        self.scratch = [pltpu.SemaphoreType.DMA((n, 7)), pltpu.SemaphoreType.DMA((n, 7)), pltpu.SemaphoreType.DMA((n,))]

    def _plan(self, src, land, sems):
        send_sems, recv_sems, local_sems = sems
        x, y, c = _position()
        me = 4 * x + 2 * y + c

        def parts(peer_idx):
            return [(s.at[peer_idx] if a < self.n_blocked else s, land[a].at[me]) for a, s in enumerate(src)]

        local = [pltpu.make_async_copy(s, d, local_sems.at[a]) for a, (s, d) in enumerate(parts(me))]
        sent = []
        for rel in range(1, NDEV):
            px = 1 - x if rel & 4 else x
            py = 1 - y if rel & 2 else y
            pc = 1 - c if rel & 1 else c
            for a, (s, d) in enumerate(parts(4 * px + 2 * py + pc)):
                sent.append(pltpu.make_async_remote_copy(
                    src_ref=s, dst_ref=d, send_sem=send_sems.at[a, rel - 1], recv_sem=recv_sems.at[a, rel - 1],
                    device_id=(px, py, pc), device_id_type=MESH))
        return local, sent

    def start(self, src, land, sems):
        local, sent = self._plan(src, land, sems)
        for cp in local + sent:
            cp.start()

    def relay(self, src, land, sems):
        pass

    def finish(self, src, land, sems):
        local, sent = self._plan(src, land, sems)
        for cp in sent:
            cp.wait_recv()
        for cp in sent:
            cp.wait_send()
        for cp in local:
            cp.wait()


def _call(body, *, name, grid, in_specs, out_specs, out_shape, args, scratch_shapes=(), aliases=None, side=None,
          relay_at=1.0):
    sem = ("arbitrary",) * len(grid)
    if side is None:
        outs = pl.pallas_call(
            body, name=name, grid=grid, in_specs=in_specs, out_specs=out_specs, out_shape=out_shape,
            scratch_shapes=list(scratch_shapes), input_output_aliases=aliases or {},
            compiler_params=pltpu.CompilerParams(dimension_semantics=sem, vmem_limit_bytes=VMEM_LIMIT))(*args)
        return outs, []
    ni, no, ns = len(in_specs), len(out_specs), len(scratch_shapes)
    si, so = len(side.inputs), len(side.out_shape)

    def hosted(*refs):
        ins, refs = refs[:ni], refs[ni:]
        sins, refs = refs[:si], refs[si:]
        outs, refs = refs[:no], refs[no:]
        souts, refs = refs[:so], refs[so:]
        scr, sscr = refs[:ns], refs[ns:]
        step = pl.program_id(0)
        for d in range(1, len(grid)):
            step = step * grid[d] + pl.program_id(d)
        last = int(np.prod(grid)) - 1

        @pl.when(step == 0)
        def _():
            side.start(sins, souts, sscr)

        body(*ins, *outs, *scr)

        @pl.when(step == min(int(relay_at * last), last))
        def _():
            side.relay(sins, souts, sscr)

        @pl.when(step == last)
        def _():
            side.finish(sins, souts, sscr)

    outs = pl.pallas_call(
        hosted, name=name, grid=grid, in_specs=list(in_specs) + [ANY] * si, out_specs=list(out_specs) + [ANY] * so,
        out_shape=list(out_shape) + side.out_shape, scratch_shapes=list(scratch_shapes) + side.scratch,
        input_output_aliases=aliases or {},
        compiler_params=pltpu.CompilerParams(dimension_semantics=sem, vmem_limit_bytes=VMEM_LIMIT, has_side_effects=True),
    )(*args, *side.inputs)
    return outs[:no], outs[no:]


def _remote_only(side, name):
    n = len(side.inputs)

    def body(*refs):
        src, dst, sems = refs[:n], refs[n:n + len(side.out_shape)], refs[n + len(side.out_shape):]
        side.start(src, dst, sems)
        side.relay(src, dst, sems)
        side.finish(src, dst, sems)

    return pl.pallas_call(
        body, name=name, in_specs=[ANY] * n, out_specs=[ANY] * len(side.out_shape), out_shape=side.out_shape,
        scratch_shapes=side.scratch, compiler_params=pltpu.CompilerParams(has_side_effects=True))(*side.inputs)


def _rms(x, g):
    r = lax.rsqrt(jnp.mean(x * x, axis=-1, keepdims=True) + EPS)
    return x * r * g


def _rms_bwd(dy, x, g):
    r = lax.rsqrt(jnp.mean(x * x, axis=-1, keepdims=True) + EPS)
    xh = x * r
    dxh = dy * g
    dx = r * (dxh - xh * jnp.mean(dxh * xh, axis=-1, keepdims=True))
    return dx, jnp.sum(dy * xh, axis=0, keepdims=True)


def _zero_at_first_step(acc_ref):
    first = pl.program_id(0) == 0

    @pl.when(first)
    def _():
        acc_ref[...] = jnp.zeros_like(acc_ref)


ROW1 = lambda: _resident((1, D), lambda i: (0, 0))
ACC = lambda: pl.BlockSpec((8, D), lambda i: (0, 0))


def _inproj_fwd(x, g, wint, l, tm, side=None, relay_at=1.0):
    T = x.shape[0]

    def body(x_ref, g_ref, w_ref, gt_ref, q_ref, k_ref, v_ref, h_ref):
        h = _rms(x_ref[...], g_ref[...]).astype(BF)
        h_ref[...] = h
        q_ref[...] = _dot(w_ref[0:NQ, :], h, NT).astype(BF)
        k_ref[...] = _dot(w_ref[NQ:NQ + NKV, :], h, NT).astype(BF)
        v_ref[...] = _dot(w_ref[NQ + NKV:NQ + 2 * NKV, :], h, NT).astype(BF)
        for s in range(5):
            lo = NQ + 2 * NKV + s * D
            gt_ref[:, s * D:(s + 1) * D] = _dot(h, w_ref[lo:lo + D, :], NT).astype(BF)

    tok = lambda w: pl.BlockSpec((tm, w), lambda i: (i, 0))
    feat = lambda w: pl.BlockSpec((w, tm), lambda i: (0, i))
    return _call(
        body, name=f"inproj_fwd{l}", grid=(T // tm,),
        in_specs=[tok(D), ROW1(), _resident((NP, D), lambda i: (0, 0))],
        out_specs=[tok(NG), feat(NQ), feat(NKV), feat(NKV), tok(D)],
        out_shape=[jax.ShapeDtypeStruct((T, NG), BF)] + [jax.ShapeDtypeStruct((w, T), BF) for w in (NQ, NKV, NKV)]
        + [jax.ShapeDtypeStruct((T, D), BF)],
        args=(x, g, wint), side=side, relay_at=relay_at)


def _band_geometry():
    j = lax.broadcasted_iota(jnp.int32, (2 * WIN, WIN), 0)
    r = lax.broadcasted_iota(jnp.int32, (2 * WIN, WIN), 1)
    dist = WIN + r - j
    dist0 = r - j
    return (dist.astype(F32), (dist >= 0) & (dist < WIN)), (dist0.astype(F32), dist0 >= 0)


def _pair_biases(sm_ref, pj):
    return [[jnp.where(ok, -sm_ref[1, pj * 2 * GQ + h] * dist, NEG) for h in range(2 * GQ)]
            for dist, ok in _band_geometry()]


def _reduce_rows(x, pair, whole):
    while x.shape[0] > 8:
        half = x.shape[0] // 2
        x = pair(x[:half], x[half:])
    return whole(x, axis=0, keepdims=True)


def _heads_on_lanes(ref, kvh, r0):
    return jnp.concatenate([ref[(kvh * GQ + g) * HD:(kvh * GQ + g + 1) * HD, pl.ds(r0, WIN)] for g in range(GQ)], axis=1)


def _band_probs(sm_ref, head0, q_ref, k_ref, r0, p0, kvh, biases):
    qt = _heads_on_lanes(q_ref, kvh, r0) * jnp.asarray(HD ** -0.5, BF)
    kt = k_ref[kvh * HD:(kvh + 1) * HD, pl.ds(p0, 2 * WIN)]
    st = _dot(kt, qt, TN)
    heads = []
    for g in range(GQ):
        sink = sm_ref[0, head0 + g]
        s = st[:, g * WIN:(g + 1) * WIN] + biases[kvh * GQ + g]
        m = jnp.maximum(_reduce_rows(s, jnp.maximum, jnp.max), sink)
        p = jnp.exp(s - m)
        ps = jnp.exp(sink - m)
        heads.append((p, ps, 1.0 / (_reduce_rows(p, jnp.add, jnp.sum) + ps)))
    return qt, kt, heads


def _attn_fwd(q, k, v, sm, l, S, side=None, relay_at=1.0):
    T = q.shape[1]
    nblk = S // WIN
    unroll = next(u for u in (5, 3, 1) if (nblk - 1) % u == 0)

    def body(sm_ref, q_ref, k_ref, v_ref, o_ref):
        pj = pl.program_id(1)
        biases = _pair_biases(sm_ref, pj)

        def block(i, first):
            r0 = 0 if first else pl.multiple_of(i * WIN, WIN)
            p0 = 0 if first else pl.multiple_of(i * WIN - WIN, WIN)
            for kvh in range(2):
                head0 = (2 * pj + kvh) * GQ
                _, _, heads = _band_probs(sm_ref, head0, q_ref, k_ref, r0, p0, kvh, biases[first])
                pt = jnp.concatenate([p.astype(BF) for p, _, _ in heads], axis=1)
                ot = _dot(v_ref[kvh * HD:(kvh + 1) * HD, pl.ds(p0, 2 * WIN)], pt, NN)
                for g in range(GQ):
                    rows = slice((kvh * GQ + g) * HD, (kvh * GQ + g + 1) * HD)
                    o_ref[rows, pl.ds(r0, WIN)] = (ot[:, g * WIN:(g + 1) * WIN] * heads[g][2]).astype(BF)

        block(0, True)

        def rest(t, c):
            for u in range(unroll):
                block(1 + unroll * t + u, False)
            return c

        lax.fori_loop(0, (nblk - 1) // unroll, rest, 0)

    wide = lambda: pl.BlockSpec((2 * GQ * HD, S), lambda s, p: (p, s))
    narrow = lambda: pl.BlockSpec((2 * HD, S), lambda s, p: (p, s))
    return _call(
        body, name=f"attn_fwd{l}", grid=(T // S, 2),
        in_specs=[pl.BlockSpec(memory_space=pltpu.SMEM), wide(), narrow(), narrow()],
        out_specs=[wide()], out_shape=[jax.ShapeDtypeStruct((D, T), BF)],
        args=(sm, q, k, v), side=side, relay_at=relay_at)


def _shift_rows(y, k, edge_rows, down):
    n = y.shape[0]
    rid = lax.broadcasted_iota(jnp.int32, y.shape, 0)
    out = pltpu.roll(y, k if down else n - k, 0)
    for t, row in enumerate(edge_rows):
        out = jnp.where(rid == (t if down else n - k + t), row, out)
    return out


def _mixer_fwd(x, gt, att, w3, bg, cw, cbias, l, S, tm, side=None, relay_at=1.0):
    T = x.shape[0]

    def body(x_ref, cb_ref, cc_ref, cu_ref, ga_ref, gc_ref, cch_ref, cuh_ref, att_ref, wao_ref, wco_ref, wo_ref,
             bg_ref, cw_ref, cbias_ref, x1_ref, mg_ref, co_ref, ya_ref, yc_ref, zb_ref):
        first = (pl.program_id(0) * tm) % S == 0
        y = cc_ref[...].astype(F32) * cu_ref[...].astype(F32)
        hy1 = cch_ref[15:16, :].astype(F32) * cuh_ref[15:16, :].astype(F32)
        hy2 = cch_ref[14:15, :].astype(F32) * cuh_ref[14:15, :].astype(F32)
        hy1, hy2 = jnp.where(first, 0.0, hy1), jnp.where(first, 0.0, hy2)
        z = (cw_ref[0:1, :] * _shift_rows(y, 2, [hy2, hy1], True) + cw_ref[1:2, :] * _shift_rows(y, 1, [hy1], True)
             + cw_ref[2:3, :] * y)
        zb = z + cbias_ref[...]
        zb_ref[...] = zb.astype(BF)
        co = (cb_ref[...].astype(F32) * zb).astype(BF)
        co_ref[...] = co
        yc = _dot(co, wco_ref[...], NN)
        ya = _dot(att_ref[...], wao_ref[...], TN)
        ya_ref[...] = ya.astype(BF)
        yc_ref[...] = yc.astype(BF)
        sa = jax.nn.sigmoid(ga_ref[...].astype(F32) + bg_ref[0:1, :])
        sc = jax.nn.sigmoid(gc_ref[...].astype(F32) + bg_ref[1:2, :])
        mg = (sa * ya + sc * yc).astype(BF)
        mg_ref[...] = mg
        x1_ref[...] = x_ref[...] + _dot(mg, wo_ref[...], NN)

    tok = lambda: pl.BlockSpec((tm, D), lambda i: (i, 0))
    seg = lambda s: pl.BlockSpec((tm, D), lambda i: (i, s))
    halo = lambda s: pl.BlockSpec((16, D), lambda i: (jnp.maximum(i * (tm // 16) - 1, 0), s))
    wsp = lambda k: _resident((None, D, D), lambda i: (k, 0, 0))
    row = lambda n: _resident((n, D), lambda i: (0, 0))
    return _call(
        body, name=f"mixer_fwd{l}", grid=(T // tm,),
        in_specs=[tok(), seg(0), seg(1), seg(2), seg(3), seg(4), halo(1), halo(2),
                  pl.BlockSpec((D, tm), lambda i: (0, i)), wsp(0), wsp(1), wsp(2), row(2), row(8), row(1)],
        out_specs=[tok()] * 6,
        out_shape=[jax.ShapeDtypeStruct((T, D), dt) for dt in (F32, BF, BF, BF, BF, BF)],
        args=(x, gt, gt, gt, gt, gt, gt, gt, att, w3, w3, w3, bg, cw, cbias), side=side, relay_at=relay_at)


def _mlp_fwd(x1, g, w2, l, tm, head=None):
    T = x1.shape[0]
    nt = T // tm
    FC = 1024

    def body(x_ref, g_ref, wup_ref, wdn_ref, *rest):
        out_ref, h_ref, a_ref = rest[-4:-1] if head else rest
        x = x_ref[...]
        h = _rms(x, g_ref[...]).astype(BF)
        h_ref[...] = h
        acc = x
        for c in range(F // FC):
            a = _dot(h, wup_ref[c * FC:(c + 1) * FC, :], NT)
            a_ref[:, c * FC:(c + 1) * FC] = a.astype(BF)
            u = jnp.maximum(a, 0.0)
            acc = acc + _dot((u * u).astype(BF), wdn_ref[c * FC:(c + 1) * FC, :], NN)
        if not head:
            out_ref[...] = acc
            return
        t_ref, gf_ref, acc_ref = rest[0], rest[1], rest[-1]
        _zero_at_first_step(acc_ref)
        gf = gf_ref[...]
        err = _rms(acc, gf) - t_ref[...]
        out_ref[...], dg = _rms_bwd(err * (1.0 / D), acc, gf)
        acc_ref[0:1, :] += dg
        acc_ref[1:2, :] += jnp.sum(err * err, axis=0, keepdims=True)

        @pl.when(pl.program_id(0) == nt - 1)
        def _():
            acc_ref[1:2, :] = jnp.zeros((1, D), F32) + (0.5 / D) * jnp.sum(acc_ref[1:2, :])

    tok = lambda w: pl.BlockSpec((tm, w), lambda i: (i, 0))
    wsp = lambda k: _resident((None, F, D), lambda i: (k, 0, 0))
    outs, _ = _call(
        body, name=f"mlp_fwd{l}", grid=(nt,),
        in_specs=[tok(D), ROW1(), wsp(0), wsp(1)] + ([tok(D), ROW1()] if head else []),
        out_specs=[tok(D), tok(D), tok(F)] + ([ACC()] if head else []),
        out_shape=[jax.ShapeDtypeStruct((T, D), F32), jax.ShapeDtypeStruct((T, D), BF), jax.ShapeDtypeStruct((T, F), BF)]
        + ([jax.ShapeDtypeStruct((8, D), F32)] if head else []),
        args=(x1, g, w2, w2) + (tuple(head) if head else ()))
    return outs


def _mlp_bwd(dx2, x1, a, g, w2, l, tm, side=None):
    T = dx2.shape[0]
    FC = 1024

    def body(d_ref, x_ref, a_ref, g_ref, wup_ref, wdn_ref, da_ref, dx1_ref, db_ref, acc_ref):
        _zero_at_first_step(acc_ref)
        d = d_ref[...]
        db = d.astype(BF)
        db_ref[...] = db
        dh = jnp.zeros((tm, D), F32)
        for c in range(F // FC):
            du = _dot(db, wdn_ref[c * FC:(c + 1) * FC, :], NT)
            da = (du * (2.0 * jnp.maximum(a_ref[:, c * FC:(c + 1) * FC].astype(F32), 0.0))).astype(BF)
            da_ref[:, c * FC:(c + 1) * FC] = da
            dh = dh + _dot(da, wup_ref[c * FC:(c + 1) * FC, :], NN)
        dx, dg = _rms_bwd(dh, x_ref[...], g_ref[...])
        dx1_ref[...] = d + dx
        acc_ref[0:1, :] += dg

    tok = lambda w: pl.BlockSpec((tm, w), lambda i: (i, 0))
    wsp = lambda k: _resident((None, F, D), lambda i: (k, 0, 0))
    return _call(
        body, name=f"mlp_bwd{l}", grid=(T // tm,),
        in_specs=[tok(D), tok(D), tok(F), ROW1(), wsp(0), wsp(1)],
        out_specs=[tok(F), tok(D), tok(D), ACC()],
        out_shape=[jax.ShapeDtypeStruct((T, F), BF), jax.ShapeDtypeStruct((T, D), F32),
                   jax.ShapeDtypeStruct((T, D), BF), jax.ShapeDtypeStruct((8, D), F32)],
        args=(dx2, x1, a, g, w2, w2), side=side)


def _mixer_bwd(dx1, gt, ya, yc, zb, w3, bg, cw, l, S, tm, side=None):
    T = dx1.shape[0]
    nt = T // tm

    def body(d_ref, cb_ref, cc_ref, cu_ref, ga_ref, gc_ref, ya_ref, yc_ref, zb_ref, wao_ref, wco_ref, wo_ref,
             bg_ref, cw_ref, dg_ref, datt_ref, dya_ref, dyc_ref, db_ref, acc_ref, carry_ref):
        ti = nt - 1 - pl.program_id(0)
        _zero_at_first_step(acc_ref)

        @pl.when(((ti + 1) * tm) % S == 0)
        def _():
            carry_ref[...] = jnp.zeros_like(carry_ref)

        db = d_ref[...].astype(BF)
        db_ref[...] = db
        dm = _dot(db, wo_ref[...], NT)
        sa = jax.nn.sigmoid(ga_ref[...].astype(F32) + bg_ref[0:1, :])
        sc = jax.nn.sigmoid(gc_ref[...].astype(F32) + bg_ref[1:2, :])
        dya32 = dm * sa
        dyc32 = dm * sc
        dya = dya32.astype(BF)
        dyc = dyc32.astype(BF)
        dya_ref[...] = dya
        dyc_ref[...] = dyc
        dga = dya32 * ya_ref[...].astype(F32) * (1.0 - sa)
        dgc = dyc32 * yc_ref[...].astype(F32) * (1.0 - sc)
        dg_ref[:, 3 * D:4 * D] = dga.astype(BF)
        dg_ref[:, 4 * D:5 * D] = dgc.astype(BF)
        acc_ref[0:1, :] += jnp.sum(dga, axis=0, keepdims=True)
        acc_ref[1:2, :] += jnp.sum(dgc, axis=0, keepdims=True)
        datt_ref[...] = _dot(wao_ref[...], dya, NT).astype(BF)
        dco = _dot(dyc, wco_ref[...], NT)

        cc = cc_ref[...].astype(F32)
        cu = cu_ref[...].astype(F32)
        y = cc * cu
        dg_ref[:, 0:D] = (dco * zb_ref[...].astype(F32)).astype(BF)
        dz = dco * cb_ref[...].astype(F32)
        u1 = _shift_rows(dz, 1, [carry_ref[0:1, :]], False)
        u2 = _shift_rows(dz, 2, [carry_ref[0:1, :], carry_ref[1:2, :]], False)
        acc_ref[2:3, :] += jnp.sum(dz, axis=0, keepdims=True)
        acc_ref[3:4, :] += jnp.sum(u2 * y, axis=0, keepdims=True)
        acc_ref[4:5, :] += jnp.sum(u1 * y, axis=0, keepdims=True)
        acc_ref[5:6, :] += jnp.sum(dz * y, axis=0, keepdims=True)
        dy = cw_ref[2:3, :] * dz + cw_ref[1:2, :] * u1 + cw_ref[0:1, :] * u2
        dg_ref[:, D:2 * D] = (dy * cu).astype(BF)
        dg_ref[:, 2 * D:3 * D] = (dy * cc).astype(BF)
        carry_ref[...] = dz[0:8, :]

    tok = lambda w=D: pl.BlockSpec((tm, w), lambda i: (nt - 1 - i, 0))
    seg = lambda s: pl.BlockSpec((tm, D), lambda i: (nt - 1 - i, s))
    wsp = lambda k: _resident((None, D, D), lambda i: (k, 0, 0))
    row = lambda n: _resident((n, D), lambda i: (0, 0))
    return _call(
        body, name=f"mixer_bwd{l}", grid=(nt,),
        in_specs=[tok(), seg(0), seg(1), seg(2), seg(3), seg(4), tok(), tok(), tok(), wsp(0), wsp(1), wsp(2),
                  row(2), row(8)],
        out_specs=[tok(NG), pl.BlockSpec((D, tm), lambda i: (0, nt - 1 - i)), tok(), tok(), tok(), ACC()],
        out_shape=[jax.ShapeDtypeStruct((T, NG), BF), jax.ShapeDtypeStruct((D, T), BF)]
        + [jax.ShapeDtypeStruct((T, D), BF)] * 3 + [jax.ShapeDtypeStruct((8, D), F32)],
        scratch_shapes=[pltpu.VMEM((8, D), F32)],
        args=(dx1, gt, gt, gt, gt, gt, ya, yc, zb, w3, w3, w3, bg, cw), side=side)


def _attn_bwd(q, k, v, att, datt, sm, l, S, side=None):
    T = q.shape[1]
    nblk = S // WIN
    unroll = next(u for u in (5, 3, 1) if (nblk - 1) % u == 0)
    scale = HD ** -0.5

    def body(sm_ref, q_ref, k_ref, v_ref, o_ref, do_ref, dq_ref, dk_ref, dv_ref, ds_ref, dka_ref, dva_ref):
        pj = pl.program_id(1)
        biases = _pair_biases(sm_ref, pj)
        dka_ref[...] = jnp.zeros_like(dka_ref)
        dva_ref[...] = jnp.zeros_like(dva_ref)

        def block(i, first, dsinks):
            r0 = 0 if first else pl.multiple_of(i * WIN, WIN)
            p0 = 0 if first else pl.multiple_of(i * WIN - WIN, WIN)
            out = []
            for kvh in range(2):
                head0 = (2 * pj + kvh) * GQ
                rows = slice(kvh * HD, (kvh + 1) * HD)
                dob = _heads_on_lanes(do_ref, kvh, r0)
                dpt = _dot(v_ref[rows, pl.ds(p0, 2 * WIN)], dob, TN)
                qt, kt, heads = _band_probs(sm_ref, head0, q_ref, k_ref, r0, p0, kvh, biases[first])
                inv = jnp.concatenate([h[2] for h in heads], axis=1)
                do32 = dob.astype(F32)
                delta = jnp.sum(do32 * _heads_on_lanes(o_ref, kvh, r0).astype(F32), axis=0, keepdims=True) * inv
                dosb = (do32 * inv).astype(BF)
                dst = jnp.concatenate(
                    [(p * (dpt[:, g * WIN:(g + 1) * WIN] * h_inv - delta[:, g * WIN:(g + 1) * WIN])).astype(BF)
                     for g, (p, _, h_inv) in enumerate(heads)], axis=1)
                pt = jnp.concatenate([p.astype(BF) for p, _, _ in heads], axis=1)
                dqt = _dot(kt, dst, NN) * scale
                for g in range(GQ):
                    hr = slice((kvh * GQ + g) * HD, (kvh * GQ + g + 1) * HD)
                    dq_ref[hr, pl.ds(r0, WIN)] = dqt[:, g * WIN:(g + 1) * WIN].astype(BF)
                dka_ref[rows, pl.ds(p0, 2 * WIN)] += _dot(qt, dst, NT)
                dva_ref[rows, pl.ds(p0, 2 * WIN)] += _dot(dosb, pt, NT)
                ps = jnp.concatenate([h[1] for h in heads], axis=1)
                out.append(dsinks[kvh] - ps * delta)
            return tuple(out)

        zero = jnp.zeros((1, GQ * WIN), F32)
        def rest(t, c):
            for u in range(unroll):
                c = block(1 + unroll * t + u, False, c)
            return c

        dsinks = lax.fori_loop(0, (nblk - 1) // unroll, rest, block(0, True, (zero, zero)))
        for kvh in range(2):
            for g in range(GQ):
                tot = jnp.sum(dsinks[kvh][:, g * WIN:(g + 1) * WIN])
                ds_ref[kvh * GQ + g:kvh * GQ + g + 1, :] = jnp.zeros((1, 128), F32) + tot
        dk_ref[...] = dka_ref[...].astype(BF)
        dv_ref[...] = dva_ref[...].astype(BF)

    wide = lambda: pl.BlockSpec((2 * GQ * HD, S), lambda s, p: (p, s))
    narrow = lambda: pl.BlockSpec((2 * HD, S), lambda s, p: (p, s))
    return _call(
        body, name=f"attn_bwd{l}", grid=(T // S, 2),
        in_specs=[pl.BlockSpec(memory_space=pltpu.SMEM), wide(), narrow(), narrow(), wide(), wide()],
        out_specs=[wide(), narrow(), narrow(), pl.BlockSpec((None, None, 8, 128), lambda s, p: (s, p, 0, 0))],
        out_shape=[jax.ShapeDtypeStruct((NQ, T), BF), jax.ShapeDtypeStruct((NKV, T), BF),
                   jax.ShapeDtypeStruct((NKV, T), BF), jax.ShapeDtypeStruct((T // S, 2, 8, 128), F32)],
        scratch_shapes=[pltpu.VMEM((2 * HD, S), F32), pltpu.VMEM((2 * HD, S), F32)],
        args=(sm, q, k, v, att, datt), side=side)


def _inproj_bwd(dgt, dq, dk, dv, x, dres, g, wint, l, tm, side=None):
    T = x.shape[0]

    def body(dg_ref, dq_ref, dk_ref, dv_ref, x_ref, dr_ref, g_ref, w_ref, dx_ref, acc_ref):
        _zero_at_first_step(acc_ref)
        dh = _dot(dq_ref[...], w_ref[0:NQ, :], TN)
        dh = dh + _dot(dk_ref[...], w_ref[NQ:NQ + NKV, :], TN)
        dh = dh + _dot(dv_ref[...], w_ref[NQ + NKV:NQ + 2 * NKV, :], TN)
        dh = dh + _dot(dg_ref[...], w_ref[NQ + 2 * NKV:NP, :], NN)
        dx, dg = _rms_bwd(dh, x_ref[...], g_ref[...])
        dx_ref[...] = dr_ref[...] + dx
        acc_ref[0:1, :] += dg

    tok = lambda w: pl.BlockSpec((tm, w), lambda i: (i, 0))
    feat = lambda w: pl.BlockSpec((w, tm), lambda i: (0, i))
    return _call(
        body, name=f"inproj_bwd{l}", grid=(T // tm,),
        in_specs=[tok(NG), feat(NQ), feat(NKV), feat(NKV), tok(D), tok(D), ROW1(), _resident((NP, D), lambda i: (0, 0))],
        out_specs=[tok(D), ACC()],
        out_shape=[jax.ShapeDtypeStruct((T, D), F32), jax.ShapeDtypeStruct((8, D), F32)],
        args=(dgt, dq, dk, dv, x, dres, g, wint), side=side)


def _wgrad(a, b, rows, row0, into, name, relu2=False, a_is_transposed=False):
    M, T = a.shape if a_is_transposed else a.shape[::-1]
    tmm = next(t for t in (1024, 512, 256) if M % t == 0 and row0 % t == 0)
    tk = min(4096 if M > 4096 else 2048, T)
    nk = T // tk
    blk0 = row0 // tmm

    def body(*refs):
        a_ref, b_ref = refs[0], refs[1]
        o_ref, acc_ref = refs[-2], refs[-1]
        kk = pl.program_id(1)

        @pl.when(kk == 0)
        def _():
            acc_ref[...] = jnp.zeros_like(acc_ref)

        av = a_ref[...]
        if relu2:
            t = jnp.maximum(av.astype(F32), 0.0)
            av = (t * t).astype(BF)
        acc_ref[...] += _dot(av, b_ref[...], NN if a_is_transposed else TN)

        @pl.when(kk == nk - 1)
        def _():
            o_ref[...] = acc_ref[...].astype(BF)

    a_spec = pl.BlockSpec((tmm, tk), lambda j, kk: (j, kk)) if a_is_transposed else pl.BlockSpec((tk, tmm), lambda j, kk: (kk, j))
    in_specs = [a_spec, pl.BlockSpec((tk, D), lambda j, kk: (kk, 0))]
    args = [a, b]
    if into is not None:
        in_specs.append(ANY)
        args.append(into)
    (out,), _ = _call(
        body, name=name, grid=(M // tmm, nk), in_specs=in_specs,
        out_specs=[pl.BlockSpec((tmm, D), lambda j, kk: (blk0 + j, 0))],
        out_shape=[jax.ShapeDtypeStruct((rows, D), BF)], scratch_shapes=[pltpu.VMEM((tmm, D), F32)],
        aliases={2: 0} if into is not None else None, args=args)
    return out


def _wgrad_square(jobs, name):
    n = len(jobs)
    T = jobs[0][1].shape[0]
    tk = min(1024, T)
    nk = T // tk

    def body(*refs):
        ins, outs, acc_ref = refs[:2 * n], refs[2 * n:3 * n], refs[3 * n]
        j, kk = pl.program_id(0), pl.program_id(1)

        @pl.when(kk == 0)
        def _():
            acc_ref[...] = jnp.zeros_like(acc_ref)

        for p, (_, _, a_t) in enumerate(jobs):
            @pl.when(j == p)
            def _(p=p, a_t=a_t):
                acc_ref[...] += _dot(ins[2 * p][...], ins[2 * p + 1][...], NN if a_t else TN)

                @pl.when(kk == nk - 1)
                def _():
                    outs[p][...] = acc_ref[...].astype(BF)

    def step(p):
        return lambda j, kk: jnp.where(j == p, kk, jnp.where(j > p, nk - 1, 0))

    in_specs, args = [], []
    for p, (a, b, a_t) in enumerate(jobs):
        at = step(p)
        in_specs.append(pl.BlockSpec((D, tk), lambda j, kk, at=at: (0, at(j, kk))) if a_t
                        else pl.BlockSpec((tk, D), lambda j, kk, at=at: (at(j, kk), 0)))
        in_specs.append(pl.BlockSpec((tk, D), lambda j, kk, at=at: (at(j, kk), 0)))
        args += [a, b]
    outs, _ = _call(
        body, name=name, grid=(n, nk), in_specs=in_specs,
        out_specs=[pl.BlockSpec((D, D), lambda j, kk: (0, 0))] * n,
        out_shape=[jax.ShapeDtypeStruct((D, D), BF)] * n, scratch_shapes=[pltpu.VMEM((D, D), F32)], args=args)
    return outs


def _adamw(w, g, m, v):
    m = B1 * m + (1.0 - B1) * g
    v = B2 * v + (1.0 - B2) * (g * g)
    m_hat = m / (1.0 - B1 ** STEP)
    v_hat = v / (1.0 - B2 ** STEP)
    return -LR * (m_hat / (jnp.sqrt(v_hat) + AEPS) + WD * w), m, v


def _adam_sum(land, w, m, v, l, into, name, transposed=False):
    _, r, _ = land.shape
    tr = 208 if r % 208 == 0 else (256 if r % 256 == 0 else r)
    tc = 256

    def body(land_ref, w_ref, m_ref, v_ref, *rest):
        g_ref, d_ref, nm_ref, nv_ref = rest[-4:]
        g = land_ref[0].astype(F32)
        for s in range(1, NDEV):
            g = g + land_ref[s].astype(F32)
        if transposed:
            g = g.T
        g_ref[...] = g
        d_ref[...], nm_ref[...], nv_ref[...] = _adamw(w_ref[...], g, m_ref[...], v_ref[...])

    if transposed:
        blk = lambda: pl.BlockSpec((None, tc, r), lambda j: (l, j, 0))
        land_spec, grid = pl.BlockSpec((NDEV, r, tc), lambda j: (0, 0, j)), (D // tc,)
    else:
        blk = lambda: pl.BlockSpec((None, tr, D), lambda j: (l, j, 0))
        land_spec, grid = pl.BlockSpec((NDEV, tr, D), lambda j: (0, j, 0)), (r // tr,)
    in_specs = [land_spec, blk(), blk(), blk()]
    args = [land, w, m, v]
    aliases = None
    if into is not None:
        in_specs += [ANY] * 4
        args += list(into)
        aliases = {4 + t: t for t in range(4)}
    outs, _ = _call(body, name=name, grid=grid, in_specs=in_specs, out_specs=[blk()] * 4,
                    out_shape=[jax.ShapeDtypeStruct(w.shape, F32)] * 4, aliases=aliases, args=args)
    return outs


SMALL_NAMES = ("g_mix", "b_gates", "sinks", "conv_b", "g_mlp", "g_final", "conv_w")


def _adam_small(land, me, masters):
    n = len(SMALL_NAMES)
    lanes = D // NDEV

    def body(land_ref, me_ref, *refs):
        ins, outs, loss_ref, gs_ref = refs[:3 * n], refs[3 * n:7 * n], refs[7 * n], refs[7 * n + 1]
        g = land_ref[0]
        for s in range(1, NDEV):
            g = g + land_ref[s]
        gs_ref[...] = g

        def update(k, g_piece, idx):
            w_ref, m_ref, v_ref = ins[3 * k:3 * k + 3]
            outs[4 * k][idx] = g_piece
            outs[4 * k + 1][idx], outs[4 * k + 2][idx], outs[4 * k + 3][idx] = _adamw(w_ref[idx], g_piece, m_ref[idx], v_ref[idx])

        whole = (slice(None), slice(None))
        update(0, gs_ref[0:2, :], whole)
        for l in range(L):
            for h in range(2):
                update(1, gs_ref[2 + 2 * l + h:3 + 2 * l + h, :], (slice(l, l + 1), slice(h * D, (h + 1) * D)))
            update(2, gs_ref[6:7, 16 * l:16 * (l + 1)], (slice(l, l + 1), slice(None)))
        update(3, gs_ref[7:9, :], whole)
        update(4, gs_ref[9:11, :], whole)
        update(5, gs_ref[11:12, :], whole)
        mine = pl.ds(pl.multiple_of(me_ref[0] * lanes, lanes), lanes)
        for l in range(L):
            for k in range(3):
                update(6, gs_ref[12 + 3 * l + k:13 + 3 * l + k, mine], (l, slice(k, k + 1), slice(None)))
        loss_ref[...] = gs_ref[18:19, 0:1]

    flat = [t for name in SMALL_NAMES for t in masters[name]]
    vmem = pl.BlockSpec(memory_space=pltpu.VMEM)
    outs = pl.pallas_call(
        body, name="adam_small",
        in_specs=[vmem, pl.BlockSpec(memory_space=pltpu.SMEM)] + [vmem] * len(flat),
        out_shape=[jax.ShapeDtypeStruct(masters[name][0].shape, F32) for name in SMALL_NAMES for _ in range(4)]
        + [jax.ShapeDtypeStruct((1, 1), F32)],
        scratch_shapes=[pltpu.VMEM((SMALL_ROWS, D), F32)],
        compiler_params=pltpu.CompilerParams(vmem_limit_bytes=VMEM_LIMIT))(land, me, *flat)
    return {name: outs[4 * k:4 * k + 4] for k, name in enumerate(SMALL_NAMES)}, outs[-1]


def _pack_small(g_mix, b_gates, sinks, conv_b, g_mlp, g_final, conv_w_rows, extra):
    sink_row = jnp.zeros((1, D), F32).at[0, :2 * 16].set(sinks.reshape(-1))
    return jnp.concatenate([g_mix, b_gates.reshape(4, D), sink_row, conv_b, g_mlp, g_final.reshape(1, D),
                            conv_w_rows, extra, jnp.zeros((SMALL_ROWS - 19, D), F32)], axis=0)


def kernel(x, g_mix, w_in, b_gates, sinks, w_attn_out, conv_w, conv_b, w_conv_out, w_o, g_mlp, w_up, w_down, g_final, loss_target, m_g_mix, m_w_in, m_b_gates, m_sinks, m_w_attn_out, m_conv_w, m_conv_b, m_w_conv_out, m_w_o, m_g_mlp, m_w_up, m_w_down, m_g_final, v_g_mix, v_w_in, v_b_gates, v_sinks, v_w_attn_out, v_conv_w, v_conv_b, v_w_conv_out, v_w_o, v_g_mlp, v_w_up, v_w_down, v_g_final):
    nseq, S, _ = x.shape
    T = nseq * S
    tm_in = min(512, S)
    tm = min(512, S)
    xi, yi, ci = _position()
    me = 4 * xi + 2 * yi + ci
    tr = lambda t: jnp.swapaxes(t, 1, 2)
    blocks = lambda t: t.reshape(NDEV, t.shape[0] // NDEV, D)

    win_t, wup_t = tr(w_in), tr(w_up)
    sh_win = [win_t[l].astype(BF)[None] for l in range(L)]
    sh_w3 = [jnp.stack([w_attn_out[l], w_conv_out[l], w_o[l]]).astype(BF) for l in range(L)]
    sh_w2 = [jnp.stack([wup_t[l], w_down[l]]).astype(BF) for l in range(L)]
    wint, w3, w2 = [None] * L, [None] * L, [None] * L
    wint0_g, cw_g = _remote_only(_Gather([sh_win[0], jnp.pad(conv_w, ((0, 0), (0, 5), (0, 0)))]), "gather_first")
    wint[0] = wint0_g.reshape(NP, D)
    cw = jnp.swapaxes(cw_g, 1, 2).reshape(L, 8, D)
    slopes = np.power(np.float32(2.0), -8.0 * np.arange(1, 17, dtype=np.float32) / 16).astype(np.float32)

    xf = x.reshape(T, D)
    saved = []
    cur = xf
    for l in range(L):
        sm = jnp.stack([sinks[l], jnp.asarray(slopes)])
        bg = b_gates[l].reshape(2, D)
        (gt, q, k, v, h), got = _inproj_fwd(cur, g_mix[l:l + 1], wint[l], l, tm_in, side=_Gather([sh_w2[l]]),
                                            relay_at=0.85)
        w2[l] = got[0].reshape(2, F, D)
        (att,), got = _attn_fwd(q, k, v, sm, l, S, side=_Gather([sh_w3[l]]), relay_at=0.5)
        w3[l] = got[0].reshape(3, D, D)
        (x1, mg, co, ya, yc, zb), got = _mixer_fwd(cur, gt, att, w3[l], bg, cw[l], conv_b[l:l + 1], l, S, tm,
                                                   side=_Gather([sh_win[l + 1]]) if l + 1 < L else None, relay_at=0.85)
        if l + 1 < L:
            wint[l + 1] = got[0].reshape(NP, D)
        head = (loss_target.reshape(T, D), g_final.reshape(1, D)) if l == L - 1 else None
        nxt, h2, a, *acc_loss = _mlp_fwd(x1, g_mlp[l:l + 1], w2[l], l, tm_in, head=head)
        saved.append(dict(x=cur, gt=gt, q=q, k=k, v=v, h=h, att=att, x1=x1, mg=mg, co=co, ya=ya, yc=yc, zb=zb, h2=h2, a=a,
                          sm=sm, bg=bg))
        cur = nxt
    dcur, acc_loss = cur, acc_loss[0]

    masters = {"w_in": (win_t, tr(m_w_in), tr(v_w_in)), "w_attn_out": (w_attn_out, m_w_attn_out, v_w_attn_out),
               "w_conv_out": (w_conv_out, m_w_conv_out, v_w_conv_out), "w_o": (w_o, m_w_o, v_w_o),
               "w_up": (w_up, m_w_up, v_w_up), "w_down": (w_down, m_w_down, v_w_down)}
    big = {name: None for name in masters}

    def adam(name, land, l):
        big[name] = _adam_sum(land, *masters[name], l, big[name], f"adam_{name}{l}", transposed=name == "w_up")

    acc_in, acc_mix, acc_mlp, dsink = [None] * L, [None] * L, [None] * L, [None] * L
    d_win_above = None
    for l in reversed(range(L)):
        sv = saved[l]
        side = _Exchange([blocks(d_win_above)]) if l + 1 < L else None
        (da, dx1, dx2b, acc_mlp[l]), got = _mlp_bwd(dcur, sv["x1"], sv["a"], g_mlp[l:l + 1], w2[l], l, tm_in, side=side)
        if l + 1 < L:
            adam("w_in", got[0], l + 1)
        d_wdn = _wgrad(sv["a"], dx2b, F, 0, None, f"wgrad_down{l}", relu2=True)
        d_wup = _wgrad(da, sv["h2"], F, 0, None, f"wgrad_up{l}")
        (dgt, datt, dya, dyc, dx1b, acc_mix[l]), got = _mixer_bwd(
            dx1, sv["gt"], sv["ya"], sv["yc"], sv["zb"], w3[l], sv["bg"], cw[l], l, S, tm,
            side=_Exchange([blocks(d_wdn)]))
        adam("w_down", got[0], l)
        d_wo, d_wao, d_wco = _wgrad_square(
            [(sv["mg"], dx1b, False), (sv["att"], dya, True), (sv["co"], dyc, False)], f"wgrad_mixer{l}")
        d_win = _wgrad(dgt, sv["h"], NP, NQ + 2 * NKV, None, f"wgrad_gates{l}")
        (dq, dk, dv, dsink[l]), got = _attn_bwd(
            sv["q"], sv["k"], sv["v"], sv["att"], datt, sv["sm"], l, S,
            side=_Exchange([blocks(d_wup), blocks(d_wo), blocks(d_wao), blocks(d_wco)]))
        for name, land in zip(["w_up", "w_o", "w_attn_out", "w_conv_out"], got):
            adam(name, land, l)
        d_win = _wgrad(dq, sv["h"], NP, 0, d_win, f"wgrad_q{l}", a_is_transposed=True)
        d_win = _wgrad(dk, sv["h"], NP, NQ, d_win, f"wgrad_k{l}", a_is_transposed=True)
        d_win = _wgrad(dv, sv["h"], NP, NQ + NKV, d_win, f"wgrad_v{l}", a_is_transposed=True)
        side = _Exchange([blocks(d_win)]) if l == 0 else None
        (dcur, acc_in[l]), got_in = _inproj_bwd(dgt, dq, dk, dv, sv["x"], dx1, g_mix[l:l + 1], wint[l], l, tm_in, side=side)
        d_win_above = d_win

    dsinks = jnp.stack([dsink[l][:, :, :, 0].sum(axis=0).reshape(16) for l in range(L)])
    small = _pack_small(
        jnp.concatenate([acc_in[l][0:1] for l in range(L)]),
        jnp.stack([acc_mix[l][0:2].reshape(2 * D) for l in range(L)]),
        dsinks,
        jnp.concatenate([acc_mix[l][2:3] for l in range(L)]),
        jnp.concatenate([acc_mlp[l][0:1] for l in range(L)]),
        acc_loss[0],
        jnp.concatenate([acc_mix[l][3:6] for l in range(L)]),
        acc_loss[1:2])
    adam("w_in", got_in[0], 0)
    (small_land,) = _remote_only(_Exchange([], [small]), "exchange_small")

    row = lambda t: t.reshape(1, D)
    small_out, loss = _adam_small(small_land, me.reshape(1).astype(jnp.int32), {
        "g_mix": (g_mix, m_g_mix, v_g_mix), "b_gates": (b_gates, m_b_gates, v_b_gates), "sinks": (sinks, m_sinks, v_sinks),
        "conv_b": (conv_b, m_conv_b, v_conv_b), "g_mlp": (g_mlp, m_g_mlp, v_g_mlp),
        "g_final": (row(g_final), row(m_g_final), row(v_g_final)), "conv_w": (conv_w, m_conv_w, v_conv_w)})
    small_out["g_final"] = [t.reshape(D) for t in small_out["g_final"]]
    big["w_in"] = [tr(o) for o in big["w_in"]]
    order = ["g_mix", "w_in", "b_gates", "sinks", "w_attn_out", "conv_w", "conv_b", "w_conv_out", "w_o", "g_mlp",
             "w_up", "w_down", "g_final"]
    out = [loss.reshape(()), dcur.reshape(nseq, S, D)]
    for kind in range(4):
        for name in order:
            out.append(big[name][kind] if name in big else small_out[name][kind])
    return tuple(out)
```

```python
import numpy as np
import jax
import jax.numpy as jnp
from jax import lax
from jax.experimental import pallas as pl
from jax.experimental.pallas import tpu as pltpu

D = 1024
NG = 5 * D
NQ = 1024
NKV = 256
NP = NQ + 2 * NKV + NG
F = 4096
HD = 64
GQ = 4
WIN = 128
L = 2
NDEV = 8
EPS = 1e-6
NEG = -1e30
SMALL_ROWS = 24
LR, B1, B2, AEPS, WD, STEP = 0.001, 0.9, 0.999, 1e-08, 0.01, 10

BF = jnp.bfloat16
F32 = jnp.float32
MESH = pl.DeviceIdType.MESH
VMEM_LIMIT = 60 * 1024 * 1024
ANY = pl.BlockSpec(memory_space=pl.ANY)

NN = ((1,), (0,))
NT = ((1,), (1,))
TN = ((0,), (0,))


def _dot(a, b, dims):
    return lax.dot_general(a, b, (dims, ((), ())), preferred_element_type=F32)


def _resident(shape, imap):
    return pl.BlockSpec(shape, imap, pipeline_mode=pl.Buffered(1))


def _position():
    return lax.axis_index("x"), lax.axis_index("y"), lax.axis_index("c")


class _Gather:
    def __init__(self, shards):
        n = len(shards)
        self.inputs = list(shards)
        self.out_shape = [jax.ShapeDtypeStruct((s.shape[0], NDEV) + s.shape[1:], s.dtype) for s in shards]
        self.scratch = [pltpu.SemaphoreType.DMA((n, 7)), pltpu.SemaphoreType.DMA((n, 7)), pltpu.SemaphoreType.DMA((n,))]

    def _plan(self, src, dst, sems):
        send_sems, recv_sems, local_sems = sems
        n = len(src)
        x, y, c = _position()
        me, sibling = (x, y, c), (x, y, 1 - c)
        chips = [(1 - x, y), (x, 1 - y), (1 - x, 1 - y)]

        def rows(a, p):
            return dst[a].at[:, 4 * p[0] + 2 * p[1] + p[2]]

        def copy(a, k, block, to, from_src=False):
            return pltpu.make_async_remote_copy(
                src_ref=src[a] if from_src else rows(a, block), dst_ref=rows(a, block),
                send_sem=send_sems.at[a, k], recv_sem=recv_sems.at[a, k], device_id=to, device_id_type=MESH)

        mine = [pltpu.make_async_copy(src[a], rows(a, me), local_sems.at[a]) for a in range(n)]
        first = []
        for a in range(n):
            first.append(copy(a, 0, me, sibling, True))
            first += [copy(a, 1 + t, me, (*chip, c), True) for t, chip in enumerate(chips)]
        return n, c, me, sibling, chips, copy, mine, first

    def start(self, src, dst, sems):
        *_, mine, first = self._plan(src, dst, sems)
        for cp in mine + first:
            cp.start()

    def relay(self, src, dst, sems):
        n, c, me, sibling, chips, copy, _, _ = self._plan(src, dst, sems)
        for t, chip in enumerate(chips):
            for a in range(n):
                copy(a, 1 + t, (*chip, c), me).wait_recv()
                copy(a, 4 + t, (*chip, c), sibling).start()

    def finish(self, src, dst, sems):
        n, c, me, sibling, chips, copy, mine, first = self._plan(src, dst, sems)
        for a in range(n):
            copy(a, 0, sibling, me).wait_recv()
            for t, chip in enumerate(chips):
                copy(a, 4 + t, (*chip, 1 - c), me).wait_recv()
        for cp in first + [copy(a, 4 + t, (*chip, c), sibling) for t, chip in enumerate(chips) for a in range(n)]:
            cp.wait_send()
        for cp in mine:
            cp.wait()


class _Exchange:
    def __init__(self, grads, everyone=()):
        self.inputs = list(grads) + list(everyone)
        self.n_blocked = len(grads)
        n = len(self.inputs)
        self.out_shape = [jax.ShapeDtypeStruct(g.shape, g.dtype) for g in grads]
        self.out_shape += [jax.ShapeDtypeStruct((NDEV,) + e.shape, e.dtype) for e in everyone]
        self.scratch = [pltpu.SemaphoreType.DMA((n, 7)), pltpu.SemaphoreType.DMA((n, 7)), pltpu.SemaphoreType.DMA((n,))]

    def _plan(self, src, land, sems):
        send_sems, recv_sems, local_sems = sems
        x, y, c = _position()
        me = 4 * x + 2 * y + c

        def parts(peer_idx):
            return [(s.at[peer_idx] if a < self.n_blocked else s, land[a].at[me]) for a, s in enumerate(src)]

        local = [pltpu.make_async_copy(s, d, local_sems.at[a]) for a, (s, d) in enumerate(parts(me))]
        sent = []
        for rel in range(1, NDEV):
            px = 1 - x if rel & 4 else x
            py = 1 - y if rel & 2 else y
            pc = 1 - c if rel & 1 else c
            for a, (s, d) in enumerate(parts(4 * px + 2 * py + pc)):
                sent.append(pltpu.make_async_remote_copy(
                    src_ref=s, dst_ref=d, send_sem=send_sems.at[a, rel - 1], recv_sem=recv_sems.at[a, rel - 1],
                    device_id=(px, py, pc), device_id_type=MESH))
        return local, sent

    def start(self, src, land, sems):
        local, sent = self._plan(src, land, sems)
        for cp in local + sent:
            cp.start()

    def relay(self, src, land, sems):
        pass

    def finish(self, src, land, sems):
        local, sent = self._plan(src, land, sems)
        for cp in sent:
            cp.wait_recv()
        for cp in sent:
            cp.wait_send()
        for cp in local:
            cp.wait()


def _call(body, *, name, grid, in_specs, out_specs, out_shape, args, scratch_shapes=(), aliases=None, side=None,
          relay_at=1.0):
    sem = ("arbitrary",) * len(grid)
    if side is None:
        outs = pl.pallas_call(
            body, name=name, grid=grid, in_specs=in_specs, out_specs=out_specs, out_shape=out_shape,
            scratch_shapes=list(scratch_shapes), input_output_aliases=aliases or {},
            compiler_params=pltpu.CompilerParams(dimension_semantics=sem, vmem_limit_bytes=VMEM_LIMIT))(*args)
        return outs, []
    ni, no, ns = len(in_specs), len(out_specs), len(scratch_shapes)
    si, so = len(side.inputs), len(side.out_shape)

    def hosted(*refs):
        ins, refs = refs[:ni], refs[ni:]
        sins, refs = refs[:si], refs[si:]
        outs, refs = refs[:no], refs[no:]
        souts, refs = refs[:so], refs[so:]
        scr, sscr = refs[:ns], refs[ns:]
        step = pl.program_id(0)
        for d in range(1, len(grid)):
            step = step * grid[d] + pl.program_id(d)
        last = int(np.prod(grid)) - 1

        @pl.when(step == 0)
        def _():
            side.start(sins, souts, sscr)

        body(*ins, *outs, *scr)

        @pl.when(step == min(int(relay_at * last), last))
        def _():
            side.relay(sins, souts, sscr)

        @pl.when(step == last)
        def _():
            side.finish(sins, souts, sscr)

    outs = pl.pallas_call(
        hosted, name=name, grid=grid, in_specs=list(in_specs) + [ANY] * si, out_specs=list(out_specs) + [ANY] * so,
        out_shape=list(out_shape) + side.out_shape, scratch_shapes=list(scratch_shapes) + side.scratch,
        input_output_aliases=aliases or {},
        compiler_params=pltpu.CompilerParams(dimension_semantics=sem, vmem_limit_bytes=VMEM_LIMIT, has_side_effects=True),
    )(*args, *side.inputs)
    return outs[:no], outs[no:]


def _remote_only(side, name):
    n = len(side.inputs)

    def body(*refs):
        src, dst, sems = refs[:n], refs[n:n + len(side.out_shape)], refs[n + len(side.out_shape):]
        side.start(src, dst, sems)
        side.relay(src, dst, sems)
        side.finish(src, dst, sems)

    return pl.pallas_call(
        body, name=name, in_specs=[ANY] * n, out_specs=[ANY] * len(side.out_shape), out_shape=side.out_shape,
        scratch_shapes=side.scratch, compiler_params=pltpu.CompilerParams(has_side_effects=True))(*side.inputs)


def _rms(x, g):
    r = lax.rsqrt(jnp.mean(x * x, axis=-1, keepdims=True) + EPS)
    return x * r * g


def _rms_bwd(dy, x, g):
    r = lax.rsqrt(jnp.mean(x * x, axis=-1, keepdims=True) + EPS)
    xh = x * r
    dxh = dy * g
    dx = r * (dxh - xh * jnp.mean(dxh * xh, axis=-1, keepdims=True))
    return dx, jnp.sum(dy * xh, axis=0, keepdims=True)


def _zero_at_first_step(acc_ref):
    first = pl.program_id(0) == 0

    @pl.when(first)
    def _():
        acc_ref[...] = jnp.zeros_like(acc_ref)


ROW1 = lambda: _resident((1, D), lambda i: (0, 0))
ACC = lambda: pl.BlockSpec((8, D), lambda i: (0, 0))


def _inproj_fwd(x, g, wint, l, tm, side=None, relay_at=1.0):
    T = x.shape[0]

    def body(x_ref, g_ref, w_ref, gt_ref, q_ref, k_ref, v_ref, h_ref):
        h = _rms(x_ref[...], g_ref[...]).astype(BF)
        h_ref[...] = h
        q_ref[...] = _dot(w_ref[0:NQ, :], h, NT).astype(BF)
        k_ref[...] = _dot(w_ref[NQ:NQ + NKV, :], h, NT).astype(BF)
        v_ref[...] = _dot(w_ref[NQ + NKV:NQ + 2 * NKV, :], h, NT).astype(BF)
        for s in range(5):
            lo = NQ + 2 * NKV + s * D
            gt_ref[:, s * D:(s + 1) * D] = _dot(h, w_ref[lo:lo + D, :], NT).astype(BF)

    tok = lambda w: pl.BlockSpec((tm, w), lambda i: (i, 0))
    feat = lambda w: pl.BlockSpec((w, tm), lambda i: (0, i))
    return _call(
        body, name=f"inproj_fwd{l}", grid=(T // tm,),
        in_specs=[tok(D), ROW1(), _resident((NP, D), lambda i: (0, 0))],
        out_specs=[tok(NG), feat(NQ), feat(NKV), feat(NKV), tok(D)],
        out_shape=[jax.ShapeDtypeStruct((T, NG), BF)] + [jax.ShapeDtypeStruct((w, T), BF) for w in (NQ, NKV, NKV)]
        + [jax.ShapeDtypeStruct((T, D), BF)],
        args=(x, g, wint), side=side, relay_at=relay_at)


def _band_geometry():
    j = lax.broadcasted_iota(jnp.int32, (2 * WIN, WIN), 0)
    r = lax.broadcasted_iota(jnp.int32, (2 * WIN, WIN), 1)
    dist = WIN + r - j
    dist0 = r - j
    return (dist.astype(F32), (dist >= 0) & (dist < WIN)), (dist0.astype(F32), dist0 >= 0)


def _pair_biases(sm_ref, pj):
    return [[jnp.where(ok, -sm_ref[1, pj * 2 * GQ + h] * dist, NEG) for h in range(2 * GQ)]
            for dist, ok in _band_geometry()]


def _reduce_rows(x, pair, whole):
    while x.shape[0] > 8:
        half = x.shape[0] // 2
        x = pair(x[:half], x[half:])
    return whole(x, axis=0, keepdims=True)


def _heads_on_lanes(ref, kvh, r0):
    return jnp.concatenate([ref[(kvh * GQ + g) * HD:(kvh * GQ + g + 1) * HD, pl.ds(r0, WIN)] for g in range(GQ)], axis=1)


def _band_probs(sm_ref, head0, q_ref, k_ref, r0, p0, kvh, biases):
    qt = _heads_on_lanes(q_ref, kvh, r0) * jnp.asarray(HD ** -0.5, BF)
    kt = k_ref[kvh * HD:(kvh + 1) * HD, pl.ds(p0, 2 * WIN)]
    st = _dot(kt, qt, TN)
    heads = []
    for g in range(GQ):
        sink = sm_ref[0, head0 + g]
        s = st[:, g * WIN:(g + 1) * WIN] + biases[kvh * GQ + g]
        m = jnp.maximum(_reduce_rows(s, jnp.maximum, jnp.max), sink)
        p = jnp.exp(s - m)
        ps = jnp.exp(sink - m)
        heads.append((p, ps, 1.0 / (_reduce_rows(p, jnp.add, jnp.sum) + ps)))
    return qt, kt, heads


def _attn_fwd(q, k, v, sm, l, S, side=None, relay_at=1.0):
    T = q.shape[1]
    nblk = S // WIN
    unroll = next(u for u in (5, 3, 1) if (nblk - 1) % u == 0)

    def body(sm_ref, q_ref, k_ref, v_ref, o_ref):
        pj = pl.program_id(1)
        biases = _pair_biases(sm_ref, pj)

        def block(i, first):
            r0 = 0 if first else pl.multiple_of(i * WIN, WIN)
            p0 = 0 if first else pl.multiple_of(i * WIN - WIN, WIN)
            for kvh in range(2):
                head0 = (2 * pj + kvh) * GQ
                _, _, heads = _band_probs(sm_ref, head0, q_ref, k_ref, r0, p0, kvh, biases[first])
                pt = jnp.concatenate([p.astype(BF) for p, _, _ in heads], axis=1)
                ot = _dot(v_ref[kvh * HD:(kvh + 1) * HD, pl.ds(p0, 2 * WIN)], pt, NN)
                for g in range(GQ):
                    rows = slice((kvh * GQ + g) * HD, (kvh * GQ + g + 1) * HD)
                    o_ref[rows, pl.ds(r0, WIN)] = (ot[:, g * WIN:(g + 1) * WIN] * heads[g][2]).astype(BF)

        block(0, True)

        def rest(t, c):
            for u in range(unroll):
                block(1 + unroll * t + u, False)
            return c

        lax.fori_loop(0, (nblk - 1) // unroll, rest, 0)

    wide = lambda: pl.BlockSpec((2 * GQ * HD, S), lambda s, p: (p, s))
    narrow = lambda: pl.BlockSpec((2 * HD, S), lambda s, p: (p, s))
    return _call(
        body, name=f"attn_fwd{l}", grid=(T // S, 2),
        in_specs=[pl.BlockSpec(memory_space=pltpu.SMEM), wide(), narrow(), narrow()],
        out_specs=[wide()], out_shape=[jax.ShapeDtypeStruct((D, T), BF)],
        args=(sm, q, k, v), side=side, relay_at=relay_at)


def _shift_rows(y, k, edge_rows, down):
    n = y.shape[0]
    rid = lax.broadcasted_iota(jnp.int32, y.shape, 0)
    out = pltpu.roll(y, k if down else n - k, 0)
    for t, row in enumerate(edge_rows):
        out = jnp.where(rid == (t if down else n - k + t), row, out)
    return out


def _mixer_fwd(x, gt, att, w3, bg, cw, cbias, l, S, tm, side=None, relay_at=1.0):
    T = x.shape[0]

    def body(x_ref, cb_ref, cc_ref, cu_ref, ga_ref, gc_ref, cch_ref, cuh_ref, att_ref, wao_ref, wco_ref, wo_ref,
             bg_ref, cw_ref, cbias_ref, x1_ref, mg_ref, co_ref, ya_ref, yc_ref, zb_ref):
        first = (pl.program_id(0) * tm) % S == 0
        y = cc_ref[...].astype(F32) * cu_ref[...].astype(F32)
        hy1 = cch_ref[15:16, :].astype(F32) * cuh_ref[15:16, :].astype(F32)
        hy2 = cch_ref[14:15, :].astype(F32) * cuh_ref[14:15, :].astype(F32)
        hy1, hy2 = jnp.where(first, 0.0, hy1), jnp.where(first, 0.0, hy2)
        z = (cw_ref[0:1, :] * _shift_rows(y, 2, [hy2, hy1], True) + cw_ref[1:2, :] * _shift_rows(y, 1, [hy1], True)
             + cw_ref[2:3, :] * y)
        zb = z + cbias_ref[...]
        zb_ref[...] = zb.astype(BF)
        co = (cb_ref[...].astype(F32) * zb).astype(BF)
        co_ref[...] = co
        yc = _dot(co, wco_ref[...], NN)
        ya = _dot(att_ref[...], wao_ref[...], TN)
        ya_ref[...] = ya.astype(BF)
        yc_ref[...] = yc.astype(BF)
        sa = jax.nn.sigmoid(ga_ref[...].astype(F32) + bg_ref[0:1, :])
        sc = jax.nn.sigmoid(gc_ref[...].astype(F32) + bg_ref[1:2, :])
        mg = (sa * ya + sc * yc).astype(BF)
        mg_ref[...] = mg
        x1_ref[...] = x_ref[...] + _dot(mg, wo_ref[...], NN)

    tok = lambda: pl.BlockSpec((tm, D), lambda i: (i, 0))
    seg = lambda s: pl.BlockSpec((tm, D), lambda i: (i, s))
    halo = lambda s: pl.BlockSpec((16, D), lambda i: (jnp.maximum(i * (tm // 16) - 1, 0), s))
    wsp = lambda k: _resident((None, D, D), lambda i: (k, 0, 0))
    row = lambda n: _resident((n, D), lambda i: (0, 0))
    return _call(
        body, name=f"mixer_fwd{l}", grid=(T // tm,),
        in_specs=[tok(), seg(0), seg(1), seg(2), seg(3), seg(4), halo(1), halo(2),
                  pl.BlockSpec((D, tm), lambda i: (0, i)), wsp(0), wsp(1), wsp(2), row(2), row(8), row(1)],
        out_specs=[tok()] * 6,
        out_shape=[jax.ShapeDtypeStruct((T, D), dt) for dt in (F32, BF, BF, BF, BF, BF)],
        args=(x, gt, gt, gt, gt, gt, gt, gt, att, w3, w3, w3, bg, cw, cbias), side=side, relay_at=relay_at)


def _mlp_fwd(x1, g, w2, l, tm, head=None, side=None, relay_at=1.0):
    T = x1.shape[0]
    nt = T // tm
    FC = 1024

    def body(x_ref, g_ref, wup_ref, wdn_ref, *rest):
        out_ref, h_ref, a_ref = rest[-4:-1] if head else rest
        x = x_ref[...]
        h = _rms(x, g_ref[...]).astype(BF)
        h_ref[...] = h
        acc = x
        for c in range(F // FC):
            a = _dot(h, wup_ref[c * FC:(c + 1) * FC, :], NT)
            a_ref[:, c * FC:(c + 1) * FC] = a.astype(BF)
            u = jnp.maximum(a, 0.0)
            acc = acc + _dot((u * u).astype(BF), wdn_ref[c * FC:(c + 1) * FC, :], NN)
        if not head:
            out_ref[...] = acc
            return
        t_ref, gf_ref, acc_ref = rest[0], rest[1], rest[-1]
        _zero_at_first_step(acc_ref)
        gf = gf_ref[...]
        err = _rms(acc, gf) - t_ref[...]
        out_ref[...], dg = _rms_bwd(err * (1.0 / D), acc, gf)
        acc_ref[0:1, :] += dg
        acc_ref[1:2, :] += jnp.sum(err * err, axis=0, keepdims=True)

        @pl.when(pl.program_id(0) == nt - 1)
        def _():
            acc_ref[1:2, :] = jnp.zeros((1, D), F32) + (0.5 / D) * jnp.sum(acc_ref[1:2, :])

    tok = lambda w: pl.BlockSpec((tm, w), lambda i: (i, 0))
    wsp = lambda k: _resident((None, F, D), lambda i: (k, 0, 0))
    return _call(
        body, name=f"mlp_fwd{l}", grid=(nt,),
        in_specs=[tok(D), ROW1(), wsp(0), wsp(1)] + ([tok(D), ROW1()] if head else []),
        out_specs=[tok(D), tok(D), tok(F)] + ([ACC()] if head else []),
        out_shape=[jax.ShapeDtypeStruct((T, D), F32), jax.ShapeDtypeStruct((T, D), BF), jax.ShapeDtypeStruct((T, F), BF)]
        + ([jax.ShapeDtypeStruct((8, D), F32)] if head else []),
        args=(x1, g, w2, w2) + (tuple(head) if head else ()), side=side, relay_at=relay_at)


def _mlp_bwd(dx2, x1, a, g, w2, l, tm, side=None):
    T = dx2.shape[0]
    FC = 1024

    def body(d_ref, x_ref, a_ref, g_ref, wup_ref, wdn_ref, da_ref, dx1_ref, db_ref, acc_ref):
        _zero_at_first_step(acc_ref)
        d = d_ref[...]
        db = d.astype(BF)
        db_ref[...] = db
        dh = jnp.zeros((tm, D), F32)
        for c in range(F // FC):
            du = _dot(db, wdn_ref[c * FC:(c + 1) * FC, :], NT)
            da = (du * (2.0 * jnp.maximum(a_ref[:, c * FC:(c + 1) * FC].astype(F32), 0.0))).astype(BF)
            da_ref[:, c * FC:(c + 1) * FC] = da
            dh = dh + _dot(da, wup_ref[c * FC:(c + 1) * FC, :], NN)
        dx, dg = _rms_bwd(dh, x_ref[...], g_ref[...])
        dx1_ref[...] = d + dx
        acc_ref[0:1, :] += dg

    tok = lambda w: pl.BlockSpec((tm, w), lambda i: (i, 0))
    wsp = lambda k: _resident((None, F, D), lambda i: (k, 0, 0))
    return _call(
        body, name=f"mlp_bwd{l}", grid=(T // tm,),
        in_specs=[tok(D), tok(D), tok(F), ROW1(), wsp(0), wsp(1)],
        out_specs=[tok(F), tok(D), tok(D), ACC()],
        out_shape=[jax.ShapeDtypeStruct((T, F), BF), jax.ShapeDtypeStruct((T, D), F32),
                   jax.ShapeDtypeStruct((T, D), BF), jax.ShapeDtypeStruct((8, D), F32)],
        args=(dx2, x1, a, g, w2, w2), side=side)


def _mixer_bwd(dx1, gt, ya, yc, zb, w3, bg, cw, l, S, tm, side=None):
    T = dx1.shape[0]
    nt = T // tm

    def body(d_ref, cb_ref, cc_ref, cu_ref, ga_ref, gc_ref, ya_ref, yc_ref, zb_ref, wao_ref, wco_ref, wo_ref,
             bg_ref, cw_ref, dg_ref, datt_ref, dya_ref, dyc_ref, db_ref, acc_ref, carry_ref):
        ti = nt - 1 - pl.program_id(0)
        _zero_at_first_step(acc_ref)

        @pl.when(((ti + 1) * tm) % S == 0)
        def _():
            carry_ref[...] = jnp.zeros_like(carry_ref)

        db = d_ref[...].astype(BF)
        db_ref[...] = db
        dm = _dot(db, wo_ref[...], NT)
        sa = jax.nn.sigmoid(ga_ref[...].astype(F32) + bg_ref[0:1, :])
        sc = jax.nn.sigmoid(gc_ref[...].astype(F32) + bg_ref[1:2, :])
        dya32 = dm * sa
        dyc32 = dm * sc
        dya = dya32.astype(BF)
        dyc = dyc32.astype(BF)
        dya_ref[...] = dya
        dyc_ref[...] = dyc
        dga = dya32 * ya_ref[...].astype(F32) * (1.0 - sa)
        dgc = dyc32 * yc_ref[...].astype(F32) * (1.0 - sc)
        dg_ref[:, 3 * D:4 * D] = dga.astype(BF)
        dg_ref[:, 4 * D:5 * D] = dgc.astype(BF)
        acc_ref[0:1, :] += jnp.sum(dga, axis=0, keepdims=True)
        acc_ref[1:2, :] += jnp.sum(dgc, axis=0, keepdims=True)
        datt_ref[...] = _dot(wao_ref[...], dya, NT).astype(BF)
        dco = _dot(dyc, wco_ref[...], NT)

        cc = cc_ref[...].astype(F32)
        cu = cu_ref[...].astype(F32)
        y = cc * cu
        dg_ref[:, 0:D] = (dco * zb_ref[...].astype(F32)).astype(BF)
        dz = dco * cb_ref[...].astype(F32)
        u1 = _shift_rows(dz, 1, [carry_ref[0:1, :]], False)
        u2 = _shift_rows(dz, 2, [carry_ref[0:1, :], carry_ref[1:2, :]], False)
        acc_ref[2:3, :] += jnp.sum(dz, axis=0, keepdims=True)
        acc_ref[3:4, :] += jnp.sum(u2 * y, axis=0, keepdims=True)
        acc_ref[4:5, :] += jnp.sum(u1 * y, axis=0, keepdims=True)
        acc_ref[5:6, :] += jnp.sum(dz * y, axis=0, keepdims=True)
        dy = cw_ref[2:3, :] * dz + cw_ref[1:2, :] * u1 + cw_ref[0:1, :] * u2
        dg_ref[:, D:2 * D] = (dy * cu).astype(BF)
        dg_ref[:, 2 * D:3 * D] = (dy * cc).astype(BF)
        carry_ref[...] = dz[0:8, :]

    tok = lambda w=D: pl.BlockSpec((tm, w), lambda i: (nt - 1 - i, 0))
    seg = lambda s: pl.BlockSpec((tm, D), lambda i: (nt - 1 - i, s))
    wsp = lambda k: _resident((None, D, D), lambda i: (k, 0, 0))
    row = lambda n: _resident((n, D), lambda i: (0, 0))
    return _call(
        body, name=f"mixer_bwd{l}", grid=(nt,),
        in_specs=[tok(), seg(0), seg(1), seg(2), seg(3), seg(4), tok(), tok(), tok(), wsp(0), wsp(1), wsp(2),
                  row(2), row(8)],
        out_specs=[tok(NG), pl.BlockSpec((D, tm), lambda i: (0, nt - 1 - i)), tok(), tok(), tok(), ACC()],
        out_shape=[jax.ShapeDtypeStruct((T, NG), BF), jax.ShapeDtypeStruct((D, T), BF)]
        + [jax.ShapeDtypeStruct((T, D), BF)] * 3 + [jax.ShapeDtypeStruct((8, D), F32)],
        scratch_shapes=[pltpu.VMEM((8, D), F32)],
        args=(dx1, gt, gt, gt, gt, gt, ya, yc, zb, w3, w3, w3, bg, cw), side=side)


def _attn_bwd(q, k, v, att, datt, sm, l, S, side=None):
    T = q.shape[1]
    nblk = S // WIN
    unroll = next(u for u in (5, 3, 1) if (nblk - 1) % u == 0)
    scale = HD ** -0.5

    def body(sm_ref, q_ref, k_ref, v_ref, o_ref, do_ref, dq_ref, dk_ref, dv_ref, ds_ref, dka_ref, dva_ref):
        pj = pl.program_id(1)
        biases = _pair_biases(sm_ref, pj)
        dka_ref[...] = jnp.zeros_like(dka_ref)
        dva_ref[...] = jnp.zeros_like(dva_ref)

        def block(i, first, dsinks):
            r0 = 0 if first else pl.multiple_of(i * WIN, WIN)
            p0 = 0 if first else pl.multiple_of(i * WIN - WIN, WIN)
            out = []
            for kvh in range(2):
                head0 = (2 * pj + kvh) * GQ
                rows = slice(kvh * HD, (kvh + 1) * HD)
                dob = _heads_on_lanes(do_ref, kvh, r0)
                dpt = _dot(v_ref[rows, pl.ds(p0, 2 * WIN)], dob, TN)
                qt, kt, heads = _band_probs(sm_ref, head0, q_ref, k_ref, r0, p0, kvh, biases[first])
                inv = jnp.concatenate([h[2] for h in heads], axis=1)
                do32 = dob.astype(F32)
                delta = jnp.sum(do32 * _heads_on_lanes(o_ref, kvh, r0).astype(F32), axis=0, keepdims=True) * inv
                dosb = (do32 * inv).astype(BF)
                dst = jnp.concatenate(
                    [(p * (dpt[:, g * WIN:(g + 1) * WIN] * h_inv - delta[:, g * WIN:(g + 1) * WIN])).astype(BF)
                     for g, (p, _, h_inv) in enumerate(heads)], axis=1)
                pt = jnp.concatenate([p.astype(BF) for p, _, _ in heads], axis=1)
                dqt = _dot(kt, dst, NN) * scale
                for g in range(GQ):
                    hr = slice((kvh * GQ + g) * HD, (kvh * GQ + g + 1) * HD)
                    dq_ref[hr, pl.ds(r0, WIN)] = dqt[:, g * WIN:(g + 1) * WIN].astype(BF)
                dka_ref[rows, pl.ds(p0, 2 * WIN)] += _dot(qt, dst, NT)
                dva_ref[rows, pl.ds(p0, 2 * WIN)] += _dot(dosb, pt, NT)
                ps = jnp.concatenate([h[1] for h in heads], axis=1)
                out.append(dsinks[kvh] - ps * delta)
            return tuple(out)

        zero = jnp.zeros((1, GQ * WIN), F32)
        def rest(t, c):
            for u in range(unroll):
                c = block(1 + unroll * t + u, False, c)
            return c

        dsinks = lax.fori_loop(0, (nblk - 1) // unroll, rest, block(0, True, (zero, zero)))
        for kvh in range(2):
            for g in range(GQ):
                tot = jnp.sum(dsinks[kvh][:, g * WIN:(g + 1) * WIN])
                ds_ref[kvh * GQ + g:kvh * GQ + g + 1, :] = jnp.zeros((1, 128), F32) + tot
        dk_ref[...] = dka_ref[...].astype(BF)
        dv_ref[...] = dva_ref[...].astype(BF)

    wide = lambda: pl.BlockSpec((2 * GQ * HD, S), lambda s, p: (p, s))
    narrow = lambda: pl.BlockSpec((2 * HD, S), lambda s, p: (p, s))
    return _call(
        body, name=f"attn_bwd{l}", grid=(T // S, 2),
        in_specs=[pl.BlockSpec(memory_space=pltpu.SMEM), wide(), narrow(), narrow(), wide(), wide()],
        out_specs=[wide(), narrow(), narrow(), pl.BlockSpec((None, None, 8, 128), lambda s, p: (s, p, 0, 0))],
        out_shape=[jax.ShapeDtypeStruct((NQ, T), BF), jax.ShapeDtypeStruct((NKV, T), BF),
                   jax.ShapeDtypeStruct((NKV, T), BF), jax.ShapeDtypeStruct((T // S, 2, 8, 128), F32)],
        scratch_shapes=[pltpu.VMEM((2 * HD, S), F32), pltpu.VMEM((2 * HD, S), F32)],
        args=(sm, q, k, v, att, datt), side=side)


def _inproj_bwd(dgt, dq, dk, dv, x, dres, g, wint, l, tm, side=None):
    T = x.shape[0]

    def body(dg_ref, dq_ref, dk_ref, dv_ref, x_ref, dr_ref, g_ref, w_ref, dx_ref, acc_ref):
        _zero_at_first_step(acc_ref)
        dh = _dot(dq_ref[...], w_ref[0:NQ, :], TN)
        dh = dh + _dot(dk_ref[...], w_ref[NQ:NQ + NKV, :], TN)
        dh = dh + _dot(dv_ref[...], w_ref[NQ + NKV:NQ + 2 * NKV, :], TN)
        dh = dh + _dot(dg_ref[...], w_ref[NQ + 2 * NKV:NP, :], NN)
        dx, dg = _rms_bwd(dh, x_ref[...], g_ref[...])
        dx_ref[...] = dr_ref[...] + dx
        acc_ref[0:1, :] += dg

    tok = lambda w: pl.BlockSpec((tm, w), lambda i: (i, 0))
    feat = lambda w: pl.BlockSpec((w, tm), lambda i: (0, i))
    return _call(
        body, name=f"inproj_bwd{l}", grid=(T // tm,),
        in_specs=[tok(NG), feat(NQ), feat(NKV), feat(NKV), tok(D), tok(D), ROW1(), _resident((NP, D), lambda i: (0, 0))],
        out_specs=[tok(D), ACC()],
        out_shape=[jax.ShapeDtypeStruct((T, D), F32), jax.ShapeDtypeStruct((8, D), F32)],
        args=(dgt, dq, dk, dv, x, dres, g, wint), side=side)


def _wgrad(a, b, rows, row0, into, name, relu2=False, a_is_transposed=False, side=None):
    M, T = a.shape if a_is_transposed else a.shape[::-1]
    tmm = next(t for t in (1024, 512, 256) if M % t == 0 and row0 % t == 0)
    tk = min(4096 if M > 4096 else 2048, T)
    nk = T // tk
    blk0 = row0 // tmm

    def body(*refs):
        a_ref, b_ref = refs[0], refs[1]
        o_ref, acc_ref = refs[-2], refs[-1]
        kk = pl.program_id(1)

        @pl.when(kk == 0)
        def _():
            acc_ref[...] = jnp.zeros_like(acc_ref)

        av = a_ref[...]
        if relu2:
            t = jnp.maximum(av.astype(F32), 0.0)
            av = (t * t).astype(BF)
        acc_ref[...] += _dot(av, b_ref[...], NN if a_is_transposed else TN)

        @pl.when(kk == nk - 1)
        def _():
            o_ref[...] = acc_ref[...].astype(BF)

    a_spec = pl.BlockSpec((tmm, tk), lambda j, kk: (j, kk)) if a_is_transposed else pl.BlockSpec((tk, tmm), lambda j, kk: (kk, j))
    in_specs = [a_spec, pl.BlockSpec((tk, D), lambda j, kk: (kk, 0))]
    args = [a, b]
    if into is not None:
        in_specs.append(ANY)
        args.append(into)
    (out,), got = _call(
        body, name=name, grid=(M // tmm, nk), in_specs=in_specs,
        out_specs=[pl.BlockSpec((tmm, D), lambda j, kk: (blk0 + j, 0))],
        out_shape=[jax.ShapeDtypeStruct((rows, D), BF)], scratch_shapes=[pltpu.VMEM((tmm, D), F32)],
        aliases={2: 0} if into is not None else None, args=args, side=side)
    return (out, got) if side is not None else out


def _wgrad_square(jobs, name):
    n = len(jobs)
    T = jobs[0][1].shape[0]
    tk = min(1024, T)
    nk = T // tk

    def body(*refs):
        ins, outs, acc_ref = refs[:2 * n], refs[2 * n:3 * n], refs[3 * n]
        j, kk = pl.program_id(0), pl.program_id(1)

        @pl.when(kk == 0)
        def _():
            acc_ref[...] = jnp.zeros_like(acc_ref)

        for p, (_, _, a_t) in enumerate(jobs):
            @pl.when(j == p)
            def _(p=p, a_t=a_t):
                acc_ref[...] += _dot(ins[2 * p][...], ins[2 * p + 1][...], NN if a_t else TN)

                @pl.when(kk == nk - 1)
                def _():
                    outs[p][...] = acc_ref[...].astype(BF)

    def step(p):
        return lambda j, kk: jnp.where(j == p, kk, jnp.where(j > p, nk - 1, 0))

    in_specs, args = [], []
    for p, (a, b, a_t) in enumerate(jobs):
        at = step(p)
        in_specs.append(pl.BlockSpec((D, tk), lambda j, kk, at=at: (0, at(j, kk))) if a_t
                        else pl.BlockSpec((tk, D), lambda j, kk, at=at: (at(j, kk), 0)))
        in_specs.append(pl.BlockSpec((tk, D), lambda j, kk, at=at: (at(j, kk), 0)))
        args += [a, b]
    outs, _ = _call(
        body, name=name, grid=(n, nk), in_specs=in_specs,
        out_specs=[pl.BlockSpec((D, D), lambda j, kk: (0, 0))] * n,
        out_shape=[jax.ShapeDtypeStruct((D, D), BF)] * n, scratch_shapes=[pltpu.VMEM((D, D), F32)], args=args)
    return outs


def _adamw(w, g, m, v):
    m = B1 * m + (1.0 - B1) * g
    v = B2 * v + (1.0 - B2) * (g * g)
    m_hat = m / (1.0 - B1 ** STEP)
    v_hat = v / (1.0 - B2 ** STEP)
    return -LR * (m_hat / (jnp.sqrt(v_hat) + AEPS) + WD * w), m, v


def _adam_sum(land, w, m, v, l, into, name, transposed=False):
    _, r, _ = land.shape
    tr = 208 if r % 208 == 0 else (256 if r % 256 == 0 else r)
    tc = 256

    def body(land_ref, w_ref, m_ref, v_ref, *rest):
        g_ref, d_ref, nm_ref, nv_ref = rest[-4:]
        g = land_ref[0].astype(F32)
        for s in range(1, NDEV):
            g = g + land_ref[s].astype(F32)
        if transposed:
            g = g.T
        g_ref[...] = g
        d_ref[...], nm_ref[...], nv_ref[...] = _adamw(w_ref[...], g, m_ref[...], v_ref[...])

    if transposed:
        blk = lambda: pl.BlockSpec((None, tc, r), lambda j: (l, j, 0))
        land_spec, grid = pl.BlockSpec((NDEV, r, tc), lambda j: (0, 0, j)), (D // tc,)
    else:
        blk = lambda: pl.BlockSpec((None, tr, D), lambda j: (l, j, 0))
        land_spec, grid = pl.BlockSpec((NDEV, tr, D), lambda j: (0, j, 0)), (r // tr,)
    in_specs = [land_spec, blk(), blk(), blk()]
    args = [land, w, m, v]
    aliases = None
    if into is not None:
        in_specs += [ANY] * 4
        args += list(into)
        aliases = {4 + t: t for t in range(4)}
    outs, _ = _call(body, name=name, grid=grid, in_specs=in_specs, out_specs=[blk()] * 4,
                    out_shape=[jax.ShapeDtypeStruct(w.shape, F32)] * 4, aliases=aliases, args=args)
    return outs


SMALL_NAMES = ("g_mix", "b_gates", "sinks", "conv_b", "g_mlp", "g_final", "conv_w")


def _adam_small(land, me, masters):
    n = len(SMALL_NAMES)
    lanes = D // NDEV

    def body(land_ref, me_ref, *refs):
        ins, outs, loss_ref, gs_ref = refs[:3 * n], refs[3 * n:7 * n], refs[7 * n], refs[7 * n + 1]
        g = land_ref[0]
        for s in range(1, NDEV):
            g = g + land_ref[s]
        gs_ref[...] = g

        def update(k, g_piece, idx):
            w_ref, m_ref, v_ref = ins[3 * k:3 * k + 3]
            outs[4 * k][idx] = g_piece
            outs[4 * k + 1][idx], outs[4 * k + 2][idx], outs[4 * k + 3][idx] = _adamw(w_ref[idx], g_piece, m_ref[idx], v_ref[idx])

        whole = (slice(None), slice(None))
        update(0, gs_ref[0:2, :], whole)
        for l in range(L):
            for h in range(2):
                update(1, gs_ref[2 + 2 * l + h:3 + 2 * l + h, :], (slice(l, l + 1), slice(h * D, (h + 1) * D)))
            update(2, gs_ref[6:7, 16 * l:16 * (l + 1)], (slice(l, l + 1), slice(None)))
        update(3, gs_ref[7:9, :], whole)
        update(4, gs_ref[9:11, :], whole)
        update(5, gs_ref[11:12, :], whole)
        mine = pl.ds(pl.multiple_of(me_ref[0] * lanes, lanes), lanes)
        for l in range(L):
            for k in range(3):
                update(6, gs_ref[12 + 3 * l + k:13 + 3 * l + k, mine], (l, slice(k, k + 1), slice(None)))
        loss_ref[...] = gs_ref[18:19, 0:1]

    flat = [t for name in SMALL_NAMES for t in masters[name]]
    vmem = pl.BlockSpec(memory_space=pltpu.VMEM)
    outs = pl.pallas_call(
        body, name="adam_small",
        in_specs=[vmem, pl.BlockSpec(memory_space=pltpu.SMEM)] + [vmem] * len(flat),
        out_shape=[jax.ShapeDtypeStruct(masters[name][0].shape, F32) for name in SMALL_NAMES for _ in range(4)]
        + [jax.ShapeDtypeStruct((1, 1), F32)],
        scratch_shapes=[pltpu.VMEM((SMALL_ROWS, D), F32)],
        compiler_params=pltpu.CompilerParams(vmem_limit_bytes=VMEM_LIMIT))(land, me, *flat)
    return {name: outs[4 * k:4 * k + 4] for k, name in enumerate(SMALL_NAMES)}, outs[-1]


def _pack_small(g_mix, b_gates, sinks, conv_b, g_mlp, g_final, conv_w_rows, extra):
    sink_row = jnp.zeros((1, D), F32).at[0, :2 * 16].set(sinks.reshape(-1))
    return jnp.concatenate([g_mix, b_gates.reshape(4, D), sink_row, conv_b, g_mlp, g_final.reshape(1, D),
                            conv_w_rows, extra, jnp.zeros((SMALL_ROWS - 19, D), F32)], axis=0)


def kernel(x, g_mix, w_in, b_gates, sinks, w_attn_out, conv_w, conv_b, w_conv_out, w_o, g_mlp, w_up, w_down, g_final, loss_target, m_g_mix, m_w_in, m_b_gates, m_sinks, m_w_attn_out, m_conv_w, m_conv_b, m_w_conv_out, m_w_o, m_g_mlp, m_w_up, m_w_down, m_g_final, v_g_mix, v_w_in, v_b_gates, v_sinks, v_w_attn_out, v_conv_w, v_conv_b, v_w_conv_out, v_w_o, v_g_mlp, v_w_up, v_w_down, v_g_final):
    nseq, S, _ = x.shape
    T = nseq * S
    tm_in = min(512, S)
    tm = min(512, S)
    xi, yi, ci = _position()
    me = 4 * xi + 2 * yi + ci
    tr = lambda t: jnp.swapaxes(t, 1, 2)
    blocks = lambda t: t.reshape(NDEV, t.shape[0] // NDEV, D)

    win_t, wup_t = tr(w_in), tr(w_up)
    sh_win = [win_t[l].astype(BF)[None] for l in range(L)]
    sh_w3 = [jnp.stack([w_attn_out[l], w_conv_out[l], w_o[l]]).astype(BF) for l in range(L)]
    sh_w2 = [jnp.stack([wup_t[l], w_down[l]]).astype(BF) for l in range(L)]
    wint, w3, w2 = [None] * L, [None] * L, [None] * L
    wint0_g, cw_g = _remote_only(_Gather([sh_win[0], jnp.pad(conv_w, ((0, 0), (0, 5), (0, 0)))]), "gather_first")
    wint[0] = wint0_g.reshape(NP, D)
    cw = jnp.swapaxes(cw_g, 1, 2).reshape(L, 8, D)
    slopes = np.power(np.float32(2.0), -8.0 * np.arange(1, 17, dtype=np.float32) / 16).astype(np.float32)

    xf = x.reshape(T, D)
    saved = []
    cur = xf
    for l in range(L):
        sm = jnp.stack([sinks[l], jnp.asarray(slopes)])
        bg = b_gates[l].reshape(2, D)
        (gt, q, k, v, h), got = _inproj_fwd(cur, g_mix[l:l + 1], wint[l], l, tm_in, side=_Gather([sh_w2[l]]),
                                            relay_at=0.85)
        w2[l] = got[0].reshape(2, F, D)
        (att,), got = _attn_fwd(q, k, v, sm, l, S, side=_Gather([sh_w3[l]]), relay_at=0.5)
        w3[l] = got[0].reshape(3, D, D)
        (x1, mg, co, ya, yc, zb), _ = _mixer_fwd(cur, gt, att, w3[l], bg, cw[l], conv_b[l:l + 1], l, S, tm)
        head = (loss_target.reshape(T, D), g_final.reshape(1, D)) if l == L - 1 else None
        (nxt, h2, a, *acc_loss), got = _mlp_fwd(x1, g_mlp[l:l + 1], w2[l], l, tm_in, head=head,
                                                side=_Gather([sh_win[l + 1]]) if l + 1 < L else None, relay_at=0.7)
        if l + 1 < L:
            wint[l + 1] = got[0].reshape(NP, D)
        saved.append(dict(x=cur, gt=gt, q=q, k=k, v=v, h=h, att=att, x1=x1, mg=mg, co=co, ya=ya, yc=yc, zb=zb, h2=h2, a=a,
                          sm=sm, bg=bg))
        cur = nxt
    dcur, acc_loss = cur, acc_loss[0]

    masters = {"w_in": (win_t, tr(m_w_in), tr(v_w_in)), "w_attn_out": (w_attn_out, m_w_attn_out, v_w_attn_out),
               "w_conv_out": (w_conv_out, m_w_conv_out, v_w_conv_out), "w_o": (w_o, m_w_o, v_w_o),
               "w_up": (w_up, m_w_up, v_w_up), "w_down": (w_down, m_w_down, v_w_down)}
    big = {name: None for name in masters}

    def adam(name, land, l):
        big[name] = _adam_sum(land, *masters[name], l, big[name], f"adam_{name}{l}", transposed=name == "w_up")

    acc_in, acc_mix, acc_mlp, dsink = [None] * L, [None] * L, [None] * L, [None] * L
    d_win_above = None
    for l in reversed(range(L)):
        sv = saved[l]
        side = _Exchange([blocks(d_win_above)]) if l + 1 < L else None
        (da, dx1, dx2b, acc_mlp[l]), got = _mlp_bwd(dcur, sv["x1"], sv["a"], g_mlp[l:l + 1], w2[l], l, tm_in, side=side)
        if l + 1 < L:
            adam("w_in", got[0], l + 1)
        d_wdn = _wgrad(sv["a"], dx2b, F, 0, None, f"wgrad_down{l}", relu2=True)
        d_wup = _wgrad(da, sv["h2"], F, 0, None, f"wgrad_up{l}")
        (dgt, datt, dya, dyc, dx1b, acc_mix[l]), got = _mixer_bwd(
            dx1, sv["gt"], sv["ya"], sv["yc"], sv["zb"], w3[l], sv["bg"], cw[l], l, S, tm,
            side=_Exchange([blocks(d_wdn)]))
        adam("w_down", got[0], l)
        d_wo, d_wao, d_wco = _wgrad_square(
            [(sv["mg"], dx1b, False), (sv["att"], dya, True), (sv["co"], dyc, False)], f"wgrad_mixer{l}")
        d_win, got = _wgrad(dgt, sv["h"], NP, NQ + 2 * NKV, None, f"wgrad_gates{l}",
                            side=_Exchange([blocks(d_wo), blocks(d_wao), blocks(d_wco)]))
        for name, land in zip(["w_o", "w_attn_out", "w_conv_out"], got):
            adam(name, land, l)
        (dq, dk, dv, dsink[l]), got = _attn_bwd(
            sv["q"], sv["k"], sv["v"], sv["att"], datt, sv["sm"], l, S, side=_Exchange([blocks(d_wup)]))
        adam("w_up", got[0], l)
        d_win = _wgrad(dq, sv["h"], NP, 0, d_win, f"wgrad_q{l}", a_is_transposed=True)
        d_win = _wgrad(dk, sv["h"], NP, NQ, d_win, f"wgrad_k{l}", a_is_transposed=True)
        d_win = _wgrad(dv, sv["h"], NP, NQ + NKV, d_win, f"wgrad_v{l}", a_is_transposed=True)
        side = _Exchange([blocks(d_win)]) if l == 0 else None
        (dcur, acc_in[l]), got_in = _inproj_bwd(dgt, dq, dk, dv, sv["x"], dx1, g_mix[l:l + 1], wint[l], l, tm_in, side=side)
        d_win_above = d_win

    dsinks = jnp.stack([dsink[l][:, :, :, 0].sum(axis=0).reshape(16) for l in range(L)])
    small = _pack_small(
        jnp.concatenate([acc_in[l][0:1] for l in range(L)]),
        jnp.stack([acc_mix[l][0:2].reshape(2 * D) for l in range(L)]),
        dsinks,
        jnp.concatenate([acc_mix[l][2:3] for l in range(L)]),
        jnp.concatenate([acc_mlp[l][0:1] for l in range(L)]),
        acc_loss[0],
        jnp.concatenate([acc_mix[l][3:6] for l in range(L)]),
        acc_loss[1:2])
    adam("w_in", got_in[0], 0)
    (small_land,) = _remote_only(_Exchange([], [small]), "exchange_small")

    row = lambda t: t.reshape(1, D)
    small_out, loss = _adam_small(small_land, me.reshape(1).astype(jnp.int32), {
        "g_mix": (g_mix, m_g_mix, v_g_mix), "b_gates": (b_gates, m_b_gates, v_b_gates), "sinks": (sinks, m_sinks, v_sinks),
        "conv_b": (conv_b, m_conv_b, v_conv_b), "g_mlp": (g_mlp, m_g_mlp, v_g_mlp),
        "g_final": (row(g_final), row(m_g_final), row(v_g_final)), "conv_w": (conv_w, m_conv_w, v_conv_w)})
    small_out["g_final"] = [t.reshape(D) for t in small_out["g_final"]]
    big["w_in"] = [tr(o) for o in big["w_in"]]
    order = ["g_mix", "w_in", "b_gates", "sinks", "w_attn_out", "conv_w", "conv_b", "w_conv_out", "w_o", "g_mlp",
             "w_up", "w_down", "g_final"]
    out = [loss.reshape(()), dcur.reshape(nseq, S, D)]
    for kind in range(4):
        for name in order:
            out.append(big[name][kind] if name in big else small_out[name][kind])
    return tuple(out)
```

```python
import numpy as np
import jax
import jax.numpy as jnp
from jax import lax
from jax.experimental import pallas as pl
from jax.experimental.pallas import tpu as pltpu

D = 1024
NG = 5 * D
NQ = 1024
NKV = 256
NP = NQ + 2 * NKV + NG
F = 4096
HD = 64
GQ = 4
WIN = 128
L = 2
NDEV = 8
EPS = 1e-6
NEG = -1e30
SMALL_ROWS = 24
LR, B1, B2, AEPS, WD, STEP = 0.001, 0.9, 0.999, 1e-08, 0.01, 10

BF = jnp.bfloat16
F32 = jnp.float32
MESH = pl.DeviceIdType.MESH
VMEM_LIMIT = 60 * 1024 * 1024
ANY = pl.BlockSpec(memory_space=pl.ANY)

NN = ((1,), (0,))
NT = ((1,), (1,))
TN = ((0,), (0,))


def _dot(a, b, dims):
    return lax.dot_general(a, b, (dims, ((), ())), preferred_element_type=F32)


def _resident(shape, imap):
    return pl.BlockSpec(shape, imap, pipeline_mode=pl.Buffered(1))


def _position():
    return lax.axis_index("x"), lax.axis_index("y"), lax.axis_index("c")


class _Gather:
    def __init__(self, shards):
        n = len(shards)
        self.inputs = list(shards)
        self.out_shape = [jax.ShapeDtypeStruct((s.shape[0], NDEV) + s.shape[1:], s.dtype) for s in shards]
        self.scratch = [pltpu.SemaphoreType.DMA((n, 7)), pltpu.SemaphoreType.DMA((n, 7)), pltpu.SemaphoreType.DMA((n,))]

    def _plan(self, src, dst, sems):
        send_sems, recv_sems, local_sems = sems
        n = len(src)
        x, y, c = _position()
        me, sibling = (x, y, c), (x, y, 1 - c)
        chips = [(1 - x, y), (x, 1 - y), (1 - x, 1 - y)]

        def rows(a, p):
            return dst[a].at[:, 4 * p[0] + 2 * p[1] + p[2]]

        def copy(a, k, block, to, from_src=False):
            return pltpu.make_async_remote_copy(
                src_ref=src[a] if from_src else rows(a, block), dst_ref=rows(a, block),
                send_sem=send_sems.at[a, k], recv_sem=recv_sems.at[a, k], device_id=to, device_id_type=MESH)

        mine = [pltpu.make_async_copy(src[a], rows(a, me), local_sems.at[a]) for a in range(n)]
        first = []
        for a in range(n):
            first.append(copy(a, 0, me, sibling, True))
            first += [copy(a, 1 + t, me, (*chip, c), True) for t, chip in enumerate(chips)]
        return n, c, me, sibling, chips, copy, mine, first

    def start(self, src, dst, sems):
        *_, mine, first = self._plan(src, dst, sems)
        for cp in mine + first:
            cp.start()

    def relay(self, src, dst, sems):
        n, c, me, sibling, chips, copy, _, _ = self._plan(src, dst, sems)
        for t, chip in enumerate(chips):
            for a in range(n):
                copy(a, 1 + t, (*chip, c), me).wait_recv()
                copy(a, 4 + t, (*chip, c), sibling).start()

    def finish(self, src, dst, sems):
        n, c, me, sibling, chips, copy, mine, first = self._plan(src, dst, sems)
        for a in range(n):
            copy(a, 0, sibling, me).wait_recv()
            for t, chip in enumerate(chips):
                copy(a, 4 + t, (*chip, 1 - c), me).wait_recv()
        for cp in first + [copy(a, 4 + t, (*chip, c), sibling) for t, chip in enumerate(chips) for a in range(n)]:
            cp.wait_send()
        for cp in mine:
            cp.wait()


class _Exchange:
    def __init__(self, grads, everyone=()):
        self.inputs = list(grads) + list(everyone)
        self.n_blocked = len(grads)
        n = len(self.inputs)
        self.out_shape = [jax.ShapeDtypeStruct(g.shape, g.dtype) for g in grads]
        self.out_shape += [jax.ShapeDtypeStruct((NDEV,) + e.shape, e.dtype) for e in everyone]
        self.scratch = [pltpu.SemaphoreType.DMA((n, 7)), pltpu.SemaphoreType.DMA((n, 7)), pltpu.SemaphoreType.DMA((n,))]

    def _plan(self, src, land, sems):
        send_sems, recv_sems, local_sems = sems
        x, y, c = _position()
        me = 4 * x + 2 * y + c

        def parts(peer_idx):
            return [(s.at[peer_idx] if a < self.n_blocked else s, land[a].at[me]) for a, s in enumerate(src)]

        local = [pltpu.make_async_copy(s, d, local_sems.at[a]) for a, (s, d) in enumerate(parts(me))]
        sent = []
        for rel in range(1, NDEV):
            px = 1 - x if rel & 4 else x
            py = 1 - y if rel & 2 else y
            pc = 1 - c if rel & 1 else c
            for a, (s, d) in enumerate(parts(4 * px + 2 * py + pc)):
                sent.append(pltpu.make_async_remote_copy(
                    src_ref=s, dst_ref=d, send_sem=send_sems.at[a, rel - 1], recv_sem=recv_sems.at[a, rel - 1],
                    device_id=(px, py, pc), device_id_type=MESH))
        return local, sent

    def start(self, src, land, sems):
        local, sent = self._plan(src, land, sems)
        for cp in local + sent:
            cp.start()

    def relay(self, src, land, sems):
        pass

    def finish(self, src, land, sems):
        local, sent = self._plan(src, land, sems)
        for cp in sent:
            cp.wait_recv()
        for cp in sent:
            cp.wait_send()
        for cp in local:
            cp.wait()


def _call(body, *, name, grid, in_specs, out_specs, out_shape, args, scratch_shapes=(), aliases=None, side=None,
          relay_at=1.0):
    sem = ("arbitrary",) * len(grid)
    if side is None:
        outs = pl.pallas_call(
            body, name=name, grid=grid, in_specs=in_specs, out_specs=out_specs, out_shape=out_shape,
            scratch_shapes=list(scratch_shapes), input_output_aliases=aliases or {},
            compiler_params=pltpu.CompilerParams(dimension_semantics=sem, vmem_limit_bytes=VMEM_LIMIT))(*args)
        return outs, []
    ni, no, ns = len(in_specs), len(out_specs), len(scratch_shapes)
    si, so = len(side.inputs), len(side.out_shape)

    def hosted(*refs):
        ins, refs = refs[:ni], refs[ni:]
        sins, refs = refs[:si], refs[si:]
        outs, refs = refs[:no], refs[no:]
        souts, refs = refs[:so], refs[so:]
        scr, sscr = refs[:ns], refs[ns:]
        step = pl.program_id(0)
        for d in range(1, len(grid)):
            step = step * grid[d] + pl.program_id(d)
        last = int(np.prod(grid)) - 1

        @pl.when(step == 0)
        def _():
            side.start(sins, souts, sscr)

        body(*ins, *outs, *scr)

        @pl.when(step == min(int(relay_at * last), last))
        def _():
            side.relay(sins, souts, sscr)

        @pl.when(step == last)
        def _():
            side.finish(sins, souts, sscr)

    outs = pl.pallas_call(
        hosted, name=name, grid=grid, in_specs=list(in_specs) + [ANY] * si, out_specs=list(out_specs) + [ANY] * so,
        out_shape=list(out_shape) + side.out_shape, scratch_shapes=list(scratch_shapes) + side.scratch,
        input_output_aliases=aliases or {},
        compiler_params=pltpu.CompilerParams(dimension_semantics=sem, vmem_limit_bytes=VMEM_LIMIT, has_side_effects=True),
    )(*args, *side.inputs)
    return outs[:no], outs[no:]


def _remote_only(side, name):
    n = len(side.inputs)

    def body(*refs):
        src, dst, sems = refs[:n], refs[n:n + len(side.out_shape)], refs[n + len(side.out_shape):]
        side.start(src, dst, sems)
        side.relay(src, dst, sems)
        side.finish(src, dst, sems)

    return pl.pallas_call(
        body, name=name, in_specs=[ANY] * n, out_specs=[ANY] * len(side.out_shape), out_shape=side.out_shape,
        scratch_shapes=side.scratch, compiler_params=pltpu.CompilerParams(has_side_effects=True))(*side.inputs)


def _rms(x, g):
    r = lax.rsqrt(jnp.mean(x * x, axis=-1, keepdims=True) + EPS)
    return x * r * g


def _rms_bwd(dy, x, g):
    r = lax.rsqrt(jnp.mean(x * x, axis=-1, keepdims=True) + EPS)
    xh = x * r
    dxh = dy * g
    dx = r * (dxh - xh * jnp.mean(dxh * xh, axis=-1, keepdims=True))
    return dx, jnp.sum(dy * xh, axis=0, keepdims=True)


def _zero_at_first_step(acc_ref):
    first = pl.program_id(0) == 0

    @pl.when(first)
    def _():
        acc_ref[...] = jnp.zeros_like(acc_ref)


ROW1 = lambda: _resident((1, D), lambda i: (0, 0))
LROWS = lambda l, n=1: _resident((None, n, D), lambda *_: (l, 0, 0))
ACC = lambda: pl.BlockSpec((8, D), lambda i: (0, 0))


def _inproj_fwd(x, g, wint, l, tm, side=None, relay_at=1.0):
    T = x.shape[0]

    def body(x_ref, g_ref, w_ref, gt_ref, q_ref, k_ref, v_ref, h_ref):
        h = _rms(x_ref[...], g_ref[...]).astype(BF)
        h_ref[...] = h
        q_ref[...] = _dot(w_ref[0:NQ, :], h, NT).astype(BF)
        k_ref[...] = _dot(w_ref[NQ:NQ + NKV, :], h, NT).astype(BF)
        v_ref[...] = _dot(w_ref[NQ + NKV:NQ + 2 * NKV, :], h, NT).astype(BF)
        for s in range(5):
            lo = NQ + 2 * NKV + s * D
            gt_ref[:, s * D:(s + 1) * D] = _dot(h, w_ref[lo:lo + D, :], NT).astype(BF)

    tok = lambda w: pl.BlockSpec((tm, w), lambda i: (i, 0))
    feat = lambda w: pl.BlockSpec((w, tm), lambda i: (0, i))
    return _call(
        body, name=f"inproj_fwd{l}", grid=(T // tm,),
        in_specs=[tok(D), LROWS(l), _resident((NP, D), lambda i: (0, 0))],
        out_specs=[tok(NG), feat(NQ), feat(NKV), feat(NKV), tok(D)],
        out_shape=[jax.ShapeDtypeStruct((T, NG), BF)] + [jax.ShapeDtypeStruct((w, T), BF) for w in (NQ, NKV, NKV)]
        + [jax.ShapeDtypeStruct((T, D), BF)],
        args=(x, g, wint), side=side, relay_at=relay_at)


def _band_geometry():
    j = lax.broadcasted_iota(jnp.int32, (2 * WIN, WIN), 0)
    r = lax.broadcasted_iota(jnp.int32, (2 * WIN, WIN), 1)
    dist = WIN + r - j
    dist0 = r - j
    return (dist.astype(F32), (dist >= 0) & (dist < WIN)), (dist0.astype(F32), dist0 >= 0)


def _pair_biases(sm_ref, pj):
    return [[jnp.where(ok, -sm_ref[1, pj * 2 * GQ + h] * dist, NEG) for h in range(2 * GQ)]
            for dist, ok in _band_geometry()]


def _reduce_rows(x, pair, whole):
    while x.shape[0] > 8:
        half = x.shape[0] // 2
        x = pair(x[:half], x[half:])
    return whole(x, axis=0, keepdims=True)


def _heads_on_lanes(ref, kvh, r0):
    return jnp.concatenate([ref[(kvh * GQ + g) * HD:(kvh * GQ + g + 1) * HD, pl.ds(r0, WIN)] for g in range(GQ)], axis=1)


def _band_probs(sm_ref, head0, q_ref, k_ref, r0, p0, kvh, biases):
    qt = _heads_on_lanes(q_ref, kvh, r0) * jnp.asarray(HD ** -0.5, BF)
    kt = k_ref[kvh * HD:(kvh + 1) * HD, pl.ds(p0, 2 * WIN)]
    st = _dot(kt, qt, TN)
    heads = []
    for g in range(GQ):
        sink = sm_ref[0, head0 + g]
        s = st[:, g * WIN:(g + 1) * WIN] + biases[kvh * GQ + g]
        m = jnp.maximum(_reduce_rows(s, jnp.maximum, jnp.max), sink)
        p = jnp.exp(s - m)
        ps = jnp.exp(sink - m)
        heads.append((p, ps, 1.0 / (_reduce_rows(p, jnp.add, jnp.sum) + ps)))
    return qt, kt, heads


def _attn_fwd(q, k, v, sm, l, S, side=None, relay_at=1.0):
    T = q.shape[1]
    nblk = S // WIN
    unroll = next(u for u in (5, 3, 1) if (nblk - 1) % u == 0)

    def body(sm_all_ref, q_ref, k_ref, v_ref, o_ref):
        sm_ref = sm_all_ref.at[l]
        pj = pl.program_id(1)
        biases = _pair_biases(sm_ref, pj)

        def block(i, first):
            r0 = 0 if first else pl.multiple_of(i * WIN, WIN)
            p0 = 0 if first else pl.multiple_of(i * WIN - WIN, WIN)
            for kvh in range(2):
                head0 = (2 * pj + kvh) * GQ
                _, _, heads = _band_probs(sm_ref, head0, q_ref, k_ref, r0, p0, kvh, biases[first])
                pt = jnp.concatenate([p.astype(BF) for p, _, _ in heads], axis=1)
                ot = _dot(v_ref[kvh * HD:(kvh + 1) * HD, pl.ds(p0, 2 * WIN)], pt, NN)
                for g in range(GQ):
                    rows = slice((kvh * GQ + g) * HD, (kvh * GQ + g + 1) * HD)
                    o_ref[rows, pl.ds(r0, WIN)] = (ot[:, g * WIN:(g + 1) * WIN] * heads[g][2]).astype(BF)

        block(0, True)

        def rest(t, c):
            for u in range(unroll):
                block(1 + unroll * t + u, False)
            return c

        lax.fori_loop(0, (nblk - 1) // unroll, rest, 0)

    wide = lambda: pl.BlockSpec((2 * GQ * HD, S), lambda s, p: (p, s))
    narrow = lambda: pl.BlockSpec((2 * HD, S), lambda s, p: (p, s))
    return _call(
        body, name=f"attn_fwd{l}", grid=(T // S, 2),
        in_specs=[pl.BlockSpec(memory_space=pltpu.SMEM), wide(), narrow(), narrow()],
        out_specs=[wide()], out_shape=[jax.ShapeDtypeStruct((D, T), BF)],
        args=(sm, q, k, v), side=side, relay_at=relay_at)


def _shift_rows(y, k, edge_rows, down):
    n = y.shape[0]
    rid = lax.broadcasted_iota(jnp.int32, y.shape, 0)
    out = pltpu.roll(y, k if down else n - k, 0)
    for t, row in enumerate(edge_rows):
        out = jnp.where(rid == (t if down else n - k + t), row, out)
    return out


def _mixer_fwd(x, gt, att, w3, bg, cw, cbias, l, S, tm):
    T = x.shape[0]

    def body(x_ref, cb_ref, cc_ref, cu_ref, ga_ref, gc_ref, cch_ref, cuh_ref, att_ref, wao_ref, wco_ref, wo_ref,
             bg_ref, cw_ref, cbias_ref, x1_ref, mg_ref, co_ref, ya_ref, yc_ref, zb_ref):
        first = (pl.program_id(0) * tm) % S == 0
        y = cc_ref[...].astype(F32) * cu_ref[...].astype(F32)
        hy1 = cch_ref[15:16, :].astype(F32) * cuh_ref[15:16, :].astype(F32)
        hy2 = cch_ref[14:15, :].astype(F32) * cuh_ref[14:15, :].astype(F32)
        hy1, hy2 = jnp.where(first, 0.0, hy1), jnp.where(first, 0.0, hy2)
        z = (cw_ref[0:1, :] * _shift_rows(y, 2, [hy2, hy1], True) + cw_ref[1:2, :] * _shift_rows(y, 1, [hy1], True)
             + cw_ref[2:3, :] * y)
        zb = z + cbias_ref[...]
        zb_ref[...] = zb.astype(BF)
        co = (cb_ref[...].astype(F32) * zb).astype(BF)
        co_ref[...] = co
        yc = _dot(co, wco_ref[...], NN)
        ya = _dot(att_ref[...], wao_ref[...], TN)
        ya_ref[...] = ya.astype(BF)
        yc_ref[...] = yc.astype(BF)
        sa = jax.nn.sigmoid(ga_ref[...].astype(F32) + bg_ref[0:1, :])
        sc = jax.nn.sigmoid(gc_ref[...].astype(F32) + bg_ref[1:2, :])
        mg = (sa * ya + sc * yc).astype(BF)
        mg_ref[...] = mg
        x1_ref[...] = x_ref[...] + _dot(mg, wo_ref[...], NN)

    tok = lambda: pl.BlockSpec((tm, D), lambda i: (i, 0))
    seg = lambda s: pl.BlockSpec((tm, D), lambda i: (i, s))
    halo = lambda s: pl.BlockSpec((16, D), lambda i: (jnp.maximum(i * (tm // 16) - 1, 0), s))
    wsp = lambda k: _resident((None, D, D), lambda i: (k, 0, 0))
    return _call(
        body, name=f"mixer_fwd{l}", grid=(T // tm,),
        in_specs=[tok(), seg(0), seg(1), seg(2), seg(3), seg(4), halo(1), halo(2),
                  pl.BlockSpec((D, tm), lambda i: (0, i)), wsp(0), wsp(1), wsp(2), LROWS(l, 2), LROWS(l, 8), LROWS(l)],
        out_specs=[tok()] * 6,
        out_shape=[jax.ShapeDtypeStruct((T, D), dt) for dt in (F32, BF, BF, BF, BF, BF)],
        args=(x, gt, gt, gt, gt, gt, gt, gt, att, w3, w3, w3, bg, cw, cbias))


def _mlp_fwd(x1, g, w2, l, tm, head=None, side=None, relay_at=1.0):
    T = x1.shape[0]
    nt = T // tm
    FC = 1024

    def body(x_ref, g_ref, wup_ref, wdn_ref, *rest):
        out_ref, h_ref, a_ref = rest[-4:-1] if head else rest
        x = x_ref[...]
        h = _rms(x, g_ref[...]).astype(BF)
        h_ref[...] = h
        acc = x
        for c in range(F // FC):
            a = _dot(h, wup_ref[c * FC:(c + 1) * FC, :], NT)
            a_ref[:, c * FC:(c + 1) * FC] = a.astype(BF)
            u = jnp.maximum(a, 0.0)
            acc = acc + _dot((u * u).astype(BF), wdn_ref[c * FC:(c + 1) * FC, :], NN)
        if not head:
            out_ref[...] = acc
            return
        t_ref, gf_ref, acc_ref = rest[0], rest[1], rest[-1]
        _zero_at_first_step(acc_ref)
        gf = gf_ref[...]
        err = _rms(acc, gf) - t_ref[...]
        out_ref[...], dg = _rms_bwd(err * (1.0 / D), acc, gf)
        acc_ref[0:1, :] += dg
        acc_ref[1:2, :] += jnp.sum(err * err, axis=0, keepdims=True)

        @pl.when(pl.program_id(0) == nt - 1)
        def _():
            acc_ref[1:2, :] = jnp.zeros((1, D), F32) + (0.5 / D) * jnp.sum(acc_ref[1:2, :])

    tok = lambda w: pl.BlockSpec((tm, w), lambda i: (i, 0))
    wsp = lambda k: _resident((None, F, D), lambda i: (k, 0, 0))
    return _call(
        body, name=f"mlp_fwd{l}", grid=(nt,),
        in_specs=[tok(D), LROWS(l), wsp(0), wsp(1)] + ([tok(D), ROW1()] if head else []),
        out_specs=[tok(D), tok(D), tok(F)] + ([ACC()] if head else []),
        out_shape=[jax.ShapeDtypeStruct((T, D), F32), jax.ShapeDtypeStruct((T, D), BF), jax.ShapeDtypeStruct((T, F), BF)]
        + ([jax.ShapeDtypeStruct((8, D), F32)] if head else []),
        args=(x1, g, w2, w2) + (tuple(head) if head else ()), side=side, relay_at=relay_at)


def _mlp_bwd(dx2, x1, a, g, w2, l, tm, side=None):
    T = dx2.shape[0]
    FC = 1024

    def body(d_ref, x_ref, a_ref, g_ref, wup_ref, wdn_ref, da_ref, dx1_ref, db_ref, acc_ref):
        _zero_at_first_step(acc_ref)
        d = d_ref[...]
        db = d.astype(BF)
        db_ref[...] = db
        dh = jnp.zeros((tm, D), F32)
        for c in range(F // FC):
            du = _dot(db, wdn_ref[c * FC:(c + 1) * FC, :], NT)
            da = (du * (2.0 * jnp.maximum(a_ref[:, c * FC:(c + 1) * FC].astype(F32), 0.0))).astype(BF)
            da_ref[:, c * FC:(c + 1) * FC] = da
            dh = dh + _dot(da, wup_ref[c * FC:(c + 1) * FC, :], NN)
        dx, dg = _rms_bwd(dh, x_ref[...], g_ref[...])
        dx1_ref[...] = d + dx
        acc_ref[0:1, :] += dg

    tok = lambda w: pl.BlockSpec((tm, w), lambda i: (i, 0))
    wsp = lambda k: _resident((None, F, D), lambda i: (k, 0, 0))
    return _call(
        body, name=f"mlp_bwd{l}", grid=(T // tm,),
        in_specs=[tok(D), tok(D), tok(F), LROWS(l), wsp(0), wsp(1)],
        out_specs=[tok(F), tok(D), tok(D), ACC()],
        out_shape=[jax.ShapeDtypeStruct((T, F), BF), jax.ShapeDtypeStruct((T, D), F32),
                   jax.ShapeDtypeStruct((T, D), BF), jax.ShapeDtypeStruct((8, D), F32)],
        args=(dx2, x1, a, g, w2, w2), side=side)


def _mixer_bwd(dx1, gt, ya, yc, zb, w3, bg, cw, l, S, tm, side=None):
    T = dx1.shape[0]
    nt = T // tm

    def body(d_ref, cb_ref, cc_ref, cu_ref, ga_ref, gc_ref, ya_ref, yc_ref, zb_ref, wao_ref, wco_ref, wo_ref,
             bg_ref, cw_ref, dg_ref, datt_ref, dya_ref, dyc_ref, db_ref, acc_ref, carry_ref):
        ti = nt - 1 - pl.program_id(0)
        _zero_at_first_step(acc_ref)

        @pl.when(((ti + 1) * tm) % S == 0)
        def _():
            carry_ref[...] = jnp.zeros_like(carry_ref)

        db = d_ref[...].astype(BF)
        db_ref[...] = db
        dm = _dot(db, wo_ref[...], NT)
        sa = jax.nn.sigmoid(ga_ref[...].astype(F32) + bg_ref[0:1, :])
        sc = jax.nn.sigmoid(gc_ref[...].astype(F32) + bg_ref[1:2, :])
        dya32 = dm * sa
        dyc32 = dm * sc
        dya = dya32.astype(BF)
        dyc = dyc32.astype(BF)
        dya_ref[...] = dya
        dyc_ref[...] = dyc
        dga = dya32 * ya_ref[...].astype(F32) * (1.0 - sa)
        dgc = dyc32 * yc_ref[...].astype(F32) * (1.0 - sc)
        dg_ref[:, 3 * D:4 * D] = dga.astype(BF)
        dg_ref[:, 4 * D:5 * D] = dgc.astype(BF)
        acc_ref[0:1, :] += jnp.sum(dga, axis=0, keepdims=True)
        acc_ref[1:2, :] += jnp.sum(dgc, axis=0, keepdims=True)
        datt_ref[...] = _dot(wao_ref[...], dya, NT).astype(BF)
        dco = _dot(dyc, wco_ref[...], NT)

        cc = cc_ref[...].astype(F32)
        cu = cu_ref[...].astype(F32)
        y = cc * cu
        dg_ref[:, 0:D] = (dco * zb_ref[...].astype(F32)).astype(BF)
        dz = dco * cb_ref[...].astype(F32)
        u1 = _shift_rows(dz, 1, [carry_ref[0:1, :]], False)
        u2 = _shift_rows(dz, 2, [carry_ref[0:1, :], carry_ref[1:2, :]], False)
        acc_ref[2:3, :] += jnp.sum(dz, axis=0, keepdims=True)
        acc_ref[3:4, :] += jnp.sum(u2 * y, axis=0, keepdims=True)
        acc_ref[4:5, :] += jnp.sum(u1 * y, axis=0, keepdims=True)
        acc_ref[5:6, :] += jnp.sum(dz * y, axis=0, keepdims=True)
        dy = cw_ref[2:3, :] * dz + cw_ref[1:2, :] * u1 + cw_ref[0:1, :] * u2
        dg_ref[:, D:2 * D] = (dy * cu).astype(BF)
        dg_ref[:, 2 * D:3 * D] = (dy * cc).astype(BF)
        carry_ref[...] = dz[0:8, :]

    tok = lambda w=D: pl.BlockSpec((tm, w), lambda i: (nt - 1 - i, 0))
    seg = lambda s: pl.BlockSpec((tm, D), lambda i: (nt - 1 - i, s))
    wsp = lambda k: _resident((None, D, D), lambda i: (k, 0, 0))
    return _call(
        body, name=f"mixer_bwd{l}", grid=(nt,),
        in_specs=[tok(), seg(0), seg(1), seg(2), seg(3), seg(4), tok(), tok(), tok(), wsp(0), wsp(1), wsp(2),
                  LROWS(l, 2), LROWS(l, 8)],
        out_specs=[tok(NG), pl.BlockSpec((D, tm), lambda i: (0, nt - 1 - i)), tok(), tok(), tok(), ACC()],
        out_shape=[jax.ShapeDtypeStruct((T, NG), BF), jax.ShapeDtypeStruct((D, T), BF)]
        + [jax.ShapeDtypeStruct((T, D), BF)] * 3 + [jax.ShapeDtypeStruct((8, D), F32)],
        scratch_shapes=[pltpu.VMEM((8, D), F32)],
        args=(dx1, gt, gt, gt, gt, gt, ya, yc, zb, w3, w3, w3, bg, cw), side=side)


def _attn_bwd(q, k, v, att, datt, sm, l, S, side=None):
    T = q.shape[1]
    nblk = S // WIN
    unroll = next(u for u in (5, 3, 1) if (nblk - 1) % u == 0)
    scale = HD ** -0.5

    def body(sm_all_ref, q_ref, k_ref, v_ref, o_ref, do_ref, dq_ref, dk_ref, dv_ref, ds_ref, dka_ref, dva_ref):
        sm_ref = sm_all_ref.at[l]
        pj = pl.program_id(1)
        biases = _pair_biases(sm_ref, pj)
        dka_ref[...] = jnp.zeros_like(dka_ref)
        dva_ref[...] = jnp.zeros_like(dva_ref)

        def block(i, first, dsinks):
            r0 = 0 if first else pl.multiple_of(i * WIN, WIN)
            p0 = 0 if first else pl.multiple_of(i * WIN - WIN, WIN)
            out = []
            for kvh in range(2):
                head0 = (2 * pj + kvh) * GQ
                rows = slice(kvh * HD, (kvh + 1) * HD)
                dob = _heads_on_lanes(do_ref, kvh, r0)
                dpt = _dot(v_ref[rows, pl.ds(p0, 2 * WIN)], dob, TN)
                qt, kt, heads = _band_probs(sm_ref, head0, q_ref, k_ref, r0, p0, kvh, biases[first])
                inv = jnp.concatenate([h[2] for h in heads], axis=1)
                do32 = dob.astype(F32)
                delta = jnp.sum(do32 * _heads_on_lanes(o_ref, kvh, r0).astype(F32), axis=0, keepdims=True) * inv
                dosb = (do32 * inv).astype(BF)
                dst = jnp.concatenate(
                    [(p * (dpt[:, g * WIN:(g + 1) * WIN] * h_inv - delta[:, g * WIN:(g + 1) * WIN])).astype(BF)
                     for g, (p, _, h_inv) in enumerate(heads)], axis=1)
                pt = jnp.concatenate([p.astype(BF) for p, _, _ in heads], axis=1)
                dqt = _dot(kt, dst, NN) * scale
                for g in range(GQ):
                    hr = slice((kvh * GQ + g) * HD, (kvh * GQ + g + 1) * HD)
                    dq_ref[hr, pl.ds(r0, WIN)] = dqt[:, g * WIN:(g + 1) * WIN].astype(BF)
                dka_ref[rows, pl.ds(p0, 2 * WIN)] += _dot(qt, dst, NT)
                dva_ref[rows, pl.ds(p0, 2 * WIN)] += _dot(dosb, pt, NT)
                ps = jnp.concatenate([h[1] for h in heads], axis=1)
                out.append(dsinks[kvh] - ps * delta)
            return tuple(out)

        zero = jnp.zeros((1, GQ * WIN), F32)
        def rest(t, c):
            for u in range(unroll):
                c = block(1 + unroll * t + u, False, c)
            return c

        dsinks = lax.fori_loop(0, (nblk - 1) // unroll, rest, block(0, True, (zero, zero)))
        for kvh in range(2):
            for g in range(GQ):
                tot = jnp.sum(dsinks[kvh][:, g * WIN:(g + 1) * WIN])
                ds_ref[kvh * GQ + g:kvh * GQ + g + 1, :] = jnp.zeros((1, 128), F32) + tot
        dk_ref[...] = dka_ref[...].astype(BF)
        dv_ref[...] = dva_ref[...].astype(BF)

    wide = lambda: pl.BlockSpec((2 * GQ * HD, S), lambda s, p: (p, s))
    narrow = lambda: pl.BlockSpec((2 * HD, S), lambda s, p: (p, s))
    return _call(
        body, name=f"attn_bwd{l}", grid=(T // S, 2),
        in_specs=[pl.BlockSpec(memory_space=pltpu.SMEM), wide(), narrow(), narrow(), wide(), wide()],
        out_specs=[wide(), narrow(), narrow(), pl.BlockSpec((None, None, 8, 128), lambda s, p: (s, p, 0, 0))],
        out_shape=[jax.ShapeDtypeStruct((NQ, T), BF), jax.ShapeDtypeStruct((NKV, T), BF),
                   jax.ShapeDtypeStruct((NKV, T), BF), jax.ShapeDtypeStruct((T // S, 2, 8, 128), F32)],
        scratch_shapes=[pltpu.VMEM((2 * HD, S), F32), pltpu.VMEM((2 * HD, S), F32)],
        args=(sm, q, k, v, att, datt), side=side)


def _inproj_bwd(dgt, dq, dk, dv, x, dres, g, wint, l, tm, side=None):
    T = x.shape[0]

    def body(dg_ref, dq_ref, dk_ref, dv_ref, x_ref, dr_ref, g_ref, w_ref, dx_ref, acc_ref):
        _zero_at_first_step(acc_ref)
        dh = _dot(dq_ref[...], w_ref[0:NQ, :], TN)
        dh = dh + _dot(dk_ref[...], w_ref[NQ:NQ + NKV, :], TN)
        dh = dh + _dot(dv_ref[...], w_ref[NQ + NKV:NQ + 2 * NKV, :], TN)
        dh = dh + _dot(dg_ref[...], w_ref[NQ + 2 * NKV:NP, :], NN)
        dx, dg = _rms_bwd(dh, x_ref[...], g_ref[...])
        dx_ref[...] = dr_ref[...] + dx
        acc_ref[0:1, :] += dg

    tok = lambda w: pl.BlockSpec((tm, w), lambda i: (i, 0))
    feat = lambda w: pl.BlockSpec((w, tm), lambda i: (0, i))
    return _call(
        body, name=f"inproj_bwd{l}", grid=(T // tm,),
        in_specs=[tok(NG), feat(NQ), feat(NKV), feat(NKV), tok(D), tok(D), LROWS(l), _resident((NP, D), lambda i: (0, 0))],
        out_specs=[tok(D), ACC()],
        out_shape=[jax.ShapeDtypeStruct((T, D), F32), jax.ShapeDtypeStruct((8, D), F32)],
        args=(dgt, dq, dk, dv, x, dres, g, wint), side=side)


def _wgrad(a, b, rows, row0, into, name, relu2=False, a_is_transposed=False):
    M, T = a.shape if a_is_transposed else a.shape[::-1]
    tmm = next(t for t in (1024, 512, 256) if M % t == 0 and row0 % t == 0)
    tk = min(4096 if M > 4096 else 2048, T)
    nk = T // tk
    blk0 = row0 // tmm

    def body(*refs):
        a_ref, b_ref = refs[0], refs[1]
        o_ref, acc_ref = refs[-2], refs[-1]
        kk = pl.program_id(1)

        @pl.when(kk == 0)
        def _():
            acc_ref[...] = jnp.zeros_like(acc_ref)

        av = a_ref[...]
        if relu2:
            t = jnp.maximum(av.astype(F32), 0.0)
            av = (t * t).astype(BF)
        acc_ref[...] += _dot(av, b_ref[...], NN if a_is_transposed else TN)

        @pl.when(kk == nk - 1)
        def _():
            o_ref[...] = acc_ref[...].astype(BF)

    a_spec = pl.BlockSpec((tmm, tk), lambda j, kk: (j, kk)) if a_is_transposed else pl.BlockSpec((tk, tmm), lambda j, kk: (kk, j))
    in_specs = [a_spec, pl.BlockSpec((tk, D), lambda j, kk: (kk, 0))]
    args = [a, b]
    if into is not None:
        in_specs.append(ANY)
        args.append(into)
    (out,), _ = _call(
        body, name=name, grid=(M // tmm, nk), in_specs=in_specs,
        out_specs=[pl.BlockSpec((tmm, D), lambda j, kk: (blk0 + j, 0))],
        out_shape=[jax.ShapeDtypeStruct((rows, D), BF)], scratch_shapes=[pltpu.VMEM((tmm, D), F32)],
        aliases={2: 0} if into is not None else None, args=args)
    return out


def _wgrad_square(jobs, name):
    n = len(jobs)
    T = jobs[0][1].shape[0]
    tk = min(1024, T)
    nk = T // tk

    def body(*refs):
        ins, outs, acc_ref = refs[:2 * n], refs[2 * n:3 * n], refs[3 * n]
        j, kk = pl.program_id(0), pl.program_id(1)

        @pl.when(kk == 0)
        def _():
            acc_ref[...] = jnp.zeros_like(acc_ref)

        for p, (_, _, a_t) in enumerate(jobs):
            @pl.when(j == p)
            def _(p=p, a_t=a_t):
                acc_ref[...] += _dot(ins[2 * p][...], ins[2 * p + 1][...], NN if a_t else TN)

                @pl.when(kk == nk - 1)
                def _():
                    outs[p][...] = acc_ref[...].astype(BF)

    def step(p):
        return lambda j, kk: jnp.where(j == p, kk, jnp.where(j > p, nk - 1, 0))

    in_specs, args = [], []
    for p, (a, b, a_t) in enumerate(jobs):
        at = step(p)
        in_specs.append(pl.BlockSpec((D, tk), lambda j, kk, at=at: (0, at(j, kk))) if a_t
                        else pl.BlockSpec((tk, D), lambda j, kk, at=at: (at(j, kk), 0)))
        in_specs.append(pl.BlockSpec((tk, D), lambda j, kk, at=at: (at(j, kk), 0)))
        args += [a, b]
    outs, _ = _call(
        body, name=name, grid=(n, nk), in_specs=in_specs,
        out_specs=[pl.BlockSpec((D, D), lambda j, kk: (0, 0))] * n,
        out_shape=[jax.ShapeDtypeStruct((D, D), BF)] * n, scratch_shapes=[pltpu.VMEM((D, D), F32)], args=args)
    return outs


def _adamw(w, g, m, v):
    m = B1 * m + (1.0 - B1) * g
    v = B2 * v + (1.0 - B2) * (g * g)
    m_hat = m / (1.0 - B1 ** STEP)
    v_hat = v / (1.0 - B2 ** STEP)
    return -LR * (m_hat / (jnp.sqrt(v_hat) + AEPS) + WD * w), m, v


def _adam_sum(land, w, m, v, l, into, name, transposed=False):
    _, r, _ = land.shape
    tr = 208 if r % 208 == 0 else (256 if r % 256 == 0 else r)
    tc = 256

    def body(land_ref, w_ref, m_ref, v_ref, *rest):
        g_ref, d_ref, nm_ref, nv_ref = rest[-4:]
        g = land_ref[0].astype(F32)
        for s in range(1, NDEV):
            g = g + land_ref[s].astype(F32)
        if transposed:
            g = g.T
        g_ref[...] = g
        d_ref[...], nm_ref[...], nv_ref[...] = _adamw(w_ref[...], g, m_ref[...], v_ref[...])

    if transposed:
        blk = lambda: pl.BlockSpec((None, tc, r), lambda j: (l, j, 0))
        land_spec, grid = pl.BlockSpec((NDEV, r, tc), lambda j: (0, 0, j)), (D // tc,)
    else:
        blk = lambda: pl.BlockSpec((None, tr, D), lambda j: (l, j, 0))
        land_spec, grid = pl.BlockSpec((NDEV, tr, D), lambda j: (0, j, 0)), (r // tr,)
    in_specs = [land_spec, blk(), blk(), blk()]
    args = [land, w, m, v]
    aliases = None
    if into is not None:
        in_specs += [ANY] * 4
        args += list(into)
        aliases = {4 + t: t for t in range(4)}
    outs, _ = _call(body, name=name, grid=grid, in_specs=in_specs, out_specs=[blk()] * 4,
                    out_shape=[jax.ShapeDtypeStruct(w.shape, F32)] * 4, aliases=aliases, args=args)
    return outs


SMALL_NAMES = ("g_mix", "b_gates", "sinks", "conv_b", "g_mlp", "g_final", "conv_w")


def _adam_small(land, me, masters):
    n = len(SMALL_NAMES)
    lanes = D // NDEV

    def body(land_ref, me_ref, *refs):
        ins, outs, loss_ref, gs_ref = refs[:3 * n], refs[3 * n:7 * n], refs[7 * n], refs[7 * n + 1]
        g = land_ref[0]
        for s in range(1, NDEV):
            g = g + land_ref[s]
        gs_ref[...] = g

        def update(k, g_piece, idx):
            w_ref, m_ref, v_ref = ins[3 * k:3 * k + 3]
            outs[4 * k][idx] = g_piece
            outs[4 * k + 1][idx], outs[4 * k + 2][idx], outs[4 * k + 3][idx] = _adamw(w_ref[idx], g_piece, m_ref[idx], v_ref[idx])

        whole = (slice(None), slice(None))
        update(0, gs_ref[0:2, :], whole)
        for l in range(L):
            for h in range(2):
                update(1, gs_ref[2 + 2 * l + h:3 + 2 * l + h, :], (slice(l, l + 1), slice(h * D, (h + 1) * D)))
            update(2, gs_ref[6:7, 16 * l:16 * (l + 1)], (slice(l, l + 1), slice(None)))
        update(3, gs_ref[7:9, :], whole)
        update(4, gs_ref[9:11, :], whole)
        update(5, gs_ref[11:12, :], whole)
        mine = pl.ds(pl.multiple_of(me_ref[0] * lanes, lanes), lanes)
        for l in range(L):
            for k in range(3):
                update(6, gs_ref[12 + 3 * l + k:13 + 3 * l + k, mine], (l, slice(k, k + 1), slice(None)))
        loss_ref[...] = gs_ref[18:19, 0:1]

    flat = [t for name in SMALL_NAMES for t in masters[name]]
    vmem = pl.BlockSpec(memory_space=pltpu.VMEM)
    outs = pl.pallas_call(
        body, name="adam_small",
        in_specs=[vmem, pl.BlockSpec(memory_space=pltpu.SMEM)] + [vmem] * len(flat),
        out_shape=[jax.ShapeDtypeStruct(masters[name][0].shape, F32) for name in SMALL_NAMES for _ in range(4)]
        + [jax.ShapeDtypeStruct((1, 1), F32)],
        scratch_shapes=[pltpu.VMEM((SMALL_ROWS, D), F32)],
        compiler_params=pltpu.CompilerParams(vmem_limit_bytes=VMEM_LIMIT))(land, me, *flat)
    return {name: outs[4 * k:4 * k + 4] for k, name in enumerate(SMALL_NAMES)}, outs[-1]


def _pack_small(g_mix, b_gates, sinks, conv_b, g_mlp, g_final, conv_w_rows, extra):
    sink_row = jnp.zeros((1, D), F32).at[0, :2 * 16].set(sinks.reshape(-1))
    return jnp.concatenate([g_mix, b_gates.reshape(4, D), sink_row, conv_b, g_mlp, g_final.reshape(1, D),
                            conv_w_rows, extra, jnp.zeros((SMALL_ROWS - 19, D), F32)], axis=0)


def kernel(x, g_mix, w_in, b_gates, sinks, w_attn_out, conv_w, conv_b, w_conv_out, w_o, g_mlp, w_up, w_down, g_final, loss_target, m_g_mix, m_w_in, m_b_gates, m_sinks, m_w_attn_out, m_conv_w, m_conv_b, m_w_conv_out, m_w_o, m_g_mlp, m_w_up, m_w_down, m_g_final, v_g_mix, v_w_in, v_b_gates, v_sinks, v_w_attn_out, v_conv_w, v_conv_b, v_w_conv_out, v_w_o, v_g_mlp, v_w_up, v_w_down, v_g_final):
    nseq, S, _ = x.shape
    T = nseq * S
    tm_in = min(512, S)
    tm = min(512, S)
    xi, yi, ci = _position()
    me = 4 * xi + 2 * yi + ci
    tr = lambda t: jnp.swapaxes(t, 1, 2)
    blocks = lambda t: t.reshape(NDEV, t.shape[0] // NDEV, D)

    win_t, wup_t = tr(w_in), tr(w_up)
    sh_win = [win_t[l].astype(BF)[None] for l in range(L)]
    sh_w3 = [jnp.stack([w_attn_out[l], w_conv_out[l], w_o[l]]).astype(BF) for l in range(L)]
    sh_w2 = [jnp.stack([wup_t[l], w_down[l]]).astype(BF) for l in range(L)]
    wint, w3, w2 = [None] * L, [None] * L, [None] * L
    wint0_g, cw_g = _remote_only(_Gather([sh_win[0], jnp.pad(conv_w, ((0, 0), (0, 5), (0, 0)))]), "gather_first")
    wint[0] = wint0_g.reshape(NP, D)
    cw = jnp.swapaxes(cw_g, 1, 2).reshape(L, 8, D)
    slopes = np.power(np.float32(2.0), -8.0 * np.arange(1, 17, dtype=np.float32) / 16).astype(np.float32)
    sm = jnp.stack([sinks, jnp.broadcast_to(jnp.asarray(slopes), sinks.shape)], axis=1)
    per_layer = lambda t, n=1: t.reshape(L, n, D)
    g_mix3, g_mlp3, conv_b3, bg = per_layer(g_mix), per_layer(g_mlp), per_layer(conv_b), per_layer(b_gates, 2)

    xf = x.reshape(T, D)
    saved = []
    cur = xf
    for l in range(L):
        (gt, q, k, v, h), got = _inproj_fwd(cur, g_mix3, wint[l], l, tm_in, side=_Gather([sh_w2[l]]),
                                            relay_at=0.85)
        w2[l] = got[0].reshape(2, F, D)
        (att,), got = _attn_fwd(q, k, v, sm, l, S, side=_Gather([sh_w3[l]]), relay_at=0.5)
        w3[l] = got[0].reshape(3, D, D)
        (x1, mg, co, ya, yc, zb), _ = _mixer_fwd(cur, gt, att, w3[l], bg, cw, conv_b3, l, S, tm)
        head = (loss_target.reshape(T, D), g_final.reshape(1, D)) if l == L - 1 else None
        (nxt, h2, a, *acc_loss), got = _mlp_fwd(x1, g_mlp3, w2[l], l, tm_in, head=head,
                                                side=_Gather([sh_win[l + 1]]) if l + 1 < L else None, relay_at=0.7)
        if l + 1 < L:
            wint[l + 1] = got[0].reshape(NP, D)
        saved.append(dict(x=cur, gt=gt, q=q, k=k, v=v, h=h, att=att, x1=x1, mg=mg, co=co, ya=ya, yc=yc, zb=zb, h2=h2, a=a))
        cur = nxt
    dcur, acc_loss = cur, acc_loss[0]

    masters = {"w_in": (win_t, tr(m_w_in), tr(v_w_in)), "w_attn_out": (w_attn_out, m_w_attn_out, v_w_attn_out),
               "w_conv_out": (w_conv_out, m_w_conv_out, v_w_conv_out), "w_o": (w_o, m_w_o, v_w_o),
               "w_up": (w_up, m_w_up, v_w_up), "w_down": (w_down, m_w_down, v_w_down)}
    big = {name: None for name in masters}

    def adam(name, land, l):
        big[name] = _adam_sum(land, *masters[name], l, big[name], f"adam_{name}{l}", transposed=name == "w_up")

    acc_in, acc_mix, acc_mlp, dsink = [None] * L, [None] * L, [None] * L, [None] * L
    d_win_above = None
    for l in reversed(range(L)):
        sv = saved[l]
        side = _Exchange([blocks(d_win_above)]) if l + 1 < L else None
        (da, dx1, dx2b, acc_mlp[l]), got = _mlp_bwd(dcur, sv["x1"], sv["a"], g_mlp3, w2[l], l, tm_in, side=side)
        if l + 1 < L:
            adam("w_in", got[0], l + 1)
        d_wdn = _wgrad(sv["a"], dx2b, F, 0, None, f"wgrad_down{l}", relu2=True)
        d_wup = _wgrad(da, sv["h2"], F, 0, None, f"wgrad_up{l}")
        (dgt, datt, dya, dyc, dx1b, acc_mix[l]), got = _mixer_bwd(
            dx1, sv["gt"], sv["ya"], sv["yc"], sv["zb"], w3[l], bg, cw, l, S, tm,
            side=_Exchange([blocks(d_wdn)]))
        adam("w_down", got[0], l)
        d_wo, d_wao, d_wco = _wgrad_square(
            [(sv["mg"], dx1b, False), (sv["att"], dya, True), (sv["co"], dyc, False)], f"wgrad_mixer{l}")
        d_win = _wgrad(dgt, sv["h"], NP, NQ + 2 * NKV, None, f"wgrad_gates{l}")
        (dq, dk, dv, dsink[l]), got = _attn_bwd(
            sv["q"], sv["k"], sv["v"], sv["att"], datt, sm, l, S,
            side=_Exchange([blocks(d_wup), blocks(d_wo), blocks(d_wao), blocks(d_wco)]))
        for name, land in zip(["w_up", "w_o", "w_attn_out", "w_conv_out"], got):
            adam(name, land, l)
        d_win = _wgrad(dq, sv["h"], NP, 0, d_win, f"wgrad_q{l}", a_is_transposed=True)
        d_win = _wgrad(dk, sv["h"], NP, NQ, d_win, f"wgrad_k{l}", a_is_transposed=True)
        d_win = _wgrad(dv, sv["h"], NP, NQ + NKV, d_win, f"wgrad_v{l}", a_is_transposed=True)
        side = _Exchange([blocks(d_win)]) if l == 0 else None
        (dcur, acc_in[l]), got_in = _inproj_bwd(dgt, dq, dk, dv, sv["x"], dx1, g_mix3, wint[l], l, tm_in, side=side)
        d_win_above = d_win

    dsinks = jnp.stack([dsink[l][:, :, :, 0].sum(axis=0).reshape(16) for l in range(L)])
    small = _pack_small(
        jnp.concatenate([acc_in[l][0:1] for l in range(L)]),
        jnp.stack([acc_mix[l][0:2].reshape(2 * D) for l in range(L)]),
        dsinks,
        jnp.concatenate([acc_mix[l][2:3] for l in range(L)]),
        jnp.concatenate([acc_mlp[l][0:1] for l in range(L)]),
        acc_loss[0],
        jnp.concatenate([acc_mix[l][3:6] for l in range(L)]),
        acc_loss[1:2])
    adam("w_in", got_in[0], 0)
    (small_land,) = _remote_only(_Exchange([], [small]), "exchange_small")

    row = lambda t: t.reshape(1, D)
    small_out, loss = _adam_small(small_land, me.reshape(1).astype(jnp.int32), {
        "g_mix": (g_mix, m_g_mix, v_g_mix), "b_gates": (b_gates, m_b_gates, v_b_gates), "sinks": (sinks, m_sinks, v_sinks),
        "conv_b": (conv_b, m_conv_b, v_conv_b), "g_mlp": (g_mlp, m_g_mlp, v_g_mlp),
        "g_final": (row(g_final), row(m_g_final), row(v_g_final)), "conv_w": (conv_w, m_conv_w, v_conv_w)})
    small_out["g_final"] = [t.reshape(D) for t in small_out["g_final"]]
    big["w_in"] = [tr(o) for o in big["w_in"]]
    order = ["g_mix", "w_in", "b_gates", "sinks", "w_attn_out", "conv_w", "conv_b", "w_conv_out", "w_o", "g_mlp",
             "w_up", "w_down", "g_final"]
    out = [loss.reshape(()), dcur.reshape(nseq, S, D)]
    for kind in range(4):
        for name in order:
            out.append(big[name][kind] if name in big else small_out[name][kind])
    return tuple(out)
```

```python
import numpy as np
import jax
import jax.numpy as jnp
from jax import lax
from jax.experimental import pallas as pl
from jax.experimental.pallas import tpu as pltpu

D = 1024
NG = 5 * D
NQ = 1024
NKV = 256
NP = NQ + 2 * NKV + NG
F = 4096
HD = 64
GQ = 4
WIN = 128
L = 2
NDEV = 8
EPS = 1e-6
NEG = -1e30
SMALL_ROWS = 24
LR, B1, B2, AEPS, WD, STEP = 0.001, 0.9, 0.999, 1e-08, 0.01, 10

BF = jnp.bfloat16
F32 = jnp.float32
MESH = pl.DeviceIdType.MESH
VMEM_LIMIT = 60 * 1024 * 1024
ANY = pl.BlockSpec(memory_space=pl.ANY)

NN = ((1,), (0,))
NT = ((1,), (1,))
TN = ((0,), (0,))


def _dot(a, b, dims):
    return lax.dot_general(a, b, (dims, ((), ())), preferred_element_type=F32)


def _resident(shape, imap):
    return pl.BlockSpec(shape, imap, pipeline_mode=pl.Buffered(1))


def _position():
    return lax.axis_index("x"), lax.axis_index("y"), lax.axis_index("c")


class _Gather:
    def __init__(self, shards):
        n = len(shards)
        self.inputs = list(shards)
        self.out_shape = [jax.ShapeDtypeStruct((s.shape[0], NDEV) + s.shape[1:], s.dtype) for s in shards]
        self.scratch = [pltpu.SemaphoreType.DMA((n, 7)), pltpu.SemaphoreType.DMA((n, 7)), pltpu.SemaphoreType.DMA((n,))]

    def _plan(self, src, dst, sems):
        send_sems, recv_sems, local_sems = sems
        n = len(src)
        x, y, c = _position()
        me, sibling = (x, y, c), (x, y, 1 - c)
        chips = [(1 - x, y), (x, 1 - y), (1 - x, 1 - y)]

        def rows(a, p):
            return dst[a].at[:, 4 * p[0] + 2 * p[1] + p[2]]

        def copy(a, k, block, to, from_src=False):
            return pltpu.make_async_remote_copy(
                src_ref=src[a] if from_src else rows(a, block), dst_ref=rows(a, block),
                send_sem=send_sems.at[a, k], recv_sem=recv_sems.at[a, k], device_id=to, device_id_type=MESH)

        mine = [pltpu.make_async_copy(src[a], rows(a, me), local_sems.at[a]) for a in range(n)]
        first = []
        for a in range(n):
            first.append(copy(a, 0, me, sibling, True))
            first += [copy(a, 1 + t, me, (*chip, c), True) for t, chip in enumerate(chips)]
        return n, c, me, sibling, chips, copy, mine, first

    def start(self, src, dst, sems):
        *_, mine, first = self._plan(src, dst, sems)
        for cp in mine + first:
            cp.start()

    def relay(self, src, dst, sems):
        n, c, me, sibling, chips, copy, _, _ = self._plan(src, dst, sems)
        for t, chip in enumerate(chips):
            for a in range(n):
                copy(a, 1 + t, (*chip, c), me).wait_recv()
                copy(a, 4 + t, (*chip, c), sibling).start()

    def finish(self, src, dst, sems):
        n, c, me, sibling, chips, copy, mine, first = self._plan(src, dst, sems)
        for a in range(n):
            copy(a, 0, sibling, me).wait_recv()
            for t, chip in enumerate(chips):
                copy(a, 4 + t, (*chip, 1 - c), me).wait_recv()
        for cp in first + [copy(a, 4 + t, (*chip, c), sibling) for t, chip in enumerate(chips) for a in range(n)]:
            cp.wait_send()
        for cp in mine:
            cp.wait()


class _Exchange:
    def __init__(self, grads, everyone=()):
        self.inputs = list(grads) + list(everyone)
        self.n_blocked = len(grads)
        n = len(self.inputs)
        self.out_shape = [jax.ShapeDtypeStruct(g.shape, g.dtype) for g in grads]
        self.out_shape += [jax.ShapeDtypeStruct((NDEV,) + e.shape, e.dtype) for e in everyone]
        self.scratch = [pltpu.SemaphoreType.DMA((n, 7)), pltpu.SemaphoreType.DMA((n, 7)), pltpu.SemaphoreType.DMA((n,))]

    def _plan(self, src, land, sems):
        send_sems, recv_sems, local_sems = sems
        x, y, c = _position()
        me = 4 * x + 2 * y + c

        def parts(peer_idx):
            return [(s.at[peer_idx] if a < self.n_blocked else s, land[a].at[me]) for a, s in enumerate(src)]

        local = [pltpu.make_async_copy(s, d, local_sems.at[a]) for a, (s, d) in enumerate(parts(me))]
        sent = []
        for rel in range(1, NDEV):
            px = 1 - x if rel & 4 else x
            py = 1 - y if rel & 2 else y
            pc = 1 - c if rel & 1 else c
            for a, (s, d) in enumerate(parts(4 * px + 2 * py + pc)):
                sent.append(pltpu.make_async_remote_copy(
                    src_ref=s, dst_ref=d, send_sem=send_sems.at[a, rel - 1], recv_sem=recv_sems.at[a, rel - 1],
                    device_id=(px, py, pc), device_id_type=MESH))
        return local, sent

    def start(self, src, land, sems):
        local, sent = self._plan(src, land, sems)
        for cp in local + sent:
            cp.start()

    def relay(self, src, land, sems):
        pass

    def finish(self, src, land, sems):
        local, sent = self._plan(src, land, sems)
        for cp in sent:
            cp.wait_recv()
        for cp in sent:
            cp.wait_send()
        for cp in local:
            cp.wait()


def _call(body, *, name, grid, in_specs, out_specs, out_shape, args, scratch_shapes=(), aliases=None, side=None,
          relay_at=1.0):
    sem = ("arbitrary",) * len(grid)
    if side is None:
        outs = pl.pallas_call(
            body, name=name, grid=grid, in_specs=in_specs, out_specs=out_specs, out_shape=out_shape,
            scratch_shapes=list(scratch_shapes), input_output_aliases=aliases or {},
            compiler_params=pltpu.CompilerParams(dimension_semantics=sem, vmem_limit_bytes=VMEM_LIMIT))(*args)
        return outs, []
    ni, no, ns = len(in_specs), len(out_specs), len(scratch_shapes)
    si, so = len(side.inputs), len(side.out_shape)

    def hosted(*refs):
        ins, refs = refs[:ni], refs[ni:]
        sins, refs = refs[:si], refs[si:]
        outs, refs = refs[:no], refs[no:]
        souts, refs = refs[:so], refs[so:]
        scr, sscr = refs[:ns], refs[ns:]
        step = pl.program_id(0)
        for d in range(1, len(grid)):
            step = step * grid[d] + pl.program_id(d)
        last = int(np.prod(grid)) - 1

        @pl.when(step == 0)
        def _():
            side.start(sins, souts, sscr)

        body(*ins, *outs, *scr)

        @pl.when(step == min(int(relay_at * last), last))
        def _():
            side.relay(sins, souts, sscr)

        @pl.when(step == last)
        def _():
            side.finish(sins, souts, sscr)

    outs = pl.pallas_call(
        hosted, name=name, grid=grid, in_specs=list(in_specs) + [ANY] * si, out_specs=list(out_specs) + [ANY] * so,
        out_shape=list(out_shape) + side.out_shape, scratch_shapes=list(scratch_shapes) + side.scratch,
        input_output_aliases=aliases or {},
        compiler_params=pltpu.CompilerParams(dimension_semantics=sem, vmem_limit_bytes=VMEM_LIMIT, has_side_effects=True),
    )(*args, *side.inputs)
    return outs[:no], outs[no:]


def _remote_only(side, name):
    n = len(side.inputs)

    def body(*refs):
        src, dst, sems = refs[:n], refs[n:n + len(side.out_shape)], refs[n + len(side.out_shape):]
        side.start(src, dst, sems)
        side.relay(src, dst, sems)
        side.finish(src, dst, sems)

    return pl.pallas_call(
        body, name=name, in_specs=[ANY] * n, out_specs=[ANY] * len(side.out_shape), out_shape=side.out_shape,
        scratch_shapes=side.scratch, compiler_params=pltpu.CompilerParams(has_side_effects=True))(*side.inputs)


def _rms(x, g):
    r = lax.rsqrt(jnp.mean(x * x, axis=-1, keepdims=True) + EPS)
    return x * r * g


def _rms_bwd(dy, x, g):
    r = lax.rsqrt(jnp.mean(x * x, axis=-1, keepdims=True) + EPS)
    xh = x * r
    dxh = dy * g
    dx = r * (dxh - xh * jnp.mean(dxh * xh, axis=-1, keepdims=True))
    return dx, jnp.sum(dy * xh, axis=0, keepdims=True)


def _halves(n):
    return [slice(0, n // 2), slice(n // 2, n)] if n % 256 == 0 else [slice(0, n)]


def _zero_at_first_step(acc_ref):
    first = pl.program_id(0) == 0

    @pl.when(first)
    def _():
        acc_ref[...] = jnp.zeros_like(acc_ref)


ROW1 = lambda: _resident((1, D), lambda i: (0, 0))
LROWS = lambda l, n=1: _resident((None, n, D), lambda *_: (l, 0, 0))
ACC = lambda: pl.BlockSpec((8, D), lambda i: (0, 0))


def _inproj_fwd(x, g, wint, l, tm, side=None, relay_at=1.0):
    T = x.shape[0]

    def body(x_ref, g_ref, w_ref, gt_ref, q_ref, k_ref, v_ref, h_ref):
        for rows in _halves(tm):
            h = _rms(x_ref[rows, :], g_ref[...]).astype(BF)
            h_ref[rows, :] = h
            q_ref[:, rows] = _dot(w_ref[0:NQ, :], h, NT).astype(BF)
            k_ref[:, rows] = _dot(w_ref[NQ:NQ + NKV, :], h, NT).astype(BF)
            v_ref[:, rows] = _dot(w_ref[NQ + NKV:NQ + 2 * NKV, :], h, NT).astype(BF)
            for s in range(5):
                lo = NQ + 2 * NKV + s * D
                gt_ref[rows, s * D:(s + 1) * D] = _dot(h, w_ref[lo:lo + D, :], NT).astype(BF)

    tok = lambda w: pl.BlockSpec((tm, w), lambda i: (i, 0))
    feat = lambda w: pl.BlockSpec((w, tm), lambda i: (0, i))
    return _call(
        body, name=f"inproj_fwd{l}", grid=(T // tm,),
        in_specs=[tok(D), LROWS(l), _resident((NP, D), lambda i: (0, 0))],
        out_specs=[tok(NG), feat(NQ), feat(NKV), feat(NKV), tok(D)],
        out_shape=[jax.ShapeDtypeStruct((T, NG), BF)] + [jax.ShapeDtypeStruct((w, T), BF) for w in (NQ, NKV, NKV)]
        + [jax.ShapeDtypeStruct((T, D), BF)],
        args=(x, g, wint), side=side, relay_at=relay_at)


def _band_geometry():
    j = lax.broadcasted_iota(jnp.int32, (2 * WIN, WIN), 0)
    r = lax.broadcasted_iota(jnp.int32, (2 * WIN, WIN), 1)
    dist = WIN + r - j
    dist0 = r - j
    return (dist.astype(F32), (dist >= 0) & (dist < WIN)), (dist0.astype(F32), dist0 >= 0)


def _pair_biases(sm_ref, pj):
    return [[jnp.where(ok, -sm_ref[1, pj * 2 * GQ + h] * dist, NEG) for h in range(2 * GQ)]
            for dist, ok in _band_geometry()]


def _reduce_rows(x, pair, whole):
    while x.shape[0] > 8:
        half = x.shape[0] // 2
        x = pair(x[:half], x[half:])
    return whole(x, axis=0, keepdims=True)


def _heads_on_lanes(ref, kvh, r0):
    return jnp.concatenate([ref[(kvh * GQ + g) * HD:(kvh * GQ + g + 1) * HD, pl.ds(r0, WIN)] for g in range(GQ)], axis=1)


def _band_probs(sm_ref, head0, q_ref, k_ref, r0, p0, kvh, biases):
    qt = _heads_on_lanes(q_ref, kvh, r0) * jnp.asarray(HD ** -0.5, BF)
    kt = k_ref[kvh * HD:(kvh + 1) * HD, pl.ds(p0, 2 * WIN)]
    st = _dot(kt, qt, TN)
    heads = []
    for g in range(GQ):
        sink = sm_ref[0, head0 + g]
        s = st[:, g * WIN:(g + 1) * WIN] + biases[kvh * GQ + g]
        m = jnp.maximum(_reduce_rows(s, jnp.maximum, jnp.max), sink)
        p = jnp.exp(s - m)
        ps = jnp.exp(sink - m)
        heads.append((p, ps, 1.0 / (_reduce_rows(p, jnp.add, jnp.sum) + ps)))
    return qt, kt, heads


def _attn_fwd(q, k, v, sm, l, S, side=None, relay_at=1.0):
    T = q.shape[1]
    nblk = S // WIN
    unroll = next(u for u in (5, 3, 1) if (nblk - 1) % u == 0)

    def body(sm_all_ref, q_ref, k_ref, v_ref, o_ref):
        sm_ref = sm_all_ref.at[l]
        pj = pl.program_id(1)
        biases = _pair_biases(sm_ref, pj)

        def block(i, first):
            r0 = 0 if first else pl.multiple_of(i * WIN, WIN)
            p0 = 0 if first else pl.multiple_of(i * WIN - WIN, WIN)
            for kvh in range(2):
                head0 = (2 * pj + kvh) * GQ
                _, _, heads = _band_probs(sm_ref, head0, q_ref, k_ref, r0, p0, kvh, biases[first])
                pt = jnp.concatenate([p.astype(BF) for p, _, _ in heads], axis=1)
                ot = _dot(v_ref[kvh * HD:(kvh + 1) * HD, pl.ds(p0, 2 * WIN)], pt, NN)
                for g in range(GQ):
                    rows = slice((kvh * GQ + g) * HD, (kvh * GQ + g + 1) * HD)
                    o_ref[rows, pl.ds(r0, WIN)] = (ot[:, g * WIN:(g + 1) * WIN] * heads[g][2]).astype(BF)

        block(0, True)

        def rest(t, c):
            for u in range(unroll):
                block(1 + unroll * t + u, False)
            return c

        lax.fori_loop(0, (nblk - 1) // unroll, rest, 0)

    wide = lambda: pl.BlockSpec((2 * GQ * HD, S), lambda s, p: (p, s))
    narrow = lambda: pl.BlockSpec((2 * HD, S), lambda s, p: (p, s))
    return _call(
        body, name=f"attn_fwd{l}", grid=(T // S, 2),
        in_specs=[pl.BlockSpec(memory_space=pltpu.SMEM), wide(), narrow(), narrow()],
        out_specs=[wide()], out_shape=[jax.ShapeDtypeStruct((D, T), BF)],
        args=(sm, q, k, v), side=side, relay_at=relay_at)


def _shift_rows(y, k, edge_rows, down):
    n = y.shape[0]
    rid = lax.broadcasted_iota(jnp.int32, y.shape, 0)
    out = pltpu.roll(y, k if down else n - k, 0)
    for t, row in enumerate(edge_rows):
        out = jnp.where(rid == (t if down else n - k + t), row, out)
    return out


def _mixer_fwd(x, gt, att, w3, bg, cw, cbias, l, S, tm):
    T = x.shape[0]

    def body(x_ref, cb_ref, cc_ref, cu_ref, ga_ref, gc_ref, cch_ref, cuh_ref, att_ref, wao_ref, wco_ref, wo_ref,
             bg_ref, cw_ref, cbias_ref, x1_ref, mg_ref, co_ref, ya_ref, yc_ref, zb_ref):
        first = (pl.program_id(0) * tm) % S == 0
        y = cc_ref[...].astype(F32) * cu_ref[...].astype(F32)
        hy1 = cch_ref[15:16, :].astype(F32) * cuh_ref[15:16, :].astype(F32)
        hy2 = cch_ref[14:15, :].astype(F32) * cuh_ref[14:15, :].astype(F32)
        hy1, hy2 = jnp.where(first, 0.0, hy1), jnp.where(first, 0.0, hy2)
        z = (cw_ref[0:1, :] * _shift_rows(y, 2, [hy2, hy1], True) + cw_ref[1:2, :] * _shift_rows(y, 1, [hy1], True)
             + cw_ref[2:3, :] * y)
        zb = z + cbias_ref[...]
        zb_ref[...] = zb.astype(BF)
        co = (cb_ref[...].astype(F32) * zb).astype(BF)
        co_ref[...] = co
        yc = _dot(co, wco_ref[...], NN)
        ya = _dot(att_ref[...], wao_ref[...], TN)
        ya_ref[...] = ya.astype(BF)
        yc_ref[...] = yc.astype(BF)
        sa = jax.nn.sigmoid(ga_ref[...].astype(F32) + bg_ref[0:1, :])
        sc = jax.nn.sigmoid(gc_ref[...].astype(F32) + bg_ref[1:2, :])
        mg = (sa * ya + sc * yc).astype(BF)
        mg_ref[...] = mg
        x1_ref[...] = x_ref[...] + _dot(mg, wo_ref[...], NN)

    tok = lambda: pl.BlockSpec((tm, D), lambda i: (i, 0))
    seg = lambda s: pl.BlockSpec((tm, D), lambda i: (i, s))
    halo = lambda s: pl.BlockSpec((16, D), lambda i: (jnp.maximum(i * (tm // 16) - 1, 0), s))
    wsp = lambda k: _resident((None, D, D), lambda i: (k, 0, 0))
    return _call(
        body, name=f"mixer_fwd{l}", grid=(T // tm,),
        in_specs=[tok(), seg(0), seg(1), seg(2), seg(3), seg(4), halo(1), halo(2),
                  pl.BlockSpec((D, tm), lambda i: (0, i)), wsp(0), wsp(1), wsp(2), LROWS(l, 2), LROWS(l, 8), LROWS(l)],
        out_specs=[tok()] * 6,
        out_shape=[jax.ShapeDtypeStruct((T, D), dt) for dt in (F32, BF, BF, BF, BF, BF)],
        args=(x, gt, gt, gt, gt, gt, gt, gt, att, w3, w3, w3, bg, cw, cbias))


def _mlp_fwd(x1, g, w2, l, tm, head=None, side=None, relay_at=1.0):
    T = x1.shape[0]
    nt = T // tm
    FC = 1024

    def body(x_ref, g_ref, wup_ref, wdn_ref, *rest):
        out_ref, h_ref, a_ref = rest[-4:-1] if head else rest
        if head:
            t_ref, gf_ref, acc_ref = rest[0], rest[1], rest[-1]
            _zero_at_first_step(acc_ref)
        for rows in _halves(tm):
            x = x_ref[rows, :]
            h = _rms(x, g_ref[...]).astype(BF)
            h_ref[rows, :] = h
            acc = x
            for c in range(F // FC):
                a = _dot(h, wup_ref[c * FC:(c + 1) * FC, :], NT)
                a_ref[rows, c * FC:(c + 1) * FC] = a.astype(BF)
                u = jnp.maximum(a, 0.0)
                acc = acc + _dot((u * u).astype(BF), wdn_ref[c * FC:(c + 1) * FC, :], NN)
            if not head:
                out_ref[rows, :] = acc
                continue
            gf = gf_ref[...]
            err = _rms(acc, gf) - t_ref[rows, :]
            out_ref[rows, :], dg = _rms_bwd(err * (1.0 / D), acc, gf)
            acc_ref[0:1, :] += dg
            acc_ref[1:2, :] += jnp.sum(err * err, axis=0, keepdims=True)
        if not head:
            return

        @pl.when(pl.program_id(0) == nt - 1)
        def _():
            acc_ref[1:2, :] = jnp.zeros((1, D), F32) + (0.5 / D) * jnp.sum(acc_ref[1:2, :])

    tok = lambda w: pl.BlockSpec((tm, w), lambda i: (i, 0))
    wsp = lambda k: _resident((None, F, D), lambda i: (k, 0, 0))
    return _call(
        body, name=f"mlp_fwd{l}", grid=(nt,),
        in_specs=[tok(D), LROWS(l), wsp(0), wsp(1)] + ([tok(D), ROW1()] if head else []),
        out_specs=[tok(D), tok(D), tok(F)] + ([ACC()] if head else []),
        out_shape=[jax.ShapeDtypeStruct((T, D), F32), jax.ShapeDtypeStruct((T, D), BF), jax.ShapeDtypeStruct((T, F), BF)]
        + ([jax.ShapeDtypeStruct((8, D), F32)] if head else []),
        args=(x1, g, w2, w2) + (tuple(head) if head else ()), side=side, relay_at=relay_at)


def _mlp_bwd(dx2, x1, a, g, w2, l, tm, side=None):
    T = dx2.shape[0]
    FC = 1024

    def body(d_ref, x_ref, a_ref, g_ref, wup_ref, wdn_ref, da_ref, dx1_ref, db_ref, acc_ref):
        _zero_at_first_step(acc_ref)
        for rows in _halves(tm):
            d = d_ref[rows, :]
            db = d.astype(BF)
            db_ref[rows, :] = db
            dh = jnp.zeros(d.shape, F32)
            for c in range(F // FC):
                du = _dot(db, wdn_ref[c * FC:(c + 1) * FC, :], NT)
                da = (du * (2.0 * jnp.maximum(a_ref[rows, c * FC:(c + 1) * FC].astype(F32), 0.0))).astype(BF)
                da_ref[rows, c * FC:(c + 1) * FC] = da
                dh = dh + _dot(da, wup_ref[c * FC:(c + 1) * FC, :], NN)
            dx, dg = _rms_bwd(dh, x_ref[rows, :], g_ref[...])
            dx1_ref[rows, :] = d + dx
            acc_ref[0:1, :] += dg

    tok = lambda w: pl.BlockSpec((tm, w), lambda i: (i, 0))
    wsp = lambda k: _resident((None, F, D), lambda i: (k, 0, 0))
    return _call(
        body, name=f"mlp_bwd{l}", grid=(T // tm,),
        in_specs=[tok(D), tok(D), tok(F), LROWS(l), wsp(0), wsp(1)],
        out_specs=[tok(F), tok(D), tok(D), ACC()],
        out_shape=[jax.ShapeDtypeStruct((T, F), BF), jax.ShapeDtypeStruct((T, D), F32),
                   jax.ShapeDtypeStruct((T, D), BF), jax.ShapeDtypeStruct((8, D), F32)],
        args=(dx2, x1, a, g, w2, w2), side=side)


def _mixer_bwd(dx1, gt, ya, yc, zb, w3, bg, cw, l, S, tm, side=None):
    T = dx1.shape[0]
    nt = T // tm

    def body(d_ref, cb_ref, cc_ref, cu_ref, ga_ref, gc_ref, ya_ref, yc_ref, zb_ref, wao_ref, wco_ref, wo_ref,
             bg_ref, cw_ref, dg_ref, datt_ref, dya_ref, dyc_ref, db_ref, acc_ref, carry_ref):
        ti = nt - 1 - pl.program_id(0)
        _zero_at_first_step(acc_ref)

        @pl.when(((ti + 1) * tm) % S == 0)
        def _():
            carry_ref[...] = jnp.zeros_like(carry_ref)

        db = d_ref[...].astype(BF)
        db_ref[...] = db
        dm = _dot(db, wo_ref[...], NT)
        sa = jax.nn.sigmoid(ga_ref[...].astype(F32) + bg_ref[0:1, :])
        sc = jax.nn.sigmoid(gc_ref[...].astype(F32) + bg_ref[1:2, :])
        dya32 = dm * sa
        dyc32 = dm * sc
        dya = dya32.astype(BF)
        dyc = dyc32.astype(BF)
        dya_ref[...] = dya
        dyc_ref[...] = dyc
        dga = dya32 * ya_ref[...].astype(F32) * (1.0 - sa)
        dgc = dyc32 * yc_ref[...].astype(F32) * (1.0 - sc)
        dg_ref[:, 3 * D:4 * D] = dga.astype(BF)
        dg_ref[:, 4 * D:5 * D] = dgc.astype(BF)
        acc_ref[0:1, :] += jnp.sum(dga, axis=0, keepdims=True)
        acc_ref[1:2, :] += jnp.sum(dgc, axis=0, keepdims=True)
        datt_ref[...] = _dot(wao_ref[...], dya, NT).astype(BF)
        dco = _dot(dyc, wco_ref[...], NT)

        cc = cc_ref[...].astype(F32)
        cu = cu_ref[...].astype(F32)
        y = cc * cu
        dg_ref[:, 0:D] = (dco * zb_ref[...].astype(F32)).astype(BF)
        dz = dco * cb_ref[...].astype(F32)
        u1 = _shift_rows(dz, 1, [carry_ref[0:1, :]], False)
        u2 = _shift_rows(dz, 2, [carry_ref[0:1, :], carry_ref[1:2, :]], False)
        acc_ref[2:3, :] += jnp.sum(dz, axis=0, keepdims=True)
        acc_ref[3:4, :] += jnp.sum(u2 * y, axis=0, keepdims=True)
        acc_ref[4:5, :] += jnp.sum(u1 * y, axis=0, keepdims=True)
        acc_ref[5:6, :] += jnp.sum(dz * y, axis=0, keepdims=True)
        dy = cw_ref[2:3, :] * dz + cw_ref[1:2, :] * u1 + cw_ref[0:1, :] * u2
        dg_ref[:, D:2 * D] = (dy * cu).astype(BF)
        dg_ref[:, 2 * D:3 * D] = (dy * cc).astype(BF)
        carry_ref[...] = dz[0:8, :]

    tok = lambda w=D: pl.BlockSpec((tm, w), lambda i: (nt - 1 - i, 0))
    seg = lambda s: pl.BlockSpec((tm, D), lambda i: (nt - 1 - i, s))
    wsp = lambda k: _resident((None, D, D), lambda i: (k, 0, 0))
    return _call(
        body, name=f"mixer_bwd{l}", grid=(nt,),
        in_specs=[tok(), seg(0), seg(1), seg(2), seg(3), seg(4), tok(), tok(), tok(), wsp(0), wsp(1), wsp(2),
                  LROWS(l, 2), LROWS(l, 8)],
        out_specs=[tok(NG), pl.BlockSpec((D, tm), lambda i: (0, nt - 1 - i)), tok(), tok(), tok(), ACC()],
        out_shape=[jax.ShapeDtypeStruct((T, NG), BF), jax.ShapeDtypeStruct((D, T), BF)]
        + [jax.ShapeDtypeStruct((T, D), BF)] * 3 + [jax.ShapeDtypeStruct((8, D), F32)],
        scratch_shapes=[pltpu.VMEM((8, D), F32)],
        args=(dx1, gt, gt, gt, gt, gt, ya, yc, zb, w3, w3, w3, bg, cw), side=side)


def _attn_bwd(q, k, v, att, datt, sm, l, S, side=None):
    T = q.shape[1]
    nblk = S // WIN
    unroll = next(u for u in (5, 3, 1) if (nblk - 1) % u == 0)
    scale = HD ** -0.5

    def body(sm_all_ref, q_ref, k_ref, v_ref, o_ref, do_ref, dq_ref, dk_ref, dv_ref, ds_ref, dka_ref, dva_ref):
        sm_ref = sm_all_ref.at[l]
        pj = pl.program_id(1)
        biases = _pair_biases(sm_ref, pj)
        dka_ref[...] = jnp.zeros_like(dka_ref)
        dva_ref[...] = jnp.zeros_like(dva_ref)

        def block(i, first, dsinks):
            r0 = 0 if first else pl.multiple_of(i * WIN, WIN)
            p0 = 0 if first else pl.multiple_of(i * WIN - WIN, WIN)
            out = []
            for kvh in range(2):
                head0 = (2 * pj + kvh) * GQ
                rows = slice(kvh * HD, (kvh + 1) * HD)
                dob = _heads_on_lanes(do_ref, kvh, r0)
                dpt = _dot(v_ref[rows, pl.ds(p0, 2 * WIN)], dob, TN)
                qt, kt, heads = _band_probs(sm_ref, head0, q_ref, k_ref, r0, p0, kvh, biases[first])
                inv = jnp.concatenate([h[2] for h in heads], axis=1)
                do32 = dob.astype(F32)
                delta = jnp.sum(do32 * _heads_on_lanes(o_ref, kvh, r0).astype(F32), axis=0, keepdims=True) * inv
                dosb = (do32 * inv).astype(BF)
                dst = jnp.concatenate(
                    [(p * (dpt[:, g * WIN:(g + 1) * WIN] * h_inv - delta[:, g * WIN:(g + 1) * WIN])).astype(BF)
                     for g, (p, _, h_inv) in enumerate(heads)], axis=1)
                pt = jnp.concatenate([p.astype(BF) for p, _, _ in heads], axis=1)
                dqt = _dot(kt, dst, NN) * scale
                for g in range(GQ):
                    hr = slice((kvh * GQ + g) * HD, (kvh * GQ + g + 1) * HD)
                    dq_ref[hr, pl.ds(r0, WIN)] = dqt[:, g * WIN:(g + 1) * WIN].astype(BF)
                dka_ref[rows, pl.ds(p0, 2 * WIN)] += _dot(qt, dst, NT)
                dva_ref[rows, pl.ds(p0, 2 * WIN)] += _dot(dosb, pt, NT)
                ps = jnp.concatenate([h[1] for h in heads], axis=1)
                out.append(dsinks[kvh] - ps * delta)
            return tuple(out)

        zero = jnp.zeros((1, GQ * WIN), F32)
        def rest(t, c):
            for u in range(unroll):
                c = block(1 + unroll * t + u, False, c)
            return c

        dsinks = lax.fori_loop(0, (nblk - 1) // unroll, rest, block(0, True, (zero, zero)))
        for kvh in range(2):
            for g in range(GQ):
                tot = jnp.sum(dsinks[kvh][:, g * WIN:(g + 1) * WIN])
                ds_ref[kvh * GQ + g:kvh * GQ + g + 1, :] = jnp.zeros((1, 128), F32) + tot
        dk_ref[...] = dka_ref[...].astype(BF)
        dv_ref[...] = dva_ref[...].astype(BF)

    wide = lambda: pl.BlockSpec((2 * GQ * HD, S), lambda s, p: (p, s))
    narrow = lambda: pl.BlockSpec((2 * HD, S), lambda s, p: (p, s))
    return _call(
        body, name=f"attn_bwd{l}", grid=(T // S, 2),
        in_specs=[pl.BlockSpec(memory_space=pltpu.SMEM), wide(), narrow(), narrow(), wide(), wide()],
        out_specs=[wide(), narrow(), narrow(), pl.BlockSpec((None, None, 8, 128), lambda s, p: (s, p, 0, 0))],
        out_shape=[jax.ShapeDtypeStruct((NQ, T), BF), jax.ShapeDtypeStruct((NKV, T), BF),
                   jax.ShapeDtypeStruct((NKV, T), BF), jax.ShapeDtypeStruct((T // S, 2, 8, 128), F32)],
        scratch_shapes=[pltpu.VMEM((2 * HD, S), F32), pltpu.VMEM((2 * HD, S), F32)],
        args=(sm, q, k, v, att, datt), side=side)


def _inproj_bwd(dgt, dq, dk, dv, x, dres, g, wint, l, tm, side=None):
    T = x.shape[0]

    def body(dg_ref, dq_ref, dk_ref, dv_ref, x_ref, dr_ref, g_ref, w_ref, dx_ref, acc_ref):
        _zero_at_first_step(acc_ref)
        for rows in _halves(tm):
            dh = _dot(dq_ref[:, rows], w_ref[0:NQ, :], TN)
            dh = dh + _dot(dk_ref[:, rows], w_ref[NQ:NQ + NKV, :], TN)
            dh = dh + _dot(dv_ref[:, rows], w_ref[NQ + NKV:NQ + 2 * NKV, :], TN)
            dh = dh + _dot(dg_ref[rows, :], w_ref[NQ + 2 * NKV:NP, :], NN)
            dx, dg = _rms_bwd(dh, x_ref[rows, :], g_ref[...])
            dx_ref[rows, :] = dr_ref[rows, :] + dx
            acc_ref[0:1, :] += dg

    tok = lambda w: pl.BlockSpec((tm, w), lambda i: (i, 0))
    feat = lambda w: pl.BlockSpec((w, tm), lambda i: (0, i))
    return _call(
        body, name=f"inproj_bwd{l}", grid=(T // tm,),
        in_specs=[tok(NG), feat(NQ), feat(NKV), feat(NKV), tok(D), tok(D), LROWS(l), _resident((NP, D), lambda i: (0, 0))],
        out_specs=[tok(D), ACC()],
        out_shape=[jax.ShapeDtypeStruct((T, D), F32), jax.ShapeDtypeStruct((8, D), F32)],
        args=(dgt, dq, dk, dv, x, dres, g, wint), side=side)


def _wgrad(a, b, rows, row0, into, name, relu2=False, a_is_transposed=False):
    M, T = a.shape if a_is_transposed else a.shape[::-1]
    tmm = next(t for t in (1024, 512, 256) if M % t == 0 and row0 % t == 0)
    tk = min(4096 if M > 4096 else 2048, T)
    nk = T // tk
    blk0 = row0 // tmm

    def body(*refs):
        a_ref, b_ref = refs[0], refs[1]
        o_ref, acc_ref = refs[-2], refs[-1]
        kk = pl.program_id(1)

        @pl.when(kk == 0)
        def _():
            acc_ref[...] = jnp.zeros_like(acc_ref)

        av = a_ref[...]
        if relu2:
            t = jnp.maximum(av.astype(F32), 0.0)
            av = (t * t).astype(BF)
        acc_ref[...] += _dot(av, b_ref[...], NN if a_is_transposed else TN)

        @pl.when(kk == nk - 1)
        def _():
            o_ref[...] = acc_ref[...].astype(BF)

    a_spec = pl.BlockSpec((tmm, tk), lambda j, kk: (j, kk)) if a_is_transposed else pl.BlockSpec((tk, tmm), lambda j, kk: (kk, j))
    in_specs = [a_spec, pl.BlockSpec((tk, D), lambda j, kk: (kk, 0))]
    args = [a, b]
    if into is not None:
        in_specs.append(ANY)
        args.append(into)
    (out,), _ = _call(
        body, name=name, grid=(M // tmm, nk), in_specs=in_specs,
        out_specs=[pl.BlockSpec((tmm, D), lambda j, kk: (blk0 + j, 0))],
        out_shape=[jax.ShapeDtypeStruct((rows, D), BF)], scratch_shapes=[pltpu.VMEM((tmm, D), F32)],
        aliases={2: 0} if into is not None else None, args=args)
    return out


def _wgrad_square(jobs, name):
    n = len(jobs)
    T = jobs[0][1].shape[0]
    tk = min(1024, T)
    nk = T // tk

    def body(*refs):
        ins, outs, acc_ref = refs[:2 * n], refs[2 * n:3 * n], refs[3 * n]
        j, kk = pl.program_id(0), pl.program_id(1)

        @pl.when(kk == 0)
        def _():
            acc_ref[...] = jnp.zeros_like(acc_ref)

        for p, (_, _, a_t) in enumerate(jobs):
            @pl.when(j == p)
            def _(p=p, a_t=a_t):
                acc_ref[...] += _dot(ins[2 * p][...], ins[2 * p + 1][...], NN if a_t else TN)

                @pl.when(kk == nk - 1)
                def _():
                    outs[p][...] = acc_ref[...].astype(BF)

    def step(p):
        return lambda j, kk: jnp.where(j == p, kk, jnp.where(j > p, nk - 1, 0))

    in_specs, args = [], []
    for p, (a, b, a_t) in enumerate(jobs):
        at = step(p)
        in_specs.append(pl.BlockSpec((D, tk), lambda j, kk, at=at: (0, at(j, kk))) if a_t
                        else pl.BlockSpec((tk, D), lambda j, kk, at=at: (at(j, kk), 0)))
        in_specs.append(pl.BlockSpec((tk, D), lambda j, kk, at=at: (at(j, kk), 0)))
        args += [a, b]
    outs, _ = _call(
        body, name=name, grid=(n, nk), in_specs=in_specs,
        out_specs=[pl.BlockSpec((D, D), lambda j, kk: (0, 0))] * n,
        out_shape=[jax.ShapeDtypeStruct((D, D), BF)] * n, scratch_shapes=[pltpu.VMEM((D, D), F32)], args=args)
    return outs


def _adamw(w, g, m, v):
    m = B1 * m + (1.0 - B1) * g
    v = B2 * v + (1.0 - B2) * (g * g)
    m_hat = m / (1.0 - B1 ** STEP)
    v_hat = v / (1.0 - B2 ** STEP)
    return -LR * (m_hat / (jnp.sqrt(v_hat) + AEPS) + WD * w), m, v


def _adam_sum(land, w, m, v, l, into, name, transposed=False):
    _, r, _ = land.shape
    tr = 208 if r % 208 == 0 else (256 if r % 256 == 0 else r)
    tc = 256

    def body(land_ref, w_ref, m_ref, v_ref, *rest):
        g_ref, d_ref, nm_ref, nv_ref = rest[-4:]
        g = land_ref[0].astype(F32)
        for s in range(1, NDEV):
            g = g + land_ref[s].astype(F32)
        if transposed:
            g = g.T
        g_ref[...] = g
        d_ref[...], nm_ref[...], nv_ref[...] = _adamw(w_ref[...], g, m_ref[...], v_ref[...])

    if transposed:
        blk = lambda: pl.BlockSpec((None, tc, r), lambda j: (l, j, 0))
        land_spec, grid = pl.BlockSpec((NDEV, r, tc), lambda j: (0, 0, j)), (D // tc,)
    else:
        blk = lambda: pl.BlockSpec((None, tr, D), lambda j: (l, j, 0))
        land_spec, grid = pl.BlockSpec((NDEV, tr, D), lambda j: (0, j, 0)), (r // tr,)
    in_specs = [land_spec, blk(), blk(), blk()]
    args = [land, w, m, v]
    aliases = None
    if into is not None:
        in_specs += [ANY] * 4
        args += list(into)
        aliases = {4 + t: t for t in range(4)}
    outs, _ = _call(body, name=name, grid=grid, in_specs=in_specs, out_specs=[blk()] * 4,
                    out_shape=[jax.ShapeDtypeStruct(w.shape, F32)] * 4, aliases=aliases, args=args)
    return outs


SMALL_NAMES = ("g_mix", "b_gates", "sinks", "conv_b", "g_mlp", "g_final", "conv_w")


def _adam_small(land, me, masters):
    n = len(SMALL_NAMES)
    lanes = D // NDEV

    def body(land_ref, me_ref, *refs):
        ins, outs, loss_ref, gs_ref = refs[:3 * n], refs[3 * n:7 * n], refs[7 * n], refs[7 * n + 1]
        g = land_ref[0]
        for s in range(1, NDEV):
            g = g + land_ref[s]
        gs_ref[...] = g

        def update(k, g_piece, idx):
            w_ref, m_ref, v_ref = ins[3 * k:3 * k + 3]
            outs[4 * k][idx] = g_piece
            outs[4 * k + 1][idx], outs[4 * k + 2][idx], outs[4 * k + 3][idx] = _adamw(w_ref[idx], g_piece, m_ref[idx], v_ref[idx])

        whole = (slice(None), slice(None))
        update(0, gs_ref[0:2, :], whole)
        for l in range(L):
            for h in range(2):
                update(1, gs_ref[2 + 2 * l + h:3 + 2 * l + h, :], (slice(l, l + 1), slice(h * D, (h + 1) * D)))
            update(2, gs_ref[6:7, 16 * l:16 * (l + 1)], (slice(l, l + 1), slice(None)))
        update(3, gs_ref[7:9, :], whole)
        update(4, gs_ref[9:11, :], whole)
        update(5, gs_ref[11:12, :], whole)
        mine = pl.ds(pl.multiple_of(me_ref[0] * lanes, lanes), lanes)
        for l in range(L):
            for k in range(3):
                update(6, gs_ref[12 + 3 * l + k:13 + 3 * l + k, mine], (l, slice(k, k + 1), slice(None)))
        loss_ref[...] = gs_ref[18:19, 0:1]

    flat = [t for name in SMALL_NAMES for t in masters[name]]
    vmem = pl.BlockSpec(memory_space=pltpu.VMEM)
    outs = pl.pallas_call(
        body, name="adam_small",
        in_specs=[vmem, pl.BlockSpec(memory_space=pltpu.SMEM)] + [vmem] * len(flat),
        out_shape=[jax.ShapeDtypeStruct(masters[name][0].shape, F32) for name in SMALL_NAMES for _ in range(4)]
        + [jax.ShapeDtypeStruct((1, 1), F32)],
        scratch_shapes=[pltpu.VMEM((SMALL_ROWS, D), F32)],
        compiler_params=pltpu.CompilerParams(vmem_limit_bytes=VMEM_LIMIT))(land, me, *flat)
    return {name: outs[4 * k:4 * k + 4] for k, name in enumerate(SMALL_NAMES)}, outs[-1]


def _pack_small(g_mix, b_gates, sinks, conv_b, g_mlp, g_final, conv_w_rows, extra):
    sink_row = jnp.zeros((1, D), F32).at[0, :2 * 16].set(sinks.reshape(-1))
    return jnp.concatenate([g_mix, b_gates.reshape(4, D), sink_row, conv_b, g_mlp, g_final.reshape(1, D),
                            conv_w_rows, extra, jnp.zeros((SMALL_ROWS - 19, D), F32)], axis=0)


def kernel(x, g_mix, w_in, b_gates, sinks, w_attn_out, conv_w, conv_b, w_conv_out, w_o, g_mlp, w_up, w_down, g_final, loss_target, m_g_mix, m_w_in, m_b_gates, m_sinks, m_w_attn_out, m_conv_w, m_conv_b, m_w_conv_out, m_w_o, m_g_mlp, m_w_up, m_w_down, m_g_final, v_g_mix, v_w_in, v_b_gates, v_sinks, v_w_attn_out, v_conv_w, v_conv_b, v_w_conv_out, v_w_o, v_g_mlp, v_w_up, v_w_down, v_g_final):
    nseq, S, _ = x.shape
    T = nseq * S
    tm_in = min(512, S)
    tm = min(512, S)
    xi, yi, ci = _position()
    me = 4 * xi + 2 * yi + ci
    tr = lambda t: jnp.swapaxes(t, 1, 2)
    blocks = lambda t: t.reshape(NDEV, t.shape[0] // NDEV, D)

    win_t, wup_t = tr(w_in), tr(w_up)
    sh_win = [win_t[l].astype(BF)[None] for l in range(L)]
    sh_w3 = [jnp.stack([w_attn_out[l], w_conv_out[l], w_o[l]]).astype(BF) for l in range(L)]
    sh_w2 = [jnp.stack([wup_t[l], w_down[l]]).astype(BF) for l in range(L)]
    wint, w3, w2 = [None] * L, [None] * L, [None] * L
    wint0_g, cw_g = _remote_only(_Gather([sh_win[0], jnp.pad(conv_w, ((0, 0), (0, 5), (0, 0)))]), "gather_first")
    wint[0] = wint0_g.reshape(NP, D)
    cw = jnp.swapaxes(cw_g, 1, 2).reshape(L, 8, D)
    slopes = np.power(np.float32(2.0), -8.0 * np.arange(1, 17, dtype=np.float32) / 16).astype(np.float32)
    sm = jnp.stack([sinks, jnp.broadcast_to(jnp.asarray(slopes), sinks.shape)], axis=1)
    per_layer = lambda t, n=1: t.reshape(L, n, D)
    g_mix3, g_mlp3, conv_b3, bg = per_layer(g_mix), per_layer(g_mlp), per_layer(conv_b), per_layer(b_gates, 2)

    xf = x.reshape(T, D)
    saved = []
    cur = xf
    for l in range(L):
        (gt, q, k, v, h), got = _inproj_fwd(cur, g_mix3, wint[l], l, tm_in, side=_Gather([sh_w2[l]]),
                                            relay_at=0.85)
        w2[l] = got[0].reshape(2, F, D)
        (att,), got = _attn_fwd(q, k, v, sm, l, S, side=_Gather([sh_w3[l]]), relay_at=0.5)
        w3[l] = got[0].reshape(3, D, D)
        (x1, mg, co, ya, yc, zb), _ = _mixer_fwd(cur, gt, att, w3[l], bg, cw, conv_b3, l, S, tm)
        head = (loss_target.reshape(T, D), g_final.reshape(1, D)) if l == L - 1 else None
        (nxt, h2, a, *acc_loss), got = _mlp_fwd(x1, g_mlp3, w2[l], l, tm_in, head=head,
                                                side=_Gather([sh_win[l + 1]]) if l + 1 < L else None, relay_at=0.7)
        if l + 1 < L:
            wint[l + 1] = got[0].reshape(NP, D)
        saved.append(dict(x=cur, gt=gt, q=q, k=k, v=v, h=h, att=att, x1=x1, mg=mg, co=co, ya=ya, yc=yc, zb=zb, h2=h2, a=a))
        cur = nxt
    dcur, acc_loss = cur, acc_loss[0]

    masters = {"w_in": (win_t, tr(m_w_in), tr(v_w_in)), "w_attn_out": (w_attn_out, m_w_attn_out, v_w_attn_out),
               "w_conv_out": (w_conv_out, m_w_conv_out, v_w_conv_out), "w_o": (w_o, m_w_o, v_w_o),
               "w_up": (w_up, m_w_up, v_w_up), "w_down": (w_down, m_w_down, v_w_down)}
    big = {name: None for name in masters}

    def adam(name, land, l):
        big[name] = _adam_sum(land, *masters[name], l, big[name], f"adam_{name}{l}", transposed=name == "w_up")

    acc_in, acc_mix, acc_mlp, dsink = [None] * L, [None] * L, [None] * L, [None] * L
    d_win_above = None
    for l in reversed(range(L)):
        sv = saved[l]
        side = _Exchange([blocks(d_win_above)]) if l + 1 < L else None
        (da, dx1, dx2b, acc_mlp[l]), got = _mlp_bwd(dcur, sv["x1"], sv["a"], g_mlp3, w2[l], l, tm_in, side=side)
        if l + 1 < L:
            adam("w_in", got[0], l + 1)
        d_wdn = _wgrad(sv["a"], dx2b, F, 0, None, f"wgrad_down{l}", relu2=True)
        d_wup = _wgrad(da, sv["h2"], F, 0, None, f"wgrad_up{l}")
        (dgt, datt, dya, dyc, dx1b, acc_mix[l]), got = _mixer_bwd(
            dx1, sv["gt"], sv["ya"], sv["yc"], sv["zb"], w3[l], bg, cw, l, S, tm,
            side=_Exchange([blocks(d_wdn)]))
        adam("w_down", got[0], l)
        d_wo, d_wao, d_wco = _wgrad_square(
            [(sv["mg"], dx1b, False), (sv["att"], dya, True), (sv["co"], dyc, False)], f"wgrad_mixer{l}")
        d_win = _wgrad(dgt, sv["h"], NP, NQ + 2 * NKV, None, f"wgrad_gates{l}")
        (dq, dk, dv, dsink[l]), got = _attn_bwd(
            sv["q"], sv["k"], sv["v"], sv["att"], datt, sm, l, S,
            side=_Exchange([blocks(d_wup), blocks(d_wo), blocks(d_wao), blocks(d_wco)]))
        for name, land in zip(["w_up", "w_o", "w_attn_out", "w_conv_out"], got):
            adam(name, land, l)
        d_win = _wgrad(dq, sv["h"], NP, 0, d_win, f"wgrad_q{l}", a_is_transposed=True)
        d_win = _wgrad(dk, sv["h"], NP, NQ, d_win, f"wgrad_k{l}", a_is_transposed=True)
        d_win = _wgrad(dv, sv["h"], NP, NQ + NKV, d_win, f"wgrad_v{l}", a_is_transposed=True)
        side = _Exchange([blocks(d_win)]) if l == 0 else None
        (dcur, acc_in[l]), got_in = _inproj_bwd(dgt, dq, dk, dv, sv["x"], dx1, g_mix3, wint[l], l, tm_in, side=side)
        d_win_above = d_win

    dsinks = jnp.stack([dsink[l][:, :, :, 0].sum(axis=0).reshape(16) for l in range(L)])
    small = _pack_small(
        jnp.concatenate([acc_in[l][0:1] for l in range(L)]),
        jnp.stack([acc_mix[l][0:2].reshape(2 * D) for l in range(L)]),
        dsinks,
        jnp.concatenate([acc_mix[l][2:3] for l in range(L)]),
        jnp.concatenate([acc_mlp[l][0:1] for l in range(L)]),
        acc_loss[0],
        jnp.concatenate([acc_mix[l][3:6] for l in range(L)]),
        acc_loss[1:2])
    adam("w_in", got_in[0], 0)
    (small_land,) = _remote_only(_Exchange([], [small]), "exchange_small")

    row = lambda t: t.reshape(1, D)
    small_out, loss = _adam_small(small_land, me.reshape(1).astype(jnp.int32), {
        "g_mix": (g_mix, m_g_mix, v_g_mix), "b_gates": (b_gates, m_b_gates, v_b_gates), "sinks": (sinks, m_sinks, v_sinks),
        "conv_b": (conv_b, m_conv_b, v_conv_b), "g_mlp": (g_mlp, m_g_mlp, v_g_mlp),
        "g_final": (row(g_final), row(m_g_final), row(v_g_final)), "conv_w": (conv_w, m_conv_w, v_conv_w)})
    small_out["g_final"] = [t.reshape(D) for t in small_out["g_final"]]
    big["w_in"] = [tr(o) for o in big["w_in"]]
    order = ["g_mix", "w_in", "b_gates", "sinks", "w_attn_out", "conv_w", "conv_b", "w_conv_out", "w_o", "g_mlp",
             "w_up", "w_down", "g_final"]
    out = [loss.reshape(()), dcur.reshape(nseq, S, D)]
    for kind in range(4):
        for name in order:
            out.append(big[name][kind] if name in big else small_out[name][kind])
    return tuple(out)
```

```python
import numpy as np
import jax
import jax.numpy as jnp
from jax import lax
from jax.experimental import pallas as pl
from jax.experimental.pallas import tpu as pltpu

D = 1024
NG = 5 * D
NQ = 1024
NKV = 256
NP = NQ + 2 * NKV + NG
F = 4096
HD = 64
GQ = 4
WIN = 128
L = 2
NDEV = 8
EPS = 1e-6
NEG = -1e30
SMALL_ROWS = 24
LR, B1, B2, AEPS, WD, STEP = 0.001, 0.9, 0.999, 1e-08, 0.01, 10

BF = jnp.bfloat16
F32 = jnp.float32
MESH = pl.DeviceIdType.MESH
VMEM_LIMIT = 60 * 1024 * 1024
ANY = pl.BlockSpec(memory_space=pl.ANY)

NN = ((1,), (0,))
NT = ((1,), (1,))
TN = ((0,), (0,))


def _dot(a, b, dims):
    return lax.dot_general(a, b, (dims, ((), ())), preferred_element_type=F32)


def _resident(shape, imap):
    return pl.BlockSpec(shape, imap, pipeline_mode=pl.Buffered(1))


def _position():
    return lax.axis_index("x"), lax.axis_index("y"), lax.axis_index("c")


class _Gather:
    def __init__(self, shards):
        n = len(shards)
        self.inputs = list(shards)
        self.out_shape = [jax.ShapeDtypeStruct((s.shape[0], NDEV) + s.shape[1:], s.dtype) for s in shards]
        self.scratch = [pltpu.SemaphoreType.DMA((n, 7)), pltpu.SemaphoreType.DMA((n, 7)), pltpu.SemaphoreType.DMA((n,))]

    def _plan(self, src, dst, sems):
        send_sems, recv_sems, local_sems = sems
        n = len(src)
        x, y, c = _position()
        me, sibling = (x, y, c), (x, y, 1 - c)
        chips = [(1 - x, y), (x, 1 - y), (1 - x, 1 - y)]

        def rows(a, p):
            return dst[a].at[:, 4 * p[0] + 2 * p[1] + p[2]]

        def copy(a, k, block, to, from_src=False):
            return pltpu.make_async_remote_copy(
                src_ref=src[a] if from_src else rows(a, block), dst_ref=rows(a, block),
                send_sem=send_sems.at[a, k], recv_sem=recv_sems.at[a, k], device_id=to, device_id_type=MESH)

        mine = [pltpu.make_async_copy(src[a], rows(a, me), local_sems.at[a]) for a in range(n)]
        first = []
        for a in range(n):
            first.append(copy(a, 0, me, sibling, True))
            first += [copy(a, 1 + t, me, (*chip, c), True) for t, chip in enumerate(chips)]
        return n, c, me, sibling, chips, copy, mine, first

    def start(self, src, dst, sems):
        *_, mine, first = self._plan(src, dst, sems)
        for cp in mine + first:
            cp.start()

    def relay(self, src, dst, sems):
        n, c, me, sibling, chips, copy, _, _ = self._plan(src, dst, sems)
        for t, chip in enumerate(chips):
            for a in range(n):
                copy(a, 1 + t, (*chip, c), me).wait_recv()
                copy(a, 4 + t, (*chip, c), sibling).start()

    def finish(self, src, dst, sems):
        n, c, me, sibling, chips, copy, mine, first = self._plan(src, dst, sems)
        for a in range(n):
            copy(a, 0, sibling, me).wait_recv()
            for t, chip in enumerate(chips):
                copy(a, 4 + t, (*chip, 1 - c), me).wait_recv()
        for cp in first + [copy(a, 4 + t, (*chip, c), sibling) for t, chip in enumerate(chips) for a in range(n)]:
            cp.wait_send()
        for cp in mine:
            cp.wait()


class _Exchange:
    def __init__(self, grads, everyone=()):
        self.inputs = list(grads) + list(everyone)
        self.n_blocked = len(grads)
        n = len(self.inputs)
        self.out_shape = [jax.ShapeDtypeStruct(g.shape, g.dtype) for g in grads]
        self.out_shape += [jax.ShapeDtypeStruct((NDEV,) + e.shape, e.dtype) for e in everyone]
        self.scratch = [pltpu.SemaphoreType.DMA((n, 7)), pltpu.SemaphoreType.DMA((n, 7)), pltpu.SemaphoreType.DMA((n,))]

    def _plan(self, src, land, sems):
        send_sems, recv_sems, local_sems = sems
        x, y, c = _position()
        me = 4 * x + 2 * y + c

        def parts(peer_idx):
            return [(s.at[peer_idx] if a < self.n_blocked else s, land[a].at[me]) for a, s in enumerate(src)]

        local = [pltpu.make_async_copy(s, d, local_sems.at[a]) for a, (s, d) in enumerate(parts(me))]
        sent = []
        for rel in range(1, NDEV):
            px = 1 - x if rel & 4 else x
            py = 1 - y if rel & 2 else y
            pc = 1 - c if rel & 1 else c
            for a, (s, d) in enumerate(parts(4 * px + 2 * py + pc)):
                sent.append(pltpu.make_async_remote_copy(
                    src_ref=s, dst_ref=d, send_sem=send_sems.at[a, rel - 1], recv_sem=recv_sems.at[a, rel - 1],
                    device_id=(px, py, pc), device_id_type=MESH))
        return local, sent

    def start(self, src, land, sems):
        local, sent = self._plan(src, land, sems)
        for cp in local + sent:
            cp.start()

    def relay(self, src, land, sems):
        pass

    def finish(self, src, land, sems):
        local, sent = self._plan(src, land, sems)
        for cp in sent:
            cp.wait_recv()
        for cp in sent:
            cp.wait_send()
        for cp in local:
            cp.wait()


def _call(body, *, name, grid, in_specs, out_specs, out_shape, args, scratch_shapes=(), aliases=None, side=None,
          relay_at=1.0):
    sem = ("arbitrary",) * len(grid)
    if side is None:
        outs = pl.pallas_call(
            body, name=name, grid=grid, in_specs=in_specs, out_specs=out_specs, out_shape=out_shape,
            scratch_shapes=list(scratch_shapes), input_output_aliases=aliases or {},
            compiler_params=pltpu.CompilerParams(dimension_semantics=sem, vmem_limit_bytes=VMEM_LIMIT))(*args)
        return outs, []
    ni, no, ns = len(in_specs), len(out_specs), len(scratch_shapes)
    si, so = len(side.inputs), len(side.out_shape)

    def hosted(*refs):
        ins, refs = refs[:ni], refs[ni:]
        sins, refs = refs[:si], refs[si:]
        outs, refs = refs[:no], refs[no:]
        souts, refs = refs[:so], refs[so:]
        scr, sscr = refs[:ns], refs[ns:]
        step = pl.program_id(0)
        for d in range(1, len(grid)):
            step = step * grid[d] + pl.program_id(d)
        last = int(np.prod(grid)) - 1

        @pl.when(step == 0)
        def _():
            side.start(sins, souts, sscr)

        body(*ins, *outs, *scr)

        @pl.when(step == min(int(relay_at * last), last))
        def _():
            side.relay(sins, souts, sscr)

        @pl.when(step == last)
        def _():
            side.finish(sins, souts, sscr)

    outs = pl.pallas_call(
        hosted, name=name, grid=grid, in_specs=list(in_specs) + [ANY] * si, out_specs=list(out_specs) + [ANY] * so,
        out_shape=list(out_shape) + side.out_shape, scratch_shapes=list(scratch_shapes) + side.scratch,
        input_output_aliases=aliases or {},
        compiler_params=pltpu.CompilerParams(dimension_semantics=sem, vmem_limit_bytes=VMEM_LIMIT, has_side_effects=True),
    )(*args, *side.inputs)
    return outs[:no], outs[no:]


def _remote_only(side, name):
    n = len(side.inputs)

    def body(*refs):
        src, dst, sems = refs[:n], refs[n:n + len(side.out_shape)], refs[n + len(side.out_shape):]
        side.start(src, dst, sems)
        side.relay(src, dst, sems)
        side.finish(src, dst, sems)

    return pl.pallas_call(
        body, name=name, in_specs=[ANY] * n, out_specs=[ANY] * len(side.out_shape), out_shape=side.out_shape,
        scratch_shapes=side.scratch, compiler_params=pltpu.CompilerParams(has_side_effects=True))(*side.inputs)


def _rms(x, g):
    r = lax.rsqrt(jnp.mean(x * x, axis=-1, keepdims=True) + EPS)
    return x * r * g


def _rms_bwd(dy, x, g):
    r = lax.rsqrt(jnp.mean(x * x, axis=-1, keepdims=True) + EPS)
    xh = x * r
    dxh = dy * g
    dx = r * (dxh - xh * jnp.mean(dxh * xh, axis=-1, keepdims=True))
    return dx, jnp.sum(dy * xh, axis=0, keepdims=True)


def _halves(n):
    return [slice(0, n // 2), slice(n // 2, n)] if n % 256 == 0 else [slice(0, n)]


def _zero_at_first_step(acc_ref):
    first = pl.program_id(0) == 0

    @pl.when(first)
    def _():
        acc_ref[...] = jnp.zeros_like(acc_ref)


ROW1 = lambda: _resident((1, D), lambda i: (0, 0))
LROWS = lambda l, n=1: _resident((None, n, D), lambda *_: (l, 0, 0))
ACC = lambda: pl.BlockSpec((8, D), lambda i: (0, 0))


def _inproj_fwd(x, g, wint, l, tm, side=None, relay_at=1.0):
    T = x.shape[0]

    def body(x_ref, g_ref, w_ref, gt_ref, q_ref, k_ref, v_ref, h_ref):
        for rows in _halves(tm):
            h = _rms(x_ref[rows, :], g_ref[...]).astype(BF)
            h_ref[rows, :] = h
            q_ref[:, rows] = _dot(w_ref[0:NQ, :], h, NT).astype(BF)
            k_ref[:, rows] = _dot(w_ref[NQ:NQ + NKV, :], h, NT).astype(BF)
            v_ref[:, rows] = _dot(w_ref[NQ + NKV:NQ + 2 * NKV, :], h, NT).astype(BF)
            for s in range(5):
                lo = NQ + 2 * NKV + s * D
                gt_ref[rows, s * D:(s + 1) * D] = _dot(h, w_ref[lo:lo + D, :], NT).astype(BF)

    tok = lambda w: pl.BlockSpec((tm, w), lambda i: (i, 0))
    feat = lambda w: pl.BlockSpec((w, tm), lambda i: (0, i))
    return _call(
        body, name=f"inproj_fwd{l}", grid=(T // tm,),
        in_specs=[tok(D), LROWS(l), _resident((NP, D), lambda i: (0, 0))],
        out_specs=[tok(NG), feat(NQ), feat(NKV), feat(NKV), tok(D)],
        out_shape=[jax.ShapeDtypeStruct((T, NG), BF)] + [jax.ShapeDtypeStruct((w, T), BF) for w in (NQ, NKV, NKV)]
        + [jax.ShapeDtypeStruct((T, D), BF)],
        args=(x, g, wint), side=side, relay_at=relay_at)


def _band_geometry():
    j = lax.broadcasted_iota(jnp.int32, (2 * WIN, WIN), 0)
    r = lax.broadcasted_iota(jnp.int32, (2 * WIN, WIN), 1)
    dist = WIN + r - j
    dist0 = r - j
    return (dist.astype(F32), (dist >= 0) & (dist < WIN)), (dist0.astype(F32), dist0 >= 0)


def _pair_biases(sm_ref, pj):
    return [[jnp.where(ok, -sm_ref[1, pj * 2 * GQ + h] * dist, NEG) for h in range(2 * GQ)]
            for dist, ok in _band_geometry()]


def _reduce_rows(x, pair, whole):
    while x.shape[0] > 8:
        half = x.shape[0] // 2
        x = pair(x[:half], x[half:])
    return whole(x, axis=0, keepdims=True)


def _heads_on_lanes(ref, kvh, r0):
    return jnp.concatenate([ref[(kvh * GQ + g) * HD:(kvh * GQ + g + 1) * HD, pl.ds(r0, WIN)] for g in range(GQ)], axis=1)


def _band_probs(sm_ref, head0, q_ref, k_ref, r0, p0, kvh, biases):
    qt = _heads_on_lanes(q_ref, kvh, r0) * jnp.asarray(HD ** -0.5, BF)
    kt = k_ref[kvh * HD:(kvh + 1) * HD, pl.ds(p0, 2 * WIN)]
    heads = []
    for g in range(GQ):
        sink = sm_ref[0, head0 + g]
        s = _dot(kt, qt[:, g * WIN:(g + 1) * WIN], TN) + biases[kvh * GQ + g]
        m = jnp.maximum(_reduce_rows(s, jnp.maximum, jnp.max), sink)
        p = jnp.exp(s - m)
        ps = jnp.exp(sink - m)
        heads.append((p, ps, 1.0 / (_reduce_rows(p, jnp.add, jnp.sum) + ps)))
    return qt, kt, heads


def _attn_fwd(q, k, v, sm, l, S, side=None, relay_at=1.0):
    T = q.shape[1]
    nblk = S // WIN
    unroll = next(u for u in (5, 3, 1) if (nblk - 1) % u == 0)

    def body(sm_all_ref, q_ref, k_ref, v_ref, o_ref):
        sm_ref = sm_all_ref.at[l]
        pj = pl.program_id(1)
        biases = _pair_biases(sm_ref, pj)

        def block(i, first):
            r0 = 0 if first else pl.multiple_of(i * WIN, WIN)
            p0 = 0 if first else pl.multiple_of(i * WIN - WIN, WIN)
            for kvh in range(2):
                head0 = (2 * pj + kvh) * GQ
                _, _, heads = _band_probs(sm_ref, head0, q_ref, k_ref, r0, p0, kvh, biases[first])
                vt = v_ref[kvh * HD:(kvh + 1) * HD, pl.ds(p0, 2 * WIN)]
                for g, (p, _, inv) in enumerate(heads):
                    rows = slice((kvh * GQ + g) * HD, (kvh * GQ + g + 1) * HD)
                    o_ref[rows, pl.ds(r0, WIN)] = (_dot(vt, p.astype(BF), NN) * inv).astype(BF)

        block(0, True)

        def rest(t, c):
            for u in range(unroll):
                block(1 + unroll * t + u, False)
            return c

        lax.fori_loop(0, (nblk - 1) // unroll, rest, 0)

    wide = lambda: pl.BlockSpec((2 * GQ * HD, S), lambda s, p: (p, s))
    narrow = lambda: pl.BlockSpec((2 * HD, S), lambda s, p: (p, s))
    return _call(
        body, name=f"attn_fwd{l}", grid=(T // S, 2),
        in_specs=[pl.BlockSpec(memory_space=pltpu.SMEM), wide(), narrow(), narrow()],
        out_specs=[wide()], out_shape=[jax.ShapeDtypeStruct((D, T), BF)],
        args=(sm, q, k, v), side=side, relay_at=relay_at)


def _shift_rows(y, k, edge_rows, down):
    n = y.shape[0]
    rid = lax.broadcasted_iota(jnp.int32, y.shape, 0)
    out = pltpu.roll(y, k if down else n - k, 0)
    for t, row in enumerate(edge_rows):
        out = jnp.where(rid == (t if down else n - k + t), row, out)
    return out


def _mixer_fwd(x, gt, att, w3, bg, cw, cbias, l, S, tm):
    T = x.shape[0]

    def body(x_ref, cb_ref, cc_ref, cu_ref, ga_ref, gc_ref, cch_ref, cuh_ref, att_ref, wao_ref, wco_ref, wo_ref,
             bg_ref, cw_ref, cbias_ref, x1_ref, mg_ref, co_ref, ya_ref, yc_ref, zb_ref):
        first = (pl.program_id(0) * tm) % S == 0
        y = cc_ref[...].astype(F32) * cu_ref[...].astype(F32)
        hy1 = cch_ref[15:16, :].astype(F32) * cuh_ref[15:16, :].astype(F32)
        hy2 = cch_ref[14:15, :].astype(F32) * cuh_ref[14:15, :].astype(F32)
        hy1, hy2 = jnp.where(first, 0.0, hy1), jnp.where(first, 0.0, hy2)
        z = (cw_ref[0:1, :] * _shift_rows(y, 2, [hy2, hy1], True) + cw_ref[1:2, :] * _shift_rows(y, 1, [hy1], True)
             + cw_ref[2:3, :] * y)
        zb = z + cbias_ref[...]
        zb_ref[...] = zb.astype(BF)
        co = (cb_ref[...].astype(F32) * zb).astype(BF)
        co_ref[...] = co
        yc = _dot(co, wco_ref[...], NN)
        ya = _dot(att_ref[...], wao_ref[...], TN)
        ya_ref[...] = ya.astype(BF)
        yc_ref[...] = yc.astype(BF)
        sa = jax.nn.sigmoid(ga_ref[...].astype(F32) + bg_ref[0:1, :])
        sc = jax.nn.sigmoid(gc_ref[...].astype(F32) + bg_ref[1:2, :])
        mg = (sa * ya + sc * yc).astype(BF)
        mg_ref[...] = mg
        x1_ref[...] = x_ref[...] + _dot(mg, wo_ref[...], NN)

    tok = lambda: pl.BlockSpec((tm, D), lambda i: (i, 0))
    seg = lambda s: pl.BlockSpec((tm, D), lambda i: (i, s))
    halo = lambda s: pl.BlockSpec((16, D), lambda i: (jnp.maximum(i * (tm // 16) - 1, 0), s))
    wsp = lambda k: _resident((None, D, D), lambda i: (k, 0, 0))
    return _call(
        body, name=f"mixer_fwd{l}", grid=(T // tm,),
        in_specs=[tok(), seg(0), seg(1), seg(2), seg(3), seg(4), halo(1), halo(2),
                  pl.BlockSpec((D, tm), lambda i: (0, i)), wsp(0), wsp(1), wsp(2), LROWS(l, 2), LROWS(l, 8), LROWS(l)],
        out_specs=[tok()] * 6,
        out_shape=[jax.ShapeDtypeStruct((T, D), dt) for dt in (F32, BF, BF, BF, BF, BF)],
        args=(x, gt, gt, gt, gt, gt, gt, gt, att, w3, w3, w3, bg, cw, cbias))


def _mlp_fwd(x1, g, w2, l, tm, head=None, side=None, relay_at=1.0):
    T = x1.shape[0]
    nt = T // tm
    FC = 1024

    def body(x_ref, g_ref, wup_ref, wdn_ref, *rest):
        out_ref, h_ref, a_ref = rest[-4:-1] if head else rest
        if head:
            t_ref, gf_ref, acc_ref = rest[0], rest[1], rest[-1]
            _zero_at_first_step(acc_ref)
        for rows in _halves(tm):
            x = x_ref[rows, :]
            h = _rms(x, g_ref[...]).astype(BF)
            h_ref[rows, :] = h
            acc = x
            for c in range(F // FC):
                a = _dot(h, wup_ref[c * FC:(c + 1) * FC, :], NT)
                a_ref[rows, c * FC:(c + 1) * FC] = a.astype(BF)
                u = jnp.maximum(a, 0.0)
                acc = acc + _dot((u * u).astype(BF), wdn_ref[c * FC:(c + 1) * FC, :], NN)
            if not head:
                out_ref[rows, :] = acc
                continue
            gf = gf_ref[...]
            err = _rms(acc, gf) - t_ref[rows, :]
            out_ref[rows, :], dg = _rms_bwd(err * (1.0 / D), acc, gf)
            acc_ref[0:1, :] += dg
            acc_ref[1:2, :] += jnp.sum(err * err, axis=0, keepdims=True)
        if not head:
            return

        @pl.when(pl.program_id(0) == nt - 1)
        def _():
            acc_ref[1:2, :] = jnp.zeros((1, D), F32) + (0.5 / D) * jnp.sum(acc_ref[1:2, :])

    tok = lambda w: pl.BlockSpec((tm, w), lambda i: (i, 0))
    wsp = lambda k: _resident((None, F, D), lambda i: (k, 0, 0))
    return _call(
        body, name=f"mlp_fwd{l}", grid=(nt,),
        in_specs=[tok(D), LROWS(l), wsp(0), wsp(1)] + ([tok(D), ROW1()] if head else []),
        out_specs=[tok(D), tok(D), tok(F)] + ([ACC()] if head else []),
        out_shape=[jax.ShapeDtypeStruct((T, D), F32), jax.ShapeDtypeStruct((T, D), BF), jax.ShapeDtypeStruct((T, F), BF)]
        + ([jax.ShapeDtypeStruct((8, D), F32)] if head else []),
        args=(x1, g, w2, w2) + (tuple(head) if head else ()), side=side, relay_at=relay_at)


def _mlp_bwd(dx2, x1, a, g, w2, l, tm, side=None):
    T = dx2.shape[0]
    FC = 1024

    def body(d_ref, x_ref, a_ref, g_ref, wup_ref, wdn_ref, da_ref, dx1_ref, db_ref, acc_ref):
        _zero_at_first_step(acc_ref)
        for rows in _halves(tm):
            d = d_ref[rows, :]
            db = d.astype(BF)
            db_ref[rows, :] = db
            dh = jnp.zeros(d.shape, F32)
            for c in range(F // FC):
                du = _dot(db, wdn_ref[c * FC:(c + 1) * FC, :], NT)
                da = (du * (2.0 * jnp.maximum(a_ref[rows, c * FC:(c + 1) * FC].astype(F32), 0.0))).astype(BF)
                da_ref[rows, c * FC:(c + 1) * FC] = da
                dh = dh + _dot(da, wup_ref[c * FC:(c + 1) * FC, :], NN)
            dx, dg = _rms_bwd(dh, x_ref[rows, :], g_ref[...])
            dx1_ref[rows, :] = d + dx
            acc_ref[0:1, :] += dg

    tok = lambda w: pl.BlockSpec((tm, w), lambda i: (i, 0))
    wsp = lambda k: _resident((None, F, D), lambda i: (k, 0, 0))
    return _call(
        body, name=f"mlp_bwd{l}", grid=(T // tm,),
        in_specs=[tok(D), tok(D), tok(F), LROWS(l), wsp(0), wsp(1)],
        out_specs=[tok(F), tok(D), tok(D), ACC()],
        out_shape=[jax.ShapeDtypeStruct((T, F), BF), jax.ShapeDtypeStruct((T, D), F32),
                   jax.ShapeDtypeStruct((T, D), BF), jax.ShapeDtypeStruct((8, D), F32)],
        args=(dx2, x1, a, g, w2, w2), side=side)


def _mixer_bwd(dx1, gt, ya, yc, zb, w3, bg, cw, l, S, tm, side=None):
    T = dx1.shape[0]
    nt = T // tm

    def body(d_ref, cb_ref, cc_ref, cu_ref, ga_ref, gc_ref, ya_ref, yc_ref, zb_ref, wao_ref, wco_ref, wo_ref,
             bg_ref, cw_ref, dg_ref, datt_ref, dya_ref, dyc_ref, db_ref, acc_ref, carry_ref):
        ti = nt - 1 - pl.program_id(0)
        _zero_at_first_step(acc_ref)

        @pl.when(((ti + 1) * tm) % S == 0)
        def _():
            carry_ref[...] = jnp.zeros_like(carry_ref)

        db = d_ref[...].astype(BF)
        db_ref[...] = db
        dm = _dot(db, wo_ref[...], NT)
        sa = jax.nn.sigmoid(ga_ref[...].astype(F32) + bg_ref[0:1, :])
        sc = jax.nn.sigmoid(gc_ref[...].astype(F32) + bg_ref[1:2, :])
        dya32 = dm * sa
        dyc32 = dm * sc
        dya = dya32.astype(BF)
        dyc = dyc32.astype(BF)
        dya_ref[...] = dya
        dyc_ref[...] = dyc
        dga = dya32 * ya_ref[...].astype(F32) * (1.0 - sa)
        dgc = dyc32 * yc_ref[...].astype(F32) * (1.0 - sc)
        dg_ref[:, 3 * D:4 * D] = dga.astype(BF)
        dg_ref[:, 4 * D:5 * D] = dgc.astype(BF)
        acc_ref[0:1, :] += jnp.sum(dga, axis=0, keepdims=True)
        acc_ref[1:2, :] += jnp.sum(dgc, axis=0, keepdims=True)
        datt_ref[...] = _dot(wao_ref[...], dya, NT).astype(BF)
        dco = _dot(dyc, wco_ref[...], NT)

        cc = cc_ref[...].astype(F32)
        cu = cu_ref[...].astype(F32)
        y = cc * cu
        dg_ref[:, 0:D] = (dco * zb_ref[...].astype(F32)).astype(BF)
        dz = dco * cb_ref[...].astype(F32)
        u1 = _shift_rows(dz, 1, [carry_ref[0:1, :]], False)
        u2 = _shift_rows(dz, 2, [carry_ref[0:1, :], carry_ref[1:2, :]], False)
        acc_ref[2:3, :] += jnp.sum(dz, axis=0, keepdims=True)
        acc_ref[3:4, :] += jnp.sum(u2 * y, axis=0, keepdims=True)
        acc_ref[4:5, :] += jnp.sum(u1 * y, axis=0, keepdims=True)
        acc_ref[5:6, :] += jnp.sum(dz * y, axis=0, keepdims=True)
        dy = cw_ref[2:3, :] * dz + cw_ref[1:2, :] * u1 + cw_ref[0:1, :] * u2
        dg_ref[:, D:2 * D] = (dy * cu).astype(BF)
        dg_ref[:, 2 * D:3 * D] = (dy * cc).astype(BF)
        carry_ref[...] = dz[0:8, :]

    tok = lambda w=D: pl.BlockSpec((tm, w), lambda i: (nt - 1 - i, 0))
    seg = lambda s: pl.BlockSpec((tm, D), lambda i: (nt - 1 - i, s))
    wsp = lambda k: _resident((None, D, D), lambda i: (k, 0, 0))
    return _call(
        body, name=f"mixer_bwd{l}", grid=(nt,),
        in_specs=[tok(), seg(0), seg(1), seg(2), seg(3), seg(4), tok(), tok(), tok(), wsp(0), wsp(1), wsp(2),
                  LROWS(l, 2), LROWS(l, 8)],
        out_specs=[tok(NG), pl.BlockSpec((D, tm), lambda i: (0, nt - 1 - i)), tok(), tok(), tok(), ACC()],
        out_shape=[jax.ShapeDtypeStruct((T, NG), BF), jax.ShapeDtypeStruct((D, T), BF)]
        + [jax.ShapeDtypeStruct((T, D), BF)] * 3 + [jax.ShapeDtypeStruct((8, D), F32)],
        scratch_shapes=[pltpu.VMEM((8, D), F32)],
        args=(dx1, gt, gt, gt, gt, gt, ya, yc, zb, w3, w3, w3, bg, cw), side=side)


def _attn_bwd(q, k, v, att, datt, sm, l, S, side=None):
    T = q.shape[1]
    nblk = S // WIN
    unroll = next(u for u in (5, 3, 1) if (nblk - 1) % u == 0)
    scale = HD ** -0.5

    def body(sm_all_ref, q_ref, k_ref, v_ref, o_ref, do_ref, dq_ref, dk_ref, dv_ref, ds_ref, dka_ref, dva_ref):
        sm_ref = sm_all_ref.at[l]
        pj = pl.program_id(1)
        biases = _pair_biases(sm_ref, pj)
        dka_ref[...] = jnp.zeros_like(dka_ref)
        dva_ref[...] = jnp.zeros_like(dva_ref)

        def block(i, first, dsinks):
            r0 = 0 if first else pl.multiple_of(i * WIN, WIN)
            p0 = 0 if first else pl.multiple_of(i * WIN - WIN, WIN)
            out = []
            for kvh in range(2):
                head0 = (2 * pj + kvh) * GQ
                rows = slice(kvh * HD, (kvh + 1) * HD)
                dob = _heads_on_lanes(do_ref, kvh, r0)
                vt = v_ref[rows, pl.ds(p0, 2 * WIN)]
                dpts = [_dot(vt, dob[:, g * WIN:(g + 1) * WIN], TN) for g in range(GQ)]
                qt, kt, heads = _band_probs(sm_ref, head0, q_ref, k_ref, r0, p0, kvh, biases[first])
                inv = jnp.concatenate([h[2] for h in heads], axis=1)
                do32 = dob.astype(F32)
                delta = jnp.sum(do32 * _heads_on_lanes(o_ref, kvh, r0).astype(F32), axis=0, keepdims=True) * inv
                dosb = (do32 * inv).astype(BF)
                dsts = [(p * (dpts[g] * h_inv - delta[:, g * WIN:(g + 1) * WIN])).astype(BF)
                        for g, (p, _, h_inv) in enumerate(heads)]
                for g in range(GQ):
                    hr = slice((kvh * GQ + g) * HD, (kvh * GQ + g + 1) * HD)
                    dq_ref[hr, pl.ds(r0, WIN)] = (_dot(kt, dsts[g], NN) * scale).astype(BF)
                dst = jnp.concatenate(dsts, axis=1)
                pt = jnp.concatenate([p.astype(BF) for p, _, _ in heads], axis=1)
                dka_ref[rows, pl.ds(p0, 2 * WIN)] += _dot(qt, dst, NT)
                dva_ref[rows, pl.ds(p0, 2 * WIN)] += _dot(dosb, pt, NT)
                ps = jnp.concatenate([h[1] for h in heads], axis=1)
                out.append(dsinks[kvh] - ps * delta)
            return tuple(out)

        zero = jnp.zeros((1, GQ * WIN), F32)
        def rest(t, c):
            for u in range(unroll):
                c = block(1 + unroll * t + u, False, c)
            return c

        dsinks = lax.fori_loop(0, (nblk - 1) // unroll, rest, block(0, True, (zero, zero)))
        for kvh in range(2):
            for g in range(GQ):
                tot = jnp.sum(dsinks[kvh][:, g * WIN:(g + 1) * WIN])
                ds_ref[kvh * GQ + g:kvh * GQ + g + 1, :] = jnp.zeros((1, 128), F32) + tot
        dk_ref[...] = dka_ref[...].astype(BF)
        dv_ref[...] = dva_ref[...].astype(BF)

    wide = lambda: pl.BlockSpec((2 * GQ * HD, S), lambda s, p: (p, s))
    narrow = lambda: pl.BlockSpec((2 * HD, S), lambda s, p: (p, s))
    return _call(
        body, name=f"attn_bwd{l}", grid=(T // S, 2),
        in_specs=[pl.BlockSpec(memory_space=pltpu.SMEM), wide(), narrow(), narrow(), wide(), wide()],
        out_specs=[wide(), narrow(), narrow(), pl.BlockSpec((None, None, 8, 128), lambda s, p: (s, p, 0, 0))],
        out_shape=[jax.ShapeDtypeStruct((NQ, T), BF), jax.ShapeDtypeStruct((NKV, T), BF),
                   jax.ShapeDtypeStruct((NKV, T), BF), jax.ShapeDtypeStruct((T // S, 2, 8, 128), F32)],
        scratch_shapes=[pltpu.VMEM((2 * HD, S), F32), pltpu.VMEM((2 * HD, S), F32)],
        args=(sm, q, k, v, att, datt), side=side)


def _inproj_bwd(dgt, dq, dk, dv, x, dres, g, wint, l, tm, side=None):
    T = x.shape[0]

    def body(dg_ref, dq_ref, dk_ref, dv_ref, x_ref, dr_ref, g_ref, w_ref, dx_ref, acc_ref):
        _zero_at_first_step(acc_ref)
        for rows in _halves(tm):
            dh = _dot(dq_ref[:, rows], w_ref[0:NQ, :], TN)
            dh = dh + _dot(dk_ref[:, rows], w_ref[NQ:NQ + NKV, :], TN)
            dh = dh + _dot(dv_ref[:, rows], w_ref[NQ + NKV:NQ + 2 * NKV, :], TN)
            dh = dh + _dot(dg_ref[rows, :], w_ref[NQ + 2 * NKV:NP, :], NN)
            dx, dg = _rms_bwd(dh, x_ref[rows, :], g_ref[...])
            dx_ref[rows, :] = dr_ref[rows, :] + dx
            acc_ref[0:1, :] += dg

    tok = lambda w: pl.BlockSpec((tm, w), lambda i: (i, 0))
    feat = lambda w: pl.BlockSpec((w, tm), lambda i: (0, i))
    return _call(
        body, name=f"inproj_bwd{l}", grid=(T // tm,),
        in_specs=[tok(NG), feat(NQ), feat(NKV), feat(NKV), tok(D), tok(D), LROWS(l), _resident((NP, D), lambda i: (0, 0))],
        out_specs=[tok(D), ACC()],
        out_shape=[jax.ShapeDtypeStruct((T, D), F32), jax.ShapeDtypeStruct((8, D), F32)],
        args=(dgt, dq, dk, dv, x, dres, g, wint), side=side)


def _wgrad(a, b, rows, row0, into, name, relu2=False, a_is_transposed=False):
    M, T = a.shape if a_is_transposed else a.shape[::-1]
    tmm = next(t for t in (1024, 512, 256) if M % t == 0 and row0 % t == 0)
    tk = min(4096 if M > 4096 else 2048, T)
    nk = T // tk
    blk0 = row0 // tmm

    def body(*refs):
        a_ref, b_ref = refs[0], refs[1]
        o_ref, acc_ref = refs[-2], refs[-1]
        kk = pl.program_id(1)

        @pl.when(kk == 0)
        def _():
            acc_ref[...] = jnp.zeros_like(acc_ref)

        av = a_ref[...]
        if relu2:
            t = jnp.maximum(av.astype(F32), 0.0)
            av = (t * t).astype(BF)
        acc_ref[...] += _dot(av, b_ref[...], NN if a_is_transposed else TN)

        @pl.when(kk == nk - 1)
        def _():
            o_ref[...] = acc_ref[...].astype(BF)

    a_spec = pl.BlockSpec((tmm, tk), lambda j, kk: (j, kk)) if a_is_transposed else pl.BlockSpec((tk, tmm), lambda j, kk: (kk, j))
    in_specs = [a_spec, pl.BlockSpec((tk, D), lambda j, kk: (kk, 0))]
    args = [a, b]
    if into is not None:
        in_specs.append(ANY)
        args.append(into)
    (out,), _ = _call(
        body, name=name, grid=(M // tmm, nk), in_specs=in_specs,
        out_specs=[pl.BlockSpec((tmm, D), lambda j, kk: (blk0 + j, 0))],
        out_shape=[jax.ShapeDtypeStruct((rows, D), BF)], scratch_shapes=[pltpu.VMEM((tmm, D), F32)],
        aliases={2: 0} if into is not None else None, args=args)
    return out


def _wgrad_square(jobs, name):
    n = len(jobs)
    T = jobs[0][1].shape[0]
    tk = min(1024, T)
    nk = T // tk

    def body(*refs):
        ins, outs, acc_ref = refs[:2 * n], refs[2 * n:3 * n], refs[3 * n]
        j, kk = pl.program_id(0), pl.program_id(1)

        @pl.when(kk == 0)
        def _():
            acc_ref[...] = jnp.zeros_like(acc_ref)

        for p, (_, _, a_t) in enumerate(jobs):
            @pl.when(j == p)
            def _(p=p, a_t=a_t):
                acc_ref[...] += _dot(ins[2 * p][...], ins[2 * p + 1][...], NN if a_t else TN)

                @pl.when(kk == nk - 1)
                def _():
                    outs[p][...] = acc_ref[...].astype(BF)

    def step(p):
        return lambda j, kk: jnp.where(j == p, kk, jnp.where(j > p, nk - 1, 0))

    in_specs, args = [], []
    for p, (a, b, a_t) in enumerate(jobs):
        at = step(p)
        in_specs.append(pl.BlockSpec((D, tk), lambda j, kk, at=at: (0, at(j, kk))) if a_t
                        else pl.BlockSpec((tk, D), lambda j, kk, at=at: (at(j, kk), 0)))
        in_specs.append(pl.BlockSpec((tk, D), lambda j, kk, at=at: (at(j, kk), 0)))
        args += [a, b]
    outs, _ = _call(
        body, name=name, grid=(n, nk), in_specs=in_specs,
        out_specs=[pl.BlockSpec((D, D), lambda j, kk: (0, 0))] * n,
        out_shape=[jax.ShapeDtypeStruct((D, D), BF)] * n, scratch_shapes=[pltpu.VMEM((D, D), F32)], args=args)
    return outs


def _adamw(w, g, m, v):
    m = B1 * m + (1.0 - B1) * g
    v = B2 * v + (1.0 - B2) * (g * g)
    m_hat = m / (1.0 - B1 ** STEP)
    v_hat = v / (1.0 - B2 ** STEP)
    return -LR * (m_hat / (jnp.sqrt(v_hat) + AEPS) + WD * w), m, v


def _adam_sum(land, w, m, v, l, into, name, transposed=False):
    _, r, _ = land.shape
    tr = 208 if r % 208 == 0 else (256 if r % 256 == 0 else r)
    tc = 256

    def body(land_ref, w_ref, m_ref, v_ref, *rest):
        g_ref, d_ref, nm_ref, nv_ref = rest[-4:]
        g = land_ref[0].astype(F32)
        for s in range(1, NDEV):
            g = g + land_ref[s].astype(F32)
        if transposed:
            g = g.T
        g_ref[...] = g
        d_ref[...], nm_ref[...], nv_ref[...] = _adamw(w_ref[...], g, m_ref[...], v_ref[...])

    if transposed:
        blk = lambda: pl.BlockSpec((None, tc, r), lambda j: (l, j, 0))
        land_spec, grid = pl.BlockSpec((NDEV, r, tc), lambda j: (0, 0, j)), (D // tc,)
    else:
        blk = lambda: pl.BlockSpec((None, tr, D), lambda j: (l, j, 0))
        land_spec, grid = pl.BlockSpec((NDEV, tr, D), lambda j: (0, j, 0)), (r // tr,)
    in_specs = [land_spec, blk(), blk(), blk()]
    args = [land, w, m, v]
    aliases = None
    if into is not None:
        in_specs += [ANY] * 4
        args += list(into)
        aliases = {4 + t: t for t in range(4)}
    outs, _ = _call(body, name=name, grid=grid, in_specs=in_specs, out_specs=[blk()] * 4,
                    out_shape=[jax.ShapeDtypeStruct(w.shape, F32)] * 4, aliases=aliases, args=args)
    return outs


SMALL_NAMES = ("g_mix", "b_gates", "sinks", "conv_b", "g_mlp", "g_final", "conv_w")


def _adam_small(land, me, masters):
    n = len(SMALL_NAMES)
    lanes = D // NDEV

    def body(land_ref, me_ref, *refs):
        ins, outs, loss_ref, gs_ref = refs[:3 * n], refs[3 * n:7 * n], refs[7 * n], refs[7 * n + 1]
        g = land_ref[0]
        for s in range(1, NDEV):
            g = g + land_ref[s]
        gs_ref[...] = g

        def update(k, g_piece, idx):
            w_ref, m_ref, v_ref = ins[3 * k:3 * k + 3]
            outs[4 * k][idx] = g_piece
            outs[4 * k + 1][idx], outs[4 * k + 2][idx], outs[4 * k + 3][idx] = _adamw(w_ref[idx], g_piece, m_ref[idx], v_ref[idx])

        whole = (slice(None), slice(None))
        update(0, gs_ref[0:2, :], whole)
        for l in range(L):
            for h in range(2):
                update(1, gs_ref[2 + 2 * l + h:3 + 2 * l + h, :], (slice(l, l + 1), slice(h * D, (h + 1) * D)))
            update(2, gs_ref[6:7, 16 * l:16 * (l + 1)], (slice(l, l + 1), slice(None)))
        update(3, gs_ref[7:9, :], whole)
        update(4, gs_ref[9:11, :], whole)
        update(5, gs_ref[11:12, :], whole)
        mine = pl.ds(pl.multiple_of(me_ref[0] * lanes, lanes), lanes)
        for l in range(L):
            for k in range(3):
                update(6, gs_ref[12 + 3 * l + k:13 + 3 * l + k, mine], (l, slice(k, k + 1), slice(None)))
        loss_ref[...] = gs_ref[18:19, 0:1]

    flat = [t for name in SMALL_NAMES for t in masters[name]]
    vmem = pl.BlockSpec(memory_space=pltpu.VMEM)
    outs = pl.pallas_call(
        body, name="adam_small",
        in_specs=[vmem, pl.BlockSpec(memory_space=pltpu.SMEM)] + [vmem] * len(flat),
        out_shape=[jax.ShapeDtypeStruct(masters[name][0].shape, F32) for name in SMALL_NAMES for _ in range(4)]
        + [jax.ShapeDtypeStruct((1, 1), F32)],
        scratch_shapes=[pltpu.VMEM((SMALL_ROWS, D), F32)],
        compiler_params=pltpu.CompilerParams(vmem_limit_bytes=VMEM_LIMIT))(land, me, *flat)
    return {name: outs[4 * k:4 * k + 4] for k, name in enumerate(SMALL_NAMES)}, outs[-1]


def _pack_small(g_mix, b_gates, sinks, conv_b, g_mlp, g_final, conv_w_rows, extra):
    sink_row = jnp.zeros((1, D), F32).at[0, :2 * 16].set(sinks.reshape(-1))
    return jnp.concatenate([g_mix, b_gates.reshape(4, D), sink_row, conv_b, g_mlp, g_final.reshape(1, D),
                            conv_w_rows, extra, jnp.zeros((SMALL_ROWS - 19, D), F32)], axis=0)


def kernel(x, g_mix, w_in, b_gates, sinks, w_attn_out, conv_w, conv_b, w_conv_out, w_o, g_mlp, w_up, w_down, g_final, loss_target, m_g_mix, m_w_in, m_b_gates, m_sinks, m_w_attn_out, m_conv_w, m_conv_b, m_w_conv_out, m_w_o, m_g_mlp, m_w_up, m_w_down, m_g_final, v_g_mix, v_w_in, v_b_gates, v_sinks, v_w_attn_out, v_conv_w, v_conv_b, v_w_conv_out, v_w_o, v_g_mlp, v_w_up, v_w_down, v_g_final):
    nseq, S, _ = x.shape
    T = nseq * S
    tm_in = min(512, S)
    tm = min(512, S)
    xi, yi, ci = _position()
    me = 4 * xi + 2 * yi + ci
    tr = lambda t: jnp.swapaxes(t, 1, 2)
    blocks = lambda t: t.reshape(NDEV, t.shape[0] // NDEV, D)

    win_t, wup_t = tr(w_in), tr(w_up)
    sh_win = [win_t[l].astype(BF)[None] for l in range(L)]
    sh_w3 = [jnp.stack([w_attn_out[l], w_conv_out[l], w_o[l]]).astype(BF) for l in range(L)]
    sh_w2 = [jnp.stack([wup_t[l], w_down[l]]).astype(BF) for l in range(L)]
    wint, w3, w2 = [None] * L, [None] * L, [None] * L
    wint0_g, cw_g = _remote_only(_Gather([sh_win[0], jnp.pad(conv_w, ((0, 0), (0, 5), (0, 0)))]), "gather_first")
    wint[0] = wint0_g.reshape(NP, D)
    cw = jnp.swapaxes(cw_g, 1, 2).reshape(L, 8, D)
    slopes = np.power(np.float32(2.0), -8.0 * np.arange(1, 17, dtype=np.float32) / 16).astype(np.float32)
    sm = jnp.stack([sinks, jnp.broadcast_to(jnp.asarray(slopes), sinks.shape)], axis=1)
    per_layer = lambda t, n=1: t.reshape(L, n, D)
    g_mix3, g_mlp3, conv_b3, bg = per_layer(g_mix), per_layer(g_mlp), per_layer(conv_b), per_layer(b_gates, 2)

    xf = x.reshape(T, D)
    saved = []
    cur = xf
    for l in range(L):
        (gt, q, k, v, h), got = _inproj_fwd(cur, g_mix3, wint[l], l, tm_in, side=_Gather([sh_w2[l]]),
                                            relay_at=0.85)
        w2[l] = got[0].reshape(2, F, D)
        (att,), got = _attn_fwd(q, k, v, sm, l, S, side=_Gather([sh_w3[l]]), relay_at=0.5)
        w3[l] = got[0].reshape(3, D, D)
        (x1, mg, co, ya, yc, zb), _ = _mixer_fwd(cur, gt, att, w3[l], bg, cw, conv_b3, l, S, tm)
        head = (loss_target.reshape(T, D), g_final.reshape(1, D)) if l == L - 1 else None
        (nxt, h2, a, *acc_loss), got = _mlp_fwd(x1, g_mlp3, w2[l], l, tm_in, head=head,
                                                side=_Gather([sh_win[l + 1]]) if l + 1 < L else None, relay_at=0.7)
        if l + 1 < L:
            wint[l + 1] = got[0].reshape(NP, D)
        saved.append(dict(x=cur, gt=gt, q=q, k=k, v=v, h=h, att=att, x1=x1, mg=mg, co=co, ya=ya, yc=yc, zb=zb, h2=h2, a=a))
        cur = nxt
    dcur, acc_loss = cur, acc_loss[0]

    masters = {"w_in": (win_t, tr(m_w_in), tr(v_w_in)), "w_attn_out": (w_attn_out, m_w_attn_out, v_w_attn_out),
               "w_conv_out": (w_conv_out, m_w_conv_out, v_w_conv_out), "w_o": (w_o, m_w_o, v_w_o),
               "w_up": (w_up, m_w_up, v_w_up), "w_down": (w_down, m_w_down, v_w_down)}
    big = {name: None for name in masters}

    def adam(name, land, l):
        big[name] = _adam_sum(land, *masters[name], l, big[name], f"adam_{name}{l}", transposed=name == "w_up")

    acc_in, acc_mix, acc_mlp, dsink = [None] * L, [None] * L, [None] * L, [None] * L
    d_win_above = None
    for l in reversed(range(L)):
        sv = saved[l]
        side = _Exchange([blocks(d_win_above)]) if l + 1 < L else None
        (da, dx1, dx2b, acc_mlp[l]), got = _mlp_bwd(dcur, sv["x1"], sv["a"], g_mlp3, w2[l], l, tm_in, side=side)
        if l + 1 < L:
            adam("w_in", got[0], l + 1)
        d_wdn = _wgrad(sv["a"], dx2b, F, 0, None, f"wgrad_down{l}", relu2=True)
        d_wup = _wgrad(da, sv["h2"], F, 0, None, f"wgrad_up{l}")
        (dgt, datt, dya, dyc, dx1b, acc_mix[l]), got = _mixer_bwd(
            dx1, sv["gt"], sv["ya"], sv["yc"], sv["zb"], w3[l], bg, cw, l, S, tm,
            side=_Exchange([blocks(d_wdn)]))
        adam("w_down", got[0], l)
        d_wo, d_wao, d_wco = _wgrad_square(
            [(sv["mg"], dx1b, False), (sv["att"], dya, True), (sv["co"], dyc, False)], f"wgrad_mixer{l}")
        d_win = _wgrad(dgt, sv["h"], NP, NQ + 2 * NKV, None, f"wgrad_gates{l}")
        (dq, dk, dv, dsink[l]), got = _attn_bwd(
            sv["q"], sv["k"], sv["v"], sv["att"], datt, sm, l, S,
            side=_Exchange([blocks(d_wup), blocks(d_wo), blocks(d_wao), blocks(d_wco)]))
        for name, land in zip(["w_up", "w_o", "w_attn_out", "w_conv_out"], got):
            adam(name, land, l)
        d_win = _wgrad(dq, sv["h"], NP, 0, d_win, f"wgrad_q{l}", a_is_transposed=True)
        d_win = _wgrad(dk, sv["h"], NP, NQ, d_win, f"wgrad_k{l}", a_is_transposed=True)
        d_win = _wgrad(dv, sv["h"], NP, NQ + NKV, d_win, f"wgrad_v{l}", a_is_transposed=True)
        side = _Exchange([blocks(d_win)]) if l == 0 else None
        (dcur, acc_in[l]), got_in = _inproj_bwd(dgt, dq, dk, dv, sv["x"], dx1, g_mix3, wint[l], l, tm_in, side=side)
        d_win_above = d_win

    dsinks = jnp.stack([dsink[l][:, :, :, 0].sum(axis=0).reshape(16) for l in range(L)])
    small = _pack_small(
        jnp.concatenate([acc_in[l][0:1] for l in range(L)]),
        jnp.stack([acc_mix[l][0:2].reshape(2 * D) for l in range(L)]),
        dsinks,
        jnp.concatenate([acc_mix[l][2:3] for l in range(L)]),
        jnp.concatenate([acc_mlp[l][0:1] for l in range(L)]),
        acc_loss[0],
        jnp.concatenate([acc_mix[l][3:6] for l in range(L)]),
        acc_loss[1:2])
    adam("w_in", got_in[0], 0)
    (small_land,) = _remote_only(_Exchange([], [small]), "exchange_small")

    row = lambda t: t.reshape(1, D)
    small_out, loss = _adam_small(small_land, me.reshape(1).astype(jnp.int32), {
        "g_mix": (g_mix, m_g_mix, v_g_mix), "b_gates": (b_gates, m_b_gates, v_b_gates), "sinks": (sinks, m_sinks, v_sinks),
        "conv_b": (conv_b, m_conv_b, v_conv_b), "g_mlp": (g_mlp, m_g_mlp, v_g_mlp),
        "g_final": (row(g_final), row(m_g_final), row(v_g_final)), "conv_w": (conv_w, m_conv_w, v_conv_w)})
    small_out["g_final"] = [t.reshape(D) for t in small_out["g_final"]]
    big["w_in"] = [tr(o) for o in big["w_in"]]
    order = ["g_mix", "w_in", "b_gates", "sinks", "w_attn_out", "conv_w", "conv_b", "w_conv_out", "w_o", "g_mlp",
             "w_up", "w_down", "g_final"]
    out = [loss.reshape(()), dcur.reshape(nseq, S, D)]
    for kind in range(4):
        for name in order:
            out.append(big[name][kind] if name in big else small_out[name][kind])
    return tuple(out)
```

```python
import numpy as np
import jax
import jax.numpy as jnp
from jax import lax
from jax.experimental import pallas as pl
from jax.experimental.pallas import tpu as pltpu

D = 1024
NG = 5 * D
NQ = 1024
NKV = 256
NP = NQ + 2 * NKV + NG
F = 4096
HD = 64
GQ = 4
WIN = 128
L = 2
NDEV = 8
EPS = 1e-6
NEG = -1e30
SMALL_ROWS = 24
LR, B1, B2, AEPS, WD, STEP = 0.001, 0.9, 0.999, 1e-08, 0.01, 10

BF = jnp.bfloat16
F32 = jnp.float32
MESH = pl.DeviceIdType.MESH
VMEM_LIMIT = 60 * 1024 * 1024
ANY = pl.BlockSpec(memory_space=pl.ANY)

NN = ((1,), (0,))
NT = ((1,), (1,))
TN = ((0,), (0,))


def _dot(a, b, dims):
    return lax.dot_general(a, b, (dims, ((), ())), preferred_element_type=F32)


def _resident(shape, imap):
    return pl.BlockSpec(shape, imap, pipeline_mode=pl.Buffered(1))


def _position():
    return lax.axis_index("x"), lax.axis_index("y"), lax.axis_index("c")


class _Gather:
    def __init__(self, shards):
        n = len(shards)
        self.inputs = list(shards)
        self.out_shape = [jax.ShapeDtypeStruct((s.shape[0], NDEV) + s.shape[1:], s.dtype) for s in shards]
        self.scratch = [pltpu.SemaphoreType.DMA((n, 7)), pltpu.SemaphoreType.DMA((n, 7)), pltpu.SemaphoreType.DMA((n,))]

    def _plan(self, src, dst, sems):
        send_sems, recv_sems, local_sems = sems
        n = len(src)
        x, y, c = _position()
        me, sibling = (x, y, c), (x, y, 1 - c)
        chips = [(1 - x, y), (x, 1 - y), (1 - x, 1 - y)]

        def rows(a, p):
            return dst[a].at[:, 4 * p[0] + 2 * p[1] + p[2]]

        def copy(a, k, block, to, from_src=False):
            return pltpu.make_async_remote_copy(
                src_ref=src[a] if from_src else rows(a, block), dst_ref=rows(a, block),
                send_sem=send_sems.at[a, k], recv_sem=recv_sems.at[a, k], device_id=to, device_id_type=MESH)

        mine = [pltpu.make_async_copy(src[a], rows(a, me), local_sems.at[a]) for a in range(n)]
        first = []
        for a in range(n):
            first.append(copy(a, 0, me, sibling, True))
            first += [copy(a, 1 + t, me, (*chip, c), True) for t, chip in enumerate(chips)]
        return n, c, me, sibling, chips, copy, mine, first

    def start(self, src, dst, sems):
        *_, mine, first = self._plan(src, dst, sems)
        for cp in mine + first:
            cp.start()

    def relay(self, src, dst, sems):
        n, c, me, sibling, chips, copy, _, _ = self._plan(src, dst, sems)
        for t, chip in enumerate(chips):
            for a in range(n):
                copy(a, 1 + t, (*chip, c), me).wait_recv()
                copy(a, 4 + t, (*chip, c), sibling).start()

    def finish(self, src, dst, sems):
        n, c, me, sibling, chips, copy, mine, first = self._plan(src, dst, sems)
        for a in range(n):
            copy(a, 0, sibling, me).wait_recv()
            for t, chip in enumerate(chips):
                copy(a, 4 + t, (*chip, 1 - c), me).wait_recv()
        for cp in first + [copy(a, 4 + t, (*chip, c), sibling) for t, chip in enumerate(chips) for a in range(n)]:
            cp.wait_send()
        for cp in mine:
            cp.wait()


class _Exchange:
    def __init__(self, grads, everyone=()):
        self.inputs = list(grads) + list(everyone)
        self.n_blocked = len(grads)
        n = len(self.inputs)
        self.out_shape = [jax.ShapeDtypeStruct(g.shape, g.dtype) for g in grads]
        self.out_shape += [jax.ShapeDtypeStruct((NDEV,) + e.shape, e.dtype) for e in everyone]
        self.scratch = [pltpu.SemaphoreType.DMA((n, 7)), pltpu.SemaphoreType.DMA((n, 7)), pltpu.SemaphoreType.DMA((n,))]

    def _plan(self, src, land, sems):
        send_sems, recv_sems, local_sems = sems
        x, y, c = _position()
        me = 4 * x + 2 * y + c

        def parts(peer_idx):
            return [(s.at[peer_idx] if a < self.n_blocked else s, land[a].at[me]) for a, s in enumerate(src)]

        local = [pltpu.make_async_copy(s, d, local_sems.at[a]) for a, (s, d) in enumerate(parts(me))]
        sent = []
        for rel in range(1, NDEV):
            px = 1 - x if rel & 4 else x
            py = 1 - y if rel & 2 else y
            pc = 1 - c if rel & 1 else c
            for a, (s, d) in enumerate(parts(4 * px + 2 * py + pc)):
                sent.append(pltpu.make_async_remote_copy(
                    src_ref=s, dst_ref=d, send_sem=send_sems.at[a, rel - 1], recv_sem=recv_sems.at[a, rel - 1],
                    device_id=(px, py, pc), device_id_type=MESH))
        return local, sent

    def start(self, src, land, sems):
        local, sent = self._plan(src, land, sems)
        for cp in local + sent:
            cp.start()

    def relay(self, src, land, sems):
        pass

    def finish(self, src, land, sems):
        local, sent = self._plan(src, land, sems)
        for cp in sent:
            cp.wait_recv()
        for cp in sent:
            cp.wait_send()
        for cp in local:
            cp.wait()


def _call(body, *, name, grid, in_specs, out_specs, out_shape, args, scratch_shapes=(), aliases=None, side=None,
          relay_at=1.0):
    sem = ("arbitrary",) * len(grid)
    if side is None:
        outs = pl.pallas_call(
            body, name=name, grid=grid, in_specs=in_specs, out_specs=out_specs, out_shape=out_shape,
            scratch_shapes=list(scratch_shapes), input_output_aliases=aliases or {},
            compiler_params=pltpu.CompilerParams(dimension_semantics=sem, vmem_limit_bytes=VMEM_LIMIT))(*args)
        return outs, []
    ni, no, ns = len(in_specs), len(out_specs), len(scratch_shapes)
    si, so = len(side.inputs), len(side.out_shape)

    def hosted(*refs):
        ins, refs = refs[:ni], refs[ni:]
        sins, refs = refs[:si], refs[si:]
        outs, refs = refs[:no], refs[no:]
        souts, refs = refs[:so], refs[so:]
        scr, sscr = refs[:ns], refs[ns:]
        step = pl.program_id(0)
        for d in range(1, len(grid)):
            step = step * grid[d] + pl.program_id(d)
        last = int(np.prod(grid)) - 1

        @pl.when(step == 0)
        def _():
            side.start(sins, souts, sscr)

        body(*ins, *outs, *scr)

        @pl.when(step == min(int(relay_at * last), last))
        def _():
            side.relay(sins, souts, sscr)

        @pl.when(step == last)
        def _():
            side.finish(sins, souts, sscr)

    outs = pl.pallas_call(
        hosted, name=name, grid=grid, in_specs=list(in_specs) + [ANY] * si, out_specs=list(out_specs) + [ANY] * so,
        out_shape=list(out_shape) + side.out_shape, scratch_shapes=list(scratch_shapes) + side.scratch,
        input_output_aliases=aliases or {},
        compiler_params=pltpu.CompilerParams(dimension_semantics=sem, vmem_limit_bytes=VMEM_LIMIT, has_side_effects=True),
    )(*args, *side.inputs)
    return outs[:no], outs[no:]


def _remote_only(side, name):
    n = len(side.inputs)

    def body(*refs):
        src, dst, sems = refs[:n], refs[n:n + len(side.out_shape)], refs[n + len(side.out_shape):]
        side.start(src, dst, sems)
        side.relay(src, dst, sems)
        side.finish(src, dst, sems)

    return pl.pallas_call(
        body, name=name, in_specs=[ANY] * n, out_specs=[ANY] * len(side.out_shape), out_shape=side.out_shape,
        scratch_shapes=side.scratch, compiler_params=pltpu.CompilerParams(has_side_effects=True))(*side.inputs)


def _rms(x, g):
    r = lax.rsqrt(jnp.mean(x * x, axis=-1, keepdims=True) + EPS)
    return x * r * g


def _rms_bwd(dy, x, g):
    r = lax.rsqrt(jnp.mean(x * x, axis=-1, keepdims=True) + EPS)
    xh = x * r
    dxh = dy * g
    dx = r * (dxh - xh * jnp.mean(dxh * xh, axis=-1, keepdims=True))
    return dx, jnp.sum(dy * xh, axis=0, keepdims=True)


def _halves(n):
    return [slice(0, n // 2), slice(n // 2, n)] if n % 256 == 0 else [slice(0, n)]


def _zero_at_first_step(acc_ref):
    first = pl.program_id(0) == 0

    @pl.when(first)
    def _():
        acc_ref[...] = jnp.zeros_like(acc_ref)


ROW1 = lambda: _resident((1, D), lambda i: (0, 0))
LROWS = lambda l, n=1: _resident((None, n, D), lambda *_: (l, 0, 0))
ACC = lambda: pl.BlockSpec((8, D), lambda i: (0, 0))


def _inproj_fwd(x, g, wint, l, tm, side=None, relay_at=1.0):
    T = x.shape[0]

    def body(x_ref, g_ref, w_ref, gt_ref, q_ref, k_ref, v_ref, h_ref):
        for rows in _halves(tm):
            h = _rms(x_ref[rows, :], g_ref[...]).astype(BF)
            h_ref[rows, :] = h
            q_ref[:, rows] = _dot(w_ref[0:NQ, :], h, NT).astype(BF)
            k_ref[:, rows] = _dot(w_ref[NQ:NQ + NKV, :], h, NT).astype(BF)
            v_ref[:, rows] = _dot(w_ref[NQ + NKV:NQ + 2 * NKV, :], h, NT).astype(BF)
            for s in range(5):
                lo = NQ + 2 * NKV + s * D
                gt_ref[rows, s * D:(s + 1) * D] = _dot(h, w_ref[lo:lo + D, :], NT).astype(BF)

    tok = lambda w: pl.BlockSpec((tm, w), lambda i: (i, 0))
    feat = lambda w: pl.BlockSpec((w, tm), lambda i: (0, i))
    return _call(
        body, name=f"inproj_fwd{l}", grid=(T // tm,),
        in_specs=[tok(D), LROWS(l), _resident((NP, D), lambda i: (0, 0))],
        out_specs=[tok(NG), feat(NQ), feat(NKV), feat(NKV), tok(D)],
        out_shape=[jax.ShapeDtypeStruct((T, NG), BF)] + [jax.ShapeDtypeStruct((w, T), BF) for w in (NQ, NKV, NKV)]
        + [jax.ShapeDtypeStruct((T, D), BF)],
        args=(x, g, wint), side=side, relay_at=relay_at)


def _band_geometry():
    j = lax.broadcasted_iota(jnp.int32, (2 * WIN, WIN), 0)
    r = lax.broadcasted_iota(jnp.int32, (2 * WIN, WIN), 1)
    dist = WIN + r - j
    dist0 = r - j
    return (dist.astype(F32), (dist >= 0) & (dist < WIN)), (dist0.astype(F32), dist0 >= 0)


def _pair_biases(sm_ref, pj):
    return [[jnp.where(ok, -sm_ref[1, pj * 2 * GQ + h] * dist, NEG) for h in range(2 * GQ)]
            for dist, ok in _band_geometry()]


def _reduce_rows(x, pair, whole):
    while x.shape[0] > 8:
        half = x.shape[0] // 2
        x = pair(x[:half], x[half:])
    return whole(x, axis=0, keepdims=True)


def _heads_on_lanes(ref, kvh, r0):
    return jnp.concatenate([ref[(kvh * GQ + g) * HD:(kvh * GQ + g + 1) * HD, pl.ds(r0, WIN)] for g in range(GQ)], axis=1)


def _band_probs(sm_ref, head0, q_ref, k_ref, r0, p0, kvh, biases):
    qt = _heads_on_lanes(q_ref, kvh, r0) * jnp.asarray(HD ** -0.5, BF)
    kt = k_ref[kvh * HD:(kvh + 1) * HD, pl.ds(p0, 2 * WIN)]
    heads = []
    for g in range(GQ):
        sink = sm_ref[0, head0 + g]
        s = _dot(kt, qt[:, g * WIN:(g + 1) * WIN], TN) + biases[kvh * GQ + g]
        m = jnp.maximum(_reduce_rows(s, jnp.maximum, jnp.max), sink)
        p = jnp.exp(s - m)
        ps = jnp.exp(sink - m)
        heads.append((p, ps, 1.0 / (_reduce_rows(p, jnp.add, jnp.sum) + ps)))
    return qt, kt, heads


def _attn_fwd(q, k, v, sm, l, S, side=None, relay_at=1.0):
    T = q.shape[1]
    nblk = S // WIN
    unroll = next(u for u in (5, 3, 1) if (nblk - 1) % u == 0)

    def body(sm_all_ref, q_ref, k_ref, v_ref, o_ref):
        sm_ref = sm_all_ref.at[l]
        pj = pl.program_id(1)
        biases = _pair_biases(sm_ref, pj)

        def block(i, first):
            r0 = 0 if first else pl.multiple_of(i * WIN, WIN)
            p0 = 0 if first else pl.multiple_of(i * WIN - WIN, WIN)
            for kvh in range(2):
                head0 = (2 * pj + kvh) * GQ
                _, _, heads = _band_probs(sm_ref, head0, q_ref, k_ref, r0, p0, kvh, biases[first])
                vt = v_ref[kvh * HD:(kvh + 1) * HD, pl.ds(p0, 2 * WIN)]
                for g, (p, _, inv) in enumerate(heads):
                    rows = slice((kvh * GQ + g) * HD, (kvh * GQ + g + 1) * HD)
                    o_ref[rows, pl.ds(r0, WIN)] = (_dot(vt, p.astype(BF), NN) * inv).astype(BF)

        block(0, True)

        def rest(t, c):
            for u in range(unroll):
                block(1 + unroll * t + u, False)
            return c

        lax.fori_loop(0, (nblk - 1) // unroll, rest, 0)

    wide = lambda: pl.BlockSpec((2 * GQ * HD, S), lambda s, p: (p, s))
    narrow = lambda: pl.BlockSpec((2 * HD, S), lambda s, p: (p, s))
    return _call(
        body, name=f"attn_fwd{l}", grid=(T // S, 2),
        in_specs=[pl.BlockSpec(memory_space=pltpu.SMEM), wide(), narrow(), narrow()],
        out_specs=[wide()], out_shape=[jax.ShapeDtypeStruct((D, T), BF)],
        args=(sm, q, k, v), side=side, relay_at=relay_at)


def _shift_rows(y, k, edge_rows, down):
    n = y.shape[0]
    rid = lax.broadcasted_iota(jnp.int32, y.shape, 0)
    out = pltpu.roll(y, k if down else n - k, 0)
    for t, row in enumerate(edge_rows):
        out = jnp.where(rid == (t if down else n - k + t), row, out)
    return out


def _mixer_fwd(x, gt, att, w3, bg, cw, cbias, l, S, tm):
    T = x.shape[0]

    def body(x_ref, cb_ref, cc_ref, cu_ref, ga_ref, gc_ref, cch_ref, cuh_ref, att_ref, wao_ref, wco_ref, wo_ref,
             bg_ref, cw_ref, cbias_ref, x1_ref, mg_ref, co_ref, ya_ref, yc_ref, zb_ref):
        first = (pl.program_id(0) * tm) % S == 0
        y = cc_ref[...].astype(F32) * cu_ref[...].astype(F32)
        hy1 = cch_ref[15:16, :].astype(F32) * cuh_ref[15:16, :].astype(F32)
        hy2 = cch_ref[14:15, :].astype(F32) * cuh_ref[14:15, :].astype(F32)
        hy1, hy2 = jnp.where(first, 0.0, hy1), jnp.where(first, 0.0, hy2)
        z = (cw_ref[0:1, :] * _shift_rows(y, 2, [hy2, hy1], True) + cw_ref[1:2, :] * _shift_rows(y, 1, [hy1], True)
             + cw_ref[2:3, :] * y)
        zb = z + cbias_ref[...]
        zb_ref[...] = zb.astype(BF)
        co = (cb_ref[...].astype(F32) * zb).astype(BF)
        co_ref[...] = co
        yc = _dot(co, wco_ref[...], NN)
        ya = _dot(att_ref[...], wao_ref[...], TN)
        ya_ref[...] = ya.astype(BF)
        yc_ref[...] = yc.astype(BF)
        sa = jax.nn.sigmoid(ga_ref[...].astype(F32) + bg_ref[0:1, :])
        sc = jax.nn.sigmoid(gc_ref[...].astype(F32) + bg_ref[1:2, :])
        mg = (sa * ya + sc * yc).astype(BF)
        mg_ref[...] = mg
        x1_ref[...] = x_ref[...] + _dot(mg, wo_ref[...], NN)

    tok = lambda: pl.BlockSpec((tm, D), lambda i: (i, 0))
    seg = lambda s: pl.BlockSpec((tm, D), lambda i: (i, s))
    halo = lambda s: pl.BlockSpec((16, D), lambda i: (jnp.maximum(i * (tm // 16) - 1, 0), s))
    wsp = lambda k: _resident((None, D, D), lambda i: (k, 0, 0))
    return _call(
        body, name=f"mixer_fwd{l}", grid=(T // tm,),
        in_specs=[tok(), seg(0), seg(1), seg(2), seg(3), seg(4), halo(1), halo(2),
                  pl.BlockSpec((D, tm), lambda i: (0, i)), wsp(0), wsp(1), wsp(2), LROWS(l, 2), LROWS(l, 8), LROWS(l)],
        out_specs=[tok()] * 6,
        out_shape=[jax.ShapeDtypeStruct((T, D), dt) for dt in (F32, BF, BF, BF, BF, BF)],
        args=(x, gt, gt, gt, gt, gt, gt, gt, att, w3, w3, w3, bg, cw, cbias))


def _mlp_fwd(x1, g, w2, l, tm, head=None, side=None, relay_at=1.0):
    T = x1.shape[0]
    nt = T // tm
    FC = 1024

    def body(x_ref, g_ref, wup_ref, wdn_ref, *rest):
        out_ref, h_ref, a_ref = rest[-4:-1] if head else rest
        if head:
            t_ref, gf_ref, acc_ref = rest[0], rest[1], rest[-1]
            _zero_at_first_step(acc_ref)
        for rows in _halves(tm):
            x = x_ref[rows, :]
            h = _rms(x, g_ref[...]).astype(BF)
            h_ref[rows, :] = h
            acc = x
            for c in range(F // FC):
                a = _dot(h, wup_ref[c * FC:(c + 1) * FC, :], NT)
                a_ref[rows, c * FC:(c + 1) * FC] = a.astype(BF)
                u = jnp.maximum(a, 0.0)
                acc = acc + _dot((u * u).astype(BF), wdn_ref[c * FC:(c + 1) * FC, :], NN)
            if not head:
                out_ref[rows, :] = acc
                continue
            gf = gf_ref[...]
            err = _rms(acc, gf) - t_ref[rows, :]
            out_ref[rows, :], dg = _rms_bwd(err * (1.0 / D), acc, gf)
            acc_ref[0:1, :] += dg
            acc_ref[1:2, :] += jnp.sum(err * err, axis=0, keepdims=True)
        if not head:
            return

        @pl.when(pl.program_id(0) == nt - 1)
        def _():
            acc_ref[1:2, :] = jnp.zeros((1, D), F32) + (0.5 / D) * jnp.sum(acc_ref[1:2, :])

    tok = lambda w: pl.BlockSpec((tm, w), lambda i: (i, 0))
    wsp = lambda k: _resident((None, F, D), lambda i: (k, 0, 0))
    return _call(
        body, name=f"mlp_fwd{l}", grid=(nt,),
        in_specs=[tok(D), LROWS(l), wsp(0), wsp(1)] + ([tok(D), ROW1()] if head else []),
        out_specs=[tok(D), tok(D), tok(F)] + ([ACC()] if head else []),
        out_shape=[jax.ShapeDtypeStruct((T, D), F32), jax.ShapeDtypeStruct((T, D), BF), jax.ShapeDtypeStruct((T, F), BF)]
        + ([jax.ShapeDtypeStruct((8, D), F32)] if head else []),
        args=(x1, g, w2, w2) + (tuple(head) if head else ()), side=side, relay_at=relay_at)


def _mlp_bwd(dx2, x1, a, g, w2, l, tm, side=None):
    T = dx2.shape[0]
    FC = 1024

    def body(d_ref, x_ref, a_ref, g_ref, wup_ref, wdn_ref, da_ref, dx1_ref, db_ref, acc_ref):
        _zero_at_first_step(acc_ref)
        for rows in _halves(tm):
            d = d_ref[rows, :]
            db = d.astype(BF)
            db_ref[rows, :] = db
            dh = jnp.zeros(d.shape, F32)
            for c in range(F // FC):
                du = _dot(db, wdn_ref[c * FC:(c + 1) * FC, :], NT)
                da = (du * (2.0 * jnp.maximum(a_ref[rows, c * FC:(c + 1) * FC].astype(F32), 0.0))).astype(BF)
                da_ref[rows, c * FC:(c + 1) * FC] = da
                dh = dh + _dot(da, wup_ref[c * FC:(c + 1) * FC, :], NN)
            dx, dg = _rms_bwd(dh, x_ref[rows, :], g_ref[...])
            dx1_ref[rows, :] = d + dx
            acc_ref[0:1, :] += dg

    tok = lambda w: pl.BlockSpec((tm, w), lambda i: (i, 0))
    wsp = lambda k: _resident((None, F, D), lambda i: (k, 0, 0))
    return _call(
        body, name=f"mlp_bwd{l}", grid=(T // tm,),
        in_specs=[tok(D), tok(D), tok(F), LROWS(l), wsp(0), wsp(1)],
        out_specs=[tok(F), tok(D), tok(D), ACC()],
        out_shape=[jax.ShapeDtypeStruct((T, F), BF), jax.ShapeDtypeStruct((T, D), F32),
                   jax.ShapeDtypeStruct((T, D), BF), jax.ShapeDtypeStruct((8, D), F32)],
        args=(dx2, x1, a, g, w2, w2), side=side)


def _mixer_bwd(dx1, gt, ya, yc, zb, w3, bg, cw, l, S, tm, side=None):
    T = dx1.shape[0]
    nt = T // tm

    def body(d_ref, cb_ref, cc_ref, cu_ref, ga_ref, gc_ref, ya_ref, yc_ref, zb_ref, wao_ref, wco_ref, wo_ref,
             bg_ref, cw_ref, dg_ref, datt_ref, dya_ref, dyc_ref, db_ref, acc_ref, carry_ref):
        ti = nt - 1 - pl.program_id(0)
        _zero_at_first_step(acc_ref)

        @pl.when(((ti + 1) * tm) % S == 0)
        def _():
            carry_ref[...] = jnp.zeros_like(carry_ref)

        db = d_ref[...].astype(BF)
        db_ref[...] = db
        dm = _dot(db, wo_ref[...], NT)
        sa = jax.nn.sigmoid(ga_ref[...].astype(F32) + bg_ref[0:1, :])
        sc = jax.nn.sigmoid(gc_ref[...].astype(F32) + bg_ref[1:2, :])
        dya32 = dm * sa
        dyc32 = dm * sc
        dya = dya32.astype(BF)
        dyc = dyc32.astype(BF)
        dya_ref[...] = dya
        dyc_ref[...] = dyc
        dga = dya32 * ya_ref[...].astype(F32) * (1.0 - sa)
        dgc = dyc32 * yc_ref[...].astype(F32) * (1.0 - sc)
        dg_ref[:, 3 * D:4 * D] = dga.astype(BF)
        dg_ref[:, 4 * D:5 * D] = dgc.astype(BF)
        acc_ref[0:1, :] += jnp.sum(dga, axis=0, keepdims=True)
        acc_ref[1:2, :] += jnp.sum(dgc, axis=0, keepdims=True)
        datt_ref[...] = _dot(wao_ref[...], dya, NT).astype(BF)
        dco = _dot(dyc, wco_ref[...], NT)

        cc = cc_ref[...].astype(F32)
        cu = cu_ref[...].astype(F32)
        y = cc * cu
        dg_ref[:, 0:D] = (dco * zb_ref[...].astype(F32)).astype(BF)
        dz = dco * cb_ref[...].astype(F32)
        u1 = _shift_rows(dz, 1, [carry_ref[0:1, :]], False)
        u2 = _shift_rows(dz, 2, [carry_ref[0:1, :], carry_ref[1:2, :]], False)
        acc_ref[2:3, :] += jnp.sum(dz, axis=0, keepdims=True)
        acc_ref[3:4, :] += jnp.sum(u2 * y, axis=0, keepdims=True)
        acc_ref[4:5, :] += jnp.sum(u1 * y, axis=0, keepdims=True)
        acc_ref[5:6, :] += jnp.sum(dz * y, axis=0, keepdims=True)
        dy = cw_ref[2:3, :] * dz + cw_ref[1:2, :] * u1 + cw_ref[0:1, :] * u2
        dg_ref[:, D:2 * D] = (dy * cu).astype(BF)
        dg_ref[:, 2 * D:3 * D] = (dy * cc).astype(BF)
        carry_ref[...] = dz[0:8, :]

    tok = lambda w=D: pl.BlockSpec((tm, w), lambda i: (nt - 1 - i, 0))
    seg = lambda s: pl.BlockSpec((tm, D), lambda i: (nt - 1 - i, s))
    wsp = lambda k: _resident((None, D, D), lambda i: (k, 0, 0))
    return _call(
        body, name=f"mixer_bwd{l}", grid=(nt,),
        in_specs=[tok(), seg(0), seg(1), seg(2), seg(3), seg(4), tok(), tok(), tok(), wsp(0), wsp(1), wsp(2),
                  LROWS(l, 2), LROWS(l, 8)],
        out_specs=[tok(NG), pl.BlockSpec((D, tm), lambda i: (0, nt - 1 - i)), tok(), tok(), tok(), ACC()],
        out_shape=[jax.ShapeDtypeStruct((T, NG), BF), jax.ShapeDtypeStruct((D, T), BF)]
        + [jax.ShapeDtypeStruct((T, D), BF)] * 3 + [jax.ShapeDtypeStruct((8, D), F32)],
        scratch_shapes=[pltpu.VMEM((8, D), F32)],
        args=(dx1, gt, gt, gt, gt, gt, ya, yc, zb, w3, w3, w3, bg, cw), side=side)


def _attn_bwd(q, k, v, att, datt, sm, l, S, side=None):
    T = q.shape[1]
    nblk = S // WIN
    unroll = next(u for u in (5, 3, 1) if (nblk - 1) % u == 0)
    scale = HD ** -0.5

    def body(sm_all_ref, q_ref, k_ref, v_ref, o_ref, do_ref, dq_ref, dk_ref, dv_ref, ds_ref, dka_ref, dva_ref):
        sm_ref = sm_all_ref.at[l]
        pj = pl.program_id(1)
        biases = _pair_biases(sm_ref, pj)
        dka_ref[...] = jnp.zeros_like(dka_ref)
        dva_ref[...] = jnp.zeros_like(dva_ref)

        def block(i, first, dsinks):
            r0 = 0 if first else pl.multiple_of(i * WIN, WIN)
            p0 = 0 if first else pl.multiple_of(i * WIN - WIN, WIN)
            out = []
            for kvh in range(2):
                head0 = (2 * pj + kvh) * GQ
                rows = slice(kvh * HD, (kvh + 1) * HD)
                dob = _heads_on_lanes(do_ref, kvh, r0)
                vt = v_ref[rows, pl.ds(p0, 2 * WIN)]
                dpts = [_dot(vt, dob[:, g * WIN:(g + 1) * WIN], TN) for g in range(GQ)]
                qt, kt, heads = _band_probs(sm_ref, head0, q_ref, k_ref, r0, p0, kvh, biases[first])
                inv = jnp.concatenate([h[2] for h in heads], axis=1)
                do32 = dob.astype(F32)
                delta = jnp.sum(do32 * _heads_on_lanes(o_ref, kvh, r0).astype(F32), axis=0, keepdims=True) * inv
                dosb = (do32 * inv).astype(BF)
                dsts = [(p * (dpts[g] * h_inv - delta[:, g * WIN:(g + 1) * WIN])).astype(BF)
                        for g, (p, _, h_inv) in enumerate(heads)]
                for g in range(GQ):
                    hr = slice((kvh * GQ + g) * HD, (kvh * GQ + g + 1) * HD)
                    dq_ref[hr, pl.ds(r0, WIN)] = (_dot(kt, dsts[g], NN) * scale).astype(BF)
                dst = jnp.concatenate(dsts, axis=1)
                pt = jnp.concatenate([p.astype(BF) for p, _, _ in heads], axis=1)
                dka_ref[rows, pl.ds(p0, 2 * WIN)] += _dot(qt, dst, NT)
                dva_ref[rows, pl.ds(p0, 2 * WIN)] += _dot(dosb, pt, NT)
                ps = jnp.concatenate([h[1] for h in heads], axis=1)
                out.append(dsinks[kvh] - ps * delta)
            return tuple(out)

        zero = jnp.zeros((1, GQ * WIN), F32)
        def rest(t, c):
            for u in range(unroll):
                c = block(1 + unroll * t + u, False, c)
            return c

        dsinks = lax.fori_loop(0, (nblk - 1) // unroll, rest, block(0, True, (zero, zero)))
        for kvh in range(2):
            for g in range(GQ):
                tot = jnp.sum(dsinks[kvh][:, g * WIN:(g + 1) * WIN])
                ds_ref[kvh * GQ + g:kvh * GQ + g + 1, :] = jnp.zeros((1, 128), F32) + tot
        dk_ref[...] = dka_ref[...].astype(BF)
        dv_ref[...] = dva_ref[...].astype(BF)

    wide = lambda: pl.BlockSpec((2 * GQ * HD, S), lambda s, p: (p, s))
    narrow = lambda: pl.BlockSpec((2 * HD, S), lambda s, p: (p, s))
    return _call(
        body, name=f"attn_bwd{l}", grid=(T // S, 2),
        in_specs=[pl.BlockSpec(memory_space=pltpu.SMEM), wide(), narrow(), narrow(), wide(), wide()],
        out_specs=[wide(), narrow(), narrow(), pl.BlockSpec((None, None, 8, 128), lambda s, p: (s, p, 0, 0))],
        out_shape=[jax.ShapeDtypeStruct((NQ, T), BF), jax.ShapeDtypeStruct((NKV, T), BF),
                   jax.ShapeDtypeStruct((NKV, T), BF), jax.ShapeDtypeStruct((T // S, 2, 8, 128), F32)],
        scratch_shapes=[pltpu.VMEM((2 * HD, S), F32), pltpu.VMEM((2 * HD, S), F32)],
        args=(sm, q, k, v, att, datt), side=side)


def _inproj_bwd(dgt, dq, dk, dv, x, dres, g, wint, l, tm, side=None):
    T = x.shape[0]

    def body(dg_ref, dq_ref, dk_ref, dv_ref, x_ref, dr_ref, g_ref, w_ref, dx_ref, acc_ref):
        _zero_at_first_step(acc_ref)
        for rows in _halves(tm):
            dh = _dot(dq_ref[:, rows], w_ref[0:NQ, :], TN)
            dh = dh + _dot(dk_ref[:, rows], w_ref[NQ:NQ + NKV, :], TN)
            dh = dh + _dot(dv_ref[:, rows], w_ref[NQ + NKV:NQ + 2 * NKV, :], TN)
            dh = dh + _dot(dg_ref[rows, :], w_ref[NQ + 2 * NKV:NP, :], NN)
            dx, dg = _rms_bwd(dh, x_ref[rows, :], g_ref[...])
            dx_ref[rows, :] = dr_ref[rows, :] + dx
            acc_ref[0:1, :] += dg

    tok = lambda w: pl.BlockSpec((tm, w), lambda i: (i, 0))
    feat = lambda w: pl.BlockSpec((w, tm), lambda i: (0, i))
    return _call(
        body, name=f"inproj_bwd{l}", grid=(T // tm,),
        in_specs=[tok(NG), feat(NQ), feat(NKV), feat(NKV), tok(D), tok(D), LROWS(l), _resident((NP, D), lambda i: (0, 0))],
        out_specs=[tok(D), ACC()],
        out_shape=[jax.ShapeDtypeStruct((T, D), F32), jax.ShapeDtypeStruct((8, D), F32)],
        args=(dgt, dq, dk, dv, x, dres, g, wint), side=side)


def _wgrad(a, b, rows, row0, into, name, relu2=False):
    T, M = a.shape
    tmm = next(t for t in (1024, 512, 256) if M % t == 0 and row0 % t == 0)
    tk = min(4096 if M > 4096 else 2048, T)
    nk = T // tk
    blk0 = row0 // tmm

    def body(*refs):
        a_ref, b_ref = refs[0], refs[1]
        o_ref, acc_ref = refs[-2], refs[-1]
        kk = pl.program_id(1)

        @pl.when(kk == 0)
        def _():
            acc_ref[...] = jnp.zeros_like(acc_ref)

        av = a_ref[...]
        if relu2:
            t = jnp.maximum(av.astype(F32), 0.0)
            av = (t * t).astype(BF)
        acc_ref[...] += _dot(av, b_ref[...], TN)

        @pl.when(kk == nk - 1)
        def _():
            o_ref[...] = acc_ref[...].astype(BF)

    in_specs = [pl.BlockSpec((tk, tmm), lambda j, kk: (kk, j)), pl.BlockSpec((tk, D), lambda j, kk: (kk, 0))]
    args = [a, b]
    if into is not None:
        in_specs.append(ANY)
        args.append(into)
    (out,), _ = _call(
        body, name=name, grid=(M // tmm, nk), in_specs=in_specs,
        out_specs=[pl.BlockSpec((tmm, D), lambda j, kk: (blk0 + j, 0))],
        out_shape=[jax.ShapeDtypeStruct((rows, D), BF)], scratch_shapes=[pltpu.VMEM((tmm, D), F32)],
        aliases={2: 0} if into is not None else None, args=args)
    return out


def _wgrad_qkv(dq, dk, dv, h, into, name):
    T = h.shape[0]
    rows = NQ + 2 * NKV
    tk = min(2048, T)
    nk = T // tk

    def body(dq_ref, dk_ref, dv_ref, h_ref, _, o_ref, acc_ref):
        kk = pl.program_id(0)

        @pl.when(kk == 0)
        def _():
            acc_ref[...] = jnp.zeros_like(acc_ref)

        hv = h_ref[...]
        acc_ref[0:NQ, :] += _dot(dq_ref[...], hv, NN)
        acc_ref[NQ:NQ + NKV, :] += _dot(dk_ref[...], hv, NN)
        acc_ref[NQ + NKV:rows, :] += _dot(dv_ref[...], hv, NN)

        @pl.when(kk == nk - 1)
        def _():
            o_ref[...] = acc_ref[...].astype(BF)

    feat = lambda w: pl.BlockSpec((w, tk), lambda kk: (0, kk))
    (out,), _ = _call(
        body, name=name, grid=(nk,),
        in_specs=[feat(NQ), feat(NKV), feat(NKV), pl.BlockSpec((tk, D), lambda kk: (kk, 0)), ANY],
        out_specs=[pl.BlockSpec((rows, D), lambda kk: (0, 0))],
        out_shape=[jax.ShapeDtypeStruct((NP, D), BF)], scratch_shapes=[pltpu.VMEM((rows, D), F32)],
        aliases={4: 0}, args=[dq, dk, dv, h, into])
    return out


def _wgrad_square(jobs, name):
    n = len(jobs)
    T = jobs[0][1].shape[0]
    tk = min(1024, T)
    nk = T // tk

    def body(*refs):
        ins, outs, acc_ref = refs[:2 * n], refs[2 * n:3 * n], refs[3 * n]
        j, kk = pl.program_id(0), pl.program_id(1)

        @pl.when(kk == 0)
        def _():
            acc_ref[...] = jnp.zeros_like(acc_ref)

        for p, (_, _, a_t) in enumerate(jobs):
            @pl.when(j == p)
            def _(p=p, a_t=a_t):
                acc_ref[...] += _dot(ins[2 * p][...], ins[2 * p + 1][...], NN if a_t else TN)

                @pl.when(kk == nk - 1)
                def _():
                    outs[p][...] = acc_ref[...].astype(BF)

    def step(p):
        return lambda j, kk: jnp.where(j == p, kk, jnp.where(j > p, nk - 1, 0))

    in_specs, args = [], []
    for p, (a, b, a_t) in enumerate(jobs):
        at = step(p)
        in_specs.append(pl.BlockSpec((D, tk), lambda j, kk, at=at: (0, at(j, kk))) if a_t
                        else pl.BlockSpec((tk, D), lambda j, kk, at=at: (at(j, kk), 0)))
        in_specs.append(pl.BlockSpec((tk, D), lambda j, kk, at=at: (at(j, kk), 0)))
        args += [a, b]
    outs, _ = _call(
        body, name=name, grid=(n, nk), in_specs=in_specs,
        out_specs=[pl.BlockSpec((D, D), lambda j, kk: (0, 0))] * n,
        out_shape=[jax.ShapeDtypeStruct((D, D), BF)] * n, scratch_shapes=[pltpu.VMEM((D, D), F32)], args=args)
    return outs


def _adamw(w, g, m, v):
    m = B1 * m + (1.0 - B1) * g
    v = B2 * v + (1.0 - B2) * (g * g)
    m_hat = m / (1.0 - B1 ** STEP)
    v_hat = v / (1.0 - B2 ** STEP)
    return -LR * (m_hat / (jnp.sqrt(v_hat) + AEPS) + WD * w), m, v


def _adam_sum(land, w, m, v, l, into, name, transposed=False):
    _, r, _ = land.shape
    tr = 208 if r % 208 == 0 else (256 if r % 256 == 0 else r)
    tc = 256

    def body(land_ref, w_ref, m_ref, v_ref, *rest):
        g_ref, d_ref, nm_ref, nv_ref = rest[-4:]
        g = land_ref[0].astype(F32)
        for s in range(1, NDEV):
            g = g + land_ref[s].astype(F32)
        if transposed:
            g = g.T
        g_ref[...] = g
        d_ref[...], nm_ref[...], nv_ref[...] = _adamw(w_ref[...], g, m_ref[...], v_ref[...])

    if transposed:
        blk = lambda: pl.BlockSpec((None, tc, r), lambda j: (l, j, 0))
        land_spec, grid = pl.BlockSpec((NDEV, r, tc), lambda j: (0, 0, j)), (D // tc,)
    else:
        blk = lambda: pl.BlockSpec((None, tr, D), lambda j: (l, j, 0))
        land_spec, grid = pl.BlockSpec((NDEV, tr, D), lambda j: (0, j, 0)), (r // tr,)
    in_specs = [land_spec, blk(), blk(), blk()]
    args = [land, w, m, v]
    aliases = None
    if into is not None:
        in_specs += [ANY] * 4
        args += list(into)
        aliases = {4 + t: t for t in range(4)}
    outs, _ = _call(body, name=name, grid=grid, in_specs=in_specs, out_specs=[blk()] * 4,
                    out_shape=[jax.ShapeDtypeStruct(w.shape, F32)] * 4, aliases=aliases, args=args)
    return outs


SMALL_NAMES = ("g_mix", "b_gates", "sinks", "conv_b", "g_mlp", "g_final", "conv_w")


def _adam_small(land, me, masters):
    n = len(SMALL_NAMES)
    lanes = D // NDEV

    def body(land_ref, me_ref, *refs):
        ins, outs, loss_ref, gs_ref = refs[:3 * n], refs[3 * n:7 * n], refs[7 * n], refs[7 * n + 1]
        g = land_ref[0]
        for s in range(1, NDEV):
            g = g + land_ref[s]
        gs_ref[...] = g

        def update(k, g_piece, idx):
            w_ref, m_ref, v_ref = ins[3 * k:3 * k + 3]
            outs[4 * k][idx] = g_piece
            outs[4 * k + 1][idx], outs[4 * k + 2][idx], outs[4 * k + 3][idx] = _adamw(w_ref[idx], g_piece, m_ref[idx], v_ref[idx])

        whole = (slice(None), slice(None))
        update(0, gs_ref[0:2, :], whole)
        for l in range(L):
            for h in range(2):
                update(1, gs_ref[2 + 2 * l + h:3 + 2 * l + h, :], (slice(l, l + 1), slice(h * D, (h + 1) * D)))
            update(2, gs_ref[6:7, 16 * l:16 * (l + 1)], (slice(l, l + 1), slice(None)))
        update(3, gs_ref[7:9, :], whole)
        update(4, gs_ref[9:11, :], whole)
        update(5, gs_ref[11:12, :], whole)
        mine = pl.ds(pl.multiple_of(me_ref[0] * lanes, lanes), lanes)
        for l in range(L):
            for k in range(3):
                update(6, gs_ref[12 + 3 * l + k:13 + 3 * l + k, mine], (l, slice(k, k + 1), slice(None)))
        loss_ref[...] = gs_ref[18:19, 0:1]

    flat = [t for name in SMALL_NAMES for t in masters[name]]
    vmem = pl.BlockSpec(memory_space=pltpu.VMEM)
    outs = pl.pallas_call(
        body, name="adam_small",
        in_specs=[vmem, pl.BlockSpec(memory_space=pltpu.SMEM)] + [vmem] * len(flat),
        out_shape=[jax.ShapeDtypeStruct(masters[name][0].shape, F32) for name in SMALL_NAMES for _ in range(4)]
        + [jax.ShapeDtypeStruct((1, 1), F32)],
        scratch_shapes=[pltpu.VMEM((SMALL_ROWS, D), F32)],
        compiler_params=pltpu.CompilerParams(vmem_limit_bytes=VMEM_LIMIT))(land, me, *flat)
    return {name: outs[4 * k:4 * k + 4] for k, name in enumerate(SMALL_NAMES)}, outs[-1]


def _pack_small(g_mix, b_gates, sinks, conv_b, g_mlp, g_final, conv_w_rows, extra):
    sink_row = jnp.zeros((1, D), F32).at[0, :2 * 16].set(sinks.reshape(-1))
    return jnp.concatenate([g_mix, b_gates.reshape(4, D), sink_row, conv_b, g_mlp, g_final.reshape(1, D),
                            conv_w_rows, extra, jnp.zeros((SMALL_ROWS - 19, D), F32)], axis=0)


def kernel(x, g_mix, w_in, b_gates, sinks, w_attn_out, conv_w, conv_b, w_conv_out, w_o, g_mlp, w_up, w_down, g_final, loss_target, m_g_mix, m_w_in, m_b_gates, m_sinks, m_w_attn_out, m_conv_w, m_conv_b, m_w_conv_out, m_w_o, m_g_mlp, m_w_up, m_w_down, m_g_final, v_g_mix, v_w_in, v_b_gates, v_sinks, v_w_attn_out, v_conv_w, v_conv_b, v_w_conv_out, v_w_o, v_g_mlp, v_w_up, v_w_down, v_g_final):
    nseq, S, _ = x.shape
    T = nseq * S
    tm_in = min(512, S)
    tm = min(512, S)
    xi, yi, ci = _position()
    me = 4 * xi + 2 * yi + ci
    tr = lambda t: jnp.swapaxes(t, 1, 2)
    blocks = lambda t: t.reshape(NDEV, t.shape[0] // NDEV, D)

    win_t, wup_t = tr(w_in), tr(w_up)
    sh_win = [win_t[l].astype(BF)[None] for l in range(L)]
    sh_w3 = [jnp.stack([w_attn_out[l], w_conv_out[l], w_o[l]]).astype(BF) for l in range(L)]
    sh_w2 = [jnp.stack([wup_t[l], w_down[l]]).astype(BF) for l in range(L)]
    wint, w3, w2 = [None] * L, [None] * L, [None] * L
    wint0_g, cw_g = _remote_only(_Gather([sh_win[0], jnp.pad(conv_w, ((0, 0), (0, 5), (0, 0)))]), "gather_first")
    wint[0] = wint0_g.reshape(NP, D)
    cw = jnp.swapaxes(cw_g, 1, 2).reshape(L, 8, D)
    slopes = np.power(np.float32(2.0), -8.0 * np.arange(1, 17, dtype=np.float32) / 16).astype(np.float32)
    sm = jnp.stack([sinks, jnp.broadcast_to(jnp.asarray(slopes), sinks.shape)], axis=1)
    per_layer = lambda t, n=1: t.reshape(L, n, D)
    g_mix3, g_mlp3, conv_b3, bg = per_layer(g_mix), per_layer(g_mlp), per_layer(conv_b), per_layer(b_gates, 2)

    xf = x.reshape(T, D)
    saved = []
    cur = xf
    for l in range(L):
        (gt, q, k, v, h), got = _inproj_fwd(cur, g_mix3, wint[l], l, tm_in, side=_Gather([sh_w2[l]]),
                                            relay_at=0.85)
        w2[l] = got[0].reshape(2, F, D)
        (att,), got = _attn_fwd(q, k, v, sm, l, S, side=_Gather([sh_w3[l]]), relay_at=0.5)
        w3[l] = got[0].reshape(3, D, D)
        (x1, mg, co, ya, yc, zb), _ = _mixer_fwd(cur, gt, att, w3[l], bg, cw, conv_b3, l, S, tm)
        head = (loss_target.reshape(T, D), g_final.reshape(1, D)) if l == L - 1 else None
        (nxt, h2, a, *acc_loss), got = _mlp_fwd(x1, g_mlp3, w2[l], l, tm_in, head=head,
                                                side=_Gather([sh_win[l + 1]]) if l + 1 < L else None, relay_at=0.7)
        if l + 1 < L:
            wint[l + 1] = got[0].reshape(NP, D)
        saved.append(dict(x=cur, gt=gt, q=q, k=k, v=v, h=h, att=att, x1=x1, mg=mg, co=co, ya=ya, yc=yc, zb=zb, h2=h2, a=a))
        cur = nxt
    dcur, acc_loss = cur, acc_loss[0]

    masters = {"w_in": (win_t, tr(m_w_in), tr(v_w_in)), "w_attn_out": (w_attn_out, m_w_attn_out, v_w_attn_out),
               "w_conv_out": (w_conv_out, m_w_conv_out, v_w_conv_out), "w_o": (w_o, m_w_o, v_w_o),
               "w_up": (w_up, m_w_up, v_w_up), "w_down": (w_down, m_w_down, v_w_down)}
    big = {name: None for name in masters}

    def adam(name, land, l):
        big[name] = _adam_sum(land, *masters[name], l, big[name], f"adam_{name}{l}", transposed=name == "w_up")

    acc_in, acc_mix, acc_mlp, dsink = [None] * L, [None] * L, [None] * L, [None] * L
    d_win_above = None
    for l in reversed(range(L)):
        sv = saved[l]
        side = _Exchange([blocks(d_win_above)]) if l + 1 < L else None
        (da, dx1, dx2b, acc_mlp[l]), got = _mlp_bwd(dcur, sv["x1"], sv["a"], g_mlp3, w2[l], l, tm_in, side=side)
        if l + 1 < L:
            adam("w_in", got[0], l + 1)
        d_wdn = _wgrad(sv["a"], dx2b, F, 0, None, f"wgrad_down{l}", relu2=True)
        d_wup = _wgrad(da, sv["h2"], F, 0, None, f"wgrad_up{l}")
        (dgt, datt, dya, dyc, dx1b, acc_mix[l]), got = _mixer_bwd(
            dx1, sv["gt"], sv["ya"], sv["yc"], sv["zb"], w3[l], bg, cw, l, S, tm,
            side=_Exchange([blocks(d_wdn)]))
        adam("w_down", got[0], l)
        d_wo, d_wao, d_wco = _wgrad_square(
            [(sv["mg"], dx1b, False), (sv["att"], dya, True), (sv["co"], dyc, False)], f"wgrad_mixer{l}")
        d_win = _wgrad(dgt, sv["h"], NP, NQ + 2 * NKV, None, f"wgrad_gates{l}")
        (dq, dk, dv, dsink[l]), got = _attn_bwd(
            sv["q"], sv["k"], sv["v"], sv["att"], datt, sm, l, S,
            side=_Exchange([blocks(d_wup), blocks(d_wo), blocks(d_wao), blocks(d_wco)]))
        for name, land in zip(["w_up", "w_o", "w_attn_out", "w_conv_out"], got):
            adam(name, land, l)
        d_win = _wgrad_qkv(dq, dk, dv, sv["h"], d_win, f"wgrad_qkv{l}")
        side = _Exchange([blocks(d_win)]) if l == 0 else None
        (dcur, acc_in[l]), got_in = _inproj_bwd(dgt, dq, dk, dv, sv["x"], dx1, g_mix3, wint[l], l, tm_in, side=side)
        d_win_above = d_win

    dsinks = jnp.stack([dsink[l][:, :, :, 0].sum(axis=0).reshape(16) for l in range(L)])
    small = _pack_small(
        jnp.concatenate([acc_in[l][0:1] for l in range(L)]),
        jnp.stack([acc_mix[l][0:2].reshape(2 * D) for l in range(L)]),
        dsinks,
        jnp.concatenate([acc_mix[l][2:3] for l in range(L)]),
        jnp.concatenate([acc_mlp[l][0:1] for l in range(L)]),
        acc_loss[0],
        jnp.concatenate([acc_mix[l][3:6] for l in range(L)]),
        acc_loss[1:2])
    adam("w_in", got_in[0], 0)
    (small_land,) = _remote_only(_Exchange([], [small]), "exchange_small")

    row = lambda t: t.reshape(1, D)
    small_out, loss = _adam_small(small_land, me.reshape(1).astype(jnp.int32), {
        "g_mix": (g_mix, m_g_mix, v_g_mix), "b_gates": (b_gates, m_b_gates, v_b_gates), "sinks": (sinks, m_sinks, v_sinks),
        "conv_b": (conv_b, m_conv_b, v_conv_b), "g_mlp": (g_mlp, m_g_mlp, v_g_mlp),
        "g_final": (row(g_final), row(m_g_final), row(v_g_final)), "conv_w": (conv_w, m_conv_w, v_conv_w)})
    small_out["g_final"] = [t.reshape(D) for t in small_out["g_final"]]
    big["w_in"] = [tr(o) for o in big["w_in"]]
    order = ["g_mix", "w_in", "b_gates", "sinks", "w_attn_out", "conv_w", "conv_b", "w_conv_out", "w_o", "g_mlp",
             "w_up", "w_down", "g_final"]
    out = [loss.reshape(()), dcur.reshape(nseq, S, D)]
    for kind in range(4):
        for name in order:
            out.append(big[name][kind] if name in big else small_out[name][kind])
    return tuple(out)
```

```python
import numpy as np
import jax
import jax.numpy as jnp
from jax import lax
from jax.experimental import pallas as pl
from jax.experimental.pallas import tpu as pltpu

D = 1024
NG = 5 * D
NQ = 1024
NKV = 256
NP = NQ + 2 * NKV + NG
F = 4096
HD = 64
GQ = 4
WIN = 128
L = 2
NDEV = 8
EPS = 1e-6
NEG = -1e30
SMALL_ROWS = 24
LR, B1, B2, AEPS, WD, STEP = 0.001, 0.9, 0.999, 1e-08, 0.01, 10

BF = jnp.bfloat16
F32 = jnp.float32
MESH = pl.DeviceIdType.MESH
VMEM_LIMIT = 60 * 1024 * 1024
ANY = pl.BlockSpec(memory_space=pl.ANY)

NN = ((1,), (0,))
NT = ((1,), (1,))
TN = ((0,), (0,))


def _dot(a, b, dims):
    return lax.dot_general(a, b, (dims, ((), ())), preferred_element_type=F32)


def _resident(shape, imap):
    return pl.BlockSpec(shape, imap, pipeline_mode=pl.Buffered(1))


def _position():
    return lax.axis_index("x"), lax.axis_index("y"), lax.axis_index("c")


class _Gather:
    def __init__(self, shards):
        n = len(shards)
        self.inputs = list(shards)
        self.out_shape = [jax.ShapeDtypeStruct((s.shape[0], NDEV) + s.shape[1:], s.dtype) for s in shards]
        self.scratch = [pltpu.SemaphoreType.DMA((n, 7)), pltpu.SemaphoreType.DMA((n, 7)), pltpu.SemaphoreType.DMA((n,))]

    def _plan(self, src, dst, sems):
        send_sems, recv_sems, local_sems = sems
        n = len(src)
        x, y, c = _position()
        me, sibling = (x, y, c), (x, y, 1 - c)
        chips = [(1 - x, y), (x, 1 - y), (1 - x, 1 - y)]

        def rows(a, p):
            return dst[a].at[:, 4 * p[0] + 2 * p[1] + p[2]]

        def copy(a, k, block, to, from_src=False):
            return pltpu.make_async_remote_copy(
                src_ref=src[a] if from_src else rows(a, block), dst_ref=rows(a, block),
                send_sem=send_sems.at[a, k], recv_sem=recv_sems.at[a, k], device_id=to, device_id_type=MESH)

        mine = [pltpu.make_async_copy(src[a], rows(a, me), local_sems.at[a]) for a in range(n)]
        first = []
        for a in range(n):
            first.append(copy(a, 0, me, sibling, True))
            first += [copy(a, 1 + t, me, (*chip, c), True) for t, chip in enumerate(chips)]
        return n, c, me, sibling, chips, copy, mine, first

    def start(self, src, dst, sems):
        *_, mine, first = self._plan(src, dst, sems)
        for cp in mine + first:
            cp.start()

    def relay(self, src, dst, sems):
        n, c, me, sibling, chips, copy, _, _ = self._plan(src, dst, sems)
        for t, chip in enumerate(chips):
            for a in range(n):
                copy(a, 1 + t, (*chip, c), me).wait_recv()
                copy(a, 4 + t, (*chip, c), sibling).start()

    def finish(self, src, dst, sems):
        n, c, me, sibling, chips, copy, mine, first = self._plan(src, dst, sems)
        for a in range(n):
            copy(a, 0, sibling, me).wait_recv()
            for t, chip in enumerate(chips):
                copy(a, 4 + t, (*chip, 1 - c), me).wait_recv()
        for cp in first + [copy(a, 4 + t, (*chip, c), sibling) for t, chip in enumerate(chips) for a in range(n)]:
            cp.wait_send()
        for cp in mine:
            cp.wait()


class _Exchange:
    def __init__(self, grads, everyone=()):
        self.inputs = list(grads) + list(everyone)
        self.n_blocked = len(grads)
        n = len(self.inputs)
        self.out_shape = [jax.ShapeDtypeStruct(g.shape, g.dtype) for g in grads]
        self.out_shape += [jax.ShapeDtypeStruct((NDEV,) + e.shape, e.dtype) for e in everyone]
        self.scratch = [pltpu.SemaphoreType.DMA((n, 7)), pltpu.SemaphoreType.DMA((n, 7)), pltpu.SemaphoreType.DMA((n,))]

    def _plan(self, src, land, sems):
        send_sems, recv_sems, local_sems = sems
        x, y, c = _position()
        me = 4 * x + 2 * y + c

        def parts(peer_idx):
            return [(s.at[peer_idx] if a < self.n_blocked else s, land[a].at[me]) for a, s in enumerate(src)]

        local = [pltpu.make_async_copy(s, d, local_sems.at[a]) for a, (s, d) in enumerate(parts(me))]
        sent = []
        for rel in range(1, NDEV):
            px = 1 - x if rel & 4 else x
            py = 1 - y if rel & 2 else y
            pc = 1 - c if rel & 1 else c
            for a, (s, d) in enumerate(parts(4 * px + 2 * py + pc)):
                sent.append(pltpu.make_async_remote_copy(
                    src_ref=s, dst_ref=d, send_sem=send_sems.at[a, rel - 1], recv_sem=recv_sems.at[a, rel - 1],
                    device_id=(px, py, pc), device_id_type=MESH))
        return local, sent

    def start(self, src, land, sems):
        local, sent = self._plan(src, land, sems)
        for cp in local + sent:
            cp.start()

    def relay(self, src, land, sems):
        pass

    def finish(self, src, land, sems):
        local, sent = self._plan(src, land, sems)
        for cp in sent:
            cp.wait_recv()
        for cp in sent:
            cp.wait_send()
        for cp in local:
            cp.wait()


def _call(body, *, name, grid, in_specs, out_specs, out_shape, args, scratch_shapes=(), aliases=None, side=None,
          relay_at=1.0):
    sem = ("arbitrary",) * len(grid)
    if side is None:
        outs = pl.pallas_call(
            body, name=name, grid=grid, in_specs=in_specs, out_specs=out_specs, out_shape=out_shape,
            scratch_shapes=list(scratch_shapes), input_output_aliases=aliases or {},
            compiler_params=pltpu.CompilerParams(dimension_semantics=sem, vmem_limit_bytes=VMEM_LIMIT))(*args)
        return outs, []
    ni, no, ns = len(in_specs), len(out_specs), len(scratch_shapes)
    si, so = len(side.inputs), len(side.out_shape)

    def hosted(*refs):
        ins, refs = refs[:ni], refs[ni:]
        sins, refs = refs[:si], refs[si:]
        outs, refs = refs[:no], refs[no:]
        souts, refs = refs[:so], refs[so:]
        scr, sscr = refs[:ns], refs[ns:]
        step = pl.program_id(0)
        for d in range(1, len(grid)):
            step = step * grid[d] + pl.program_id(d)
        last = int(np.prod(grid)) - 1

        @pl.when(step == 0)
        def _():
            side.start(sins, souts, sscr)

        body(*ins, *outs, *scr)

        @pl.when(step == min(int(relay_at * last), last))
        def _():
            side.relay(sins, souts, sscr)

        @pl.when(step == last)
        def _():
            side.finish(sins, souts, sscr)

    outs = pl.pallas_call(
        hosted, name=name, grid=grid, in_specs=list(in_specs) + [ANY] * si, out_specs=list(out_specs) + [ANY] * so,
        out_shape=list(out_shape) + side.out_shape, scratch_shapes=list(scratch_shapes) + side.scratch,
        input_output_aliases=aliases or {},
        compiler_params=pltpu.CompilerParams(dimension_semantics=sem, vmem_limit_bytes=VMEM_LIMIT, has_side_effects=True),
    )(*args, *side.inputs)
    return outs[:no], outs[no:]


def _remote_only(side, name):
    n = len(side.inputs)

    def body(*refs):
        src, dst, sems = refs[:n], refs[n:n + len(side.out_shape)], refs[n + len(side.out_shape):]
        side.start(src, dst, sems)
        side.relay(src, dst, sems)
        side.finish(src, dst, sems)

    return pl.pallas_call(
        body, name=name, in_specs=[ANY] * n, out_specs=[ANY] * len(side.out_shape), out_shape=side.out_shape,
        scratch_shapes=side.scratch, compiler_params=pltpu.CompilerParams(has_side_effects=True))(*side.inputs)


def _rms(x, g):
    r = lax.rsqrt(jnp.mean(x * x, axis=-1, keepdims=True) + EPS)
    return x * r * g


def _rms_bwd(dy, x, g):
    r = lax.rsqrt(jnp.mean(x * x, axis=-1, keepdims=True) + EPS)
    xh = x * r
    dxh = dy * g
    dx = r * (dxh - xh * jnp.mean(dxh * xh, axis=-1, keepdims=True))
    return dx, jnp.sum(dy * xh, axis=0, keepdims=True)


def _halves(n):
    return [slice(0, n // 2), slice(n // 2, n)] if n % 256 == 0 else [slice(0, n)]


def _zero_at_first_step(acc_ref):
    first = pl.program_id(0) == 0

    @pl.when(first)
    def _():
        acc_ref[...] = jnp.zeros_like(acc_ref)


ROW1 = lambda: _resident((1, D), lambda i: (0, 0))
LROWS = lambda l, n=1: _resident((None, n, D), lambda *_: (l, 0, 0))
ACC = lambda: pl.BlockSpec((8, D), lambda i: (0, 0))


def _inproj_fwd(x, g, wint, l, tm, side=None, relay_at=1.0):
    T = x.shape[0]

    def body(x_ref, g_ref, w_ref, gt_ref, q_ref, k_ref, v_ref, h_ref):
        for rows in _halves(tm):
            h = _rms(x_ref[rows, :], g_ref[...]).astype(BF)
            h_ref[rows, :] = h
            q_ref[:, rows] = _dot(w_ref[0:NQ, :], h, NT).astype(BF)
            k_ref[:, rows] = _dot(w_ref[NQ:NQ + NKV, :], h, NT).astype(BF)
            v_ref[:, rows] = _dot(w_ref[NQ + NKV:NQ + 2 * NKV, :], h, NT).astype(BF)
            for s in range(5):
                lo = NQ + 2 * NKV + s * D
                gt_ref[rows, s * D:(s + 1) * D] = _dot(h, w_ref[lo:lo + D, :], NT).astype(BF)

    tok = lambda w: pl.BlockSpec((tm, w), lambda i: (i, 0))
    feat = lambda w: pl.BlockSpec((w, tm), lambda i: (0, i))
    return _call(
        body, name=f"inproj_fwd{l}", grid=(T // tm,),
        in_specs=[tok(D), LROWS(l), _resident((NP, D), lambda i: (0, 0))],
        out_specs=[tok(NG), feat(NQ), feat(NKV), feat(NKV), tok(D)],
        out_shape=[jax.ShapeDtypeStruct((T, NG), BF)] + [jax.ShapeDtypeStruct((w, T), BF) for w in (NQ, NKV, NKV)]
        + [jax.ShapeDtypeStruct((T, D), BF)],
        args=(x, g, wint), side=side, relay_at=relay_at)


def _band_geometry():
    j = lax.broadcasted_iota(jnp.int32, (2 * WIN, WIN), 0)
    r = lax.broadcasted_iota(jnp.int32, (2 * WIN, WIN), 1)
    dist = WIN + r - j
    dist0 = r - j
    return (dist.astype(F32), (dist >= 0) & (dist < WIN)), (dist0.astype(F32), dist0 >= 0)


def _pair_biases(sm_ref, pj):
    return [[jnp.where(ok, -sm_ref[1, pj * 2 * GQ + h] * dist, NEG) for h in range(2 * GQ)]
            for dist, ok in _band_geometry()]


def _reduce_rows(x, pair, whole):
    while x.shape[0] > 8:
        half = x.shape[0] // 2
        x = pair(x[:half], x[half:])
    return whole(x, axis=0, keepdims=True)


def _heads_on_lanes(ref, kvh, r0):
    return jnp.concatenate([ref[(kvh * GQ + g) * HD:(kvh * GQ + g + 1) * HD, pl.ds(r0, WIN)] for g in range(GQ)], axis=1)


def _band_probs(sm_ref, head0, q_ref, k_ref, r0, p0, kvh, biases):
    qt = _heads_on_lanes(q_ref, kvh, r0) * jnp.asarray(HD ** -0.5, BF)
    kt = k_ref[kvh * HD:(kvh + 1) * HD, pl.ds(p0, 2 * WIN)]
    heads = []
    for g in range(GQ):
        sink = sm_ref[0, head0 + g]
        s = _dot(kt, qt[:, g * WIN:(g + 1) * WIN], TN) + biases[kvh * GQ + g]
        m = jnp.maximum(_reduce_rows(s, jnp.maximum, jnp.max), sink)
        p = jnp.exp(s - m)
        ps = jnp.exp(sink - m)
        heads.append((p, ps, 1.0 / (_reduce_rows(p, jnp.add, jnp.sum) + ps)))
    return qt, kt, heads


def _attn_fwd(q, k, v, sm, l, S, side=None, relay_at=1.0):
    T = q.shape[1]
    nblk = S // WIN
    unroll = next(u for u in (15, 5, 3, 1) if (nblk - 1) % u == 0)

    def body(sm_all_ref, q_ref, k_ref, v_ref, o_ref):
        sm_ref = sm_all_ref.at[l]
        pj = pl.program_id(1)
        biases = _pair_biases(sm_ref, pj)

        def block(i, first):
            r0 = 0 if first else pl.multiple_of(i * WIN, WIN)
            p0 = 0 if first else pl.multiple_of(i * WIN - WIN, WIN)
            for kvh in range(2):
                head0 = (2 * pj + kvh) * GQ
                _, _, heads = _band_probs(sm_ref, head0, q_ref, k_ref, r0, p0, kvh, biases[first])
                vt = v_ref[kvh * HD:(kvh + 1) * HD, pl.ds(p0, 2 * WIN)]
                for g, (p, _, inv) in enumerate(heads):
                    rows = slice((kvh * GQ + g) * HD, (kvh * GQ + g + 1) * HD)
                    o_ref[rows, pl.ds(r0, WIN)] = (_dot(vt, p.astype(BF), NN) * inv).astype(BF)

        block(0, True)

        def rest(t, c):
            for u in range(unroll):
                block(1 + unroll * t + u, False)
            return c

        lax.fori_loop(0, (nblk - 1) // unroll, rest, 0)

    wide = lambda: pl.BlockSpec((2 * GQ * HD, S), lambda s, p: (p, s))
    narrow = lambda: pl.BlockSpec((2 * HD, S), lambda s, p: (p, s))
    return _call(
        body, name=f"attn_fwd{l}", grid=(T // S, 2),
        in_specs=[pl.BlockSpec(memory_space=pltpu.SMEM), wide(), narrow(), narrow()],
        out_specs=[wide()], out_shape=[jax.ShapeDtypeStruct((D, T), BF)],
        args=(sm, q, k, v), side=side, relay_at=relay_at)


def _shift_rows(y, k, edge_rows, down):
    n = y.shape[0]
    rid = lax.broadcasted_iota(jnp.int32, y.shape, 0)
    out = pltpu.roll(y, k if down else n - k, 0)
    for t, row in enumerate(edge_rows):
        out = jnp.where(rid == (t if down else n - k + t), row, out)
    return out


def _mixer_fwd(x, gt, att, w3, bg, cw, cbias, l, S, tm):
    T = x.shape[0]

    def body(x_ref, cb_ref, cc_ref, cu_ref, ga_ref, gc_ref, cch_ref, cuh_ref, att_ref, wao_ref, wco_ref, wo_ref,
             bg_ref, cw_ref, cbias_ref, x1_ref, mg_ref, co_ref, ya_ref, yc_ref, zb_ref):
        first = (pl.program_id(0) * tm) % S == 0
        y = cc_ref[...].astype(F32) * cu_ref[...].astype(F32)
        hy1 = cch_ref[15:16, :].astype(F32) * cuh_ref[15:16, :].astype(F32)
        hy2 = cch_ref[14:15, :].astype(F32) * cuh_ref[14:15, :].astype(F32)
        hy1, hy2 = jnp.where(first, 0.0, hy1), jnp.where(first, 0.0, hy2)
        z = (cw_ref[0:1, :] * _shift_rows(y, 2, [hy2, hy1], True) + cw_ref[1:2, :] * _shift_rows(y, 1, [hy1], True)
             + cw_ref[2:3, :] * y)
        zb = z + cbias_ref[...]
        zb_ref[...] = zb.astype(BF)
        co = (cb_ref[...].astype(F32) * zb).astype(BF)
        co_ref[...] = co
        yc = _dot(co, wco_ref[...], NN)
        ya = _dot(att_ref[...], wao_ref[...], TN)
        ya_ref[...] = ya.astype(BF)
        yc_ref[...] = yc.astype(BF)
        sa = jax.nn.sigmoid(ga_ref[...].astype(F32) + bg_ref[0:1, :])
        sc = jax.nn.sigmoid(gc_ref[...].astype(F32) + bg_ref[1:2, :])
        mg = (sa * ya + sc * yc).astype(BF)
        mg_ref[...] = mg
        x1_ref[...] = x_ref[...] + _dot(mg, wo_ref[...], NN)

    tok = lambda: pl.BlockSpec((tm, D), lambda i: (i, 0))
    seg = lambda s: pl.BlockSpec((tm, D), lambda i: (i, s))
    halo = lambda s: pl.BlockSpec((16, D), lambda i: (jnp.maximum(i * (tm // 16) - 1, 0), s))
    wsp = lambda k: _resident((None, D, D), lambda i: (k, 0, 0))
    return _call(
        body, name=f"mixer_fwd{l}", grid=(T // tm,),
        in_specs=[tok(), seg(0), seg(1), seg(2), seg(3), seg(4), halo(1), halo(2),
                  pl.BlockSpec((D, tm), lambda i: (0, i)), wsp(0), wsp(1), wsp(2), LROWS(l, 2), LROWS(l, 8), LROWS(l)],
        out_specs=[tok()] * 6,
        out_shape=[jax.ShapeDtypeStruct((T, D), dt) for dt in (F32, BF, BF, BF, BF, BF)],
        args=(x, gt, gt, gt, gt, gt, gt, gt, att, w3, w3, w3, bg, cw, cbias))


def _mlp_fwd(x1, g, w2, l, tm, head=None, side=None, relay_at=1.0):
    T = x1.shape[0]
    nt = T // tm
    FC = 1024

    def body(x_ref, g_ref, wup_ref, wdn_ref, *rest):
        out_ref, h_ref, a_ref = rest[-4:-1] if head else rest
        if head:
            t_ref, gf_ref, acc_ref = rest[0], rest[1], rest[-1]
            _zero_at_first_step(acc_ref)
        for rows in _halves(tm):
            x = x_ref[rows, :]
            h = _rms(x, g_ref[...]).astype(BF)
            h_ref[rows, :] = h
            acc = x
            for c in range(F // FC):
                a = _dot(h, wup_ref[c * FC:(c + 1) * FC, :], NT)
                a_ref[rows, c * FC:(c + 1) * FC] = a.astype(BF)
                u = jnp.maximum(a, 0.0)
                acc = acc + _dot((u * u).astype(BF), wdn_ref[c * FC:(c + 1) * FC, :], NN)
            if not head:
                out_ref[rows, :] = acc
                continue
            gf = gf_ref[...]
            err = _rms(acc, gf) - t_ref[rows, :]
            out_ref[rows, :], dg = _rms_bwd(err * (1.0 / D), acc, gf)
            acc_ref[0:1, :] += dg
            acc_ref[1:2, :] += jnp.sum(err * err, axis=0, keepdims=True)
        if not head:
            return

        @pl.when(pl.program_id(0) == nt - 1)
        def _():
            acc_ref[1:2, :] = jnp.zeros((1, D), F32) + (0.5 / D) * jnp.sum(acc_ref[1:2, :])

    tok = lambda w: pl.BlockSpec((tm, w), lambda i: (i, 0))
    wsp = lambda k: _resident((None, F, D), lambda i: (k, 0, 0))
    return _call(
        body, name=f"mlp_fwd{l}", grid=(nt,),
        in_specs=[tok(D), LROWS(l), wsp(0), wsp(1)] + ([tok(D), ROW1()] if head else []),
        out_specs=[tok(D), tok(D), tok(F)] + ([ACC()] if head else []),
        out_shape=[jax.ShapeDtypeStruct((T, D), F32), jax.ShapeDtypeStruct((T, D), BF), jax.ShapeDtypeStruct((T, F), BF)]
        + ([jax.ShapeDtypeStruct((8, D), F32)] if head else []),
        args=(x1, g, w2, w2) + (tuple(head) if head else ()), side=side, relay_at=relay_at)


def _mlp_bwd(dx2, x1, a, g, w2, l, tm, side=None):
    T = dx2.shape[0]
    FC = 1024

    def body(d_ref, x_ref, a_ref, g_ref, wup_ref, wdn_ref, da_ref, dx1_ref, db_ref, acc_ref):
        _zero_at_first_step(acc_ref)
        for rows in _halves(tm):
            d = d_ref[rows, :]
            db = d.astype(BF)
            db_ref[rows, :] = db
            dh = jnp.zeros(d.shape, F32)
            for c in range(F // FC):
                du = _dot(db, wdn_ref[c * FC:(c + 1) * FC, :], NT)
                da = (du * (2.0 * jnp.maximum(a_ref[rows, c * FC:(c + 1) * FC].astype(F32), 0.0))).astype(BF)
                da_ref[rows, c * FC:(c + 1) * FC] = da
                dh = dh + _dot(da, wup_ref[c * FC:(c + 1) * FC, :], NN)
            dx, dg = _rms_bwd(dh, x_ref[rows, :], g_ref[...])
            dx1_ref[rows, :] = d + dx
            acc_ref[0:1, :] += dg

    tok = lambda w: pl.BlockSpec((tm, w), lambda i: (i, 0))
    wsp = lambda k: _resident((None, F, D), lambda i: (k, 0, 0))
    return _call(
        body, name=f"mlp_bwd{l}", grid=(T // tm,),
        in_specs=[tok(D), tok(D), tok(F), LROWS(l), wsp(0), wsp(1)],
        out_specs=[tok(F), tok(D), tok(D), ACC()],
        out_shape=[jax.ShapeDtypeStruct((T, F), BF), jax.ShapeDtypeStruct((T, D), F32),
                   jax.ShapeDtypeStruct((T, D), BF), jax.ShapeDtypeStruct((8, D), F32)],
        args=(dx2, x1, a, g, w2, w2), side=side)


def _mixer_bwd(dx1, gt, ya, yc, zb, w3, bg, cw, l, S, tm, side=None):
    T = dx1.shape[0]
    nt = T // tm

    def body(d_ref, cb_ref, cc_ref, cu_ref, ga_ref, gc_ref, ya_ref, yc_ref, zb_ref, wao_ref, wco_ref, wo_ref,
             bg_ref, cw_ref, dg_ref, datt_ref, dya_ref, dyc_ref, db_ref, acc_ref, carry_ref):
        ti = nt - 1 - pl.program_id(0)
        _zero_at_first_step(acc_ref)

        @pl.when(((ti + 1) * tm) % S == 0)
        def _():
            carry_ref[...] = jnp.zeros_like(carry_ref)

        db = d_ref[...].astype(BF)
        db_ref[...] = db
        dm = _dot(db, wo_ref[...], NT)
        sa = jax.nn.sigmoid(ga_ref[...].astype(F32) + bg_ref[0:1, :])
        sc = jax.nn.sigmoid(gc_ref[...].astype(F32) + bg_ref[1:2, :])
        dya32 = dm * sa
        dyc32 = dm * sc
        dya = dya32.astype(BF)
        dyc = dyc32.astype(BF)
        dya_ref[...] = dya
        dyc_ref[...] = dyc
        dga = dya32 * ya_ref[...].astype(F32) * (1.0 - sa)
        dgc = dyc32 * yc_ref[...].astype(F32) * (1.0 - sc)
        dg_ref[:, 3 * D:4 * D] = dga.astype(BF)
        dg_ref[:, 4 * D:5 * D] = dgc.astype(BF)
        acc_ref[0:1, :] += jnp.sum(dga, axis=0, keepdims=True)
        acc_ref[1:2, :] += jnp.sum(dgc, axis=0, keepdims=True)
        datt_ref[...] = _dot(wao_ref[...], dya, NT).astype(BF)
        dco = _dot(dyc, wco_ref[...], NT)

        cc = cc_ref[...].astype(F32)
        cu = cu_ref[...].astype(F32)
        y = cc * cu
        dg_ref[:, 0:D] = (dco * zb_ref[...].astype(F32)).astype(BF)
        dz = dco * cb_ref[...].astype(F32)
        u1 = _shift_rows(dz, 1, [carry_ref[0:1, :]], False)
        u2 = _shift_rows(dz, 2, [carry_ref[0:1, :], carry_ref[1:2, :]], False)
        acc_ref[2:3, :] += jnp.sum(dz, axis=0, keepdims=True)
        acc_ref[3:4, :] += jnp.sum(u2 * y, axis=0, keepdims=True)
        acc_ref[4:5, :] += jnp.sum(u1 * y, axis=0, keepdims=True)
        acc_ref[5:6, :] += jnp.sum(dz * y, axis=0, keepdims=True)
        dy = cw_ref[2:3, :] * dz + cw_ref[1:2, :] * u1 + cw_ref[0:1, :] * u2
        dg_ref[:, D:2 * D] = (dy * cu).astype(BF)
        dg_ref[:, 2 * D:3 * D] = (dy * cc).astype(BF)
        carry_ref[...] = dz[0:8, :]

    tok = lambda w=D: pl.BlockSpec((tm, w), lambda i: (nt - 1 - i, 0))
    seg = lambda s: pl.BlockSpec((tm, D), lambda i: (nt - 1 - i, s))
    wsp = lambda k: _resident((None, D, D), lambda i: (k, 0, 0))
    return _call(
        body, name=f"mixer_bwd{l}", grid=(nt,),
        in_specs=[tok(), seg(0), seg(1), seg(2), seg(3), seg(4), tok(), tok(), tok(), wsp(0), wsp(1), wsp(2),
                  LROWS(l, 2), LROWS(l, 8)],
        out_specs=[tok(NG), pl.BlockSpec((D, tm), lambda i: (0, nt - 1 - i)), tok(), tok(), tok(), ACC()],
        out_shape=[jax.ShapeDtypeStruct((T, NG), BF), jax.ShapeDtypeStruct((D, T), BF)]
        + [jax.ShapeDtypeStruct((T, D), BF)] * 3 + [jax.ShapeDtypeStruct((8, D), F32)],
        scratch_shapes=[pltpu.VMEM((8, D), F32)],
        args=(dx1, gt, gt, gt, gt, gt, ya, yc, zb, w3, w3, w3, bg, cw), side=side)


def _attn_bwd(q, k, v, att, datt, sm, l, S, side=None):
    T = q.shape[1]
    nblk = S // WIN
    unroll = next(u for u in (5, 3, 1) if (nblk - 1) % u == 0)
    scale = HD ** -0.5

    def body(sm_all_ref, q_ref, k_ref, v_ref, o_ref, do_ref, dq_ref, dk_ref, dv_ref, ds_ref, dka_ref, dva_ref):
        sm_ref = sm_all_ref.at[l]
        pj = pl.program_id(1)
        biases = _pair_biases(sm_ref, pj)
        dka_ref[...] = jnp.zeros_like(dka_ref)
        dva_ref[...] = jnp.zeros_like(dva_ref)

        def block(i, first, dsinks):
            r0 = 0 if first else pl.multiple_of(i * WIN, WIN)
            p0 = 0 if first else pl.multiple_of(i * WIN - WIN, WIN)
            out = []
            for kvh in range(2):
                head0 = (2 * pj + kvh) * GQ
                rows = slice(kvh * HD, (kvh + 1) * HD)
                dob = _heads_on_lanes(do_ref, kvh, r0)
                vt = v_ref[rows, pl.ds(p0, 2 * WIN)]
                dpts = [_dot(vt, dob[:, g * WIN:(g + 1) * WIN], TN) for g in range(GQ)]
                qt, kt, heads = _band_probs(sm_ref, head0, q_ref, k_ref, r0, p0, kvh, biases[first])
                inv = jnp.concatenate([h[2] for h in heads], axis=1)
                do32 = dob.astype(F32)
                delta = jnp.sum(do32 * _heads_on_lanes(o_ref, kvh, r0).astype(F32), axis=0, keepdims=True) * inv
                dosb = (do32 * inv).astype(BF)
                dsts = [(p * (dpts[g] * h_inv - delta[:, g * WIN:(g + 1) * WIN])).astype(BF)
                        for g, (p, _, h_inv) in enumerate(heads)]
                for g in range(GQ):
                    hr = slice((kvh * GQ + g) * HD, (kvh * GQ + g + 1) * HD)
                    dq_ref[hr, pl.ds(r0, WIN)] = (_dot(kt, dsts[g], NN) * scale).astype(BF)
                dst = jnp.concatenate(dsts, axis=1)
                pt = jnp.concatenate([p.astype(BF) for p, _, _ in heads], axis=1)
                dka_ref[rows, pl.ds(p0, 2 * WIN)] += _dot(qt, dst, NT)
                dva_ref[rows, pl.ds(p0, 2 * WIN)] += _dot(dosb, pt, NT)
                ps = jnp.concatenate([h[1] for h in heads], axis=1)
                out.append(dsinks[kvh] - ps * delta)
            return tuple(out)

        zero = jnp.zeros((1, GQ * WIN), F32)
        def rest(t, c):
            for u in range(unroll):
                c = block(1 + unroll * t + u, False, c)
            return c

        dsinks = lax.fori_loop(0, (nblk - 1) // unroll, rest, block(0, True, (zero, zero)))
        for kvh in range(2):
            for g in range(GQ):
                tot = jnp.sum(dsinks[kvh][:, g * WIN:(g + 1) * WIN])
                ds_ref[kvh * GQ + g:kvh * GQ + g + 1, :] = jnp.zeros((1, 128), F32) + tot
        dk_ref[...] = dka_ref[...].astype(BF)
        dv_ref[...] = dva_ref[...].astype(BF)

    wide = lambda: pl.BlockSpec((2 * GQ * HD, S), lambda s, p: (p, s))
    narrow = lambda: pl.BlockSpec((2 * HD, S), lambda s, p: (p, s))
    return _call(
        body, name=f"attn_bwd{l}", grid=(T // S, 2),
        in_specs=[pl.BlockSpec(memory_space=pltpu.SMEM), wide(), narrow(), narrow(), wide(), wide()],
        out_specs=[wide(), narrow(), narrow(), pl.BlockSpec((None, None, 8, 128), lambda s, p: (s, p, 0, 0))],
        out_shape=[jax.ShapeDtypeStruct((NQ, T), BF), jax.ShapeDtypeStruct((NKV, T), BF),
                   jax.ShapeDtypeStruct((NKV, T), BF), jax.ShapeDtypeStruct((T // S, 2, 8, 128), F32)],
        scratch_shapes=[pltpu.VMEM((2 * HD, S), F32), pltpu.VMEM((2 * HD, S), F32)],
        args=(sm, q, k, v, att, datt), side=side)


def _inproj_bwd(dgt, dq, dk, dv, x, dres, g, wint, l, tm, side=None):
    T = x.shape[0]

    def body(dg_ref, dq_ref, dk_ref, dv_ref, x_ref, dr_ref, g_ref, w_ref, dx_ref, acc_ref):
        _zero_at_first_step(acc_ref)
        for rows in _halves(tm):
            dh = _dot(dq_ref[:, rows], w_ref[0:NQ, :], TN)
            dh = dh + _dot(dk_ref[:, rows], w_ref[NQ:NQ + NKV, :], TN)
            dh = dh + _dot(dv_ref[:, rows], w_ref[NQ + NKV:NQ + 2 * NKV, :], TN)
            dh = dh + _dot(dg_ref[rows, :], w_ref[NQ + 2 * NKV:NP, :], NN)
            dx, dg = _rms_bwd(dh, x_ref[rows, :], g_ref[...])
            dx_ref[rows, :] = dr_ref[rows, :] + dx
            acc_ref[0:1, :] += dg

    tok = lambda w: pl.BlockSpec((tm, w), lambda i: (i, 0))
    feat = lambda w: pl.BlockSpec((w, tm), lambda i: (0, i))
    return _call(
        body, name=f"inproj_bwd{l}", grid=(T // tm,),
        in_specs=[tok(NG), feat(NQ), feat(NKV), feat(NKV), tok(D), tok(D), LROWS(l), _resident((NP, D), lambda i: (0, 0))],
        out_specs=[tok(D), ACC()],
        out_shape=[jax.ShapeDtypeStruct((T, D), F32), jax.ShapeDtypeStruct((8, D), F32)],
        args=(dgt, dq, dk, dv, x, dres, g, wint), side=side)


def _wgrad(a, b, rows, row0, into, name, relu2=False):
    T, M = a.shape
    tmm = next(t for t in (1024, 512, 256) if M % t == 0 and row0 % t == 0)
    tk = min(4096 if M > 4096 else 2048, T)
    nk = T // tk
    blk0 = row0 // tmm

    def body(*refs):
        a_ref, b_ref = refs[0], refs[1]
        o_ref, acc_ref = refs[-2], refs[-1]
        kk = pl.program_id(1)

        @pl.when(kk == 0)
        def _():
            acc_ref[...] = jnp.zeros_like(acc_ref)

        av = a_ref[...]
        if relu2:
            t = jnp.maximum(av.astype(F32), 0.0)
            av = (t * t).astype(BF)
        acc_ref[...] += _dot(av, b_ref[...], TN)

        @pl.when(kk == nk - 1)
        def _():
            o_ref[...] = acc_ref[...].astype(BF)

    in_specs = [pl.BlockSpec((tk, tmm), lambda j, kk: (kk, j)), pl.BlockSpec((tk, D), lambda j, kk: (kk, 0))]
    args = [a, b]
    if into is not None:
        in_specs.append(ANY)
        args.append(into)
    (out,), _ = _call(
        body, name=name, grid=(M // tmm, nk), in_specs=in_specs,
        out_specs=[pl.BlockSpec((tmm, D), lambda j, kk: (blk0 + j, 0))],
        out_shape=[jax.ShapeDtypeStruct((rows, D), BF)], scratch_shapes=[pltpu.VMEM((tmm, D), F32)],
        aliases={2: 0} if into is not None else None, args=args)
    return out


def _wgrad_qkv(dq, dk, dv, h, into, name):
    T = h.shape[0]
    rows = NQ + 2 * NKV
    tk = min(2048, T)
    nk = T // tk

    def body(dq_ref, dk_ref, dv_ref, h_ref, _, o_ref, acc_ref):
        kk = pl.program_id(0)

        @pl.when(kk == 0)
        def _():
            acc_ref[...] = jnp.zeros_like(acc_ref)

        hv = h_ref[...]
        acc_ref[0:NQ, :] += _dot(dq_ref[...], hv, NN)
        acc_ref[NQ:NQ + NKV, :] += _dot(dk_ref[...], hv, NN)
        acc_ref[NQ + NKV:rows, :] += _dot(dv_ref[...], hv, NN)

        @pl.when(kk == nk - 1)
        def _():
            o_ref[...] = acc_ref[...].astype(BF)

    feat = lambda w: pl.BlockSpec((w, tk), lambda kk: (0, kk))
    (out,), _ = _call(
        body, name=name, grid=(nk,),
        in_specs=[feat(NQ), feat(NKV), feat(NKV), pl.BlockSpec((tk, D), lambda kk: (kk, 0)), ANY],
        out_specs=[pl.BlockSpec((rows, D), lambda kk: (0, 0))],
        out_shape=[jax.ShapeDtypeStruct((NP, D), BF)], scratch_shapes=[pltpu.VMEM((rows, D), F32)],
        aliases={4: 0}, args=[dq, dk, dv, h, into])
    return out


def _wgrad_square(jobs, name):
    n = len(jobs)
    T = jobs[0][1].shape[0]
    tk = min(1024, T)
    nk = T // tk

    def body(*refs):
        ins, outs, acc_ref = refs[:2 * n], refs[2 * n:3 * n], refs[3 * n]
        j, kk = pl.program_id(0), pl.program_id(1)

        @pl.when(kk == 0)
        def _():
            acc_ref[...] = jnp.zeros_like(acc_ref)

        for p, (_, _, a_t) in enumerate(jobs):
            @pl.when(j == p)
            def _(p=p, a_t=a_t):
                acc_ref[...] += _dot(ins[2 * p][...], ins[2 * p + 1][...], NN if a_t else TN)

                @pl.when(kk == nk - 1)
                def _():
                    outs[p][...] = acc_ref[...].astype(BF)

    def step(p):
        return lambda j, kk: jnp.where(j == p, kk, jnp.where(j > p, nk - 1, 0))

    in_specs, args = [], []
    for p, (a, b, a_t) in enumerate(jobs):
        at = step(p)
        in_specs.append(pl.BlockSpec((D, tk), lambda j, kk, at=at: (0, at(j, kk))) if a_t
                        else pl.BlockSpec((tk, D), lambda j, kk, at=at: (at(j, kk), 0)))
        in_specs.append(pl.BlockSpec((tk, D), lambda j, kk, at=at: (at(j, kk), 0)))
        args += [a, b]
    outs, _ = _call(
        body, name=name, grid=(n, nk), in_specs=in_specs,
        out_specs=[pl.BlockSpec((D, D), lambda j, kk: (0, 0))] * n,
        out_shape=[jax.ShapeDtypeStruct((D, D), BF)] * n, scratch_shapes=[pltpu.VMEM((D, D), F32)], args=args)
    return outs


def _adamw(w, g, m, v):
    m = B1 * m + (1.0 - B1) * g
    v = B2 * v + (1.0 - B2) * (g * g)
    m_hat = m / (1.0 - B1 ** STEP)
    v_hat = v / (1.0 - B2 ** STEP)
    return -LR * (m_hat / (jnp.sqrt(v_hat) + AEPS) + WD * w), m, v


def _adam_sum(land, w, m, v, l, into, name, transposed=False):
    _, r, _ = land.shape
    tr = 208 if r % 208 == 0 else (256 if r % 256 == 0 else r)
    tc = 256

    def body(land_ref, w_ref, m_ref, v_ref, *rest):
        g_ref, d_ref, nm_ref, nv_ref = rest[-4:]
        g = land_ref[0].astype(F32)
        for s in range(1, NDEV):
            g = g + land_ref[s].astype(F32)
        if transposed:
            g = g.T
        g_ref[...] = g
        d_ref[...], nm_ref[...], nv_ref[...] = _adamw(w_ref[...], g, m_ref[...], v_ref[...])

    if transposed:
        blk = lambda: pl.BlockSpec((None, tc, r), lambda j: (l, j, 0))
        land_spec, grid = pl.BlockSpec((NDEV, r, tc), lambda j: (0, 0, j)), (D // tc,)
    else:
        blk = lambda: pl.BlockSpec((None, tr, D), lambda j: (l, j, 0))
        land_spec, grid = pl.BlockSpec((NDEV, tr, D), lambda j: (0, j, 0)), (r // tr,)
    in_specs = [land_spec, blk(), blk(), blk()]
    args = [land, w, m, v]
    aliases = None
    if into is not None:
        in_specs += [ANY] * 4
        args += list(into)
        aliases = {4 + t: t for t in range(4)}
    outs, _ = _call(body, name=name, grid=grid, in_specs=in_specs, out_specs=[blk()] * 4,
                    out_shape=[jax.ShapeDtypeStruct(w.shape, F32)] * 4, aliases=aliases, args=args)
    return outs


SMALL_NAMES = ("g_mix", "b_gates", "sinks", "conv_b", "g_mlp", "g_final", "conv_w")


def _adam_small(land, me, masters):
    n = len(SMALL_NAMES)
    lanes = D // NDEV

    def body(land_ref, me_ref, *refs):
        ins, outs, loss_ref, gs_ref = refs[:3 * n], refs[3 * n:7 * n], refs[7 * n], refs[7 * n + 1]
        g = land_ref[0]
        for s in range(1, NDEV):
            g = g + land_ref[s]
        gs_ref[...] = g

        def update(k, g_piece, idx):
            w_ref, m_ref, v_ref = ins[3 * k:3 * k + 3]
            outs[4 * k][idx] = g_piece
            outs[4 * k + 1][idx], outs[4 * k + 2][idx], outs[4 * k + 3][idx] = _adamw(w_ref[idx], g_piece, m_ref[idx], v_ref[idx])

        whole = (slice(None), slice(None))
        update(0, gs_ref[0:2, :], whole)
        for l in range(L):
            for h in range(2):
                update(1, gs_ref[2 + 2 * l + h:3 + 2 * l + h, :], (slice(l, l + 1), slice(h * D, (h + 1) * D)))
            update(2, gs_ref[6:7, 16 * l:16 * (l + 1)], (slice(l, l + 1), slice(None)))
        update(3, gs_ref[7:9, :], whole)
        update(4, gs_ref[9:11, :], whole)
        update(5, gs_ref[11:12, :], whole)
        mine = pl.ds(pl.multiple_of(me_ref[0] * lanes, lanes), lanes)
        for l in range(L):
            for k in range(3):
                update(6, gs_ref[12 + 3 * l + k:13 + 3 * l + k, mine], (l, slice(k, k + 1), slice(None)))
        loss_ref[...] = gs_ref[18:19, 0:1]

    flat = [t for name in SMALL_NAMES for t in masters[name]]
    vmem = pl.BlockSpec(memory_space=pltpu.VMEM)
    outs = pl.pallas_call(
        body, name="adam_small",
        in_specs=[vmem, pl.BlockSpec(memory_space=pltpu.SMEM)] + [vmem] * len(flat),
        out_shape=[jax.ShapeDtypeStruct(masters[name][0].shape, F32) for name in SMALL_NAMES for _ in range(4)]
        + [jax.ShapeDtypeStruct((1, 1), F32)],
        scratch_shapes=[pltpu.VMEM((SMALL_ROWS, D), F32)],
        compiler_params=pltpu.CompilerParams(vmem_limit_bytes=VMEM_LIMIT))(land, me, *flat)
    return {name: outs[4 * k:4 * k + 4] for k, name in enumerate(SMALL_NAMES)}, outs[-1]


def _pack_small(g_mix, b_gates, sinks, conv_b, g_mlp, g_final, conv_w_rows, extra):
    sink_row = jnp.zeros((1, D), F32).at[0, :2 * 16].set(sinks.reshape(-1))
    return jnp.concatenate([g_mix, b_gates.reshape(4, D), sink_row, conv_b, g_mlp, g_final.reshape(1, D),
                            conv_w_rows, extra, jnp.zeros((SMALL_ROWS - 19, D), F32)], axis=0)


def kernel(x, g_mix, w_in, b_gates, sinks, w_attn_out, conv_w, conv_b, w_conv_out, w_o, g_mlp, w_up, w_down, g_final, loss_target, m_g_mix, m_w_in, m_b_gates, m_sinks, m_w_attn_out, m_conv_w, m_conv_b, m_w_conv_out, m_w_o, m_g_mlp, m_w_up, m_w_down, m_g_final, v_g_mix, v_w_in, v_b_gates, v_sinks, v_w_attn_out, v_conv_w, v_conv_b, v_w_conv_out, v_w_o, v_g_mlp, v_w_up, v_w_down, v_g_final):
    nseq, S, _ = x.shape
    T = nseq * S
    tm_in = min(512, S)
    tm = min(512, S)
    xi, yi, ci = _position()
    me = 4 * xi + 2 * yi + ci
    tr = lambda t: jnp.swapaxes(t, 1, 2)
    blocks = lambda t: t.reshape(NDEV, t.shape[0] // NDEV, D)

    win_t, wup_t = tr(w_in), tr(w_up)
    sh_win = [win_t[l].astype(BF)[None] for l in range(L)]
    sh_w3 = [jnp.stack([w_attn_out[l], w_conv_out[l], w_o[l]]).astype(BF) for l in range(L)]
    sh_w2 = [jnp.stack([wup_t[l], w_down[l]]).astype(BF) for l in range(L)]
    wint, w3, w2 = [None] * L, [None] * L, [None] * L
    wint0_g, cw_g = _remote_only(_Gather([sh_win[0], jnp.pad(conv_w, ((0, 0), (0, 5), (0, 0)))]), "gather_first")
    wint[0] = wint0_g.reshape(NP, D)
    cw = jnp.swapaxes(cw_g, 1, 2).reshape(L, 8, D)
    slopes = np.power(np.float32(2.0), -8.0 * np.arange(1, 17, dtype=np.float32) / 16).astype(np.float32)
    sm = jnp.stack([sinks, jnp.broadcast_to(jnp.asarray(slopes), sinks.shape)], axis=1)
    per_layer = lambda t, n=1: t.reshape(L, n, D)
    g_mix3, g_mlp3, conv_b3, bg = per_layer(g_mix), per_layer(g_mlp), per_layer(conv_b), per_layer(b_gates, 2)

    xf = x.reshape(T, D)
    saved = []
    cur = xf
    for l in range(L):
        (gt, q, k, v, h), got = _inproj_fwd(cur, g_mix3, wint[l], l, tm_in, side=_Gather([sh_w2[l]]),
                                            relay_at=0.85)
        w2[l] = got[0].reshape(2, F, D)
        (att,), got = _attn_fwd(q, k, v, sm, l, S, side=_Gather([sh_w3[l]]), relay_at=0.5)
        w3[l] = got[0].reshape(3, D, D)
        (x1, mg, co, ya, yc, zb), _ = _mixer_fwd(cur, gt, att, w3[l], bg, cw, conv_b3, l, S, tm)
        head = (loss_target.reshape(T, D), g_final.reshape(1, D)) if l == L - 1 else None
        (nxt, h2, a, *acc_loss), got = _mlp_fwd(x1, g_mlp3, w2[l], l, tm_in, head=head,
                                                side=_Gather([sh_win[l + 1]]) if l + 1 < L else None, relay_at=0.7)
        if l + 1 < L:
            wint[l + 1] = got[0].reshape(NP, D)
        saved.append(dict(x=cur, gt=gt, q=q, k=k, v=v, h=h, att=att, x1=x1, mg=mg, co=co, ya=ya, yc=yc, zb=zb, h2=h2, a=a))
        cur = nxt
    dcur, acc_loss = cur, acc_loss[0]

    masters = {"w_in": (win_t, tr(m_w_in), tr(v_w_in)), "w_attn_out": (w_attn_out, m_w_attn_out, v_w_attn_out),
               "w_conv_out": (w_conv_out, m_w_conv_out, v_w_conv_out), "w_o": (w_o, m_w_o, v_w_o),
               "w_up": (w_up, m_w_up, v_w_up), "w_down": (w_down, m_w_down, v_w_down)}
    big = {name: None for name in masters}

    def adam(name, land, l):
        big[name] = _adam_sum(land, *masters[name], l, big[name], f"adam_{name}{l}", transposed=name == "w_up")

    acc_in, acc_mix, acc_mlp, dsink = [None] * L, [None] * L, [None] * L, [None] * L
    d_win_above = None
    for l in reversed(range(L)):
        sv = saved[l]
        side = _Exchange([blocks(d_win_above)]) if l + 1 < L else None
        (da, dx1, dx2b, acc_mlp[l]), got = _mlp_bwd(dcur, sv["x1"], sv["a"], g_mlp3, w2[l], l, tm_in, side=side)
        if l + 1 < L:
            adam("w_in", got[0], l + 1)
        d_wdn = _wgrad(sv["a"], dx2b, F, 0, None, f"wgrad_down{l}", relu2=True)
        d_wup = _wgrad(da, sv["h2"], F, 0, None, f"wgrad_up{l}")
        (dgt, datt, dya, dyc, dx1b, acc_mix[l]), got = _mixer_bwd(
            dx1, sv["gt"], sv["ya"], sv["yc"], sv["zb"], w3[l], bg, cw, l, S, tm,
            side=_Exchange([blocks(d_wdn)]))
        adam("w_down", got[0], l)
        d_wo, d_wao, d_wco = _wgrad_square(
            [(sv["mg"], dx1b, False), (sv["att"], dya, True), (sv["co"], dyc, False)], f"wgrad_mixer{l}")
        d_win = _wgrad(dgt, sv["h"], NP, NQ + 2 * NKV, None, f"wgrad_gates{l}")
        (dq, dk, dv, dsink[l]), got = _attn_bwd(
            sv["q"], sv["k"], sv["v"], sv["att"], datt, sm, l, S,
            side=_Exchange([blocks(d_wup), blocks(d_wo), blocks(d_wao), blocks(d_wco)]))
        for name, land in zip(["w_up", "w_o", "w_attn_out", "w_conv_out"], got):
            adam(name, land, l)
        d_win = _wgrad_qkv(dq, dk, dv, sv["h"], d_win, f"wgrad_qkv{l}")
        side = _Exchange([blocks(d_win)]) if l == 0 else None
        (dcur, acc_in[l]), got_in = _inproj_bwd(dgt, dq, dk, dv, sv["x"], dx1, g_mix3, wint[l], l, tm_in, side=side)
        d_win_above = d_win

    dsinks = jnp.stack([dsink[l][:, :, :, 0].sum(axis=0).reshape(16) for l in range(L)])
    small = _pack_small(
        jnp.concatenate([acc_in[l][0:1] for l in range(L)]),
        jnp.stack([acc_mix[l][0:2].reshape(2 * D) for l in range(L)]),
        dsinks,
        jnp.concatenate([acc_mix[l][2:3] for l in range(L)]),
        jnp.concatenate([acc_mlp[l][0:1] for l in range(L)]),
        acc_loss[0],
        jnp.concatenate([acc_mix[l][3:6] for l in range(L)]),
        acc_loss[1:2])
    adam("w_in", got_in[0], 0)
    (small_land,) = _remote_only(_Exchange([], [small]), "exchange_small")

    row = lambda t: t.reshape(1, D)
    small_out, loss = _adam_small(small_land, me.reshape(1).astype(jnp.int32), {
        "g_mix": (g_mix, m_g_mix, v_g_mix), "b_gates": (b_gates, m_b_gates, v_b_gates), "sinks": (sinks, m_sinks, v_sinks),
        "conv_b": (conv_b, m_conv_b, v_conv_b), "g_mlp": (g_mlp, m_g_mlp, v_g_mlp),
        "g_final": (row(g_final), row(m_g_final), row(v_g_final)), "conv_w": (conv_w, m_conv_w, v_conv_w)})
    small_out["g_final"] = [t.reshape(D) for t in small_out["g_final"]]
    big["w_in"] = [tr(o) for o in big["w_in"]]
    order = ["g_mix", "w_in", "b_gates", "sinks", "w_attn_out", "conv_w", "conv_b", "w_conv_out", "w_o", "g_mlp",
             "w_up", "w_down", "g_final"]
    out = [loss.reshape(()), dcur.reshape(nseq, S, D)]
    for kind in range(4):
        for name in order:
            out.append(big[name][kind] if name in big else small_out[name][kind])
    return tuple(out)
```

```python
import numpy as np
import jax
import jax.numpy as jnp
from jax import lax
from jax.experimental import pallas as pl
from jax.experimental.pallas import tpu as pltpu

D = 1024
NG = 5 * D
NQ = 1024
NKV = 256
NP = NQ + 2 * NKV + NG
F = 4096
HD = 64
GQ = 4
WIN = 128
L = 2
NDEV = 8
EPS = 1e-6
NEG = -1e30
SMALL_ROWS = 24
LR, B1, B2, AEPS, WD, STEP = 0.001, 0.9, 0.999, 1e-08, 0.01, 10

BF = jnp.bfloat16
F32 = jnp.float32
MESH = pl.DeviceIdType.MESH
VMEM_LIMIT = 60 * 1024 * 1024
ANY = pl.BlockSpec(memory_space=pl.ANY)

NN = ((1,), (0,))
NT = ((1,), (1,))
TN = ((0,), (0,))


def _dot(a, b, dims):
    return lax.dot_general(a, b, (dims, ((), ())), preferred_element_type=F32)


def _resident(shape, imap):
    return pl.BlockSpec(shape, imap, pipeline_mode=pl.Buffered(1))


def _position():
    return lax.axis_index("x"), lax.axis_index("y"), lax.axis_index("c")


class _Gather:
    def __init__(self, shards):
        n = len(shards)
        self.inputs = list(shards)
        self.out_shape = [jax.ShapeDtypeStruct((s.shape[0], NDEV) + s.shape[1:], s.dtype) for s in shards]
        self.scratch = [pltpu.SemaphoreType.DMA((n, 7)), pltpu.SemaphoreType.DMA((n, 7)), pltpu.SemaphoreType.DMA((n,))]

    def _plan(self, src, dst, sems):
        send_sems, recv_sems, local_sems = sems
        n = len(src)
        x, y, c = _position()
        me, sibling = (x, y, c), (x, y, 1 - c)
        chips = [(1 - x, y), (x, 1 - y), (1 - x, 1 - y)]

        def rows(a, p):
            return dst[a].at[:, 4 * p[0] + 2 * p[1] + p[2]]

        def copy(a, k, block, to, from_src=False):
            return pltpu.make_async_remote_copy(
                src_ref=src[a] if from_src else rows(a, block), dst_ref=rows(a, block),
                send_sem=send_sems.at[a, k], recv_sem=recv_sems.at[a, k], device_id=to, device_id_type=MESH)

        mine = [pltpu.make_async_copy(src[a], rows(a, me), local_sems.at[a]) for a in range(n)]
        first = []
        for a in range(n):
            first.append(copy(a, 0, me, sibling, True))
            first += [copy(a, 1 + t, me, (*chip, c), True) for t, chip in enumerate(chips)]
        return n, c, me, sibling, chips, copy, mine, first

    def start(self, src, dst, sems):
        *_, mine, first = self._plan(src, dst, sems)
        for cp in mine + first:
            cp.start()

    def relay(self, src, dst, sems):
        n, c, me, sibling, chips, copy, _, _ = self._plan(src, dst, sems)
        for t, chip in enumerate(chips):
            for a in range(n):
                copy(a, 1 + t, (*chip, c), me).wait_recv()
                copy(a, 4 + t, (*chip, c), sibling).start()

    def finish(self, src, dst, sems):
        n, c, me, sibling, chips, copy, mine, first = self._plan(src, dst, sems)
        for a in range(n):
            copy(a, 0, sibling, me).wait_recv()
            for t, chip in enumerate(chips):
                copy(a, 4 + t, (*chip, 1 - c), me).wait_recv()
        for cp in first + [copy(a, 4 + t, (*chip, c), sibling) for t, chip in enumerate(chips) for a in range(n)]:
            cp.wait_send()
        for cp in mine:
            cp.wait()


class _Exchange:
    def __init__(self, grads, everyone=()):
        self.inputs = list(grads) + list(everyone)
        self.n_blocked = len(grads)
        n = len(self.inputs)
        self.out_shape = [jax.ShapeDtypeStruct(g.shape, g.dtype) for g in grads]
        self.out_shape += [jax.ShapeDtypeStruct((NDEV,) + e.shape, e.dtype) for e in everyone]
        self.scratch = [pltpu.SemaphoreType.DMA((n, 7)), pltpu.SemaphoreType.DMA((n, 7)), pltpu.SemaphoreType.DMA((n,))]

    def _plan(self, src, land, sems):
        send_sems, recv_sems, local_sems = sems
        x, y, c = _position()
        me = 4 * x + 2 * y + c

        def parts(peer_idx):
            return [(s.at[peer_idx] if a < self.n_blocked else s, land[a].at[me]) for a, s in enumerate(src)]

        local = [pltpu.make_async_copy(s, d, local_sems.at[a]) for a, (s, d) in enumerate(parts(me))]
        sent = []
        for rel in range(1, NDEV):
            px = 1 - x if rel & 4 else x
            py = 1 - y if rel & 2 else y
            pc = 1 - c if rel & 1 else c
            for a, (s, d) in enumerate(parts(4 * px + 2 * py + pc)):
                sent.append(pltpu.make_async_remote_copy(
                    src_ref=s, dst_ref=d, send_sem=send_sems.at[a, rel - 1], recv_sem=recv_sems.at[a, rel - 1],
                    device_id=(px, py, pc), device_id_type=MESH))
        return local, sent

    def start(self, src, land, sems):
        local, sent = self._plan(src, land, sems)
        for cp in local + sent:
            cp.start()

    def relay(self, src, land, sems):
        pass

    def finish(self, src, land, sems):
        local, sent = self._plan(src, land, sems)
        for cp in sent:
            cp.wait_recv()
        for cp in sent:
            cp.wait_send()
        for cp in local:
            cp.wait()


def _call(body, *, name, grid, in_specs, out_specs, out_shape, args, scratch_shapes=(), aliases=None, side=None,
          relay_at=1.0):
    sem = ("arbitrary",) * len(grid)
    if side is None:
        outs = pl.pallas_call(
            body, name=name, grid=grid, in_specs=in_specs, out_specs=out_specs, out_shape=out_shape,
            scratch_shapes=list(scratch_shapes), input_output_aliases=aliases or {},
            compiler_params=pltpu.CompilerParams(dimension_semantics=sem, vmem_limit_bytes=VMEM_LIMIT))(*args)
        return outs, []
    ni, no, ns = len(in_specs), len(out_specs), len(scratch_shapes)
    si, so = len(side.inputs), len(side.out_shape)

    def hosted(*refs):
        ins, refs = refs[:ni], refs[ni:]
        sins, refs = refs[:si], refs[si:]
        outs, refs = refs[:no], refs[no:]
        souts, refs = refs[:so], refs[so:]
        scr, sscr = refs[:ns], refs[ns:]
        step = pl.program_id(0)
        for d in range(1, len(grid)):
            step = step * grid[d] + pl.program_id(d)
        last = int(np.prod(grid)) - 1

        @pl.when(step == 0)
        def _():
            side.start(sins, souts, sscr)

        body(*ins, *outs, *scr)

        @pl.when(step == min(int(relay_at * last), last))
        def _():
            side.relay(sins, souts, sscr)

        @pl.when(step == last)
        def _():
            side.finish(sins, souts, sscr)

    outs = pl.pallas_call(
        hosted, name=name, grid=grid, in_specs=list(in_specs) + [ANY] * si, out_specs=list(out_specs) + [ANY] * so,
        out_shape=list(out_shape) + side.out_shape, scratch_shapes=list(scratch_shapes) + side.scratch,
        input_output_aliases=aliases or {},
        compiler_params=pltpu.CompilerParams(dimension_semantics=sem, vmem_limit_bytes=VMEM_LIMIT, has_side_effects=True),
    )(*args, *side.inputs)
    return outs[:no], outs[no:]


def _remote_only(side, name):
    n = len(side.inputs)

    def body(*refs):
        src, dst, sems = refs[:n], refs[n:n + len(side.out_shape)], refs[n + len(side.out_shape):]
        side.start(src, dst, sems)
        side.relay(src, dst, sems)
        side.finish(src, dst, sems)

    return pl.pallas_call(
        body, name=name, in_specs=[ANY] * n, out_specs=[ANY] * len(side.out_shape), out_shape=side.out_shape,
        scratch_shapes=side.scratch, compiler_params=pltpu.CompilerParams(has_side_effects=True))(*side.inputs)


def _rms(x, g):
    r = lax.rsqrt(jnp.mean(x * x, axis=-1, keepdims=True) + EPS)
    return x * r * g


def _rms_bwd(dy, x, g):
    r = lax.rsqrt(jnp.mean(x * x, axis=-1, keepdims=True) + EPS)
    xh = x * r
    dxh = dy * g
    dx = r * (dxh - xh * jnp.mean(dxh * xh, axis=-1, keepdims=True))
    return dx, jnp.sum(dy * xh, axis=0, keepdims=True)


def _halves(n):
    return [slice(0, n // 2), slice(n // 2, n)] if n % 256 == 0 else [slice(0, n)]


def _zero_at_first_step(acc_ref):
    first = pl.program_id(0) == 0

    @pl.when(first)
    def _():
        acc_ref[...] = jnp.zeros_like(acc_ref)


ROW1 = lambda: _resident((1, D), lambda i: (0, 0))
LROWS = lambda l, n=1: _resident((None, n, D), lambda *_: (l, 0, 0))
ACC = lambda: pl.BlockSpec((8, D), lambda i: (0, 0))


def _inproj_fwd(x, g, wint, l, tm, side=None, relay_at=1.0):
    T = x.shape[0]

    def body(x_ref, g_ref, w_ref, gt_ref, q_ref, k_ref, v_ref, h_ref):
        for rows in _halves(tm):
            h = _rms(x_ref[rows, :], g_ref[...]).astype(BF)
            h_ref[rows, :] = h
            q_ref[:, rows] = _dot(w_ref[0:NQ, :], h, NT).astype(BF)
            k_ref[:, rows] = _dot(w_ref[NQ:NQ + NKV, :], h, NT).astype(BF)
            v_ref[:, rows] = _dot(w_ref[NQ + NKV:NQ + 2 * NKV, :], h, NT).astype(BF)
            for s in range(5):
                lo = NQ + 2 * NKV + s * D
                gt_ref[rows, s * D:(s + 1) * D] = _dot(h, w_ref[lo:lo + D, :], NT).astype(BF)

    tok = lambda w: pl.BlockSpec((tm, w), lambda i: (i, 0))
    feat = lambda w: pl.BlockSpec((w, tm), lambda i: (0, i))
    return _call(
        body, name=f"inproj_fwd{l}", grid=(T // tm,),
        in_specs=[tok(D), LROWS(l), _resident((NP, D), lambda i: (0, 0))],
        out_specs=[tok(NG), feat(NQ), feat(NKV), feat(NKV), tok(D)],
        out_shape=[jax.ShapeDtypeStruct((T, NG), BF)] + [jax.ShapeDtypeStruct((w, T), BF) for w in (NQ, NKV, NKV)]
        + [jax.ShapeDtypeStruct((T, D), BF)],
        args=(x, g, wint), side=side, relay_at=relay_at)


def _band_geometry():
    j = lax.broadcasted_iota(jnp.int32, (2 * WIN, WIN), 0)
    r = lax.broadcasted_iota(jnp.int32, (2 * WIN, WIN), 1)
    dist = WIN + r - j
    dist0 = r - j
    return (dist.astype(F32), (dist >= 0) & (dist < WIN)), (dist0.astype(F32), dist0 >= 0)


def _pair_biases(sm_ref, pj):
    return [[jnp.where(ok, -sm_ref[1, pj * 2 * GQ + h] * dist, NEG) for h in range(2 * GQ)]
            for dist, ok in _band_geometry()]


def _reduce_rows(x, pair, whole):
    while x.shape[0] > 8:
        half = x.shape[0] // 2
        x = pair(x[:half], x[half:])
    return whole(x, axis=0, keepdims=True)


def _heads_on_lanes(ref, kvh, r0):
    return jnp.concatenate([ref[(kvh * GQ + g) * HD:(kvh * GQ + g + 1) * HD, pl.ds(r0, WIN)] for g in range(GQ)], axis=1)


def _band_probs(sm_ref, head0, q_ref, k_ref, r0, p0, kvh, biases):
    qt = _heads_on_lanes(q_ref, kvh, r0) * jnp.asarray(HD ** -0.5, BF)
    kt = k_ref[kvh * HD:(kvh + 1) * HD, pl.ds(p0, 2 * WIN)]
    heads = []
    for g in range(GQ):
        sink = sm_ref[0, head0 + g]
        s = _dot(kt, qt[:, g * WIN:(g + 1) * WIN], TN) + biases[kvh * GQ + g]
        m = jnp.maximum(_reduce_rows(s, jnp.maximum, jnp.max), sink)
        p = jnp.exp(s - m)
        ps = jnp.exp(sink - m)
        heads.append((p, ps, 1.0 / (_reduce_rows(p, jnp.add, jnp.sum) + ps)))
    return qt, kt, heads


def _attn_fwd(q, k, v, sm, l, S, side=None, relay_at=1.0):
    T = q.shape[1]
    nblk = S // WIN
    unroll = next(u for u in (15, 5, 3, 1) if (nblk - 1) % u == 0)

    def body(sm_all_ref, q_ref, k_ref, v_ref, o_ref):
        sm_ref = sm_all_ref.at[l]
        pj = pl.program_id(1)
        biases = _pair_biases(sm_ref, pj)

        def block(i, first):
            r0 = 0 if first else pl.multiple_of(i * WIN, WIN)
            p0 = 0 if first else pl.multiple_of(i * WIN - WIN, WIN)
            for kvh in range(2):
                head0 = (2 * pj + kvh) * GQ
                _, _, heads = _band_probs(sm_ref, head0, q_ref, k_ref, r0, p0, kvh, biases[first])
                vt = v_ref[kvh * HD:(kvh + 1) * HD, pl.ds(p0, 2 * WIN)]
                for g, (p, _, inv) in enumerate(heads):
                    rows = slice((kvh * GQ + g) * HD, (kvh * GQ + g + 1) * HD)
                    o_ref[rows, pl.ds(r0, WIN)] = (_dot(vt, p.astype(BF), NN) * inv).astype(BF)

        block(0, True)

        def rest(t, c):
            for u in range(unroll):
                block(1 + unroll * t + u, False)
            return c

        lax.fori_loop(0, (nblk - 1) // unroll, rest, 0)

    wide = lambda: pl.BlockSpec((2 * GQ * HD, S), lambda s, p: (p, s))
    narrow = lambda: pl.BlockSpec((2 * HD, S), lambda s, p: (p, s))
    return _call(
        body, name=f"attn_fwd{l}", grid=(T // S, 2),
        in_specs=[pl.BlockSpec(memory_space=pltpu.SMEM), wide(), narrow(), narrow()],
        out_specs=[wide()], out_shape=[jax.ShapeDtypeStruct((D, T), BF)],
        args=(sm, q, k, v), side=side, relay_at=relay_at)


def _shift_rows(y, k, edge_rows, down):
    n = y.shape[0]
    rid = lax.broadcasted_iota(jnp.int32, y.shape, 0)
    out = pltpu.roll(y, k if down else n - k, 0)
    for t, row in enumerate(edge_rows):
        out = jnp.where(rid == (t if down else n - k + t), row, out)
    return out


def _mixer_fwd(x, gt, att, w3, bg, cw, cbias, l, S, tm):
    T = x.shape[0]

    def body(x_ref, cb_ref, cc_ref, cu_ref, ga_ref, gc_ref, cch_ref, cuh_ref, att_ref, wao_ref, wco_ref, wo_ref,
             bg_ref, cw_ref, cbias_ref, x1_ref, mg_ref, co_ref, ya_ref, yc_ref, zb_ref):
        first = (pl.program_id(0) * tm) % S == 0
        y = cc_ref[...].astype(F32) * cu_ref[...].astype(F32)
        hy1 = cch_ref[15:16, :].astype(F32) * cuh_ref[15:16, :].astype(F32)
        hy2 = cch_ref[14:15, :].astype(F32) * cuh_ref[14:15, :].astype(F32)
        hy1, hy2 = jnp.where(first, 0.0, hy1), jnp.where(first, 0.0, hy2)
        z = (cw_ref[0:1, :] * _shift_rows(y, 2, [hy2, hy1], True) + cw_ref[1:2, :] * _shift_rows(y, 1, [hy1], True)
             + cw_ref[2:3, :] * y)
        zb = z + cbias_ref[...]
        zb_ref[...] = zb.astype(BF)
        co = (cb_ref[...].astype(F32) * zb).astype(BF)
        co_ref[...] = co
        yc = _dot(co, wco_ref[...], NN)
        ya = _dot(att_ref[...], wao_ref[...], TN)
        ya_ref[...] = ya.astype(BF)
        yc_ref[...] = yc.astype(BF)
        sa = jax.nn.sigmoid(ga_ref[...].astype(F32) + bg_ref[0:1, :])
        sc = jax.nn.sigmoid(gc_ref[...].astype(F32) + bg_ref[1:2, :])
        mg = (sa * ya + sc * yc).astype(BF)
        mg_ref[...] = mg
        x1_ref[...] = x_ref[...] + _dot(mg, wo_ref[...], NN)

    tok = lambda: pl.BlockSpec((tm, D), lambda i: (i, 0))
    seg = lambda s: pl.BlockSpec((tm, D), lambda i: (i, s))
    halo = lambda s: pl.BlockSpec((16, D), lambda i: (jnp.maximum(i * (tm // 16) - 1, 0), s))
    wsp = lambda k: _resident((None, D, D), lambda i: (k, 0, 0))
    return _call(
        body, name=f"mixer_fwd{l}", grid=(T // tm,),
        in_specs=[tok(), seg(0), seg(1), seg(2), seg(3), seg(4), halo(1), halo(2),
                  pl.BlockSpec((D, tm), lambda i: (0, i)), wsp(0), wsp(1), wsp(2), LROWS(l, 2), LROWS(l, 8), LROWS(l)],
        out_specs=[tok()] * 6,
        out_shape=[jax.ShapeDtypeStruct((T, D), dt) for dt in (F32, BF, BF, BF, BF, BF)],
        args=(x, gt, gt, gt, gt, gt, gt, gt, att, w3, w3, w3, bg, cw, cbias))


def _mlp_fwd(x1, g, w2, l, tm, head=None, side=None, relay_at=1.0):
    T = x1.shape[0]
    nt = T // tm
    FC = 1024

    def body(x_ref, g_ref, wup_ref, wdn_ref, *rest):
        out_ref, h_ref, a_ref = rest[-4:-1] if head else rest
        if head:
            t_ref, gf_ref, acc_ref = rest[0], rest[1], rest[-1]
            _zero_at_first_step(acc_ref)
        for rows in _halves(tm):
            x = x_ref[rows, :]
            h = _rms(x, g_ref[...]).astype(BF)
            h_ref[rows, :] = h
            acc = x
            for c in range(F // FC):
                a = _dot(h, wup_ref[c * FC:(c + 1) * FC, :], NT)
                a_ref[rows, c * FC:(c + 1) * FC] = a.astype(BF)
                u = jnp.maximum(a, 0.0)
                acc = acc + _dot((u * u).astype(BF), wdn_ref[c * FC:(c + 1) * FC, :], NN)
            if not head:
                out_ref[rows, :] = acc
                continue
            gf = gf_ref[...]
            err = _rms(acc, gf) - t_ref[rows, :]
            out_ref[rows, :], dg = _rms_bwd(err * (1.0 / D), acc, gf)
            acc_ref[0:1, :] += dg
            acc_ref[1:2, :] += jnp.sum(err * err, axis=0, keepdims=True)
        if not head:
            return

        @pl.when(pl.program_id(0) == nt - 1)
        def _():
            acc_ref[1:2, :] = jnp.zeros((1, D), F32) + (0.5 / D) * jnp.sum(acc_ref[1:2, :])

    tok = lambda w: pl.BlockSpec((tm, w), lambda i: (i, 0))
    wsp = lambda k: _resident((None, F, D), lambda i: (k, 0, 0))
    return _call(
        body, name=f"mlp_fwd{l}", grid=(nt,),
        in_specs=[tok(D), LROWS(l), wsp(0), wsp(1)] + ([tok(D), ROW1()] if head else []),
        out_specs=[tok(D), tok(D), tok(F)] + ([ACC()] if head else []),
        out_shape=[jax.ShapeDtypeStruct((T, D), F32), jax.ShapeDtypeStruct((T, D), BF), jax.ShapeDtypeStruct((T, F), BF)]
        + ([jax.ShapeDtypeStruct((8, D), F32)] if head else []),
        args=(x1, g, w2, w2) + (tuple(head) if head else ()), side=side, relay_at=relay_at)


def _mlp_bwd(dx2, x1, a, g, w2, l, tm, side=None):
    T = dx2.shape[0]
    FC = 1024

    def body(d_ref, x_ref, a_ref, g_ref, wup_ref, wdn_ref, da_ref, dx1_ref, db_ref, acc_ref):
        _zero_at_first_step(acc_ref)
        for rows in _halves(tm):
            d = d_ref[rows, :]
            db = d.astype(BF)
            db_ref[rows, :] = db
            dh = jnp.zeros(d.shape, F32)
            for c in range(F // FC):
                du = _dot(db, wdn_ref[c * FC:(c + 1) * FC, :], NT)
                da = (du * (2.0 * jnp.maximum(a_ref[rows, c * FC:(c + 1) * FC].astype(F32), 0.0))).astype(BF)
                da_ref[rows, c * FC:(c + 1) * FC] = da
                dh = dh + _dot(da, wup_ref[c * FC:(c + 1) * FC, :], NN)
            dx, dg = _rms_bwd(dh, x_ref[rows, :], g_ref[...])
            dx1_ref[rows, :] = d + dx
            acc_ref[0:1, :] += dg

    tok = lambda w: pl.BlockSpec((tm, w), lambda i: (i, 0))
    wsp = lambda k: _resident((None, F, D), lambda i: (k, 0, 0))
    return _call(
        body, name=f"mlp_bwd{l}", grid=(T // tm,),
        in_specs=[tok(D), tok(D), tok(F), LROWS(l), wsp(0), wsp(1)],
        out_specs=[tok(F), tok(D), tok(D), ACC()],
        out_shape=[jax.ShapeDtypeStruct((T, F), BF), jax.ShapeDtypeStruct((T, D), F32),
                   jax.ShapeDtypeStruct((T, D), BF), jax.ShapeDtypeStruct((8, D), F32)],
        args=(dx2, x1, a, g, w2, w2), side=side)


def _mixer_bwd(dx1, gt, ya, yc, zb, w3, bg, cw, l, S, tm, side=None):
    T = dx1.shape[0]
    nt = T // tm

    def body(d_ref, cb_ref, cc_ref, cu_ref, ga_ref, gc_ref, ya_ref, yc_ref, zb_ref, wao_ref, wco_ref, wo_ref,
             bg_ref, cw_ref, dg_ref, datt_ref, dya_ref, dyc_ref, db_ref, acc_ref, carry_ref):
        ti = nt - 1 - pl.program_id(0)
        _zero_at_first_step(acc_ref)

        @pl.when(((ti + 1) * tm) % S == 0)
        def _():
            carry_ref[...] = jnp.zeros_like(carry_ref)

        db = d_ref[...].astype(BF)
        db_ref[...] = db
        dm = _dot(db, wo_ref[...], NT)
        sa = jax.nn.sigmoid(ga_ref[...].astype(F32) + bg_ref[0:1, :])
        sc = jax.nn.sigmoid(gc_ref[...].astype(F32) + bg_ref[1:2, :])
        dya32 = dm * sa
        dyc32 = dm * sc
        dya = dya32.astype(BF)
        dyc = dyc32.astype(BF)
        dya_ref[...] = dya
        dyc_ref[...] = dyc
        dga = dya32 * ya_ref[...].astype(F32) * (1.0 - sa)
        dgc = dyc32 * yc_ref[...].astype(F32) * (1.0 - sc)
        dg_ref[:, 3 * D:4 * D] = dga.astype(BF)
        dg_ref[:, 4 * D:5 * D] = dgc.astype(BF)
        acc_ref[0:1, :] += jnp.sum(dga, axis=0, keepdims=True)
        acc_ref[1:2, :] += jnp.sum(dgc, axis=0, keepdims=True)
        datt_ref[...] = _dot(wao_ref[...], dya, NT).astype(BF)
        dco = _dot(dyc, wco_ref[...], NT)

        cc = cc_ref[...].astype(F32)
        cu = cu_ref[...].astype(F32)
        y = cc * cu
        dg_ref[:, 0:D] = (dco * zb_ref[...].astype(F32)).astype(BF)
        dz = dco * cb_ref[...].astype(F32)
        u1 = _shift_rows(dz, 1, [carry_ref[0:1, :]], False)
        u2 = _shift_rows(dz, 2, [carry_ref[0:1, :], carry_ref[1:2, :]], False)
        acc_ref[2:3, :] += jnp.sum(dz, axis=0, keepdims=True)
        acc_ref[3:4, :] += jnp.sum(u2 * y, axis=0, keepdims=True)
        acc_ref[4:5, :] += jnp.sum(u1 * y, axis=0, keepdims=True)
        acc_ref[5:6, :] += jnp.sum(dz * y, axis=0, keepdims=True)
        dy = cw_ref[2:3, :] * dz + cw_ref[1:2, :] * u1 + cw_ref[0:1, :] * u2
        dg_ref[:, D:2 * D] = (dy * cu).astype(BF)
        dg_ref[:, 2 * D:3 * D] = (dy * cc).astype(BF)
        carry_ref[...] = dz[0:8, :]

    tok = lambda w=D: pl.BlockSpec((tm, w), lambda i: (nt - 1 - i, 0))
    seg = lambda s: pl.BlockSpec((tm, D), lambda i: (nt - 1 - i, s))
    wsp = lambda k: _resident((None, D, D), lambda i: (k, 0, 0))
    return _call(
        body, name=f"mixer_bwd{l}", grid=(nt,),
        in_specs=[tok(), seg(0), seg(1), seg(2), seg(3), seg(4), tok(), tok(), tok(), wsp(0), wsp(1), wsp(2),
                  LROWS(l, 2), LROWS(l, 8)],
        out_specs=[tok(NG), pl.BlockSpec((D, tm), lambda i: (0, nt - 1 - i)), tok(), tok(), tok(), ACC()],
        out_shape=[jax.ShapeDtypeStruct((T, NG), BF), jax.ShapeDtypeStruct((D, T), BF)]
        + [jax.ShapeDtypeStruct((T, D), BF)] * 3 + [jax.ShapeDtypeStruct((8, D), F32)],
        scratch_shapes=[pltpu.VMEM((8, D), F32)],
        args=(dx1, gt, gt, gt, gt, gt, ya, yc, zb, w3, w3, w3, bg, cw), side=side)


def _attn_bwd(q, k, v, att, datt, sm, l, S, side=None):
    T = q.shape[1]
    nblk = S // WIN
    unroll = next(u for u in (15, 5, 3, 1) if (nblk - 1) % u == 0)
    scale = HD ** -0.5

    def body(sm_all_ref, q_ref, k_ref, v_ref, o_ref, do_ref, dq_ref, dk_ref, dv_ref, ds_ref, dka_ref, dva_ref):
        sm_ref = sm_all_ref.at[l]
        pj = pl.program_id(1)
        biases = _pair_biases(sm_ref, pj)
        dka_ref[...] = jnp.zeros_like(dka_ref)
        dva_ref[...] = jnp.zeros_like(dva_ref)

        def block(i, first, dsinks):
            r0 = 0 if first else pl.multiple_of(i * WIN, WIN)
            p0 = 0 if first else pl.multiple_of(i * WIN - WIN, WIN)
            out = []
            for kvh in range(2):
                head0 = (2 * pj + kvh) * GQ
                rows = slice(kvh * HD, (kvh + 1) * HD)
                dob = _heads_on_lanes(do_ref, kvh, r0)
                vt = v_ref[rows, pl.ds(p0, 2 * WIN)]
                dpts = [_dot(vt, dob[:, g * WIN:(g + 1) * WIN], TN) for g in range(GQ)]
                qt, kt, heads = _band_probs(sm_ref, head0, q_ref, k_ref, r0, p0, kvh, biases[first])
                inv = jnp.concatenate([h[2] for h in heads], axis=1)
                do32 = dob.astype(F32)
                delta = jnp.sum(do32 * _heads_on_lanes(o_ref, kvh, r0).astype(F32), axis=0, keepdims=True) * inv
                dosb = (do32 * inv).astype(BF)
                dsts = [(p * (dpts[g] * h_inv - delta[:, g * WIN:(g + 1) * WIN])).astype(BF)
                        for g, (p, _, h_inv) in enumerate(heads)]
                for g in range(GQ):
                    hr = slice((kvh * GQ + g) * HD, (kvh * GQ + g + 1) * HD)
                    dq_ref[hr, pl.ds(r0, WIN)] = (_dot(kt, dsts[g], NN) * scale).astype(BF)
                dst = jnp.concatenate(dsts, axis=1)
                pt = jnp.concatenate([p.astype(BF) for p, _, _ in heads], axis=1)
                dka_ref[rows, pl.ds(p0, 2 * WIN)] += _dot(qt, dst, NT)
                dva_ref[rows, pl.ds(p0, 2 * WIN)] += _dot(dosb, pt, NT)
                ps = jnp.concatenate([h[1] for h in heads], axis=1)
                out.append(dsinks[kvh] - ps * delta)
            return tuple(out)

        zero = jnp.zeros((1, GQ * WIN), F32)
        def rest(t, c):
            for u in range(unroll):
                c = block(1 + unroll * t + u, False, c)
            return c

        dsinks = lax.fori_loop(0, (nblk - 1) // unroll, rest, block(0, True, (zero, zero)))
        for kvh in range(2):
            for g in range(GQ):
                tot = jnp.sum(dsinks[kvh][:, g * WIN:(g + 1) * WIN])
                ds_ref[kvh * GQ + g:kvh * GQ + g + 1, :] = jnp.zeros((1, 128), F32) + tot
        dk_ref[...] = dka_ref[...].astype(BF)
        dv_ref[...] = dva_ref[...].astype(BF)

    wide = lambda: pl.BlockSpec((2 * GQ * HD, S), lambda s, p: (p, s))
    narrow = lambda: pl.BlockSpec((2 * HD, S), lambda s, p: (p, s))
    return _call(
        body, name=f"attn_bwd{l}", grid=(T // S, 2),
        in_specs=[pl.BlockSpec(memory_space=pltpu.SMEM), wide(), narrow(), narrow(), wide(), wide()],
        out_specs=[wide(), narrow(), narrow(), pl.BlockSpec((None, None, 8, 128), lambda s, p: (s, p, 0, 0))],
        out_shape=[jax.ShapeDtypeStruct((NQ, T), BF), jax.ShapeDtypeStruct((NKV, T), BF),
                   jax.ShapeDtypeStruct((NKV, T), BF), jax.ShapeDtypeStruct((T // S, 2, 8, 128), F32)],
        scratch_shapes=[pltpu.VMEM((2 * HD, S), F32), pltpu.VMEM((2 * HD, S), F32)],
        args=(sm, q, k, v, att, datt), side=side)


def _inproj_bwd(dgt, dq, dk, dv, x, dres, g, wint, l, tm, side=None):
    T = x.shape[0]

    def body(dg_ref, dq_ref, dk_ref, dv_ref, x_ref, dr_ref, g_ref, w_ref, dx_ref, acc_ref):
        _zero_at_first_step(acc_ref)
        for rows in _halves(tm):
            dh = _dot(dq_ref[:, rows], w_ref[0:NQ, :], TN)
            dh = dh + _dot(dk_ref[:, rows], w_ref[NQ:NQ + NKV, :], TN)
            dh = dh + _dot(dv_ref[:, rows], w_ref[NQ + NKV:NQ + 2 * NKV, :], TN)
            dh = dh + _dot(dg_ref[rows, :], w_ref[NQ + 2 * NKV:NP, :], NN)
            dx, dg = _rms_bwd(dh, x_ref[rows, :], g_ref[...])
            dx_ref[rows, :] = dr_ref[rows, :] + dx
            acc_ref[0:1, :] += dg

    tok = lambda w: pl.BlockSpec((tm, w), lambda i: (i, 0))
    feat = lambda w: pl.BlockSpec((w, tm), lambda i: (0, i))
    return _call(
        body, name=f"inproj_bwd{l}", grid=(T // tm,),
        in_specs=[tok(NG), feat(NQ), feat(NKV), feat(NKV), tok(D), tok(D), LROWS(l), _resident((NP, D), lambda i: (0, 0))],
        out_specs=[tok(D), ACC()],
        out_shape=[jax.ShapeDtypeStruct((T, D), F32), jax.ShapeDtypeStruct((8, D), F32)],
        args=(dgt, dq, dk, dv, x, dres, g, wint), side=side)


def _wgrad(a, b, rows, row0, into, name, relu2=False):
    T, M = a.shape
    tmm = next(t for t in (1024, 512, 256) if M % t == 0 and row0 % t == 0)
    tk = min(4096 if M > 4096 else 2048, T)
    nk = T // tk
    blk0 = row0 // tmm

    def body(*refs):
        a_ref, b_ref = refs[0], refs[1]
        o_ref, acc_ref = refs[-2], refs[-1]
        kk = pl.program_id(1)

        @pl.when(kk == 0)
        def _():
            acc_ref[...] = jnp.zeros_like(acc_ref)

        av = a_ref[...]
        if relu2:
            t = jnp.maximum(av.astype(F32), 0.0)
            av = (t * t).astype(BF)
        acc_ref[...] += _dot(av, b_ref[...], TN)

        @pl.when(kk == nk - 1)
        def _():
            o_ref[...] = acc_ref[...].astype(BF)

    in_specs = [pl.BlockSpec((tk, tmm), lambda j, kk: (kk, j)), pl.BlockSpec((tk, D), lambda j, kk: (kk, 0))]
    args = [a, b]
    if into is not None:
        in_specs.append(ANY)
        args.append(into)
    (out,), _ = _call(
        body, name=name, grid=(M // tmm, nk), in_specs=in_specs,
        out_specs=[pl.BlockSpec((tmm, D), lambda j, kk: (blk0 + j, 0))],
        out_shape=[jax.ShapeDtypeStruct((rows, D), BF)], scratch_shapes=[pltpu.VMEM((tmm, D), F32)],
        aliases={2: 0} if into is not None else None, args=args)
    return out


def _wgrad_qkv(dq, dk, dv, h, into, name):
    T = h.shape[0]
    rows = NQ + 2 * NKV
    tk = min(2048, T)
    nk = T // tk

    def body(dq_ref, dk_ref, dv_ref, h_ref, _, o_ref, acc_ref):
        kk = pl.program_id(0)

        @pl.when(kk == 0)
        def _():
            acc_ref[...] = jnp.zeros_like(acc_ref)

        hv = h_ref[...]
        acc_ref[0:NQ, :] += _dot(dq_ref[...], hv, NN)
        acc_ref[NQ:NQ + NKV, :] += _dot(dk_ref[...], hv, NN)
        acc_ref[NQ + NKV:rows, :] += _dot(dv_ref[...], hv, NN)

        @pl.when(kk == nk - 1)
        def _():
            o_ref[...] = acc_ref[...].astype(BF)

    feat = lambda w: pl.BlockSpec((w, tk), lambda kk: (0, kk))
    (out,), _ = _call(
        body, name=name, grid=(nk,),
        in_specs=[feat(NQ), feat(NKV), feat(NKV), pl.BlockSpec((tk, D), lambda kk: (kk, 0)), ANY],
        out_specs=[pl.BlockSpec((rows, D), lambda kk: (0, 0))],
        out_shape=[jax.ShapeDtypeStruct((NP, D), BF)], scratch_shapes=[pltpu.VMEM((rows, D), F32)],
        aliases={4: 0}, args=[dq, dk, dv, h, into])
    return out


def _wgrad_square(jobs, name):
    n = len(jobs)
    T = jobs[0][1].shape[0]
    tk = min(1024, T)
    nk = T // tk

    def body(*refs):
        ins, outs, acc_ref = refs[:2 * n], refs[2 * n:3 * n], refs[3 * n]
        j, kk = pl.program_id(0), pl.program_id(1)

        @pl.when(kk == 0)
        def _():
            acc_ref[...] = jnp.zeros_like(acc_ref)

        for p, (_, _, a_t) in enumerate(jobs):
            @pl.when(j == p)
            def _(p=p, a_t=a_t):
                acc_ref[...] += _dot(ins[2 * p][...], ins[2 * p + 1][...], NN if a_t else TN)

                @pl.when(kk == nk - 1)
                def _():
                    outs[p][...] = acc_ref[...].astype(BF)

    def step(p):
        return lambda j, kk: jnp.where(j == p, kk, jnp.where(j > p, nk - 1, 0))

    in_specs, args = [], []
    for p, (a, b, a_t) in enumerate(jobs):
        at = step(p)
        in_specs.append(pl.BlockSpec((D, tk), lambda j, kk, at=at: (0, at(j, kk))) if a_t
                        else pl.BlockSpec((tk, D), lambda j, kk, at=at: (at(j, kk), 0)))
        in_specs.append(pl.BlockSpec((tk, D), lambda j, kk, at=at: (at(j, kk), 0)))
        args += [a, b]
    outs, _ = _call(
        body, name=name, grid=(n, nk), in_specs=in_specs,
        out_specs=[pl.BlockSpec((D, D), lambda j, kk: (0, 0))] * n,
        out_shape=[jax.ShapeDtypeStruct((D, D), BF)] * n, scratch_shapes=[pltpu.VMEM((D, D), F32)], args=args)
    return outs


def _adamw(w, g, m, v):
    m = B1 * m + (1.0 - B1) * g
    v = B2 * v + (1.0 - B2) * (g * g)
    m_hat = m / (1.0 - B1 ** STEP)
    v_hat = v / (1.0 - B2 ** STEP)
    return -LR * (m_hat / (jnp.sqrt(v_hat) + AEPS) + WD * w), m, v


def _adam_sum(land, w, m, v, l, into, name, transposed=False):
    _, r, _ = land.shape
    tr = 208 if r % 208 == 0 else (256 if r % 256 == 0 else r)
    tc = 256

    def body(land_ref, w_ref, m_ref, v_ref, *rest):
        g_ref, d_ref, nm_ref, nv_ref = rest[-4:]
        g = land_ref[0].astype(F32)
        for s in range(1, NDEV):
            g = g + land_ref[s].astype(F32)
        if transposed:
            g = g.T
        g_ref[...] = g
        d_ref[...], nm_ref[...], nv_ref[...] = _adamw(w_ref[...], g, m_ref[...], v_ref[...])

    if transposed:
        blk = lambda: pl.BlockSpec((None, tc, r), lambda j: (l, j, 0))
        land_spec, grid = pl.BlockSpec((NDEV, r, tc), lambda j: (0, 0, j)), (D // tc,)
    else:
        blk = lambda: pl.BlockSpec((None, tr, D), lambda j: (l, j, 0))
        land_spec, grid = pl.BlockSpec((NDEV, tr, D), lambda j: (0, j, 0)), (r // tr,)
    in_specs = [land_spec, blk(), blk(), blk()]
    args = [land, w, m, v]
    aliases = None
    if into is not None:
        in_specs += [ANY] * 4
        args += list(into)
        aliases = {4 + t: t for t in range(4)}
    outs, _ = _call(body, name=name, grid=grid, in_specs=in_specs, out_specs=[blk()] * 4,
                    out_shape=[jax.ShapeDtypeStruct(w.shape, F32)] * 4, aliases=aliases, args=args)
    return outs


SMALL_NAMES = ("g_mix", "b_gates", "sinks", "conv_b", "g_mlp", "g_final", "conv_w")


def _adam_small(land, me, masters):
    n = len(SMALL_NAMES)
    lanes = D // NDEV

    def body(land_ref, me_ref, *refs):
        ins, outs, loss_ref, gs_ref = refs[:3 * n], refs[3 * n:7 * n], refs[7 * n], refs[7 * n + 1]
        g = land_ref[0]
        for s in range(1, NDEV):
            g = g + land_ref[s]
        gs_ref[...] = g

        def update(k, g_piece, idx):
            w_ref, m_ref, v_ref = ins[3 * k:3 * k + 3]
            outs[4 * k][idx] = g_piece
            outs[4 * k + 1][idx], outs[4 * k + 2][idx], outs[4 * k + 3][idx] = _adamw(w_ref[idx], g_piece, m_ref[idx], v_ref[idx])

        whole = (slice(None), slice(None))
        update(0, gs_ref[0:2, :], whole)
        for l in range(L):
            for h in range(2):
                update(1, gs_ref[2 + 2 * l + h:3 + 2 * l + h, :], (slice(l, l + 1), slice(h * D, (h + 1) * D)))
            update(2, gs_ref[6:7, 16 * l:16 * (l + 1)], (slice(l, l + 1), slice(None)))
        update(3, gs_ref[7:9, :], whole)
        update(4, gs_ref[9:11, :], whole)
        update(5, gs_ref[11:12, :], whole)
        mine = pl.ds(pl.multiple_of(me_ref[0] * lanes, lanes), lanes)
        for l in range(L):
            for k in range(3):
                update(6, gs_ref[12 + 3 * l + k:13 + 3 * l + k, mine], (l, slice(k, k + 1), slice(None)))
        loss_ref[...] = gs_ref[18:19, 0:1]

    flat = [t for name in SMALL_NAMES for t in masters[name]]
    vmem = pl.BlockSpec(memory_space=pltpu.VMEM)
    outs = pl.pallas_call(
        body, name="adam_small",
        in_specs=[vmem, pl.BlockSpec(memory_space=pltpu.SMEM)] + [vmem] * len(flat),
        out_shape=[jax.ShapeDtypeStruct(masters[name][0].shape, F32) for name in SMALL_NAMES for _ in range(4)]
        + [jax.ShapeDtypeStruct((1, 1), F32)],
        scratch_shapes=[pltpu.VMEM((SMALL_ROWS, D), F32)],
        compiler_params=pltpu.CompilerParams(vmem_limit_bytes=VMEM_LIMIT))(land, me, *flat)
    return {name: outs[4 * k:4 * k + 4] for k, name in enumerate(SMALL_NAMES)}, outs[-1]


def _pack_small(g_mix, b_gates, sinks, conv_b, g_mlp, g_final, conv_w_rows, extra):
    sink_row = jnp.zeros((1, D), F32).at[0, :2 * 16].set(sinks.reshape(-1))
    return jnp.concatenate([g_mix, b_gates.reshape(4, D), sink_row, conv_b, g_mlp, g_final.reshape(1, D),
                            conv_w_rows, extra, jnp.zeros((SMALL_ROWS - 19, D), F32)], axis=0)


def kernel(x, g_mix, w_in, b_gates, sinks, w_attn_out, conv_w, conv_b, w_conv_out, w_o, g_mlp, w_up, w_down, g_final, loss_target, m_g_mix, m_w_in, m_b_gates, m_sinks, m_w_attn_out, m_conv_w, m_conv_b, m_w_conv_out, m_w_o, m_g_mlp, m_w_up, m_w_down, m_g_final, v_g_mix, v_w_in, v_b_gates, v_sinks, v_w_attn_out, v_conv_w, v_conv_b, v_w_conv_out, v_w_o, v_g_mlp, v_w_up, v_w_down, v_g_final):
    nseq, S, _ = x.shape
    T = nseq * S
    tm_in = min(512, S)
    tm = min(512, S)
    xi, yi, ci = _position()
    me = 4 * xi + 2 * yi + ci
    tr = lambda t: jnp.swapaxes(t, 1, 2)
    blocks = lambda t: t.reshape(NDEV, t.shape[0] // NDEV, D)

    win_t, wup_t = tr(w_in), tr(w_up)
    sh_win = [win_t[l].astype(BF)[None] for l in range(L)]
    sh_w3 = [jnp.stack([w_attn_out[l], w_conv_out[l], w_o[l]]).astype(BF) for l in range(L)]
    sh_w2 = [jnp.stack([wup_t[l], w_down[l]]).astype(BF) for l in range(L)]
    wint, w3, w2 = [None] * L, [None] * L, [None] * L
    wint0_g, cw_g = _remote_only(_Gather([sh_win[0], jnp.pad(conv_w, ((0, 0), (0, 5), (0, 0)))]), "gather_first")
    wint[0] = wint0_g.reshape(NP, D)
    cw = jnp.swapaxes(cw_g, 1, 2).reshape(L, 8, D)
    slopes = np.power(np.float32(2.0), -8.0 * np.arange(1, 17, dtype=np.float32) / 16).astype(np.float32)
    sm = jnp.stack([sinks, jnp.broadcast_to(jnp.asarray(slopes), sinks.shape)], axis=1)
    per_layer = lambda t, n=1: t.reshape(L, n, D)
    g_mix3, g_mlp3, conv_b3, bg = per_layer(g_mix), per_layer(g_mlp), per_layer(conv_b), per_layer(b_gates, 2)

    xf = x.reshape(T, D)
    saved = []
    cur = xf
    for l in range(L):
        (gt, q, k, v, h), got = _inproj_fwd(cur, g_mix3, wint[l], l, tm_in, side=_Gather([sh_w2[l]]),
                                            relay_at=0.85)
        w2[l] = got[0].reshape(2, F, D)
        (att,), got = _attn_fwd(q, k, v, sm, l, S, side=_Gather([sh_w3[l]]), relay_at=0.5)
        w3[l] = got[0].reshape(3, D, D)
        (x1, mg, co, ya, yc, zb), _ = _mixer_fwd(cur, gt, att, w3[l], bg, cw, conv_b3, l, S, tm)
        head = (loss_target.reshape(T, D), g_final.reshape(1, D)) if l == L - 1 else None
        (nxt, h2, a, *acc_loss), got = _mlp_fwd(x1, g_mlp3, w2[l], l, tm_in, head=head,
                                                side=_Gather([sh_win[l + 1]]) if l + 1 < L else None, relay_at=0.7)
        if l + 1 < L:
            wint[l + 1] = got[0].reshape(NP, D)
        saved.append(dict(x=cur, gt=gt, q=q, k=k, v=v, h=h, att=att, x1=x1, mg=mg, co=co, ya=ya, yc=yc, zb=zb, h2=h2, a=a))
        cur = nxt
    dcur, acc_loss = cur, acc_loss[0]

    masters = {"w_in": (win_t, tr(m_w_in), tr(v_w_in)), "w_attn_out": (w_attn_out, m_w_attn_out, v_w_attn_out),
               "w_conv_out": (w_conv_out, m_w_conv_out, v_w_conv_out), "w_o": (w_o, m_w_o, v_w_o),
               "w_up": (w_up, m_w_up, v_w_up), "w_down": (w_down, m_w_down, v_w_down)}
    big = {name: None for name in masters}

    def adam(name, land, l):
        big[name] = _adam_sum(land, *masters[name], l, big[name], f"adam_{name}{l}", transposed=name == "w_up")

    acc_in, acc_mix, acc_mlp, dsink = [None] * L, [None] * L, [None] * L, [None] * L
    d_win_above = None
    for l in reversed(range(L)):
        sv = saved[l]
        side = _Exchange([blocks(d_win_above)]) if l + 1 < L else None
        (da, dx1, dx2b, acc_mlp[l]), got = _mlp_bwd(dcur, sv["x1"], sv["a"], g_mlp3, w2[l], l, tm_in, side=side)
        if l + 1 < L:
            adam("w_in", got[0], l + 1)
        d_wdn = _wgrad(sv["a"], dx2b, F, 0, None, f"wgrad_down{l}", relu2=True)
        d_wup = _wgrad(da, sv["h2"], F, 0, None, f"wgrad_up{l}")
        (dgt, datt, dya, dyc, dx1b, acc_mix[l]), got = _mixer_bwd(
            dx1, sv["gt"], sv["ya"], sv["yc"], sv["zb"], w3[l], bg, cw, l, S, tm,
            side=_Exchange([blocks(d_wdn)]))
        adam("w_down", got[0], l)
        d_wo, d_wao, d_wco = _wgrad_square(
            [(sv["mg"], dx1b, False), (sv["att"], dya, True), (sv["co"], dyc, False)], f"wgrad_mixer{l}")
        d_win = _wgrad(dgt, sv["h"], NP, NQ + 2 * NKV, None, f"wgrad_gates{l}")
        (dq, dk, dv, dsink[l]), got = _attn_bwd(
            sv["q"], sv["k"], sv["v"], sv["att"], datt, sm, l, S,
            side=_Exchange([blocks(d_wup), blocks(d_wo), blocks(d_wao), blocks(d_wco)]))
        for name, land in zip(["w_up", "w_o", "w_attn_out", "w_conv_out"], got):
            adam(name, land, l)
        d_win = _wgrad_qkv(dq, dk, dv, sv["h"], d_win, f"wgrad_qkv{l}")
        side = _Exchange([blocks(d_win)]) if l == 0 else None
        (dcur, acc_in[l]), got_in = _inproj_bwd(dgt, dq, dk, dv, sv["x"], dx1, g_mix3, wint[l], l, tm_in, side=side)
        d_win_above = d_win

    dsinks = jnp.stack([dsink[l][:, :, :, 0].sum(axis=0).reshape(16) for l in range(L)])
    small = _pack_small(
        jnp.concatenate([acc_in[l][0:1] for l in range(L)]),
        jnp.stack([acc_mix[l][0:2].reshape(2 * D) for l in range(L)]),
        dsinks,
        jnp.concatenate([acc_mix[l][2:3] for l in range(L)]),
        jnp.concatenate([acc_mlp[l][0:1] for l in range(L)]),
        acc_loss[0],
        jnp.concatenate([acc_mix[l][3:6] for l in range(L)]),
        acc_loss[1:2])
    adam("w_in", got_in[0], 0)
    (small_land,) = _remote_only(_Exchange([], [small]), "exchange_small")

    row = lambda t: t.reshape(1, D)
    small_out, loss = _adam_small(small_land, me.reshape(1).astype(jnp.int32), {
        "g_mix": (g_mix, m_g_mix, v_g_mix), "b_gates": (b_gates, m_b_gates, v_b_gates), "sinks": (sinks, m_sinks, v_sinks),
        "conv_b": (conv_b, m_conv_b, v_conv_b), "g_mlp": (g_mlp, m_g_mlp, v_g_mlp),
        "g_final": (row(g_final), row(m_g_final), row(v_g_final)), "conv_w": (conv_w, m_conv_w, v_conv_w)})
    small_out["g_final"] = [t.reshape(D) for t in small_out["g_final"]]
    big["w_in"] = [tr(o) for o in big["w_in"]]
    order = ["g_mix", "w_in", "b_gates", "sinks", "w_attn_out", "conv_w", "conv_b", "w_conv_out", "w_o", "g_mlp",
             "w_up", "w_down", "g_final"]
    out = [loss.reshape(()), dcur.reshape(nseq, S, D)]
    for kind in range(4):
        for name in order:
            out.append(big[name][kind] if name in big else small_out[name][kind])
    return tuple(out)
```

```python
import numpy as np
import jax
import jax.numpy as jnp
from jax import lax
from jax.experimental import pallas as pl
from jax.experimental.pallas import tpu as pltpu

D = 1024
NG = 5 * D
NQ = 1024
NKV = 256
NP = NQ + 2 * NKV + NG
F = 4096
HD = 64
GQ = 4
WIN = 128
L = 2
NDEV = 8
EPS = 1e-6
NEG = -1e30
SMALL_ROWS = 24
LR, B1, B2, AEPS, WD, STEP = 0.001, 0.9, 0.999, 1e-08, 0.01, 10

BF = jnp.bfloat16
F32 = jnp.float32
MESH = pl.DeviceIdType.MESH
VMEM_LIMIT = 60 * 1024 * 1024
ANY = pl.BlockSpec(memory_space=pl.ANY)

NN = ((1,), (0,))
NT = ((1,), (1,))
TN = ((0,), (0,))


def _dot(a, b, dims):
    return lax.dot_general(a, b, (dims, ((), ())), preferred_element_type=F32)


def _resident(shape, imap):
    return pl.BlockSpec(shape, imap, pipeline_mode=pl.Buffered(1))


def _position():
    return lax.axis_index("x"), lax.axis_index("y"), lax.axis_index("c")


class _Gather:
    def __init__(self, shards):
        n = len(shards)
        self.inputs = list(shards)
        self.out_shape = [jax.ShapeDtypeStruct((s.shape[0], NDEV) + s.shape[1:], s.dtype) for s in shards]
        self.scratch = [pltpu.SemaphoreType.DMA((n, 7)), pltpu.SemaphoreType.DMA((n, 7)), pltpu.SemaphoreType.DMA((n,))]

    def _plan(self, src, dst, sems):
        send_sems, recv_sems, local_sems = sems
        n = len(src)
        x, y, c = _position()
        me, sibling = (x, y, c), (x, y, 1 - c)
        chips = [(1 - x, y), (x, 1 - y), (1 - x, 1 - y)]

        def rows(a, p):
            return dst[a].at[:, 4 * p[0] + 2 * p[1] + p[2]]

        def copy(a, k, block, to, from_src=False):
            return pltpu.make_async_remote_copy(
                src_ref=src[a] if from_src else rows(a, block), dst_ref=rows(a, block),
                send_sem=send_sems.at[a, k], recv_sem=recv_sems.at[a, k], device_id=to, device_id_type=MESH)

        mine = [pltpu.make_async_copy(src[a], rows(a, me), local_sems.at[a]) for a in range(n)]
        first = []
        for a in range(n):
            first.append(copy(a, 0, me, sibling, True))
            first += [copy(a, 1 + t, me, (*chip, c), True) for t, chip in enumerate(chips)]
        return n, c, me, sibling, chips, copy, mine, first

    def start(self, src, dst, sems):
        *_, mine, first = self._plan(src, dst, sems)
        for cp in mine + first:
            cp.start()

    def relay(self, src, dst, sems):
        n, c, me, sibling, chips, copy, _, _ = self._plan(src, dst, sems)
        for t, chip in enumerate(chips):
            for a in range(n):
                copy(a, 1 + t, (*chip, c), me).wait_recv()
                copy(a, 4 + t, (*chip, c), sibling).start()

    def finish(self, src, dst, sems):
        n, c, me, sibling, chips, copy, mine, first = self._plan(src, dst, sems)
        for a in range(n):
            copy(a, 0, sibling, me).wait_recv()
            for t, chip in enumerate(chips):
                copy(a, 4 + t, (*chip, 1 - c), me).wait_recv()
        for cp in first + [copy(a, 4 + t, (*chip, c), sibling) for t, chip in enumerate(chips) for a in range(n)]:
            cp.wait_send()
        for cp in mine:
            cp.wait()


class _Exchange:
    def __init__(self, grads, everyone=()):
        self.inputs = list(grads) + list(everyone)
        self.n_blocked = len(grads)
        n = len(self.inputs)
        self.out_shape = [jax.ShapeDtypeStruct(g.shape, g.dtype) for g in grads]
        self.out_shape += [jax.ShapeDtypeStruct((NDEV,) + e.shape, e.dtype) for e in everyone]
        self.scratch = [pltpu.SemaphoreType.DMA((n, 7)), pltpu.SemaphoreType.DMA((n, 7)), pltpu.SemaphoreType.DMA((n,))]

    def _plan(self, src, land, sems):
        send_sems, recv_sems, local_sems = sems
        x, y, c = _position()
        me = 4 * x + 2 * y + c

        def parts(peer_idx):
            return [(s.at[peer_idx] if a < self.n_blocked else s, land[a].at[me]) for a, s in enumerate(src)]

        local = [pltpu.make_async_copy(s, d, local_sems.at[a]) for a, (s, d) in enumerate(parts(me))]
        sent = []
        for rel in range(1, NDEV):
            px = 1 - x if rel & 4 else x
            py = 1 - y if rel & 2 else y
            pc = 1 - c if rel & 1 else c
            for a, (s, d) in enumerate(parts(4 * px + 2 * py + pc)):
                sent.append(pltpu.make_async_remote_copy(
                    src_ref=s, dst_ref=d, send_sem=send_sems.at[a, rel - 1], recv_sem=recv_sems.at[a, rel - 1],
                    device_id=(px, py, pc), device_id_type=MESH))
        return local, sent

    def start(self, src, land, sems):
        local, sent = self._plan(src, land, sems)
        for cp in local + sent:
            cp.start()

    def relay(self, src, land, sems):
        pass

    def finish(self, src, land, sems):
        local, sent = self._plan(src, land, sems)
        for cp in sent:
            cp.wait_recv()
        for cp in sent:
            cp.wait_send()
        for cp in local:
            cp.wait()


def _call(body, *, name, grid, in_specs, out_specs, out_shape, args, scratch_shapes=(), aliases=None, side=None,
          relay_at=1.0):
    sem = ("arbitrary",) * len(grid)
    if side is None:
        outs = pl.pallas_call(
            body, name=name, grid=grid, in_specs=in_specs, out_specs=out_specs, out_shape=out_shape,
            scratch_shapes=list(scratch_shapes), input_output_aliases=aliases or {},
            compiler_params=pltpu.CompilerParams(dimension_semantics=sem, vmem_limit_bytes=VMEM_LIMIT))(*args)
        return outs, []
    ni, no, ns = len(in_specs), len(out_specs), len(scratch_shapes)
    si, so = len(side.inputs), len(side.out_shape)

    def hosted(*refs):
        ins, refs = refs[:ni], refs[ni:]
        sins, refs = refs[:si], refs[si:]
        outs, refs = refs[:no], refs[no:]
        souts, refs = refs[:so], refs[so:]
        scr, sscr = refs[:ns], refs[ns:]
        step = pl.program_id(0)
        for d in range(1, len(grid)):
            step = step * grid[d] + pl.program_id(d)
        last = int(np.prod(grid)) - 1

        @pl.when(step == 0)
        def _():
            side.start(sins, souts, sscr)

        body(*ins, *outs, *scr)

        @pl.when(step == min(int(relay_at * last), last))
        def _():
            side.relay(sins, souts, sscr)

        @pl.when(step == last)
        def _():
            side.finish(sins, souts, sscr)

    outs = pl.pallas_call(
        hosted, name=name, grid=grid, in_specs=list(in_specs) + [ANY] * si, out_specs=list(out_specs) + [ANY] * so,
        out_shape=list(out_shape) + side.out_shape, scratch_shapes=list(scratch_shapes) + side.scratch,
        input_output_aliases=aliases or {},
        compiler_params=pltpu.CompilerParams(dimension_semantics=sem, vmem_limit_bytes=VMEM_LIMIT, has_side_effects=True),
    )(*args, *side.inputs)
    return outs[:no], outs[no:]


def _remote_only(side, name):
    n = len(side.inputs)

    def body(*refs):
        src, dst, sems = refs[:n], refs[n:n + len(side.out_shape)], refs[n + len(side.out_shape):]
        side.start(src, dst, sems)
        side.relay(src, dst, sems)
        side.finish(src, dst, sems)

    return pl.pallas_call(
        body, name=name, in_specs=[ANY] * n, out_specs=[ANY] * len(side.out_shape), out_shape=side.out_shape,
        scratch_shapes=side.scratch, compiler_params=pltpu.CompilerParams(has_side_effects=True))(*side.inputs)


def _rms(x, g):
    r = lax.rsqrt(jnp.mean(x * x, axis=-1, keepdims=True) + EPS)
    return x * r * g


def _rms_bwd(dy, x, g):
    r = lax.rsqrt(jnp.mean(x * x, axis=-1, keepdims=True) + EPS)
    xh = x * r
    dxh = dy * g
    dx = r * (dxh - xh * jnp.mean(dxh * xh, axis=-1, keepdims=True))
    return dx, jnp.sum(dy * xh, axis=0, keepdims=True)


def _halves(n):
    return [slice(0, n // 2), slice(n // 2, n)] if n % 256 == 0 else [slice(0, n)]


def _zero_at_first_step(acc_ref):
    first = pl.program_id(0) == 0

    @pl.when(first)
    def _():
        acc_ref[...] = jnp.zeros_like(acc_ref)


ROW1 = lambda: _resident((1, D), lambda i: (0, 0))
LROWS = lambda l, n=1: _resident((None, n, D), lambda *_: (l, 0, 0))
ACC = lambda: pl.BlockSpec((8, D), lambda i: (0, 0))


def _inproj_fwd(x, g, wint, l, tm, side=None, relay_at=1.0):
    T = x.shape[0]

    def body(x_ref, g_ref, w_ref, gt_ref, q_ref, k_ref, v_ref, h_ref):
        for rows in _halves(tm):
            h = _rms(x_ref[rows, :], g_ref[...]).astype(BF)
            h_ref[rows, :] = h
            q_ref[:, rows] = _dot(w_ref[0:NQ, :], h, NT).astype(BF)
            k_ref[:, rows] = _dot(w_ref[NQ:NQ + NKV, :], h, NT).astype(BF)
            v_ref[:, rows] = _dot(w_ref[NQ + NKV:NQ + 2 * NKV, :], h, NT).astype(BF)
            for s in range(5):
                lo = NQ + 2 * NKV + s * D
                gt_ref[rows, s * D:(s + 1) * D] = _dot(h, w_ref[lo:lo + D, :], NT).astype(BF)

    tok = lambda w: pl.BlockSpec((tm, w), lambda i: (i, 0))
    feat = lambda w: pl.BlockSpec((w, tm), lambda i: (0, i))
    return _call(
        body, name=f"inproj_fwd{l}", grid=(T // tm,),
        in_specs=[tok(D), LROWS(l), _resident((NP, D), lambda i: (0, 0))],
        out_specs=[tok(NG), feat(NQ), feat(NKV), feat(NKV), tok(D)],
        out_shape=[jax.ShapeDtypeStruct((T, NG), BF)] + [jax.ShapeDtypeStruct((w, T), BF) for w in (NQ, NKV, NKV)]
        + [jax.ShapeDtypeStruct((T, D), BF)],
        args=(x, g, wint), side=side, relay_at=relay_at)


def _band_geometry():
    j = lax.broadcasted_iota(jnp.int32, (2 * WIN, WIN), 0)
    r = lax.broadcasted_iota(jnp.int32, (2 * WIN, WIN), 1)
    dist = WIN + r - j
    dist0 = r - j
    return (dist.astype(F32), (dist >= 0) & (dist < WIN)), (dist0.astype(F32), dist0 >= 0)


def _pair_biases(sm_ref, pj):
    return [[jnp.where(ok, -sm_ref[1, pj * 2 * GQ + h] * dist, NEG) for h in range(2 * GQ)]
            for dist, ok in _band_geometry()]


def _reduce_rows(x, pair, whole):
    while x.shape[0] > 8:
        half = x.shape[0] // 2
        x = pair(x[:half], x[half:])
    return whole(x, axis=0, keepdims=True)


def _heads_on_lanes(ref, kvh, r0):
    return jnp.concatenate([ref[(kvh * GQ + g) * HD:(kvh * GQ + g + 1) * HD, pl.ds(r0, WIN)] for g in range(GQ)], axis=1)


def _band_probs(sm_ref, head0, q_ref, k_ref, r0, p0, kvh, biases):
    qt = _heads_on_lanes(q_ref, kvh, r0) * jnp.asarray(HD ** -0.5, BF)
    kt = k_ref[kvh * HD:(kvh + 1) * HD, pl.ds(p0, 2 * WIN)]
    heads = []
    for g in range(GQ):
        sink = sm_ref[0, head0 + g]
        s = _dot(kt, qt[:, g * WIN:(g + 1) * WIN], TN) + biases[kvh * GQ + g]
        m = jnp.maximum(_reduce_rows(s, jnp.maximum, jnp.max), sink)
        p = jnp.exp(s - m)
        ps = jnp.exp(sink - m)
        heads.append((p, ps, 1.0 / (_reduce_rows(p, jnp.add, jnp.sum) + ps)))
    return qt, kt, heads


def _attn_fwd(q, k, v, sm, l, S, side=None, relay_at=1.0):
    T = q.shape[1]
    nblk = S // WIN
    unroll = next(u for u in (15, 5, 3, 1) if (nblk - 1) % u == 0)

    def body(sm_all_ref, q_ref, k_ref, v_ref, o_ref):
        sm_ref = sm_all_ref.at[l]
        pj = pl.program_id(1)
        biases = _pair_biases(sm_ref, pj)

        def block(i, first):
            r0 = 0 if first else pl.multiple_of(i * WIN, WIN)
            p0 = 0 if first else pl.multiple_of(i * WIN - WIN, WIN)
            for kvh in range(2):
                head0 = (2 * pj + kvh) * GQ
                _, _, heads = _band_probs(sm_ref, head0, q_ref, k_ref, r0, p0, kvh, biases[first])
                vt = v_ref[kvh * HD:(kvh + 1) * HD, pl.ds(p0, 2 * WIN)]
                for g, (p, _, inv) in enumerate(heads):
                    rows = slice((kvh * GQ + g) * HD, (kvh * GQ + g + 1) * HD)
                    o_ref[rows, pl.ds(r0, WIN)] = (_dot(vt, p.astype(BF), NN) * inv).astype(BF)

        block(0, True)

        def rest(t, c):
            for u in range(unroll):
                block(1 + unroll * t + u, False)
            return c

        lax.fori_loop(0, (nblk - 1) // unroll, rest, 0)

    wide = lambda: pl.BlockSpec((2 * GQ * HD, S), lambda s, p: (p, s))
    narrow = lambda: pl.BlockSpec((2 * HD, S), lambda s, p: (p, s))
    return _call(
        body, name=f"attn_fwd{l}", grid=(T // S, 2),
        in_specs=[pl.BlockSpec(memory_space=pltpu.SMEM), wide(), narrow(), narrow()],
        out_specs=[wide()], out_shape=[jax.ShapeDtypeStruct((D, T), BF)],
        args=(sm, q, k, v), side=side, relay_at=relay_at)


def _shift_rows(y, k, edge_rows, down):
    n = y.shape[0]
    rid = lax.broadcasted_iota(jnp.int32, y.shape, 0)
    out = pltpu.roll(y, k if down else n - k, 0)
    for t, row in enumerate(edge_rows):
        out = jnp.where(rid == (t if down else n - k + t), row, out)
    return out


def _mixer_fwd(x, gt, att, w3, bg, cw, cbias, l, S, tm):
    T = x.shape[0]

    def body(x_ref, cb_ref, cc_ref, cu_ref, ga_ref, gc_ref, cch_ref, cuh_ref, att_ref, wao_ref, wco_ref, wo_ref,
             bg_ref, cw_ref, cbias_ref, x1_ref, mg_ref, co_ref, ya_ref, yc_ref, zb_ref):
        first = (pl.program_id(0) * tm) % S == 0
        y = cc_ref[...].astype(F32) * cu_ref[...].astype(F32)
        hy1 = cch_ref[15:16, :].astype(F32) * cuh_ref[15:16, :].astype(F32)
        hy2 = cch_ref[14:15, :].astype(F32) * cuh_ref[14:15, :].astype(F32)
        hy1, hy2 = jnp.where(first, 0.0, hy1), jnp.where(first, 0.0, hy2)
        z = (cw_ref[0:1, :] * _shift_rows(y, 2, [hy2, hy1], True) + cw_ref[1:2, :] * _shift_rows(y, 1, [hy1], True)
             + cw_ref[2:3, :] * y)
        zb = z + cbias_ref[...]
        zb_ref[...] = zb.astype(BF)
        co = (cb_ref[...].astype(F32) * zb).astype(BF)
        co_ref[...] = co
        yc = _dot(co, wco_ref[...], NN)
        ya = _dot(att_ref[...], wao_ref[...], TN)
        ya_ref[...] = ya.astype(BF)
        yc_ref[...] = yc.astype(BF)
        sa = jax.nn.sigmoid(ga_ref[...].astype(F32) + bg_ref[0:1, :])
        sc = jax.nn.sigmoid(gc_ref[...].astype(F32) + bg_ref[1:2, :])
        mg = (sa * ya + sc * yc).astype(BF)
        mg_ref[...] = mg
        x1_ref[...] = x_ref[...] + _dot(mg, wo_ref[...], NN)

    tok = lambda: pl.BlockSpec((tm, D), lambda i: (i, 0))
    seg = lambda s: pl.BlockSpec((tm, D), lambda i: (i, s))
    halo = lambda s: pl.BlockSpec((16, D), lambda i: (jnp.maximum(i * (tm // 16) - 1, 0), s))
    wsp = lambda k: _resident((None, D, D), lambda i: (k, 0, 0))
    return _call(
        body, name=f"mixer_fwd{l}", grid=(T // tm,),
        in_specs=[tok(), seg(0), seg(1), seg(2), seg(3), seg(4), halo(1), halo(2),
                  pl.BlockSpec((D, tm), lambda i: (0, i)), wsp(0), wsp(1), wsp(2), LROWS(l, 2), LROWS(l, 8), LROWS(l)],
        out_specs=[tok()] * 6,
        out_shape=[jax.ShapeDtypeStruct((T, D), dt) for dt in (F32, BF, BF, BF, BF, BF)],
        args=(x, gt, gt, gt, gt, gt, gt, gt, att, w3, w3, w3, bg, cw, cbias))


def _mlp_fwd(x1, g, w2, l, tm, head=None, side=None, relay_at=1.0):
    T = x1.shape[0]
    nt = T // tm
    FC = 1024

    def body(x_ref, g_ref, wup_ref, wdn_ref, *rest):
        out_ref, h_ref, a_ref = rest[-4:-1] if head else rest
        if head:
            t_ref, gf_ref, acc_ref = rest[0], rest[1], rest[-1]
            _zero_at_first_step(acc_ref)
        for rows in _halves(tm):
            x = x_ref[rows, :]
            h = _rms(x, g_ref[...]).astype(BF)
            h_ref[rows, :] = h
            acc = x
            for c in range(F // FC):
                a = _dot(h, wup_ref[c * FC:(c + 1) * FC, :], NT)
                a_ref[rows, c * FC:(c + 1) * FC] = a.astype(BF)
                u = jnp.maximum(a, 0.0)
                acc = acc + _dot((u * u).astype(BF), wdn_ref[c * FC:(c + 1) * FC, :], NN)
            if not head:
                out_ref[rows, :] = acc
                continue
            gf = gf_ref[...]
            err = _rms(acc, gf) - t_ref[rows, :]
            out_ref[rows, :], dg = _rms_bwd(err * (1.0 / D), acc, gf)
            acc_ref[0:1, :] += dg
            acc_ref[1:2, :] += jnp.sum(err * err, axis=0, keepdims=True)
        if not head:
            return

        @pl.when(pl.program_id(0) == nt - 1)
        def _():
            acc_ref[1:2, :] = jnp.zeros((1, D), F32) + (0.5 / D) * jnp.sum(acc_ref[1:2, :])

    tok = lambda w: pl.BlockSpec((tm, w), lambda i: (i, 0))
    wsp = lambda k: _resident((None, F, D), lambda i: (k, 0, 0))
    return _call(
        body, name=f"mlp_fwd{l}", grid=(nt,),
        in_specs=[tok(D), LROWS(l), wsp(0), wsp(1)] + ([tok(D), ROW1()] if head else []),
        out_specs=[tok(D), tok(D), tok(F)] + ([ACC()] if head else []),
        out_shape=[jax.ShapeDtypeStruct((T, D), F32), jax.ShapeDtypeStruct((T, D), BF), jax.ShapeDtypeStruct((T, F), BF)]
        + ([jax.ShapeDtypeStruct((8, D), F32)] if head else []),
        args=(x1, g, w2, w2) + (tuple(head) if head else ()), side=side, relay_at=relay_at)


def _mlp_bwd(dx2, x1, a, g, w2, l, tm, side=None):
    T = dx2.shape[0]
    FC = 1024

    def body(d_ref, x_ref, a_ref, g_ref, wup_ref, wdn_ref, da_ref, dx1_ref, db_ref, acc_ref):
        _zero_at_first_step(acc_ref)
        for rows in _halves(tm):
            d = d_ref[rows, :]
            db = d.astype(BF)
            db_ref[rows, :] = db
            dh = jnp.zeros(d.shape, F32)
            for c in range(F // FC):
                du = _dot(db, wdn_ref[c * FC:(c + 1) * FC, :], NT)
                da = (du * (2.0 * jnp.maximum(a_ref[rows, c * FC:(c + 1) * FC].astype(F32), 0.0))).astype(BF)
                da_ref[rows, c * FC:(c + 1) * FC] = da
                dh = dh + _dot(da, wup_ref[c * FC:(c + 1) * FC, :], NN)
            dx, dg = _rms_bwd(dh, x_ref[rows, :], g_ref[...])
            dx1_ref[rows, :] = d + dx
            acc_ref[0:1, :] += dg

    tok = lambda w: pl.BlockSpec((tm, w), lambda i: (i, 0))
    wsp = lambda k: _resident((None, F, D), lambda i: (k, 0, 0))
    return _call(
        body, name=f"mlp_bwd{l}", grid=(T // tm,),
        in_specs=[tok(D), tok(D), tok(F), LROWS(l), wsp(0), wsp(1)],
        out_specs=[tok(F), tok(D), tok(D), ACC()],
        out_shape=[jax.ShapeDtypeStruct((T, F), BF), jax.ShapeDtypeStruct((T, D), F32),
                   jax.ShapeDtypeStruct((T, D), BF), jax.ShapeDtypeStruct((8, D), F32)],
        args=(dx2, x1, a, g, w2, w2), side=side)


def _mixer_bwd(dx1, gt, ya, yc, zb, w3, bg, cw, l, S, tm, side=None):
    T = dx1.shape[0]
    nt = T // tm

    def body(d_ref, cb_ref, cc_ref, cu_ref, ga_ref, gc_ref, ya_ref, yc_ref, zb_ref, wao_ref, wco_ref, wo_ref,
             bg_ref, cw_ref, dg_ref, datt_ref, dya_ref, dyc_ref, db_ref, acc_ref, carry_ref):
        ti = nt - 1 - pl.program_id(0)
        _zero_at_first_step(acc_ref)

        @pl.when(((ti + 1) * tm) % S == 0)
        def _():
            carry_ref[...] = jnp.zeros_like(carry_ref)

        db = d_ref[...].astype(BF)
        db_ref[...] = db
        dm = _dot(db, wo_ref[...], NT)
        sa = jax.nn.sigmoid(ga_ref[...].astype(F32) + bg_ref[0:1, :])
        sc = jax.nn.sigmoid(gc_ref[...].astype(F32) + bg_ref[1:2, :])
        dya32 = dm * sa
        dyc32 = dm * sc
        dya = dya32.astype(BF)
        dyc = dyc32.astype(BF)
        dya_ref[...] = dya
        dyc_ref[...] = dyc
        dga = dya32 * ya_ref[...].astype(F32) * (1.0 - sa)
        dgc = dyc32 * yc_ref[...].astype(F32) * (1.0 - sc)
        dg_ref[:, 3 * D:4 * D] = dga.astype(BF)
        dg_ref[:, 4 * D:5 * D] = dgc.astype(BF)
        acc_ref[0:1, :] += jnp.sum(dga, axis=0, keepdims=True)
        acc_ref[1:2, :] += jnp.sum(dgc, axis=0, keepdims=True)
        datt_ref[...] = _dot(wao_ref[...], dya, NT).astype(BF)
        dco = _dot(dyc, wco_ref[...], NT)

        cc = cc_ref[...].astype(F32)
        cu = cu_ref[...].astype(F32)
        y = cc * cu
        dg_ref[:, 0:D] = (dco * zb_ref[...].astype(F32)).astype(BF)
        dz = dco * cb_ref[...].astype(F32)
        u1 = _shift_rows(dz, 1, [carry_ref[0:1, :]], False)
        u2 = _shift_rows(dz, 2, [carry_ref[0:1, :], carry_ref[1:2, :]], False)
        acc_ref[2:3, :] += jnp.sum(dz, axis=0, keepdims=True)
        acc_ref[3:4, :] += jnp.sum(u2 * y, axis=0, keepdims=True)
        acc_ref[4:5, :] += jnp.sum(u1 * y, axis=0, keepdims=True)
        acc_ref[5:6, :] += jnp.sum(dz * y, axis=0, keepdims=True)
        dy = cw_ref[2:3, :] * dz + cw_ref[1:2, :] * u1 + cw_ref[0:1, :] * u2
        dg_ref[:, D:2 * D] = (dy * cu).astype(BF)
        dg_ref[:, 2 * D:3 * D] = (dy * cc).astype(BF)
        carry_ref[...] = dz[0:8, :]

    tok = lambda w=D: pl.BlockSpec((tm, w), lambda i: (nt - 1 - i, 0))
    seg = lambda s: pl.BlockSpec((tm, D), lambda i: (nt - 1 - i, s))
    wsp = lambda k: _resident((None, D, D), lambda i: (k, 0, 0))
    return _call(
        body, name=f"mixer_bwd{l}", grid=(nt,),
        in_specs=[tok(), seg(0), seg(1), seg(2), seg(3), seg(4), tok(), tok(), tok(), wsp(0), wsp(1), wsp(2),
                  LROWS(l, 2), LROWS(l, 8)],
        out_specs=[tok(NG), pl.BlockSpec((D, tm), lambda i: (0, nt - 1 - i)), tok(), tok(), tok(), ACC()],
        out_shape=[jax.ShapeDtypeStruct((T, NG), BF), jax.ShapeDtypeStruct((D, T), BF)]
        + [jax.ShapeDtypeStruct((T, D), BF)] * 3 + [jax.ShapeDtypeStruct((8, D), F32)],
        scratch_shapes=[pltpu.VMEM((8, D), F32)],
        args=(dx1, gt, gt, gt, gt, gt, ya, yc, zb, w3, w3, w3, bg, cw), side=side)


def _attn_bwd(q, k, v, att, datt, sm, l, S, side=None):
    T = q.shape[1]
    nblk = S // WIN
    unroll = next(u for u in (15, 5, 3, 1) if (nblk - 1) % u == 0)
    scale = HD ** -0.5

    def body(sm_all_ref, q_ref, k_ref, v_ref, o_ref, do_ref, dq_ref, dk_ref, dv_ref, ds_ref, dka_ref, dva_ref):
        sm_ref = sm_all_ref.at[l]
        pj = pl.program_id(1)
        biases = _pair_biases(sm_ref, pj)
        dka_ref[...] = jnp.zeros_like(dka_ref)
        dva_ref[...] = jnp.zeros_like(dva_ref)

        def block(i, first, dsinks):
            r0 = 0 if first else pl.multiple_of(i * WIN, WIN)
            p0 = 0 if first else pl.multiple_of(i * WIN - WIN, WIN)
            out = []
            for kvh in range(2):
                head0 = (2 * pj + kvh) * GQ
                rows = slice(kvh * HD, (kvh + 1) * HD)
                dob = _heads_on_lanes(do_ref, kvh, r0)
                vt = v_ref[rows, pl.ds(p0, 2 * WIN)]
                dpts = [_dot(vt, dob[:, g * WIN:(g + 1) * WIN], TN) for g in range(GQ)]
                qt, kt, heads = _band_probs(sm_ref, head0, q_ref, k_ref, r0, p0, kvh, biases[first])
                inv = jnp.concatenate([h[2] for h in heads], axis=1)
                do32 = dob.astype(F32)
                delta = jnp.sum(do32 * _heads_on_lanes(o_ref, kvh, r0).astype(F32), axis=0, keepdims=True) * inv
                dosb = (do32 * inv).astype(BF)
                dsts = [(p * (dpts[g] * h_inv - delta[:, g * WIN:(g + 1) * WIN])).astype(BF)
                        for g, (p, _, h_inv) in enumerate(heads)]
                for g in range(GQ):
                    hr = slice((kvh * GQ + g) * HD, (kvh * GQ + g + 1) * HD)
                    dq_ref[hr, pl.ds(r0, WIN)] = (_dot(kt, dsts[g], NN) * scale).astype(BF)
                lanes = lambda g: slice(g * WIN, (g + 1) * WIN)
                dka_ref[rows, pl.ds(p0, 2 * WIN)] += sum(_dot(qt[:, lanes(g)], dsts[g], NT) for g in range(GQ))
                dva_ref[rows, pl.ds(p0, 2 * WIN)] += sum(
                    _dot(dosb[:, lanes(g)], p.astype(BF), NT) for g, (p, _, _) in enumerate(heads))
                ps = jnp.concatenate([h[1] for h in heads], axis=1)
                out.append(dsinks[kvh] - ps * delta)
            return tuple(out)

        zero = jnp.zeros((1, GQ * WIN), F32)
        def rest(t, c):
            for u in range(unroll):
                c = block(1 + unroll * t + u, False, c)
            return c

        dsinks = lax.fori_loop(0, (nblk - 1) // unroll, rest, block(0, True, (zero, zero)))
        for kvh in range(2):
            for g in range(GQ):
                tot = jnp.sum(dsinks[kvh][:, g * WIN:(g + 1) * WIN])
                ds_ref[kvh * GQ + g:kvh * GQ + g + 1, :] = jnp.zeros((1, 128), F32) + tot
        dk_ref[...] = dka_ref[...].astype(BF)
        dv_ref[...] = dva_ref[...].astype(BF)

    wide = lambda: pl.BlockSpec((2 * GQ * HD, S), lambda s, p: (p, s))
    narrow = lambda: pl.BlockSpec((2 * HD, S), lambda s, p: (p, s))
    return _call(
        body, name=f"attn_bwd{l}", grid=(T // S, 2),
        in_specs=[pl.BlockSpec(memory_space=pltpu.SMEM), wide(), narrow(), narrow(), wide(), wide()],
        out_specs=[wide(), narrow(), narrow(), pl.BlockSpec((None, None, 8, 128), lambda s, p: (s, p, 0, 0))],
        out_shape=[jax.ShapeDtypeStruct((NQ, T), BF), jax.ShapeDtypeStruct((NKV, T), BF),
                   jax.ShapeDtypeStruct((NKV, T), BF), jax.ShapeDtypeStruct((T // S, 2, 8, 128), F32)],
        scratch_shapes=[pltpu.VMEM((2 * HD, S), F32), pltpu.VMEM((2 * HD, S), F32)],
        args=(sm, q, k, v, att, datt), side=side)


def _inproj_bwd(dgt, dq, dk, dv, x, dres, g, wint, l, tm, side=None):
    T = x.shape[0]

    def body(dg_ref, dq_ref, dk_ref, dv_ref, x_ref, dr_ref, g_ref, w_ref, dx_ref, acc_ref):
        _zero_at_first_step(acc_ref)
        for rows in _halves(tm):
            dh = _dot(dq_ref[:, rows], w_ref[0:NQ, :], TN)
            dh = dh + _dot(dk_ref[:, rows], w_ref[NQ:NQ + NKV, :], TN)
            dh = dh + _dot(dv_ref[:, rows], w_ref[NQ + NKV:NQ + 2 * NKV, :], TN)
            dh = dh + _dot(dg_ref[rows, :], w_ref[NQ + 2 * NKV:NP, :], NN)
            dx, dg = _rms_bwd(dh, x_ref[rows, :], g_ref[...])
            dx_ref[rows, :] = dr_ref[rows, :] + dx
            acc_ref[0:1, :] += dg

    tok = lambda w: pl.BlockSpec((tm, w), lambda i: (i, 0))
    feat = lambda w: pl.BlockSpec((w, tm), lambda i: (0, i))
    return _call(
        body, name=f"inproj_bwd{l}", grid=(T // tm,),
        in_specs=[tok(NG), feat(NQ), feat(NKV), feat(NKV), tok(D), tok(D), LROWS(l), _resident((NP, D), lambda i: (0, 0))],
        out_specs=[tok(D), ACC()],
        out_shape=[jax.ShapeDtypeStruct((T, D), F32), jax.ShapeDtypeStruct((8, D), F32)],
        args=(dgt, dq, dk, dv, x, dres, g, wint), side=side)


def _wgrad(a, b, rows, row0, into, name, relu2=False):
    T, M = a.shape
    tmm = next(t for t in (1024, 512, 256) if M % t == 0 and row0 % t == 0)
    tk = min(4096 if M > 4096 else 2048, T)
    nk = T // tk
    blk0 = row0 // tmm

    def body(*refs):
        a_ref, b_ref = refs[0], refs[1]
        o_ref, acc_ref = refs[-2], refs[-1]
        kk = pl.program_id(1)

        @pl.when(kk == 0)
        def _():
            acc_ref[...] = jnp.zeros_like(acc_ref)

        av = a_ref[...]
        if relu2:
            t = jnp.maximum(av.astype(F32), 0.0)
            av = (t * t).astype(BF)
        acc_ref[...] += _dot(av, b_ref[...], TN)

        @pl.when(kk == nk - 1)
        def _():
            o_ref[...] = acc_ref[...].astype(BF)

    in_specs = [pl.BlockSpec((tk, tmm), lambda j, kk: (kk, j)), pl.BlockSpec((tk, D), lambda j, kk: (kk, 0))]
    args = [a, b]
    if into is not None:
        in_specs.append(ANY)
        args.append(into)
    (out,), _ = _call(
        body, name=name, grid=(M // tmm, nk), in_specs=in_specs,
        out_specs=[pl.BlockSpec((tmm, D), lambda j, kk: (blk0 + j, 0))],
        out_shape=[jax.ShapeDtypeStruct((rows, D), BF)], scratch_shapes=[pltpu.VMEM((tmm, D), F32)],
        aliases={2: 0} if into is not None else None, args=args)
    return out


def _wgrad_qkv(dq, dk, dv, h, into, name):
    T = h.shape[0]
    rows = NQ + 2 * NKV
    tk = min(2048, T)
    nk = T // tk

    def body(dq_ref, dk_ref, dv_ref, h_ref, _, o_ref, acc_ref):
        kk = pl.program_id(0)

        @pl.when(kk == 0)
        def _():
            acc_ref[...] = jnp.zeros_like(acc_ref)

        hv = h_ref[...]
        acc_ref[0:NQ, :] += _dot(dq_ref[...], hv, NN)
        acc_ref[NQ:NQ + NKV, :] += _dot(dk_ref[...], hv, NN)
        acc_ref[NQ + NKV:rows, :] += _dot(dv_ref[...], hv, NN)

        @pl.when(kk == nk - 1)
        def _():
            o_ref[...] = acc_ref[...].astype(BF)

    feat = lambda w: pl.BlockSpec((w, tk), lambda kk: (0, kk))
    (out,), _ = _call(
        body, name=name, grid=(nk,),
        in_specs=[feat(NQ), feat(NKV), feat(NKV), pl.BlockSpec((tk, D), lambda kk: (kk, 0)), ANY],
        out_specs=[pl.BlockSpec((rows, D), lambda kk: (0, 0))],
        out_shape=[jax.ShapeDtypeStruct((NP, D), BF)], scratch_shapes=[pltpu.VMEM((rows, D), F32)],
        aliases={4: 0}, args=[dq, dk, dv, h, into])
    return out


def _wgrad_square(jobs, name):
    n = len(jobs)
    T = jobs[0][1].shape[0]
    tk = min(1024, T)
    nk = T // tk

    def body(*refs):
        ins, outs, acc_ref = refs[:2 * n], refs[2 * n:3 * n], refs[3 * n]
        j, kk = pl.program_id(0), pl.program_id(1)

        @pl.when(kk == 0)
        def _():
            acc_ref[...] = jnp.zeros_like(acc_ref)

        for p, (_, _, a_t) in enumerate(jobs):
            @pl.when(j == p)
            def _(p=p, a_t=a_t):
                acc_ref[...] += _dot(ins[2 * p][...], ins[2 * p + 1][...], NN if a_t else TN)

                @pl.when(kk == nk - 1)
                def _():
                    outs[p][...] = acc_ref[...].astype(BF)

    def step(p):
        return lambda j, kk: jnp.where(j == p, kk, jnp.where(j > p, nk - 1, 0))

    in_specs, args = [], []
    for p, (a, b, a_t) in enumerate(jobs):
        at = step(p)
        in_specs.append(pl.BlockSpec((D, tk), lambda j, kk, at=at: (0, at(j, kk))) if a_t
                        else pl.BlockSpec((tk, D), lambda j, kk, at=at: (at(j, kk), 0)))
        in_specs.append(pl.BlockSpec((tk, D), lambda j, kk, at=at: (at(j, kk), 0)))
        args += [a, b]
    outs, _ = _call(
        body, name=name, grid=(n, nk), in_specs=in_specs,
        out_specs=[pl.BlockSpec((D, D), lambda j, kk: (0, 0))] * n,
        out_shape=[jax.ShapeDtypeStruct((D, D), BF)] * n, scratch_shapes=[pltpu.VMEM((D, D), F32)], args=args)
    return outs


def _adamw(w, g, m, v):
    m = B1 * m + (1.0 - B1) * g
    v = B2 * v + (1.0 - B2) * (g * g)
    m_hat = m / (1.0 - B1 ** STEP)
    v_hat = v / (1.0 - B2 ** STEP)
    return -LR * (m_hat / (jnp.sqrt(v_hat) + AEPS) + WD * w), m, v


def _adam_sum(land, w, m, v, l, into, name, transposed=False):
    _, r, _ = land.shape
    tr = 208 if r % 208 == 0 else (256 if r % 256 == 0 else r)
    tc = 256

    def body(land_ref, w_ref, m_ref, v_ref, *rest):
        g_ref, d_ref, nm_ref, nv_ref = rest[-4:]
        g = land_ref[0].astype(F32)
        for s in range(1, NDEV):
            g = g + land_ref[s].astype(F32)
        if transposed:
            g = g.T
        g_ref[...] = g
        d_ref[...], nm_ref[...], nv_ref[...] = _adamw(w_ref[...], g, m_ref[...], v_ref[...])

    if transposed:
        blk = lambda: pl.BlockSpec((None, tc, r), lambda j: (l, j, 0))
        land_spec, grid = pl.BlockSpec((NDEV, r, tc), lambda j: (0, 0, j)), (D // tc,)
    else:
        blk = lambda: pl.BlockSpec((None, tr, D), lambda j: (l, j, 0))
        land_spec, grid = pl.BlockSpec((NDEV, tr, D), lambda j: (0, j, 0)), (r // tr,)
    in_specs = [land_spec, blk(), blk(), blk()]
    args = [land, w, m, v]
    aliases = None
    if into is not None:
        in_specs += [ANY] * 4
        args += list(into)
        aliases = {4 + t: t for t in range(4)}
    outs, _ = _call(body, name=name, grid=grid, in_specs=in_specs, out_specs=[blk()] * 4,
                    out_shape=[jax.ShapeDtypeStruct(w.shape, F32)] * 4, aliases=aliases, args=args)
    return outs


SMALL_NAMES = ("g_mix", "b_gates", "sinks", "conv_b", "g_mlp", "g_final", "conv_w")


def _adam_small(land, me, masters):
    n = len(SMALL_NAMES)
    lanes = D // NDEV

    def body(land_ref, me_ref, *refs):
        ins, outs, loss_ref, gs_ref = refs[:3 * n], refs[3 * n:7 * n], refs[7 * n], refs[7 * n + 1]
        g = land_ref[0]
        for s in range(1, NDEV):
            g = g + land_ref[s]
        gs_ref[...] = g

        def update(k, g_piece, idx):
            w_ref, m_ref, v_ref = ins[3 * k:3 * k + 3]
            outs[4 * k][idx] = g_piece
            outs[4 * k + 1][idx], outs[4 * k + 2][idx], outs[4 * k + 3][idx] = _adamw(w_ref[idx], g_piece, m_ref[idx], v_ref[idx])

        whole = (slice(None), slice(None))
        update(0, gs_ref[0:2, :], whole)
        for l in range(L):
            for h in range(2):
                update(1, gs_ref[2 + 2 * l + h:3 + 2 * l + h, :], (slice(l, l + 1), slice(h * D, (h + 1) * D)))
            update(2, gs_ref[6:7, 16 * l:16 * (l + 1)], (slice(l, l + 1), slice(None)))
        update(3, gs_ref[7:9, :], whole)
        update(4, gs_ref[9:11, :], whole)
        update(5, gs_ref[11:12, :], whole)
        mine = pl.ds(pl.multiple_of(me_ref[0] * lanes, lanes), lanes)
        for l in range(L):
            for k in range(3):
                update(6, gs_ref[12 + 3 * l + k:13 + 3 * l + k, mine], (l, slice(k, k + 1), slice(None)))
        loss_ref[...] = gs_ref[18:19, 0:1]

    flat = [t for name in SMALL_NAMES for t in masters[name]]
    vmem = pl.BlockSpec(memory_space=pltpu.VMEM)
    outs = pl.pallas_call(
        body, name="adam_small",
        in_specs=[vmem, pl.BlockSpec(memory_space=pltpu.SMEM)] + [vmem] * len(flat),
        out_shape=[jax.ShapeDtypeStruct(masters[name][0].shape, F32) for name in SMALL_NAMES for _ in range(4)]
        + [jax.ShapeDtypeStruct((1, 1), F32)],
        scratch_shapes=[pltpu.VMEM((SMALL_ROWS, D), F32)],
        compiler_params=pltpu.CompilerParams(vmem_limit_bytes=VMEM_LIMIT))(land, me, *flat)
    return {name: outs[4 * k:4 * k + 4] for k, name in enumerate(SMALL_NAMES)}, outs[-1]


def _pack_small(g_mix, b_gates, sinks, conv_b, g_mlp, g_final, conv_w_rows, extra):
    sink_row = jnp.zeros((1, D), F32).at[0, :2 * 16].set(sinks.reshape(-1))
    return jnp.concatenate([g_mix, b_gates.reshape(4, D), sink_row, conv_b, g_mlp, g_final.reshape(1, D),
                            conv_w_rows, extra, jnp.zeros((SMALL_ROWS - 19, D), F32)], axis=0)


def kernel(x, g_mix, w_in, b_gates, sinks, w_attn_out, conv_w, conv_b, w_conv_out, w_o, g_mlp, w_up, w_down, g_final, loss_target, m_g_mix, m_w_in, m_b_gates, m_sinks, m_w_attn_out, m_conv_w, m_conv_b, m_w_conv_out, m_w_o, m_g_mlp, m_w_up, m_w_down, m_g_final, v_g_mix, v_w_in, v_b_gates, v_sinks, v_w_attn_out, v_conv_w, v_conv_b, v_w_conv_out, v_w_o, v_g_mlp, v_w_up, v_w_down, v_g_final):
    nseq, S, _ = x.shape
    T = nseq * S
    tm_in = min(512, S)
    tm = min(512, S)
    xi, yi, ci = _position()
    me = 4 * xi + 2 * yi + ci
    tr = lambda t: jnp.swapaxes(t, 1, 2)
    blocks = lambda t: t.reshape(NDEV, t.shape[0] // NDEV, D)

    win_t, wup_t = tr(w_in), tr(w_up)
    sh_win = [win_t[l].astype(BF)[None] for l in range(L)]
    sh_w3 = [jnp.stack([w_attn_out[l], w_conv_out[l], w_o[l]]).astype(BF) for l in range(L)]
    sh_w2 = [jnp.stack([wup_t[l], w_down[l]]).astype(BF) for l in range(L)]
    wint, w3, w2 = [None] * L, [None] * L, [None] * L
    wint0_g, cw_g = _remote_only(_Gather([sh_win[0], jnp.pad(conv_w, ((0, 0), (0, 5), (0, 0)))]), "gather_first")
    wint[0] = wint0_g.reshape(NP, D)
    cw = jnp.swapaxes(cw_g, 1, 2).reshape(L, 8, D)
    slopes = np.power(np.float32(2.0), -8.0 * np.arange(1, 17, dtype=np.float32) / 16).astype(np.float32)
    sm = jnp.stack([sinks, jnp.broadcast_to(jnp.asarray(slopes), sinks.shape)], axis=1)
    per_layer = lambda t, n=1: t.reshape(L, n, D)
    g_mix3, g_mlp3, conv_b3, bg = per_layer(g_mix), per_layer(g_mlp), per_layer(conv_b), per_layer(b_gates, 2)

    xf = x.reshape(T, D)
    saved = []
    cur = xf
    for l in range(L):
        (gt, q, k, v, h), got = _inproj_fwd(cur, g_mix3, wint[l], l, tm_in, side=_Gather([sh_w2[l]]),
                                            relay_at=0.85)
        w2[l] = got[0].reshape(2, F, D)
        (att,), got = _attn_fwd(q, k, v, sm, l, S, side=_Gather([sh_w3[l]]), relay_at=0.5)
        w3[l] = got[0].reshape(3, D, D)
        (x1, mg, co, ya, yc, zb), _ = _mixer_fwd(cur, gt, att, w3[l], bg, cw, conv_b3, l, S, tm)
        head = (loss_target.reshape(T, D), g_final.reshape(1, D)) if l == L - 1 else None
        (nxt, h2, a, *acc_loss), got = _mlp_fwd(x1, g_mlp3, w2[l], l, tm_in, head=head,
                                                side=_Gather([sh_win[l + 1]]) if l + 1 < L else None, relay_at=0.7)
        if l + 1 < L:
            wint[l + 1] = got[0].reshape(NP, D)
        saved.append(dict(x=cur, gt=gt, q=q, k=k, v=v, h=h, att=att, x1=x1, mg=mg, co=co, ya=ya, yc=yc, zb=zb, h2=h2, a=a))
        cur = nxt
    dcur, acc_loss = cur, acc_loss[0]

    masters = {"w_in": (win_t, tr(m_w_in), tr(v_w_in)), "w_attn_out": (w_attn_out, m_w_attn_out, v_w_attn_out),
               "w_conv_out": (w_conv_out, m_w_conv_out, v_w_conv_out), "w_o": (w_o, m_w_o, v_w_o),
               "w_up": (w_up, m_w_up, v_w_up), "w_down": (w_down, m_w_down, v_w_down)}
    big = {name: None for name in masters}

    def adam(name, land, l):
        big[name] = _adam_sum(land, *masters[name], l, big[name], f"adam_{name}{l}", transposed=name == "w_up")

    acc_in, acc_mix, acc_mlp, dsink = [None] * L, [None] * L, [None] * L, [None] * L
    d_win_above = None
    for l in reversed(range(L)):
        sv = saved[l]
        side = _Exchange([blocks(d_win_above)]) if l + 1 < L else None
        (da, dx1, dx2b, acc_mlp[l]), got = _mlp_bwd(dcur, sv["x1"], sv["a"], g_mlp3, w2[l], l, tm_in, side=side)
        if l + 1 < L:
            adam("w_in", got[0], l + 1)
        d_wdn = _wgrad(sv["a"], dx2b, F, 0, None, f"wgrad_down{l}", relu2=True)
        d_wup = _wgrad(da, sv["h2"], F, 0, None, f"wgrad_up{l}")
        (dgt, datt, dya, dyc, dx1b, acc_mix[l]), got = _mixer_bwd(
            dx1, sv["gt"], sv["ya"], sv["yc"], sv["zb"], w3[l], bg, cw, l, S, tm,
            side=_Exchange([blocks(d_wdn)]))
        adam("w_down", got[0], l)
        d_wo, d_wao, d_wco = _wgrad_square(
            [(sv["mg"], dx1b, False), (sv["att"], dya, True), (sv["co"], dyc, False)], f"wgrad_mixer{l}")
        d_win = _wgrad(dgt, sv["h"], NP, NQ + 2 * NKV, None, f"wgrad_gates{l}")
        (dq, dk, dv, dsink[l]), got = _attn_bwd(
            sv["q"], sv["k"], sv["v"], sv["att"], datt, sm, l, S,
            side=_Exchange([blocks(d_wup), blocks(d_wo), blocks(d_wao), blocks(d_wco)]))
        for name, land in zip(["w_up", "w_o", "w_attn_out", "w_conv_out"], got):
            adam(name, land, l)
        d_win = _wgrad_qkv(dq, dk, dv, sv["h"], d_win, f"wgrad_qkv{l}")
        side = _Exchange([blocks(d_win)]) if l == 0 else None
        (dcur, acc_in[l]), got_in = _inproj_bwd(dgt, dq, dk, dv, sv["x"], dx1, g_mix3, wint[l], l, tm_in, side=side)
        d_win_above = d_win

    dsinks = jnp.stack([dsink[l][:, :, :, 0].sum(axis=0).reshape(16) for l in range(L)])
    small = _pack_small(
        jnp.concatenate([acc_in[l][0:1] for l in range(L)]),
        jnp.stack([acc_mix[l][0:2].reshape(2 * D) for l in range(L)]),
        dsinks,
        jnp.concatenate([acc_mix[l][2:3] for l in range(L)]),
        jnp.concatenate([acc_mlp[l][0:1] for l in range(L)]),
        acc_loss[0],
        jnp.concatenate([acc_mix[l][3:6] for l in range(L)]),
        acc_loss[1:2])
    adam("w_in", got_in[0], 0)
    (small_land,) = _remote_only(_Exchange([], [small]), "exchange_small")

    row = lambda t: t.reshape(1, D)
    small_out, loss = _adam_small(small_land, me.reshape(1).astype(jnp.int32), {
        "g_mix": (g_mix, m_g_mix, v_g_mix), "b_gates": (b_gates, m_b_gates, v_b_gates), "sinks": (sinks, m_sinks, v_sinks),
        "conv_b": (conv_b, m_conv_b, v_conv_b), "g_mlp": (g_mlp, m_g_mlp, v_g_mlp),
        "g_final": (row(g_final), row(m_g_final), row(v_g_final)), "conv_w": (conv_w, m_conv_w, v_conv_w)})
    small_out["g_final"] = [t.reshape(D) for t in small_out["g_final"]]
    big["w_in"] = [tr(o) for o in big["w_in"]]
    order = ["g_mix", "w_in", "b_gates", "sinks", "w_attn_out", "conv_w", "conv_b", "w_conv_out", "w_o", "g_mlp",
             "w_up", "w_down", "g_final"]
    out = [loss.reshape(()), dcur.reshape(nseq, S, D)]
    for kind in range(4):
        for name in order:
            out.append(big[name][kind] if name in big else small_out[name][kind])
    return tuple(out)
```

```python
import numpy as np
import jax
import jax.numpy as jnp
from jax import lax
from jax.experimental import pallas as pl
from jax.experimental.pallas import tpu as pltpu

D = 1024
NG = 5 * D
NQ = 1024
NKV = 256
NP = NQ + 2 * NKV + NG
F = 4096
HD = 64
GQ = 4
WIN = 128
L = 2
NDEV = 8
EPS = 1e-6
NEG = -1e30
SMALL_ROWS = 24
LR, B1, B2, AEPS, WD, STEP = 0.001, 0.9, 0.999, 1e-08, 0.01, 10

BF = jnp.bfloat16
F32 = jnp.float32
MESH = pl.DeviceIdType.MESH
VMEM_LIMIT = 60 * 1024 * 1024
ANY = pl.BlockSpec(memory_space=pl.ANY)

NN = ((1,), (0,))
NT = ((1,), (1,))
TN = ((0,), (0,))


def _dot(a, b, dims):
    return lax.dot_general(a, b, (dims, ((), ())), preferred_element_type=F32)


def _resident(shape, imap):
    return pl.BlockSpec(shape, imap, pipeline_mode=pl.Buffered(1))


def _position():
    return lax.axis_index("x"), lax.axis_index("y"), lax.axis_index("c")


class _Gather:
    def __init__(self, shards):
        n = len(shards)
        self.inputs = list(shards)
        self.out_shape = [jax.ShapeDtypeStruct((s.shape[0], NDEV) + s.shape[1:], s.dtype) for s in shards]
        self.scratch = [pltpu.SemaphoreType.DMA((n, 7)), pltpu.SemaphoreType.DMA((n, 7)), pltpu.SemaphoreType.DMA((n,))]

    def _plan(self, src, dst, sems):
        send_sems, recv_sems, local_sems = sems
        n = len(src)
        x, y, c = _position()
        me, sibling = (x, y, c), (x, y, 1 - c)
        chips = [(1 - x, y), (x, 1 - y), (1 - x, 1 - y)]

        def rows(a, p):
            return dst[a].at[:, 4 * p[0] + 2 * p[1] + p[2]]

        def copy(a, k, block, to, from_src=False):
            return pltpu.make_async_remote_copy(
                src_ref=src[a] if from_src else rows(a, block), dst_ref=rows(a, block),
                send_sem=send_sems.at[a, k], recv_sem=recv_sems.at[a, k], device_id=to, device_id_type=MESH)

        mine = [pltpu.make_async_copy(src[a], rows(a, me), local_sems.at[a]) for a in range(n)]
        first = []
        for a in range(n):
            first.append(copy(a, 0, me, sibling, True))
            first += [copy(a, 1 + t, me, (*chip, c), True) for t, chip in enumerate(chips)]
        return n, c, me, sibling, chips, copy, mine, first

    def start(self, src, dst, sems):
        *_, mine, first = self._plan(src, dst, sems)
        for cp in mine + first:
            cp.start()

    def relay(self, src, dst, sems):
        n, c, me, sibling, chips, copy, _, _ = self._plan(src, dst, sems)
        for t, chip in enumerate(chips):
            for a in range(n):
                copy(a, 1 + t, (*chip, c), me).wait_recv()
                copy(a, 4 + t, (*chip, c), sibling).start()

    def finish(self, src, dst, sems):
        n, c, me, sibling, chips, copy, mine, first = self._plan(src, dst, sems)
        for a in range(n):
            copy(a, 0, sibling, me).wait_recv()
            for t, chip in enumerate(chips):
                copy(a, 4 + t, (*chip, 1 - c), me).wait_recv()
        for cp in first + [copy(a, 4 + t, (*chip, c), sibling) for t, chip in enumerate(chips) for a in range(n)]:
            cp.wait_send()
        for cp in mine:
            cp.wait()


class _Exchange:
    def __init__(self, grads, everyone=()):
        self.inputs = list(grads) + list(everyone)
        self.n_blocked = len(grads)
        n = len(self.inputs)
        self.out_shape = [jax.ShapeDtypeStruct(g.shape, g.dtype) for g in grads]
        self.out_shape += [jax.ShapeDtypeStruct((NDEV,) + e.shape, e.dtype) for e in everyone]
        self.scratch = [pltpu.SemaphoreType.DMA((n, 7)), pltpu.SemaphoreType.DMA((n, 7)), pltpu.SemaphoreType.DMA((n,))]

    def _plan(self, src, land, sems):
        send_sems, recv_sems, local_sems = sems
        x, y, c = _position()
        me = 4 * x + 2 * y + c

        def parts(peer_idx):
            return [(s.at[peer_idx] if a < self.n_blocked else s, land[a].at[me]) for a, s in enumerate(src)]

        local = [pltpu.make_async_copy(s, d, local_sems.at[a]) for a, (s, d) in enumerate(parts(me))]
        sent = []
        for rel in range(1, NDEV):
            px = 1 - x if rel & 4 else x
            py = 1 - y if rel & 2 else y
            pc = 1 - c if rel & 1 else c
            for a, (s, d) in enumerate(parts(4 * px + 2 * py + pc)):
                sent.append(pltpu.make_async_remote_copy(
                    src_ref=s, dst_ref=d, send_sem=send_sems.at[a, rel - 1], recv_sem=recv_sems.at[a, rel - 1],
                    device_id=(px, py, pc), device_id_type=MESH))
        return local, sent

    def start(self, src, land, sems):
        local, sent = self._plan(src, land, sems)
        for cp in local + sent:
            cp.start()

    def relay(self, src, land, sems):
        pass

    def finish(self, src, land, sems):
        local, sent = self._plan(src, land, sems)
        for cp in sent:
            cp.wait_recv()
        for cp in sent:
            cp.wait_send()
        for cp in local:
            cp.wait()


def _call(body, *, name, grid, in_specs, out_specs, out_shape, args, scratch_shapes=(), aliases=None, side=None,
          relay_at=1.0):
    sem = ("arbitrary",) * len(grid)
    if side is None:
        outs = pl.pallas_call(
            body, name=name, grid=grid, in_specs=in_specs, out_specs=out_specs, out_shape=out_shape,
            scratch_shapes=list(scratch_shapes), input_output_aliases=aliases or {},
            compiler_params=pltpu.CompilerParams(dimension_semantics=sem, vmem_limit_bytes=VMEM_LIMIT))(*args)
        return outs, []
    ni, no, ns = len(in_specs), len(out_specs), len(scratch_shapes)
    si, so = len(side.inputs), len(side.out_shape)

    def hosted(*refs):
        ins, refs = refs[:ni], refs[ni:]
        sins, refs = refs[:si], refs[si:]
        outs, refs = refs[:no], refs[no:]
        souts, refs = refs[:so], refs[so:]
        scr, sscr = refs[:ns], refs[ns:]
        step = pl.program_id(0)
        for d in range(1, len(grid)):
            step = step * grid[d] + pl.program_id(d)
        last = int(np.prod(grid)) - 1

        @pl.when(step == 0)
        def _():
            side.start(sins, souts, sscr)

        body(*ins, *outs, *scr)

        @pl.when(step == min(int(relay_at * last), last))
        def _():
            side.relay(sins, souts, sscr)

        @pl.when(step == last)
        def _():
            side.finish(sins, souts, sscr)

    outs = pl.pallas_call(
        hosted, name=name, grid=grid, in_specs=list(in_specs) + [ANY] * si, out_specs=list(out_specs) + [ANY] * so,
        out_shape=list(out_shape) + side.out_shape, scratch_shapes=list(scratch_shapes) + side.scratch,
        input_output_aliases=aliases or {},
        compiler_params=pltpu.CompilerParams(dimension_semantics=sem, vmem_limit_bytes=VMEM_LIMIT, has_side_effects=True),
    )(*args, *side.inputs)
    return outs[:no], outs[no:]


def _remote_only(side, name):
    n = len(side.inputs)

    def body(*refs):
        src, dst, sems = refs[:n], refs[n:n + len(side.out_shape)], refs[n + len(side.out_shape):]
        side.start(src, dst, sems)
        side.relay(src, dst, sems)
        side.finish(src, dst, sems)

    return pl.pallas_call(
        body, name=name, in_specs=[ANY] * n, out_specs=[ANY] * len(side.out_shape), out_shape=side.out_shape,
        scratch_shapes=side.scratch, compiler_params=pltpu.CompilerParams(has_side_effects=True))(*side.inputs)


def _rms(x, g):
    r = lax.rsqrt(jnp.mean(x * x, axis=-1, keepdims=True) + EPS)
    return x * r * g


def _rms_bwd(dy, x, g):
    r = lax.rsqrt(jnp.mean(x * x, axis=-1, keepdims=True) + EPS)
    xh = x * r
    dxh = dy * g
    dx = r * (dxh - xh * jnp.mean(dxh * xh, axis=-1, keepdims=True))
    return dx, jnp.sum(dy * xh, axis=0, keepdims=True)


def _halves(n):
    return [slice(0, n // 2), slice(n // 2, n)] if n % 256 == 0 else [slice(0, n)]


def _zero_at_first_step(acc_ref):
    first = pl.program_id(0) == 0

    @pl.when(first)
    def _():
        acc_ref[...] = jnp.zeros_like(acc_ref)


ROW1 = lambda: _resident((1, D), lambda i: (0, 0))
LROWS = lambda l, n=1: _resident((None, n, D), lambda *_: (l, 0, 0))
ACC = lambda: pl.BlockSpec((8, D), lambda i: (0, 0))


def _inproj_fwd(x, g, wint, l, tm, side=None, relay_at=1.0):
    T = x.shape[0]

    def body(x_ref, g_ref, w_ref, gt_ref, q_ref, k_ref, v_ref, h_ref):
        for rows in _halves(tm):
            h = _rms(x_ref[rows, :], g_ref[...]).astype(BF)
            h_ref[rows, :] = h
            q_ref[:, rows] = _dot(w_ref[0:NQ, :], h, NT).astype(BF)
            k_ref[:, rows] = _dot(w_ref[NQ:NQ + NKV, :], h, NT).astype(BF)
            v_ref[:, rows] = _dot(w_ref[NQ + NKV:NQ + 2 * NKV, :], h, NT).astype(BF)
            for s in range(5):
                lo = NQ + 2 * NKV + s * D
                gt_ref[rows, s * D:(s + 1) * D] = _dot(h, w_ref[lo:lo + D, :], NT).astype(BF)

    tok = lambda w: pl.BlockSpec((tm, w), lambda i: (i, 0))
    feat = lambda w: pl.BlockSpec((w, tm), lambda i: (0, i))
    return _call(
        body, name=f"inproj_fwd{l}", grid=(T // tm,),
        in_specs=[tok(D), LROWS(l), _resident((NP, D), lambda i: (0, 0))],
        out_specs=[tok(NG), feat(NQ), feat(NKV), feat(NKV), tok(D)],
        out_shape=[jax.ShapeDtypeStruct((T, NG), BF)] + [jax.ShapeDtypeStruct((w, T), BF) for w in (NQ, NKV, NKV)]
        + [jax.ShapeDtypeStruct((T, D), BF)],
        args=(x, g, wint), side=side, relay_at=relay_at)


def _band_geometry():
    j = lax.broadcasted_iota(jnp.int32, (2 * WIN, WIN), 0)
    r = lax.broadcasted_iota(jnp.int32, (2 * WIN, WIN), 1)
    dist = WIN + r - j
    dist0 = (r - j)[:WIN]
    return (dist.astype(F32), (dist >= 0) & (dist < WIN)), (dist0.astype(F32), dist0 >= 0)


def _pair_biases(sm_ref, pj):
    return [[jnp.where(ok, -sm_ref[1, pj * 2 * GQ + h] * dist, NEG) for h in range(2 * GQ)]
            for dist, ok in _band_geometry()]


def _reduce_rows(x, pair, whole):
    while x.shape[0] > 8:
        half = x.shape[0] // 2
        x = pair(x[:half], x[half:])
    return whole(x, axis=0, keepdims=True)


def _heads_on_lanes(ref, kvh, r0):
    return jnp.concatenate([ref[(kvh * GQ + g) * HD:(kvh * GQ + g + 1) * HD, pl.ds(r0, WIN)] for g in range(GQ)], axis=1)


def _band_probs(sm_ref, head0, q_ref, k_ref, r0, p0, kvh, biases):
    qt = _heads_on_lanes(q_ref, kvh, r0) * jnp.asarray(HD ** -0.5, BF)
    kt = k_ref[kvh * HD:(kvh + 1) * HD, pl.ds(p0, biases[0].shape[0])]
    heads = []
    for g in range(GQ):
        sink = sm_ref[0, head0 + g]
        s = _dot(kt, qt[:, g * WIN:(g + 1) * WIN], TN) + biases[kvh * GQ + g]
        m = jnp.maximum(_reduce_rows(s, jnp.maximum, jnp.max), sink)
        p = jnp.exp(s - m)
        ps = jnp.exp(sink - m)
        heads.append((p, ps, 1.0 / (_reduce_rows(p, jnp.add, jnp.sum) + ps)))
    return qt, kt, heads


def _attn_fwd(q, k, v, sm, l, S, side=None, relay_at=1.0):
    T = q.shape[1]
    nblk = S // WIN
    unroll = next(u for u in (15, 5, 3, 1) if (nblk - 1) % u == 0)

    def body(sm_all_ref, q_ref, k_ref, v_ref, o_ref):
        sm_ref = sm_all_ref.at[l]
        pj = pl.program_id(1)
        biases = _pair_biases(sm_ref, pj)

        def block(i, first):
            r0 = 0 if first else pl.multiple_of(i * WIN, WIN)
            p0 = 0 if first else pl.multiple_of(i * WIN - WIN, WIN)
            for kvh in range(2):
                head0 = (2 * pj + kvh) * GQ
                _, _, heads = _band_probs(sm_ref, head0, q_ref, k_ref, r0, p0, kvh, biases[first])
                vt = v_ref[kvh * HD:(kvh + 1) * HD, pl.ds(p0, WIN if first else 2 * WIN)]
                for g, (p, _, inv) in enumerate(heads):
                    rows = slice((kvh * GQ + g) * HD, (kvh * GQ + g + 1) * HD)
                    o_ref[rows, pl.ds(r0, WIN)] = (_dot(vt, p.astype(BF), NN) * inv).astype(BF)

        block(0, True)

        def rest(t, c):
            for u in range(unroll):
                block(1 + unroll * t + u, False)
            return c

        lax.fori_loop(0, (nblk - 1) // unroll, rest, 0)

    wide = lambda: pl.BlockSpec((2 * GQ * HD, S), lambda s, p: (p, s))
    narrow = lambda: pl.BlockSpec((2 * HD, S), lambda s, p: (p, s))
    return _call(
        body, name=f"attn_fwd{l}", grid=(T // S, 2),
        in_specs=[pl.BlockSpec(memory_space=pltpu.SMEM), wide(), narrow(), narrow()],
        out_specs=[wide()], out_shape=[jax.ShapeDtypeStruct((D, T), BF)],
        args=(sm, q, k, v), side=side, relay_at=relay_at)


def _shift_rows(y, k, edge_rows, down):
    n = y.shape[0]
    rid = lax.broadcasted_iota(jnp.int32, y.shape, 0)
    out = pltpu.roll(y, k if down else n - k, 0)
    for t, row in enumerate(edge_rows):
        out = jnp.where(rid == (t if down else n - k + t), row, out)
    return out


def _mixer_fwd(x, gt, att, w3, bg, cw, cbias, l, S, tm):
    T = x.shape[0]

    def body(x_ref, cb_ref, cc_ref, cu_ref, ga_ref, gc_ref, cch_ref, cuh_ref, att_ref, wao_ref, wco_ref, wo_ref,
             bg_ref, cw_ref, cbias_ref, x1_ref, mg_ref, co_ref, ya_ref, yc_ref, zb_ref):
        first = (pl.program_id(0) * tm) % S == 0
        y = cc_ref[...].astype(F32) * cu_ref[...].astype(F32)
        hy1 = cch_ref[15:16, :].astype(F32) * cuh_ref[15:16, :].astype(F32)
        hy2 = cch_ref[14:15, :].astype(F32) * cuh_ref[14:15, :].astype(F32)
        hy1, hy2 = jnp.where(first, 0.0, hy1), jnp.where(first, 0.0, hy2)
        z = (cw_ref[0:1, :] * _shift_rows(y, 2, [hy2, hy1], True) + cw_ref[1:2, :] * _shift_rows(y, 1, [hy1], True)
             + cw_ref[2:3, :] * y)
        zb = z + cbias_ref[...]
        zb_ref[...] = zb.astype(BF)
        co = (cb_ref[...].astype(F32) * zb).astype(BF)
        co_ref[...] = co
        yc = _dot(co, wco_ref[...], NN)
        ya = _dot(att_ref[...], wao_ref[...], TN)
        ya_ref[...] = ya.astype(BF)
        yc_ref[...] = yc.astype(BF)
        sa = jax.nn.sigmoid(ga_ref[...].astype(F32) + bg_ref[0:1, :])
        sc = jax.nn.sigmoid(gc_ref[...].astype(F32) + bg_ref[1:2, :])
        mg = (sa * ya + sc * yc).astype(BF)
        mg_ref[...] = mg
        x1_ref[...] = x_ref[...] + _dot(mg, wo_ref[...], NN)

    tok = lambda: pl.BlockSpec((tm, D), lambda i: (i, 0))
    seg = lambda s: pl.BlockSpec((tm, D), lambda i: (i, s))
    halo = lambda s: pl.BlockSpec((16, D), lambda i: (jnp.maximum(i * (tm // 16) - 1, 0), s))
    wsp = lambda k: _resident((None, D, D), lambda i: (k, 0, 0))
    return _call(
        body, name=f"mixer_fwd{l}", grid=(T // tm,),
        in_specs=[tok(), seg(0), seg(1), seg(2), seg(3), seg(4), halo(1), halo(2),
                  pl.BlockSpec((D, tm), lambda i: (0, i)), wsp(0), wsp(1), wsp(2), LROWS(l, 2), LROWS(l, 8), LROWS(l)],
        out_specs=[tok()] * 6,
        out_shape=[jax.ShapeDtypeStruct((T, D), dt) for dt in (F32, BF, BF, BF, BF, BF)],
        args=(x, gt, gt, gt, gt, gt, gt, gt, att, w3, w3, w3, bg, cw, cbias))


def _mlp_fwd(x1, g, w2, l, tm, head=None, side=None, relay_at=1.0):
    T = x1.shape[0]
    nt = T // tm
    FC = 1024

    def body(x_ref, g_ref, wup_ref, wdn_ref, *rest):
        out_ref, h_ref, a_ref = rest[-4:-1] if head else rest
        if head:
            t_ref, gf_ref, acc_ref = rest[0], rest[1], rest[-1]
            _zero_at_first_step(acc_ref)
        for rows in _halves(tm):
            x = x_ref[rows, :]
            h = _rms(x, g_ref[...]).astype(BF)
            h_ref[rows, :] = h
            acc = x
            for c in range(F // FC):
                a = _dot(h, wup_ref[c * FC:(c + 1) * FC, :], NT)
                a_ref[rows, c * FC:(c + 1) * FC] = a.astype(BF)
                u = jnp.maximum(a, 0.0)
                acc = acc + _dot((u * u).astype(BF), wdn_ref[c * FC:(c + 1) * FC, :], NN)
            if not head:
                out_ref[rows, :] = acc
                continue
            gf = gf_ref[...]
            err = _rms(acc, gf) - t_ref[rows, :]
            out_ref[rows, :], dg = _rms_bwd(err * (1.0 / D), acc, gf)
            acc_ref[0:1, :] += dg
            acc_ref[1:2, :] += jnp.sum(err * err, axis=0, keepdims=True)
        if not head:
            return

        @pl.when(pl.program_id(0) == nt - 1)
        def _():
            acc_ref[1:2, :] = jnp.zeros((1, D), F32) + (0.5 / D) * jnp.sum(acc_ref[1:2, :])

    tok = lambda w: pl.BlockSpec((tm, w), lambda i: (i, 0))
    wsp = lambda k: _resident((None, F, D), lambda i: (k, 0, 0))
    return _call(
        body, name=f"mlp_fwd{l}", grid=(nt,),
        in_specs=[tok(D), LROWS(l), wsp(0), wsp(1)] + ([tok(D), ROW1()] if head else []),
        out_specs=[tok(D), tok(D), tok(F)] + ([ACC()] if head else []),
        out_shape=[jax.ShapeDtypeStruct((T, D), F32), jax.ShapeDtypeStruct((T, D), BF), jax.ShapeDtypeStruct((T, F), BF)]
        + ([jax.ShapeDtypeStruct((8, D), F32)] if head else []),
        args=(x1, g, w2, w2) + (tuple(head) if head else ()), side=side, relay_at=relay_at)


def _mlp_bwd(dx2, x1, a, g, w2, l, tm, side=None):
    T = dx2.shape[0]
    FC = 1024

    def body(d_ref, x_ref, a_ref, g_ref, wup_ref, wdn_ref, da_ref, dx1_ref, db_ref, acc_ref):
        _zero_at_first_step(acc_ref)
        for rows in _halves(tm):
            d = d_ref[rows, :]
            db = d.astype(BF)
            db_ref[rows, :] = db
            dh = jnp.zeros(d.shape, F32)
            for c in range(F // FC):
                du = _dot(db, wdn_ref[c * FC:(c + 1) * FC, :], NT)
                da = (du * (2.0 * jnp.maximum(a_ref[rows, c * FC:(c + 1) * FC].astype(F32), 0.0))).astype(BF)
                da_ref[rows, c * FC:(c + 1) * FC] = da
                dh = dh + _dot(da, wup_ref[c * FC:(c + 1) * FC, :], NN)
            dx, dg = _rms_bwd(dh, x_ref[rows, :], g_ref[...])
            dx1_ref[rows, :] = d + dx
            acc_ref[0:1, :] += dg

    tok = lambda w: pl.BlockSpec((tm, w), lambda i: (i, 0))
    wsp = lambda k: _resident((None, F, D), lambda i: (k, 0, 0))
    return _call(
        body, name=f"mlp_bwd{l}", grid=(T // tm,),
        in_specs=[tok(D), tok(D), tok(F), LROWS(l), wsp(0), wsp(1)],
        out_specs=[tok(F), tok(D), tok(D), ACC()],
        out_shape=[jax.ShapeDtypeStruct((T, F), BF), jax.ShapeDtypeStruct((T, D), F32),
                   jax.ShapeDtypeStruct((T, D), BF), jax.ShapeDtypeStruct((8, D), F32)],
        args=(dx2, x1, a, g, w2, w2), side=side)


def _mixer_bwd(dx1, gt, ya, yc, zb, w3, bg, cw, l, S, tm, side=None):
    T = dx1.shape[0]
    nt = T // tm

    def body(d_ref, cb_ref, cc_ref, cu_ref, ga_ref, gc_ref, ya_ref, yc_ref, zb_ref, wao_ref, wco_ref, wo_ref,
             bg_ref, cw_ref, dg_ref, datt_ref, dya_ref, dyc_ref, db_ref, acc_ref, carry_ref):
        ti = nt - 1 - pl.program_id(0)
        _zero_at_first_step(acc_ref)

        @pl.when(((ti + 1) * tm) % S == 0)
        def _():
            carry_ref[...] = jnp.zeros_like(carry_ref)

        db = d_ref[...].astype(BF)
        db_ref[...] = db
        dm = _dot(db, wo_ref[...], NT)
        sa = jax.nn.sigmoid(ga_ref[...].astype(F32) + bg_ref[0:1, :])
        sc = jax.nn.sigmoid(gc_ref[...].astype(F32) + bg_ref[1:2, :])
        dya32 = dm * sa
        dyc32 = dm * sc
        dya = dya32.astype(BF)
        dyc = dyc32.astype(BF)
        dya_ref[...] = dya
        dyc_ref[...] = dyc
        dga = dya32 * ya_ref[...].astype(F32) * (1.0 - sa)
        dgc = dyc32 * yc_ref[...].astype(F32) * (1.0 - sc)
        dg_ref[:, 3 * D:4 * D] = dga.astype(BF)
        dg_ref[:, 4 * D:5 * D] = dgc.astype(BF)
        acc_ref[0:1, :] += jnp.sum(dga, axis=0, keepdims=True)
        acc_ref[1:2, :] += jnp.sum(dgc, axis=0, keepdims=True)
        datt_ref[...] = _dot(wao_ref[...], dya, NT).astype(BF)
        dco = _dot(dyc, wco_ref[...], NT)

        cc = cc_ref[...].astype(F32)
        cu = cu_ref[...].astype(F32)
        y = cc * cu
        dg_ref[:, 0:D] = (dco * zb_ref[...].astype(F32)).astype(BF)
        dz = dco * cb_ref[...].astype(F32)
        u1 = _shift_rows(dz, 1, [carry_ref[0:1, :]], False)
        u2 = _shift_rows(dz, 2, [carry_ref[0:1, :], carry_ref[1:2, :]], False)
        acc_ref[2:3, :] += jnp.sum(dz, axis=0, keepdims=True)
        acc_ref[3:4, :] += jnp.sum(u2 * y, axis=0, keepdims=True)
        acc_ref[4:5, :] += jnp.sum(u1 * y, axis=0, keepdims=True)
        acc_ref[5:6, :] += jnp.sum(dz * y, axis=0, keepdims=True)
        dy = cw_ref[2:3, :] * dz + cw_ref[1:2, :] * u1 + cw_ref[0:1, :] * u2
        dg_ref[:, D:2 * D] = (dy * cu).astype(BF)
        dg_ref[:, 2 * D:3 * D] = (dy * cc).astype(BF)
        carry_ref[...] = dz[0:8, :]

    tok = lambda w=D: pl.BlockSpec((tm, w), lambda i: (nt - 1 - i, 0))
    seg = lambda s: pl.BlockSpec((tm, D), lambda i: (nt - 1 - i, s))
    wsp = lambda k: _resident((None, D, D), lambda i: (k, 0, 0))
    return _call(
        body, name=f"mixer_bwd{l}", grid=(nt,),
        in_specs=[tok(), seg(0), seg(1), seg(2), seg(3), seg(4), tok(), tok(), tok(), wsp(0), wsp(1), wsp(2),
                  LROWS(l, 2), LROWS(l, 8)],
        out_specs=[tok(NG), pl.BlockSpec((D, tm), lambda i: (0, nt - 1 - i)), tok(), tok(), tok(), ACC()],
        out_shape=[jax.ShapeDtypeStruct((T, NG), BF), jax.ShapeDtypeStruct((D, T), BF)]
        + [jax.ShapeDtypeStruct((T, D), BF)] * 3 + [jax.ShapeDtypeStruct((8, D), F32)],
        scratch_shapes=[pltpu.VMEM((8, D), F32)],
        args=(dx1, gt, gt, gt, gt, gt, ya, yc, zb, w3, w3, w3, bg, cw), side=side)


def _attn_bwd(q, k, v, att, datt, sm, l, S, side=None):
    T = q.shape[1]
    nblk = S // WIN
    unroll = next(u for u in (15, 5, 3, 1) if (nblk - 1) % u == 0)
    scale = HD ** -0.5

    def body(sm_all_ref, q_ref, k_ref, v_ref, o_ref, do_ref, dq_ref, dk_ref, dv_ref, ds_ref, dka_ref, dva_ref):
        sm_ref = sm_all_ref.at[l]
        pj = pl.program_id(1)
        biases = _pair_biases(sm_ref, pj)
        dka_ref[...] = jnp.zeros_like(dka_ref)
        dva_ref[...] = jnp.zeros_like(dva_ref)

        def block(i, first, dsinks):
            r0 = 0 if first else pl.multiple_of(i * WIN, WIN)
            p0 = 0 if first else pl.multiple_of(i * WIN - WIN, WIN)
            out = []
            for kvh in range(2):
                head0 = (2 * pj + kvh) * GQ
                rows = slice(kvh * HD, (kvh + 1) * HD)
                dob = _heads_on_lanes(do_ref, kvh, r0)
                band = pl.ds(p0, WIN if first else 2 * WIN)
                vt = v_ref[rows, band]
                dpts = [_dot(vt, dob[:, g * WIN:(g + 1) * WIN], TN) for g in range(GQ)]
                qt, kt, heads = _band_probs(sm_ref, head0, q_ref, k_ref, r0, p0, kvh, biases[first])
                inv = jnp.concatenate([h[2] for h in heads], axis=1)
                do32 = dob.astype(F32)
                delta = jnp.sum(do32 * _heads_on_lanes(o_ref, kvh, r0).astype(F32), axis=0, keepdims=True) * inv
                dosb = (do32 * inv).astype(BF)
                dsts = [(p * (dpts[g] * h_inv - delta[:, g * WIN:(g + 1) * WIN])).astype(BF)
                        for g, (p, _, h_inv) in enumerate(heads)]
                for g in range(GQ):
                    hr = slice((kvh * GQ + g) * HD, (kvh * GQ + g + 1) * HD)
                    dq_ref[hr, pl.ds(r0, WIN)] = (_dot(kt, dsts[g], NN) * scale).astype(BF)
                lanes = lambda g: slice(g * WIN, (g + 1) * WIN)
                dka_ref[rows, band] += sum(_dot(qt[:, lanes(g)], dsts[g], NT) for g in range(GQ))
                dva_ref[rows, band] += sum(
                    _dot(dosb[:, lanes(g)], p.astype(BF), NT) for g, (p, _, _) in enumerate(heads))
                ps = jnp.concatenate([h[1] for h in heads], axis=1)
                out.append(dsinks[kvh] - ps * delta)
            return tuple(out)

        zero = jnp.zeros((1, GQ * WIN), F32)
        def rest(t, c):
            for u in range(unroll):
                c = block(1 + unroll * t + u, False, c)
            return c

        dsinks = lax.fori_loop(0, (nblk - 1) // unroll, rest, block(0, True, (zero, zero)))
        for kvh in range(2):
            for g in range(GQ):
                tot = jnp.sum(dsinks[kvh][:, g * WIN:(g + 1) * WIN])
                ds_ref[kvh * GQ + g:kvh * GQ + g + 1, :] = jnp.zeros((1, 128), F32) + tot
        dk_ref[...] = dka_ref[...].astype(BF)
        dv_ref[...] = dva_ref[...].astype(BF)

    wide = lambda: pl.BlockSpec((2 * GQ * HD, S), lambda s, p: (p, s))
    narrow = lambda: pl.BlockSpec((2 * HD, S), lambda s, p: (p, s))
    return _call(
        body, name=f"attn_bwd{l}", grid=(T // S, 2),
        in_specs=[pl.BlockSpec(memory_space=pltpu.SMEM), wide(), narrow(), narrow(), wide(), wide()],
        out_specs=[wide(), narrow(), narrow(), pl.BlockSpec((None, None, 8, 128), lambda s, p: (s, p, 0, 0))],
        out_shape=[jax.ShapeDtypeStruct((NQ, T), BF), jax.ShapeDtypeStruct((NKV, T), BF),
                   jax.ShapeDtypeStruct((NKV, T), BF), jax.ShapeDtypeStruct((T // S, 2, 8, 128), F32)],
        scratch_shapes=[pltpu.VMEM((2 * HD, S), F32), pltpu.VMEM((2 * HD, S), F32)],
        args=(sm, q, k, v, att, datt), side=side)


def _inproj_bwd(dgt, dq, dk, dv, x, dres, g, wint, l, tm, side=None):
    T = x.shape[0]

    def body(dg_ref, dq_ref, dk_ref, dv_ref, x_ref, dr_ref, g_ref, w_ref, dx_ref, acc_ref):
        _zero_at_first_step(acc_ref)
        for rows in _halves(tm):
            dh = _dot(dq_ref[:, rows], w_ref[0:NQ, :], TN)
            dh = dh + _dot(dk_ref[:, rows], w_ref[NQ:NQ + NKV, :], TN)
            dh = dh + _dot(dv_ref[:, rows], w_ref[NQ + NKV:NQ + 2 * NKV, :], TN)
            dh = dh + _dot(dg_ref[rows, :], w_ref[NQ + 2 * NKV:NP, :], NN)
            dx, dg = _rms_bwd(dh, x_ref[rows, :], g_ref[...])
            dx_ref[rows, :] = dr_ref[rows, :] + dx
            acc_ref[0:1, :] += dg

    tok = lambda w: pl.BlockSpec((tm, w), lambda i: (i, 0))
    feat = lambda w: pl.BlockSpec((w, tm), lambda i: (0, i))
    return _call(
        body, name=f"inproj_bwd{l}", grid=(T // tm,),
        in_specs=[tok(NG), feat(NQ), feat(NKV), feat(NKV), tok(D), tok(D), LROWS(l), _resident((NP, D), lambda i: (0, 0))],
        out_specs=[tok(D), ACC()],
        out_shape=[jax.ShapeDtypeStruct((T, D), F32), jax.ShapeDtypeStruct((8, D), F32)],
        args=(dgt, dq, dk, dv, x, dres, g, wint), side=side)


def _wgrad(a, b, rows, row0, into, name, relu2=False):
    T, M = a.shape
    tmm = next(t for t in (1024, 512, 256) if M % t == 0 and row0 % t == 0)
    tk = min(4096 if M > 4096 else 2048, T)
    nk = T // tk
    blk0 = row0 // tmm

    def body(*refs):
        a_ref, b_ref = refs[0], refs[1]
        o_ref, acc_ref = refs[-2], refs[-1]
        kk = pl.program_id(1)

        @pl.when(kk == 0)
        def _():
            acc_ref[...] = jnp.zeros_like(acc_ref)

        av = a_ref[...]
        if relu2:
            t = jnp.maximum(av.astype(F32), 0.0)
            av = (t * t).astype(BF)
        acc_ref[...] += _dot(av, b_ref[...], TN)

        @pl.when(kk == nk - 1)
        def _():
            o_ref[...] = acc_ref[...].astype(BF)

    in_specs = [pl.BlockSpec((tk, tmm), lambda j, kk: (kk, j)), pl.BlockSpec((tk, D), lambda j, kk: (kk, 0))]
    args = [a, b]
    if into is not None:
        in_specs.append(ANY)
        args.append(into)
    (out,), _ = _call(
        body, name=name, grid=(M // tmm, nk), in_specs=in_specs,
        out_specs=[pl.BlockSpec((tmm, D), lambda j, kk: (blk0 + j, 0))],
        out_shape=[jax.ShapeDtypeStruct((rows, D), BF)], scratch_shapes=[pltpu.VMEM((tmm, D), F32)],
        aliases={2: 0} if into is not None else None, args=args)
    return out


def _wgrad_qkv(dq, dk, dv, h, into, name):
    T = h.shape[0]
    rows = NQ + 2 * NKV
    tk = min(2048, T)
    nk = T // tk

    def body(dq_ref, dk_ref, dv_ref, h_ref, _, o_ref, acc_ref):
        kk = pl.program_id(0)

        @pl.when(kk == 0)
        def _():
            acc_ref[...] = jnp.zeros_like(acc_ref)

        hv = h_ref[...]
        acc_ref[0:NQ, :] += _dot(dq_ref[...], hv, NN)
        acc_ref[NQ:NQ + NKV, :] += _dot(dk_ref[...], hv, NN)
        acc_ref[NQ + NKV:rows, :] += _dot(dv_ref[...], hv, NN)

        @pl.when(kk == nk - 1)
        def _():
            o_ref[...] = acc_ref[...].astype(BF)

    feat = lambda w: pl.BlockSpec((w, tk), lambda kk: (0, kk))
    (out,), _ = _call(
        body, name=name, grid=(nk,),
        in_specs=[feat(NQ), feat(NKV), feat(NKV), pl.BlockSpec((tk, D), lambda kk: (kk, 0)), ANY],
        out_specs=[pl.BlockSpec((rows, D), lambda kk: (0, 0))],
        out_shape=[jax.ShapeDtypeStruct((NP, D), BF)], scratch_shapes=[pltpu.VMEM((rows, D), F32)],
        aliases={4: 0}, args=[dq, dk, dv, h, into])
    return out


def _wgrad_square(jobs, name):
    n = len(jobs)
    T = jobs[0][1].shape[0]
    tk = min(1024, T)
    nk = T // tk

    def body(*refs):
        ins, outs, acc_ref = refs[:2 * n], refs[2 * n:3 * n], refs[3 * n]
        j, kk = pl.program_id(0), pl.program_id(1)

        @pl.when(kk == 0)
        def _():
            acc_ref[...] = jnp.zeros_like(acc_ref)

        for p, (_, _, a_t) in enumerate(jobs):
            @pl.when(j == p)
            def _(p=p, a_t=a_t):
                acc_ref[...] += _dot(ins[2 * p][...], ins[2 * p + 1][...], NN if a_t else TN)

                @pl.when(kk == nk - 1)
                def _():
                    outs[p][...] = acc_ref[...].astype(BF)

    def step(p):
        return lambda j, kk: jnp.where(j == p, kk, jnp.where(j > p, nk - 1, 0))

    in_specs, args = [], []
    for p, (a, b, a_t) in enumerate(jobs):
        at = step(p)
        in_specs.append(pl.BlockSpec((D, tk), lambda j, kk, at=at: (0, at(j, kk))) if a_t
                        else pl.BlockSpec((tk, D), lambda j, kk, at=at: (at(j, kk), 0)))
        in_specs.append(pl.BlockSpec((tk, D), lambda j, kk, at=at: (at(j, kk), 0)))
        args += [a, b]
    outs, _ = _call(
        body, name=name, grid=(n, nk), in_specs=in_specs,
        out_specs=[pl.BlockSpec((D, D), lambda j, kk: (0, 0))] * n,
        out_shape=[jax.ShapeDtypeStruct((D, D), BF)] * n, scratch_shapes=[pltpu.VMEM((D, D), F32)], args=args)
    return outs


def _adamw(w, g, m, v):
    m = B1 * m + (1.0 - B1) * g
    v = B2 * v + (1.0 - B2) * (g * g)
    m_hat = m / (1.0 - B1 ** STEP)
    v_hat = v / (1.0 - B2 ** STEP)
    return -LR * (m_hat / (jnp.sqrt(v_hat) + AEPS) + WD * w), m, v


def _adam_sum(land, w, m, v, l, into, name, transposed=False):
    _, r, _ = land.shape
    tr = 208 if r % 208 == 0 else (256 if r % 256 == 0 else r)
    tc = 256

    def body(land_ref, w_ref, m_ref, v_ref, *rest):
        g_ref, d_ref, nm_ref, nv_ref = rest[-4:]
        g = land_ref[0].astype(F32)
        for s in range(1, NDEV):
            g = g + land_ref[s].astype(F32)
        if transposed:
            g = g.T
        g_ref[...] = g
        d_ref[...], nm_ref[...], nv_ref[...] = _adamw(w_ref[...], g, m_ref[...], v_ref[...])

    if transposed:
        blk = lambda: pl.BlockSpec((None, tc, r), lambda j: (l, j, 0))
        land_spec, grid = pl.BlockSpec((NDEV, r, tc), lambda j: (0, 0, j)), (D // tc,)
    else:
        blk = lambda: pl.BlockSpec((None, tr, D), lambda j: (l, j, 0))
        land_spec, grid = pl.BlockSpec((NDEV, tr, D), lambda j: (0, j, 0)), (r // tr,)
    in_specs = [land_spec, blk(), blk(), blk()]
    args = [land, w, m, v]
    aliases = None
    if into is not None:
        in_specs += [ANY] * 4
        args += list(into)
        aliases = {4 + t: t for t in range(4)}
    outs, _ = _call(body, name=name, grid=grid, in_specs=in_specs, out_specs=[blk()] * 4,
                    out_shape=[jax.ShapeDtypeStruct(w.shape, F32)] * 4, aliases=aliases, args=args)
    return outs


SMALL_NAMES = ("g_mix", "b_gates", "sinks", "conv_b", "g_mlp", "g_final", "conv_w")


def _adam_small(land, me, masters):
    n = len(SMALL_NAMES)
    lanes = D // NDEV

    def body(land_ref, me_ref, *refs):
        ins, outs, loss_ref, gs_ref = refs[:3 * n], refs[3 * n:7 * n], refs[7 * n], refs[7 * n + 1]
        g = land_ref[0]
        for s in range(1, NDEV):
            g = g + land_ref[s]
        gs_ref[...] = g

        def update(k, g_piece, idx):
            w_ref, m_ref, v_ref = ins[3 * k:3 * k + 3]
            outs[4 * k][idx] = g_piece
            outs[4 * k + 1][idx], outs[4 * k + 2][idx], outs[4 * k + 3][idx] = _adamw(w_ref[idx], g_piece, m_ref[idx], v_ref[idx])

        whole = (slice(None), slice(None))
        update(0, gs_ref[0:2, :], whole)
        for l in range(L):
            for h in range(2):
                update(1, gs_ref[2 + 2 * l + h:3 + 2 * l + h, :], (slice(l, l + 1), slice(h * D, (h + 1) * D)))
            update(2, gs_ref[6:7, 16 * l:16 * (l + 1)], (slice(l, l + 1), slice(None)))
        update(3, gs_ref[7:9, :], whole)
        update(4, gs_ref[9:11, :], whole)
        update(5, gs_ref[11:12, :], whole)
        mine = pl.ds(pl.multiple_of(me_ref[0] * lanes, lanes), lanes)
        for l in range(L):
            for k in range(3):
                update(6, gs_ref[12 + 3 * l + k:13 + 3 * l + k, mine], (l, slice(k, k + 1), slice(None)))
        loss_ref[...] = gs_ref[18:19, 0:1]

    flat = [t for name in SMALL_NAMES for t in masters[name]]
    vmem = pl.BlockSpec(memory_space=pltpu.VMEM)
    outs = pl.pallas_call(
        body, name="adam_small",
        in_specs=[vmem, pl.BlockSpec(memory_space=pltpu.SMEM)] + [vmem] * len(flat),
        out_shape=[jax.ShapeDtypeStruct(masters[name][0].shape, F32) for name in SMALL_NAMES for _ in range(4)]
        + [jax.ShapeDtypeStruct((1, 1), F32)],
        scratch_shapes=[pltpu.VMEM((SMALL_ROWS, D), F32)],
        compiler_params=pltpu.CompilerParams(vmem_limit_bytes=VMEM_LIMIT))(land, me, *flat)
    return {name: outs[4 * k:4 * k + 4] for k, name in enumerate(SMALL_NAMES)}, outs[-1]


def _pack_small(g_mix, b_gates, sinks, conv_b, g_mlp, g_final, conv_w_rows, extra):
    sink_row = jnp.zeros((1, D), F32).at[0, :2 * 16].set(sinks.reshape(-1))
    return jnp.concatenate([g_mix, b_gates.reshape(4, D), sink_row, conv_b, g_mlp, g_final.reshape(1, D),
                            conv_w_rows, extra, jnp.zeros((SMALL_ROWS - 19, D), F32)], axis=0)


def kernel(x, g_mix, w_in, b_gates, sinks, w_attn_out, conv_w, conv_b, w_conv_out, w_o, g_mlp, w_up, w_down, g_final, loss_target, m_g_mix, m_w_in, m_b_gates, m_sinks, m_w_attn_out, m_conv_w, m_conv_b, m_w_conv_out, m_w_o, m_g_mlp, m_w_up, m_w_down, m_g_final, v_g_mix, v_w_in, v_b_gates, v_sinks, v_w_attn_out, v_conv_w, v_conv_b, v_w_conv_out, v_w_o, v_g_mlp, v_w_up, v_w_down, v_g_final):
    nseq, S, _ = x.shape
    T = nseq * S
    tm_in = min(512, S)
    tm = min(512, S)
    xi, yi, ci = _position()
    me = 4 * xi + 2 * yi + ci
    tr = lambda t: jnp.swapaxes(t, 1, 2)
    blocks = lambda t: t.reshape(NDEV, t.shape[0] // NDEV, D)

    win_t, wup_t = tr(w_in), tr(w_up)
    sh_win = [win_t[l].astype(BF)[None] for l in range(L)]
    sh_w3 = [jnp.stack([w_attn_out[l], w_conv_out[l], w_o[l]]).astype(BF) for l in range(L)]
    sh_w2 = [jnp.stack([wup_t[l], w_down[l]]).astype(BF) for l in range(L)]
    wint, w3, w2 = [None] * L, [None] * L, [None] * L
    wint0_g, cw_g = _remote_only(_Gather([sh_win[0], jnp.pad(conv_w, ((0, 0), (0, 5), (0, 0)))]), "gather_first")
    wint[0] = wint0_g.reshape(NP, D)
    cw = jnp.swapaxes(cw_g, 1, 2).reshape(L, 8, D)
    slopes = np.power(np.float32(2.0), -8.0 * np.arange(1, 17, dtype=np.float32) / 16).astype(np.float32)
    sm = jnp.stack([sinks, jnp.broadcast_to(jnp.asarray(slopes), sinks.shape)], axis=1)
    per_layer = lambda t, n=1: t.reshape(L, n, D)
    g_mix3, g_mlp3, conv_b3, bg = per_layer(g_mix), per_layer(g_mlp), per_layer(conv_b), per_layer(b_gates, 2)

    xf = x.reshape(T, D)
    saved = []
    cur = xf
    for l in range(L):
        (gt, q, k, v, h), got = _inproj_fwd(cur, g_mix3, wint[l], l, tm_in, side=_Gather([sh_w2[l]]),
                                            relay_at=0.85)
        w2[l] = got[0].reshape(2, F, D)
        (att,), got = _attn_fwd(q, k, v, sm, l, S, side=_Gather([sh_w3[l]]), relay_at=0.5)
        w3[l] = got[0].reshape(3, D, D)
        (x1, mg, co, ya, yc, zb), _ = _mixer_fwd(cur, gt, att, w3[l], bg, cw, conv_b3, l, S, tm)
        head = (loss_target.reshape(T, D), g_final.reshape(1, D)) if l == L - 1 else None
        (nxt, h2, a, *acc_loss), got = _mlp_fwd(x1, g_mlp3, w2[l], l, tm_in, head=head,
                                                side=_Gather([sh_win[l + 1]]) if l + 1 < L else None, relay_at=0.7)
        if l + 1 < L:
            wint[l + 1] = got[0].reshape(NP, D)
        saved.append(dict(x=cur, gt=gt, q=q, k=k, v=v, h=h, att=att, x1=x1, mg=mg, co=co, ya=ya, yc=yc, zb=zb, h2=h2, a=a))
        cur = nxt
    dcur, acc_loss = cur, acc_loss[0]

    masters = {"w_in": (win_t, tr(m_w_in), tr(v_w_in)), "w_attn_out": (w_attn_out, m_w_attn_out, v_w_attn_out),
               "w_conv_out": (w_conv_out, m_w_conv_out, v_w_conv_out), "w_o": (w_o, m_w_o, v_w_o),
               "w_up": (w_up, m_w_up, v_w_up), "w_down": (w_down, m_w_down, v_w_down)}
    big = {name: None for name in masters}

    def adam(name, land, l):
        big[name] = _adam_sum(land, *masters[name], l, big[name], f"adam_{name}{l}", transposed=name == "w_up")

    acc_in, acc_mix, acc_mlp, dsink = [None] * L, [None] * L, [None] * L, [None] * L
    d_win_above = None
    for l in reversed(range(L)):
        sv = saved[l]
        side = _Exchange([blocks(d_win_above)]) if l + 1 < L else None
        (da, dx1, dx2b, acc_mlp[l]), got = _mlp_bwd(dcur, sv["x1"], sv["a"], g_mlp3, w2[l], l, tm_in, side=side)
        if l + 1 < L:
            adam("w_in", got[0], l + 1)
        d_wdn = _wgrad(sv["a"], dx2b, F, 0, None, f"wgrad_down{l}", relu2=True)
        d_wup = _wgrad(da, sv["h2"], F, 0, None, f"wgrad_up{l}")
        (dgt, datt, dya, dyc, dx1b, acc_mix[l]), got = _mixer_bwd(
            dx1, sv["gt"], sv["ya"], sv["yc"], sv["zb"], w3[l], bg, cw, l, S, tm,
            side=_Exchange([blocks(d_wdn)]))
        adam("w_down", got[0], l)
        d_wo, d_wao, d_wco = _wgrad_square(
            [(sv["mg"], dx1b, False), (sv["att"], dya, True), (sv["co"], dyc, False)], f"wgrad_mixer{l}")
        d_win = _wgrad(dgt, sv["h"], NP, NQ + 2 * NKV, None, f"wgrad_gates{l}")
        (dq, dk, dv, dsink[l]), got = _attn_bwd(
            sv["q"], sv["k"], sv["v"], sv["att"], datt, sm, l, S,
            side=_Exchange([blocks(d_wup), blocks(d_wo), blocks(d_wao), blocks(d_wco)]))
        for name, land in zip(["w_up", "w_o", "w_attn_out", "w_conv_out"], got):
            adam(name, land, l)
        d_win = _wgrad_qkv(dq, dk, dv, sv["h"], d_win, f"wgrad_qkv{l}")
        side = _Exchange([blocks(d_win)]) if l == 0 else None
        (dcur, acc_in[l]), got_in = _inproj_bwd(dgt, dq, dk, dv, sv["x"], dx1, g_mix3, wint[l], l, tm_in, side=side)
        d_win_above = d_win

    dsinks = jnp.stack([dsink[l][:, :, :, 0].sum(axis=0).reshape(16) for l in range(L)])
    small = _pack_small(
        jnp.concatenate([acc_in[l][0:1] for l in range(L)]),
        jnp.stack([acc_mix[l][0:2].reshape(2 * D) for l in range(L)]),
        dsinks,
        jnp.concatenate([acc_mix[l][2:3] for l in range(L)]),
        jnp.concatenate([acc_mlp[l][0:1] for l in range(L)]),
        acc_loss[0],
        jnp.concatenate([acc_mix[l][3:6] for l in range(L)]),
        acc_loss[1:2])
    adam("w_in", got_in[0], 0)
    (small_land,) = _remote_only(_Exchange([], [small]), "exchange_small")

    row = lambda t: t.reshape(1, D)
    small_out, loss = _adam_small(small_land, me.reshape(1).astype(jnp.int32), {
        "g_mix": (g_mix, m_g_mix, v_g_mix), "b_gates": (b_gates, m_b_gates, v_b_gates), "sinks": (sinks, m_sinks, v_sinks),
        "conv_b": (conv_b, m_conv_b, v_conv_b), "g_mlp": (g_mlp, m_g_mlp, v_g_mlp),
        "g_final": (row(g_final), row(m_g_final), row(v_g_final)), "conv_w": (conv_w, m_conv_w, v_conv_w)})
    small_out["g_final"] = [t.reshape(D) for t in small_out["g_final"]]
    big["w_in"] = [tr(o) for o in big["w_in"]]
    order = ["g_mix", "w_in", "b_gates", "sinks", "w_attn_out", "conv_w", "conv_b", "w_conv_out", "w_o", "g_mlp",
             "w_up", "w_down", "g_final"]
    out = [loss.reshape(()), dcur.reshape(nseq, S, D)]
    for kind in range(4):
        for name in order:
            out.append(big[name][kind] if name in big else small_out[name][kind])
    return tuple(out)
```

```python
import numpy as np
import jax
import jax.numpy as jnp
from jax import lax
from jax.experimental import pallas as pl
from jax.experimental.pallas import tpu as pltpu

D = 1024
NG = 5 * D
NQ = 1024
NKV = 256
NP = NQ + 2 * NKV + NG
F = 4096
HD = 64
GQ = 4
WIN = 128
L = 2
NDEV = 8
EPS = 1e-6
NEG = -1e30
SMALL_ROWS = 24
LR, B1, B2, AEPS, WD, STEP = 0.001, 0.9, 0.999, 1e-08, 0.01, 10

BF = jnp.bfloat16
F32 = jnp.float32
MESH = pl.DeviceIdType.MESH
VMEM_LIMIT = 60 * 1024 * 1024
STREAM_BUFFERS = 3
ANY = pl.BlockSpec(memory_space=pl.ANY)

NN = ((1,), (0,))
NT = ((1,), (1,))
TN = ((0,), (0,))


def _dot(a, b, dims):
    return lax.dot_general(a, b, (dims, ((), ())), preferred_element_type=F32)


def _resident(shape, imap):
    return pl.BlockSpec(shape, imap, pipeline_mode=pl.Buffered(1))


def _position():
    return lax.axis_index("x"), lax.axis_index("y"), lax.axis_index("c")


class _Gather:
    def __init__(self, shards):
        n = len(shards)
        self.inputs = list(shards)
        self.out_shape = [jax.ShapeDtypeStruct((s.shape[0], NDEV) + s.shape[1:], s.dtype) for s in shards]
        self.scratch = [pltpu.SemaphoreType.DMA((n, 7)), pltpu.SemaphoreType.DMA((n, 7)), pltpu.SemaphoreType.DMA((n,))]

    def _plan(self, src, dst, sems):
        send_sems, recv_sems, local_sems = sems
        n = len(src)
        x, y, c = _position()
        me, sibling = (x, y, c), (x, y, 1 - c)
        chips = [(1 - x, y), (x, 1 - y), (1 - x, 1 - y)]

        def rows(a, p):
            return dst[a].at[:, 4 * p[0] + 2 * p[1] + p[2]]

        def copy(a, k, block, to, from_src=False):
            return pltpu.make_async_remote_copy(
                src_ref=src[a] if from_src else rows(a, block), dst_ref=rows(a, block),
                send_sem=send_sems.at[a, k], recv_sem=recv_sems.at[a, k], device_id=to, device_id_type=MESH)

        mine = [pltpu.make_async_copy(src[a], rows(a, me), local_sems.at[a]) for a in range(n)]
        first = []
        for a in range(n):
            first.append(copy(a, 0, me, sibling, True))
            first += [copy(a, 1 + t, me, (*chip, c), True) for t, chip in enumerate(chips)]
        return n, c, me, sibling, chips, copy, mine, first

    def start(self, src, dst, sems):
        *_, mine, first = self._plan(src, dst, sems)
        for cp in mine + first:
            cp.start()

    def relay(self, src, dst, sems):
        n, c, me, sibling, chips, copy, _, _ = self._plan(src, dst, sems)
        for t, chip in enumerate(chips):
            for a in range(n):
                copy(a, 1 + t, (*chip, c), me).wait_recv()
                copy(a, 4 + t, (*chip, c), sibling).start()

    def finish(self, src, dst, sems):
        n, c, me, sibling, chips, copy, mine, first = self._plan(src, dst, sems)
        for a in range(n):
            copy(a, 0, sibling, me).wait_recv()
            for t, chip in enumerate(chips):
                copy(a, 4 + t, (*chip, 1 - c), me).wait_recv()
        for cp in first + [copy(a, 4 + t, (*chip, c), sibling) for t, chip in enumerate(chips) for a in range(n)]:
            cp.wait_send()
        for cp in mine:
            cp.wait()


class _Exchange:
    def __init__(self, grads, everyone=()):
        self.inputs = list(grads) + list(everyone)
        self.n_blocked = len(grads)
        n = len(self.inputs)
        self.out_shape = [jax.ShapeDtypeStruct(g.shape, g.dtype) for g in grads]
        self.out_shape += [jax.ShapeDtypeStruct((NDEV,) + e.shape, e.dtype) for e in everyone]
        self.scratch = [pltpu.SemaphoreType.DMA((n, 7)), pltpu.SemaphoreType.DMA((n, 7)), pltpu.SemaphoreType.DMA((n,))]

    def _plan(self, src, land, sems):
        send_sems, recv_sems, local_sems = sems
        x, y, c = _position()
        me = 4 * x + 2 * y + c

        def parts(peer_idx):
            return [(s.at[peer_idx] if a < self.n_blocked else s, land[a].at[me]) for a, s in enumerate(src)]

        local = [pltpu.make_async_copy(s, d, local_sems.at[a]) for a, (s, d) in enumerate(parts(me))]
        sent = []
        for rel in range(1, NDEV):
            px = 1 - x if rel & 4 else x
            py = 1 - y if rel & 2 else y
            pc = 1 - c if rel & 1 else c
            for a, (s, d) in enumerate(parts(4 * px + 2 * py + pc)):
                sent.append(pltpu.make_async_remote_copy(
                    src_ref=s, dst_ref=d, send_sem=send_sems.at[a, rel - 1], recv_sem=recv_sems.at[a, rel - 1],
                    device_id=(px, py, pc), device_id_type=MESH))
        return local, sent

    def start(self, src, land, sems):
        local, sent = self._plan(src, land, sems)
        for cp in local + sent:
            cp.start()

    def relay(self, src, land, sems):
        pass

    def finish(self, src, land, sems):
        local, sent = self._plan(src, land, sems)
        for cp in sent:
            cp.wait_recv()
        for cp in sent:
            cp.wait_send()
        for cp in local:
            cp.wait()


def _call(body, *, name, grid, in_specs, out_specs, out_shape, args, scratch_shapes=(), aliases=None, side=None,
          relay_at=1.0):
    sem = ("arbitrary",) * len(grid)
    if side is None:
        outs = pl.pallas_call(
            body, name=name, grid=grid, in_specs=in_specs, out_specs=out_specs, out_shape=out_shape,
            scratch_shapes=list(scratch_shapes), input_output_aliases=aliases or {},
            compiler_params=pltpu.CompilerParams(dimension_semantics=sem, vmem_limit_bytes=VMEM_LIMIT))(*args)
        return outs, []
    ni, no, ns = len(in_specs), len(out_specs), len(scratch_shapes)
    si, so = len(side.inputs), len(side.out_shape)

    def hosted(*refs):
        ins, refs = refs[:ni], refs[ni:]
        sins, refs = refs[:si], refs[si:]
        outs, refs = refs[:no], refs[no:]
        souts, refs = refs[:so], refs[so:]
        scr, sscr = refs[:ns], refs[ns:]
        step = pl.program_id(0)
        for d in range(1, len(grid)):
            step = step * grid[d] + pl.program_id(d)
        last = int(np.prod(grid)) - 1

        @pl.when(step == 0)
        def _():
            side.start(sins, souts, sscr)

        body(*ins, *outs, *scr)

        @pl.when(step == min(int(relay_at * last), last))
        def _():
            side.relay(sins, souts, sscr)

        @pl.when(step == last)
        def _():
            side.finish(sins, souts, sscr)

    outs = pl.pallas_call(
        hosted, name=name, grid=grid, in_specs=list(in_specs) + [ANY] * si, out_specs=list(out_specs) + [ANY] * so,
        out_shape=list(out_shape) + side.out_shape, scratch_shapes=list(scratch_shapes) + side.scratch,
        input_output_aliases=aliases or {},
        compiler_params=pltpu.CompilerParams(dimension_semantics=sem, vmem_limit_bytes=VMEM_LIMIT, has_side_effects=True),
    )(*args, *side.inputs)
    return outs[:no], outs[no:]


def _remote_only(side, name):
    n = len(side.inputs)

    def body(*refs):
        src, dst, sems = refs[:n], refs[n:n + len(side.out_shape)], refs[n + len(side.out_shape):]
        side.start(src, dst, sems)
        side.relay(src, dst, sems)
        side.finish(src, dst, sems)

    return pl.pallas_call(
        body, name=name, in_specs=[ANY] * n, out_specs=[ANY] * len(side.out_shape), out_shape=side.out_shape,
        scratch_shapes=side.scratch, compiler_params=pltpu.CompilerParams(has_side_effects=True))(*side.inputs)


def _rms(x, g):
    r = lax.rsqrt(jnp.mean(x * x, axis=-1, keepdims=True) + EPS)
    return x * r * g


def _rms_bwd(dy, x, g):
    r = lax.rsqrt(jnp.mean(x * x, axis=-1, keepdims=True) + EPS)
    xh = x * r
    dxh = dy * g
    dx = r * (dxh - xh * jnp.mean(dxh * xh, axis=-1, keepdims=True))
    return dx, jnp.sum(dy * xh, axis=0, keepdims=True)


def _halves(n):
    return [slice(0, n // 2), slice(n // 2, n)] if n % 256 == 0 else [slice(0, n)]


def _zero_at_first_step(acc_ref):
    first = pl.program_id(0) == 0

    @pl.when(first)
    def _():
        acc_ref[...] = jnp.zeros_like(acc_ref)


ROW1 = lambda: _resident((1, D), lambda i: (0, 0))
LROWS = lambda l, n=1: _resident((None, n, D), lambda *_: (l, 0, 0))
ACC = lambda: pl.BlockSpec((8, D), lambda i: (0, 0))


def _inproj_fwd(x, g, wint, l, tm, side=None, relay_at=1.0):
    T = x.shape[0]

    def body(x_ref, g_ref, w_ref, gt_ref, q_ref, k_ref, v_ref, h_ref):
        for rows in _halves(tm):
            h = _rms(x_ref[rows, :], g_ref[...]).astype(BF)
            h_ref[rows, :] = h
            q_ref[:, rows] = _dot(w_ref[0:NQ, :], h, NT).astype(BF)
            k_ref[:, rows] = _dot(w_ref[NQ:NQ + NKV, :], h, NT).astype(BF)
            v_ref[:, rows] = _dot(w_ref[NQ + NKV:NQ + 2 * NKV, :], h, NT).astype(BF)
            for s in range(5):
                lo = NQ + 2 * NKV + s * D
                gt_ref[rows, s * D:(s + 1) * D] = _dot(h, w_ref[lo:lo + D, :], NT).astype(BF)

    tok = lambda w: pl.BlockSpec((tm, w), lambda i: (i, 0))
    feat = lambda w: pl.BlockSpec((w, tm), lambda i: (0, i))
    return _call(
        body, name=f"inproj_fwd{l}", grid=(T // tm,),
        in_specs=[tok(D), LROWS(l), _resident((NP, D), lambda i: (0, 0))],
        out_specs=[tok(NG), feat(NQ), feat(NKV), feat(NKV), tok(D)],
        out_shape=[jax.ShapeDtypeStruct((T, NG), BF)] + [jax.ShapeDtypeStruct((w, T), BF) for w in (NQ, NKV, NKV)]
        + [jax.ShapeDtypeStruct((T, D), BF)],
        args=(x, g, wint), side=side, relay_at=relay_at)


def _band_geometry():
    j = lax.broadcasted_iota(jnp.int32, (2 * WIN, WIN), 0)
    r = lax.broadcasted_iota(jnp.int32, (2 * WIN, WIN), 1)
    dist = WIN + r - j
    dist0 = r - j
    return (dist.astype(F32), (dist >= 0) & (dist < WIN)), (dist0.astype(F32), dist0 >= 0)


def _pair_biases(sm_ref, pj):
    return [[jnp.where(ok, -sm_ref[1, pj * 2 * GQ + h] * dist, NEG) for h in range(2 * GQ)]
            for dist, ok in _band_geometry()]


def _reduce_rows(x, pair, whole):
    while x.shape[0] > 8:
        half = x.shape[0] // 2
        x = pair(x[:half], x[half:])
    return whole(x, axis=0, keepdims=True)


def _heads_on_lanes(ref, kvh, r0):
    return jnp.concatenate([ref[(kvh * GQ + g) * HD:(kvh * GQ + g + 1) * HD, pl.ds(r0, WIN)] for g in range(GQ)], axis=1)


def _band_probs(sm_ref, head0, q_ref, k_ref, r0, p0, kvh, biases):
    qt = _heads_on_lanes(q_ref, kvh, r0) * jnp.asarray(HD ** -0.5, BF)
    kt = k_ref[kvh * HD:(kvh + 1) * HD, pl.ds(p0, 2 * WIN)]
    heads = []
    for g in range(GQ):
        sink = sm_ref[0, head0 + g]
        s = _dot(kt, qt[:, g * WIN:(g + 1) * WIN], TN) + biases[kvh * GQ + g]
        m = jnp.maximum(_reduce_rows(s, jnp.maximum, jnp.max), sink)
        p = jnp.exp(s - m)
        ps = jnp.exp(sink - m)
        heads.append((p, ps, 1.0 / (_reduce_rows(p, jnp.add, jnp.sum) + ps)))
    return qt, kt, heads


def _attn_fwd(q, k, v, sm, l, S, side=None, relay_at=1.0):
    T = q.shape[1]
    nblk = S // WIN
    unroll = next(u for u in (15, 5, 3, 1) if (nblk - 1) % u == 0)

    def body(sm_all_ref, q_ref, k_ref, v_ref, o_ref):
        sm_ref = sm_all_ref.at[l]
        pj = pl.program_id(1)
        biases = _pair_biases(sm_ref, pj)

        def block(i, first):
            r0 = 0 if first else pl.multiple_of(i * WIN, WIN)
            p0 = 0 if first else pl.multiple_of(i * WIN - WIN, WIN)
            for kvh in range(2):
                head0 = (2 * pj + kvh) * GQ
                _, _, heads = _band_probs(sm_ref, head0, q_ref, k_ref, r0, p0, kvh, biases[first])
                vt = v_ref[kvh * HD:(kvh + 1) * HD, pl.ds(p0, 2 * WIN)]
                for g, (p, _, inv) in enumerate(heads):
                    rows = slice((kvh * GQ + g) * HD, (kvh * GQ + g + 1) * HD)
                    o_ref[rows, pl.ds(r0, WIN)] = (_dot(vt, p.astype(BF), NN) * inv).astype(BF)

        block(0, True)

        def rest(t, c):
            for u in range(unroll):
                block(1 + unroll * t + u, False)
            return c

        lax.fori_loop(0, (nblk - 1) // unroll, rest, 0)

    wide = lambda: pl.BlockSpec((2 * GQ * HD, S), lambda s, p: (p, s))
    narrow = lambda: pl.BlockSpec((2 * HD, S), lambda s, p: (p, s))
    return _call(
        body, name=f"attn_fwd{l}", grid=(T // S, 2),
        in_specs=[pl.BlockSpec(memory_space=pltpu.SMEM), wide(), narrow(), narrow()],
        out_specs=[wide()], out_shape=[jax.ShapeDtypeStruct((D, T), BF)],
        args=(sm, q, k, v), side=side, relay_at=relay_at)


def _shift_rows(y, k, edge_rows, down):
    n = y.shape[0]
    rid = lax.broadcasted_iota(jnp.int32, y.shape, 0)
    out = pltpu.roll(y, k if down else n - k, 0)
    for t, row in enumerate(edge_rows):
        out = jnp.where(rid == (t if down else n - k + t), row, out)
    return out


def _mixer_fwd(x, gt, att, w3, bg, cw, cbias, l, S, tm):
    T = x.shape[0]

    nt = T // tm

    def body(x_ref, gt_ref, cch_ref, cuh_ref, att_ref, wao_ref, wco_ref, wo_ref,
             bg_ref, cw_ref, cbias_ref, x1_ref, mg_ref, co_ref, ya_ref, yc_ref, zb_ref, ring_ref, ring_sem):
        i = pl.program_id(0)

        def fetch(t):
            slot = t % STREAM_BUFFERS
            row0 = t * tm if isinstance(t, int) else pl.multiple_of(t * tm, tm)
            return pltpu.make_async_copy(gt_ref.at[pl.ds(row0, tm), :], ring_ref.at[slot], ring_sem.at[slot])

        @pl.when(i == 0)
        def _():
            for t in range(min(STREAM_BUFFERS - 1, nt)):
                fetch(t).start()

        @pl.when(i + STREAM_BUFFERS - 1 < nt)
        def _():
            fetch(i + STREAM_BUFFERS - 1).start()

        fetch(i).wait()
        tile = ring_ref.at[i % STREAM_BUFFERS]
        cb_ref, cc_ref, cu_ref, ga_ref, gc_ref = (tile.at[:, s * D:(s + 1) * D] for s in range(5))
        first = (i * tm) % S == 0
        y = cc_ref[...].astype(F32) * cu_ref[...].astype(F32)
        hy1 = cch_ref[15:16, :].astype(F32) * cuh_ref[15:16, :].astype(F32)
        hy2 = cch_ref[14:15, :].astype(F32) * cuh_ref[14:15, :].astype(F32)
        hy1, hy2 = jnp.where(first, 0.0, hy1), jnp.where(first, 0.0, hy2)
        z = (cw_ref[0:1, :] * _shift_rows(y, 2, [hy2, hy1], True) + cw_ref[1:2, :] * _shift_rows(y, 1, [hy1], True)
             + cw_ref[2:3, :] * y)
        zb = z + cbias_ref[...]
        zb_ref[...] = zb.astype(BF)
        co = (cb_ref[...].astype(F32) * zb).astype(BF)
        co_ref[...] = co
        yc = _dot(co, wco_ref[...], NN)
        ya = _dot(att_ref[...], wao_ref[...], TN)
        ya_ref[...] = ya.astype(BF)
        yc_ref[...] = yc.astype(BF)
        sa = jax.nn.sigmoid(ga_ref[...].astype(F32) + bg_ref[0:1, :])
        sc = jax.nn.sigmoid(gc_ref[...].astype(F32) + bg_ref[1:2, :])
        mg = (sa * ya + sc * yc).astype(BF)
        mg_ref[...] = mg
        x1_ref[...] = x_ref[...] + _dot(mg, wo_ref[...], NN)

    tok = lambda: pl.BlockSpec((tm, D), lambda i: (i, 0))
    halo = lambda s: pl.BlockSpec((16, D), lambda i: (jnp.maximum(i * (tm // 16) - 1, 0), s))
    wsp = lambda k: _resident((None, D, D), lambda i: (k, 0, 0))
    return _call(
        body, name=f"mixer_fwd{l}", grid=(nt,),
        in_specs=[tok(), ANY, halo(1), halo(2), pl.BlockSpec((D, tm), lambda i: (0, i)),
                  wsp(0), wsp(1), wsp(2), LROWS(l, 2), LROWS(l, 8), LROWS(l)],
        out_specs=[tok()] * 6,
        out_shape=[jax.ShapeDtypeStruct((T, D), dt) for dt in (F32, BF, BF, BF, BF, BF)],
        scratch_shapes=[pltpu.VMEM((STREAM_BUFFERS, tm, NG), BF), pltpu.SemaphoreType.DMA((STREAM_BUFFERS,))],
        args=(x, gt, gt, gt, att, w3, w3, w3, bg, cw, cbias))


def _mlp_fwd(x1, g, w2, l, tm, head=None, side=None, relay_at=1.0):
    T = x1.shape[0]
    nt = T // tm
    FC = 1024

    def body(x_ref, g_ref, wup_ref, wdn_ref, *rest):
        out_ref, h_ref, a_ref = rest[-4:-1] if head else rest
        if head:
            t_ref, gf_ref, acc_ref = rest[0], rest[1], rest[-1]
            _zero_at_first_step(acc_ref)
        for rows in _halves(tm):
            x = x_ref[rows, :]
            h = _rms(x, g_ref[...]).astype(BF)
            h_ref[rows, :] = h
            acc = x
            for c in range(F // FC):
                a = _dot(h, wup_ref[c * FC:(c + 1) * FC, :], NT)
                a_ref[rows, c * FC:(c + 1) * FC] = a.astype(BF)
                u = jnp.maximum(a, 0.0)
                acc = acc + _dot((u * u).astype(BF), wdn_ref[c * FC:(c + 1) * FC, :], NN)
            if not head:
                out_ref[rows, :] = acc
                continue
            gf = gf_ref[...]
            err = _rms(acc, gf) - t_ref[rows, :]
            out_ref[rows, :], dg = _rms_bwd(err * (1.0 / D), acc, gf)
            acc_ref[0:1, :] += dg
            acc_ref[1:2, :] += jnp.sum(err * err, axis=0, keepdims=True)
        if not head:
            return

        @pl.when(pl.program_id(0) == nt - 1)
        def _():
            acc_ref[1:2, :] = jnp.zeros((1, D), F32) + (0.5 / D) * jnp.sum(acc_ref[1:2, :])

    tok = lambda w: pl.BlockSpec((tm, w), lambda i: (i, 0))
    wsp = lambda k: _resident((None, F, D), lambda i: (k, 0, 0))
    return _call(
        body, name=f"mlp_fwd{l}", grid=(nt,),
        in_specs=[tok(D), LROWS(l), wsp(0), wsp(1)] + ([tok(D), ROW1()] if head else []),
        out_specs=[tok(D), tok(D), tok(F)] + ([ACC()] if head else []),
        out_shape=[jax.ShapeDtypeStruct((T, D), F32), jax.ShapeDtypeStruct((T, D), BF), jax.ShapeDtypeStruct((T, F), BF)]
        + ([jax.ShapeDtypeStruct((8, D), F32)] if head else []),
        args=(x1, g, w2, w2) + (tuple(head) if head else ()), side=side, relay_at=relay_at)


def _mlp_bwd(dx2, x1, a, g, w2, l, tm, side=None):
    T = dx2.shape[0]
    FC = 1024

    def body(d_ref, x_ref, a_ref, g_ref, wup_ref, wdn_ref, da_ref, dx1_ref, db_ref, acc_ref):
        _zero_at_first_step(acc_ref)
        for rows in _halves(tm):
            d = d_ref[rows, :]
            db = d.astype(BF)
            db_ref[rows, :] = db
            dh = jnp.zeros(d.shape, F32)
            for c in range(F // FC):
                du = _dot(db, wdn_ref[c * FC:(c + 1) * FC, :], NT)
                da = (du * (2.0 * jnp.maximum(a_ref[rows, c * FC:(c + 1) * FC].astype(F32), 0.0))).astype(BF)
                da_ref[rows, c * FC:(c + 1) * FC] = da
                dh = dh + _dot(da, wup_ref[c * FC:(c + 1) * FC, :], NN)
            dx, dg = _rms_bwd(dh, x_ref[rows, :], g_ref[...])
            dx1_ref[rows, :] = d + dx
            acc_ref[0:1, :] += dg

    tok = lambda w: pl.BlockSpec((tm, w), lambda i: (i, 0))
    wsp = lambda k: _resident((None, F, D), lambda i: (k, 0, 0))
    return _call(
        body, name=f"mlp_bwd{l}", grid=(T // tm,),
        in_specs=[tok(D), tok(D), tok(F), LROWS(l), wsp(0), wsp(1)],
        out_specs=[tok(F), tok(D), tok(D), ACC()],
        out_shape=[jax.ShapeDtypeStruct((T, F), BF), jax.ShapeDtypeStruct((T, D), F32),
                   jax.ShapeDtypeStruct((T, D), BF), jax.ShapeDtypeStruct((8, D), F32)],
        args=(dx2, x1, a, g, w2, w2), side=side)


def _mixer_bwd(dx1, gt, ya, yc, zb, w3, bg, cw, l, S, tm, side=None):
    T = dx1.shape[0]
    nt = T // tm

    def body(d_ref, cb_ref, cc_ref, cu_ref, ga_ref, gc_ref, ya_ref, yc_ref, zb_ref, wao_ref, wco_ref, wo_ref,
             bg_ref, cw_ref, dg_ref, datt_ref, dya_ref, dyc_ref, db_ref, acc_ref, carry_ref):
        ti = nt - 1 - pl.program_id(0)
        _zero_at_first_step(acc_ref)

        @pl.when(((ti + 1) * tm) % S == 0)
        def _():
            carry_ref[...] = jnp.zeros_like(carry_ref)

        db = d_ref[...].astype(BF)
        db_ref[...] = db
        dm = _dot(db, wo_ref[...], NT)
        sa = jax.nn.sigmoid(ga_ref[...].astype(F32) + bg_ref[0:1, :])
        sc = jax.nn.sigmoid(gc_ref[...].astype(F32) + bg_ref[1:2, :])
        dya32 = dm * sa
        dyc32 = dm * sc
        dya = dya32.astype(BF)
        dyc = dyc32.astype(BF)
        dya_ref[...] = dya
        dyc_ref[...] = dyc
        dga = dya32 * ya_ref[...].astype(F32) * (1.0 - sa)
        dgc = dyc32 * yc_ref[...].astype(F32) * (1.0 - sc)
        dg_ref[:, 3 * D:4 * D] = dga.astype(BF)
        dg_ref[:, 4 * D:5 * D] = dgc.astype(BF)
        acc_ref[0:1, :] += jnp.sum(dga, axis=0, keepdims=True)
        acc_ref[1:2, :] += jnp.sum(dgc, axis=0, keepdims=True)
        datt_ref[...] = _dot(wao_ref[...], dya, NT).astype(BF)
        dco = _dot(dyc, wco_ref[...], NT)

        cc = cc_ref[...].astype(F32)
        cu = cu_ref[...].astype(F32)
        y = cc * cu
        dg_ref[:, 0:D] = (dco * zb_ref[...].astype(F32)).astype(BF)
        dz = dco * cb_ref[...].astype(F32)
        u1 = _shift_rows(dz, 1, [carry_ref[0:1, :]], False)
        u2 = _shift_rows(dz, 2, [carry_ref[0:1, :], carry_ref[1:2, :]], False)
        acc_ref[2:3, :] += jnp.sum(dz, axis=0, keepdims=True)
        acc_ref[3:4, :] += jnp.sum(u2 * y, axis=0, keepdims=True)
        acc_ref[4:5, :] += jnp.sum(u1 * y, axis=0, keepdims=True)
        acc_ref[5:6, :] += jnp.sum(dz * y, axis=0, keepdims=True)
        dy = cw_ref[2:3, :] * dz + cw_ref[1:2, :] * u1 + cw_ref[0:1, :] * u2
        dg_ref[:, D:2 * D] = (dy * cu).astype(BF)
        dg_ref[:, 2 * D:3 * D] = (dy * cc).astype(BF)
        carry_ref[...] = dz[0:8, :]

    tok = lambda w=D: pl.BlockSpec((tm, w), lambda i: (nt - 1 - i, 0))
    seg = lambda s: pl.BlockSpec((tm, D), lambda i: (nt - 1 - i, s))
    wsp = lambda k: _resident((None, D, D), lambda i: (k, 0, 0))
    return _call(
        body, name=f"mixer_bwd{l}", grid=(nt,),
        in_specs=[tok(), seg(0), seg(1), seg(2), seg(3), seg(4), tok(), tok(), tok(), wsp(0), wsp(1), wsp(2),
                  LROWS(l, 2), LROWS(l, 8)],
        out_specs=[tok(NG), pl.BlockSpec((D, tm), lambda i: (0, nt - 1 - i)), tok(), tok(), tok(), ACC()],
        out_shape=[jax.ShapeDtypeStruct((T, NG), BF), jax.ShapeDtypeStruct((D, T), BF)]
        + [jax.ShapeDtypeStruct((T, D), BF)] * 3 + [jax.ShapeDtypeStruct((8, D), F32)],
        scratch_shapes=[pltpu.VMEM((8, D), F32)],
        args=(dx1, gt, gt, gt, gt, gt, ya, yc, zb, w3, w3, w3, bg, cw), side=side)


def _attn_bwd(q, k, v, att, datt, sm, l, S, side=None):
    T = q.shape[1]
    nblk = S // WIN
    unroll = next(u for u in (15, 5, 3, 1) if (nblk - 1) % u == 0)
    scale = HD ** -0.5

    def body(sm_all_ref, q_ref, k_ref, v_ref, o_ref, do_ref, dq_ref, dk_ref, dv_ref, ds_ref, dka_ref, dva_ref):
        sm_ref = sm_all_ref.at[l]
        pj = pl.program_id(1)
        biases = _pair_biases(sm_ref, pj)
        dka_ref[...] = jnp.zeros_like(dka_ref)
        dva_ref[...] = jnp.zeros_like(dva_ref)

        def block(i, first, dsinks):
            r0 = 0 if first else pl.multiple_of(i * WIN, WIN)
            p0 = 0 if first else pl.multiple_of(i * WIN - WIN, WIN)
            out = []
            for kvh in range(2):
                head0 = (2 * pj + kvh) * GQ
                rows = slice(kvh * HD, (kvh + 1) * HD)
                dob = _heads_on_lanes(do_ref, kvh, r0)
                vt = v_ref[rows, pl.ds(p0, 2 * WIN)]
                dpts = [_dot(vt, dob[:, g * WIN:(g + 1) * WIN], TN) for g in range(GQ)]
                qt, kt, heads = _band_probs(sm_ref, head0, q_ref, k_ref, r0, p0, kvh, biases[first])
                inv = jnp.concatenate([h[2] for h in heads], axis=1)
                do32 = dob.astype(F32)
                delta = jnp.sum(do32 * _heads_on_lanes(o_ref, kvh, r0).astype(F32), axis=0, keepdims=True) * inv
                dosb = (do32 * inv).astype(BF)
                dsts = [(p * (dpts[g] * h_inv - delta[:, g * WIN:(g + 1) * WIN])).astype(BF)
                        for g, (p, _, h_inv) in enumerate(heads)]
                for g in range(GQ):
                    hr = slice((kvh * GQ + g) * HD, (kvh * GQ + g + 1) * HD)
                    dq_ref[hr, pl.ds(r0, WIN)] = (_dot(kt, dsts[g], NN) * scale).astype(BF)
                lanes = lambda g: slice(g * WIN, (g + 1) * WIN)
                dka_ref[rows, pl.ds(p0, 2 * WIN)] += sum(_dot(qt[:, lanes(g)], dsts[g], NT) for g in range(GQ))
                dva_ref[rows, pl.ds(p0, 2 * WIN)] += sum(
                    _dot(dosb[:, lanes(g)], p.astype(BF), NT) for g, (p, _, _) in enumerate(heads))
                ps = jnp.concatenate([h[1] for h in heads], axis=1)
                out.append(dsinks[kvh] - ps * delta)
            return tuple(out)

        zero = jnp.zeros((1, GQ * WIN), F32)
        def rest(t, c):
            for u in range(unroll):
                c = block(1 + unroll * t + u, False, c)
            return c

        dsinks = lax.fori_loop(0, (nblk - 1) // unroll, rest, block(0, True, (zero, zero)))
        for kvh in range(2):
            for g in range(GQ):
                tot = jnp.sum(dsinks[kvh][:, g * WIN:(g + 1) * WIN])
                ds_ref[kvh * GQ + g:kvh * GQ + g + 1, :] = jnp.zeros((1, 128), F32) + tot
        dk_ref[...] = dka_ref[...].astype(BF)
        dv_ref[...] = dva_ref[...].astype(BF)

    wide = lambda: pl.BlockSpec((2 * GQ * HD, S), lambda s, p: (p, s))
    narrow = lambda: pl.BlockSpec((2 * HD, S), lambda s, p: (p, s))
    return _call(
        body, name=f"attn_bwd{l}", grid=(T // S, 2),
        in_specs=[pl.BlockSpec(memory_space=pltpu.SMEM), wide(), narrow(), narrow(), wide(), wide()],
        out_specs=[wide(), narrow(), narrow(), pl.BlockSpec((None, None, 8, 128), lambda s, p: (s, p, 0, 0))],
        out_shape=[jax.ShapeDtypeStruct((NQ, T), BF), jax.ShapeDtypeStruct((NKV, T), BF),
                   jax.ShapeDtypeStruct((NKV, T), BF), jax.ShapeDtypeStruct((T // S, 2, 8, 128), F32)],
        scratch_shapes=[pltpu.VMEM((2 * HD, S), F32), pltpu.VMEM((2 * HD, S), F32)],
        args=(sm, q, k, v, att, datt), side=side)


def _inproj_bwd(dgt, dq, dk, dv, x, dres, g, wint, l, tm, side=None):
    T = x.shape[0]

    def body(dg_ref, dq_ref, dk_ref, dv_ref, x_ref, dr_ref, g_ref, w_ref, dx_ref, acc_ref):
        _zero_at_first_step(acc_ref)
        for rows in _halves(tm):
            dh = _dot(dq_ref[:, rows], w_ref[0:NQ, :], TN)
            dh = dh + _dot(dk_ref[:, rows], w_ref[NQ:NQ + NKV, :], TN)
            dh = dh + _dot(dv_ref[:, rows], w_ref[NQ + NKV:NQ + 2 * NKV, :], TN)
            dh = dh + _dot(dg_ref[rows, :], w_ref[NQ + 2 * NKV:NP, :], NN)
            dx, dg = _rms_bwd(dh, x_ref[rows, :], g_ref[...])
            dx_ref[rows, :] = dr_ref[rows, :] + dx
            acc_ref[0:1, :] += dg

    tok = lambda w: pl.BlockSpec((tm, w), lambda i: (i, 0))
    feat = lambda w: pl.BlockSpec((w, tm), lambda i: (0, i))
    return _call(
        body, name=f"inproj_bwd{l}", grid=(T // tm,),
        in_specs=[tok(NG), feat(NQ), feat(NKV), feat(NKV), tok(D), tok(D), LROWS(l), _resident((NP, D), lambda i: (0, 0))],
        out_specs=[tok(D), ACC()],
        out_shape=[jax.ShapeDtypeStruct((T, D), F32), jax.ShapeDtypeStruct((8, D), F32)],
        args=(dgt, dq, dk, dv, x, dres, g, wint), side=side)


def _wgrad(a, b, rows, row0, into, name, relu2=False):
    T, M = a.shape
    tmm = next(t for t in (1024, 512, 256) if M % t == 0 and row0 % t == 0)
    tk = min(4096 if M > 4096 else 2048, T)
    nk = T // tk
    blk0 = row0 // tmm

    def body(*refs):
        a_ref, b_ref = refs[0], refs[1]
        o_ref, acc_ref = refs[-2], refs[-1]
        kk = pl.program_id(1)

        @pl.when(kk == 0)
        def _():
            acc_ref[...] = jnp.zeros_like(acc_ref)

        av = a_ref[...]
        if relu2:
            t = jnp.maximum(av.astype(F32), 0.0)
            av = (t * t).astype(BF)
        acc_ref[...] += _dot(av, b_ref[...], TN)

        @pl.when(kk == nk - 1)
        def _():
            o_ref[...] = acc_ref[...].astype(BF)

    in_specs = [pl.BlockSpec((tk, tmm), lambda j, kk: (kk, j)), pl.BlockSpec((tk, D), lambda j, kk: (kk, 0))]
    args = [a, b]
    if into is not None:
        in_specs.append(ANY)
        args.append(into)
    (out,), _ = _call(
        body, name=name, grid=(M // tmm, nk), in_specs=in_specs,
        out_specs=[pl.BlockSpec((tmm, D), lambda j, kk: (blk0 + j, 0))],
        out_shape=[jax.ShapeDtypeStruct((rows, D), BF)], scratch_shapes=[pltpu.VMEM((tmm, D), F32)],
        aliases={2: 0} if into is not None else None, args=args)
    return out


def _wgrad_qkv(dq, dk, dv, h, into, name):
    T = h.shape[0]
    rows = NQ + 2 * NKV
    tk = min(2048, T)
    nk = T // tk

    def body(dq_ref, dk_ref, dv_ref, h_ref, _, o_ref, acc_ref):
        kk = pl.program_id(0)

        @pl.when(kk == 0)
        def _():
            acc_ref[...] = jnp.zeros_like(acc_ref)

        hv = h_ref[...]
        acc_ref[0:NQ, :] += _dot(dq_ref[...], hv, NN)
        acc_ref[NQ:NQ + NKV, :] += _dot(dk_ref[...], hv, NN)
        acc_ref[NQ + NKV:rows, :] += _dot(dv_ref[...], hv, NN)

        @pl.when(kk == nk - 1)
        def _():
            o_ref[...] = acc_ref[...].astype(BF)

    feat = lambda w: pl.BlockSpec((w, tk), lambda kk: (0, kk))
    (out,), _ = _call(
        body, name=name, grid=(nk,),
        in_specs=[feat(NQ), feat(NKV), feat(NKV), pl.BlockSpec((tk, D), lambda kk: (kk, 0)), ANY],
        out_specs=[pl.BlockSpec((rows, D), lambda kk: (0, 0))],
        out_shape=[jax.ShapeDtypeStruct((NP, D), BF)], scratch_shapes=[pltpu.VMEM((rows, D), F32)],
        aliases={4: 0}, args=[dq, dk, dv, h, into])
    return out


def _wgrad_square(jobs, name):
    n = len(jobs)
    T = jobs[0][1].shape[0]
    tk = min(1024, T)
    nk = T // tk

    def body(*refs):
        ins, outs, acc_ref = refs[:2 * n], refs[2 * n:3 * n], refs[3 * n]
        j, kk = pl.program_id(0), pl.program_id(1)

        @pl.when(kk == 0)
        def _():
            acc_ref[...] = jnp.zeros_like(acc_ref)

        for p, (_, _, a_t) in enumerate(jobs):
            @pl.when(j == p)
            def _(p=p, a_t=a_t):
                acc_ref[...] += _dot(ins[2 * p][...], ins[2 * p + 1][...], NN if a_t else TN)

                @pl.when(kk == nk - 1)
                def _():
                    outs[p][...] = acc_ref[...].astype(BF)

    def step(p):
        return lambda j, kk: jnp.where(j == p, kk, jnp.where(j > p, nk - 1, 0))

    in_specs, args = [], []
    for p, (a, b, a_t) in enumerate(jobs):
        at = step(p)
        in_specs.append(pl.BlockSpec((D, tk), lambda j, kk, at=at: (0, at(j, kk))) if a_t
                        else pl.BlockSpec((tk, D), lambda j, kk, at=at: (at(j, kk), 0)))
        in_specs.append(pl.BlockSpec((tk, D), lambda j, kk, at=at: (at(j, kk), 0)))
        args += [a, b]
    outs, _ = _call(
        body, name=name, grid=(n, nk), in_specs=in_specs,
        out_specs=[pl.BlockSpec((D, D), lambda j, kk: (0, 0))] * n,
        out_shape=[jax.ShapeDtypeStruct((D, D), BF)] * n, scratch_shapes=[pltpu.VMEM((D, D), F32)], args=args)
    return outs


def _adamw(w, g, m, v):
    m = B1 * m + (1.0 - B1) * g
    v = B2 * v + (1.0 - B2) * (g * g)
    m_hat = m / (1.0 - B1 ** STEP)
    v_hat = v / (1.0 - B2 ** STEP)
    return -LR * (m_hat / (jnp.sqrt(v_hat) + AEPS) + WD * w), m, v


def _adam_sum(land, w, m, v, l, into, name, transposed=False):
    _, r, _ = land.shape
    tr = 208 if r % 208 == 0 else (256 if r % 256 == 0 else r)
    tc = 256

    def body(land_ref, w_ref, m_ref, v_ref, *rest):
        g_ref, d_ref, nm_ref, nv_ref = rest[-4:]
        g = land_ref[0].astype(F32)
        for s in range(1, NDEV):
            g = g + land_ref[s].astype(F32)
        if transposed:
            g = g.T
        g_ref[...] = g
        d_ref[...], nm_ref[...], nv_ref[...] = _adamw(w_ref[...], g, m_ref[...], v_ref[...])

    if transposed:
        blk = lambda: pl.BlockSpec((None, tc, r), lambda j: (l, j, 0))
        land_spec, grid = pl.BlockSpec((NDEV, r, tc), lambda j: (0, 0, j)), (D // tc,)
    else:
        blk = lambda: pl.BlockSpec((None, tr, D), lambda j: (l, j, 0))
        land_spec, grid = pl.BlockSpec((NDEV, tr, D), lambda j: (0, j, 0)), (r // tr,)
    in_specs = [land_spec, blk(), blk(), blk()]
    args = [land, w, m, v]
    aliases = None
    if into is not None:
        in_specs += [ANY] * 4
        args += list(into)
        aliases = {4 + t: t for t in range(4)}
    outs, _ = _call(body, name=name, grid=grid, in_specs=in_specs, out_specs=[blk()] * 4,
                    out_shape=[jax.ShapeDtypeStruct(w.shape, F32)] * 4, aliases=aliases, args=args)
    return outs


SMALL_NAMES = ("g_mix", "b_gates", "sinks", "conv_b", "g_mlp", "g_final", "conv_w")


def _adam_small(land, me, masters):
    n = len(SMALL_NAMES)
    lanes = D // NDEV

    def body(land_ref, me_ref, *refs):
        ins, outs, loss_ref, gs_ref = refs[:3 * n], refs[3 * n:7 * n], refs[7 * n], refs[7 * n + 1]
        g = land_ref[0]
        for s in range(1, NDEV):
            g = g + land_ref[s]
        gs_ref[...] = g

        def update(k, g_piece, idx):
            w_ref, m_ref, v_ref = ins[3 * k:3 * k + 3]
            outs[4 * k][idx] = g_piece
            outs[4 * k + 1][idx], outs[4 * k + 2][idx], outs[4 * k + 3][idx] = _adamw(w_ref[idx], g_piece, m_ref[idx], v_ref[idx])

        whole = (slice(None), slice(None))
        update(0, gs_ref[0:2, :], whole)
        for l in range(L):
            for h in range(2):
                update(1, gs_ref[2 + 2 * l + h:3 + 2 * l + h, :], (slice(l, l + 1), slice(h * D, (h + 1) * D)))
            update(2, gs_ref[6:7, 16 * l:16 * (l + 1)], (slice(l, l + 1), slice(None)))
        update(3, gs_ref[7:9, :], whole)
        update(4, gs_ref[9:11, :], whole)
        update(5, gs_ref[11:12, :], whole)
        mine = pl.ds(pl.multiple_of(me_ref[0] * lanes, lanes), lanes)
        for l in range(L):
            for k in range(3):
                update(6, gs_ref[12 + 3 * l + k:13 + 3 * l + k, mine], (l, slice(k, k + 1), slice(None)))
        loss_ref[...] = gs_ref[18:19, 0:1]

    flat = [t for name in SMALL_NAMES for t in masters[name]]
    vmem = pl.BlockSpec(memory_space=pltpu.VMEM)
    outs = pl.pallas_call(
        body, name="adam_small",
        in_specs=[vmem, pl.BlockSpec(memory_space=pltpu.SMEM)] + [vmem] * len(flat),
        out_shape=[jax.ShapeDtypeStruct(masters[name][0].shape, F32) for name in SMALL_NAMES for _ in range(4)]
        + [jax.ShapeDtypeStruct((1, 1), F32)],
        scratch_shapes=[pltpu.VMEM((SMALL_ROWS, D), F32)],
        compiler_params=pltpu.CompilerParams(vmem_limit_bytes=VMEM_LIMIT))(land, me, *flat)
    return {name: outs[4 * k:4 * k + 4] for k, name in enumerate(SMALL_NAMES)}, outs[-1]


def _pack_small(g_mix, b_gates, sinks, conv_b, g_mlp, g_final, conv_w_rows, extra):
    sink_row = jnp.zeros((1, D), F32).at[0, :2 * 16].set(sinks.reshape(-1))
    return jnp.concatenate([g_mix, b_gates.reshape(4, D), sink_row, conv_b, g_mlp, g_final.reshape(1, D),
                            conv_w_rows, extra, jnp.zeros((SMALL_ROWS - 19, D), F32)], axis=0)


def kernel(x, g_mix, w_in, b_gates, sinks, w_attn_out, conv_w, conv_b, w_conv_out, w_o, g_mlp, w_up, w_down, g_final, loss_target, m_g_mix, m_w_in, m_b_gates, m_sinks, m_w_attn_out, m_conv_w, m_conv_b, m_w_conv_out, m_w_o, m_g_mlp, m_w_up, m_w_down, m_g_final, v_g_mix, v_w_in, v_b_gates, v_sinks, v_w_attn_out, v_conv_w, v_conv_b, v_w_conv_out, v_w_o, v_g_mlp, v_w_up, v_w_down, v_g_final):
    nseq, S, _ = x.shape
    T = nseq * S
    tm_in = min(512, S)
    tm = min(512, S)
    xi, yi, ci = _position()
    me = 4 * xi + 2 * yi + ci
    tr = lambda t: jnp.swapaxes(t, 1, 2)
    blocks = lambda t: t.reshape(NDEV, t.shape[0] // NDEV, D)

    win_t, wup_t = tr(w_in), tr(w_up)
    sh_win = [win_t[l].astype(BF)[None] for l in range(L)]
    sh_w3 = [jnp.stack([w_attn_out[l], w_conv_out[l], w_o[l]]).astype(BF) for l in range(L)]
    sh_w2 = [jnp.stack([wup_t[l], w_down[l]]).astype(BF) for l in range(L)]
    wint, w3, w2 = [None] * L, [None] * L, [None] * L
    wint0_g, cw_g = _remote_only(_Gather([sh_win[0], jnp.pad(conv_w, ((0, 0), (0, 5), (0, 0)))]), "gather_first")
    wint[0] = wint0_g.reshape(NP, D)
    cw = jnp.swapaxes(cw_g, 1, 2).reshape(L, 8, D)
    slopes = np.power(np.float32(2.0), -8.0 * np.arange(1, 17, dtype=np.float32) / 16).astype(np.float32)
    sm = jnp.stack([sinks, jnp.broadcast_to(jnp.asarray(slopes), sinks.shape)], axis=1)
    per_layer = lambda t, n=1: t.reshape(L, n, D)
    g_mix3, g_mlp3, conv_b3, bg = per_layer(g_mix), per_layer(g_mlp), per_layer(conv_b), per_layer(b_gates, 2)

    xf = x.reshape(T, D)
    saved = []
    cur = xf
    for l in range(L):
        (gt, q, k, v, h), got = _inproj_fwd(cur, g_mix3, wint[l], l, tm_in, side=_Gather([sh_w2[l]]),
                                            relay_at=0.85)
        w2[l] = got[0].reshape(2, F, D)
        (att,), got = _attn_fwd(q, k, v, sm, l, S, side=_Gather([sh_w3[l]]), relay_at=0.5)
        w3[l] = got[0].reshape(3, D, D)
        (x1, mg, co, ya, yc, zb), _ = _mixer_fwd(cur, gt, att, w3[l], bg, cw, conv_b3, l, S, tm)
        head = (loss_target.reshape(T, D), g_final.reshape(1, D)) if l == L - 1 else None
        (nxt, h2, a, *acc_loss), got = _mlp_fwd(x1, g_mlp3, w2[l], l, tm_in, head=head,
                                                side=_Gather([sh_win[l + 1]]) if l + 1 < L else None, relay_at=0.7)
        if l + 1 < L:
            wint[l + 1] = got[0].reshape(NP, D)
        saved.append(dict(x=cur, gt=gt, q=q, k=k, v=v, h=h, att=att, x1=x1, mg=mg, co=co, ya=ya, yc=yc, zb=zb, h2=h2, a=a))
        cur = nxt
    dcur, acc_loss = cur, acc_loss[0]

    masters = {"w_in": (win_t, tr(m_w_in), tr(v_w_in)), "w_attn_out": (w_attn_out, m_w_attn_out, v_w_attn_out),
               "w_conv_out": (w_conv_out, m_w_conv_out, v_w_conv_out), "w_o": (w_o, m_w_o, v_w_o),
               "w_up": (w_up, m_w_up, v_w_up), "w_down": (w_down, m_w_down, v_w_down)}
    big = {name: None for name in masters}

    def adam(name, land, l):
        big[name] = _adam_sum(land, *masters[name], l, big[name], f"adam_{name}{l}", transposed=name == "w_up")

    acc_in, acc_mix, acc_mlp, dsink = [None] * L, [None] * L, [None] * L, [None] * L
    d_win_above = None
    for l in reversed(range(L)):
        sv = saved[l]
        side = _Exchange([blocks(d_win_above)]) if l + 1 < L else None
        (da, dx1, dx2b, acc_mlp[l]), got = _mlp_bwd(dcur, sv["x1"], sv["a"], g_mlp3, w2[l], l, tm_in, side=side)
        if l + 1 < L:
            adam("w_in", got[0], l + 1)
        d_wdn = _wgrad(sv["a"], dx2b, F, 0, None, f"wgrad_down{l}", relu2=True)
        d_wup = _wgrad(da, sv["h2"], F, 0, None, f"wgrad_up{l}")
        (dgt, datt, dya, dyc, dx1b, acc_mix[l]), got = _mixer_bwd(
            dx1, sv["gt"], sv["ya"], sv["yc"], sv["zb"], w3[l], bg, cw, l, S, tm,
            side=_Exchange([blocks(d_wdn)]))
        adam("w_down", got[0], l)
        d_wo, d_wao, d_wco = _wgrad_square(
            [(sv["mg"], dx1b, False), (sv["att"], dya, True), (sv["co"], dyc, False)], f"wgrad_mixer{l}")
        d_win = _wgrad(dgt, sv["h"], NP, NQ + 2 * NKV, None, f"wgrad_gates{l}")
        (dq, dk, dv, dsink[l]), got = _attn_bwd(
            sv["q"], sv["k"], sv["v"], sv["att"], datt, sm, l, S,
            side=_Exchange([blocks(d_wup), blocks(d_wo), blocks(d_wao), blocks(d_wco)]))
        for name, land in zip(["w_up", "w_o", "w_attn_out", "w_conv_out"], got):
            adam(name, land, l)
        d_win = _wgrad_qkv(dq, dk, dv, sv["h"], d_win, f"wgrad_qkv{l}")
        side = _Exchange([blocks(d_win)]) if l == 0 else None
        (dcur, acc_in[l]), got_in = _inproj_bwd(dgt, dq, dk, dv, sv["x"], dx1, g_mix3, wint[l], l, tm_in, side=side)
        d_win_above = d_win

    dsinks = jnp.stack([dsink[l][:, :, :, 0].sum(axis=0).reshape(16) for l in range(L)])
    small = _pack_small(
        jnp.concatenate([acc_in[l][0:1] for l in range(L)]),
        jnp.stack([acc_mix[l][0:2].reshape(2 * D) for l in range(L)]),
        dsinks,
        jnp.concatenate([acc_mix[l][2:3] for l in range(L)]),
        jnp.concatenate([acc_mlp[l][0:1] for l in range(L)]),
        acc_loss[0],
        jnp.concatenate([acc_mix[l][3:6] for l in range(L)]),
        acc_loss[1:2])
    adam("w_in", got_in[0], 0)
    (small_land,) = _remote_only(_Exchange([], [small]), "exchange_small")

    row = lambda t: t.reshape(1, D)
    small_out, loss = _adam_small(small_land, me.reshape(1).astype(jnp.int32), {
        "g_mix": (g_mix, m_g_mix, v_g_mix), "b_gates": (b_gates, m_b_gates, v_b_gates), "sinks": (sinks, m_sinks, v_sinks),
        "conv_b": (conv_b, m_conv_b, v_conv_b), "g_mlp": (g_mlp, m_g_mlp, v_g_mlp),
        "g_final": (row(g_final), row(m_g_final), row(v_g_final)), "conv_w": (conv_w, m_conv_w, v_conv_w)})
    small_out["g_final"] = [t.reshape(D) for t in small_out["g_final"]]
    big["w_in"] = [tr(o) for o in big["w_in"]]
    order = ["g_mix", "w_in", "b_gates", "sinks", "w_attn_out", "conv_w", "conv_b", "w_conv_out", "w_o", "g_mlp",
             "w_up", "w_down", "g_final"]
    out = [loss.reshape(()), dcur.reshape(nseq, S, D)]
    for kind in range(4):
        for name in order:
            out.append(big[name][kind] if name in big else small_out[name][kind])
    return tuple(out)
```

```python
import numpy as np
import jax
import jax.numpy as jnp
from jax import lax
from jax.experimental import pallas as pl
from jax.experimental.pallas import tpu as pltpu

D = 1024
NG = 5 * D
NQ = 1024
NKV = 256
NP = NQ + 2 * NKV + NG
F = 4096
HD = 64
GQ = 4
WIN = 128
L = 2
NDEV = 8
EPS = 1e-6
NEG = -1e30
SMALL_ROWS = 24
LR, B1, B2, AEPS, WD, STEP = 0.001, 0.9, 0.999, 1e-08, 0.01, 10

BF = jnp.bfloat16
F32 = jnp.float32
MESH = pl.DeviceIdType.MESH
VMEM_LIMIT = 60 * 1024 * 1024
STREAM_BUFFERS = 3
ANY = pl.BlockSpec(memory_space=pl.ANY)

NN = ((1,), (0,))
NT = ((1,), (1,))
TN = ((0,), (0,))


def _dot(a, b, dims):
    return lax.dot_general(a, b, (dims, ((), ())), preferred_element_type=F32)


def _resident(shape, imap):
    return pl.BlockSpec(shape, imap, pipeline_mode=pl.Buffered(1))


def _position():
    return lax.axis_index("x"), lax.axis_index("y"), lax.axis_index("c")


class _Gather:
    def __init__(self, shards):
        n = len(shards)
        self.inputs = list(shards)
        self.out_shape = [jax.ShapeDtypeStruct((s.shape[0], NDEV) + s.shape[1:], s.dtype) for s in shards]
        self.scratch = [pltpu.SemaphoreType.DMA((n, 7)), pltpu.SemaphoreType.DMA((n, 7)), pltpu.SemaphoreType.DMA((n,))]

    def _plan(self, src, dst, sems):
        send_sems, recv_sems, local_sems = sems
        n = len(src)
        x, y, c = _position()
        me, sibling = (x, y, c), (x, y, 1 - c)
        chips = [(1 - x, y), (x, 1 - y), (1 - x, 1 - y)]

        def rows(a, p):
            return dst[a].at[:, 4 * p[0] + 2 * p[1] + p[2]]

        def copy(a, k, block, to, from_src=False):
            return pltpu.make_async_remote_copy(
                src_ref=src[a] if from_src else rows(a, block), dst_ref=rows(a, block),
                send_sem=send_sems.at[a, k], recv_sem=recv_sems.at[a, k], device_id=to, device_id_type=MESH)

        mine = [pltpu.make_async_copy(src[a], rows(a, me), local_sems.at[a]) for a in range(n)]
        first = []
        for a in range(n):
            first.append(copy(a, 0, me, sibling, True))
            first += [copy(a, 1 + t, me, (*chip, c), True) for t, chip in enumerate(chips)]
        return n, c, me, sibling, chips, copy, mine, first

    def start(self, src, dst, sems):
        *_, mine, first = self._plan(src, dst, sems)
        for cp in mine + first:
            cp.start()

    def relay(self, src, dst, sems):
        n, c, me, sibling, chips, copy, _, _ = self._plan(src, dst, sems)
        for t, chip in enumerate(chips):
            for a in range(n):
                copy(a, 1 + t, (*chip, c), me).wait_recv()
                copy(a, 4 + t, (*chip, c), sibling).start()

    def finish(self, src, dst, sems):
        n, c, me, sibling, chips, copy, mine, first = self._plan(src, dst, sems)
        for a in range(n):
            copy(a, 0, sibling, me).wait_recv()
            for t, chip in enumerate(chips):
                copy(a, 4 + t, (*chip, 1 - c), me).wait_recv()
        for cp in first + [copy(a, 4 + t, (*chip, c), sibling) for t, chip in enumerate(chips) for a in range(n)]:
            cp.wait_send()
        for cp in mine:
            cp.wait()


class _Exchange:
    def __init__(self, grads, everyone=()):
        self.inputs = list(grads) + list(everyone)
        self.n_blocked = len(grads)
        n = len(self.inputs)
        self.out_shape = [jax.ShapeDtypeStruct(g.shape, g.dtype) for g in grads]
        self.out_shape += [jax.ShapeDtypeStruct((NDEV,) + e.shape, e.dtype) for e in everyone]
        self.scratch = [pltpu.SemaphoreType.DMA((n, 7)), pltpu.SemaphoreType.DMA((n, 7)), pltpu.SemaphoreType.DMA((n,))]

    def _plan(self, src, land, sems):
        send_sems, recv_sems, local_sems = sems
        x, y, c = _position()
        me = 4 * x + 2 * y + c

        def parts(peer_idx):
            return [(s.at[peer_idx] if a < self.n_blocked else s, land[a].at[me]) for a, s in enumerate(src)]

        local = [pltpu.make_async_copy(s, d, local_sems.at[a]) for a, (s, d) in enumerate(parts(me))]
        sent = []
        for rel in range(1, NDEV):
            px = 1 - x if rel & 4 else x
            py = 1 - y if rel & 2 else y
            pc = 1 - c if rel & 1 else c
            for a, (s, d) in enumerate(parts(4 * px + 2 * py + pc)):
                sent.append(pltpu.make_async_remote_copy(
                    src_ref=s, dst_ref=d, send_sem=send_sems.at[a, rel - 1], recv_sem=recv_sems.at[a, rel - 1],
                    device_id=(px, py, pc), device_id_type=MESH))
        return local, sent

    def start(self, src, land, sems):
        local, sent = self._plan(src, land, sems)
        for cp in local + sent:
            cp.start()

    def relay(self, src, land, sems):
        pass

    def finish(self, src, land, sems):
        local, sent = self._plan(src, land, sems)
        for cp in sent:
            cp.wait_recv()
        for cp in sent:
            cp.wait_send()
        for cp in local:
            cp.wait()


def _call(body, *, name, grid, in_specs, out_specs, out_shape, args, scratch_shapes=(), aliases=None, side=None,
          relay_at=1.0):
    sem = ("arbitrary",) * len(grid)
    if side is None:
        outs = pl.pallas_call(
            body, name=name, grid=grid, in_specs=in_specs, out_specs=out_specs, out_shape=out_shape,
            scratch_shapes=list(scratch_shapes), input_output_aliases=aliases or {},
            compiler_params=pltpu.CompilerParams(dimension_semantics=sem, vmem_limit_bytes=VMEM_LIMIT))(*args)
        return outs, []
    ni, no, ns = len(in_specs), len(out_specs), len(scratch_shapes)
    si, so = len(side.inputs), len(side.out_shape)

    def hosted(*refs):
        ins, refs = refs[:ni], refs[ni:]
        sins, refs = refs[:si], refs[si:]
        outs, refs = refs[:no], refs[no:]
        souts, refs = refs[:so], refs[so:]
        scr, sscr = refs[:ns], refs[ns:]
        step = pl.program_id(0)
        for d in range(1, len(grid)):
            step = step * grid[d] + pl.program_id(d)
        last = int(np.prod(grid)) - 1

        @pl.when(step == 0)
        def _():
            side.start(sins, souts, sscr)

        body(*ins, *outs, *scr)

        @pl.when(step == min(int(relay_at * last), last))
        def _():
            side.relay(sins, souts, sscr)

        @pl.when(step == last)
        def _():
            side.finish(sins, souts, sscr)

    outs = pl.pallas_call(
        hosted, name=name, grid=grid, in_specs=list(in_specs) + [ANY] * si, out_specs=list(out_specs) + [ANY] * so,
        out_shape=list(out_shape) + side.out_shape, scratch_shapes=list(scratch_shapes) + side.scratch,
        input_output_aliases=aliases or {},
        compiler_params=pltpu.CompilerParams(dimension_semantics=sem, vmem_limit_bytes=VMEM_LIMIT, has_side_effects=True),
    )(*args, *side.inputs)
    return outs[:no], outs[no:]


def _remote_only(side, name):
    n = len(side.inputs)

    def body(*refs):
        src, dst, sems = refs[:n], refs[n:n + len(side.out_shape)], refs[n + len(side.out_shape):]
        side.start(src, dst, sems)
        side.relay(src, dst, sems)
        side.finish(src, dst, sems)

    return pl.pallas_call(
        body, name=name, in_specs=[ANY] * n, out_specs=[ANY] * len(side.out_shape), out_shape=side.out_shape,
        scratch_shapes=side.scratch, compiler_params=pltpu.CompilerParams(has_side_effects=True))(*side.inputs)


def _rms(x, g):
    r = lax.rsqrt(jnp.mean(x * x, axis=-1, keepdims=True) + EPS)
    return x * r * g


def _rms_bwd(dy, x, g):
    r = lax.rsqrt(jnp.mean(x * x, axis=-1, keepdims=True) + EPS)
    xh = x * r
    dxh = dy * g
    dx = r * (dxh - xh * jnp.mean(dxh * xh, axis=-1, keepdims=True))
    return dx, jnp.sum(dy * xh, axis=0, keepdims=True)


def _halves(n):
    return [slice(0, n // 2), slice(n // 2, n)] if n % 256 == 0 else [slice(0, n)]


def _zero_at_first_step(acc_ref):
    first = pl.program_id(0) == 0

    @pl.when(first)
    def _():
        acc_ref[...] = jnp.zeros_like(acc_ref)


ROW1 = lambda: _resident((1, D), lambda i: (0, 0))
LROWS = lambda l, n=1: _resident((None, n, D), lambda *_: (l, 0, 0))
ACC = lambda: pl.BlockSpec((8, D), lambda i: (0, 0))


def _inproj_fwd(x, g, wint, l, tm, side=None, relay_at=1.0):
    T = x.shape[0]

    def body(x_ref, g_ref, w_ref, gt_ref, q_ref, k_ref, v_ref, h_ref):
        for rows in _halves(tm):
            h = _rms(x_ref[rows, :], g_ref[...]).astype(BF)
            h_ref[rows, :] = h
            q_ref[:, rows] = _dot(w_ref[0:NQ, :], h, NT).astype(BF)
            k_ref[:, rows] = _dot(w_ref[NQ:NQ + NKV, :], h, NT).astype(BF)
            v_ref[:, rows] = _dot(w_ref[NQ + NKV:NQ + 2 * NKV, :], h, NT).astype(BF)
            for s in range(5):
                lo = NQ + 2 * NKV + s * D
                gt_ref[rows, s * D:(s + 1) * D] = _dot(h, w_ref[lo:lo + D, :], NT).astype(BF)

    tok = lambda w: pl.BlockSpec((tm, w), lambda i: (i, 0))
    feat = lambda w: pl.BlockSpec((w, tm), lambda i: (0, i))
    return _call(
        body, name=f"inproj_fwd{l}", grid=(T // tm,),
        in_specs=[tok(D), LROWS(l), _resident((NP, D), lambda i: (0, 0))],
        out_specs=[tok(NG), feat(NQ), feat(NKV), feat(NKV), tok(D)],
        out_shape=[jax.ShapeDtypeStruct((T, NG), BF)] + [jax.ShapeDtypeStruct((w, T), BF) for w in (NQ, NKV, NKV)]
        + [jax.ShapeDtypeStruct((T, D), BF)],
        args=(x, g, wint), side=side, relay_at=relay_at)


def _band_geometry():
    j = lax.broadcasted_iota(jnp.int32, (2 * WIN, WIN), 0)
    r = lax.broadcasted_iota(jnp.int32, (2 * WIN, WIN), 1)
    dist = WIN + r - j
    dist0 = (r - j)[:WIN]
    return (dist.astype(F32), (dist >= 0) & (dist < WIN)), (dist0.astype(F32), dist0 >= 0)


def _pair_biases(sm_ref, pj):
    return [[jnp.where(ok, -sm_ref[1, pj * 2 * GQ + h] * dist, NEG) for h in range(2 * GQ)]
            for dist, ok in _band_geometry()]


def _reduce_rows(x, pair, whole):
    while x.shape[0] > 8:
        half = x.shape[0] // 2
        x = pair(x[:half], x[half:])
    return whole(x, axis=0, keepdims=True)


def _heads_on_lanes(ref, kvh, r0):
    return jnp.concatenate([ref[(kvh * GQ + g) * HD:(kvh * GQ + g + 1) * HD, pl.ds(r0, WIN)] for g in range(GQ)], axis=1)


def _band_probs(sm_ref, head0, q_ref, k_ref, r0, p0, kvh, biases):
    qt = _heads_on_lanes(q_ref, kvh, r0) * jnp.asarray(HD ** -0.5, BF)
    kt = k_ref[kvh * HD:(kvh + 1) * HD, pl.ds(p0, biases[0].shape[0])]
    heads = []
    for g in range(GQ):
        sink = sm_ref[0, head0 + g]
        s = _dot(kt, qt[:, g * WIN:(g + 1) * WIN], TN) + biases[kvh * GQ + g]
        m = jnp.maximum(_reduce_rows(s, jnp.maximum, jnp.max), sink)
        p = jnp.exp(s - m)
        ps = jnp.exp(sink - m)
        heads.append((p, ps, 1.0 / (_reduce_rows(p, jnp.add, jnp.sum) + ps)))
    return qt, kt, heads


def _attn_fwd(q, k, v, sm, l, S, side=None, relay_at=1.0):
    T = q.shape[1]
    nblk = S // WIN
    unroll = next(u for u in (15, 5, 3, 1) if (nblk - 1) % u == 0)

    def body(sm_all_ref, q_ref, k_ref, v_ref, o_ref):
        sm_ref = sm_all_ref.at[l]
        pj = pl.program_id(1)
        biases = _pair_biases(sm_ref, pj)

        def block(i, first):
            r0 = 0 if first else pl.multiple_of(i * WIN, WIN)
            p0 = 0 if first else pl.multiple_of(i * WIN - WIN, WIN)
            for kvh in range(2):
                head0 = (2 * pj + kvh) * GQ
                _, _, heads = _band_probs(sm_ref, head0, q_ref, k_ref, r0, p0, kvh, biases[first])
                vt = v_ref[kvh * HD:(kvh + 1) * HD, pl.ds(p0, WIN if first else 2 * WIN)]
                for g, (p, _, inv) in enumerate(heads):
                    rows = slice((kvh * GQ + g) * HD, (kvh * GQ + g + 1) * HD)
                    o_ref[rows, pl.ds(r0, WIN)] = (_dot(vt, p.astype(BF), NN) * inv).astype(BF)

        block(0, True)

        def rest(t, c):
            for u in range(unroll):
                block(1 + unroll * t + u, False)
            return c

        lax.fori_loop(0, (nblk - 1) // unroll, rest, 0)

    wide = lambda: pl.BlockSpec((2 * GQ * HD, S), lambda s, p: (p, s))
    narrow = lambda: pl.BlockSpec((2 * HD, S), lambda s, p: (p, s))
    return _call(
        body, name=f"attn_fwd{l}", grid=(T // S, 2),
        in_specs=[pl.BlockSpec(memory_space=pltpu.SMEM), wide(), narrow(), narrow()],
        out_specs=[wide()], out_shape=[jax.ShapeDtypeStruct((D, T), BF)],
        args=(sm, q, k, v), side=side, relay_at=relay_at)


def _shift_rows(y, k, edge_rows, down):
    n = y.shape[0]
    rid = lax.broadcasted_iota(jnp.int32, y.shape, 0)
    out = pltpu.roll(y, k if down else n - k, 0)
    for t, row in enumerate(edge_rows):
        out = jnp.where(rid == (t if down else n - k + t), row, out)
    return out


def _mixer_fwd(x, gt, att, w3, bg, cw, cbias, l, S, tm):
    T = x.shape[0]

    nt = T // tm

    def body(x_ref, gt_ref, cch_ref, cuh_ref, att_ref, wao_ref, wco_ref, wo_ref,
             bg_ref, cw_ref, cbias_ref, x1_ref, mg_ref, co_ref, ya_ref, yc_ref, zb_ref, ring_ref, ring_sem):
        i = pl.program_id(0)

        def fetch(t):
            slot = t % STREAM_BUFFERS
            row0 = t * tm if isinstance(t, int) else pl.multiple_of(t * tm, tm)
            return pltpu.make_async_copy(gt_ref.at[pl.ds(row0, tm), :], ring_ref.at[slot], ring_sem.at[slot])

        @pl.when(i == 0)
        def _():
            for t in range(min(STREAM_BUFFERS - 1, nt)):
                fetch(t).start()

        @pl.when(i + STREAM_BUFFERS - 1 < nt)
        def _():
            fetch(i + STREAM_BUFFERS - 1).start()

        fetch(i).wait()
        tile = ring_ref.at[i % STREAM_BUFFERS]
        cb_ref, cc_ref, cu_ref, ga_ref, gc_ref = (tile.at[:, s * D:(s + 1) * D] for s in range(5))
        first = (i * tm) % S == 0
        y = cc_ref[...].astype(F32) * cu_ref[...].astype(F32)
        hy1 = cch_ref[15:16, :].astype(F32) * cuh_ref[15:16, :].astype(F32)
        hy2 = cch_ref[14:15, :].astype(F32) * cuh_ref[14:15, :].astype(F32)
        hy1, hy2 = jnp.where(first, 0.0, hy1), jnp.where(first, 0.0, hy2)
        z = (cw_ref[0:1, :] * _shift_rows(y, 2, [hy2, hy1], True) + cw_ref[1:2, :] * _shift_rows(y, 1, [hy1], True)
             + cw_ref[2:3, :] * y)
        zb = z + cbias_ref[...]
        zb_ref[...] = zb.astype(BF)
        co = (cb_ref[...].astype(F32) * zb).astype(BF)
        co_ref[...] = co
        yc = _dot(co, wco_ref[...], NN)
        ya = _dot(att_ref[...], wao_ref[...], TN)
        ya_ref[...] = ya.astype(BF)
        yc_ref[...] = yc.astype(BF)
        sa = jax.nn.sigmoid(ga_ref[...].astype(F32) + bg_ref[0:1, :])
        sc = jax.nn.sigmoid(gc_ref[...].astype(F32) + bg_ref[1:2, :])
        mg = (sa * ya + sc * yc).astype(BF)
        mg_ref[...] = mg
        x1_ref[...] = x_ref[...] + _dot(mg, wo_ref[...], NN)

    tok = lambda: pl.BlockSpec((tm, D), lambda i: (i, 0))
    halo = lambda s: pl.BlockSpec((16, D), lambda i: (jnp.maximum(i * (tm // 16) - 1, 0), s))
    wsp = lambda k: _resident((None, D, D), lambda i: (k, 0, 0))
    return _call(
        body, name=f"mixer_fwd{l}", grid=(nt,),
        in_specs=[tok(), ANY, halo(1), halo(2), pl.BlockSpec((D, tm), lambda i: (0, i)),
                  wsp(0), wsp(1), wsp(2), LROWS(l, 2), LROWS(l, 8), LROWS(l)],
        out_specs=[tok()] * 6,
        out_shape=[jax.ShapeDtypeStruct((T, D), dt) for dt in (F32, BF, BF, BF, BF, BF)],
        scratch_shapes=[pltpu.VMEM((STREAM_BUFFERS, tm, NG), BF), pltpu.SemaphoreType.DMA((STREAM_BUFFERS,))],
        args=(x, gt, gt, gt, att, w3, w3, w3, bg, cw, cbias))


def _mlp_fwd(x1, g, w2, l, tm, head=None, side=None, relay_at=1.0):
    T = x1.shape[0]
    nt = T // tm
    FC = 1024

    def body(x_ref, g_ref, wup_ref, wdn_ref, *rest):
        out_ref, h_ref, a_ref = rest[-4:-1] if head else rest
        if head:
            t_ref, gf_ref, acc_ref = rest[0], rest[1], rest[-1]
            _zero_at_first_step(acc_ref)
        for rows in _halves(tm):
            x = x_ref[rows, :]
            h = _rms(x, g_ref[...]).astype(BF)
            h_ref[rows, :] = h
            acc = x
            for c in range(F // FC):
                a = _dot(h, wup_ref[c * FC:(c + 1) * FC, :], NT)
                a_ref[rows, c * FC:(c + 1) * FC] = a.astype(BF)
                u = jnp.maximum(a, 0.0)
                acc = acc + _dot((u * u).astype(BF), wdn_ref[c * FC:(c + 1) * FC, :], NN)
            if not head:
                out_ref[rows, :] = acc
                continue
            gf = gf_ref[...]
            err = _rms(acc, gf) - t_ref[rows, :]
            out_ref[rows, :], dg = _rms_bwd(err * (1.0 / D), acc, gf)
            acc_ref[0:1, :] += dg
            acc_ref[1:2, :] += jnp.sum(err * err, axis=0, keepdims=True)
        if not head:
            return

        @pl.when(pl.program_id(0) == nt - 1)
        def _():
            acc_ref[1:2, :] = jnp.zeros((1, D), F32) + (0.5 / D) * jnp.sum(acc_ref[1:2, :])

    tok = lambda w: pl.BlockSpec((tm, w), lambda i: (i, 0))
    wsp = lambda k: _resident((None, F, D), lambda i: (k, 0, 0))
    return _call(
        body, name=f"mlp_fwd{l}", grid=(nt,),
        in_specs=[tok(D), LROWS(l), wsp(0), wsp(1)] + ([tok(D), ROW1()] if head else []),
        out_specs=[tok(D), tok(D), tok(F)] + ([ACC()] if head else []),
        out_shape=[jax.ShapeDtypeStruct((T, D), F32), jax.ShapeDtypeStruct((T, D), BF), jax.ShapeDtypeStruct((T, F), BF)]
        + ([jax.ShapeDtypeStruct((8, D), F32)] if head else []),
        args=(x1, g, w2, w2) + (tuple(head) if head else ()), side=side, relay_at=relay_at)


def _mlp_bwd(dx2, x1, a, g, w2, l, tm, side=None):
    T = dx2.shape[0]
    FC = 1024

    def body(d_ref, x_ref, a_ref, g_ref, wup_ref, wdn_ref, da_ref, dx1_ref, db_ref, acc_ref):
        _zero_at_first_step(acc_ref)
        for rows in _halves(tm):
            d = d_ref[rows, :]
            db = d.astype(BF)
            db_ref[rows, :] = db
            dh = jnp.zeros(d.shape, F32)
            for c in range(F // FC):
                du = _dot(db, wdn_ref[c * FC:(c + 1) * FC, :], NT)
                da = (du * (2.0 * jnp.maximum(a_ref[rows, c * FC:(c + 1) * FC].astype(F32), 0.0))).astype(BF)
                da_ref[rows, c * FC:(c + 1) * FC] = da
                dh = dh + _dot(da, wup_ref[c * FC:(c + 1) * FC, :], NN)
            dx, dg = _rms_bwd(dh, x_ref[rows, :], g_ref[...])
            dx1_ref[rows, :] = d + dx
            acc_ref[0:1, :] += dg

    tok = lambda w: pl.BlockSpec((tm, w), lambda i: (i, 0))
    wsp = lambda k: _resident((None, F, D), lambda i: (k, 0, 0))
    return _call(
        body, name=f"mlp_bwd{l}", grid=(T // tm,),
        in_specs=[tok(D), tok(D), tok(F), LROWS(l), wsp(0), wsp(1)],
        out_specs=[tok(F), tok(D), tok(D), ACC()],
        out_shape=[jax.ShapeDtypeStruct((T, F), BF), jax.ShapeDtypeStruct((T, D), F32),
                   jax.ShapeDtypeStruct((T, D), BF), jax.ShapeDtypeStruct((8, D), F32)],
        args=(dx2, x1, a, g, w2, w2), side=side)


def _mixer_bwd(dx1, gt, ya, yc, zb, w3, bg, cw, l, S, tm, side=None):
    T = dx1.shape[0]
    nt = T // tm

    def body(d_ref, cb_ref, cc_ref, cu_ref, ga_ref, gc_ref, ya_ref, yc_ref, zb_ref, wao_ref, wco_ref, wo_ref,
             bg_ref, cw_ref, dg_ref, datt_ref, dya_ref, dyc_ref, db_ref, acc_ref, carry_ref):
        ti = nt - 1 - pl.program_id(0)
        _zero_at_first_step(acc_ref)

        @pl.when(((ti + 1) * tm) % S == 0)
        def _():
            carry_ref[...] = jnp.zeros_like(carry_ref)

        db = d_ref[...].astype(BF)
        db_ref[...] = db
        dm = _dot(db, wo_ref[...], NT)
        sa = jax.nn.sigmoid(ga_ref[...].astype(F32) + bg_ref[0:1, :])
        sc = jax.nn.sigmoid(gc_ref[...].astype(F32) + bg_ref[1:2, :])
        dya32 = dm * sa
        dyc32 = dm * sc
        dya = dya32.astype(BF)
        dyc = dyc32.astype(BF)
        dya_ref[...] = dya
        dyc_ref[...] = dyc
        dga = dya32 * ya_ref[...].astype(F32) * (1.0 - sa)
        dgc = dyc32 * yc_ref[...].astype(F32) * (1.0 - sc)
        dg_ref[:, 3 * D:4 * D] = dga.astype(BF)
        dg_ref[:, 4 * D:5 * D] = dgc.astype(BF)
        acc_ref[0:1, :] += jnp.sum(dga, axis=0, keepdims=True)
        acc_ref[1:2, :] += jnp.sum(dgc, axis=0, keepdims=True)
        datt_ref[...] = _dot(wao_ref[...], dya, NT).astype(BF)
        dco = _dot(dyc, wco_ref[...], NT)

        cc = cc_ref[...].astype(F32)
        cu = cu_ref[...].astype(F32)
        y = cc * cu
        dg_ref[:, 0:D] = (dco * zb_ref[...].astype(F32)).astype(BF)
        dz = dco * cb_ref[...].astype(F32)
        u1 = _shift_rows(dz, 1, [carry_ref[0:1, :]], False)
        u2 = _shift_rows(dz, 2, [carry_ref[0:1, :], carry_ref[1:2, :]], False)
        acc_ref[2:3, :] += jnp.sum(dz, axis=0, keepdims=True)
        acc_ref[3:4, :] += jnp.sum(u2 * y, axis=0, keepdims=True)
        acc_ref[4:5, :] += jnp.sum(u1 * y, axis=0, keepdims=True)
        acc_ref[5:6, :] += jnp.sum(dz * y, axis=0, keepdims=True)
        dy = cw_ref[2:3, :] * dz + cw_ref[1:2, :] * u1 + cw_ref[0:1, :] * u2
        dg_ref[:, D:2 * D] = (dy * cu).astype(BF)
        dg_ref[:, 2 * D:3 * D] = (dy * cc).astype(BF)
        carry_ref[...] = dz[0:8, :]

    tok = lambda w=D: pl.BlockSpec((tm, w), lambda i: (nt - 1 - i, 0))
    seg = lambda s: pl.BlockSpec((tm, D), lambda i: (nt - 1 - i, s))
    wsp = lambda k: _resident((None, D, D), lambda i: (k, 0, 0))
    return _call(
        body, name=f"mixer_bwd{l}", grid=(nt,),
        in_specs=[tok(), seg(0), seg(1), seg(2), seg(3), seg(4), tok(), tok(), tok(), wsp(0), wsp(1), wsp(2),
                  LROWS(l, 2), LROWS(l, 8)],
        out_specs=[tok(NG), pl.BlockSpec((D, tm), lambda i: (0, nt - 1 - i)), tok(), tok(), tok(), ACC()],
        out_shape=[jax.ShapeDtypeStruct((T, NG), BF), jax.ShapeDtypeStruct((D, T), BF)]
        + [jax.ShapeDtypeStruct((T, D), BF)] * 3 + [jax.ShapeDtypeStruct((8, D), F32)],
        scratch_shapes=[pltpu.VMEM((8, D), F32)],
        args=(dx1, gt, gt, gt, gt, gt, ya, yc, zb, w3, w3, w3, bg, cw), side=side)


def _attn_bwd(q, k, v, att, datt, sm, l, S, side=None):
    T = q.shape[1]
    nblk = S // WIN
    unroll = next(u for u in (15, 5, 3, 1) if (nblk - 1) % u == 0)
    scale = HD ** -0.5

    def body(sm_all_ref, q_ref, k_ref, v_ref, o_ref, do_ref, dq_ref, dk_ref, dv_ref, ds_ref, dka_ref, dva_ref):
        sm_ref = sm_all_ref.at[l]
        pj = pl.program_id(1)
        biases = _pair_biases(sm_ref, pj)
        dka_ref[...] = jnp.zeros_like(dka_ref)
        dva_ref[...] = jnp.zeros_like(dva_ref)

        def block(i, first, dsinks):
            r0 = 0 if first else pl.multiple_of(i * WIN, WIN)
            p0 = 0 if first else pl.multiple_of(i * WIN - WIN, WIN)
            out = []
            for kvh in range(2):
                head0 = (2 * pj + kvh) * GQ
                rows = slice(kvh * HD, (kvh + 1) * HD)
                dob = _heads_on_lanes(do_ref, kvh, r0)
                band = pl.ds(p0, WIN if first else 2 * WIN)
                vt = v_ref[rows, band]
                dpts = [_dot(vt, dob[:, g * WIN:(g + 1) * WIN], TN) for g in range(GQ)]
                qt, kt, heads = _band_probs(sm_ref, head0, q_ref, k_ref, r0, p0, kvh, biases[first])
                inv = jnp.concatenate([h[2] for h in heads], axis=1)
                do32 = dob.astype(F32)
                delta = jnp.sum(do32 * _heads_on_lanes(o_ref, kvh, r0).astype(F32), axis=0, keepdims=True) * inv
                dosb = (do32 * inv).astype(BF)
                dsts = [(p * (dpts[g] * h_inv - delta[:, g * WIN:(g + 1) * WIN])).astype(BF)
                        for g, (p, _, h_inv) in enumerate(heads)]
                for g in range(GQ):
                    hr = slice((kvh * GQ + g) * HD, (kvh * GQ + g + 1) * HD)
                    dq_ref[hr, pl.ds(r0, WIN)] = (_dot(kt, dsts[g], NN) * scale).astype(BF)
                lanes = lambda g: slice(g * WIN, (g + 1) * WIN)
                dka_ref[rows, band] += sum(_dot(qt[:, lanes(g)], dsts[g], NT) for g in range(GQ))
                dva_ref[rows, band] += sum(
                    _dot(dosb[:, lanes(g)], p.astype(BF), NT) for g, (p, _, _) in enumerate(heads))
                ps = jnp.concatenate([h[1] for h in heads], axis=1)
                out.append(dsinks[kvh] - ps * delta)
            return tuple(out)

        zero = jnp.zeros((1, GQ * WIN), F32)
        def rest(t, c):
            for u in range(unroll):
                c = block(1 + unroll * t + u, False, c)
            return c

        dsinks = lax.fori_loop(0, (nblk - 1) // unroll, rest, block(0, True, (zero, zero)))
        for kvh in range(2):
            for g in range(GQ):
                tot = jnp.sum(dsinks[kvh][:, g * WIN:(g + 1) * WIN])
                ds_ref[kvh * GQ + g:kvh * GQ + g + 1, :] = jnp.zeros((1, 128), F32) + tot
        dk_ref[...] = dka_ref[...].astype(BF)
        dv_ref[...] = dva_ref[...].astype(BF)

    wide = lambda: pl.BlockSpec((2 * GQ * HD, S), lambda s, p: (p, s))
    narrow = lambda: pl.BlockSpec((2 * HD, S), lambda s, p: (p, s))
    return _call(
        body, name=f"attn_bwd{l}", grid=(T // S, 2),
        in_specs=[pl.BlockSpec(memory_space=pltpu.SMEM), wide(), narrow(), narrow(), wide(), wide()],
        out_specs=[wide(), narrow(), narrow(), pl.BlockSpec((None, None, 8, 128), lambda s, p: (s, p, 0, 0))],
        out_shape=[jax.ShapeDtypeStruct((NQ, T), BF), jax.ShapeDtypeStruct((NKV, T), BF),
                   jax.ShapeDtypeStruct((NKV, T), BF), jax.ShapeDtypeStruct((T // S, 2, 8, 128), F32)],
        scratch_shapes=[pltpu.VMEM((2 * HD, S), F32), pltpu.VMEM((2 * HD, S), F32)],
        args=(sm, q, k, v, att, datt), side=side)


def _inproj_bwd(dgt, dq, dk, dv, x, dres, g, wint, l, tm, side=None):
    T = x.shape[0]

    def body(dg_ref, dq_ref, dk_ref, dv_ref, x_ref, dr_ref, g_ref, w_ref, dx_ref, acc_ref):
        _zero_at_first_step(acc_ref)
        for rows in _halves(tm):
            dh = _dot(dq_ref[:, rows], w_ref[0:NQ, :], TN)
            dh = dh + _dot(dk_ref[:, rows], w_ref[NQ:NQ + NKV, :], TN)
            dh = dh + _dot(dv_ref[:, rows], w_ref[NQ + NKV:NQ + 2 * NKV, :], TN)
            dh = dh + _dot(dg_ref[rows, :], w_ref[NQ + 2 * NKV:NP, :], NN)
            dx, dg = _rms_bwd(dh, x_ref[rows, :], g_ref[...])
            dx_ref[rows, :] = dr_ref[rows, :] + dx
            acc_ref[0:1, :] += dg

    tok = lambda w: pl.BlockSpec((tm, w), lambda i: (i, 0))
    feat = lambda w: pl.BlockSpec((w, tm), lambda i: (0, i))
    return _call(
        body, name=f"inproj_bwd{l}", grid=(T // tm,),
        in_specs=[tok(NG), feat(NQ), feat(NKV), feat(NKV), tok(D), tok(D), LROWS(l), _resident((NP, D), lambda i: (0, 0))],
        out_specs=[tok(D), ACC()],
        out_shape=[jax.ShapeDtypeStruct((T, D), F32), jax.ShapeDtypeStruct((8, D), F32)],
        args=(dgt, dq, dk, dv, x, dres, g, wint), side=side)


def _wgrad(a, b, rows, row0, into, name, relu2=False):
    T, M = a.shape
    tmm = next(t for t in (1024, 512, 256) if M % t == 0 and row0 % t == 0)
    tk = min(4096 if M > 4096 else 2048, T)
    nk = T // tk
    blk0 = row0 // tmm

    def body(*refs):
        a_ref, b_ref = refs[0], refs[1]
        o_ref, acc_ref = refs[-2], refs[-1]
        kk = pl.program_id(1)

        @pl.when(kk == 0)
        def _():
            acc_ref[...] = jnp.zeros_like(acc_ref)

        av = a_ref[...]
        if relu2:
            t = jnp.maximum(av.astype(F32), 0.0)
            av = (t * t).astype(BF)
        acc_ref[...] += _dot(av, b_ref[...], TN)

        @pl.when(kk == nk - 1)
        def _():
            o_ref[...] = acc_ref[...].astype(BF)

    in_specs = [pl.BlockSpec((tk, tmm), lambda j, kk: (kk, j)), pl.BlockSpec((tk, D), lambda j, kk: (kk, 0))]
    args = [a, b]
    if into is not None:
        in_specs.append(ANY)
        args.append(into)
    (out,), _ = _call(
        body, name=name, grid=(M // tmm, nk), in_specs=in_specs,
        out_specs=[pl.BlockSpec((tmm, D), lambda j, kk: (blk0 + j, 0))],
        out_shape=[jax.ShapeDtypeStruct((rows, D), BF)], scratch_shapes=[pltpu.VMEM((tmm, D), F32)],
        aliases={2: 0} if into is not None else None, args=args)
    return out


def _wgrad_qkv(dq, dk, dv, h, into, name):
    T = h.shape[0]
    rows = NQ + 2 * NKV
    tk = min(2048, T)
    nk = T // tk

    def body(dq_ref, dk_ref, dv_ref, h_ref, _, o_ref, acc_ref):
        kk = pl.program_id(0)

        @pl.when(kk == 0)
        def _():
            acc_ref[...] = jnp.zeros_like(acc_ref)

        hv = h_ref[...]
        acc_ref[0:NQ, :] += _dot(dq_ref[...], hv, NN)
        acc_ref[NQ:NQ + NKV, :] += _dot(dk_ref[...], hv, NN)
        acc_ref[NQ + NKV:rows, :] += _dot(dv_ref[...], hv, NN)

        @pl.when(kk == nk - 1)
        def _():
            o_ref[...] = acc_ref[...].astype(BF)

    feat = lambda w: pl.BlockSpec((w, tk), lambda kk: (0, kk))
    (out,), _ = _call(
        body, name=name, grid=(nk,),
        in_specs=[feat(NQ), feat(NKV), feat(NKV), pl.BlockSpec((tk, D), lambda kk: (kk, 0)), ANY],
        out_specs=[pl.BlockSpec((rows, D), lambda kk: (0, 0))],
        out_shape=[jax.ShapeDtypeStruct((NP, D), BF)], scratch_shapes=[pltpu.VMEM((rows, D), F32)],
        aliases={4: 0}, args=[dq, dk, dv, h, into])
    return out


def _wgrad_square(jobs, name):
    n = len(jobs)
    T = jobs[0][1].shape[0]
    tk = min(1024, T)
    nk = T // tk

    def body(*refs):
        ins, outs, acc_ref = refs[:2 * n], refs[2 * n:3 * n], refs[3 * n]
        j, kk = pl.program_id(0), pl.program_id(1)

        @pl.when(kk == 0)
        def _():
            acc_ref[...] = jnp.zeros_like(acc_ref)

        for p, (_, _, a_t) in enumerate(jobs):
            @pl.when(j == p)
            def _(p=p, a_t=a_t):
                acc_ref[...] += _dot(ins[2 * p][...], ins[2 * p + 1][...], NN if a_t else TN)

                @pl.when(kk == nk - 1)
                def _():
                    outs[p][...] = acc_ref[...].astype(BF)

    def step(p):
        return lambda j, kk: jnp.where(j == p, kk, jnp.where(j > p, nk - 1, 0))

    in_specs, args = [], []
    for p, (a, b, a_t) in enumerate(jobs):
        at = step(p)
        in_specs.append(pl.BlockSpec((D, tk), lambda j, kk, at=at: (0, at(j, kk))) if a_t
                        else pl.BlockSpec((tk, D), lambda j, kk, at=at: (at(j, kk), 0)))
        in_specs.append(pl.BlockSpec((tk, D), lambda j, kk, at=at: (at(j, kk), 0)))
        args += [a, b]
    outs, _ = _call(
        body, name=name, grid=(n, nk), in_specs=in_specs,
        out_specs=[pl.BlockSpec((D, D), lambda j, kk: (0, 0))] * n,
        out_shape=[jax.ShapeDtypeStruct((D, D), BF)] * n, scratch_shapes=[pltpu.VMEM((D, D), F32)], args=args)
    return outs


def _adamw(w, g, m, v):
    m = B1 * m + (1.0 - B1) * g
    v = B2 * v + (1.0 - B2) * (g * g)
    m_hat = m / (1.0 - B1 ** STEP)
    v_hat = v / (1.0 - B2 ** STEP)
    return -LR * (m_hat / (jnp.sqrt(v_hat) + AEPS) + WD * w), m, v


def _adam_sum(land, w, m, v, l, into, name, transposed=False):
    _, r, _ = land.shape
    tr = 208 if r % 208 == 0 else (256 if r % 256 == 0 else r)
    tc = 256

    def body(land_ref, w_ref, m_ref, v_ref, *rest):
        g_ref, d_ref, nm_ref, nv_ref = rest[-4:]
        g = land_ref[0].astype(F32)
        for s in range(1, NDEV):
            g = g + land_ref[s].astype(F32)
        if transposed:
            g = g.T
        g_ref[...] = g
        d_ref[...], nm_ref[...], nv_ref[...] = _adamw(w_ref[...], g, m_ref[...], v_ref[...])

    if transposed:
        blk = lambda: pl.BlockSpec((None, tc, r), lambda j: (l, j, 0))
        land_spec, grid = pl.BlockSpec((NDEV, r, tc), lambda j: (0, 0, j)), (D // tc,)
    else:
        blk = lambda: pl.BlockSpec((None, tr, D), lambda j: (l, j, 0))
        land_spec, grid = pl.BlockSpec((NDEV, tr, D), lambda j: (0, j, 0)), (r // tr,)
    in_specs = [land_spec, blk(), blk(), blk()]
    args = [land, w, m, v]
    aliases = None
    if into is not None:
        in_specs += [ANY] * 4
        args += list(into)
        aliases = {4 + t: t for t in range(4)}
    outs, _ = _call(body, name=name, grid=grid, in_specs=in_specs, out_specs=[blk()] * 4,
                    out_shape=[jax.ShapeDtypeStruct(w.shape, F32)] * 4, aliases=aliases, args=args)
    return outs


SMALL_NAMES = ("g_mix", "b_gates", "sinks", "conv_b", "g_mlp", "g_final", "conv_w")


def _adam_small(land, me, masters):
    n = len(SMALL_NAMES)
    lanes = D // NDEV

    def body(land_ref, me_ref, *refs):
        ins, outs, loss_ref, gs_ref = refs[:3 * n], refs[3 * n:7 * n], refs[7 * n], refs[7 * n + 1]
        g = land_ref[0]
        for s in range(1, NDEV):
            g = g + land_ref[s]
        gs_ref[...] = g

        def update(k, g_piece, idx):
            w_ref, m_ref, v_ref = ins[3 * k:3 * k + 3]
            outs[4 * k][idx] = g_piece
            outs[4 * k + 1][idx], outs[4 * k + 2][idx], outs[4 * k + 3][idx] = _adamw(w_ref[idx], g_piece, m_ref[idx], v_ref[idx])

        whole = (slice(None), slice(None))
        update(0, gs_ref[0:2, :], whole)
        for l in range(L):
            for h in range(2):
                update(1, gs_ref[2 + 2 * l + h:3 + 2 * l + h, :], (slice(l, l + 1), slice(h * D, (h + 1) * D)))
            update(2, gs_ref[6:7, 16 * l:16 * (l + 1)], (slice(l, l + 1), slice(None)))
        update(3, gs_ref[7:9, :], whole)
        update(4, gs_ref[9:11, :], whole)
        update(5, gs_ref[11:12, :], whole)
        mine = pl.ds(pl.multiple_of(me_ref[0] * lanes, lanes), lanes)
        for l in range(L):
            for k in range(3):
                update(6, gs_ref[12 + 3 * l + k:13 + 3 * l + k, mine], (l, slice(k, k + 1), slice(None)))
        loss_ref[...] = gs_ref[18:19, 0:1]

    flat = [t for name in SMALL_NAMES for t in masters[name]]
    vmem = pl.BlockSpec(memory_space=pltpu.VMEM)
    outs = pl.pallas_call(
        body, name="adam_small",
        in_specs=[vmem, pl.BlockSpec(memory_space=pltpu.SMEM)] + [vmem] * len(flat),
        out_shape=[jax.ShapeDtypeStruct(masters[name][0].shape, F32) for name in SMALL_NAMES for _ in range(4)]
        + [jax.ShapeDtypeStruct((1, 1), F32)],
        scratch_shapes=[pltpu.VMEM((SMALL_ROWS, D), F32)],
        compiler_params=pltpu.CompilerParams(vmem_limit_bytes=VMEM_LIMIT))(land, me, *flat)
    return {name: outs[4 * k:4 * k + 4] for k, name in enumerate(SMALL_NAMES)}, outs[-1]


def _pack_small(g_mix, b_gates, sinks, conv_b, g_mlp, g_final, conv_w_rows, extra):
    sink_row = jnp.zeros((1, D), F32).at[0, :2 * 16].set(sinks.reshape(-1))
    return jnp.concatenate([g_mix, b_gates.reshape(4, D), sink_row, conv_b, g_mlp, g_final.reshape(1, D),
                            conv_w_rows, extra, jnp.zeros((SMALL_ROWS - 19, D), F32)], axis=0)


def kernel(x, g_mix, w_in, b_gates, sinks, w_attn_out, conv_w, conv_b, w_conv_out, w_o, g_mlp, w_up, w_down, g_final, loss_target, m_g_mix, m_w_in, m_b_gates, m_sinks, m_w_attn_out, m_conv_w, m_conv_b, m_w_conv_out, m_w_o, m_g_mlp, m_w_up, m_w_down, m_g_final, v_g_mix, v_w_in, v_b_gates, v_sinks, v_w_attn_out, v_conv_w, v_conv_b, v_w_conv_out, v_w_o, v_g_mlp, v_w_up, v_w_down, v_g_final):
    nseq, S, _ = x.shape
    T = nseq * S
    tm_in = min(512, S)
    tm = min(512, S)
    xi, yi, ci = _position()
    me = 4 * xi + 2 * yi + ci
    tr = lambda t: jnp.swapaxes(t, 1, 2)
    blocks = lambda t: t.reshape(NDEV, t.shape[0] // NDEV, D)

    win_t, wup_t = tr(w_in), tr(w_up)
    sh_win = [win_t[l].astype(BF)[None] for l in range(L)]
    sh_w3 = [jnp.stack([w_attn_out[l], w_conv_out[l], w_o[l]]).astype(BF) for l in range(L)]
    sh_w2 = [jnp.stack([wup_t[l], w_down[l]]).astype(BF) for l in range(L)]
    wint, w3, w2 = [None] * L, [None] * L, [None] * L
    wint0_g, cw_g = _remote_only(_Gather([sh_win[0], jnp.pad(conv_w, ((0, 0), (0, 5), (0, 0)))]), "gather_first")
    wint[0] = wint0_g.reshape(NP, D)
    cw = jnp.swapaxes(cw_g, 1, 2).reshape(L, 8, D)
    slopes = np.power(np.float32(2.0), -8.0 * np.arange(1, 17, dtype=np.float32) / 16).astype(np.float32)
    sm = jnp.stack([sinks, jnp.broadcast_to(jnp.asarray(slopes), sinks.shape)], axis=1)
    per_layer = lambda t, n=1: t.reshape(L, n, D)
    g_mix3, g_mlp3, conv_b3, bg = per_layer(g_mix), per_layer(g_mlp), per_layer(conv_b), per_layer(b_gates, 2)

    xf = x.reshape(T, D)
    saved = []
    cur = xf
    for l in range(L):
        (gt, q, k, v, h), got = _inproj_fwd(cur, g_mix3, wint[l], l, tm_in, side=_Gather([sh_w2[l]]),
                                            relay_at=0.85)
        w2[l] = got[0].reshape(2, F, D)
        (att,), got = _attn_fwd(q, k, v, sm, l, S, side=_Gather([sh_w3[l]]), relay_at=0.5)
        w3[l] = got[0].reshape(3, D, D)
        (x1, mg, co, ya, yc, zb), _ = _mixer_fwd(cur, gt, att, w3[l], bg, cw, conv_b3, l, S, tm)
        head = (loss_target.reshape(T, D), g_final.reshape(1, D)) if l == L - 1 else None
        (nxt, h2, a, *acc_loss), got = _mlp_fwd(x1, g_mlp3, w2[l], l, tm_in, head=head,
                                                side=_Gather([sh_win[l + 1]]) if l + 1 < L else None, relay_at=0.7)
        if l + 1 < L:
            wint[l + 1] = got[0].reshape(NP, D)
        saved.append(dict(x=cur, gt=gt, q=q, k=k, v=v, h=h, att=att, x1=x1, mg=mg, co=co, ya=ya, yc=yc, zb=zb, h2=h2, a=a))
        cur = nxt
    dcur, acc_loss = cur, acc_loss[0]

    masters = {"w_in": (win_t, tr(m_w_in), tr(v_w_in)), "w_attn_out": (w_attn_out, m_w_attn_out, v_w_attn_out),
               "w_conv_out": (w_conv_out, m_w_conv_out, v_w_conv_out), "w_o": (w_o, m_w_o, v_w_o),
               "w_up": (w_up, m_w_up, v_w_up), "w_down": (w_down, m_w_down, v_w_down)}
    big = {name: None for name in masters}

    def adam(name, land, l):
        big[name] = _adam_sum(land, *masters[name], l, big[name], f"adam_{name}{l}", transposed=name == "w_up")

    acc_in, acc_mix, acc_mlp, dsink = [None] * L, [None] * L, [None] * L, [None] * L
    d_win_above = None
    for l in reversed(range(L)):
        sv = saved[l]
        side = _Exchange([blocks(d_win_above)]) if l + 1 < L else None
        (da, dx1, dx2b, acc_mlp[l]), got = _mlp_bwd(dcur, sv["x1"], sv["a"], g_mlp3, w2[l], l, tm_in, side=side)
        if l + 1 < L:
            adam("w_in", got[0], l + 1)
        d_wdn = _wgrad(sv["a"], dx2b, F, 0, None, f"wgrad_down{l}", relu2=True)
        d_wup = _wgrad(da, sv["h2"], F, 0, None, f"wgrad_up{l}")
        (dgt, datt, dya, dyc, dx1b, acc_mix[l]), got = _mixer_bwd(
            dx1, sv["gt"], sv["ya"], sv["yc"], sv["zb"], w3[l], bg, cw, l, S, tm,
            side=_Exchange([blocks(d_wdn)]))
        adam("w_down", got[0], l)
        d_wo, d_wao, d_wco = _wgrad_square(
            [(sv["mg"], dx1b, False), (sv["att"], dya, True), (sv["co"], dyc, False)], f"wgrad_mixer{l}")
        d_win = _wgrad(dgt, sv["h"], NP, NQ + 2 * NKV, None, f"wgrad_gates{l}")
        (dq, dk, dv, dsink[l]), got = _attn_bwd(
            sv["q"], sv["k"], sv["v"], sv["att"], datt, sm, l, S,
            side=_Exchange([blocks(d_wup), blocks(d_wo), blocks(d_wao), blocks(d_wco)]))
        for name, land in zip(["w_up", "w_o", "w_attn_out", "w_conv_out"], got):
            adam(name, land, l)
        d_win = _wgrad_qkv(dq, dk, dv, sv["h"], d_win, f"wgrad_qkv{l}")
        side = _Exchange([blocks(d_win)]) if l == 0 else None
        (dcur, acc_in[l]), got_in = _inproj_bwd(dgt, dq, dk, dv, sv["x"], dx1, g_mix3, wint[l], l, tm_in, side=side)
        d_win_above = d_win

    dsinks = jnp.stack([dsink[l][:, :, :, 0].sum(axis=0).reshape(16) for l in range(L)])
    small = _pack_small(
        jnp.concatenate([acc_in[l][0:1] for l in range(L)]),
        jnp.stack([acc_mix[l][0:2].reshape(2 * D) for l in range(L)]),
        dsinks,
        jnp.concatenate([acc_mix[l][2:3] for l in range(L)]),
        jnp.concatenate([acc_mlp[l][0:1] for l in range(L)]),
        acc_loss[0],
        jnp.concatenate([acc_mix[l][3:6] for l in range(L)]),
        acc_loss[1:2])
    adam("w_in", got_in[0], 0)
    (small_land,) = _remote_only(_Exchange([], [small]), "exchange_small")

    row = lambda t: t.reshape(1, D)
    small_out, loss = _adam_small(small_land, me.reshape(1).astype(jnp.int32), {
        "g_mix": (g_mix, m_g_mix, v_g_mix), "b_gates": (b_gates, m_b_gates, v_b_gates), "sinks": (sinks, m_sinks, v_sinks),
        "conv_b": (conv_b, m_conv_b, v_conv_b), "g_mlp": (g_mlp, m_g_mlp, v_g_mlp),
        "g_final": (row(g_final), row(m_g_final), row(v_g_final)), "conv_w": (conv_w, m_conv_w, v_conv_w)})
    small_out["g_final"] = [t.reshape(D) for t in small_out["g_final"]]
    big["w_in"] = [tr(o) for o in big["w_in"]]
    order = ["g_mix", "w_in", "b_gates", "sinks", "w_attn_out", "conv_w", "conv_b", "w_conv_out", "w_o", "g_mlp",
             "w_up", "w_down", "g_final"]
    out = [loss.reshape(()), dcur.reshape(nseq, S, D)]
    for kind in range(4):
        for name in order:
            out.append(big[name][kind] if name in big else small_out[name][kind])
    return tuple(out)
```
